```python
import math
import jax
import jax.numpy as jnp
from jax import lax
import numpy as np


D_MODEL = 2048
BATCH = 8
SEQ = 2048
DEPTH = 2

HEAD_DIM = 64
N_HEADS_A = 12
DILATED_CONFIGS = ((128, 1), (512, 4), (2048, 16))
DILATED_BLOCK = 64
N_HEADS_B = 10
N_KV_B = 2
WINDOW_B = 128
BLOCK_B = 128
N_HEADS_C = 10
GRID_W = 64
NA_ROWS = 8
NA_COLS = 16
WIDTH_A = N_HEADS_A * HEAD_DIM
WIDTH_B = N_HEADS_B * HEAD_DIM
WIDTH_B_KV = N_KV_B * HEAD_DIM
WIDTH_C = N_HEADS_C * HEAD_DIM
MIX_WIDTH = WIDTH_A + WIDTH_B + WIDTH_C
IN_COLS = 3 * WIDTH_A + WIDTH_B + 2 * WIDTH_B_KV + 3 * WIDTH_C
D_FF = 5632
CONV_WIDTH = 3
ROPE_THETA = 10000.0
EPS = 1e-6
NEG_INF = -1e30

kernel_name = 'hybrid_parallel_local_attention_encoder'


def rms_normalize(x):
    x32 = x.astype(jnp.float32)
    return (x32 * lax.rsqrt(jnp.mean(x32 * x32, axis=-1, keepdims=True) + EPS)).astype(x.dtype)


def rmsnorm(x, gain):
    return rms_normalize(x) * gain


def heads(x, n):
    b, t, _ = x.shape
    return x.reshape(b, t, n, HEAD_DIM).transpose(0, 2, 1, 3)


def merge_heads(x):
    b, h, t, d = x.shape
    return x.transpose(0, 2, 1, 3).reshape(b, t, h * d)


def rotary(x):
    t, dh = x.shape[-2], x.shape[-1]
    inv_freq = ROPE_THETA ** (-jnp.arange(0, dh, 2, dtype=jnp.float32) / dh)
    ang = jnp.arange(t, dtype=jnp.float32)[:, None] * inv_freq[None, :]
    cos = jnp.cos(ang).astype(x.dtype)
    sin = jnp.sin(ang).astype(x.dtype)
    x1, x2 = x[..., : dh // 2], x[..., dh // 2:]
    return jnp.concatenate([x1 * cos - x2 * sin, x2 * cos + x1 * sin], axis=-1)


def banded_attention(q, k, v, half, block):
    n, g, length, dh = q.shape
    nb = length // block
    span = block + 2 * half
    kp = jnp.pad(k, ((0, 0), (half, half), (0, 0)))
    vp = jnp.pad(v, ((0, 0), (half, half), (0, 0)))
    idx = jnp.arange(nb)[:, None] * block + jnp.arange(span)[None, :]
    kb = kp[:, idx]
    vb = vp[:, idx]
    qb = q.reshape(n, g, nb, block, dh)
    s = jnp.einsum('ngbqd,nbkd->ngbqk', qb, kb).astype(jnp.float32) * (dh ** -0.5)
    qpos = jnp.arange(nb)[:, None] * block + jnp.arange(block)[None, :]
    kpos = idx - half
    valid = ((jnp.abs(qpos[:, :, None] - kpos[:, None, :]) <= half)
             & (kpos >= 0)[:, None, :] & (kpos < length)[:, None, :])
    s = jnp.where(valid, s, NEG_INF)
    lse = jax.nn.logsumexp(s, axis=-1)
    p = jnp.exp(s - lse[..., None]).astype(v.dtype)
    o = jnp.einsum('ngbqk,nbkd->ngbqd', p, vb)
    return o.reshape(n, g, length, dh), lse.reshape(n, g, length)


def to_strided(x, r):
    b, h, t, d = x.shape
    return x.reshape(b, h, t // r, r, d).transpose(0, 1, 3, 2, 4).reshape(b * h * r, t // r, d)


def from_strided(x, b, h, r):
    length = x.shape[1]
    rest = x.shape[2:]
    y = x.reshape((b, h, r, length) + rest)
    y = jnp.moveaxis(y, 2, 3)
    return y.reshape((b, h, length * r) + rest)


def dilated_attention(q, k, v):
    b, h, t, dh = q.shape
    outs, lses = [], []
    for window, r in DILATED_CONFIGS:
        length = t // r
        half = window // (2 * r)
        block = math.gcd(length, DILATED_BLOCK)
        o, lse = banded_attention(to_strided(q, r)[:, None], to_strided(k, r), to_strided(v, r), half, block)
        outs.append(from_strided(o[:, 0], b, h, r))
        lses.append(from_strided(lse[:, 0], b, h, r))
    weights = jax.nn.softmax(jnp.stack(lses), axis=0)
    return jnp.einsum('cbht,cbhtd->bhtd', weights.astype(q.dtype), jnp.stack(outs))


def sink_window_attention(q, k, v, sink):
    b, hq, t, dh = q.shape
    g = hq // N_KV_B
    o, lse = banded_attention(q.reshape(b * N_KV_B, g, t, dh), k.reshape(b * N_KV_B, t, dh),
                              v.reshape(b * N_KV_B, t, dh), WINDOW_B, BLOCK_B)
    sink_g = jnp.tile(sink.astype(jnp.float32).reshape(N_KV_B, g), (b, 1))[:, :, None]
    keep = jnp.exp(lse - jnp.logaddexp(lse, sink_g))
    return (o * keep[..., None].astype(o.dtype)).reshape(b, hq, t, dh)


def neighborhood_attention(q, k, v, rpb):
    b, h, t, dh = q.shape
    rows = t // GRID_W
    kr = min(NA_ROWS, rows)
    kc = NA_COLS
    r = jnp.arange(rows)
    c = jnp.arange(GRID_W)
    row_start = jnp.clip(r - kr // 2, 0, rows - kr)
    ridx = row_start[:, None] + jnp.arange(kr)[None, :]
    col_start = jnp.clip(c - kc // 2, 0, GRID_W - kc)
    col_mask = (c[None, :] >= col_start[:, None]) & (c[None, :] < col_start[:, None] + kc)
    qg = q.reshape(b, h, rows, GRID_W, dh)
    kg = k.reshape(b, h, rows, GRID_W, dh)[:, :, ridx]
    vg = v.reshape(b, h, rows, GRID_W, dh)[:, :, ridx]
    s = jnp.einsum('bhrcd,bhrkwd->bhrckw', qg, kg).astype(jnp.float32) * (dh ** -0.5)
    roff = (ridx - r[:, None]) + (NA_ROWS - 1)
    coff = jnp.clip(c[None, :] - c[:, None] + (kc - 1), 0, 2 * kc - 2)
    bias = rpb[:, roff[:, None, :, None], coff[None, :, None, :]]
    s = jnp.where(col_mask[:, None, :], s + bias.astype(jnp.float32)[None], NEG_INF)
    p = jax.nn.softmax(s.reshape(b, h, rows, GRID_W, kr * GRID_W), axis=-1)
    p = p.reshape(s.shape).astype(v.dtype)
    o = jnp.einsum('bhrckw,bhrkwd->bhrcd', p, vg)
    return o.reshape(b, h, t, dh)


def split_projection(proj):
    sizes = (WIDTH_A,) * 3 + (WIDTH_B, WIDTH_B_KV, WIDTH_B_KV) + (WIDTH_C,) * 3
    offsets = np.cumsum(sizes)[:-1].tolist()
    return jnp.split(proj, offsets, axis=-1)


def depthwise_conv(u, w, bias):
    t = u.shape[1]
    pad = CONV_WIDTH // 2
    up = jnp.pad(u, ((0, 0), (pad, pad), (0, 0)))
    return sum(up[:, j:j + t] * w[j] for j in range(CONV_WIDTH)) + bias


def _fwd_setup_inputs(seed: int = 0) -> dict:
    key = jax.random.key(seed)
    ks = jax.random.split(key, 13)
    nrm = jax.random.normal
    f32 = jnp.float32
    return {
        'x': nrm(ks[0], (BATCH, SEQ, D_MODEL), f32),
        'ln_attn': 1.0 + 0.02 * nrm(ks[1], (DEPTH, D_MODEL), f32),
        'w_in': nrm(ks[2], (DEPTH, D_MODEL, IN_COLS), f32) * D_MODEL ** -0.5,
        'sink_b': 0.5 * nrm(ks[3], (DEPTH, N_HEADS_B), f32),
        'rpb_c': 0.1 * nrm(ks[4], (DEPTH, N_HEADS_C, 2 * NA_ROWS - 1, 2 * NA_COLS - 1), f32),
        'mix_gain': 1.0 + 0.02 * nrm(ks[5], (DEPTH, MIX_WIDTH), f32),
        'w_out': nrm(ks[6], (DEPTH, MIX_WIDTH, D_MODEL), f32) * MIX_WIDTH ** -0.5,
        'ln_ffn': 1.0 + 0.02 * nrm(ks[7], (DEPTH, D_MODEL), f32),
        'w_up': nrm(ks[8], (DEPTH, D_MODEL, 2 * D_FF), f32) * D_MODEL ** -0.5,
        'conv_w': nrm(ks[9], (DEPTH, CONV_WIDTH, 2 * D_FF), f32) * CONV_WIDTH ** -0.5,
        'conv_b': 0.01 * nrm(ks[10], (DEPTH, 2 * D_FF), f32),
        'w_down': nrm(ks[11], (DEPTH, D_FF, D_MODEL), f32) * D_FF ** -0.5,
        'ln_final': 1.0 + 0.02 * nrm(ks[12], (D_MODEL,), f32),
    }


def _fwd_reference(x, ln_attn, w_in, sink_b, rpb_c, mix_gain, w_out, ln_ffn, w_up, conv_w, conv_b, w_down, ln_final):
    for l in range(DEPTH):
        h = rmsnorm(x, ln_attn[l])
        qa, ka, va, qb, kb, vb, qc, kc, vc = split_projection(h @ w_in[l])
        oa = dilated_attention(rotary(heads(qa, N_HEADS_A)), rotary(heads(ka, N_HEADS_A)), heads(va, N_HEADS_A))
        ob = sink_window_attention(rotary(heads(qb, N_HEADS_B)), rotary(heads(kb, N_KV_B)), heads(vb, N_KV_B), sink_b[l])
        oc = neighborhood_attention(heads(qc, N_HEADS_C), heads(kc, N_HEADS_C), heads(vc, N_HEADS_C), rpb_c[l])
        mixed = jnp.concatenate([rms_normalize(merge_heads(oa)), rms_normalize(merge_heads(ob)),
                                 rms_normalize(merge_heads(oc))], axis=-1) * mix_gain[l]
        x = x + mixed @ w_out[l]
        h = rmsnorm(x, ln_ffn[l])
        u = depthwise_conv(h @ w_up[l], conv_w[l], conv_b[l])
        gate, val = jnp.split(u, 2, axis=-1)
        x = x + (jax.nn.silu(gate) * val) @ w_down[l]
    return rmsnorm(x, ln_final)


import jax as _jax
import jax.numpy as _jnp

TWIN_FORMAT = 'train_step'
FWD_PARAMS = ['x', 'ln_attn', 'w_in', 'sink_b', 'rpb_c', 'mix_gain', 'w_out', 'ln_ffn', 'w_up', 'conv_w', 'conv_b', 'w_down', 'ln_final']
TWIN_WEIGHTS = ['ln_attn', 'w_in', 'sink_b', 'rpb_c', 'mix_gain', 'w_out', 'ln_ffn', 'w_up', 'conv_w', 'conv_b', 'w_down', 'ln_final']
TWIN_DIFF_INPUT = 'x'
TWIN_INPUTS = ['x', 'ln_attn', 'w_in', 'sink_b', 'rpb_c', 'mix_gain', 'w_out', 'ln_ffn', 'w_up', 'conv_w', 'conv_b', 'w_down', 'ln_final', 'loss_target', 'm_ln_attn', 'm_w_in', 'm_sink_b', 'm_rpb_c', 'm_mix_gain', 'm_w_out', 'm_ln_ffn', 'm_w_up', 'm_conv_w', 'm_conv_b', 'm_w_down', 'm_ln_final', 'v_ln_attn', 'v_w_in', 'v_sink_b', 'v_rpb_c', 'v_mix_gain', 'v_w_out', 'v_ln_ffn', 'v_w_up', 'v_conv_w', 'v_conv_b', 'v_w_down', 'v_ln_final']
TWIN_OUTPUTS = ['loss', 'grad_x', 'grad_ln_attn', 'grad_w_in', 'grad_sink_b', 'grad_rpb_c', 'grad_mix_gain', 'grad_w_out', 'grad_ln_ffn', 'grad_w_up', 'grad_conv_w', 'grad_conv_b', 'grad_w_down', 'grad_ln_final', 'delta_ln_attn', 'delta_w_in', 'delta_sink_b', 'delta_rpb_c', 'delta_mix_gain', 'delta_w_out', 'delta_ln_ffn', 'delta_w_up', 'delta_conv_w', 'delta_conv_b', 'delta_w_down', 'delta_ln_final', 'new_m_ln_attn', 'new_m_w_in', 'new_m_sink_b', 'new_m_rpb_c', 'new_m_mix_gain', 'new_m_w_out', 'new_m_ln_ffn', 'new_m_w_up', 'new_m_conv_w', 'new_m_conv_b', 'new_m_w_down', 'new_m_ln_final', 'new_v_ln_attn', 'new_v_w_in', 'new_v_sink_b', 'new_v_rpb_c', 'new_v_mix_gain', 'new_v_w_out', 'new_v_ln_ffn', 'new_v_w_up', 'new_v_conv_w', 'new_v_conv_b', 'new_v_w_down', 'new_v_ln_final']
TWIN_LEAF_KINDS = {'loss': 'loss', 'grad_x': 'grad_x', 'grad_ln_attn': 'grad_w', 'grad_w_in': 'grad_w', 'grad_sink_b': 'grad_w', 'grad_rpb_c': 'grad_w', 'grad_mix_gain': 'grad_w', 'grad_w_out': 'grad_w', 'grad_ln_ffn': 'grad_w', 'grad_w_up': 'grad_w', 'grad_conv_w': 'grad_w', 'grad_conv_b': 'grad_w', 'grad_w_down': 'grad_w', 'grad_ln_final': 'grad_w', 'delta_ln_attn': 'delta_w', 'delta_w_in': 'delta_w', 'delta_sink_b': 'delta_w', 'delta_rpb_c': 'delta_w', 'delta_mix_gain': 'delta_w', 'delta_w_out': 'delta_w', 'delta_ln_ffn': 'delta_w', 'delta_w_up': 'delta_w', 'delta_conv_w': 'delta_w', 'delta_conv_b': 'delta_w', 'delta_w_down': 'delta_w', 'delta_ln_final': 'delta_w', 'new_m_ln_attn': 'new_m', 'new_m_w_in': 'new_m', 'new_m_sink_b': 'new_m', 'new_m_rpb_c': 'new_m', 'new_m_mix_gain': 'new_m', 'new_m_w_out': 'new_m', 'new_m_ln_ffn': 'new_m', 'new_m_w_up': 'new_m', 'new_m_conv_w': 'new_m', 'new_m_conv_b': 'new_m', 'new_m_w_down': 'new_m', 'new_m_ln_final': 'new_m', 'new_v_ln_attn': 'new_v', 'new_v_w_in': 'new_v', 'new_v_sink_b': 'new_v', 'new_v_rpb_c': 'new_v', 'new_v_mix_gain': 'new_v', 'new_v_w_out': 'new_v', 'new_v_ln_ffn': 'new_v', 'new_v_w_up': 'new_v', 'new_v_conv_w': 'new_v', 'new_v_conv_b': 'new_v', 'new_v_w_down': 'new_v', 'new_v_ln_final': 'new_v'}


def _forward(args):
    return _fwd_reference(*[args[k] for k in FWD_PARAMS])


def _output_shape():
    out = _jax.eval_shape(lambda: _forward(_fwd_setup_inputs(0)))
    return out.shape, out.dtype

N_MICROBATCH = 1
ADAM_LR = 0.001
ADAM_B1 = 0.9
ADAM_B2 = 0.999
ADAM_EPS = 1e-08
ADAM_WD = 0.01
ADAM_STEP = 10
PER_EXAMPLE_BATCH_AXIS = {'x': 0, 'loss_target': 0}
SHARED_INPUTS = []
_WEIGHT_DTYPES = {'ln_attn': _jnp.float32, 'w_in': _jnp.float32, 'sink_b': _jnp.float32, 'rpb_c': _jnp.float32, 'mix_gain': _jnp.float32, 'w_out': _jnp.float32, 'ln_ffn': _jnp.float32, 'w_up': _jnp.float32, 'conv_w': _jnp.float32, 'conv_b': _jnp.float32, 'w_down': _jnp.float32, 'ln_final': _jnp.float32}
MOMENT_SCALE = {'ln_attn': 7.137175e-02, 'w_in': 4.449005e-02, 'sink_b': 2.611535e-03, 'rpb_c': 1.295753e-02, 'mix_gain': 4.971651e-02, 'w_out': 4.973040e-02, 'ln_ffn': 3.252174e-02, 'w_up': 1.385919e-02, 'conv_w': 1.399622e-02, 'conv_b': 1.655154e-02, 'w_down': 2.276245e-02, 'ln_final': 7.950437e+00}


def _to_microbatches(a, axis):
    t = _jnp.moveaxis(a, axis, 0)
    t = t.reshape((N_MICROBATCH, t.shape[0] // N_MICROBATCH) + t.shape[1:])
    return _jnp.moveaxis(t, 1, axis + 1)


def setup_inputs(seed: int = 0) -> dict:
    inp = _fwd_setup_inputs(seed)
    key = _jax.random.fold_in(_jax.random.key(seed), 7919)
    shape, _ = _output_shape()
    out = dict(inp)
    out["loss_target"] = _jax.random.normal(_jax.random.fold_in(key, 0), shape, _jnp.float32)
    for i, name in enumerate(TWIN_WEIGHTS):
        w = inp[name].astype(_jnp.float32)
        if MOMENT_SCALE is None:
            s = _jnp.sqrt(_jnp.mean(_jnp.square(w)) + 1e-30)
        else:
            s = MOMENT_SCALE[name]
        km, kv = _jax.random.split(_jax.random.fold_in(key, i + 1))
        out[name] = w
        out["m_" + name] = s * _jax.random.normal(km, w.shape, _jnp.float32)
        out["v_" + name] = (s * s) * _jax.random.uniform(kv, w.shape, _jnp.float32, 0.5, 1.5)
    if N_MICROBATCH > 1:
        for name, axis in PER_EXAMPLE_BATCH_AXIS.items():
            out[name] = _to_microbatches(out[name], axis)
    return {'x': out['x'], 'ln_attn': out['ln_attn'], 'w_in': out['w_in'], 'sink_b': out['sink_b'], 'rpb_c': out['rpb_c'], 'mix_gain': out['mix_gain'], 'w_out': out['w_out'], 'ln_ffn': out['ln_ffn'], 'w_up': out['w_up'], 'conv_w': out['conv_w'], 'conv_b': out['conv_b'], 'w_down': out['w_down'], 'ln_final': out['ln_final'], 'loss_target': out['loss_target'], 'm_ln_attn': out['m_ln_attn'], 'm_w_in': out['m_w_in'], 'm_sink_b': out['m_sink_b'], 'm_rpb_c': out['m_rpb_c'], 'm_mix_gain': out['m_mix_gain'], 'm_w_out': out['m_w_out'], 'm_ln_ffn': out['m_ln_ffn'], 'm_w_up': out['m_w_up'], 'm_conv_w': out['m_conv_w'], 'm_conv_b': out['m_conv_b'], 'm_w_down': out['m_w_down'], 'm_ln_final': out['m_ln_final'], 'v_ln_attn': out['v_ln_attn'], 'v_w_in': out['v_w_in'], 'v_sink_b': out['v_sink_b'], 'v_rpb_c': out['v_rpb_c'], 'v_mix_gain': out['v_mix_gain'], 'v_w_out': out['v_w_out'], 'v_ln_ffn': out['v_ln_ffn'], 'v_w_up': out['v_w_up'], 'v_conv_w': out['v_conv_w'], 'v_conv_b': out['v_conv_b'], 'v_w_down': out['v_w_down'], 'v_ln_final': out['v_ln_final']}


def _loss(weights, diff, rest, loss_target):
    with _jax.named_scope("forward"):
        args = {**rest, TWIN_DIFF_INPUT: diff, **{k: w.astype(_WEIGHT_DTYPES[k]) for k, w in weights.items()}}
        y = _forward(args)
    with _jax.named_scope("loss_head"):
        err = _jnp.square(y.astype(_jnp.float32) - loss_target)
        return 0.5 * _jnp.sum(_jnp.mean(err, axis=-1)) if err.ndim else 0.5 * err


def _adamw(w, g, m, v):
    m = ADAM_B1 * m + (1.0 - ADAM_B1) * g
    v = ADAM_B2 * v + (1.0 - ADAM_B2) * _jnp.square(g)
    m_hat = m / (1.0 - ADAM_B1 ** ADAM_STEP)
    v_hat = v / (1.0 - ADAM_B2 ** ADAM_STEP)
    delta = -ADAM_LR * (m_hat / (_jnp.sqrt(v_hat) + ADAM_EPS) + ADAM_WD * w)
    return delta, m, v


def reference(x, ln_attn, w_in, sink_b, rpb_c, mix_gain, w_out, ln_ffn, w_up, conv_w, conv_b, w_down, ln_final, loss_target, m_ln_attn, m_w_in, m_sink_b, m_rpb_c, m_mix_gain, m_w_out, m_ln_ffn, m_w_up, m_conv_w, m_conv_b, m_w_down, m_ln_final, v_ln_attn, v_w_in, v_sink_b, v_rpb_c, v_mix_gain, v_w_out, v_ln_ffn, v_w_up, v_conv_w, v_conv_b, v_w_down, v_ln_final):
    given = dict(x=x, ln_attn=ln_attn, w_in=w_in, sink_b=sink_b, rpb_c=rpb_c, mix_gain=mix_gain, w_out=w_out, ln_ffn=ln_ffn, w_up=w_up, conv_w=conv_w, conv_b=conv_b, w_down=w_down, ln_final=ln_final, loss_target=loss_target, m_ln_attn=m_ln_attn, m_w_in=m_w_in, m_sink_b=m_sink_b, m_rpb_c=m_rpb_c, m_mix_gain=m_mix_gain, m_w_out=m_w_out, m_ln_ffn=m_ln_ffn, m_w_up=m_w_up, m_conv_w=m_conv_w, m_conv_b=m_conv_b, m_w_down=m_w_down, m_ln_final=m_ln_final, v_ln_attn=v_ln_attn, v_w_in=v_w_in, v_sink_b=v_sink_b, v_rpb_c=v_rpb_c, v_mix_gain=v_mix_gain, v_w_out=v_w_out, v_ln_ffn=v_ln_ffn, v_w_up=v_w_up, v_conv_w=v_conv_w, v_conv_b=v_conv_b, v_w_down=v_w_down, v_ln_final=v_ln_final)
    weights = {n: given[n] for n in TWIN_WEIGHTS}
    shared = {n: given[n] for n in SHARED_INPUTS}
    per_example = {n: given[n] for n in ['x']}
    grad_fn = _jax.value_and_grad(_loss, argnums=(0, 1))

    def one_microbatch(ex, loss_target):
        ex = dict(ex)
        diff = ex.pop(TWIN_DIFF_INPUT)
        return grad_fn(weights, diff, {**shared, **ex}, loss_target)

    if N_MICROBATCH == 1:
        loss, (grad_w, grad_x) = one_microbatch(per_example, given["loss_target"])
    else:
        def body(carry, xs):
            loss_sum, grad_sum = carry
            l_k, (gw_k, gx_k) = one_microbatch(xs[0], xs[1])
            with _jax.named_scope("update"):
                return (loss_sum + l_k, _jax.tree.map(_jnp.add, grad_sum, gw_k)), gx_k

        init = (_jnp.zeros((), _jnp.float32), _jax.tree.map(_jnp.zeros_like, weights))
        (loss, grad_w), grad_x = _jax.lax.scan(body, init, (per_example, given["loss_target"]))
    with _jax.named_scope("update"):
        delta_w, new_m, new_v = {}, {}, {}
        for n in TWIN_WEIGHTS:
            delta_w[n], new_m[n], new_v[n] = _adamw(weights[n], grad_w[n], given["m_" + n], given["v_" + n])
    return (loss, grad_x, *[grad_w[n] for n in TWIN_WEIGHTS], *[delta_w[n] for n in TWIN_WEIGHTS],
            *[new_m[n] for n in TWIN_WEIGHTS], *[new_v[n] for n in TWIN_WEIGHTS])
```

```python
import functools
import math

import jax
import jax.numpy as jnp
from jax import lax
from jax.experimental import pallas as pl
from jax.experimental.pallas import tpu as pltpu

F32 = jnp.float32
BF16 = jnp.bfloat16
MESH = pl.DeviceIdType.MESH

D_MODEL = 2048
SEQ = 2048
DEPTH = 2
HEAD_DIM = 64
N_HEADS_A = 12
N_HEADS_B = 10
N_KV_B = 2
N_HEADS_C = 10
WINDOW_B = 128
GRID_W = 64
NA_ROWS = 8
NA_COLS = 16
WIDTH_A = N_HEADS_A * HEAD_DIM
WIDTH_B = N_HEADS_B * HEAD_DIM
WIDTH_C = N_HEADS_C * HEAD_DIM
IN_COLS = 5120
D_FF = 5632
ROPE_THETA = 10000.0
EPS = 1e-6
NEG_INF = -1e30
N_SHARDS = 4

ADAM_LR = 0.001
ADAM_B1 = 0.9
ADAM_B2 = 0.999
ADAM_EPS = 1e-08
ADAM_WD = 0.01
ADAM_STEP = 10

LANES = 128
QB = 256
NQB = SEQ // QB
ROWS = 256
MIB = 2 ** 20

A_BLK = (0, 6, 12)
B_BLK = (18, 23, 24)
C_BLK = (25, 30, 35)
ROPE_BLKS = tuple(range(0, 12)) + tuple(range(18, 24))
QSCALE_BLKS = tuple(range(0, 6)) + tuple(range(18, 23)) + tuple(range(25, 30))
N_PBLK = IN_COLS // LANES


def _params(sem, vmem_mib):
    return pltpu.CompilerParams(dimension_semantics=sem, vmem_limit_bytes=vmem_mib * MIB)


def _weight_spec(w, layer, cols, t_in, t_out, transposed):
    _, s, r, c = w.shape
    if cols:
        per = c // t_out
        k_dim, n = r, s * c
        if transposed:
            index = lambda j, rr: (layer, rr // per, j, rr % per)
        else:
            index = lambda j, kk: (layer, j // per, kk, j % per)
    else:
        per = r // t_in
        k_dim, n = s * r, c
        if transposed:
            index = lambda j, rr: (layer, j // per, j % per, rr)
        else:
            index = lambda j, kk: (layer, kk // per, kk % per, j)
    return pl.BlockSpec((None, None, t_in, t_out), index), k_dim, n


def _mm_nn(a, w, *, layer, cols, tn, tk, out_dtype, name, residual=None, out_split=1):
    m, k_dim = a.shape
    w_spec, k_w, n = _weight_spec(w, layer, cols, tk, tn, False)
    assert k_w == k_dim
    nj, nk = n // tn, k_dim // tk
    in_specs = [pl.BlockSpec((m, tk), lambda j, k: (0, k)), w_spec]
    args = [a, w]
    if residual is not None:
        in_specs.append(pl.BlockSpec((m, tn), lambda j, k: (0, j)))
        args.append(residual)
    if out_split > 1:
        per_o = n // out_split // tn
        out_spec = pl.BlockSpec((None, m, tn), lambda j, k: (j // per_o, 0, j % per_o))
        out_shape = jax.ShapeDtypeStruct((out_split, m, n // out_split), out_dtype)
    else:
        out_spec = pl.BlockSpec((m, tn), lambda j, k: (0, j))
        out_shape = jax.ShapeDtypeStruct((m, n), out_dtype)

    def body(*refs):
        a_ref, w_ref = refs[0], refs[1]
        r_ref = refs[2] if residual is not None else None
        o_ref = refs[3] if residual is not None else refs[2]

        def finish(val):
            if r_ref is not None:
                val = r_ref[...] + val
            o_ref[...] = val.astype(o_ref.dtype)

        part = jnp.dot(a_ref[...], w_ref[...], preferred_element_type=F32)
        if nk == 1:
            finish(part)
        else:
            acc = refs[-1]
            kk = pl.program_id(1)

            @pl.when(kk == 0)
            def _():
                acc[...] = part

            @pl.when(kk > 0)
            def _():
                acc[...] += part

            @pl.when(kk == nk - 1)
            def _():
                finish(acc[...])

    return pl.pallas_call(
        body, name=name, grid=(nj, nk), in_specs=in_specs, out_specs=out_spec, out_shape=out_shape,
        scratch_shapes=[pltpu.VMEM((m, tn), F32)] if nk > 1 else [],
        compiler_params=_params(("arbitrary", "arbitrary"), 56),
    )(*args)


def _mm_nt(dy, w, *, layer, cols, to, tr, out_dtype, name):
    if dy.ndim == 3:
        m = dy.shape[1]
        n = dy.shape[0] * dy.shape[2]
        per_d = dy.shape[2] // tr
        dy_spec = pl.BlockSpec((None, m, tr), lambda j, r: (r // per_d, 0, r % per_d))
    else:
        m, n = dy.shape
        dy_spec = pl.BlockSpec((m, tr), lambda j, r: (0, r))
    w_spec, k_dim, n_w = _weight_spec(w, layer, cols, to, tr, True)
    assert n_w == n
    nj, nr = k_dim // to, n // tr

    def body(dy_ref, w_ref, o_ref, *scratch):
        part = lax.dot_general(dy_ref[...], w_ref[...], (((1,), (1,)), ((), ())), preferred_element_type=F32)
        if nr == 1:
            o_ref[...] = part.astype(o_ref.dtype)
        else:
            acc = scratch[0]
            rr = pl.program_id(1)

            @pl.when(rr == 0)
            def _():
                acc[...] = part

            @pl.when(rr > 0)
            def _():
                acc[...] += part

            @pl.when(rr == nr - 1)
            def _():
                o_ref[...] = acc[...].astype(o_ref.dtype)

    return pl.pallas_call(
        body, name=name, grid=(nj, nr), in_specs=[dy_spec, w_spec],
        out_specs=pl.BlockSpec((m, to), lambda j, r: (0, j)),
        out_shape=jax.ShapeDtypeStruct((m, k_dim), out_dtype),
        scratch_shapes=[pltpu.VMEM((m, to), F32)] if nr > 1 else [],
        compiler_params=_params(("arbitrary", "arbitrary"), 56),
    )(dy, w)


def _mm_tn(x, dy, *, tk, tn, layer, shards, name, prev=None):
    m, k_dim = x.shape
    if dy.ndim == 3:
        n = dy.shape[0] * dy.shape[2]
        per_d = dy.shape[2] // tn
        dy_spec = pl.BlockSpec((None, m, tn), lambda i, j: (j // per_d, 0, j % per_d))
    else:
        n = dy.shape[1]
        dy_spec = pl.BlockSpec((m, tn), lambda i, j: (0, j))
    if shards > 0:
        per = n // shards // tn
        out_shape = jax.ShapeDtypeStruct((DEPTH, shards, k_dim, n // shards), F32)
        out_spec = pl.BlockSpec((None, None, tk, tn), lambda i, j: (layer, j // per, i, j % per))
    else:
        s = -shards
        per = k_dim // s // tk
        out_shape = jax.ShapeDtypeStruct((DEPTH, s, k_dim // s, n), F32)
        out_spec = pl.BlockSpec((None, None, tk, tn), lambda i, j: (layer, i // per, i % per, j))
    in_specs = [pl.BlockSpec((m, tk), lambda i, j: (0, i)), dy_spec]
    args = [x, dy]
    aliases = {}
    if prev is not None:
        in_specs.append(pl.BlockSpec(memory_space=pl.ANY))
        args.append(prev)
        aliases = {2: 0}

    def body(x_ref, dy_ref, *rest):
        o_ref = rest[-1]
        o_ref[...] = lax.dot_general(x_ref[...], dy_ref[...], (((0,), (0,)), ((), ())), preferred_element_type=F32)

    return pl.pallas_call(
        body, name=name, grid=(k_dim // tk, n // tn), in_specs=in_specs, out_specs=out_spec, out_shape=out_shape,
        input_output_aliases=aliases,
        compiler_params=_params(("arbitrary", "arbitrary"), 56),
    )(*args)


def _row_spec(width, rows=ROWS):
    return pl.BlockSpec((rows, width), lambda i: (i, 0))


def _vec_spec(width):
    return pl.BlockSpec((1, width), lambda i: (0, 0))


def _rms_stats(x):
    r = lax.rsqrt(jnp.mean(x * x, axis=-1, keepdims=True) + EPS)
    return r, x * r


def _rmsnorm_fwd(x, gain, name):
    t, d = x.shape

    def body(x_ref, g_ref, o_ref):
        _, n = _rms_stats(x_ref[...])
        o_ref[...] = (n * g_ref[...]).astype(BF16)

    return pl.pallas_call(
        body, name=name, grid=(t // ROWS,), in_specs=[_row_spec(d), _vec_spec(d)], out_specs=_row_spec(d),
        out_shape=jax.ShapeDtypeStruct((t, d), BF16), compiler_params=_params(("arbitrary",), 32),
    )(x, gain)


def _rmsnorm_bwd(x, gain, dh, dres, name):
    t, d = x.shape

    def body(x_ref, g_ref, dh_ref, dres_ref, dx_ref, dxb_ref, dg_ref):
        r, n = _rms_stats(x_ref[...])
        dh_v = dh_ref[...]
        dn = dh_v * g_ref[...]
        dx = dres_ref[...] + r * (dn - n * jnp.mean(dn * n, axis=-1, keepdims=True))
        dx_ref[...] = dx
        dxb_ref[...] = dx.astype(BF16)
        part = jnp.sum(dh_v * n, axis=0, keepdims=True)

        @pl.when(pl.program_id(0) == 0)
        def _():
            dg_ref[...] = part

        @pl.when(pl.program_id(0) > 0)
        def _():
            dg_ref[...] += part

    return pl.pallas_call(
        body, name=name, grid=(t // ROWS,),
        in_specs=[_row_spec(d), _vec_spec(d), _row_spec(d), _row_spec(d)],
        out_specs=[_row_spec(d), _row_spec(d), _vec_spec(d)],
        out_shape=[jax.ShapeDtypeStruct((t, d), F32), jax.ShapeDtypeStruct((t, d), BF16),
                   jax.ShapeDtypeStruct((1, d), F32)],
        compiler_params=_params(("arbitrary",), 40),
    )(x, gain, dh, dres)


def _loss_head(x, gain, target, name):
    t, d = x.shape

    def body(x_ref, g_ref, t_ref, loss_ref, dx_ref, dxb_ref, dg_ref):
        r, n = _rms_stats(x_ref[...])
        g = g_ref[...]
        err = n * g - t_ref[...]
        dy = err * (1.0 / d)
        dn = dy * g
        dx = r * (dn - n * jnp.mean(dn * n, axis=-1, keepdims=True))
        dx_ref[...] = dx
        dxb_ref[...] = dx.astype(BF16)
        part = jnp.sum(dy * n, axis=0, keepdims=True)
        lpart = jnp.zeros((8, LANES), F32) + 0.5 * jnp.sum(jnp.mean(err * err, axis=-1, keepdims=True))

        @pl.when(pl.program_id(0) == 0)
        def _():
            dg_ref[...] = part
            loss_ref[...] = lpart

        @pl.when(pl.program_id(0) > 0)
        def _():
            dg_ref[...] += part
            loss_ref[...] += lpart

    return pl.pallas_call(
        body, name=name, grid=(t // ROWS,),
        in_specs=[_row_spec(d), _vec_spec(d), _row_spec(d)],
        out_specs=[pl.BlockSpec((8, LANES), lambda i: (0, 0)), _row_spec(d), _row_spec(d), _vec_spec(d)],
        out_shape=[jax.ShapeDtypeStruct((8, LANES), F32), jax.ShapeDtypeStruct((t, d), F32),
                   jax.ShapeDtypeStruct((t, d), BF16), jax.ShapeDtypeStruct((1, d), F32)],
        compiler_params=_params(("arbitrary",), 40),
    )(x, gain, target)


def _swap_halves(x):
    lane = lax.broadcasted_iota(jnp.int32, x.shape, 1)
    return jnp.where((lane % HEAD_DIM) < HEAD_DIM // 2, pltpu.roll(x, LANES - HEAD_DIM // 2, 1),
                     pltpu.roll(x, HEAD_DIM // 2, 1))


def _rope_tables(t):
    inv_freq = ROPE_THETA ** (-jnp.arange(0, HEAD_DIM, 2, dtype=F32) / HEAD_DIM)
    ang = jnp.arange(t, dtype=F32)[:, None] * inv_freq[None, :]
    cos = jnp.tile(jnp.cos(ang), (1, LANES // (HEAD_DIM // 2)))
    sin = jnp.tile(jnp.sin(ang), (1, LANES // (HEAD_DIM // 2)))
    lane = jnp.arange(LANES)[None, :]
    return cos, jnp.where((lane % HEAD_DIM) < HEAD_DIM // 2, -sin, sin)


def _rope_fwd(proj, cos, sin, name):
    t = proj.shape[0]
    scale = HEAD_DIM ** -0.5

    def body(p_ref, c_ref, s_ref, o_ref):
        cos_v, sin_v = c_ref[...], s_ref[...]
        for b in range(N_PBLK):
            cols = slice(b * LANES, (b + 1) * LANES)
            v = p_ref[:, cols]
            if b in ROPE_BLKS:
                v = v * cos_v + _swap_halves(v) * sin_v
            if b in QSCALE_BLKS:
                v = v * scale
            o_ref[:, cols] = v.astype(BF16)

    return pl.pallas_call(
        body, name=name, grid=(t // ROWS,),
        in_specs=[_row_spec(IN_COLS), _row_spec(LANES), _row_spec(LANES)], out_specs=_row_spec(IN_COLS),
        out_shape=jax.ShapeDtypeStruct((t, IN_COLS), BF16), compiler_params=_params(("arbitrary",), 40),
    )(proj, cos, sin)


def _rope_bwd(grads, cos, sin, name):
    t = grads[0].shape[0]
    scale = HEAD_DIM ** -0.5
    group = N_HEADS_B // N_KV_B

    def body(*refs):
        c_ref, s_ref, o_ref = refs[9], refs[10], refs[11]
        cos_v, sin_v = c_ref[...], s_ref[...]

        def kv_sum(ref):
            parts = []
            for g in range(N_KV_B):
                acc = ref[:, g * group * HEAD_DIM:(g * group + 1) * HEAD_DIM]
                for h in range(g * group + 1, (g + 1) * group):
                    acc = acc + ref[:, h * HEAD_DIM:(h + 1) * HEAD_DIM]
                parts.append(acc)
            return jnp.concatenate(parts, axis=1)

        def emit(b, v):
            if b in ROPE_BLKS:
                v = v * cos_v - _swap_halves(v) * sin_v
            if b in QSCALE_BLKS:
                v = v * scale
            o_ref[:, b * LANES:(b + 1) * LANES] = v.astype(BF16)

        starts = (A_BLK[0], A_BLK[1], A_BLK[2], B_BLK[0], None, None, C_BLK[0], C_BLK[1], C_BLK[2])
        for idx, start in enumerate(starts):
            if start is None:
                continue
            for j in range(refs[idx].shape[1] // LANES):
                emit(start + j, refs[idx][:, j * LANES:(j + 1) * LANES])
        emit(B_BLK[1], kv_sum(refs[4]))
        emit(B_BLK[2], kv_sum(refs[5]))

    return pl.pallas_call(
        body, name=name, grid=(t // ROWS,),
        in_specs=[_row_spec(g.shape[1]) for g in grads] + [_row_spec(LANES), _row_spec(LANES)],
        out_specs=_row_spec(IN_COLS),
        out_shape=jax.ShapeDtypeStruct((t, IN_COLS), BF16), compiler_params=_params(("arbitrary",), 40),
    )(*grads, cos, sin)


GROUP_COLS = ((0, WIDTH_A), (WIDTH_A, WIDTH_A + WIDTH_B), (WIDTH_A + WIDTH_B, D_MODEL))


def _mix_fwd(oa, ob, oc, gain, name):
    t = oa.shape[0]

    def body(a_ref, b_ref, c_ref, g_ref, o_ref):
        for ref, (lo, hi) in zip((a_ref, b_ref, c_ref), GROUP_COLS):
            _, n = _rms_stats(ref[...])
            o_ref[:, lo:hi] = (n * g_ref[:, lo:hi]).astype(BF16)

    return pl.pallas_call(
        body, name=name, grid=(t // ROWS,),
        in_specs=[_row_spec(WIDTH_A), _row_spec(WIDTH_B), _row_spec(WIDTH_C), _vec_spec(D_MODEL)],
        out_specs=_row_spec(D_MODEL),
        out_shape=jax.ShapeDtypeStruct((t, D_MODEL), BF16), compiler_params=_params(("arbitrary",), 32),
    )(oa, ob, oc, gain)


def _mix_bwd(oa, ob, oc, gain, dmixed, name):
    t = oa.shape[0]

    def body(a_ref, b_ref, c_ref, g_ref, dm_ref, da_ref, db_ref, dc_ref, dg_ref):
        first = pl.program_id(0) == 0
        for ref, dref, (lo, hi) in zip((a_ref, b_ref, c_ref), (da_ref, db_ref, dc_ref), GROUP_COLS):
            r, n = _rms_stats(ref[...])
            dm = dm_ref[:, lo:hi]
            dn = dm * g_ref[:, lo:hi]
            dref[...] = r * (dn - n * jnp.mean(dn * n, axis=-1, keepdims=True))
            part = jnp.sum(dm * n, axis=0, keepdims=True)

            @pl.when(first)
            def _():
                dg_ref[:, lo:hi] = part

            @pl.when(jnp.logical_not(first))
            def _():
                dg_ref[:, lo:hi] += part

    return pl.pallas_call(
        body, name=name, grid=(t // ROWS,),
        in_specs=[_row_spec(WIDTH_A), _row_spec(WIDTH_B), _row_spec(WIDTH_C), _vec_spec(D_MODEL), _row_spec(D_MODEL)],
        out_specs=[_row_spec(WIDTH_A), _row_spec(WIDTH_B), _row_spec(WIDTH_C), _vec_spec(D_MODEL)],
        out_shape=[jax.ShapeDtypeStruct((t, WIDTH_A), F32), jax.ShapeDtypeStruct((t, WIDTH_B), F32),
                   jax.ShapeDtypeStruct((t, WIDTH_C), F32), jax.ShapeDtypeStruct((1, D_MODEL), F32)],
        compiler_params=_params(("arbitrary",), 40),
    )(oa, ob, oc, gain, dmixed)


FF_COLS = 256


def _shift_rows(x, by):
    t = x.shape[0]
    row = lax.broadcasted_iota(jnp.int32, x.shape, 0)
    rolled = pltpu.roll(x, by % t, 0)
    return jnp.where(row == (0 if by == 1 else t - 1), 0.0, rolled)


def _conv(u0, w_ref, b_ref, h):
    return (_shift_rows(u0, 1) * w_ref[h, 0:1, :] + u0 * w_ref[h, 1:2, :]) + _shift_rows(u0, -1) * w_ref[h, 2:3, :] \
        + b_ref[h]


def _ff_specs(t):
    u_spec = pl.BlockSpec((2, t, FF_COLS), lambda j: (0, 0, j))
    w_spec = pl.BlockSpec((2, 3, FF_COLS), lambda j: (0, 0, j))
    b_spec = pl.BlockSpec((2, 1, FF_COLS), lambda j: (0, 0, j))
    return u_spec, w_spec, b_spec


def _convgate_fwd(u0, conv_w, conv_b, name):
    t = u0.shape[1]
    u_spec, w_spec, b_spec = _ff_specs(t)

    def body(u_ref, w_ref, b_ref, o_ref):
        gate = _conv(u_ref[0], w_ref, b_ref, 0)
        val = _conv(u_ref[1], w_ref, b_ref, 1)
        o_ref[...] = (gate * jax.nn.sigmoid(gate) * val).astype(BF16)

    return pl.pallas_call(
        body, name=name, grid=(D_FF // FF_COLS,), in_specs=[u_spec, w_spec, b_spec],
        out_specs=pl.BlockSpec((t, FF_COLS), lambda j: (0, j)),
        out_shape=jax.ShapeDtypeStruct((t, D_FF), BF16), compiler_params=_params(("arbitrary",), 48),
    )(u0, conv_w, conv_b)


def _convgate_bwd(u0, conv_w, conv_b, d_act, name):
    t = u0.shape[1]
    u_spec, w_spec, b_spec = _ff_specs(t)

    def body(u_ref, w_ref, b_ref, da_ref, du_ref, dw_ref, db_ref):
        gate = _conv(u_ref[0], w_ref, b_ref, 0)
        val = _conv(u_ref[1], w_ref, b_ref, 1)
        sig = jax.nn.sigmoid(gate)
        da = da_ref[...]
        d_half = (da * val * (sig * (1.0 + gate * (1.0 - sig))), da * (gate * sig))
        for h in range(2):
            du = d_half[h]
            u0_h = u_ref[h]
            db_ref[h] = jnp.sum(du, axis=0, keepdims=True)
            dw_ref[h, 0:1, :] = jnp.sum(du * _shift_rows(u0_h, 1), axis=0, keepdims=True)
            dw_ref[h, 1:2, :] = jnp.sum(du * u0_h, axis=0, keepdims=True)
            dw_ref[h, 2:3, :] = jnp.sum(du * _shift_rows(u0_h, -1), axis=0, keepdims=True)
            du_ref[h] = ((_shift_rows(du, -1) * w_ref[h, 0:1, :] + du * w_ref[h, 1:2, :])
                         + _shift_rows(du, 1) * w_ref[h, 2:3, :]).astype(BF16)

    return pl.pallas_call(
        body, name=name, grid=(D_FF // FF_COLS,),
        in_specs=[u_spec, w_spec, b_spec, pl.BlockSpec((t, FF_COLS), lambda j: (0, j))],
        out_specs=[u_spec, w_spec, b_spec],
        out_shape=[jax.ShapeDtypeStruct((2, t, D_FF), BF16), jax.ShapeDtypeStruct((2, 3, D_FF), F32),
                   jax.ShapeDtypeStruct((2, 1, D_FF), F32)],
        compiler_params=_params(("arbitrary",), 56),
    )(u0, conv_w, conv_b, d_act)


class _Group:
    def __init__(self, heads, blks, full, gqa, bias_per_head):
        self.heads = heads
        self.pairs = heads // 2
        self.q_blk, self.k_blk, self.v_blk = blks
        self.full = full
        self.gqa = gqa
        self.bias_per_head = bias_per_head
        self.n_win = NQB if full else 3
        self.width = heads * HEAD_DIM


GROUP_A = _Group(N_HEADS_A, A_BLK, True, False, False)
GROUP_B = _Group(N_HEADS_B, B_BLK, False, True, False)
GROUP_C = _Group(N_HEADS_C, C_BLK, False, False, True)


def _win_start(i):
    return jnp.clip(i - 1, 0, NQB - 3)


def _win_variant(i):
    return jnp.minimum(i, 1) + (i == NQB - 1).astype(jnp.int32)


def _attn_in_specs(grp, t):
    q_spec = pl.BlockSpec((QB, LANES), lambda p, i: (i, grp.q_blk + p))

    def col(blk):
        return (lambda p: blk) if grp.gqa else (lambda p: blk + p)

    def kv_specs(blk):
        c = col(blk)
        if grp.full:
            return [pl.BlockSpec((t, LANES), lambda p, i: (0, c(p)))]
        return [pl.BlockSpec((QB, LANES), functools.partial(lambda p, i, w: (_win_start(i) + w, c(p)), w=w))
                for w in range(3)]

    nwk = grp.n_win * QB
    if grp.bias_per_head:
        bias_spec = pl.BlockSpec((2, None, QB, nwk), lambda p, i: (p, _win_variant(i), 0, 0))
    elif grp.full:
        bias_spec = pl.BlockSpec((1, None, QB, nwk), lambda p, i: (0, i, 0, 0))
    else:
        bias_spec = pl.BlockSpec((1, None, QB, nwk), lambda p, i: (0, _win_variant(i), 0, 0))
    sink_spec = pl.BlockSpec((1, LANES), lambda p, i: (0, p))
    return q_spec, kv_specs(grp.k_blk), kv_specs(grp.v_blk), bias_spec, sink_spec


def _head_kv(grp, whole, e, p):
    lo, hi = whole[:, :HEAD_DIM], whole[:, HEAD_DIM:]
    if grp.gqa:
        return jnp.where(2 * p + e >= N_HEADS_B // N_KV_B, hi, lo)
    return hi if e else lo


def _softmax_parts(q, k, bias, sink):
    s = lax.dot_general(q, k, (((1,), (1,)), ((), ())), preferred_element_type=F32) + bias
    m = jnp.maximum(jnp.max(s, axis=-1, keepdims=True), sink)
    pe = jnp.exp(s - m)
    denom = jnp.sum(pe, axis=-1, keepdims=True) + jnp.exp(sink - m)
    return pe, m, 1.0 / denom


def _attn_fwd(grp, proj, bias, sink, name):
    t = proj.shape[0]
    q_spec, k_specs, v_specs, bias_spec, sink_spec = _attn_in_specs(grp, t)
    nkv = len(k_specs)

    def body(*refs):
        q_ref = refs[0]
        k_refs, v_refs = refs[1:1 + nkv], refs[1 + nkv:1 + 2 * nkv]
        bias_ref, sink_ref, o_ref = refs[1 + 2 * nkv:4 + 2 * nkv]
        p = pl.program_id(0)
        k_all = jnp.concatenate([r[...] for r in k_refs], axis=0)
        v_all = jnp.concatenate([r[...] for r in v_refs], axis=0)
        outs = []
        for e in range(2):
            q = q_ref[:, e * HEAD_DIM:(e + 1) * HEAD_DIM]
            k = _head_kv(grp, k_all, e, p)
            v = _head_kv(grp, v_all, e, p)
            snk = sink_ref[0:1, e * HEAD_DIM:e * HEAD_DIM + 1]
            pe, _, inv = _softmax_parts(q, k, bias_ref[e if grp.bias_per_head else 0], snk)
            outs.append(jnp.dot(pe.astype(BF16), v, preferred_element_type=F32) * inv)
        o_ref[...] = jnp.concatenate(outs, axis=1)

    return pl.pallas_call(
        body, name=name, grid=(grp.pairs, NQB),
        in_specs=[q_spec, *k_specs, *v_specs, bias_spec, sink_spec],
        out_specs=pl.BlockSpec((QB, LANES), lambda p, i: (i, p)),
        out_shape=jax.ShapeDtypeStruct((t, grp.width), F32),
        compiler_params=_params(("arbitrary", "arbitrary"), 48),
    )(proj, *([proj] * (2 * nkv)), bias, sink)


def _attn_bwd(grp, proj, bias, sink, out, d_out, name):
    t = proj.shape[0]
    q_spec, k_specs, v_specs, bias_spec, sink_spec = _attn_in_specs(grp, t)
    nkv = len(k_specs)
    n_off = 2 * NA_ROWS - 1
    rows_q = QB // GRID_W
    o_spec = pl.BlockSpec((QB, LANES), lambda p, i: (i, p))
    acc_spec = pl.BlockSpec((t, LANES), lambda p, i: (0, p))
    out_specs = [o_spec, acc_spec, acc_spec, pl.BlockSpec((None, 8, LANES), lambda p, i: (p, 0, 0))]
    out_shape = [jax.ShapeDtypeStruct((t, grp.width), F32)] * 3 + [jax.ShapeDtypeStruct((grp.pairs, 8, LANES), F32)]
    if grp.bias_per_head:
        out_specs.append(pl.BlockSpec((2, n_off, GRID_W, GRID_W), lambda p, i: (p, 0, 0, 0)))
        out_shape.append(jax.ShapeDtypeStruct((grp.heads, n_off, GRID_W, GRID_W), F32))

    def body(*refs):
        q_ref = refs[0]
        k_refs, v_refs = refs[1:1 + nkv], refs[1 + nkv:1 + 2 * nkv]
        bias_ref, sink_ref, o_ref, do_ref = refs[1 + 2 * nkv:5 + 2 * nkv]
        dq_ref, dk_ref, dv_ref, dsink_ref = refs[5 + 2 * nkv:9 + 2 * nkv]
        dbias_ref = refs[9 + 2 * nkv] if grp.bias_per_head else None
        p, i = pl.program_id(0), pl.program_id(1)

        @pl.when(i == 0)
        def _():
            dk_ref[...] = jnp.zeros_like(dk_ref)
            dv_ref[...] = jnp.zeros_like(dv_ref)
            dsink_ref[...] = jnp.zeros_like(dsink_ref)
            if dbias_ref is not None:
                dbias_ref[...] = jnp.zeros_like(dbias_ref)

        k_all = jnp.concatenate([r[...] for r in k_refs], axis=0)
        v_all = jnp.concatenate([r[...] for r in v_refs], axis=0)
        start = 0 if grp.full else _win_start(i)
        dqs, dks, dvs, dsinks = [], [], [], []
        for e in range(2):
            cols = slice(e * HEAD_DIM, (e + 1) * HEAD_DIM)
            q = q_ref[:, cols]
            k = _head_kv(grp, k_all, e, p)
            v = _head_kv(grp, v_all, e, p)
            snk = sink_ref[0:1, e * HEAD_DIM:e * HEAD_DIM + 1]
            pe, m, inv = _softmax_parts(q, k, bias_ref[e if grp.bias_per_head else 0], snk)
            prob = pe * inv
            do = do_ref[:, cols]
            do_b = do.astype(BF16)
            delta = jnp.sum(do * o_ref[:, cols], axis=-1, keepdims=True)
            dp = lax.dot_general(do_b, v, (((1,), (1,)), ((), ())), preferred_element_type=F32)
            ds = prob * (dp - delta)
            ds_b = ds.astype(BF16)
            dqs.append(jnp.dot(ds_b, k, preferred_element_type=F32))
            dks.append(lax.dot_general(ds_b, q, (((0,), (0,)), ((), ())), preferred_element_type=F32))
            dvs.append(lax.dot_general(prob.astype(BF16), do_b, (((0,), (0,)), ((), ())), preferred_element_type=F32))
            dsinks.append(-jnp.sum(jnp.exp(snk - m) * inv * delta, axis=0, keepdims=True))
            if dbias_ref is not None:
                shift = rows_q * (i - start)
                for rq in range(rows_q):
                    for rk in range(grp.n_win * rows_q):
                        off = jnp.clip(rk - rq + (NA_ROWS - 1) - shift, 0, n_off - 1)
                        dbias_ref[e, off] += ds[rq * GRID_W:(rq + 1) * GRID_W, rk * GRID_W:(rk + 1) * GRID_W]
        dq_ref[...] = jnp.concatenate(dqs, axis=1)
        rows = pl.ds(0, t) if grp.full else pl.ds(pl.multiple_of(start * QB, QB), grp.n_win * QB)
        dk_ref[rows, :] += jnp.concatenate(dks, axis=1)
        dv_ref[rows, :] += jnp.concatenate(dvs, axis=1)
        lane = lax.broadcasted_iota(jnp.int32, (8, LANES), 1)
        dsink_ref[...] += jnp.where(lane < HEAD_DIM, dsinks[0], dsinks[1])

    return pl.pallas_call(
        body, name=name, grid=(grp.pairs, NQB),
        in_specs=[q_spec, *k_specs, *v_specs, bias_spec, sink_spec, o_spec, o_spec],
        out_specs=out_specs, out_shape=out_shape,
        compiler_params=_params(("arbitrary", "arbitrary"), 56),
    )(proj, *([proj] * (2 * nkv)), bias, sink, out, d_out)


DILATED_CONFIGS = ((128, 1), (512, 4), (2048, 16))


def _bias_a():
    d = jnp.arange(SEQ)[None, :] - jnp.arange(SEQ)[:, None]
    mult = jnp.zeros((SEQ, SEQ), F32)
    for window, r in DILATED_CONFIGS:
        reach = (window // (2 * r)) * r
        mult = mult + ((d % r == 0) & (jnp.abs(d) <= reach)).astype(F32)
    return jnp.where(mult > 0, jnp.log(jnp.maximum(mult, 1.0)), NEG_INF).reshape(1, NQB, QB, SEQ)


def _bias_b():
    row = jnp.arange(QB)[None, :, None]
    col = jnp.arange(3 * QB)[None, None, :]
    var = jnp.arange(3)[:, None, None]
    d = col - (QB * var + row)
    return jnp.where(jnp.abs(d) <= WINDOW_B, 0.0, NEG_INF).astype(F32)[None]


def _offset_onehot():
    c = jnp.arange(GRID_W)[:, None, None]
    c2 = jnp.arange(GRID_W)[None, :, None]
    b = jnp.arange(LANES)[None, None, :]
    return (c2 - c + NA_COLS - 1 == b).astype(BF16).reshape(GRID_W * GRID_W, LANES)


def _split_dot(x, g):
    hi = x.astype(BF16)
    rest = x - hi.astype(F32)
    mid = rest.astype(BF16)
    lo = (rest - mid.astype(F32)).astype(BF16)
    return (jnp.dot(hi, g, preferred_element_type=F32) + jnp.dot(mid, g, preferred_element_type=F32)
            + jnp.dot(lo, g, preferred_element_type=F32))


def _table_mm(x, g, name):
    def body(x_ref, g_ref, o_ref):
        o_ref[...] = _split_dot(x_ref[...], g_ref[...])

    return pl.pallas_call(
        body, name=name, out_shape=jax.ShapeDtypeStruct((x.shape[0], g.shape[1]), F32),
        in_specs=[pl.BlockSpec(memory_space=pltpu.VMEM)] * 2, out_specs=pl.BlockSpec(memory_space=pltpu.VMEM),
        compiler_params=pltpu.CompilerParams(vmem_limit_bytes=32 * MIB),
    )(x, g)


N_OFF = 2 * NA_ROWS - 1
TABLE_ROWS = 152


def _bias_c(rpb):
    table = jnp.zeros((TABLE_ROWS, LANES), F32).at[:N_HEADS_C * N_OFF, :2 * NA_COLS - 1].set(
        rpb.reshape(N_HEADS_C * N_OFF, 2 * NA_COLS - 1))
    tiles = _table_mm(table, _offset_onehot().T, "rpb_tiles")[:N_HEADS_C * N_OFF]
    tiles = tiles.reshape(N_HEADS_C, N_OFF, GRID_W, GRID_W)
    c = jnp.arange(GRID_W)
    col_start = jnp.clip(c - NA_COLS // 2, 0, GRID_W - NA_COLS)
    col_ok = (c[None, :] >= col_start[:, None]) & (c[None, :] < col_start[:, None] + NA_COLS)
    tiles = jnp.where(col_ok, tiles, NEG_INF)
    masked = jnp.full((N_HEADS_C, GRID_W, GRID_W), NEG_INF, F32)
    rows_q = QB // GRID_W
    variants = []
    for var in range(3):
        q_rows = []
        for rq in range(rows_q):
            r_l = rows_q * var + rq
            first = min(max(r_l - NA_ROWS // 2, 0), 3 * rows_q - NA_ROWS)
            q_rows.append(jnp.concatenate(
                [tiles[:, rk - r_l + NA_ROWS - 1] if first <= rk < first + NA_ROWS else masked
                 for rk in range(3 * rows_q)], axis=-1))
        variants.append(jnp.concatenate(q_rows, axis=-2))
    return jnp.stack(variants, axis=1)


def _rpb_grad(d_tiles):
    flat = jnp.zeros((TABLE_ROWS, GRID_W * GRID_W), F32).at[:N_HEADS_C * N_OFF].set(
        d_tiles.reshape(N_HEADS_C * N_OFF, GRID_W * GRID_W))
    out = _table_mm(flat, _offset_onehot(), "rpb_grad")
    return out[:N_HEADS_C * N_OFF, :2 * NA_COLS - 1].reshape(N_HEADS_C, N_OFF, 2 * NA_COLS - 1)


def _sink_lanes(sink):
    return jnp.repeat(sink.astype(F32), HEAD_DIM)[None, :]


def _attention_fwd(proj_r, sink_b, bias_a, bias_b, bias_c):
    no_sink_a = jnp.full((1, WIDTH_A), NEG_INF, F32)
    no_sink_c = jnp.full((1, WIDTH_C), NEG_INF, F32)
    oa = _attn_fwd(GROUP_A, proj_r, bias_a, no_sink_a, "attn_a_fwd")
    ob = _attn_fwd(GROUP_B, proj_r, bias_b, _sink_lanes(sink_b), "attn_b_fwd")
    oc = _attn_fwd(GROUP_C, proj_r, bias_c, no_sink_c, "attn_c_fwd")
    return oa, ob, oc


def _attention_bwd(proj_r, sink_b, bias_a, bias_b, bias_c, outs, d_outs, cos, sin):
    no_sink_a = jnp.full((1, WIDTH_A), NEG_INF, F32)
    no_sink_c = jnp.full((1, WIDTH_C), NEG_INF, F32)
    dqa, dka, dva, _ = _attn_bwd(GROUP_A, proj_r, bias_a, no_sink_a, outs[0], d_outs[0], "attn_a_bwd")
    dqb, dkb, dvb, dsink = _attn_bwd(GROUP_B, proj_r, bias_b, _sink_lanes(sink_b), outs[1], d_outs[1], "attn_b_bwd")
    dqc, dkc, dvc, _, d_tiles = _attn_bwd(GROUP_C, proj_r, bias_c, no_sink_c, outs[2], d_outs[2], "attn_c_bwd")
    d_proj = _rope_bwd((dqa, dka, dva, dqb, dkb, dvb, dqc, dkc, dvc), cos, sin, "rope_bwd")
    d_sink = dsink[:, 0, :].reshape(GROUP_B.pairs, 2, HEAD_DIM)[:, :, 0].reshape(N_HEADS_B)
    return d_proj, d_sink, _rpb_grad(d_tiles)


def _adamw(w, g, m, v, name):
    r, c = w.shape
    rows = r
    for cand in (512, 256, 128, 64, 32, 16, 8):
        if r % cand == 0 and cand * c * 4 <= MIB:
            rows = cand
            break
    spec = pl.BlockSpec((rows, c), lambda i: (i, 0))

    def body(w_ref, g_ref, m_ref, v_ref, d_ref, mo_ref, vo_ref):
        grad = g_ref[...]
        m_new = ADAM_B1 * m_ref[...] + (1.0 - ADAM_B1) * grad
        v_new = ADAM_B2 * v_ref[...] + (1.0 - ADAM_B2) * jnp.square(grad)
        m_hat = m_new / (1.0 - ADAM_B1 ** ADAM_STEP)
        v_hat = v_new / (1.0 - ADAM_B2 ** ADAM_STEP)
        d_ref[...] = -ADAM_LR * (m_hat / (jnp.sqrt(v_hat) + ADAM_EPS) + ADAM_WD * w_ref[...])
        mo_ref[...] = m_new
        vo_ref[...] = v_new

    return pl.pallas_call(
        body, name=name, grid=(r // rows,), in_specs=[spec] * 4, out_specs=[spec] * 3,
        out_shape=[jax.ShapeDtypeStruct((r, c), F32)] * 3, compiler_params=_params(("arbitrary",), 32),
    )(w, g, m, v)


def _layer_fwd(x0, p, big, tabs, layer):
    h1 = _rmsnorm_fwd(x0, p["ln_attn"], "ln_attn_fwd")
    proj = _mm_nn(h1, big["w_in"], layer=layer, cols=True, tn=256, tk=D_MODEL, out_dtype=F32, name="mm_in")
    proj_r = _rope_fwd(proj, tabs["cos"], tabs["sin"], "rope_fwd")
    outs = _attention_fwd(proj_r, p["sink_b"], tabs["bias_a"], tabs["bias_b"], p["bias_c"])
    mixed = _mix_fwd(*outs, p["mix_gain"], "mix_fwd")
    x1 = _mm_nn(mixed, big["w_out"], layer=layer, cols=False, tn=256, tk=D_MODEL, out_dtype=F32, name="mm_out",
                residual=x0)
    h2 = _rmsnorm_fwd(x1, p["ln_ffn"], "ln_ffn_fwd")
    u0 = _mm_nn(h2, big["w_up"], layer=layer, cols=True, tn=256, tk=D_MODEL, out_dtype=F32, name="mm_up", out_split=2)
    act = _convgate_fwd(u0, p["conv_w"], p["conv_b"], "convgate_fwd")
    x2 = _mm_nn(act, big["w_down"], layer=layer, cols=False, tn=256, tk=D_FF // 2, out_dtype=F32, name="mm_down",
                residual=x1)
    return x2, (x0, h1, proj_r, outs, mixed, x1, h2, u0, act)


def _layer_bwd(dx2, dx2_b, saved, p, big, tabs, layer, gbufs):
    x0, h1, proj_r, outs, mixed, x1, h2, u0, act = saved
    d_act = _mm_nt(dx2_b, big["w_down"], layer=layer, cols=False, to=512, tr=D_MODEL, out_dtype=F32, name="nt_down")
    g_down = _mm_tn(act, dx2_b, tk=D_FF // N_SHARDS, tn=512, layer=layer, shards=-N_SHARDS, name="tn_down",
                    prev=gbufs.get("w_down"))
    du0, d_conv_w, d_conv_b = _convgate_bwd(u0, p["conv_w"], p["conv_b"], d_act, "convgate_bwd")
    dh2 = _mm_nt(du0, big["w_up"], layer=layer, cols=True, to=512, tr=D_FF // 4, out_dtype=F32, name="nt_up")
    g_up = _mm_tn(h2, du0, tk=512, tn=D_FF // 4, layer=layer, shards=N_SHARDS, name="tn_up", prev=gbufs.get("w_up"))
    dx1, dx1_b, d_ln_ffn = _rmsnorm_bwd(x1, p["ln_ffn"], dh2, dx2, "ln_ffn_bwd")
    d_mixed = _mm_nt(dx1_b, big["w_out"], layer=layer, cols=False, to=512, tr=D_MODEL, out_dtype=F32, name="nt_out")
    g_out = _mm_tn(mixed, dx1_b, tk=D_MODEL // N_SHARDS, tn=512, layer=layer, shards=-N_SHARDS, name="tn_out",
                   prev=gbufs.get("w_out"))
    *d_outs, d_mix_gain = _mix_bwd(*outs, p["mix_gain"], d_mixed, "mix_bwd")
    d_proj, d_sink, d_rpb = _attention_bwd(proj_r, p["sink_b"], tabs["bias_a"], tabs["bias_b"], p["bias_c"], outs,
                                           d_outs, tabs["cos"], tabs["sin"])
    dh1 = _mm_nt(d_proj, big["w_in"], layer=layer, cols=True, to=512, tr=IN_COLS // N_SHARDS, out_dtype=F32,
                 name="nt_in")
    g_in = _mm_tn(h1, d_proj, tk=512, tn=IN_COLS // N_SHARDS, layer=layer, shards=N_SHARDS, name="tn_in",
                  prev=gbufs.get("w_in"))
    dx0, dx0_b, d_ln_attn = _rmsnorm_bwd(x0, p["ln_attn"], dh1, dx1, "ln_attn_bwd")
    g_big = {"w_in": g_in, "w_out": g_out, "w_up": g_up, "w_down": g_down}
    small = {"ln_attn": d_ln_attn, "sink_b": d_sink, "rpb_c": d_rpb, "mix_gain": d_mix_gain, "ln_ffn": d_ln_ffn,
             "conv_w": d_conv_w, "conv_b": d_conv_b}
    return dx0, dx0_b, g_big, small


HBM_SPEC = pl.BlockSpec(memory_space=pl.ANY)


def _place():
    x, y, c = lax.axis_index("x"), lax.axis_index("y"), lax.axis_index("c")
    chips = ((1 - x, y), (x, 1 - y), (1 - x, 1 - y))
    return x, y, c, chips


def _shard_index(px, py):
    return 2 * px + py


def _remote(src, dst, send_sem, recv_sem, to):
    return pltpu.make_async_remote_copy(src_ref=src, dst_ref=dst, send_sem=send_sem, recv_sem=recv_sem,
                                        device_id=to, device_id_type=MESH)


def _gather_weights(tensors):
    n = len(tensors)

    def body(*refs):
        ins, outs = refs[:n], refs[n:2 * n]
        send1, recv1, send2, recv2, local_sem = refs[2 * n:]
        x, y, c, chips = _place()
        me = _shard_index(x, y)
        sibling = (x, y, 1 - c)
        local = []
        for t in range(n):
            for layer in range(DEPTH):
                cp = pltpu.make_async_copy(ins[t].at[layer], outs[t].at[layer, me], local_sem.at[t * DEPTH + layer])
                cp.start()
                local.append(cp)
        first = []
        for t in range(n):
            for j, (px, py) in enumerate(chips):
                cp = _remote(ins[t].at[c], outs[t].at[c, me], send1.at[t * 3 + j], recv1.at[t * 3 + j], (px, py, c))
                cp.start()
                first.append(cp)
        passed = []
        for t in range(n):
            for j, (px, py) in enumerate(chips):
                slot = outs[t].at[c, _shard_index(px, py)]
                _remote(slot, slot, send1.at[t * 3 + j], recv1.at[t * 3 + j], (px, py, c)).wait_recv()
                cp = _remote(slot, slot, send2.at[t * 3 + j], recv2.at[t * 3 + j], sibling)
                cp.start()
                passed.append(cp)
        for t in range(n):
            for j, (px, py) in enumerate(chips):
                slot = outs[t].at[1 - c, _shard_index(px, py)]
                _remote(slot, slot, send2.at[t * 3 + j], recv2.at[t * 3 + j], sibling).wait_recv()
        for cp in first + passed:
            cp.wait_send()
        for cp in local:
            cp.wait()

    return pl.pallas_call(
        body, name="gather_weights", in_specs=[HBM_SPEC] * n, out_specs=[HBM_SPEC] * n,
        out_shape=[jax.ShapeDtypeStruct((DEPTH, N_SHARDS) + ten.shape[1:], ten.dtype) for ten in tensors],
        scratch_shapes=[pltpu.SemaphoreType.DMA((n * 3,))] * 4 + [pltpu.SemaphoreType.DMA((n * DEPTH,))],
    )(*tensors)


def _pair_exchange(bufs):
    n = len(bufs)

    def body(*refs):
        ins, outs = refs[:n], refs[n:2 * n]
        send, recv = refs[2 * n:]
        x, y, c, _ = _place()
        sibling = (x, y, 1 - c)
        cps = [_remote(ins[t].at[1 - c], outs[t], send.at[t], recv.at[t], sibling) for t in range(n)]
        for cp in cps:
            cp.start()
        for cp in cps:
            cp.wait()

    return pl.pallas_call(
        body, name="pair_exchange", in_specs=[HBM_SPEC] * n, out_specs=[HBM_SPEC] * n,
        out_shape=[jax.ShapeDtypeStruct(b.shape[1:], b.dtype) for b in bufs],
        scratch_shapes=[pltpu.SemaphoreType.DMA((n,))] * 2,
    )(*bufs)


def _pair_sum(own, other, name):
    _, s, r, c_dim = own.shape
    rows = min(r, LANES)
    per = r // rows
    layer = lax.axis_index("c").astype(jnp.int32).reshape(1)

    def body(layer_ref, a_ref, b_ref, o_ref):
        o_ref[...] = (a_ref[...] + b_ref[...]).astype(BF16)

    return pl.pallas_call(
        body, name=name,
        grid_spec=pltpu.PrefetchScalarGridSpec(
            num_scalar_prefetch=1, grid=(s * per,),
            in_specs=[pl.BlockSpec((None, None, rows, c_dim), lambda i, lay: (lay[0], i // per, i % per, 0)),
                      pl.BlockSpec((None, rows, c_dim), lambda i, lay: (i // per, i % per, 0))],
            out_specs=pl.BlockSpec((None, rows, c_dim), lambda i, lay: (i // per, i % per, 0))),
        out_shape=jax.ShapeDtypeStruct((s, r, c_dim), BF16), compiler_params=_params(("arbitrary",), 40),
    )(layer, own, other)


def _chip_exchange(bufs):
    n = len(bufs)

    def body(*refs):
        ins, outs = refs[:n], refs[n:2 * n]
        send, recv, local_sem = refs[2 * n:]
        x, y, c, chips = _place()
        me = _shard_index(x, y)
        local = [pltpu.make_async_copy(ins[t].at[me], outs[t].at[me], local_sem.at[t]) for t in range(n)]
        for cp in local:
            cp.start()
        cps = []
        for t in range(n):
            for j, (px, py) in enumerate(chips):
                cp = _remote(ins[t].at[_shard_index(px, py)], outs[t].at[me], send.at[t * 3 + j], recv.at[t * 3 + j],
                             (px, py, c))
                cp.start()
                cps.append(cp)
        for t in range(n):
            for j, (px, py) in enumerate(chips):
                slot = outs[t].at[_shard_index(px, py)]
                _remote(slot, slot, send.at[t * 3 + j], recv.at[t * 3 + j], (px, py, c)).wait_recv()
        for cp in cps:
            cp.wait_send()
        for cp in local:
            cp.wait()

    return pl.pallas_call(
        body, name="chip_exchange", in_specs=[HBM_SPEC] * n, out_specs=[HBM_SPEC] * n,
        out_shape=[jax.ShapeDtypeStruct(b.shape, b.dtype) for b in bufs],
        scratch_shapes=[pltpu.SemaphoreType.DMA((n * 3,))] * 2 + [pltpu.SemaphoreType.DMA((n,))],
    )(*bufs)


def _sum_slots(buf, name):
    s, r, c_dim = buf.shape
    rows = min(r, LANES)

    def body(i_ref, o_ref):
        acc = i_ref[0].astype(F32)
        for k in range(1, s):
            acc = acc + i_ref[k].astype(F32)
        o_ref[...] = acc

    return pl.pallas_call(
        body, name=name, grid=(r // rows,),
        in_specs=[pl.BlockSpec((s, rows, c_dim), lambda i: (0, i, 0))],
        out_specs=pl.BlockSpec((rows, c_dim), lambda i: (i, 0)),
        out_shape=jax.ShapeDtypeStruct((r, c_dim), F32), compiler_params=_params(("arbitrary",), 40),
    )(buf)


def _pair_gather(bufs):
    n = len(bufs)

    def body(*refs):
        ins, outs = refs[:n], refs[n:2 * n]
        send, recv, local_sem = refs[2 * n:]
        x, y, c, _ = _place()
        sibling = (x, y, 1 - c)
        local = [pltpu.make_async_copy(ins[t], outs[t].at[c], local_sem.at[t]) for t in range(n)]
        cps = [_remote(ins[t], outs[t].at[c], send.at[t], recv.at[t], sibling) for t in range(n)]
        for cp in local + cps:
            cp.start()
        for t in range(n):
            slot = outs[t].at[1 - c]
            _remote(slot, slot, send.at[t], recv.at[t], sibling).wait_recv()
        for cp in cps:
            cp.wait_send()
        for cp in local:
            cp.wait()

    return pl.pallas_call(
        body, name="pair_gather", in_specs=[HBM_SPEC] * n, out_specs=[HBM_SPEC] * n,
        out_shape=[jax.ShapeDtypeStruct((DEPTH,) + b.shape, b.dtype) for b in bufs],
        scratch_shapes=[pltpu.SemaphoreType.DMA((n,))] * 3,
    )(*bufs)


N_DEV = 8


def _all_gather_small(vec):
    def body(v_ref, o_ref, send, recv, local_sem):
        x, y, c, _ = _place()
        me = 4 * x + 2 * y + c
        local = pltpu.make_async_copy(v_ref, o_ref.at[me], local_sem)
        local.start()
        flips = [(fx, fy, fc) for fx in (0, 1) for fy in (0, 1) for fc in (0, 1)][1:]
        peers = [((1 - x) if fx else x, (1 - y) if fy else y, (1 - c) if fc else c) for fx, fy, fc in flips]
        cps = [_remote(v_ref, o_ref.at[me], send.at[k], recv.at[k], peer) for k, peer in enumerate(peers)]
        for cp in cps:
            cp.start()
        for k, (px, py, pc) in enumerate(peers):
            slot = o_ref.at[4 * px + 2 * py + pc]
            _remote(slot, slot, send.at[k], recv.at[k], (px, py, pc)).wait_recv()
        for cp in cps:
            cp.wait_send()
        local.wait()

    return pl.pallas_call(
        body, name="all_gather_small", in_specs=[HBM_SPEC], out_specs=HBM_SPEC,
        out_shape=jax.ShapeDtypeStruct((N_DEV,) + vec.shape, vec.dtype),
        scratch_shapes=[pltpu.SemaphoreType.DMA((N_DEV - 1,))] * 2 + [pltpu.SemaphoreType.DMA(())],
    )(vec)


WEIGHT_NAMES = ("ln_attn", "w_in", "sink_b", "rpb_c", "mix_gain", "w_out", "ln_ffn", "w_up", "conv_w", "conv_b",
                "w_down", "ln_final")
BIG_NAMES = ("w_in", "w_out", "w_up", "w_down")
REPLICATED_NAMES = ("ln_attn", "sink_b", "rpb_c", "mix_gain", "ln_ffn", "conv_b", "ln_final")
PACK_TILE = 8 * LANES


def _pack(arrays, row_multiple):
    pieces = []
    for a in arrays:
        flat = a.reshape(-1)
        pieces.append(jnp.pad(flat, (0, (-flat.shape[0]) % PACK_TILE)))
    flat = jnp.concatenate(pieces)
    flat = jnp.pad(flat, (0, (-flat.shape[0]) % (row_multiple * LANES)))
    return flat.reshape(-1, LANES)


def _unpack(packed, shapes):
    flat = packed.reshape(-1)
    out, off = [], 0
    for shape in shapes:
        size = math.prod(shape)
        out.append(flat[off:off + size].reshape(shape))
        off += size + (-size) % PACK_TILE
    return out


def kernel(x, ln_attn, w_in, sink_b, rpb_c, mix_gain, w_out, ln_ffn, w_up, conv_w, conv_b, w_down, ln_final, loss_target, m_ln_attn, m_w_in, m_sink_b, m_rpb_c, m_mix_gain, m_w_out, m_ln_ffn, m_w_up, m_conv_w, m_conv_b, m_w_down, m_ln_final, v_ln_attn, v_w_in, v_sink_b, v_rpb_c, v_mix_gain, v_w_out, v_ln_ffn, v_w_up, v_conv_w, v_conv_b, v_w_down, v_ln_final):
    w = dict(ln_attn=ln_attn, w_in=w_in, sink_b=sink_b, rpb_c=rpb_c, mix_gain=mix_gain, w_out=w_out, ln_ffn=ln_ffn,
             w_up=w_up, conv_w=conv_w, conv_b=conv_b, w_down=w_down, ln_final=ln_final)
    m = dict(ln_attn=m_ln_attn, w_in=m_w_in, sink_b=m_sink_b, rpb_c=m_rpb_c, mix_gain=m_mix_gain, w_out=m_w_out,
             ln_ffn=m_ln_ffn, w_up=m_w_up, conv_w=m_conv_w, conv_b=m_conv_b, w_down=m_w_down, ln_final=m_ln_final)
    v = dict(ln_attn=v_ln_attn, w_in=v_w_in, sink_b=v_sink_b, rpb_c=v_rpb_c, mix_gain=v_mix_gain, w_out=v_w_out,
             ln_ffn=v_ln_ffn, w_up=v_w_up, conv_w=v_conv_w, conv_b=v_conv_b, w_down=v_w_down, ln_final=v_ln_final)
    shard = _shard_index(lax.axis_index("x"), lax.axis_index("y"))
    up_cols = w_up.shape[2]

    g_in, g_out, g_up, g_down, g_conv = _gather_weights(
        [w_in.astype(BF16), w_out.astype(BF16), w_up.astype(BF16), w_down.astype(BF16), conv_w])
    big = {"w_in": g_in, "w_out": g_out.reshape(DEPTH, 1, D_MODEL, D_MODEL), "w_up": g_up,
           "w_down": g_down.reshape(DEPTH, 1, D_FF, D_MODEL)}
    cos, sin = _rope_tables(SEQ)
    tabs = {"cos": cos, "sin": sin, "bias_a": _bias_a(), "bias_b": _bias_b()}
    layers = []
    for l in range(DEPTH):
        conv_w_l = g_conv[l].reshape(2, N_SHARDS // 2, 3, up_cols).transpose(0, 2, 1, 3).reshape(2, 3, D_FF)
        layers.append({"ln_attn": ln_attn[l][None], "sink_b": sink_b[l], "bias_c": _bias_c(rpb_c[l]),
                       "mix_gain": mix_gain[l][None], "ln_ffn": ln_ffn[l][None], "conv_w": conv_w_l,
                       "conv_b": conv_b[l].reshape(2, 1, D_FF)})

    act = x[0]
    saved = []
    for l in range(DEPTH):
        act, keep = _layer_fwd(act, layers[l], big, tabs, l)
        saved.append(keep)
    loss_part, dx, dx_b, d_ln_final = _loss_head(act, ln_final[None], loss_target[0], "loss_head")
    loss = lax.psum(loss_part[0, 0], ("x", "y", "c"))
    gbufs = {}
    small = [None] * DEPTH
    for l in reversed(range(DEPTH)):
        dx, dx_b, gbufs, small[l] = _layer_bwd(dx, dx_b, saved[l], layers[l], big, tabs, l, gbufs)

    partial = [gbufs[k] for k in BIG_NAMES]
    theirs = _pair_exchange(partial)
    pair = [_pair_sum(own, other, "pair_sum_" + k) for k, own, other in zip(BIG_NAMES, partial, theirs)]
    landed = _chip_exchange(pair)
    mine = [_sum_slots(buf, "chip_sum_" + k) for k, buf in zip(BIG_NAMES, landed)]
    grads = dict(zip(BIG_NAMES, _pair_gather(mine)))

    stacked = {k: jnp.stack([small[l][k] for l in range(DEPTH)]) for k in small[0]}
    part = {"ln_attn": stacked["ln_attn"][:, 0], "sink_b": stacked["sink_b"], "rpb_c": stacked["rpb_c"],
            "mix_gain": stacked["mix_gain"][:, 0], "ln_ffn": stacked["ln_ffn"][:, 0],
            "conv_b": stacked["conv_b"].reshape(DEPTH, 2 * D_FF), "ln_final": d_ln_final[0],
            "conv_w": stacked["conv_w"].transpose(0, 2, 1, 3).reshape(DEPTH, 3, 2 * D_FF)}
    names = REPLICATED_NAMES + ("conv_w",)
    total = _sum_slots(_all_gather_small(_pack([part[k] for k in names], 256)), "sum_small")
    for k, g in zip(names, _unpack(total, [part[k].shape for k in names])):
        grads[k] = g
    grads["conv_w"] = lax.dynamic_slice_in_dim(grads["conv_w"], shard * up_cols, up_cols, axis=2)

    delta, new_m, new_v = {}, {}, {}
    for k in BIG_NAMES + ("conv_w",):
        shape = w[k].shape
        flat = (shape[0] * shape[1], shape[2])
        res = _adamw(w[k].reshape(flat), grads[k].reshape(flat), m[k].reshape(flat), v[k].reshape(flat), "adamw_" + k)
        delta[k], new_m[k], new_v[k] = (r.reshape(shape) for r in res)
    shapes = [w[k].shape for k in REPLICATED_NAMES]
    packed = [_pack([d[k] for k in REPLICATED_NAMES], 128) for d in (w, grads, m, v)]
    for d, res in zip((delta, new_m, new_v), _adamw(*packed, "adamw_small")):
        for k, r in zip(REPLICATED_NAMES, _unpack(res, shapes)):
            d[k] = r

    return (loss, dx[None], *[grads[k] for k in WEIGHT_NAMES], *[delta[k] for k in WEIGHT_NAMES],
            *[new_m[k] for k in WEIGHT_NAMES], *[new_v[k] for k in WEIGHT_NAMES])
```

```python
import functools
import math

import jax
import jax.numpy as jnp
from jax import lax
from jax.experimental import pallas as pl
from jax.experimental.pallas import tpu as pltpu

F32 = jnp.float32
BF16 = jnp.bfloat16
MESH = pl.DeviceIdType.MESH

D_MODEL = 2048
SEQ = 2048
DEPTH = 2
HEAD_DIM = 64
N_HEADS_A = 12
N_HEADS_B = 10
N_KV_B = 2
N_HEADS_C = 10
WINDOW_B = 128
GRID_W = 64
NA_ROWS = 8
NA_COLS = 16
WIDTH_A = N_HEADS_A * HEAD_DIM
WIDTH_B = N_HEADS_B * HEAD_DIM
WIDTH_C = N_HEADS_C * HEAD_DIM
IN_COLS = 5120
D_FF = 5632
ROPE_THETA = 10000.0
EPS = 1e-6
NEG_INF = -1e30
N_SHARDS = 4

ADAM_LR = 0.001
ADAM_B1 = 0.9
ADAM_B2 = 0.999
ADAM_EPS = 1e-08
ADAM_WD = 0.01
ADAM_STEP = 10

LANES = 128
QB = 256
NQB = SEQ // QB
ROWS = 256
MIB = 2 ** 20

A_BLK = (0, 6, 12)
B_BLK = (18, 23, 24)
C_BLK = (25, 30, 35)
ROPE_BLKS = tuple(range(0, 12)) + tuple(range(18, 24))
QSCALE_BLKS = tuple(range(0, 6)) + tuple(range(18, 23)) + tuple(range(25, 30))
N_PBLK = IN_COLS // LANES


def _params(sem, vmem_mib):
    return pltpu.CompilerParams(dimension_semantics=sem, vmem_limit_bytes=vmem_mib * MIB)


def _weight_spec(w, layer, cols, t_in, t_out, transposed):
    _, s, r, c = w.shape
    if cols:
        per = c // t_out
        k_dim, n = r, s * c
        if transposed:
            index = lambda j, rr: (layer, rr // per, j, rr % per)
        else:
            index = lambda j, kk: (layer, j // per, kk, j % per)
    else:
        per = r // t_in
        k_dim, n = s * r, c
        if transposed:
            index = lambda j, rr: (layer, j // per, j % per, rr)
        else:
            index = lambda j, kk: (layer, kk // per, kk % per, j)
    return pl.BlockSpec((None, None, t_in, t_out), index), k_dim, n


def _mm_nn(a, w, *, layer, cols, tn, tk, out_dtype, name, residual=None, out_split=1):
    m, k_dim = a.shape
    w_spec, k_w, n = _weight_spec(w, layer, cols, tk, tn, False)
    assert k_w == k_dim
    nj, nk = n // tn, k_dim // tk
    in_specs = [pl.BlockSpec((m, tk), lambda j, k: (0, k)), w_spec]
    args = [a, w]
    if residual is not None:
        in_specs.append(pl.BlockSpec((m, tn), lambda j, k: (0, j)))
        args.append(residual)
    if out_split > 1:
        per_o = n // out_split // tn
        out_spec = pl.BlockSpec((None, m, tn), lambda j, k: (j // per_o, 0, j % per_o))
        out_shape = jax.ShapeDtypeStruct((out_split, m, n // out_split), out_dtype)
    else:
        out_spec = pl.BlockSpec((m, tn), lambda j, k: (0, j))
        out_shape = jax.ShapeDtypeStruct((m, n), out_dtype)

    def body(*refs):
        a_ref, w_ref = refs[0], refs[1]
        r_ref = refs[2] if residual is not None else None
        o_ref = refs[3] if residual is not None else refs[2]

        def finish(val):
            if r_ref is not None:
                val = r_ref[...] + val
            o_ref[...] = val.astype(o_ref.dtype)

        part = jnp.dot(a_ref[...], w_ref[...], preferred_element_type=F32)
        if nk == 1:
            finish(part)
        else:
            acc = refs[-1]
            kk = pl.program_id(1)

            @pl.when(kk == 0)
            def _():
                acc[...] = part

            @pl.when(kk > 0)
            def _():
                acc[...] += part

            @pl.when(kk == nk - 1)
            def _():
                finish(acc[...])

    return pl.pallas_call(
        body, name=name, grid=(nj, nk), in_specs=in_specs, out_specs=out_spec, out_shape=out_shape,
        scratch_shapes=[pltpu.VMEM((m, tn), F32)] if nk > 1 else [],
        compiler_params=_params(("arbitrary", "arbitrary"), 56),
    )(*args)


def _mm_nt(dy, w, *, layer, cols, to, tr, out_dtype, name):
    if dy.ndim == 3:
        m = dy.shape[1]
        n = dy.shape[0] * dy.shape[2]
        per_d = dy.shape[2] // tr
        dy_spec = pl.BlockSpec((None, m, tr), lambda j, r: (r // per_d, 0, r % per_d))
    else:
        m, n = dy.shape
        dy_spec = pl.BlockSpec((m, tr), lambda j, r: (0, r))
    w_spec, k_dim, n_w = _weight_spec(w, layer, cols, to, tr, True)
    assert n_w == n
    nj, nr = k_dim // to, n // tr

    def body(dy_ref, w_ref, o_ref, *scratch):
        part = lax.dot_general(dy_ref[...], w_ref[...], (((1,), (1,)), ((), ())), preferred_element_type=F32)
        if nr == 1:
            o_ref[...] = part.astype(o_ref.dtype)
        else:
            acc = scratch[0]
            rr = pl.program_id(1)

            @pl.when(rr == 0)
            def _():
                acc[...] = part

            @pl.when(rr > 0)
            def _():
                acc[...] += part

            @pl.when(rr == nr - 1)
            def _():
                o_ref[...] = acc[...].astype(o_ref.dtype)

    return pl.pallas_call(
        body, name=name, grid=(nj, nr), in_specs=[dy_spec, w_spec],
        out_specs=pl.BlockSpec((m, to), lambda j, r: (0, j)),
        out_shape=jax.ShapeDtypeStruct((m, k_dim), out_dtype),
        scratch_shapes=[pltpu.VMEM((m, to), F32)] if nr > 1 else [],
        compiler_params=_params(("arbitrary", "arbitrary"), 56),
    )(dy, w)


def _mm_tn(x, dy, *, tk, tn, layer, shards, name, prev=None):
    m, k_dim = x.shape
    if dy.ndim == 3:
        n = dy.shape[0] * dy.shape[2]
        per_d = dy.shape[2] // tn
        dy_spec = pl.BlockSpec((None, m, tn), lambda i, j: (j // per_d, 0, j % per_d))
    else:
        n = dy.shape[1]
        dy_spec = pl.BlockSpec((m, tn), lambda i, j: (0, j))
    if shards > 0:
        per = n // shards // tn
        out_shape = jax.ShapeDtypeStruct((DEPTH, shards, k_dim, n // shards), F32)
        out_spec = pl.BlockSpec((None, None, tk, tn), lambda i, j: (layer, j // per, i, j % per))
    else:
        s = -shards
        per = k_dim // s // tk
        out_shape = jax.ShapeDtypeStruct((DEPTH, s, k_dim // s, n), F32)
        out_spec = pl.BlockSpec((None, None, tk, tn), lambda i, j: (layer, i // per, i % per, j))
    in_specs = [pl.BlockSpec((m, tk), lambda i, j: (0, i)), dy_spec]
    args = [x, dy]
    aliases = {}
    if prev is not None:
        in_specs.append(pl.BlockSpec(memory_space=pl.ANY))
        args.append(prev)
        aliases = {2: 0}

    def body(x_ref, dy_ref, *rest):
        o_ref = rest[-1]
        o_ref[...] = lax.dot_general(x_ref[...], dy_ref[...], (((0,), (0,)), ((), ())), preferred_element_type=F32)

    return pl.pallas_call(
        body, name=name, grid=(k_dim // tk, n // tn), in_specs=in_specs, out_specs=out_spec, out_shape=out_shape,
        input_output_aliases=aliases,
        compiler_params=_params(("arbitrary", "arbitrary"), 56),
    )(*args)


def _row_spec(width, rows=ROWS):
    return pl.BlockSpec((rows, width), lambda i: (i, 0))


def _vec_spec(width):
    return pl.BlockSpec((1, width), lambda i: (0, 0))


def _rms_stats(x):
    r = lax.rsqrt(jnp.mean(x * x, axis=-1, keepdims=True) + EPS)
    return r, x * r


def _rmsnorm_fwd(x, gain, name):
    t, d = x.shape

    def body(x_ref, g_ref, o_ref):
        _, n = _rms_stats(x_ref[...])
        o_ref[...] = (n * g_ref[...]).astype(BF16)

    return pl.pallas_call(
        body, name=name, grid=(t // ROWS,), in_specs=[_row_spec(d), _vec_spec(d)], out_specs=_row_spec(d),
        out_shape=jax.ShapeDtypeStruct((t, d), BF16), compiler_params=_params(("arbitrary",), 32),
    )(x, gain)


def _rmsnorm_bwd(x, gain, dh, dres, name):
    t, d = x.shape

    def body(x_ref, g_ref, dh_ref, dres_ref, dx_ref, dxb_ref, dg_ref):
        r, n = _rms_stats(x_ref[...])
        dh_v = dh_ref[...]
        dn = dh_v * g_ref[...]
        dx = dres_ref[...] + r * (dn - n * jnp.mean(dn * n, axis=-1, keepdims=True))
        dx_ref[...] = dx
        dxb_ref[...] = dx.astype(BF16)
        part = jnp.sum(dh_v * n, axis=0, keepdims=True)

        @pl.when(pl.program_id(0) == 0)
        def _():
            dg_ref[...] = part

        @pl.when(pl.program_id(0) > 0)
        def _():
            dg_ref[...] += part

    return pl.pallas_call(
        body, name=name, grid=(t // ROWS,),
        in_specs=[_row_spec(d), _vec_spec(d), _row_spec(d), _row_spec(d)],
        out_specs=[_row_spec(d), _row_spec(d), _vec_spec(d)],
        out_shape=[jax.ShapeDtypeStruct((t, d), F32), jax.ShapeDtypeStruct((t, d), BF16),
                   jax.ShapeDtypeStruct((1, d), F32)],
        compiler_params=_params(("arbitrary",), 40),
    )(x, gain, dh, dres)


def _loss_head(x, gain, target, name):
    t, d = x.shape

    def body(x_ref, g_ref, t_ref, loss_ref, dx_ref, dxb_ref, dg_ref):
        r, n = _rms_stats(x_ref[...])
        g = g_ref[...]
        err = n * g - t_ref[...]
        dy = err * (1.0 / d)
        dn = dy * g
        dx = r * (dn - n * jnp.mean(dn * n, axis=-1, keepdims=True))
        dx_ref[...] = dx
        dxb_ref[...] = dx.astype(BF16)
        part = jnp.sum(dy * n, axis=0, keepdims=True)
        lpart = jnp.zeros((8, LANES), F32) + 0.5 * jnp.sum(jnp.mean(err * err, axis=-1, keepdims=True))

        @pl.when(pl.program_id(0) == 0)
        def _():
            dg_ref[...] = part
            loss_ref[...] = lpart

        @pl.when(pl.program_id(0) > 0)
        def _():
            dg_ref[...] += part
            loss_ref[...] += lpart

    return pl.pallas_call(
        body, name=name, grid=(t // ROWS,),
        in_specs=[_row_spec(d), _vec_spec(d), _row_spec(d)],
        out_specs=[pl.BlockSpec((8, LANES), lambda i: (0, 0)), _row_spec(d), _row_spec(d), _vec_spec(d)],
        out_shape=[jax.ShapeDtypeStruct((8, LANES), F32), jax.ShapeDtypeStruct((t, d), F32),
                   jax.ShapeDtypeStruct((t, d), BF16), jax.ShapeDtypeStruct((1, d), F32)],
        compiler_params=_params(("arbitrary",), 40),
    )(x, gain, target)


def _swap_halves(x):
    lane = lax.broadcasted_iota(jnp.int32, x.shape, 1)
    return jnp.where((lane % HEAD_DIM) < HEAD_DIM // 2, pltpu.roll(x, LANES - HEAD_DIM // 2, 1),
                     pltpu.roll(x, HEAD_DIM // 2, 1))


def _rope_tables(t):
    inv_freq = ROPE_THETA ** (-jnp.arange(0, HEAD_DIM, 2, dtype=F32) / HEAD_DIM)
    ang = jnp.arange(t, dtype=F32)[:, None] * inv_freq[None, :]
    cos = jnp.tile(jnp.cos(ang), (1, LANES // (HEAD_DIM // 2)))
    sin = jnp.tile(jnp.sin(ang), (1, LANES // (HEAD_DIM // 2)))
    lane = jnp.arange(LANES)[None, :]
    return cos, jnp.where((lane % HEAD_DIM) < HEAD_DIM // 2, -sin, sin)


def _rope_fwd(proj, cos, sin, name):
    t = proj.shape[0]
    scale = HEAD_DIM ** -0.5

    def body(p_ref, c_ref, s_ref, o_ref):
        cos_v, sin_v = c_ref[...], s_ref[...]
        for b in range(N_PBLK):
            cols = slice(b * LANES, (b + 1) * LANES)
            v = p_ref[:, cols]
            if b in ROPE_BLKS:
                v = v * cos_v + _swap_halves(v) * sin_v
            if b in QSCALE_BLKS:
                v = v * scale
            o_ref[:, cols] = v.astype(BF16)

    return pl.pallas_call(
        body, name=name, grid=(t // ROWS,),
        in_specs=[_row_spec(IN_COLS), _row_spec(LANES), _row_spec(LANES)], out_specs=_row_spec(IN_COLS),
        out_shape=jax.ShapeDtypeStruct((t, IN_COLS), BF16), compiler_params=_params(("arbitrary",), 40),
    )(proj, cos, sin)


def _rope_bwd(grads, cos, sin, name):
    t = grads[0].shape[0]
    scale = HEAD_DIM ** -0.5
    group = N_HEADS_B // N_KV_B

    def body(*refs):
        c_ref, s_ref, o_ref = refs[9], refs[10], refs[11]
        cos_v, sin_v = c_ref[...], s_ref[...]

        def kv_sum(ref):
            parts = []
            for g in range(N_KV_B):
                acc = ref[:, g * group * HEAD_DIM:(g * group + 1) * HEAD_DIM]
                for h in range(g * group + 1, (g + 1) * group):
                    acc = acc + ref[:, h * HEAD_DIM:(h + 1) * HEAD_DIM]
                parts.append(acc)
            return jnp.concatenate(parts, axis=1)

        def emit(b, v):
            if b in ROPE_BLKS:
                v = v * cos_v - _swap_halves(v) * sin_v
            if b in QSCALE_BLKS:
                v = v * scale
            o_ref[:, b * LANES:(b + 1) * LANES] = v.astype(BF16)

        starts = (A_BLK[0], A_BLK[1], A_BLK[2], B_BLK[0], None, None, C_BLK[0], C_BLK[1], C_BLK[2])
        for idx, start in enumerate(starts):
            if start is None:
                continue
            for j in range(refs[idx].shape[1] // LANES):
                emit(start + j, refs[idx][:, j * LANES:(j + 1) * LANES])
        emit(B_BLK[1], kv_sum(refs[4]))
        emit(B_BLK[2], kv_sum(refs[5]))

    return pl.pallas_call(
        body, name=name, grid=(t // ROWS,),
        in_specs=[_row_spec(g.shape[1]) for g in grads] + [_row_spec(LANES), _row_spec(LANES)],
        out_specs=_row_spec(IN_COLS),
        out_shape=jax.ShapeDtypeStruct((t, IN_COLS), BF16), compiler_params=_params(("arbitrary",), 40),
    )(*grads, cos, sin)


GROUP_COLS = ((0, WIDTH_A), (WIDTH_A, WIDTH_A + WIDTH_B), (WIDTH_A + WIDTH_B, D_MODEL))


def _mix_fwd(oa, ob, oc, gain, name):
    t = oa.shape[0]

    def body(a_ref, b_ref, c_ref, g_ref, o_ref):
        for ref, (lo, hi) in zip((a_ref, b_ref, c_ref), GROUP_COLS):
            _, n = _rms_stats(ref[...])
            o_ref[:, lo:hi] = (n * g_ref[:, lo:hi]).astype(BF16)

    return pl.pallas_call(
        body, name=name, grid=(t // ROWS,),
        in_specs=[_row_spec(WIDTH_A), _row_spec(WIDTH_B), _row_spec(WIDTH_C), _vec_spec(D_MODEL)],
        out_specs=_row_spec(D_MODEL),
        out_shape=jax.ShapeDtypeStruct((t, D_MODEL), BF16), compiler_params=_params(("arbitrary",), 32),
    )(oa, ob, oc, gain)


def _mix_bwd(oa, ob, oc, gain, dmixed, name):
    t = oa.shape[0]

    def body(a_ref, b_ref, c_ref, g_ref, dm_ref, da_ref, db_ref, dc_ref, dg_ref):
        first = pl.program_id(0) == 0
        for ref, dref, (lo, hi) in zip((a_ref, b_ref, c_ref), (da_ref, db_ref, dc_ref), GROUP_COLS):
            r, n = _rms_stats(ref[...])
            dm = dm_ref[:, lo:hi]
            dn = dm * g_ref[:, lo:hi]
            dref[...] = r * (dn - n * jnp.mean(dn * n, axis=-1, keepdims=True))
            part = jnp.sum(dm * n, axis=0, keepdims=True)

            @pl.when(first)
            def _():
                dg_ref[:, lo:hi] = part

            @pl.when(jnp.logical_not(first))
            def _():
                dg_ref[:, lo:hi] += part

    return pl.pallas_call(
        body, name=name, grid=(t // ROWS,),
        in_specs=[_row_spec(WIDTH_A), _row_spec(WIDTH_B), _row_spec(WIDTH_C), _vec_spec(D_MODEL), _row_spec(D_MODEL)],
        out_specs=[_row_spec(WIDTH_A), _row_spec(WIDTH_B), _row_spec(WIDTH_C), _vec_spec(D_MODEL)],
        out_shape=[jax.ShapeDtypeStruct((t, WIDTH_A), F32), jax.ShapeDtypeStruct((t, WIDTH_B), F32),
                   jax.ShapeDtypeStruct((t, WIDTH_C), F32), jax.ShapeDtypeStruct((1, D_MODEL), F32)],
        compiler_params=_params(("arbitrary",), 40),
    )(oa, ob, oc, gain, dmixed)


FF_COLS = 256


def _shift_rows(x, by):
    t = x.shape[0]
    row = lax.broadcasted_iota(jnp.int32, x.shape, 0)
    rolled = pltpu.roll(x, by % t, 0)
    return jnp.where(row == (0 if by == 1 else t - 1), 0.0, rolled)


def _conv(u0, w_ref, b_ref, h):
    return (_shift_rows(u0, 1) * w_ref[h, 0:1, :] + u0 * w_ref[h, 1:2, :]) + _shift_rows(u0, -1) * w_ref[h, 2:3, :] \
        + b_ref[h]


def _ff_specs(t):
    u_spec = pl.BlockSpec((2, t, FF_COLS), lambda j: (0, 0, j))
    w_spec = pl.BlockSpec((2, 3, FF_COLS), lambda j: (0, 0, j))
    b_spec = pl.BlockSpec((2, 1, FF_COLS), lambda j: (0, 0, j))
    return u_spec, w_spec, b_spec


def _convgate_fwd(u0, conv_w, conv_b, name):
    t = u0.shape[1]
    u_spec, w_spec, b_spec = _ff_specs(t)

    def body(u_ref, w_ref, b_ref, o_ref):
        gate = _conv(u_ref[0], w_ref, b_ref, 0)
        val = _conv(u_ref[1], w_ref, b_ref, 1)
        o_ref[...] = (gate * jax.nn.sigmoid(gate) * val).astype(BF16)

    return pl.pallas_call(
        body, name=name, grid=(D_FF // FF_COLS,), in_specs=[u_spec, w_spec, b_spec],
        out_specs=pl.BlockSpec((t, FF_COLS), lambda j: (0, j)),
        out_shape=jax.ShapeDtypeStruct((t, D_FF), BF16), compiler_params=_params(("arbitrary",), 48),
    )(u0, conv_w, conv_b)


def _convgate_bwd(u0, conv_w, conv_b, d_act, name):
    t = u0.shape[1]
    u_spec, w_spec, b_spec = _ff_specs(t)

    def body(u_ref, w_ref, b_ref, da_ref, du_ref, dw_ref, db_ref):
        gate = _conv(u_ref[0], w_ref, b_ref, 0)
        val = _conv(u_ref[1], w_ref, b_ref, 1)
        sig = jax.nn.sigmoid(gate)
        da = da_ref[...]
        d_half = (da * val * (sig * (1.0 + gate * (1.0 - sig))), da * (gate * sig))
        for h in range(2):
            du = d_half[h]
            u0_h = u_ref[h]
            db_ref[h] = jnp.sum(du, axis=0, keepdims=True)
            dw_ref[h, 0:1, :] = jnp.sum(du * _shift_rows(u0_h, 1), axis=0, keepdims=True)
            dw_ref[h, 1:2, :] = jnp.sum(du * u0_h, axis=0, keepdims=True)
            dw_ref[h, 2:3, :] = jnp.sum(du * _shift_rows(u0_h, -1), axis=0, keepdims=True)
            du_ref[h] = ((_shift_rows(du, -1) * w_ref[h, 0:1, :] + du * w_ref[h, 1:2, :])
                         + _shift_rows(du, 1) * w_ref[h, 2:3, :]).astype(BF16)

    return pl.pallas_call(
        body, name=name, grid=(D_FF // FF_COLS,),
        in_specs=[u_spec, w_spec, b_spec, pl.BlockSpec((t, FF_COLS), lambda j: (0, j))],
        out_specs=[u_spec, w_spec, b_spec],
        out_shape=[jax.ShapeDtypeStruct((2, t, D_FF), BF16), jax.ShapeDtypeStruct((2, 3, D_FF), F32),
                   jax.ShapeDtypeStruct((2, 1, D_FF), F32)],
        compiler_params=_params(("arbitrary",), 56),
    )(u0, conv_w, conv_b, d_act)


class _Group:
    def __init__(self, heads, blks, full, gqa, bias_per_head):
        self.heads = heads
        self.pairs = heads // 2
        self.q_blk, self.k_blk, self.v_blk = blks
        self.full = full
        self.gqa = gqa
        self.bias_per_head = bias_per_head
        self.n_win = NQB if full else 3
        self.width = heads * HEAD_DIM


GROUP_A = _Group(N_HEADS_A, A_BLK, True, False, False)
GROUP_B = _Group(N_HEADS_B, B_BLK, False, True, False)
GROUP_C = _Group(N_HEADS_C, C_BLK, False, False, True)


def _win_start(i):
    return jnp.clip(i - 1, 0, NQB - 3)


def _win_variant(i):
    return jnp.minimum(i, 1) + (i == NQB - 1).astype(jnp.int32)


def _attn_in_specs(grp, t):
    q_spec = pl.BlockSpec((QB, LANES), lambda p, i: (i, grp.q_blk + p))

    def col(blk):
        return (lambda p: blk) if grp.gqa else (lambda p: blk + p)

    def kv_specs(blk):
        c = col(blk)
        if grp.full:
            return [pl.BlockSpec((t, LANES), lambda p, i: (0, c(p)))]
        return [pl.BlockSpec((QB, LANES), functools.partial(lambda p, i, w: (_win_start(i) + w, c(p)), w=w))
                for w in range(3)]

    nwk = grp.n_win * QB
    if grp.bias_per_head:
        bias_spec = pl.BlockSpec((2, None, QB, nwk), lambda p, i: (p, _win_variant(i), 0, 0))
    elif grp.full:
        bias_spec = pl.BlockSpec((1, None, QB, nwk), lambda p, i: (0, i, 0, 0))
    else:
        bias_spec = pl.BlockSpec((1, None, QB, nwk), lambda p, i: (0, _win_variant(i), 0, 0))
    sink_spec = pl.BlockSpec((1, LANES), lambda p, i: (0, p))
    return q_spec, kv_specs(grp.k_blk), kv_specs(grp.v_blk), bias_spec, sink_spec


def _head_kv(grp, whole, e, p):
    lo, hi = whole[:, :HEAD_DIM], whole[:, HEAD_DIM:]
    if grp.gqa:
        return jnp.where(2 * p + e >= N_HEADS_B // N_KV_B, hi, lo)
    return hi if e else lo


def _softmax_parts(q, k, bias, sink):
    s = lax.dot_general(q, k, (((1,), (1,)), ((), ())), preferred_element_type=F32) + bias
    m = jnp.maximum(jnp.max(s, axis=-1, keepdims=True), sink)
    pe = jnp.exp(s - m)
    denom = jnp.sum(pe, axis=-1, keepdims=True) + jnp.exp(sink - m)
    return pe, m, 1.0 / denom


def _attn_fwd(grp, proj, bias, sink, name):
    t = proj.shape[0]
    q_spec, k_specs, v_specs, bias_spec, sink_spec = _attn_in_specs(grp, t)
    nkv = len(k_specs)

    def body(*refs):
        q_ref = refs[0]
        k_refs, v_refs = refs[1:1 + nkv], refs[1 + nkv:1 + 2 * nkv]
        bias_ref, sink_ref, o_ref = refs[1 + 2 * nkv:4 + 2 * nkv]
        p = pl.program_id(0)
        k_all = jnp.concatenate([r[...] for r in k_refs], axis=0)
        v_all = jnp.concatenate([r[...] for r in v_refs], axis=0)
        outs = []
        for e in range(2):
            q = q_ref[:, e * HEAD_DIM:(e + 1) * HEAD_DIM]
            k = _head_kv(grp, k_all, e, p)
            v = _head_kv(grp, v_all, e, p)
            snk = sink_ref[0:1, e * HEAD_DIM:e * HEAD_DIM + 1]
            pe, _, inv = _softmax_parts(q, k, bias_ref[e if grp.bias_per_head else 0], snk)
            outs.append(jnp.dot(pe.astype(BF16), v, preferred_element_type=F32) * inv)
        o_ref[...] = jnp.concatenate(outs, axis=1)

    return pl.pallas_call(
        body, name=name, grid=(grp.pairs, NQB),
        in_specs=[q_spec, *k_specs, *v_specs, bias_spec, sink_spec],
        out_specs=pl.BlockSpec((QB, LANES), lambda p, i: (i, p)),
        out_shape=jax.ShapeDtypeStruct((t, grp.width), F32),
        compiler_params=_params(("arbitrary", "arbitrary"), 48),
    )(proj, *([proj] * (2 * nkv)), bias, sink)


def _attn_bwd(grp, proj, bias, sink, out, d_out, name):
    t = proj.shape[0]
    q_spec, k_specs, v_specs, bias_spec, sink_spec = _attn_in_specs(grp, t)
    nkv = len(k_specs)
    n_off = 2 * NA_ROWS - 1
    rows_q = QB // GRID_W
    o_spec = pl.BlockSpec((QB, LANES), lambda p, i: (i, p))
    acc_spec = pl.BlockSpec((t, LANES), lambda p, i: (0, p))
    out_specs = [o_spec, acc_spec, acc_spec, pl.BlockSpec((None, 8, LANES), lambda p, i: (p, 0, 0))]
    out_shape = [jax.ShapeDtypeStruct((t, grp.width), F32)] * 3 + [jax.ShapeDtypeStruct((grp.pairs, 8, LANES), F32)]
    if grp.bias_per_head:
        out_specs.append(pl.BlockSpec((2, n_off, GRID_W, GRID_W), lambda p, i: (p, 0, 0, 0)))
        out_shape.append(jax.ShapeDtypeStruct((grp.heads, n_off, GRID_W, GRID_W), F32))

    def body(*refs):
        q_ref = refs[0]
        k_refs, v_refs = refs[1:1 + nkv], refs[1 + nkv:1 + 2 * nkv]
        bias_ref, sink_ref, o_ref, do_ref = refs[1 + 2 * nkv:5 + 2 * nkv]
        dq_ref, dk_ref, dv_ref, dsink_ref = refs[5 + 2 * nkv:9 + 2 * nkv]
        dbias_ref = refs[9 + 2 * nkv] if grp.bias_per_head else None
        p, i = pl.program_id(0), pl.program_id(1)

        @pl.when(i == 0)
        def _():
            dk_ref[...] = jnp.zeros_like(dk_ref)
            dv_ref[...] = jnp.zeros_like(dv_ref)
            dsink_ref[...] = jnp.zeros_like(dsink_ref)
            if dbias_ref is not None:
                dbias_ref[...] = jnp.zeros_like(dbias_ref)

        k_all = jnp.concatenate([r[...] for r in k_refs], axis=0)
        v_all = jnp.concatenate([r[...] for r in v_refs], axis=0)
        start = 0 if grp.full else _win_start(i)
        dqs, dks, dvs, dsinks = [], [], [], []
        for e in range(2):
            cols = slice(e * HEAD_DIM, (e + 1) * HEAD_DIM)
            q = q_ref[:, cols]
            k = _head_kv(grp, k_all, e, p)
            v = _head_kv(grp, v_all, e, p)
            snk = sink_ref[0:1, e * HEAD_DIM:e * HEAD_DIM + 1]
            pe, m, inv = _softmax_parts(q, k, bias_ref[e if grp.bias_per_head else 0], snk)
            prob = pe * inv
            do = do_ref[:, cols]
            do_b = do.astype(BF16)
            delta = jnp.sum(do * o_ref[:, cols], axis=-1, keepdims=True)
            dp = lax.dot_general(do_b, v, (((1,), (1,)), ((), ())), preferred_element_type=F32)
            ds = prob * (dp - delta)
            ds_b = ds.astype(BF16)
            dqs.append(jnp.dot(ds_b, k, preferred_element_type=F32))
            dks.append(lax.dot_general(ds_b, q, (((0,), (0,)), ((), ())), preferred_element_type=F32))
            dvs.append(lax.dot_general(prob.astype(BF16), do_b, (((0,), (0,)), ((), ())), preferred_element_type=F32))
            dsinks.append(-jnp.sum(jnp.exp(snk - m) * inv * delta, axis=0, keepdims=True))
            if dbias_ref is not None:
                shift = rows_q * (i - start)
                for rq in range(rows_q):
                    for rk in range(grp.n_win * rows_q):
                        off = jnp.clip(rk - rq + (NA_ROWS - 1) - shift, 0, n_off - 1)
                        dbias_ref[e, off] += ds[rq * GRID_W:(rq + 1) * GRID_W, rk * GRID_W:(rk + 1) * GRID_W]
        dq_ref[...] = jnp.concatenate(dqs, axis=1)
        rows = pl.ds(0, t) if grp.full else pl.ds(pl.multiple_of(start * QB, QB), grp.n_win * QB)
        dk_ref[rows, :] += jnp.concatenate(dks, axis=1)
        dv_ref[rows, :] += jnp.concatenate(dvs, axis=1)
        lane = lax.broadcasted_iota(jnp.int32, (8, LANES), 1)
        dsink_ref[...] += jnp.where(lane < HEAD_DIM, dsinks[0], dsinks[1])

    return pl.pallas_call(
        body, name=name, grid=(grp.pairs, NQB),
        in_specs=[q_spec, *k_specs, *v_specs, bias_spec, sink_spec, o_spec, o_spec],
        out_specs=out_specs, out_shape=out_shape,
        compiler_params=_params(("arbitrary", "arbitrary"), 56),
    )(proj, *([proj] * (2 * nkv)), bias, sink, out, d_out)


DILATED_CONFIGS = ((128, 1), (512, 4), (2048, 16))


def _bias_a():
    d = jnp.arange(SEQ)[None, :] - jnp.arange(SEQ)[:, None]
    mult = jnp.zeros((SEQ, SEQ), F32)
    for window, r in DILATED_CONFIGS:
        reach = (window // (2 * r)) * r
        mult = mult + ((d % r == 0) & (jnp.abs(d) <= reach)).astype(F32)
    return jnp.where(mult > 0, jnp.log(jnp.maximum(mult, 1.0)), NEG_INF).reshape(1, NQB, QB, SEQ)


def _bias_b():
    row = jnp.arange(QB)[None, :, None]
    col = jnp.arange(3 * QB)[None, None, :]
    var = jnp.arange(3)[:, None, None]
    d = col - (QB * var + row)
    return jnp.where(jnp.abs(d) <= WINDOW_B, 0.0, NEG_INF).astype(F32)[None]


def _offset_onehot():
    c = jnp.arange(GRID_W)[:, None, None]
    c2 = jnp.arange(GRID_W)[None, :, None]
    b = jnp.arange(LANES)[None, None, :]
    return (c2 - c + NA_COLS - 1 == b).astype(BF16).reshape(GRID_W * GRID_W, LANES)


def _split_dot(x, g):
    hi = x.astype(BF16)
    rest = x - hi.astype(F32)
    mid = rest.astype(BF16)
    lo = (rest - mid.astype(F32)).astype(BF16)
    return (jnp.dot(hi, g, preferred_element_type=F32) + jnp.dot(mid, g, preferred_element_type=F32)
            + jnp.dot(lo, g, preferred_element_type=F32))


def _table_mm(x, g, name):
    def body(x_ref, g_ref, o_ref):
        o_ref[...] = _split_dot(x_ref[...], g_ref[...])

    return pl.pallas_call(
        body, name=name, out_shape=jax.ShapeDtypeStruct((x.shape[0], g.shape[1]), F32),
        in_specs=[pl.BlockSpec(memory_space=pltpu.VMEM)] * 2, out_specs=pl.BlockSpec(memory_space=pltpu.VMEM),
        compiler_params=pltpu.CompilerParams(vmem_limit_bytes=32 * MIB),
    )(x, g)


N_OFF = 2 * NA_ROWS - 1
TABLE_ROWS = 152


def _bias_c(rpb):
    table = jnp.zeros((TABLE_ROWS, LANES), F32).at[:N_HEADS_C * N_OFF, :2 * NA_COLS - 1].set(
        rpb.reshape(N_HEADS_C * N_OFF, 2 * NA_COLS - 1))
    tiles = _table_mm(table, _offset_onehot().T, "rpb_tiles")[:N_HEADS_C * N_OFF]
    tiles = tiles.reshape(N_HEADS_C, N_OFF, GRID_W, GRID_W)
    c = jnp.arange(GRID_W)
    col_start = jnp.clip(c - NA_COLS // 2, 0, GRID_W - NA_COLS)
    col_ok = (c[None, :] >= col_start[:, None]) & (c[None, :] < col_start[:, None] + NA_COLS)
    tiles = jnp.where(col_ok, tiles, NEG_INF)
    masked = jnp.full((N_HEADS_C, GRID_W, GRID_W), NEG_INF, F32)
    rows_q = QB // GRID_W
    variants = []
    for var in range(3):
        q_rows = []
        for rq in range(rows_q):
            r_l = rows_q * var + rq
            first = min(max(r_l - NA_ROWS // 2, 0), 3 * rows_q - NA_ROWS)
            q_rows.append(jnp.concatenate(
                [tiles[:, rk - r_l + NA_ROWS - 1] if first <= rk < first + NA_ROWS else masked
                 for rk in range(3 * rows_q)], axis=-1))
        variants.append(jnp.concatenate(q_rows, axis=-2))
    return jnp.stack(variants, axis=1)


def _rpb_grad(d_tiles):
    flat = jnp.zeros((TABLE_ROWS, GRID_W * GRID_W), F32).at[:N_HEADS_C * N_OFF].set(
        d_tiles.reshape(N_HEADS_C * N_OFF, GRID_W * GRID_W))
    out = _table_mm(flat, _offset_onehot(), "rpb_grad")
    return out[:N_HEADS_C * N_OFF, :2 * NA_COLS - 1].reshape(N_HEADS_C, N_OFF, 2 * NA_COLS - 1)


def _sink_lanes(sink):
    return jnp.repeat(sink.astype(F32), HEAD_DIM)[None, :]


def _attention_fwd(proj_r, sink_b, bias_a, bias_b, bias_c):
    no_sink_a = jnp.full((1, WIDTH_A), NEG_INF, F32)
    no_sink_c = jnp.full((1, WIDTH_C), NEG_INF, F32)
    oa = _attn_fwd(GROUP_A, proj_r, bias_a, no_sink_a, "attn_a_fwd")
    ob = _attn_fwd(GROUP_B, proj_r, bias_b, _sink_lanes(sink_b), "attn_b_fwd")
    oc = _attn_fwd(GROUP_C, proj_r, bias_c, no_sink_c, "attn_c_fwd")
    return oa, ob, oc


def _attention_bwd(proj_r, sink_b, bias_a, bias_b, bias_c, outs, d_outs, cos, sin):
    no_sink_a = jnp.full((1, WIDTH_A), NEG_INF, F32)
    no_sink_c = jnp.full((1, WIDTH_C), NEG_INF, F32)
    dqa, dka, dva, _ = _attn_bwd(GROUP_A, proj_r, bias_a, no_sink_a, outs[0], d_outs[0], "attn_a_bwd")
    dqb, dkb, dvb, dsink = _attn_bwd(GROUP_B, proj_r, bias_b, _sink_lanes(sink_b), outs[1], d_outs[1], "attn_b_bwd")
    dqc, dkc, dvc, _, d_tiles = _attn_bwd(GROUP_C, proj_r, bias_c, no_sink_c, outs[2], d_outs[2], "attn_c_bwd")
    d_proj = _rope_bwd((dqa, dka, dva, dqb, dkb, dvb, dqc, dkc, dvc), cos, sin, "rope_bwd")
    d_sink = dsink[:, 0, :].reshape(GROUP_B.pairs, 2, HEAD_DIM)[:, :, 0].reshape(N_HEADS_B)
    return d_proj, d_sink, _rpb_grad(d_tiles)


def _adamw(w, g, m, v, name):
    r, c = w.shape
    rows = r
    for cand in (512, 256, 128, 64, 32, 16, 8):
        if r % cand == 0 and cand * c * 4 <= MIB:
            rows = cand
            break
    spec = pl.BlockSpec((rows, c), lambda i: (i, 0))

    def body(w_ref, g_ref, m_ref, v_ref, d_ref, mo_ref, vo_ref):
        grad = g_ref[...]
        m_new = ADAM_B1 * m_ref[...] + (1.0 - ADAM_B1) * grad
        v_new = ADAM_B2 * v_ref[...] + (1.0 - ADAM_B2) * jnp.square(grad)
        m_hat = m_new / (1.0 - ADAM_B1 ** ADAM_STEP)
        v_hat = v_new / (1.0 - ADAM_B2 ** ADAM_STEP)
        d_ref[...] = -ADAM_LR * (m_hat / (jnp.sqrt(v_hat) + ADAM_EPS) + ADAM_WD * w_ref[...])
        mo_ref[...] = m_new
        vo_ref[...] = v_new

    return pl.pallas_call(
        body, name=name, grid=(r // rows,), in_specs=[spec] * 4, out_specs=[spec] * 3,
        out_shape=[jax.ShapeDtypeStruct((r, c), F32)] * 3, compiler_params=_params(("arbitrary",), 32),
    )(w, g, m, v)


def _layer_fwd(x0, p, big, tabs, layer):
    h1 = _rmsnorm_fwd(x0, p["ln_attn"], "ln_attn_fwd")
    proj = _mm_nn(h1, big["w_in"], layer=layer, cols=True, tn=256, tk=D_MODEL, out_dtype=F32, name="mm_in")
    proj_r = _rope_fwd(proj, tabs["cos"], tabs["sin"], "rope_fwd")
    outs = _attention_fwd(proj_r, p["sink_b"], tabs["bias_a"], tabs["bias_b"], p["bias_c"])
    mixed = _mix_fwd(*outs, p["mix_gain"], "mix_fwd")
    x1 = _mm_nn(mixed, big["w_out"], layer=layer, cols=False, tn=256, tk=D_MODEL, out_dtype=F32, name="mm_out",
                residual=x0)
    h2 = _rmsnorm_fwd(x1, p["ln_ffn"], "ln_ffn_fwd")
    u0 = _mm_nn(h2, big["w_up"], layer=layer, cols=True, tn=256, tk=D_MODEL, out_dtype=F32, name="mm_up", out_split=2)
    act = _convgate_fwd(u0, p["conv_w"], p["conv_b"], "convgate_fwd")
    x2 = _mm_nn(act, big["w_down"], layer=layer, cols=False, tn=256, tk=D_FF // 2, out_dtype=F32, name="mm_down",
                residual=x1)
    return x2, (x0, h1, proj_r, outs, mixed, x1, h2, u0, act)


def _layer_bwd(dx2, dx2_b, saved, p, big, tabs, layer, gbufs):
    x0, h1, proj_r, outs, mixed, x1, h2, u0, act = saved
    d_act = _mm_nt(dx2_b, big["w_down"], layer=layer, cols=False, to=512, tr=D_MODEL, out_dtype=F32, name="nt_down")
    g_down = _mm_tn(act, dx2_b, tk=D_FF // N_SHARDS, tn=512, layer=layer, shards=-N_SHARDS, name="tn_down",
                    prev=gbufs.get("w_down"))
    du0, d_conv_w, d_conv_b = _convgate_bwd(u0, p["conv_w"], p["conv_b"], d_act, "convgate_bwd")
    dh2 = _mm_nt(du0, big["w_up"], layer=layer, cols=True, to=512, tr=D_FF // 4, out_dtype=F32, name="nt_up")
    g_up = _mm_tn(h2, du0, tk=512, tn=D_FF // 4, layer=layer, shards=N_SHARDS, name="tn_up", prev=gbufs.get("w_up"))
    dx1, dx1_b, d_ln_ffn = _rmsnorm_bwd(x1, p["ln_ffn"], dh2, dx2, "ln_ffn_bwd")
    d_mixed = _mm_nt(dx1_b, big["w_out"], layer=layer, cols=False, to=512, tr=D_MODEL, out_dtype=F32, name="nt_out")
    g_out = _mm_tn(mixed, dx1_b, tk=D_MODEL // N_SHARDS, tn=512, layer=layer, shards=-N_SHARDS, name="tn_out",
                   prev=gbufs.get("w_out"))
    *d_outs, d_mix_gain = _mix_bwd(*outs, p["mix_gain"], d_mixed, "mix_bwd")
    d_proj, d_sink, d_rpb = _attention_bwd(proj_r, p["sink_b"], tabs["bias_a"], tabs["bias_b"], p["bias_c"], outs,
                                           d_outs, tabs["cos"], tabs["sin"])
    dh1 = _mm_nt(d_proj, big["w_in"], layer=layer, cols=True, to=512, tr=IN_COLS // N_SHARDS, out_dtype=F32,
                 name="nt_in")
    g_in = _mm_tn(h1, d_proj, tk=512, tn=IN_COLS // N_SHARDS, layer=layer, shards=N_SHARDS, name="tn_in",
                  prev=gbufs.get("w_in"))
    dx0, dx0_b, d_ln_attn = _rmsnorm_bwd(x0, p["ln_attn"], dh1, dx1, "ln_attn_bwd")
    g_big = {"w_in": g_in, "w_out": g_out, "w_up": g_up, "w_down": g_down}
    small = {"ln_attn": d_ln_attn, "sink_b": d_sink, "rpb_c": d_rpb, "mix_gain": d_mix_gain, "ln_ffn": d_ln_ffn,
             "conv_w": d_conv_w, "conv_b": d_conv_b}
    return dx0, dx0_b, g_big, small


HBM_SPEC = pl.BlockSpec(memory_space=pl.ANY)


def _place():
    x, y, c = lax.axis_index("x"), lax.axis_index("y"), lax.axis_index("c")
    chips = ((1 - x, y), (x, 1 - y), (1 - x, 1 - y))
    return x, y, c, chips


def _shard_index(px, py):
    return 2 * px + py


def _remote(src, dst, send_sem, recv_sem, to):
    return pltpu.make_async_remote_copy(src_ref=src, dst_ref=dst, send_sem=send_sem, recv_sem=recv_sem,
                                        device_id=to, device_id_type=MESH)


def _own_slot(w, shard, dtype, name):
    depth, r, c_dim = w.shape
    rows = r
    for cand in (512, 256, 128):
        if r % cand == 0 and cand * c_dim * 4 <= 2 * MIB:
            rows = cand
            break

    def body(s_ref, w_ref, o_ref):
        o_ref[...] = w_ref[...].astype(dtype)

    return pl.pallas_call(
        body, name=name,
        grid_spec=pltpu.PrefetchScalarGridSpec(
            num_scalar_prefetch=1, grid=(depth, r // rows),
            in_specs=[pl.BlockSpec((None, rows, c_dim), lambda l, i, s: (l, i, 0))],
            out_specs=pl.BlockSpec((None, None, rows, c_dim), lambda l, i, s: (l, s[0], i, 0))),
        out_shape=jax.ShapeDtypeStruct((depth, N_SHARDS, r, c_dim), dtype),
        compiler_params=_params(("arbitrary", "arbitrary"), 32),
    )(shard.astype(jnp.int32).reshape(1), w)


def _gather_weights(bufs):
    n = len(bufs)

    def body(*refs):
        outs = refs[n:2 * n]
        send1, recv1, send2, recv2 = refs[2 * n:]
        x, y, c, chips = _place()
        me = _shard_index(x, y)
        sibling = (x, y, 1 - c)
        first = []
        for t in range(n):
            for j, (px, py) in enumerate(chips):
                mine = outs[t].at[c, me]
                cp = _remote(mine, mine, send1.at[t * 3 + j], recv1.at[t * 3 + j], (px, py, c))
                cp.start()
                first.append(cp)
        passed = []
        for t in range(n):
            for j, (px, py) in enumerate(chips):
                slot = outs[t].at[c, _shard_index(px, py)]
                _remote(slot, slot, send1.at[t * 3 + j], recv1.at[t * 3 + j], (px, py, c)).wait_recv()
                cp = _remote(slot, slot, send2.at[t * 3 + j], recv2.at[t * 3 + j], sibling)
                cp.start()
                passed.append(cp)
        for t in range(n):
            for j, (px, py) in enumerate(chips):
                slot = outs[t].at[1 - c, _shard_index(px, py)]
                _remote(slot, slot, send2.at[t * 3 + j], recv2.at[t * 3 + j], sibling).wait_recv()
        for cp in first + passed:
            cp.wait_send()

    return pl.pallas_call(
        body, name="gather_weights", in_specs=[HBM_SPEC] * n, out_specs=[HBM_SPEC] * n,
        out_shape=[jax.ShapeDtypeStruct(b.shape, b.dtype) for b in bufs],
        input_output_aliases={t: t for t in range(n)},
        scratch_shapes=[pltpu.SemaphoreType.DMA((n * 3,))] * 4,
    )(*bufs)


def _pair_exchange(bufs):
    n = len(bufs)

    def body(*refs):
        ins, outs = refs[:n], refs[n:2 * n]
        send, recv = refs[2 * n:]
        x, y, c, _ = _place()
        sibling = (x, y, 1 - c)
        cps = [_remote(ins[t].at[1 - c], outs[t], send.at[t], recv.at[t], sibling) for t in range(n)]
        for cp in cps:
            cp.start()
        for cp in cps:
            cp.wait()

    return pl.pallas_call(
        body, name="pair_exchange", in_specs=[HBM_SPEC] * n, out_specs=[HBM_SPEC] * n,
        out_shape=[jax.ShapeDtypeStruct(b.shape[1:], b.dtype) for b in bufs],
        scratch_shapes=[pltpu.SemaphoreType.DMA((n,))] * 2,
    )(*bufs)


def _pair_sum(own, other, name):
    _, s, r, c_dim = own.shape
    rows = min(r, LANES)
    per = r // rows
    layer = lax.axis_index("c").astype(jnp.int32).reshape(1)

    def body(layer_ref, a_ref, b_ref, o_ref):
        o_ref[...] = (a_ref[...] + b_ref[...]).astype(BF16)

    return pl.pallas_call(
        body, name=name,
        grid_spec=pltpu.PrefetchScalarGridSpec(
            num_scalar_prefetch=1, grid=(s * per,),
            in_specs=[pl.BlockSpec((None, None, rows, c_dim), lambda i, lay: (lay[0], i // per, i % per, 0)),
                      pl.BlockSpec((None, rows, c_dim), lambda i, lay: (i // per, i % per, 0))],
            out_specs=pl.BlockSpec((None, rows, c_dim), lambda i, lay: (i // per, i % per, 0))),
        out_shape=jax.ShapeDtypeStruct((s, r, c_dim), BF16), compiler_params=_params(("arbitrary",), 40),
    )(layer, own, other)


def _chip_exchange(bufs):
    n = len(bufs)

    def body(*refs):
        ins, outs = refs[:n], refs[n:2 * n]
        send, recv = refs[2 * n:]
        x, y, c, chips = _place()
        me = _shard_index(x, y)
        cps = []
        for t in range(n):
            for j, (px, py) in enumerate(chips):
                cp = _remote(ins[t].at[_shard_index(px, py)], outs[t].at[me], send.at[t * 3 + j], recv.at[t * 3 + j],
                             (px, py, c))
                cp.start()
                cps.append(cp)
        for t in range(n):
            for j, (px, py) in enumerate(chips):
                slot = outs[t].at[_shard_index(px, py)]
                _remote(slot, slot, send.at[t * 3 + j], recv.at[t * 3 + j], (px, py, c)).wait_recv()
        for cp in cps:
            cp.wait_send()

    return pl.pallas_call(
        body, name="chip_exchange", in_specs=[HBM_SPEC] * n, out_specs=[HBM_SPEC] * n,
        out_shape=[jax.ShapeDtypeStruct(b.shape, b.dtype) for b in bufs],
        scratch_shapes=[pltpu.SemaphoreType.DMA((n * 3,))] * 2,
    )(*bufs)


def _chip_sum(pair, landed, name):
    s, r, c_dim = pair.shape
    rows = min(r, LANES)
    shard = _shard_index(lax.axis_index("x"), lax.axis_index("y"))
    where = jnp.stack([shard, lax.axis_index("c")]).astype(jnp.int32)

    def landed_spec(k):
        return pl.BlockSpec((None, rows, c_dim), lambda i, w: (jnp.where(w[0] == k, (k + 1) % s, k), i, 0))

    def body(w_ref, own_ref, *rest):
        o_ref = rest[s]
        acc = None
        for k in range(s):
            term = jnp.where(w_ref[0] == k, own_ref[...], rest[k][...]).astype(F32)
            acc = term if acc is None else acc + term
        o_ref[...] = acc

    return pl.pallas_call(
        body, name=name,
        grid_spec=pltpu.PrefetchScalarGridSpec(
            num_scalar_prefetch=1, grid=(r // rows,),
            in_specs=[pl.BlockSpec((None, rows, c_dim), lambda i, w: (w[0], i, 0))] + [landed_spec(k) for k in range(s)],
            out_specs=pl.BlockSpec((None, rows, c_dim), lambda i, w: (w[1], i, 0))),
        out_shape=jax.ShapeDtypeStruct((DEPTH, r, c_dim), F32), compiler_params=_params(("arbitrary",), 40),
    )(where, pair, *([landed] * s))


def _sum_slots(buf, name):
    s, r, c_dim = buf.shape
    rows = min(r, LANES)

    def body(i_ref, o_ref):
        acc = i_ref[0].astype(F32)
        for k in range(1, s):
            acc = acc + i_ref[k].astype(F32)
        o_ref[...] = acc

    return pl.pallas_call(
        body, name=name, grid=(r // rows,),
        in_specs=[pl.BlockSpec((s, rows, c_dim), lambda i: (0, i, 0))],
        out_specs=pl.BlockSpec((rows, c_dim), lambda i: (i, 0)),
        out_shape=jax.ShapeDtypeStruct((r, c_dim), F32), compiler_params=_params(("arbitrary",), 40),
    )(buf)


def _pair_gather(bufs):
    n = len(bufs)

    def body(*refs):
        outs = refs[n:2 * n]
        send, recv = refs[2 * n:]
        x, y, c, _ = _place()
        sibling = (x, y, 1 - c)
        cps = [_remote(outs[t].at[c], outs[t].at[c], send.at[t], recv.at[t], sibling) for t in range(n)]
        for cp in cps:
            cp.start()
        for t in range(n):
            slot = outs[t].at[1 - c]
            _remote(slot, slot, send.at[t], recv.at[t], sibling).wait_recv()
        for cp in cps:
            cp.wait_send()

    return pl.pallas_call(
        body, name="pair_gather", in_specs=[HBM_SPEC] * n, out_specs=[HBM_SPEC] * n,
        out_shape=[jax.ShapeDtypeStruct(b.shape, b.dtype) for b in bufs],
        input_output_aliases={t: t for t in range(n)},
        scratch_shapes=[pltpu.SemaphoreType.DMA((n,))] * 2,
    )(*bufs)


N_DEV = 8


def _all_gather_small(vec):
    def body(v_ref, o_ref, send, recv, local_sem):
        x, y, c, _ = _place()
        me = 4 * x + 2 * y + c
        local = pltpu.make_async_copy(v_ref, o_ref.at[me], local_sem)
        local.start()
        flips = [(fx, fy, fc) for fx in (0, 1) for fy in (0, 1) for fc in (0, 1)][1:]
        peers = [((1 - x) if fx else x, (1 - y) if fy else y, (1 - c) if fc else c) for fx, fy, fc in flips]
        cps = [_remote(v_ref, o_ref.at[me], send.at[k], recv.at[k], peer) for k, peer in enumerate(peers)]
        for cp in cps:
            cp.start()
        for k, (px, py, pc) in enumerate(peers):
            slot = o_ref.at[4 * px + 2 * py + pc]
            _remote(slot, slot, send.at[k], recv.at[k], (px, py, pc)).wait_recv()
        for cp in cps:
            cp.wait_send()
        local.wait()

    return pl.pallas_call(
        body, name="all_gather_small", in_specs=[HBM_SPEC], out_specs=HBM_SPEC,
        out_shape=jax.ShapeDtypeStruct((N_DEV,) + vec.shape, vec.dtype),
        scratch_shapes=[pltpu.SemaphoreType.DMA((N_DEV - 1,))] * 2 + [pltpu.SemaphoreType.DMA(())],
    )(vec)


WEIGHT_NAMES = ("ln_attn", "w_in", "sink_b", "rpb_c", "mix_gain", "w_out", "ln_ffn", "w_up", "conv_w", "conv_b",
                "w_down", "ln_final")
BIG_NAMES = ("w_in", "w_out", "w_up", "w_down")
REPLICATED_NAMES = ("ln_attn", "sink_b", "rpb_c", "mix_gain", "ln_ffn", "conv_b", "ln_final")
PACK_TILE = 8 * LANES


def _pack(arrays, row_multiple):
    pieces = []
    for a in arrays:
        flat = a.reshape(-1)
        pieces.append(jnp.pad(flat, (0, (-flat.shape[0]) % PACK_TILE)))
    flat = jnp.concatenate(pieces)
    flat = jnp.pad(flat, (0, (-flat.shape[0]) % (row_multiple * LANES)))
    return flat.reshape(-1, LANES)


def _unpack(packed, shapes):
    flat = packed.reshape(-1)
    out, off = [], 0
    for shape in shapes:
        size = math.prod(shape)
        out.append(flat[off:off + size].reshape(shape))
        off += size + (-size) % PACK_TILE
    return out


def kernel(x, ln_attn, w_in, sink_b, rpb_c, mix_gain, w_out, ln_ffn, w_up, conv_w, conv_b, w_down, ln_final, loss_target, m_ln_attn, m_w_in, m_sink_b, m_rpb_c, m_mix_gain, m_w_out, m_ln_ffn, m_w_up, m_conv_w, m_conv_b, m_w_down, m_ln_final, v_ln_attn, v_w_in, v_sink_b, v_rpb_c, v_mix_gain, v_w_out, v_ln_ffn, v_w_up, v_conv_w, v_conv_b, v_w_down, v_ln_final):
    w = dict(ln_attn=ln_attn, w_in=w_in, sink_b=sink_b, rpb_c=rpb_c, mix_gain=mix_gain, w_out=w_out, ln_ffn=ln_ffn,
             w_up=w_up, conv_w=conv_w, conv_b=conv_b, w_down=w_down, ln_final=ln_final)
    m = dict(ln_attn=m_ln_attn, w_in=m_w_in, sink_b=m_sink_b, rpb_c=m_rpb_c, mix_gain=m_mix_gain, w_out=m_w_out,
             ln_ffn=m_ln_ffn, w_up=m_w_up, conv_w=m_conv_w, conv_b=m_conv_b, w_down=m_w_down, ln_final=m_ln_final)
    v = dict(ln_attn=v_ln_attn, w_in=v_w_in, sink_b=v_sink_b, rpb_c=v_rpb_c, mix_gain=v_mix_gain, w_out=v_w_out,
             ln_ffn=v_ln_ffn, w_up=v_w_up, conv_w=v_conv_w, conv_b=v_conv_b, w_down=v_w_down, ln_final=v_ln_final)
    shard = _shard_index(lax.axis_index("x"), lax.axis_index("y"))
    up_cols = w_up.shape[2]

    g_in, g_out, g_up, g_down, g_conv = _gather_weights(
        [_own_slot(w[k], shard, BF16, "own_" + k) for k in BIG_NAMES] + [_own_slot(conv_w, shard, F32, "own_conv_w")])
    big = {"w_in": g_in, "w_out": g_out.reshape(DEPTH, 1, D_MODEL, D_MODEL), "w_up": g_up,
           "w_down": g_down.reshape(DEPTH, 1, D_FF, D_MODEL)}
    cos, sin = _rope_tables(SEQ)
    tabs = {"cos": cos, "sin": sin, "bias_a": _bias_a(), "bias_b": _bias_b()}
    layers = []
    for l in range(DEPTH):
        conv_w_l = g_conv[l].reshape(2, N_SHARDS // 2, 3, up_cols).transpose(0, 2, 1, 3).reshape(2, 3, D_FF)
        layers.append({"ln_attn": ln_attn[l][None], "sink_b": sink_b[l], "bias_c": _bias_c(rpb_c[l]),
                       "mix_gain": mix_gain[l][None], "ln_ffn": ln_ffn[l][None], "conv_w": conv_w_l,
                       "conv_b": conv_b[l].reshape(2, 1, D_FF)})

    act = x[0]
    saved = []
    for l in range(DEPTH):
        act, keep = _layer_fwd(act, layers[l], big, tabs, l)
        saved.append(keep)
    loss_part, dx, dx_b, d_ln_final = _loss_head(act, ln_final[None], loss_target[0], "loss_head")
    loss = lax.psum(loss_part[0, 0], ("x", "y", "c"))
    gbufs = {}
    small = [None] * DEPTH
    for l in reversed(range(DEPTH)):
        dx, dx_b, gbufs, small[l] = _layer_bwd(dx, dx_b, saved[l], layers[l], big, tabs, l, gbufs)

    partial = [gbufs[k] for k in BIG_NAMES]
    theirs = _pair_exchange(partial)
    pair = [_pair_sum(own, other, "pair_sum_" + k) for k, own, other in zip(BIG_NAMES, partial, theirs)]
    landed = _chip_exchange(pair)
    mine = [_chip_sum(own, got, "chip_sum_" + k) for k, own, got in zip(BIG_NAMES, pair, landed)]
    grads = dict(zip(BIG_NAMES, _pair_gather(mine)))

    stacked = {k: jnp.stack([small[l][k] for l in range(DEPTH)]) for k in small[0]}
    part = {"ln_attn": stacked["ln_attn"][:, 0], "sink_b": stacked["sink_b"], "rpb_c": stacked["rpb_c"],
            "mix_gain": stacked["mix_gain"][:, 0], "ln_ffn": stacked["ln_ffn"][:, 0],
            "conv_b": stacked["conv_b"].reshape(DEPTH, 2 * D_FF), "ln_final": d_ln_final[0],
            "conv_w": stacked["conv_w"].transpose(0, 2, 1, 3).reshape(DEPTH, 3, 2 * D_FF)}
    names = REPLICATED_NAMES + ("conv_w",)
    total = _sum_slots(_all_gather_small(_pack([part[k] for k in names], 256)), "sum_small")
    for k, g in zip(names, _unpack(total, [part[k].shape for k in names])):
        grads[k] = g
    grads["conv_w"] = lax.dynamic_slice_in_dim(grads["conv_w"], shard * up_cols, up_cols, axis=2)

    delta, new_m, new_v = {}, {}, {}
    for k in BIG_NAMES + ("conv_w",):
        shape = w[k].shape
        flat = (shape[0] * shape[1], shape[2])
        res = _adamw(w[k].reshape(flat), grads[k].reshape(flat), m[k].reshape(flat), v[k].reshape(flat), "adamw_" + k)
        delta[k], new_m[k], new_v[k] = (r.reshape(shape) for r in res)
    shapes = [w[k].shape for k in REPLICATED_NAMES]
    packed = [_pack([d[k] for k in REPLICATED_NAMES], 128) for d in (w, grads, m, v)]
    for d, res in zip((delta, new_m, new_v), _adamw(*packed, "adamw_small")):
        for k, r in zip(REPLICATED_NAMES, _unpack(res, shapes)):
            d[k] = r

    return (loss, dx[None], *[grads[k] for k in WEIGHT_NAMES], *[delta[k] for k in WEIGHT_NAMES],
            *[new_m[k] for k in WEIGHT_NAMES], *[new_v[k] for k in WEIGHT_NAMES])
```

```python
import functools
import math

import jax
import jax.numpy as jnp
from jax import lax
from jax.experimental import pallas as pl
from jax.experimental.pallas import tpu as pltpu

F32 = jnp.float32
BF16 = jnp.bfloat16
MESH = pl.DeviceIdType.MESH

D_MODEL = 2048
SEQ = 2048
DEPTH = 2
HEAD_DIM = 64
N_HEADS_A = 12
N_HEADS_B = 10
N_KV_B = 2
N_HEADS_C = 10
WINDOW_B = 128
GRID_W = 64
NA_ROWS = 8
NA_COLS = 16
WIDTH_A = N_HEADS_A * HEAD_DIM
WIDTH_B = N_HEADS_B * HEAD_DIM
WIDTH_C = N_HEADS_C * HEAD_DIM
IN_COLS = 5120
D_FF = 5632
ROPE_THETA = 10000.0
EPS = 1e-6
NEG_INF = -1e30
N_SHARDS = 4

ADAM_LR = 0.001
ADAM_B1 = 0.9
ADAM_B2 = 0.999
ADAM_EPS = 1e-08
ADAM_WD = 0.01
ADAM_STEP = 10

LANES = 128
QB = 256
NQB = SEQ // QB
ROWS = 256
MIB = 2 ** 20

A_BLK = (0, 6, 12)
B_BLK = (18, 23, 24)
C_BLK = (25, 30, 35)
ROPE_BLKS = tuple(range(0, 12)) + tuple(range(18, 24))
QSCALE_BLKS = tuple(range(0, 6)) + tuple(range(18, 23)) + tuple(range(25, 30))
N_PBLK = IN_COLS // LANES


def _params(sem, vmem_mib):
    return pltpu.CompilerParams(dimension_semantics=sem, vmem_limit_bytes=vmem_mib * MIB)


def _weight_spec(w, cols, t_in, t_out, transposed):
    s, r, c = w.shape
    if cols:
        per = c // t_out
        k_dim, n = r, s * c
        if transposed:
            index = lambda j, rr: (rr // per, j, rr % per)
        else:
            index = lambda j, kk: (j // per, kk, j % per)
    else:
        per = r // t_in
        k_dim, n = s * r, c
        if transposed:
            index = lambda j, rr: (j // per, j % per, rr)
        else:
            index = lambda j, kk: (kk // per, kk % per, j)
    return pl.BlockSpec((None, t_in, t_out), index), k_dim, n


def _mm_nn(a, w, *, cols, tn, tk, out_dtype, name, residual=None, out_split=1):
    m, k_dim = a.shape
    w_spec, k_w, n = _weight_spec(w, cols, tk, tn, False)
    assert k_w == k_dim
    nj, nk = n // tn, k_dim // tk
    in_specs = [pl.BlockSpec((m, tk), lambda j, k: (0, k)), w_spec]
    args = [a, w]
    if residual is not None:
        in_specs.append(pl.BlockSpec((m, tn), lambda j, k: (0, j)))
        args.append(residual)
    if out_split > 1:
        per_o = n // out_split // tn
        out_spec = pl.BlockSpec((None, m, tn), lambda j, k: (j // per_o, 0, j % per_o))
        out_shape = jax.ShapeDtypeStruct((out_split, m, n // out_split), out_dtype)
    else:
        out_spec = pl.BlockSpec((m, tn), lambda j, k: (0, j))
        out_shape = jax.ShapeDtypeStruct((m, n), out_dtype)

    def body(*refs):
        a_ref, w_ref = refs[0], refs[1]
        r_ref = refs[2] if residual is not None else None
        o_ref = refs[3] if residual is not None else refs[2]

        def finish(val):
            if r_ref is not None:
                val = r_ref[...] + val
            o_ref[...] = val.astype(o_ref.dtype)

        part = jnp.dot(a_ref[...], w_ref[...], preferred_element_type=F32)
        if nk == 1:
            finish(part)
        else:
            acc = refs[-1]
            kk = pl.program_id(1)

            @pl.when(kk == 0)
            def _():
                acc[...] = part

            @pl.when(kk > 0)
            def _():
                acc[...] += part

            @pl.when(kk == nk - 1)
            def _():
                finish(acc[...])

    return pl.pallas_call(
        body, name=name, grid=(nj, nk), in_specs=in_specs, out_specs=out_spec, out_shape=out_shape,
        scratch_shapes=[pltpu.VMEM((m, tn), F32)] if nk > 1 else [],
        compiler_params=_params(("arbitrary", "arbitrary"), 56),
    )(*args)


ANY_SPEC = pl.BlockSpec(memory_space=pl.ANY)


def _mm_nt(dy, w, *, cols, to, tr, out_dtype, name, after=()):
    if dy.ndim == 3:
        m = dy.shape[1]
        n = dy.shape[0] * dy.shape[2]
        per_d = dy.shape[2] // tr
        dy_spec = pl.BlockSpec((None, m, tr), lambda j, r: (r // per_d, 0, r % per_d))
    else:
        m, n = dy.shape
        dy_spec = pl.BlockSpec((m, tr), lambda j, r: (0, r))
    w_spec, k_dim, n_w = _weight_spec(w, cols, to, tr, True)
    assert n_w == n
    nj, nr = k_dim // to, n // tr

    n_after = len(after)

    def body(dy_ref, w_ref, *rest):
        o_ref = rest[n_after]
        part = lax.dot_general(dy_ref[...], w_ref[...], (((1,), (1,)), ((), ())), preferred_element_type=F32)
        if nr == 1:
            o_ref[...] = part.astype(o_ref.dtype)
        else:
            acc = rest[n_after + 1]
            rr = pl.program_id(1)

            @pl.when(rr == 0)
            def _():
                acc[...] = part

            @pl.when(rr > 0)
            def _():
                acc[...] += part

            @pl.when(rr == nr - 1)
            def _():
                o_ref[...] = acc[...].astype(o_ref.dtype)

    return pl.pallas_call(
        body, name=name, grid=(nj, nr), in_specs=[dy_spec, w_spec] + [ANY_SPEC] * n_after,
        out_specs=pl.BlockSpec((m, to), lambda j, r: (0, j)),
        out_shape=jax.ShapeDtypeStruct((m, k_dim), out_dtype),
        scratch_shapes=[pltpu.VMEM((m, to), F32)] if nr > 1 else [],
        compiler_params=_params(("arbitrary", "arbitrary"), 56),
    )(dy, w, *after)


def _mm_tn(x, dy, *, tk, tn, shards, name):
    m, k_dim = x.shape
    if dy.ndim == 3:
        n = dy.shape[0] * dy.shape[2]
        per_d = dy.shape[2] // tn
        dy_spec = pl.BlockSpec((None, m, tn), lambda i, j: (j // per_d, 0, j % per_d))
    else:
        n = dy.shape[1]
        dy_spec = pl.BlockSpec((m, tn), lambda i, j: (0, j))
    if shards > 0:
        per = n // shards // tn
        out_shape = jax.ShapeDtypeStruct((shards, k_dim, n // shards), BF16)
        out_spec = pl.BlockSpec((None, tk, tn), lambda i, j: (j // per, i, j % per))
    else:
        s = -shards
        per = k_dim // s // tk
        out_shape = jax.ShapeDtypeStruct((s, k_dim // s, n), BF16)
        out_spec = pl.BlockSpec((None, tk, tn), lambda i, j: (i // per, i % per, j))

    def body(x_ref, dy_ref, o_ref):
        o_ref[...] = lax.dot_general(x_ref[...], dy_ref[...], (((0,), (0,)), ((), ())),
                                     preferred_element_type=F32).astype(BF16)

    return pl.pallas_call(
        body, name=name, grid=(k_dim // tk, n // tn),
        in_specs=[pl.BlockSpec((m, tk), lambda i, j: (0, i)), dy_spec], out_specs=out_spec, out_shape=out_shape,
        compiler_params=_params(("arbitrary", "arbitrary"), 56),
    )(x, dy)


def _row_spec(width, rows=ROWS):
    return pl.BlockSpec((rows, width), lambda i: (i, 0))


def _vec_spec(width):
    return pl.BlockSpec((1, width), lambda i: (0, 0))


def _rms_stats(x):
    r = lax.rsqrt(jnp.mean(x * x, axis=-1, keepdims=True) + EPS)
    return r, x * r


def _rmsnorm_fwd(x, gain, name):
    t, d = x.shape

    def body(x_ref, g_ref, o_ref):
        _, n = _rms_stats(x_ref[...])
        o_ref[...] = (n * g_ref[...]).astype(BF16)

    return pl.pallas_call(
        body, name=name, grid=(t // ROWS,), in_specs=[_row_spec(d), _vec_spec(d)], out_specs=_row_spec(d),
        out_shape=jax.ShapeDtypeStruct((t, d), BF16), compiler_params=_params(("arbitrary",), 32),
    )(x, gain)


def _rmsnorm_bwd(x, gain, dh, dres, name, after=()):
    t, d = x.shape
    n_after = len(after)

    def body(x_ref, g_ref, dh_ref, dres_ref, *rest):
        dx_ref, dxb_ref, dg_ref = rest[n_after:]
        r, n = _rms_stats(x_ref[...])
        dh_v = dh_ref[...]
        dn = dh_v * g_ref[...]
        dx = dres_ref[...] + r * (dn - n * jnp.mean(dn * n, axis=-1, keepdims=True))
        dx_ref[...] = dx
        dxb_ref[...] = dx.astype(BF16)
        part = jnp.sum(dh_v * n, axis=0, keepdims=True)

        @pl.when(pl.program_id(0) == 0)
        def _():
            dg_ref[...] = part

        @pl.when(pl.program_id(0) > 0)
        def _():
            dg_ref[...] += part

    return pl.pallas_call(
        body, name=name, grid=(t // ROWS,),
        in_specs=[_row_spec(d), _vec_spec(d), _row_spec(d), _row_spec(d)] + [ANY_SPEC] * n_after,
        out_specs=[_row_spec(d), _row_spec(d), _vec_spec(d)],
        out_shape=[jax.ShapeDtypeStruct((t, d), F32), jax.ShapeDtypeStruct((t, d), BF16),
                   jax.ShapeDtypeStruct((1, d), F32)],
        compiler_params=_params(("arbitrary",), 40),
    )(x, gain, dh, dres, *after)


def _loss_head(x, gain, target, name):
    t, d = x.shape

    def body(x_ref, g_ref, t_ref, loss_ref, dx_ref, dxb_ref, dg_ref):
        r, n = _rms_stats(x_ref[...])
        g = g_ref[...]
        err = n * g - t_ref[...]
        dy = err * (1.0 / d)
        dn = dy * g
        dx = r * (dn - n * jnp.mean(dn * n, axis=-1, keepdims=True))
        dx_ref[...] = dx
        dxb_ref[...] = dx.astype(BF16)
        part = jnp.sum(dy * n, axis=0, keepdims=True)
        lpart = jnp.zeros((8, LANES), F32) + 0.5 * jnp.sum(jnp.mean(err * err, axis=-1, keepdims=True))

        @pl.when(pl.program_id(0) == 0)
        def _():
            dg_ref[...] = part
            loss_ref[...] = lpart

        @pl.when(pl.program_id(0) > 0)
        def _():
            dg_ref[...] += part
            loss_ref[...] += lpart

    return pl.pallas_call(
        body, name=name, grid=(t // ROWS,),
        in_specs=[_row_spec(d), _vec_spec(d), _row_spec(d)],
        out_specs=[pl.BlockSpec((8, LANES), lambda i: (0, 0)), _row_spec(d), _row_spec(d), _vec_spec(d)],
        out_shape=[jax.ShapeDtypeStruct((8, LANES), F32), jax.ShapeDtypeStruct((t, d), F32),
                   jax.ShapeDtypeStruct((t, d), BF16), jax.ShapeDtypeStruct((1, d), F32)],
        compiler_params=_params(("arbitrary",), 40),
    )(x, gain, target)


def _swap_halves(x):
    lane = lax.broadcasted_iota(jnp.int32, x.shape, 1)
    return jnp.where((lane % HEAD_DIM) < HEAD_DIM // 2, pltpu.roll(x, LANES - HEAD_DIM // 2, 1),
                     pltpu.roll(x, HEAD_DIM // 2, 1))


def _rope_tables(t):
    inv_freq = ROPE_THETA ** (-jnp.arange(0, HEAD_DIM, 2, dtype=F32) / HEAD_DIM)
    ang = jnp.arange(t, dtype=F32)[:, None] * inv_freq[None, :]
    cos = jnp.tile(jnp.cos(ang), (1, LANES // (HEAD_DIM // 2)))
    sin = jnp.tile(jnp.sin(ang), (1, LANES // (HEAD_DIM // 2)))
    lane = jnp.arange(LANES)[None, :]
    return cos, jnp.where((lane % HEAD_DIM) < HEAD_DIM // 2, -sin, sin)


def _rope_fwd(proj, cos, sin, name):
    t = proj.shape[0]
    scale = HEAD_DIM ** -0.5

    def body(p_ref, c_ref, s_ref, o_ref):
        cos_v, sin_v = c_ref[...], s_ref[...]
        for b in range(N_PBLK):
            cols = slice(b * LANES, (b + 1) * LANES)
            v = p_ref[:, cols]
            if b in ROPE_BLKS:
                v = v * cos_v + _swap_halves(v) * sin_v
            if b in QSCALE_BLKS:
                v = v * scale
            o_ref[:, cols] = v.astype(BF16)

    return pl.pallas_call(
        body, name=name, grid=(t // ROWS,),
        in_specs=[_row_spec(IN_COLS), _row_spec(LANES), _row_spec(LANES)], out_specs=_row_spec(IN_COLS),
        out_shape=jax.ShapeDtypeStruct((t, IN_COLS), BF16), compiler_params=_params(("arbitrary",), 40),
    )(proj, cos, sin)


def _rope_bwd(grads, cos, sin, name):
    t = grads[0].shape[0]
    scale = HEAD_DIM ** -0.5
    group = N_HEADS_B // N_KV_B

    def body(*refs):
        c_ref, s_ref, o_ref = refs[9], refs[10], refs[11]
        cos_v, sin_v = c_ref[...], s_ref[...]

        def kv_sum(ref):
            parts = []
            for g in range(N_KV_B):
                acc = ref[:, g * group * HEAD_DIM:(g * group + 1) * HEAD_DIM]
                for h in range(g * group + 1, (g + 1) * group):
                    acc = acc + ref[:, h * HEAD_DIM:(h + 1) * HEAD_DIM]
                parts.append(acc)
            return jnp.concatenate(parts, axis=1)

        def emit(b, v):
            if b in ROPE_BLKS:
                v = v * cos_v - _swap_halves(v) * sin_v
            if b in QSCALE_BLKS:
                v = v * scale
            o_ref[:, b * LANES:(b + 1) * LANES] = v.astype(BF16)

        starts = (A_BLK[0], A_BLK[1], A_BLK[2], B_BLK[0], None, None, C_BLK[0], C_BLK[1], C_BLK[2])
        for idx, start in enumerate(starts):
            if start is None:
                continue
            for j in range(refs[idx].shape[1] // LANES):
                emit(start + j, refs[idx][:, j * LANES:(j + 1) * LANES])
        emit(B_BLK[1], kv_sum(refs[4]))
        emit(B_BLK[2], kv_sum(refs[5]))

    return pl.pallas_call(
        body, name=name, grid=(t // ROWS,),
        in_specs=[_row_spec(g.shape[1]) for g in grads] + [_row_spec(LANES), _row_spec(LANES)],
        out_specs=_row_spec(IN_COLS),
        out_shape=jax.ShapeDtypeStruct((t, IN_COLS), BF16), compiler_params=_params(("arbitrary",), 40),
    )(*grads, cos, sin)


GROUP_COLS = ((0, WIDTH_A), (WIDTH_A, WIDTH_A + WIDTH_B), (WIDTH_A + WIDTH_B, D_MODEL))


def _mix_fwd(oa, ob, oc, gain, name):
    t = oa.shape[0]

    def body(a_ref, b_ref, c_ref, g_ref, o_ref):
        for ref, (lo, hi) in zip((a_ref, b_ref, c_ref), GROUP_COLS):
            _, n = _rms_stats(ref[...])
            o_ref[:, lo:hi] = (n * g_ref[:, lo:hi]).astype(BF16)

    return pl.pallas_call(
        body, name=name, grid=(t // ROWS,),
        in_specs=[_row_spec(WIDTH_A), _row_spec(WIDTH_B), _row_spec(WIDTH_C), _vec_spec(D_MODEL)],
        out_specs=_row_spec(D_MODEL),
        out_shape=jax.ShapeDtypeStruct((t, D_MODEL), BF16), compiler_params=_params(("arbitrary",), 32),
    )(oa, ob, oc, gain)


def _mix_bwd(oa, ob, oc, gain, dmixed, name):
    t = oa.shape[0]

    def body(a_ref, b_ref, c_ref, g_ref, dm_ref, da_ref, db_ref, dc_ref, dg_ref):
        first = pl.program_id(0) == 0
        for ref, dref, (lo, hi) in zip((a_ref, b_ref, c_ref), (da_ref, db_ref, dc_ref), GROUP_COLS):
            r, n = _rms_stats(ref[...])
            dm = dm_ref[:, lo:hi]
            dn = dm * g_ref[:, lo:hi]
            dref[...] = r * (dn - n * jnp.mean(dn * n, axis=-1, keepdims=True))
            part = jnp.sum(dm * n, axis=0, keepdims=True)

            @pl.when(first)
            def _():
                dg_ref[:, lo:hi] = part

            @pl.when(jnp.logical_not(first))
            def _():
                dg_ref[:, lo:hi] += part

    return pl.pallas_call(
        body, name=name, grid=(t // ROWS,),
        in_specs=[_row_spec(WIDTH_A), _row_spec(WIDTH_B), _row_spec(WIDTH_C), _vec_spec(D_MODEL), _row_spec(D_MODEL)],
        out_specs=[_row_spec(WIDTH_A), _row_spec(WIDTH_B), _row_spec(WIDTH_C), _vec_spec(D_MODEL)],
        out_shape=[jax.ShapeDtypeStruct((t, WIDTH_A), F32), jax.ShapeDtypeStruct((t, WIDTH_B), F32),
                   jax.ShapeDtypeStruct((t, WIDTH_C), F32), jax.ShapeDtypeStruct((1, D_MODEL), F32)],
        compiler_params=_params(("arbitrary",), 40),
    )(oa, ob, oc, gain, dmixed)


FF_COLS = 256


def _shift_rows(x, by):
    t = x.shape[0]
    row = lax.broadcasted_iota(jnp.int32, x.shape, 0)
    rolled = pltpu.roll(x, by % t, 0)
    return jnp.where(row == (0 if by == 1 else t - 1), 0.0, rolled)


def _conv(u0, w_ref, b_ref, h):
    return (_shift_rows(u0, 1) * w_ref[h, 0:1, :] + u0 * w_ref[h, 1:2, :]) + _shift_rows(u0, -1) * w_ref[h, 2:3, :] \
        + b_ref[h]


def _ff_specs(t):
    u_spec = pl.BlockSpec((2, t, FF_COLS), lambda j: (0, 0, j))
    w_spec = pl.BlockSpec((2, 3, FF_COLS), lambda j: (0, 0, j))
    b_spec = pl.BlockSpec((2, 1, FF_COLS), lambda j: (0, 0, j))
    return u_spec, w_spec, b_spec


def _convgate_fwd(u0, conv_w, conv_b, name):
    t = u0.shape[1]
    u_spec, w_spec, b_spec = _ff_specs(t)

    def body(u_ref, w_ref, b_ref, o_ref):
        gate = _conv(u_ref[0], w_ref, b_ref, 0)
        val = _conv(u_ref[1], w_ref, b_ref, 1)
        o_ref[...] = (gate * jax.nn.sigmoid(gate) * val).astype(BF16)

    return pl.pallas_call(
        body, name=name, grid=(D_FF // FF_COLS,), in_specs=[u_spec, w_spec, b_spec],
        out_specs=pl.BlockSpec((t, FF_COLS), lambda j: (0, j)),
        out_shape=jax.ShapeDtypeStruct((t, D_FF), BF16), compiler_params=_params(("arbitrary",), 48),
    )(u0, conv_w, conv_b)


def _convgate_bwd(u0, conv_w, conv_b, d_act, name):
    t = u0.shape[1]
    u_spec, w_spec, b_spec = _ff_specs(t)

    def body(u_ref, w_ref, b_ref, da_ref, du_ref, dw_ref, db_ref):
        gate = _conv(u_ref[0], w_ref, b_ref, 0)
        val = _conv(u_ref[1], w_ref, b_ref, 1)
        sig = jax.nn.sigmoid(gate)
        da = da_ref[...]
        d_half = (da * val * (sig * (1.0 + gate * (1.0 - sig))), da * (gate * sig))
        for h in range(2):
            du = d_half[h]
            u0_h = u_ref[h]
            db_ref[h] = jnp.sum(du, axis=0, keepdims=True)
            dw_ref[h, 0:1, :] = jnp.sum(du * _shift_rows(u0_h, 1), axis=0, keepdims=True)
            dw_ref[h, 1:2, :] = jnp.sum(du * u0_h, axis=0, keepdims=True)
            dw_ref[h, 2:3, :] = jnp.sum(du * _shift_rows(u0_h, -1), axis=0, keepdims=True)
            du_ref[h] = ((_shift_rows(du, -1) * w_ref[h, 0:1, :] + du * w_ref[h, 1:2, :])
                         + _shift_rows(du, 1) * w_ref[h, 2:3, :]).astype(BF16)

    return pl.pallas_call(
        body, name=name, grid=(D_FF // FF_COLS,),
        in_specs=[u_spec, w_spec, b_spec, pl.BlockSpec((t, FF_COLS), lambda j: (0, j))],
        out_specs=[u_spec, w_spec, b_spec],
        out_shape=[jax.ShapeDtypeStruct((2, t, D_FF), BF16), jax.ShapeDtypeStruct((2, 3, D_FF), F32),
                   jax.ShapeDtypeStruct((2, 1, D_FF), F32)],
        compiler_params=_params(("arbitrary",), 56),
    )(u0, conv_w, conv_b, d_act)


class _Group:
    def __init__(self, heads, blks, full, gqa, bias_per_head):
        self.heads = heads
        self.pairs = heads // 2
        self.q_blk, self.k_blk, self.v_blk = blks
        self.full = full
        self.gqa = gqa
        self.bias_per_head = bias_per_head
        self.n_win = NQB if full else 3
        self.width = heads * HEAD_DIM


GROUP_A = _Group(N_HEADS_A, A_BLK, True, False, False)
GROUP_B = _Group(N_HEADS_B, B_BLK, False, True, False)
GROUP_C = _Group(N_HEADS_C, C_BLK, False, False, True)


def _win_start(i):
    return jnp.clip(i - 1, 0, NQB - 3)


def _win_variant(i):
    return jnp.minimum(i, 1) + (i == NQB - 1).astype(jnp.int32)


def _attn_in_specs(grp, t):
    q_spec = pl.BlockSpec((QB, LANES), lambda p, i: (i, grp.q_blk + p))

    def col(blk):
        return (lambda p: blk) if grp.gqa else (lambda p: blk + p)

    def kv_specs(blk):
        c = col(blk)
        if grp.full:
            return [pl.BlockSpec((t, LANES), lambda p, i: (0, c(p)))]
        return [pl.BlockSpec((QB, LANES), functools.partial(lambda p, i, w: (_win_start(i) + w, c(p)), w=w))
                for w in range(3)]

    nwk = grp.n_win * QB
    if grp.bias_per_head:
        bias_spec = pl.BlockSpec((2, None, QB, nwk), lambda p, i: (p, _win_variant(i), 0, 0))
    elif grp.full:
        bias_spec = pl.BlockSpec((1, None, QB, nwk), lambda p, i: (0, i, 0, 0))
    else:
        bias_spec = pl.BlockSpec((1, None, QB, nwk), lambda p, i: (0, _win_variant(i), 0, 0))
    sink_spec = pl.BlockSpec((1, LANES), lambda p, i: (0, p))
    return q_spec, kv_specs(grp.k_blk), kv_specs(grp.v_blk), bias_spec, sink_spec


def _head_kv(grp, whole, e, p):
    lo, hi = whole[:, :HEAD_DIM], whole[:, HEAD_DIM:]
    if grp.gqa:
        return jnp.where(2 * p + e >= N_HEADS_B // N_KV_B, hi, lo)
    return hi if e else lo


def _softmax_parts(q, k, bias, sink):
    s = lax.dot_general(q, k, (((1,), (1,)), ((), ())), preferred_element_type=F32) + bias
    m = jnp.maximum(jnp.max(s, axis=-1, keepdims=True), sink)
    pe = jnp.exp(s - m)
    denom = jnp.sum(pe, axis=-1, keepdims=True) + jnp.exp(sink - m)
    return pe, m, 1.0 / denom


def _attn_fwd(grp, proj, bias, sink, name):
    t = proj.shape[0]
    q_spec, k_specs, v_specs, bias_spec, sink_spec = _attn_in_specs(grp, t)
    nkv = len(k_specs)

    def body(*refs):
        q_ref = refs[0]
        k_refs, v_refs = refs[1:1 + nkv], refs[1 + nkv:1 + 2 * nkv]
        bias_ref, sink_ref, o_ref = refs[1 + 2 * nkv:4 + 2 * nkv]
        p = pl.program_id(0)
        k_all = jnp.concatenate([r[...] for r in k_refs], axis=0)
        v_all = jnp.concatenate([r[...] for r in v_refs], axis=0)
        outs = []
        for e in range(2):
            q = q_ref[:, e * HEAD_DIM:(e + 1) * HEAD_DIM]
            k = _head_kv(grp, k_all, e, p)
            v = _head_kv(grp, v_all, e, p)
            snk = sink_ref[0:1, e * HEAD_DIM:e * HEAD_DIM + 1]
            pe, _, inv = _softmax_parts(q, k, bias_ref[e if grp.bias_per_head else 0], snk)
            outs.append(jnp.dot(pe.astype(BF16), v, preferred_element_type=F32) * inv)
        o_ref[...] = jnp.concatenate(outs, axis=1)

    return pl.pallas_call(
        body, name=name, grid=(grp.pairs, NQB),
        in_specs=[q_spec, *k_specs, *v_specs, bias_spec, sink_spec],
        out_specs=pl.BlockSpec((QB, LANES), lambda p, i: (i, p)),
        out_shape=jax.ShapeDtypeStruct((t, grp.width), F32),
        compiler_params=_params(("arbitrary", "arbitrary"), 48),
    )(proj, *([proj] * (2 * nkv)), bias, sink)


def _attn_bwd(grp, proj, bias, sink, out, d_out, name):
    t = proj.shape[0]
    q_spec, k_specs, v_specs, bias_spec, sink_spec = _attn_in_specs(grp, t)
    nkv = len(k_specs)
    n_off = 2 * NA_ROWS - 1
    rows_q = QB // GRID_W
    o_spec = pl.BlockSpec((QB, LANES), lambda p, i: (i, p))
    acc_spec = pl.BlockSpec((t, LANES), lambda p, i: (0, p))
    out_specs = [o_spec, acc_spec, acc_spec, pl.BlockSpec((None, 8, LANES), lambda p, i: (p, 0, 0))]
    out_shape = [jax.ShapeDtypeStruct((t, grp.width), F32)] * 3 + [jax.ShapeDtypeStruct((grp.pairs, 8, LANES), F32)]
    if grp.bias_per_head:
        out_specs.append(pl.BlockSpec((2, n_off, GRID_W, GRID_W), lambda p, i: (p, 0, 0, 0)))
        out_shape.append(jax.ShapeDtypeStruct((grp.heads, n_off, GRID_W, GRID_W), F32))

    def body(*refs):
        q_ref = refs[0]
        k_refs, v_refs = refs[1:1 + nkv], refs[1 + nkv:1 + 2 * nkv]
        bias_ref, sink_ref, o_ref, do_ref = refs[1 + 2 * nkv:5 + 2 * nkv]
        dq_ref, dk_ref, dv_ref, dsink_ref = refs[5 + 2 * nkv:9 + 2 * nkv]
        dbias_ref = refs[9 + 2 * nkv] if grp.bias_per_head else None
        p, i = pl.program_id(0), pl.program_id(1)

        @pl.when(i == 0)
        def _():
            dk_ref[...] = jnp.zeros_like(dk_ref)
            dv_ref[...] = jnp.zeros_like(dv_ref)
            dsink_ref[...] = jnp.zeros_like(dsink_ref)
            if dbias_ref is not None:
                dbias_ref[...] = jnp.zeros_like(dbias_ref)

        k_all = jnp.concatenate([r[...] for r in k_refs], axis=0)
        v_all = jnp.concatenate([r[...] for r in v_refs], axis=0)
        start = 0 if grp.full else _win_start(i)
        dqs, dks, dvs, dsinks = [], [], [], []
        for e in range(2):
            cols = slice(e * HEAD_DIM, (e + 1) * HEAD_DIM)
            q = q_ref[:, cols]
            k = _head_kv(grp, k_all, e, p)
            v = _head_kv(grp, v_all, e, p)
            snk = sink_ref[0:1, e * HEAD_DIM:e * HEAD_DIM + 1]
            pe, m, inv = _softmax_parts(q, k, bias_ref[e if grp.bias_per_head else 0], snk)
            prob = pe * inv
            do = do_ref[:, cols]
            do_b = do.astype(BF16)
            delta = jnp.sum(do * o_ref[:, cols], axis=-1, keepdims=True)
            dp = lax.dot_general(do_b, v, (((1,), (1,)), ((), ())), preferred_element_type=F32)
            ds = prob * (dp - delta)
            ds_b = ds.astype(BF16)
            dqs.append(jnp.dot(ds_b, k, preferred_element_type=F32))
            dks.append(lax.dot_general(ds_b, q, (((0,), (0,)), ((), ())), preferred_element_type=F32))
            dvs.append(lax.dot_general(prob.astype(BF16), do_b, (((0,), (0,)), ((), ())), preferred_element_type=F32))
            dsinks.append(-jnp.sum(jnp.exp(snk - m) * inv * delta, axis=0, keepdims=True))
            if dbias_ref is not None:
                shift = rows_q * (i - start)
                for rq in range(rows_q):
                    for rk in range(grp.n_win * rows_q):
                        off = jnp.clip(rk - rq + (NA_ROWS - 1) - shift, 0, n_off - 1)
                        dbias_ref[e, off] += ds[rq * GRID_W:(rq + 1) * GRID_W, rk * GRID_W:(rk + 1) * GRID_W]
        dq_ref[...] = jnp.concatenate(dqs, axis=1)
        rows = pl.ds(0, t) if grp.full else pl.ds(pl.multiple_of(start * QB, QB), grp.n_win * QB)
        dk_ref[rows, :] += jnp.concatenate(dks, axis=1)
        dv_ref[rows, :] += jnp.concatenate(dvs, axis=1)
        lane = lax.broadcasted_iota(jnp.int32, (8, LANES), 1)
        dsink_ref[...] += jnp.where(lane < HEAD_DIM, dsinks[0], dsinks[1])

    return pl.pallas_call(
        body, name=name, grid=(grp.pairs, NQB),
        in_specs=[q_spec, *k_specs, *v_specs, bias_spec, sink_spec, o_spec, o_spec],
        out_specs=out_specs, out_shape=out_shape,
        compiler_params=_params(("arbitrary", "arbitrary"), 56),
    )(proj, *([proj] * (2 * nkv)), bias, sink, out, d_out)


DILATED_CONFIGS = ((128, 1), (512, 4), (2048, 16))


def _bias_a():
    d = jnp.arange(SEQ)[None, :] - jnp.arange(SEQ)[:, None]
    mult = jnp.zeros((SEQ, SEQ), F32)
    for window, r in DILATED_CONFIGS:
        reach = (window // (2 * r)) * r
        mult = mult + ((d % r == 0) & (jnp.abs(d) <= reach)).astype(F32)
    return jnp.where(mult > 0, jnp.log(jnp.maximum(mult, 1.0)), NEG_INF).reshape(1, NQB, QB, SEQ)


def _bias_b():
    row = jnp.arange(QB)[None, :, None]
    col = jnp.arange(3 * QB)[None, None, :]
    var = jnp.arange(3)[:, None, None]
    d = col - (QB * var + row)
    return jnp.where(jnp.abs(d) <= WINDOW_B, 0.0, NEG_INF).astype(F32)[None]


def _offset_onehot():
    c = jnp.arange(GRID_W)[:, None, None]
    c2 = jnp.arange(GRID_W)[None, :, None]
    b = jnp.arange(LANES)[None, None, :]
    return (c2 - c + NA_COLS - 1 == b).astype(BF16).reshape(GRID_W * GRID_W, LANES)


def _split_dot(x, g):
    hi = x.astype(BF16)
    rest = x - hi.astype(F32)
    mid = rest.astype(BF16)
    lo = (rest - mid.astype(F32)).astype(BF16)
    return (jnp.dot(hi, g, preferred_element_type=F32) + jnp.dot(mid, g, preferred_element_type=F32)
            + jnp.dot(lo, g, preferred_element_type=F32))


def _table_mm(x, g, name):
    def body(x_ref, g_ref, o_ref):
        o_ref[...] = _split_dot(x_ref[...], g_ref[...])

    return pl.pallas_call(
        body, name=name, out_shape=jax.ShapeDtypeStruct((x.shape[0], g.shape[1]), F32),
        in_specs=[pl.BlockSpec(memory_space=pltpu.VMEM)] * 2, out_specs=pl.BlockSpec(memory_space=pltpu.VMEM),
        compiler_params=pltpu.CompilerParams(vmem_limit_bytes=32 * MIB),
    )(x, g)


N_OFF = 2 * NA_ROWS - 1
TABLE_ROWS = 152


def _bias_c(rpb):
    table = jnp.zeros((TABLE_ROWS, LANES), F32).at[:N_HEADS_C * N_OFF, :2 * NA_COLS - 1].set(
        rpb.reshape(N_HEADS_C * N_OFF, 2 * NA_COLS - 1))
    tiles = _table_mm(table, _offset_onehot().T, "rpb_tiles")[:N_HEADS_C * N_OFF]
    tiles = tiles.reshape(N_HEADS_C, N_OFF, GRID_W, GRID_W)
    c = jnp.arange(GRID_W)
    col_start = jnp.clip(c - NA_COLS // 2, 0, GRID_W - NA_COLS)
    col_ok = (c[None, :] >= col_start[:, None]) & (c[None, :] < col_start[:, None] + NA_COLS)
    tiles = jnp.where(col_ok, tiles, NEG_INF)
    masked = jnp.full((N_HEADS_C, GRID_W, GRID_W), NEG_INF, F32)
    rows_q = QB // GRID_W
    variants = []
    for var in range(3):
        q_rows = []
        for rq in range(rows_q):
            r_l = rows_q * var + rq
            first = min(max(r_l - NA_ROWS // 2, 0), 3 * rows_q - NA_ROWS)
            q_rows.append(jnp.concatenate(
                [tiles[:, rk - r_l + NA_ROWS - 1] if first <= rk < first + NA_ROWS else masked
                 for rk in range(3 * rows_q)], axis=-1))
        variants.append(jnp.concatenate(q_rows, axis=-2))
    return jnp.stack(variants, axis=1)


def _rpb_grad(d_tiles):
    flat = jnp.zeros((TABLE_ROWS, GRID_W * GRID_W), F32).at[:N_HEADS_C * N_OFF].set(
        d_tiles.reshape(N_HEADS_C * N_OFF, GRID_W * GRID_W))
    out = _table_mm(flat, _offset_onehot(), "rpb_grad")
    return out[:N_HEADS_C * N_OFF, :2 * NA_COLS - 1].reshape(N_HEADS_C, N_OFF, 2 * NA_COLS - 1)


def _sink_lanes(sink):
    return jnp.repeat(sink.astype(F32), HEAD_DIM)[None, :]


def _attention_fwd(proj_r, sink_b, bias_a, bias_b, bias_c):
    no_sink_a = jnp.full((1, WIDTH_A), NEG_INF, F32)
    no_sink_c = jnp.full((1, WIDTH_C), NEG_INF, F32)
    oa = _attn_fwd(GROUP_A, proj_r, bias_a, no_sink_a, "attn_a_fwd")
    ob = _attn_fwd(GROUP_B, proj_r, bias_b, _sink_lanes(sink_b), "attn_b_fwd")
    oc = _attn_fwd(GROUP_C, proj_r, bias_c, no_sink_c, "attn_c_fwd")
    return oa, ob, oc


def _attention_bwd(proj_r, sink_b, bias_a, bias_b, bias_c, outs, d_outs, cos, sin):
    no_sink_a = jnp.full((1, WIDTH_A), NEG_INF, F32)
    no_sink_c = jnp.full((1, WIDTH_C), NEG_INF, F32)
    dqa, dka, dva, _ = _attn_bwd(GROUP_A, proj_r, bias_a, no_sink_a, outs[0], d_outs[0], "attn_a_bwd")
    dqb, dkb, dvb, dsink = _attn_bwd(GROUP_B, proj_r, bias_b, _sink_lanes(sink_b), outs[1], d_outs[1], "attn_b_bwd")
    dqc, dkc, dvc, _, d_tiles = _attn_bwd(GROUP_C, proj_r, bias_c, no_sink_c, outs[2], d_outs[2], "attn_c_bwd")
    d_proj = _rope_bwd((dqa, dka, dva, dqb, dkb, dvb, dqc, dkc, dvc), cos, sin, "rope_bwd")
    d_sink = dsink[:, 0, :].reshape(GROUP_B.pairs, 2, HEAD_DIM)[:, :, 0].reshape(N_HEADS_B)
    return d_proj, d_sink, _rpb_grad(d_tiles)


def _adamw(w, g, m, v, name):
    r, c = w.shape
    rows = r
    for cand in (512, 256, 128, 64, 32, 16, 8):
        if r % cand == 0 and cand * c * 4 <= MIB:
            rows = cand
            break
    spec = pl.BlockSpec((rows, c), lambda i: (i, 0))

    def body(w_ref, g_ref, m_ref, v_ref, d_ref, mo_ref, vo_ref):
        grad = g_ref[...]
        m_new = ADAM_B1 * m_ref[...] + (1.0 - ADAM_B1) * grad
        v_new = ADAM_B2 * v_ref[...] + (1.0 - ADAM_B2) * jnp.square(grad)
        m_hat = m_new / (1.0 - ADAM_B1 ** ADAM_STEP)
        v_hat = v_new / (1.0 - ADAM_B2 ** ADAM_STEP)
        d_ref[...] = -ADAM_LR * (m_hat / (jnp.sqrt(v_hat) + ADAM_EPS) + ADAM_WD * w_ref[...])
        mo_ref[...] = m_new
        vo_ref[...] = v_new

    return pl.pallas_call(
        body, name=name, grid=(r // rows,), in_specs=[spec] * 4, out_specs=[spec] * 3,
        out_shape=[jax.ShapeDtypeStruct((r, c), F32)] * 3, compiler_params=_params(("arbitrary",), 32),
    )(w, g, m, v)


def _layer_fwd(x0, p, weight, tabs):
    h1 = _rmsnorm_fwd(x0, p["ln_attn"], "ln_attn_fwd")
    proj = _mm_nn(h1, weight("w_in", h1), cols=True, tn=256, tk=D_MODEL, out_dtype=F32, name="mm_in")
    proj_r = _rope_fwd(proj, tabs["cos"], tabs["sin"], "rope_fwd")
    outs = _attention_fwd(proj_r, p["sink_b"], tabs["bias_a"], tabs["bias_b"], p["bias_c"])
    mixed = _mix_fwd(*outs, p["mix_gain"], "mix_fwd")
    x1 = _mm_nn(mixed, weight("w_out", mixed), cols=False, tn=256, tk=D_MODEL, out_dtype=F32, name="mm_out",
                residual=x0)
    h2 = _rmsnorm_fwd(x1, p["ln_ffn"], "ln_ffn_fwd")
    u0 = _mm_nn(h2, weight("w_up", h2), cols=True, tn=256, tk=D_MODEL, out_dtype=F32, name="mm_up", out_split=2)
    act = _convgate_fwd(u0, p["conv_w"], p["conv_b"], "convgate_fwd")
    x2 = _mm_nn(act, weight("w_down", act), cols=False, tn=256, tk=D_FF // 2, out_dtype=F32, name="mm_down",
                residual=x1)
    return x2, (x0, h1, proj_r, outs, mixed, x1, h2, u0, act)


def _layer_bwd(dx2, dx2_b, saved, p, big, tabs, send, after):
    x0, h1, proj_r, outs, mixed, x1, h2, u0, act = saved
    d_act = _mm_nt(dx2_b, big["w_down"], cols=False, to=512, tr=D_MODEL, out_dtype=F32, name="nt_down", after=after)
    g_down = _mm_tn(act, dx2_b, tk=D_FF // N_SHARDS, tn=512, shards=-N_SHARDS, name="tn_down")
    du0, d_conv_w, d_conv_b = _convgate_bwd(u0, p["conv_w"], p["conv_b"], d_act, "convgate_bwd")
    dh2 = _mm_nt(du0, big["w_up"], cols=True, to=512, tr=D_FF // 4, out_dtype=F32, name="nt_up")
    g_up = _mm_tn(h2, du0, tk=512, tn=D_FF // 4, shards=N_SHARDS, name="tn_up")
    token = send({"w_down": g_down, "w_up": g_up})
    dx1, dx1_b, d_ln_ffn = _rmsnorm_bwd(x1, p["ln_ffn"], dh2, dx2, "ln_ffn_bwd", after=[token])
    d_mixed = _mm_nt(dx1_b, big["w_out"], cols=False, to=512, tr=D_MODEL, out_dtype=F32, name="nt_out")
    g_out = _mm_tn(mixed, dx1_b, tk=D_MODEL // N_SHARDS, tn=512, shards=-N_SHARDS, name="tn_out")
    *d_outs, d_mix_gain = _mix_bwd(*outs, p["mix_gain"], d_mixed, "mix_bwd")
    d_proj, d_sink, d_rpb = _attention_bwd(proj_r, p["sink_b"], tabs["bias_a"], tabs["bias_b"], p["bias_c"], outs,
                                           d_outs, tabs["cos"], tabs["sin"])
    dh1 = _mm_nt(d_proj, big["w_in"], cols=True, to=512, tr=IN_COLS // N_SHARDS, out_dtype=F32, name="nt_in")
    g_in = _mm_tn(h1, d_proj, tk=512, tn=IN_COLS // N_SHARDS, shards=N_SHARDS, name="tn_in")
    dx0, dx0_b, d_ln_attn = _rmsnorm_bwd(x0, p["ln_attn"], dh1, dx1, "ln_attn_bwd")
    token = send({"w_out": g_out, "w_in": g_in})
    small = {"ln_attn": d_ln_attn, "sink_b": d_sink, "rpb_c": d_rpb, "mix_gain": d_mix_gain, "ln_ffn": d_ln_ffn,
             "conv_w": d_conv_w, "conv_b": d_conv_b}
    return dx0, dx0_b, small, token


HBM_SPEC = pl.BlockSpec(memory_space=pl.ANY)


def _place():
    x, y, c = lax.axis_index("x"), lax.axis_index("y"), lax.axis_index("c")
    chips = ((1 - x, y), (x, 1 - y), (1 - x, 1 - y))
    return x, y, c, chips


def _shard_index(px, py):
    return 2 * px + py


def _remote(src, dst, send_sem, recv_sem, to):
    return pltpu.make_async_remote_copy(src_ref=src, dst_ref=dst, send_sem=send_sem, recv_sem=recv_sem,
                                        device_id=to, device_id_type=MESH)


def _own_slot(w, layer, shard, name):
    _, r, c_dim = w.shape
    rows = r
    for cand in (512, 256, 128):
        if r % cand == 0 and cand * c_dim * 4 <= 2 * MIB:
            rows = cand
            break

    def body(s_ref, w_ref, o_ref):
        o_ref[...] = w_ref[...].astype(BF16)

    return pl.pallas_call(
        body, name=name,
        grid_spec=pltpu.PrefetchScalarGridSpec(
            num_scalar_prefetch=1, grid=(r // rows,),
            in_specs=[pl.BlockSpec((None, rows, c_dim), lambda i, s: (layer, i, 0))],
            out_specs=pl.BlockSpec((None, rows, c_dim), lambda i, s: (s[0], i, 0))),
        out_shape=jax.ShapeDtypeStruct((N_SHARDS, r, c_dim), BF16),
        compiler_params=_params(("arbitrary",), 32),
    )(shard.astype(jnp.int32).reshape(1), w)


def _gather_weights(bufs):
    n = len(bufs)

    def body(*refs):
        outs = refs[n:2 * n]
        send1, recv1, send2, recv2 = refs[2 * n:]
        x, y, c, chips = _place()
        me = _shard_index(x, y)
        sibling = (x, y, 1 - c)
        first = []
        for t in range(n):
            for j, (px, py) in enumerate(chips):
                mine = outs[t].at[c, me]
                cp = _remote(mine, mine, send1.at[t * 3 + j], recv1.at[t * 3 + j], (px, py, c))
                cp.start()
                first.append(cp)
        passed = []
        for t in range(n):
            for j, (px, py) in enumerate(chips):
                slot = outs[t].at[c, _shard_index(px, py)]
                _remote(slot, slot, send1.at[t * 3 + j], recv1.at[t * 3 + j], (px, py, c)).wait_recv()
                cp = _remote(slot, slot, send2.at[t * 3 + j], recv2.at[t * 3 + j], sibling)
                cp.start()
                passed.append(cp)
        for t in range(n):
            for j, (px, py) in enumerate(chips):
                slot = outs[t].at[1 - c, _shard_index(px, py)]
                _remote(slot, slot, send2.at[t * 3 + j], recv2.at[t * 3 + j], sibling).wait_recv()
        for cp in first + passed:
            cp.wait_send()

    return pl.pallas_call(
        body, name="gather_weights", in_specs=[HBM_SPEC] * n, out_specs=[HBM_SPEC] * n,
        out_shape=[jax.ShapeDtypeStruct(b.shape, b.dtype) for b in bufs],
        input_output_aliases={t: t for t in range(n)},
        scratch_shapes=[pltpu.SemaphoreType.DMA((n * 3,))] * 4,
    )(*bufs)


def _pair_exchange(bufs):
    n = len(bufs)

    def body(*refs):
        ins, outs = refs[:n], refs[n:2 * n]
        send, recv = refs[2 * n:]
        x, y, c, _ = _place()
        sibling = (x, y, 1 - c)
        cps = [_remote(ins[t].at[1 - c], outs[t], send.at[t], recv.at[t], sibling) for t in range(n)]
        for cp in cps:
            cp.start()
        for cp in cps:
            cp.wait()

    return pl.pallas_call(
        body, name="pair_exchange", in_specs=[HBM_SPEC] * n, out_specs=[HBM_SPEC] * n,
        out_shape=[jax.ShapeDtypeStruct(b.shape[1:], b.dtype) for b in bufs],
        scratch_shapes=[pltpu.SemaphoreType.DMA((n,))] * 2,
    )(*bufs)


def _pair_sum(own, other, name):
    _, s, r, c_dim = own.shape
    rows = min(r, LANES)
    per = r // rows
    layer = lax.axis_index("c").astype(jnp.int32).reshape(1)

    def body(layer_ref, a_ref, b_ref, o_ref):
        o_ref[...] = (a_ref[...] + b_ref[...]).astype(BF16)

    return pl.pallas_call(
        body, name=name,
        grid_spec=pltpu.PrefetchScalarGridSpec(
            num_scalar_prefetch=1, grid=(s * per,),
            in_specs=[pl.BlockSpec((None, None, rows, c_dim), lambda i, lay: (lay[0], i // per, i % per, 0)),
                      pl.BlockSpec((None, rows, c_dim), lambda i, lay: (i // per, i % per, 0))],
            out_specs=pl.BlockSpec((None, rows, c_dim), lambda i, lay: (i // per, i % per, 0))),
        out_shape=jax.ShapeDtypeStruct((s, r, c_dim), BF16), compiler_params=_params(("arbitrary",), 40),
    )(layer, own, other)


def _chip_exchange(bufs):
    n = len(bufs)

    def body(*refs):
        ins, outs = refs[:n], refs[n:2 * n]
        send, recv = refs[2 * n:]
        x, y, c, chips = _place()
        me = _shard_index(x, y)
        cps = []
        for t in range(n):
            for j, (px, py) in enumerate(chips):
                cp = _remote(ins[t].at[_shard_index(px, py)], outs[t].at[me], send.at[t * 3 + j], recv.at[t * 3 + j],
                             (px, py, c))
                cp.start()
                cps.append(cp)
        for t in range(n):
            for j, (px, py) in enumerate(chips):
                slot = outs[t].at[_shard_index(px, py)]
                _remote(slot, slot, send.at[t * 3 + j], recv.at[t * 3 + j], (px, py, c)).wait_recv()
        for cp in cps:
            cp.wait_send()

    return pl.pallas_call(
        body, name="chip_exchange", in_specs=[HBM_SPEC] * n, out_specs=[HBM_SPEC] * n,
        out_shape=[jax.ShapeDtypeStruct(b.shape, b.dtype) for b in bufs],
        scratch_shapes=[pltpu.SemaphoreType.DMA((n * 3,))] * 2,
    )(*bufs)


HBM_ONLY = pl.BlockSpec(memory_space=pltpu.HBM)
SEM_SPEC = pl.BlockSpec(memory_space=pltpu.SEMAPHORE)
DATAFLOW = pltpu.SideEffectType.DATAFLOW_SIDE_EFFECTING


def _in_hbm(a):
    return pltpu.with_memory_space_constraint(a, pltpu.HBM)


def _chip_exchange_start(bufs, name):
    n = len(bufs)

    def body(*refs):
        ins, lands = refs[:n], refs[n:2 * n]
        send, recv = refs[2 * n], refs[2 * n + 1]
        token = refs[-1]
        x, y, c, chips = _place()
        me = _shard_index(x, y)
        for t in range(n):
            for j, (px, py) in enumerate(chips):
                _remote(ins[t].at[_shard_index(px, py)], lands[t].at[me], send.at[t * 3 + j], recv.at[t * 3 + j],
                        (px, py, c)).start()
        token[...] = jnp.zeros_like(token)

    thru = [pltpu.HBM(b.shape, b.dtype) for b in bufs]
    res = pl.pallas_call(
        body, name=name,
        out_shape=(pltpu.SemaphoreType.DMA((n * 3,)), pltpu.SemaphoreType.DMA((n * 3,)), *thru, *thru,
                   jax.ShapeDtypeStruct((8, LANES), F32)),
        in_specs=[HBM_ONLY] * (2 * n),
        out_specs=(SEM_SPEC, SEM_SPEC, *([HBM_ONLY] * (2 * n)), pl.BlockSpec(memory_space=pltpu.VMEM)),
        input_output_aliases={i: 2 + i for i in range(2 * n)},
        compiler_params=pltpu.CompilerParams(has_side_effects=DATAFLOW),
    )(*[_in_hbm(b) for b in bufs], *[_in_hbm(lax.empty(b.shape, b.dtype)) for b in bufs])
    return res[0], res[1], list(res[2:2 + n]), list(res[2 + n:2 + 2 * n]), res[-1]


def _chip_exchange_wait(send, recv, bufs, lands, after, name):
    n = len(bufs)

    def body(*refs):
        ins, outs = refs[:n], refs[n:2 * n]
        send_ref, recv_ref = refs[2 * n], refs[2 * n + 1]
        x, y, c, chips = _place()
        for t in range(n):
            for j, (px, py) in enumerate(chips):
                sent = ins[t].at[_shard_index(px, py)]
                slot = outs[t].at[_shard_index(px, py)]
                cp = _remote(sent, slot, send_ref.at[t * 3 + j], recv_ref.at[t * 3 + j], (px, py, c))
                cp.wait_send()
                cp.wait_recv()

    thru = [pltpu.HBM(b.shape, b.dtype) for b in bufs]
    res = pl.pallas_call(
        body, name=name, out_shape=(*thru, *thru),
        in_specs=[HBM_ONLY] * (2 * n) + [SEM_SPEC, SEM_SPEC, pl.BlockSpec(memory_space=pl.ANY)],
        out_specs=[HBM_ONLY] * (2 * n),
        input_output_aliases={i: i for i in range(2 * n)},
        compiler_params=pltpu.CompilerParams(has_side_effects=DATAFLOW),
    )(*bufs, *lands, send, recv, after)
    return list(res[:n]), list(res[n:])


def _chip_sum(pair, landed, name):
    s, r, c_dim = pair.shape
    rows = min(r, LANES)
    shard = _shard_index(lax.axis_index("x"), lax.axis_index("y"))
    where = jnp.stack([shard, lax.axis_index("c")]).astype(jnp.int32)

    def landed_spec(k):
        return pl.BlockSpec((None, rows, c_dim), lambda i, w: (jnp.where(w[0] == k, (k + 1) % s, k), i, 0))

    def body(w_ref, own_ref, *rest):
        o_ref = rest[s]
        acc = None
        for k in range(s):
            term = jnp.where(w_ref[0] == k, own_ref[...], rest[k][...]).astype(F32)
            acc = term if acc is None else acc + term
        o_ref[...] = acc

    return pl.pallas_call(
        body, name=name,
        grid_spec=pltpu.PrefetchScalarGridSpec(
            num_scalar_prefetch=1, grid=(r // rows,),
            in_specs=[pl.BlockSpec((None, rows, c_dim), lambda i, w: (w[0], i, 0))] + [landed_spec(k) for k in range(s)],
            out_specs=pl.BlockSpec((None, rows, c_dim), lambda i, w: (w[1], i, 0))),
        out_shape=jax.ShapeDtypeStruct((DEPTH, r, c_dim), F32), compiler_params=_params(("arbitrary",), 40),
    )(where, pair, *([landed] * s))


def _sum_slots(buf, name):
    s, r, c_dim = buf.shape
    rows = min(r, LANES)

    def body(i_ref, o_ref):
        acc = i_ref[0].astype(F32)
        for k in range(1, s):
            acc = acc + i_ref[k].astype(F32)
        o_ref[...] = acc

    return pl.pallas_call(
        body, name=name, grid=(r // rows,),
        in_specs=[pl.BlockSpec((s, rows, c_dim), lambda i: (0, i, 0))],
        out_specs=pl.BlockSpec((rows, c_dim), lambda i: (i, 0)),
        out_shape=jax.ShapeDtypeStruct((r, c_dim), F32), compiler_params=_params(("arbitrary",), 40),
    )(buf)


def _pair_gather(bufs):
    n = len(bufs)

    def body(*refs):
        outs = refs[n:2 * n]
        send, recv = refs[2 * n:]
        x, y, c, _ = _place()
        sibling = (x, y, 1 - c)
        cps = [_remote(outs[t].at[c], outs[t].at[c], send.at[t], recv.at[t], sibling) for t in range(n)]
        for cp in cps:
            cp.start()
        for t in range(n):
            slot = outs[t].at[1 - c]
            _remote(slot, slot, send.at[t], recv.at[t], sibling).wait_recv()
        for cp in cps:
            cp.wait_send()

    return pl.pallas_call(
        body, name="pair_gather", in_specs=[HBM_SPEC] * n, out_specs=[HBM_SPEC] * n,
        out_shape=[jax.ShapeDtypeStruct(b.shape, b.dtype) for b in bufs],
        input_output_aliases={t: t for t in range(n)},
        scratch_shapes=[pltpu.SemaphoreType.DMA((n,))] * 2,
    )(*bufs)


N_DEV = 8


def _all_gather_small(vec):
    def body(v_ref, o_ref, send, recv, local_sem):
        x, y, c, _ = _place()
        me = 4 * x + 2 * y + c
        local = pltpu.make_async_copy(v_ref, o_ref.at[me], local_sem)
        local.start()
        flips = [(fx, fy, fc) for fx in (0, 1) for fy in (0, 1) for fc in (0, 1)][1:]
        peers = [((1 - x) if fx else x, (1 - y) if fy else y, (1 - c) if fc else c) for fx, fy, fc in flips]
        cps = [_remote(v_ref, o_ref.at[me], send.at[k], recv.at[k], peer) for k, peer in enumerate(peers)]
        for cp in cps:
            cp.start()
        for k, (px, py, pc) in enumerate(peers):
            slot = o_ref.at[4 * px + 2 * py + pc]
            _remote(slot, slot, send.at[k], recv.at[k], (px, py, pc)).wait_recv()
        for cp in cps:
            cp.wait_send()
        local.wait()

    return pl.pallas_call(
        body, name="all_gather_small", in_specs=[HBM_SPEC], out_specs=HBM_SPEC,
        out_shape=jax.ShapeDtypeStruct((N_DEV,) + vec.shape, vec.dtype),
        scratch_shapes=[pltpu.SemaphoreType.DMA((N_DEV - 1,))] * 2 + [pltpu.SemaphoreType.DMA(())],
    )(vec)


def _half(ref, slot, c):
    half = ref.shape[1] // 2
    return ref.at[slot, pl.ds(pl.multiple_of(c * half, 8), half)]


def _gather_start(bufs, after, name):
    n = len(bufs)
    n_after = len(after)

    def body(*refs):
        ins = refs[:n]
        send, recv = refs[n + n_after], refs[n + n_after + 1]
        token = refs[-1]
        x, y, c, chips = _place()
        me = _shard_index(x, y)
        for t in range(n):
            for j, (px, py) in enumerate(chips):
                mine = _half(ins[t], me, c)
                _remote(mine, mine, send.at[t * 3 + j], recv.at[t * 3 + j], (px, py, c)).start()
        token[...] = jnp.zeros_like(token)

    thru = [pltpu.HBM(b.shape, b.dtype) for b in bufs]
    res = pl.pallas_call(
        body, name=name,
        out_shape=(pltpu.SemaphoreType.DMA((n * 3,)), pltpu.SemaphoreType.DMA((n * 3,)), *thru,
                   jax.ShapeDtypeStruct((8, LANES), F32)),
        in_specs=[HBM_ONLY] * n + [ANY_SPEC] * n_after,
        out_specs=(SEM_SPEC, SEM_SPEC, *([HBM_ONLY] * n), pl.BlockSpec(memory_space=pltpu.VMEM)),
        input_output_aliases={i: 2 + i for i in range(n)},
        compiler_params=pltpu.CompilerParams(has_side_effects=DATAFLOW),
    )(*[_in_hbm(b) for b in bufs], *after)
    return res[0], res[1], list(res[2:2 + n]), res[-1]


def _gather_wait(send, recv, bufs, after, name):
    n = len(bufs)

    def body(*refs):
        ins = refs[:n]
        send_ref, recv_ref = refs[n], refs[n + 1]
        x, y, c, chips = _place()
        me = _shard_index(x, y)
        for t in range(n):
            for j, (px, py) in enumerate(chips):
                cp = _remote(_half(ins[t], me, c), _half(ins[t], _shard_index(px, py), c), send_ref.at[t * 3 + j],
                             recv_ref.at[t * 3 + j], (px, py, c))
                cp.wait_send()
                cp.wait_recv()

    res = pl.pallas_call(
        body, name=name, out_shape=tuple(pltpu.HBM(b.shape, b.dtype) for b in bufs),
        in_specs=[HBM_ONLY] * n + [SEM_SPEC, SEM_SPEC] + [ANY_SPEC] * len(after), out_specs=[HBM_ONLY] * n,
        input_output_aliases={i: i for i in range(n)},
        compiler_params=pltpu.CompilerParams(has_side_effects=DATAFLOW),
    )(*bufs, send, recv, *after)
    return list(res)


def _gather_forward(bufs, name):
    n = len(bufs)

    def body(*refs):
        outs = refs[n:2 * n]
        send, recv = refs[2 * n:]
        x, y, c, chips = _place()
        sibling = (x, y, 1 - c)
        cps = []
        for t in range(n):
            for j, (px, py) in enumerate(chips):
                got = _half(outs[t], _shard_index(px, py), c)
                cp = _remote(got, got, send.at[t * 3 + j], recv.at[t * 3 + j], sibling)
                cp.start()
                cps.append(cp)
        for t in range(n):
            for j, (px, py) in enumerate(chips):
                theirs = _half(outs[t], _shard_index(px, py), 1 - c)
                _remote(theirs, theirs, send.at[t * 3 + j], recv.at[t * 3 + j], sibling).wait_recv()
        for cp in cps:
            cp.wait_send()

    return pl.pallas_call(
        body, name=name, in_specs=[HBM_SPEC] * n, out_specs=[HBM_SPEC] * n,
        out_shape=[jax.ShapeDtypeStruct(b.shape, b.dtype) for b in bufs],
        input_output_aliases={t: t for t in range(n)},
        scratch_shapes=[pltpu.SemaphoreType.DMA((n * 3,))] * 2,
    )(*bufs)


def _half_exchange(grads, name):
    n = len(grads)

    def body(*refs):
        ins, outs = refs[:n], refs[n:2 * n]
        send, recv = refs[2 * n:]
        x, y, c, _ = _place()
        sibling = (x, y, 1 - c)
        cps = []
        for t in range(n):
            half = ins[t].shape[1] // 2
            theirs = ins[t].at[:, pl.ds(pl.multiple_of((1 - c) * half, 8), half)]
            cps.append(_remote(theirs, outs[t], send.at[t], recv.at[t], sibling))
        for cp in cps:
            cp.start()
        for cp in cps:
            cp.wait()

    return pl.pallas_call(
        body, name=name, in_specs=[HBM_SPEC] * n, out_specs=[HBM_SPEC] * n,
        out_shape=[jax.ShapeDtypeStruct((g.shape[0], g.shape[1] // 2, g.shape[2]), g.dtype) for g in grads],
        scratch_shapes=[pltpu.SemaphoreType.DMA((n,))] * 2,
    )(*grads)


HALF_ROWS = 64


def _core_index():
    return lax.axis_index("c").astype(jnp.int32).reshape(1)


def _half_sum(own, other, name):
    s, r, c_dim = own.shape
    per = r // 2 // HALF_ROWS

    def body(c_ref, a_ref, b_ref, o_ref):
        o_ref[...] = (a_ref[...].astype(F32) + b_ref[...].astype(F32)).astype(BF16)

    return pl.pallas_call(
        body, name=name,
        grid_spec=pltpu.PrefetchScalarGridSpec(
            num_scalar_prefetch=1, grid=(s, per),
            in_specs=[pl.BlockSpec((None, HALF_ROWS, c_dim), lambda k, i, c: (k, c[0] * per + i, 0)),
                      pl.BlockSpec((None, HALF_ROWS, c_dim), lambda k, i, c: (k, i, 0))],
            out_specs=pl.BlockSpec((None, HALF_ROWS, c_dim), lambda k, i, c: (k, i, 0))),
        out_shape=jax.ShapeDtypeStruct((s, r // 2, c_dim), BF16), compiler_params=_params(("arbitrary", "arbitrary"), 32),
    )(_core_index(), own, other)


def _reduce_start(pairs, name):
    n = len(pairs)

    def body(*refs):
        ins, lands = refs[:n], refs[n:2 * n]
        send, recv = refs[2 * n], refs[2 * n + 1]
        token = refs[-1]
        x, y, c, chips = _place()
        me = _shard_index(x, y)
        for t in range(n):
            for j, (px, py) in enumerate(chips):
                _remote(ins[t].at[_shard_index(px, py)], lands[t].at[me], send.at[t * 3 + j], recv.at[t * 3 + j],
                        (px, py, c)).start()
        token[...] = jnp.zeros_like(token)

    thru = [pltpu.HBM(b.shape, b.dtype) for b in pairs]
    res = pl.pallas_call(
        body, name=name,
        out_shape=(pltpu.SemaphoreType.DMA((n * 3,)), pltpu.SemaphoreType.DMA((n * 3,)), *thru, *thru,
                   jax.ShapeDtypeStruct((8, LANES), F32)),
        in_specs=[HBM_ONLY] * (2 * n),
        out_specs=(SEM_SPEC, SEM_SPEC, *([HBM_ONLY] * (2 * n)), pl.BlockSpec(memory_space=pltpu.VMEM)),
        input_output_aliases={i: 2 + i for i in range(2 * n)},
        compiler_params=pltpu.CompilerParams(has_side_effects=DATAFLOW),
    )(*[_in_hbm(b) for b in pairs], *[_in_hbm(lax.empty(b.shape, b.dtype)) for b in pairs])
    return res[0], res[1], list(res[2:2 + n]), list(res[2 + n:2 + 2 * n]), res[-1]


def _reduce_wait(send, recv, pairs, lands, after, name):
    n = len(pairs)

    def body(*refs):
        ins, got = refs[:n], refs[n:2 * n]
        send_ref, recv_ref = refs[2 * n], refs[2 * n + 1]
        x, y, c, chips = _place()
        for t in range(n):
            for j, (px, py) in enumerate(chips):
                s = _shard_index(px, py)
                cp = _remote(ins[t].at[s], got[t].at[s], send_ref.at[t * 3 + j], recv_ref.at[t * 3 + j], (px, py, c))
                cp.wait_send()
                cp.wait_recv()

    thru = [pltpu.HBM(b.shape, b.dtype) for b in pairs]
    res = pl.pallas_call(
        body, name=name, out_shape=(*thru, *thru),
        in_specs=[HBM_ONLY] * (2 * n) + [SEM_SPEC, SEM_SPEC] + [ANY_SPEC] * len(after),
        out_specs=[HBM_ONLY] * (2 * n),
        input_output_aliases={i: i for i in range(2 * n)},
        compiler_params=pltpu.CompilerParams(has_side_effects=DATAFLOW),
    )(*pairs, *lands, send, recv, *after)
    return list(res[:n]), list(res[n:])


def _reduce_sum(pair, landed, layer, prev, name):
    s, half, c_dim = pair.shape
    per = half // HALF_ROWS
    shard = _shard_index(lax.axis_index("x"), lax.axis_index("y"))
    where = jnp.stack([shard, lax.axis_index("c")]).astype(jnp.int32)

    def landed_spec(k):
        return pl.BlockSpec((None, HALF_ROWS, c_dim), lambda i, w: (jnp.where(w[0] == k, (k + 1) % s, k), i, 0))

    def body(w_ref, own_ref, *rest):
        o_ref = rest[-1]
        acc = None
        for k in range(s):
            term = jnp.where(w_ref[0] == k, own_ref[...], rest[k][...]).astype(F32)
            acc = term if acc is None else acc + term
        o_ref[...] = acc

    args = [where, pair] + [landed] * s
    in_specs = [pl.BlockSpec((None, HALF_ROWS, c_dim), lambda i, w: (w[0], i, 0))] + [landed_spec(k) for k in range(s)]
    aliases = {}
    if prev is not None:
        args.append(prev)
        in_specs.append(ANY_SPEC)
        aliases = {len(args) - 1: 0}
    return pl.pallas_call(
        body, name=name,
        grid_spec=pltpu.PrefetchScalarGridSpec(
            num_scalar_prefetch=1, grid=(per,), in_specs=in_specs,
            out_specs=pl.BlockSpec((None, HALF_ROWS, c_dim), lambda i, w: (layer, w[1] * per + i, 0))),
        out_shape=jax.ShapeDtypeStruct((DEPTH, 2 * half, c_dim), F32), input_output_aliases=aliases,
        compiler_params=_params(("arbitrary",), 40),
    )(*args)


def _half_gather(bufs, name):
    n = len(bufs)

    def body(*refs):
        outs = refs[n:2 * n]
        send, recv = refs[2 * n:]
        x, y, c, _ = _place()
        sibling = (x, y, 1 - c)

        def rows(t, which):
            half = outs[t].shape[1] // 2
            return outs[t].at[:, pl.ds(pl.multiple_of(which * half, 8), half)]

        cps = [_remote(rows(t, c), rows(t, c), send.at[t], recv.at[t], sibling) for t in range(n)]
        for cp in cps:
            cp.start()
        for t in range(n):
            _remote(rows(t, 1 - c), rows(t, 1 - c), send.at[t], recv.at[t], sibling).wait_recv()
        for cp in cps:
            cp.wait_send()

    return pl.pallas_call(
        body, name=name, in_specs=[HBM_SPEC] * n, out_specs=[HBM_SPEC] * n,
        out_shape=[jax.ShapeDtypeStruct(b.shape, b.dtype) for b in bufs],
        input_output_aliases={t: t for t in range(n)},
        scratch_shapes=[pltpu.SemaphoreType.DMA((n,))] * 2,
    )(*bufs)


WEIGHT_NAMES = ("ln_attn", "w_in", "sink_b", "rpb_c", "mix_gain", "w_out", "ln_ffn", "w_up", "conv_w", "conv_b",
                "w_down", "ln_final")
BIG_NAMES = ("w_in", "w_out", "w_up", "w_down")
REPLICATED_NAMES = ("ln_attn", "sink_b", "rpb_c", "mix_gain", "ln_ffn", "conv_b", "ln_final")
PACK_TILE = 8 * LANES


def _pack(arrays, row_multiple):
    pieces = []
    for a in arrays:
        flat = a.reshape(-1)
        pieces.append(jnp.pad(flat, (0, (-flat.shape[0]) % PACK_TILE)))
    flat = jnp.concatenate(pieces)
    flat = jnp.pad(flat, (0, (-flat.shape[0]) % (row_multiple * LANES)))
    return flat.reshape(-1, LANES)


def _unpack(packed, shapes):
    flat = packed.reshape(-1)
    out, off = [], 0
    for shape in shapes:
        size = math.prod(shape)
        out.append(flat[off:off + size].reshape(shape))
        off += size + (-size) % PACK_TILE
    return out


def kernel(x, ln_attn, w_in, sink_b, rpb_c, mix_gain, w_out, ln_ffn, w_up, conv_w, conv_b, w_down, ln_final, loss_target, m_ln_attn, m_w_in, m_sink_b, m_rpb_c, m_mix_gain, m_w_out, m_ln_ffn, m_w_up, m_conv_w, m_conv_b, m_w_down, m_ln_final, v_ln_attn, v_w_in, v_sink_b, v_rpb_c, v_mix_gain, v_w_out, v_ln_ffn, v_w_up, v_conv_w, v_conv_b, v_w_down, v_ln_final):
    w = dict(ln_attn=ln_attn, w_in=w_in, sink_b=sink_b, rpb_c=rpb_c, mix_gain=mix_gain, w_out=w_out, ln_ffn=ln_ffn,
             w_up=w_up, conv_w=conv_w, conv_b=conv_b, w_down=w_down, ln_final=ln_final)
    m = dict(ln_attn=m_ln_attn, w_in=m_w_in, sink_b=m_sink_b, rpb_c=m_rpb_c, mix_gain=m_mix_gain, w_out=m_w_out,
             ln_ffn=m_ln_ffn, w_up=m_w_up, conv_w=m_conv_w, conv_b=m_conv_b, w_down=m_w_down, ln_final=m_ln_final)
    v = dict(ln_attn=v_ln_attn, w_in=v_w_in, sink_b=v_sink_b, rpb_c=v_rpb_c, mix_gain=v_mix_gain, w_out=v_w_out,
             ln_ffn=v_ln_ffn, w_up=v_w_up, conv_w=v_conv_w, conv_b=v_conv_b, w_down=v_w_down, ln_final=v_ln_final)
    shard = _shard_index(lax.axis_index("x"), lax.axis_index("y"))
    up_cols = w_up.shape[2]

    conv_all = _all_gather_small(_pack([conv_w], 8))[0::2].reshape(N_SHARDS, -1)[:, :conv_w.size]
    conv_all = conv_all.reshape((N_SHARDS,) + conv_w.shape)

    arrivals = []
    group_of = {}
    tokens = []
    for l, names in ((0, ("w_in",)), (0, ("w_out", "w_up", "w_down")), (1, BIG_NAMES)):
        bufs = [_own_slot(w[k], l, shard, "own_" + k) for k in names]
        send, recv, bufs, token = _gather_start(bufs, tokens[-1:], "gather_start_%d" % len(arrivals))
        tokens.append(token)
        for k in names:
            group_of[l, k] = len(arrivals)
        arrivals.append({"names": names, "send": send, "recv": recv, "bufs": bufs, "done": None})

    def gathered(l, name, after):
        idx = group_of[l, name]
        group = arrivals[idx]
        if group["done"] is None:
            bufs = _gather_wait(group["send"], group["recv"], group["bufs"], list(after) + tokens[-1:],
                                "gather_wait_%d" % idx)
            group["done"] = dict(zip(group["names"], _gather_forward(bufs, "gather_forward_%d" % idx)))
        buf = group["done"][name]
        return buf.reshape(1, -1, buf.shape[2]) if name in ("w_out", "w_down") else buf

    cos, sin = _rope_tables(SEQ)
    tabs = {"cos": cos, "sin": sin, "bias_a": _bias_a(), "bias_b": _bias_b()}
    layers = []
    for l in range(DEPTH):
        conv_w_l = conv_all[:, l].reshape(2, N_SHARDS // 2, 3, up_cols).transpose(0, 2, 1, 3).reshape(2, 3, D_FF)
        layers.append({"ln_attn": ln_attn[l][None], "sink_b": sink_b[l], "bias_c": _bias_c(rpb_c[l]),
                       "mix_gain": mix_gain[l][None], "ln_ffn": ln_ffn[l][None], "conv_w": conv_w_l,
                       "conv_b": conv_b[l].reshape(2, 1, D_FF)})

    act = x[0]
    saved = []
    for l in range(DEPTH):
        act, keep = _layer_fwd(act, layers[l], lambda name, after, l=l: gathered(l, name, [after]), tabs)
        saved.append(keep)
    loss_part, dx, dx_b, d_ln_final = _loss_head(act, ln_final[None], loss_target[0], "loss_head")
    loss = lax.psum(loss_part[0, 0], ("x", "y", "c"))

    reductions = []

    def sender(l):
        def send(partial):
            idx = len(reductions)
            names = tuple(partial)
            mine = [partial[k] for k in names]
            theirs = _half_exchange(mine, "half_exchange_%d" % idx)
            pairs = [_half_sum(a, b, "half_sum_" + k) for k, a, b in zip(names, mine, theirs)]
            send_sem, recv_sem, pairs, lands, token = _reduce_start(pairs, "reduce_start_%d" % idx)
            reductions.append({"layer": l, "names": names, "send": send_sem, "recv": recv_sem, "pairs": pairs,
                               "lands": lands})
            return token
        return send

    small = [None] * DEPTH
    after = []
    for l in reversed(range(DEPTH)):
        big = {k: gathered(l, k, []) for k in BIG_NAMES}
        dx, dx_b, small[l], token = _layer_bwd(dx, dx_b, saved[l], layers[l], big, tabs, sender(l), after)
        after = [token]

    reduced = {}
    for idx, group in enumerate(reductions):
        pairs, lands = _reduce_wait(group["send"], group["recv"], group["pairs"], group["lands"], after,
                                    "reduce_wait_%d" % idx)
        for k, pair, landed in zip(group["names"], pairs, lands):
            reduced[k] = _reduce_sum(pair, landed, group["layer"], reduced.get(k), "reduce_sum_" + k)
        after = [reduced[group["names"][-1]]]
    grads = dict(zip(BIG_NAMES, _half_gather([reduced[k] for k in BIG_NAMES], "half_gather")))

    stacked = {k: jnp.stack([small[l][k] for l in range(DEPTH)]) for k in small[0]}
    part = {"ln_attn": stacked["ln_attn"][:, 0], "sink_b": stacked["sink_b"], "rpb_c": stacked["rpb_c"],
            "mix_gain": stacked["mix_gain"][:, 0], "ln_ffn": stacked["ln_ffn"][:, 0],
            "conv_b": stacked["conv_b"].reshape(DEPTH, 2 * D_FF), "ln_final": d_ln_final[0],
            "conv_w": stacked["conv_w"].transpose(0, 2, 1, 3).reshape(DEPTH, 3, 2 * D_FF)}
    names = REPLICATED_NAMES + ("conv_w",)
    total = _sum_slots(_all_gather_small(_pack([part[k] for k in names], 256)), "sum_small")
    for k, g in zip(names, _unpack(total, [part[k].shape for k in names])):
        grads[k] = g
    grads["conv_w"] = lax.dynamic_slice_in_dim(grads["conv_w"], shard * up_cols, up_cols, axis=2)

    delta, new_m, new_v = {}, {}, {}
    for k in BIG_NAMES + ("conv_w",):
        shape = w[k].shape
        flat = (shape[0] * shape[1], shape[2])
        res = _adamw(w[k].reshape(flat), grads[k].reshape(flat), m[k].reshape(flat), v[k].reshape(flat), "adamw_" + k)
        delta[k], new_m[k], new_v[k] = (r.reshape(shape) for r in res)
    shapes = [w[k].shape for k in REPLICATED_NAMES]
    packed = [_pack([d[k] for k in REPLICATED_NAMES], 128) for d in (w, grads, m, v)]
    for d, res in zip((delta, new_m, new_v), _adamw(*packed, "adamw_small")):
        for k, r in zip(REPLICATED_NAMES, _unpack(res, shapes)):
            d[k] = r

    return (loss, dx[None], *[grads[k] for k in WEIGHT_NAMES], *[delta[k] for k in WEIGHT_NAMES],
            *[new_m[k] for k in WEIGHT_NAMES], *[new_v[k] for k in WEIGHT_NAMES])
```

```python
import functools
import math

import jax
import jax.numpy as jnp
from jax import lax
from jax.experimental import pallas as pl
from jax.experimental.pallas import tpu as pltpu

F32 = jnp.float32
BF16 = jnp.bfloat16
MESH = pl.DeviceIdType.MESH

D_MODEL = 2048
SEQ = 2048
DEPTH = 2
HEAD_DIM = 64
N_HEADS_A = 12
N_HEADS_B = 10
N_KV_B = 2
N_HEADS_C = 10
WINDOW_B = 128
GRID_W = 64
NA_ROWS = 8
NA_COLS = 16
WIDTH_A = N_HEADS_A * HEAD_DIM
WIDTH_B = N_HEADS_B * HEAD_DIM
WIDTH_C = N_HEADS_C * HEAD_DIM
IN_COLS = 5120
D_FF = 5632
ROPE_THETA = 10000.0
EPS = 1e-6
NEG_INF = -1e30
N_SHARDS = 4

ADAM_LR = 0.001
ADAM_B1 = 0.9
ADAM_B2 = 0.999
ADAM_EPS = 1e-08
ADAM_WD = 0.01
ADAM_STEP = 10

LANES = 128
QB = 256
NQB = SEQ // QB
ROWS = 256
MIB = 2 ** 20

A_BLK = (0, 6, 12)
B_BLK = (18, 23, 24)
C_BLK = (25, 30, 35)
ROPE_BLKS = tuple(range(0, 12)) + tuple(range(18, 24))
QSCALE_BLKS = tuple(range(0, 6)) + tuple(range(18, 23)) + tuple(range(25, 30))
N_PBLK = IN_COLS // LANES


def _params(sem, vmem_mib):
    return pltpu.CompilerParams(dimension_semantics=sem, vmem_limit_bytes=vmem_mib * MIB)


def _weight_spec(w, cols, t_in, t_out, transposed):
    s, r, c = w.shape
    if cols:
        per = c // t_out
        k_dim, n = r, s * c
        if transposed:
            index = lambda j, rr: (rr // per, j, rr % per)
        else:
            index = lambda j, kk: (j // per, kk, j % per)
    else:
        per = r // t_in
        k_dim, n = s * r, c
        if transposed:
            index = lambda j, rr: (j // per, j % per, rr)
        else:
            index = lambda j, kk: (kk // per, kk % per, j)
    return pl.BlockSpec((None, t_in, t_out), index), k_dim, n


def _mm_nn(a, w, *, cols, tn, tk, out_dtype, name, residual=None, out_split=1):
    m, k_dim = a.shape
    w_spec, k_w, n = _weight_spec(w, cols, tk, tn, False)
    assert k_w == k_dim
    nj, nk = n // tn, k_dim // tk
    in_specs = [pl.BlockSpec((m, tk), lambda j, k: (0, k)), w_spec]
    args = [a, w]
    if residual is not None:
        in_specs.append(pl.BlockSpec((m, tn), lambda j, k: (0, j)))
        args.append(residual)
    if out_split > 1:
        per_o = n // out_split // tn
        out_spec = pl.BlockSpec((None, m, tn), lambda j, k: (j // per_o, 0, j % per_o))
        out_shape = jax.ShapeDtypeStruct((out_split, m, n // out_split), out_dtype)
    else:
        out_spec = pl.BlockSpec((m, tn), lambda j, k: (0, j))
        out_shape = jax.ShapeDtypeStruct((m, n), out_dtype)

    def body(*refs):
        a_ref, w_ref = refs[0], refs[1]
        r_ref = refs[2] if residual is not None else None
        o_ref = refs[3] if residual is not None else refs[2]

        def finish(val):
            if r_ref is not None:
                val = r_ref[...] + val
            o_ref[...] = val.astype(o_ref.dtype)

        part = jnp.dot(a_ref[...], w_ref[...], preferred_element_type=F32)
        if nk == 1:
            finish(part)
        else:
            acc = refs[-1]
            kk = pl.program_id(1)

            @pl.when(kk == 0)
            def _():
                acc[...] = part

            @pl.when(kk > 0)
            def _():
                acc[...] += part

            @pl.when(kk == nk - 1)
            def _():
                finish(acc[...])

    return pl.pallas_call(
        body, name=name, grid=(nj, nk), in_specs=in_specs, out_specs=out_spec, out_shape=out_shape,
        scratch_shapes=[pltpu.VMEM((m, tn), F32)] if nk > 1 else [],
        compiler_params=_params(("arbitrary", "arbitrary"), 56),
    )(*args)


ANY_SPEC = pl.BlockSpec(memory_space=pl.ANY)


def _mm_nt(dy, w, *, cols, to, tr, out_dtype, name, after=()):
    if dy.ndim == 3:
        m = dy.shape[1]
        n = dy.shape[0] * dy.shape[2]
        per_d = dy.shape[2] // tr
        dy_spec = pl.BlockSpec((None, m, tr), lambda j, r: (r // per_d, 0, r % per_d))
    else:
        m, n = dy.shape
        dy_spec = pl.BlockSpec((m, tr), lambda j, r: (0, r))
    w_spec, k_dim, n_w = _weight_spec(w, cols, to, tr, True)
    assert n_w == n
    nj, nr = k_dim // to, n // tr

    n_after = len(after)

    def body(dy_ref, w_ref, *rest):
        o_ref = rest[n_after]
        part = lax.dot_general(dy_ref[...], w_ref[...], (((1,), (1,)), ((), ())), preferred_element_type=F32)
        if nr == 1:
            o_ref[...] = part.astype(o_ref.dtype)
        else:
            acc = rest[n_after + 1]
            rr = pl.program_id(1)

            @pl.when(rr == 0)
            def _():
                acc[...] = part

            @pl.when(rr > 0)
            def _():
                acc[...] += part

            @pl.when(rr == nr - 1)
            def _():
                o_ref[...] = acc[...].astype(o_ref.dtype)

    return pl.pallas_call(
        body, name=name, grid=(nj, nr), in_specs=[dy_spec, w_spec] + [ANY_SPEC] * n_after,
        out_specs=pl.BlockSpec((m, to), lambda j, r: (0, j)),
        out_shape=jax.ShapeDtypeStruct((m, k_dim), out_dtype),
        scratch_shapes=[pltpu.VMEM((m, to), F32)] if nr > 1 else [],
        compiler_params=_params(("arbitrary", "arbitrary"), 56),
    )(dy, w, *after)


def _mm_tn(x, dy, *, tk, tn, shards, name):
    m, k_dim = x.shape
    if dy.ndim == 3:
        n = dy.shape[0] * dy.shape[2]
        per_d = dy.shape[2] // tn
        dy_spec = pl.BlockSpec((None, m, tn), lambda i, j: (j // per_d, 0, j % per_d))
    else:
        n = dy.shape[1]
        dy_spec = pl.BlockSpec((m, tn), lambda i, j: (0, j))
    if shards > 0:
        per = n // shards // tn
        out_shape = jax.ShapeDtypeStruct((shards, k_dim, n // shards), BF16)
        out_spec = pl.BlockSpec((None, tk, tn), lambda i, j: (j // per, i, j % per))
    else:
        s = -shards
        per = k_dim // s // tk
        out_shape = jax.ShapeDtypeStruct((s, k_dim // s, n), BF16)
        out_spec = pl.BlockSpec((None, tk, tn), lambda i, j: (i // per, i % per, j))

    def body(x_ref, dy_ref, o_ref):
        o_ref[...] = lax.dot_general(x_ref[...], dy_ref[...], (((0,), (0,)), ((), ())),
                                     preferred_element_type=F32).astype(BF16)

    return pl.pallas_call(
        body, name=name, grid=(k_dim // tk, n // tn),
        in_specs=[pl.BlockSpec((m, tk), lambda i, j: (0, i)), dy_spec], out_specs=out_spec, out_shape=out_shape,
        compiler_params=_params(("arbitrary", "arbitrary"), 56),
    )(x, dy)


def _row_spec(width, rows=ROWS):
    return pl.BlockSpec((rows, width), lambda i: (i, 0))


def _vec_spec(width):
    return pl.BlockSpec((1, width), lambda i: (0, 0))


def _rms_stats(x):
    r = lax.rsqrt(jnp.mean(x * x, axis=-1, keepdims=True) + EPS)
    return r, x * r


def _rmsnorm_fwd(x, gain, name):
    t, d = x.shape

    def body(x_ref, g_ref, o_ref):
        _, n = _rms_stats(x_ref[...])
        o_ref[...] = (n * g_ref[...]).astype(BF16)

    return pl.pallas_call(
        body, name=name, grid=(t // ROWS,), in_specs=[_row_spec(d), _vec_spec(d)], out_specs=_row_spec(d),
        out_shape=jax.ShapeDtypeStruct((t, d), BF16), compiler_params=_params(("arbitrary",), 32),
    )(x, gain)


def _rmsnorm_bwd(x, gain, dh, dres, name, after=()):
    t, d = x.shape
    n_after = len(after)

    def body(x_ref, g_ref, dh_ref, dres_ref, *rest):
        dx_ref, dxb_ref, dg_ref = rest[n_after:]
        r, n = _rms_stats(x_ref[...])
        dh_v = dh_ref[...]
        dn = dh_v * g_ref[...]
        dx = dres_ref[...] + r * (dn - n * jnp.mean(dn * n, axis=-1, keepdims=True))
        dx_ref[...] = dx
        dxb_ref[...] = dx.astype(BF16)
        part = jnp.sum(dh_v * n, axis=0, keepdims=True)

        @pl.when(pl.program_id(0) == 0)
        def _():
            dg_ref[...] = part

        @pl.when(pl.program_id(0) > 0)
        def _():
            dg_ref[...] += part

    return pl.pallas_call(
        body, name=name, grid=(t // ROWS,),
        in_specs=[_row_spec(d), _vec_spec(d), _row_spec(d), _row_spec(d)] + [ANY_SPEC] * n_after,
        out_specs=[_row_spec(d), _row_spec(d), _vec_spec(d)],
        out_shape=[jax.ShapeDtypeStruct((t, d), F32), jax.ShapeDtypeStruct((t, d), BF16),
                   jax.ShapeDtypeStruct((1, d), F32)],
        compiler_params=_params(("arbitrary",), 40),
    )(x, gain, dh, dres, *after)


def _loss_head(x, gain, target, name):
    t, d = x.shape

    def body(x_ref, g_ref, t_ref, loss_ref, dx_ref, dxb_ref, dg_ref):
        r, n = _rms_stats(x_ref[...])
        g = g_ref[...]
        err = n * g - t_ref[...]
        dy = err * (1.0 / d)
        dn = dy * g
        dx = r * (dn - n * jnp.mean(dn * n, axis=-1, keepdims=True))
        dx_ref[...] = dx
        dxb_ref[...] = dx.astype(BF16)
        part = jnp.sum(dy * n, axis=0, keepdims=True)
        lpart = jnp.zeros((8, LANES), F32) + 0.5 * jnp.sum(jnp.mean(err * err, axis=-1, keepdims=True))

        @pl.when(pl.program_id(0) == 0)
        def _():
            dg_ref[...] = part
            loss_ref[...] = lpart

        @pl.when(pl.program_id(0) > 0)
        def _():
            dg_ref[...] += part
            loss_ref[...] += lpart

    return pl.pallas_call(
        body, name=name, grid=(t // ROWS,),
        in_specs=[_row_spec(d), _vec_spec(d), _row_spec(d)],
        out_specs=[pl.BlockSpec((8, LANES), lambda i: (0, 0)), _row_spec(d), _row_spec(d), _vec_spec(d)],
        out_shape=[jax.ShapeDtypeStruct((8, LANES), F32), jax.ShapeDtypeStruct((t, d), F32),
                   jax.ShapeDtypeStruct((t, d), BF16), jax.ShapeDtypeStruct((1, d), F32)],
        compiler_params=_params(("arbitrary",), 40),
    )(x, gain, target)


def _swap_halves(x):
    lane = lax.broadcasted_iota(jnp.int32, x.shape, 1)
    return jnp.where((lane % HEAD_DIM) < HEAD_DIM // 2, pltpu.roll(x, LANES - HEAD_DIM // 2, 1),
                     pltpu.roll(x, HEAD_DIM // 2, 1))


def _rope_tables(t):
    inv_freq = ROPE_THETA ** (-jnp.arange(0, HEAD_DIM, 2, dtype=F32) / HEAD_DIM)
    ang = jnp.arange(t, dtype=F32)[:, None] * inv_freq[None, :]
    cos = jnp.tile(jnp.cos(ang), (1, LANES // (HEAD_DIM // 2)))
    sin = jnp.tile(jnp.sin(ang), (1, LANES // (HEAD_DIM // 2)))
    lane = jnp.arange(LANES)[None, :]
    return cos, jnp.where((lane % HEAD_DIM) < HEAD_DIM // 2, -sin, sin)


def _rope_fwd(proj, cos, sin, name):
    t = proj.shape[0]
    scale = HEAD_DIM ** -0.5

    def body(p_ref, c_ref, s_ref, o_ref):
        cos_v, sin_v = c_ref[...], s_ref[...]
        for b in range(N_PBLK):
            cols = slice(b * LANES, (b + 1) * LANES)
            v = p_ref[:, cols]
            if b in ROPE_BLKS:
                v = v * cos_v + _swap_halves(v) * sin_v
            if b in QSCALE_BLKS:
                v = v * scale
            o_ref[:, cols] = v.astype(BF16)

    return pl.pallas_call(
        body, name=name, grid=(t // ROWS,),
        in_specs=[_row_spec(IN_COLS), _row_spec(LANES), _row_spec(LANES)], out_specs=_row_spec(IN_COLS),
        out_shape=jax.ShapeDtypeStruct((t, IN_COLS), BF16), compiler_params=_params(("arbitrary",), 40),
    )(proj, cos, sin)


def _rope_bwd(grads, cos, sin, name):
    t = grads[0].shape[0]
    scale = HEAD_DIM ** -0.5
    group = N_HEADS_B // N_KV_B

    def body(*refs):
        c_ref, s_ref, o_ref = refs[9], refs[10], refs[11]
        cos_v, sin_v = c_ref[...], s_ref[...]

        def kv_sum(ref):
            parts = []
            for g in range(N_KV_B):
                acc = ref[:, g * group * HEAD_DIM:(g * group + 1) * HEAD_DIM]
                for h in range(g * group + 1, (g + 1) * group):
                    acc = acc + ref[:, h * HEAD_DIM:(h + 1) * HEAD_DIM]
                parts.append(acc)
            return jnp.concatenate(parts, axis=1)

        def emit(b, v):
            if b in ROPE_BLKS:
                v = v * cos_v - _swap_halves(v) * sin_v
            if b in QSCALE_BLKS:
                v = v * scale
            o_ref[:, b * LANES:(b + 1) * LANES] = v.astype(BF16)

        starts = (A_BLK[0], A_BLK[1], A_BLK[2], B_BLK[0], None, None, C_BLK[0], C_BLK[1], C_BLK[2])
        for idx, start in enumerate(starts):
            if start is None:
                continue
            for j in range(refs[idx].shape[1] // LANES):
                emit(start + j, refs[idx][:, j * LANES:(j + 1) * LANES])
        emit(B_BLK[1], kv_sum(refs[4]))
        emit(B_BLK[2], kv_sum(refs[5]))

    return pl.pallas_call(
        body, name=name, grid=(t // ROWS,),
        in_specs=[_row_spec(g.shape[1]) for g in grads] + [_row_spec(LANES), _row_spec(LANES)],
        out_specs=_row_spec(IN_COLS),
        out_shape=jax.ShapeDtypeStruct((t, IN_COLS), BF16), compiler_params=_params(("arbitrary",), 40),
    )(*grads, cos, sin)


GROUP_COLS = ((0, WIDTH_A), (WIDTH_A, WIDTH_A + WIDTH_B), (WIDTH_A + WIDTH_B, D_MODEL))


def _mix_fwd(oa, ob, oc, gain, name):
    t = oa.shape[0]

    def body(a_ref, b_ref, c_ref, g_ref, o_ref):
        for ref, (lo, hi) in zip((a_ref, b_ref, c_ref), GROUP_COLS):
            _, n = _rms_stats(ref[...])
            o_ref[:, lo:hi] = (n * g_ref[:, lo:hi]).astype(BF16)

    return pl.pallas_call(
        body, name=name, grid=(t // ROWS,),
        in_specs=[_row_spec(WIDTH_A), _row_spec(WIDTH_B), _row_spec(WIDTH_C), _vec_spec(D_MODEL)],
        out_specs=_row_spec(D_MODEL),
        out_shape=jax.ShapeDtypeStruct((t, D_MODEL), BF16), compiler_params=_params(("arbitrary",), 32),
    )(oa, ob, oc, gain)


def _mix_bwd(oa, ob, oc, gain, dmixed, name):
    t = oa.shape[0]

    def body(a_ref, b_ref, c_ref, g_ref, dm_ref, da_ref, db_ref, dc_ref, dg_ref):
        first = pl.program_id(0) == 0
        for ref, dref, (lo, hi) in zip((a_ref, b_ref, c_ref), (da_ref, db_ref, dc_ref), GROUP_COLS):
            r, n = _rms_stats(ref[...])
            dm = dm_ref[:, lo:hi]
            dn = dm * g_ref[:, lo:hi]
            dref[...] = r * (dn - n * jnp.mean(dn * n, axis=-1, keepdims=True))
            part = jnp.sum(dm * n, axis=0, keepdims=True)

            @pl.when(first)
            def _():
                dg_ref[:, lo:hi] = part

            @pl.when(jnp.logical_not(first))
            def _():
                dg_ref[:, lo:hi] += part

    return pl.pallas_call(
        body, name=name, grid=(t // ROWS,),
        in_specs=[_row_spec(WIDTH_A), _row_spec(WIDTH_B), _row_spec(WIDTH_C), _vec_spec(D_MODEL), _row_spec(D_MODEL)],
        out_specs=[_row_spec(WIDTH_A), _row_spec(WIDTH_B), _row_spec(WIDTH_C), _vec_spec(D_MODEL)],
        out_shape=[jax.ShapeDtypeStruct((t, WIDTH_A), F32), jax.ShapeDtypeStruct((t, WIDTH_B), F32),
                   jax.ShapeDtypeStruct((t, WIDTH_C), F32), jax.ShapeDtypeStruct((1, D_MODEL), F32)],
        compiler_params=_params(("arbitrary",), 40),
    )(oa, ob, oc, gain, dmixed)


FF_COLS = 256


def _shift_rows(x, by):
    t = x.shape[0]
    row = lax.broadcasted_iota(jnp.int32, x.shape, 0)
    rolled = pltpu.roll(x, by % t, 0)
    return jnp.where(row == (0 if by == 1 else t - 1), 0.0, rolled)


def _conv(u0, w_ref, b_ref, h):
    return (_shift_rows(u0, 1) * w_ref[h, 0:1, :] + u0 * w_ref[h, 1:2, :]) + _shift_rows(u0, -1) * w_ref[h, 2:3, :] \
        + b_ref[h]


def _ff_specs(t):
    u_spec = pl.BlockSpec((2, t, FF_COLS), lambda j: (0, 0, j))
    w_spec = pl.BlockSpec((2, 3, FF_COLS), lambda j: (0, 0, j))
    b_spec = pl.BlockSpec((2, 1, FF_COLS), lambda j: (0, 0, j))
    return u_spec, w_spec, b_spec


def _convgate_fwd(u0, conv_w, conv_b, name):
    t = u0.shape[1]
    u_spec, w_spec, b_spec = _ff_specs(t)

    def body(u_ref, w_ref, b_ref, o_ref):
        gate = _conv(u_ref[0], w_ref, b_ref, 0)
        val = _conv(u_ref[1], w_ref, b_ref, 1)
        o_ref[...] = (gate * jax.nn.sigmoid(gate) * val).astype(BF16)

    return pl.pallas_call(
        body, name=name, grid=(D_FF // FF_COLS,), in_specs=[u_spec, w_spec, b_spec],
        out_specs=pl.BlockSpec((t, FF_COLS), lambda j: (0, j)),
        out_shape=jax.ShapeDtypeStruct((t, D_FF), BF16), compiler_params=_params(("arbitrary",), 48),
    )(u0, conv_w, conv_b)


def _convgate_bwd(u0, conv_w, conv_b, d_act, name):
    t = u0.shape[1]
    u_spec, w_spec, b_spec = _ff_specs(t)

    def body(u_ref, w_ref, b_ref, da_ref, du_ref, dw_ref, db_ref):
        gate = _conv(u_ref[0], w_ref, b_ref, 0)
        val = _conv(u_ref[1], w_ref, b_ref, 1)
        sig = jax.nn.sigmoid(gate)
        da = da_ref[...]
        d_half = (da * val * (sig * (1.0 + gate * (1.0 - sig))), da * (gate * sig))
        for h in range(2):
            du = d_half[h]
            u0_h = u_ref[h]
            db_ref[h] = jnp.sum(du, axis=0, keepdims=True)
            dw_ref[h, 0:1, :] = jnp.sum(du * _shift_rows(u0_h, 1), axis=0, keepdims=True)
            dw_ref[h, 1:2, :] = jnp.sum(du * u0_h, axis=0, keepdims=True)
            dw_ref[h, 2:3, :] = jnp.sum(du * _shift_rows(u0_h, -1), axis=0, keepdims=True)
            du_ref[h] = ((_shift_rows(du, -1) * w_ref[h, 0:1, :] + du * w_ref[h, 1:2, :])
                         + _shift_rows(du, 1) * w_ref[h, 2:3, :]).astype(BF16)

    return pl.pallas_call(
        body, name=name, grid=(D_FF // FF_COLS,),
        in_specs=[u_spec, w_spec, b_spec, pl.BlockSpec((t, FF_COLS), lambda j: (0, j))],
        out_specs=[u_spec, w_spec, b_spec],
        out_shape=[jax.ShapeDtypeStruct((2, t, D_FF), BF16), jax.ShapeDtypeStruct((2, 3, D_FF), F32),
                   jax.ShapeDtypeStruct((2, 1, D_FF), F32)],
        compiler_params=_params(("arbitrary",), 56),
    )(u0, conv_w, conv_b, d_act)


class _Group:
    def __init__(self, heads, blks, kv_rows, n_win, gqa, bias_per_head):
        self.heads = heads
        self.pairs = heads // 2
        self.q_blk, self.k_blk, self.v_blk = blks
        self.kv_rows = kv_rows
        self.n_win = n_win
        self.full = kv_rows == SEQ
        self.gqa = gqa
        self.bias_per_head = bias_per_head
        self.width = heads * HEAD_DIM
        self.keys = kv_rows * n_win


GROUP_A = _Group(N_HEADS_A, A_BLK, SEQ, 1, False, False)
GROUP_B = _Group(N_HEADS_B, B_BLK, WINDOW_B, 4, True, False)
GROUP_C = _Group(N_HEADS_C, C_BLK, QB, 3, False, True)


def _win_start(grp, i):
    return jnp.clip(i * (QB // grp.kv_rows) - 1, 0, SEQ // grp.kv_rows - grp.n_win)


def _win_variant(i):
    return jnp.minimum(i, 1) + (i == NQB - 1).astype(jnp.int32)


def _attn_in_specs(grp, t):
    q_spec = pl.BlockSpec((QB, LANES), lambda p, i: (i, grp.q_blk + p))

    def col(blk):
        return (lambda p: blk) if grp.gqa else (lambda p: blk + p)

    def kv_specs(blk):
        c = col(blk)
        if grp.full:
            return [pl.BlockSpec((t, LANES), lambda p, i: (0, c(p)))]
        return [pl.BlockSpec((grp.kv_rows, LANES),
                             functools.partial(lambda p, i, w: (_win_start(grp, i) + w, c(p)), w=w))
                for w in range(grp.n_win)]

    nwk = grp.keys
    if grp.bias_per_head:
        bias_spec = pl.BlockSpec((2, None, QB, nwk), lambda p, i: (p, _win_variant(i), 0, 0))
    elif grp.full:
        bias_spec = pl.BlockSpec((1, None, QB, nwk), lambda p, i: (0, i, 0, 0))
    else:
        bias_spec = pl.BlockSpec((1, None, QB, nwk), lambda p, i: (0, _win_variant(i), 0, 0))
    sink_spec = pl.BlockSpec((1, LANES), lambda p, i: (0, p))
    return q_spec, kv_specs(grp.k_blk), kv_specs(grp.v_blk), bias_spec, sink_spec


def _head_kv(grp, whole, e, p):
    lo, hi = whole[:, :HEAD_DIM], whole[:, HEAD_DIM:]
    if grp.gqa:
        return jnp.where(2 * p + e >= N_HEADS_B // N_KV_B, hi, lo)
    return hi if e else lo


def _softmax_parts(q, k, bias, sink):
    s = lax.dot_general(q, k, (((1,), (1,)), ((), ())), preferred_element_type=F32) + bias
    m = jnp.maximum(jnp.max(s, axis=-1, keepdims=True), sink)
    pe = jnp.exp(s - m)
    denom = jnp.sum(pe, axis=-1, keepdims=True) + jnp.exp(sink - m)
    return pe, m, 1.0 / denom


def _attn_fwd(grp, proj, bias, sink, name):
    t = proj.shape[0]
    q_spec, k_specs, v_specs, bias_spec, sink_spec = _attn_in_specs(grp, t)
    nkv = len(k_specs)

    def body(*refs):
        q_ref = refs[0]
        k_refs, v_refs = refs[1:1 + nkv], refs[1 + nkv:1 + 2 * nkv]
        bias_ref, sink_ref, o_ref = refs[1 + 2 * nkv:4 + 2 * nkv]
        p = pl.program_id(0)
        k_all = jnp.concatenate([r[...] for r in k_refs], axis=0)
        v_all = jnp.concatenate([r[...] for r in v_refs], axis=0)
        outs = []
        for e in range(2):
            q = q_ref[:, e * HEAD_DIM:(e + 1) * HEAD_DIM]
            k = _head_kv(grp, k_all, e, p)
            v = _head_kv(grp, v_all, e, p)
            snk = sink_ref[0:1, e * HEAD_DIM:e * HEAD_DIM + 1]
            pe, _, inv = _softmax_parts(q, k, bias_ref[e if grp.bias_per_head else 0], snk)
            outs.append(jnp.dot(pe.astype(BF16), v, preferred_element_type=F32) * inv)
        o_ref[...] = jnp.concatenate(outs, axis=1)

    return pl.pallas_call(
        body, name=name, grid=(grp.pairs, NQB),
        in_specs=[q_spec, *k_specs, *v_specs, bias_spec, sink_spec],
        out_specs=pl.BlockSpec((QB, LANES), lambda p, i: (i, p)),
        out_shape=jax.ShapeDtypeStruct((t, grp.width), F32),
        compiler_params=_params(("arbitrary", "arbitrary"), 48),
    )(proj, *([proj] * (2 * nkv)), bias, sink)


def _attn_bwd(grp, proj, bias, sink, out, d_out, name):
    t = proj.shape[0]
    q_spec, k_specs, v_specs, bias_spec, sink_spec = _attn_in_specs(grp, t)
    nkv = len(k_specs)
    n_off = 2 * NA_ROWS - 1
    rows_q = QB // GRID_W
    o_spec = pl.BlockSpec((QB, LANES), lambda p, i: (i, p))
    acc_spec = pl.BlockSpec((t, LANES), lambda p, i: (0, p))
    out_specs = [o_spec, acc_spec, acc_spec, pl.BlockSpec((None, 8, LANES), lambda p, i: (p, 0, 0))]
    out_shape = [jax.ShapeDtypeStruct((t, grp.width), F32)] * 3 + [jax.ShapeDtypeStruct((grp.pairs, 8, LANES), F32)]
    if grp.bias_per_head:
        out_specs.append(pl.BlockSpec((2, n_off, GRID_W, GRID_W), lambda p, i: (p, 0, 0, 0)))
        out_shape.append(jax.ShapeDtypeStruct((grp.heads, n_off, GRID_W, GRID_W), F32))

    def body(*refs):
        q_ref = refs[0]
        k_refs, v_refs = refs[1:1 + nkv], refs[1 + nkv:1 + 2 * nkv]
        bias_ref, sink_ref, o_ref, do_ref = refs[1 + 2 * nkv:5 + 2 * nkv]
        dq_ref, dk_ref, dv_ref, dsink_ref = refs[5 + 2 * nkv:9 + 2 * nkv]
        dbias_ref = refs[9 + 2 * nkv] if grp.bias_per_head else None
        p, i = pl.program_id(0), pl.program_id(1)

        @pl.when(i == 0)
        def _():
            dk_ref[...] = jnp.zeros_like(dk_ref)
            dv_ref[...] = jnp.zeros_like(dv_ref)
            dsink_ref[...] = jnp.zeros_like(dsink_ref)
            if dbias_ref is not None:
                dbias_ref[...] = jnp.zeros_like(dbias_ref)

        k_all = jnp.concatenate([r[...] for r in k_refs], axis=0)
        v_all = jnp.concatenate([r[...] for r in v_refs], axis=0)
        start = 0 if grp.full else _win_start(grp, i)
        dqs, dks, dvs, dsinks = [], [], [], []
        for e in range(2):
            cols = slice(e * HEAD_DIM, (e + 1) * HEAD_DIM)
            q = q_ref[:, cols]
            k = _head_kv(grp, k_all, e, p)
            v = _head_kv(grp, v_all, e, p)
            snk = sink_ref[0:1, e * HEAD_DIM:e * HEAD_DIM + 1]
            pe, m, inv = _softmax_parts(q, k, bias_ref[e if grp.bias_per_head else 0], snk)
            prob = pe * inv
            do = do_ref[:, cols]
            do_b = do.astype(BF16)
            delta = jnp.sum(do * o_ref[:, cols], axis=-1, keepdims=True)
            dp = lax.dot_general(do_b, v, (((1,), (1,)), ((), ())), preferred_element_type=F32)
            ds = prob * (dp - delta)
            ds_b = ds.astype(BF16)
            dqs.append(jnp.dot(ds_b, k, preferred_element_type=F32))
            dks.append(lax.dot_general(ds_b, q, (((0,), (0,)), ((), ())), preferred_element_type=F32))
            dvs.append(lax.dot_general(prob.astype(BF16), do_b, (((0,), (0,)), ((), ())), preferred_element_type=F32))
            dsinks.append(-jnp.sum(jnp.exp(snk - m) * inv * delta, axis=0, keepdims=True))
            if dbias_ref is not None:
                shift = (i * QB - start * grp.kv_rows) // GRID_W
                for rq in range(rows_q):
                    for rk in range(grp.keys // GRID_W):
                        off = jnp.clip(rk - rq + (NA_ROWS - 1) - shift, 0, n_off - 1)
                        dbias_ref[e, off] += ds[rq * GRID_W:(rq + 1) * GRID_W, rk * GRID_W:(rk + 1) * GRID_W]
        dq_ref[...] = jnp.concatenate(dqs, axis=1)
        rows = pl.ds(0, t) if grp.full else pl.ds(pl.multiple_of(start * grp.kv_rows, grp.kv_rows), grp.keys)
        dk_ref[rows, :] += jnp.concatenate(dks, axis=1)
        dv_ref[rows, :] += jnp.concatenate(dvs, axis=1)
        lane = lax.broadcasted_iota(jnp.int32, (8, LANES), 1)
        dsink_ref[...] += jnp.where(lane < HEAD_DIM, dsinks[0], dsinks[1])

    return pl.pallas_call(
        body, name=name, grid=(grp.pairs, NQB),
        in_specs=[q_spec, *k_specs, *v_specs, bias_spec, sink_spec, o_spec, o_spec],
        out_specs=out_specs, out_shape=out_shape,
        compiler_params=_params(("arbitrary", "arbitrary"), 56),
    )(proj, *([proj] * (2 * nkv)), bias, sink, out, d_out)


DILATED_CONFIGS = ((128, 1), (512, 4), (2048, 16))


def _bias_a():
    d = jnp.arange(SEQ)[None, :] - jnp.arange(SEQ)[:, None]
    mult = jnp.zeros((SEQ, SEQ), F32)
    for window, r in DILATED_CONFIGS:
        reach = (window // (2 * r)) * r
        mult = mult + ((d % r == 0) & (jnp.abs(d) <= reach)).astype(F32)
    return jnp.where(mult > 0, jnp.log(jnp.maximum(mult, 1.0)), NEG_INF).reshape(1, NQB, QB, SEQ)


def _bias_b():
    row = jnp.arange(QB)[None, :, None]
    col = jnp.arange(GROUP_B.keys)[None, None, :]
    var = jnp.arange(3)[:, None, None]
    d = col - (GROUP_B.kv_rows * var + row)
    return jnp.where(jnp.abs(d) <= WINDOW_B, 0.0, NEG_INF).astype(F32)[None]


def _offset_onehot():
    c = jnp.arange(GRID_W)[:, None, None]
    c2 = jnp.arange(GRID_W)[None, :, None]
    b = jnp.arange(LANES)[None, None, :]
    return (c2 - c + NA_COLS - 1 == b).astype(BF16).reshape(GRID_W * GRID_W, LANES)


def _split_dot(x, g):
    hi = x.astype(BF16)
    rest = x - hi.astype(F32)
    mid = rest.astype(BF16)
    lo = (rest - mid.astype(F32)).astype(BF16)
    return (jnp.dot(hi, g, preferred_element_type=F32) + jnp.dot(mid, g, preferred_element_type=F32)
            + jnp.dot(lo, g, preferred_element_type=F32))


def _table_mm(x, g, name):
    def body(x_ref, g_ref, o_ref):
        o_ref[...] = _split_dot(x_ref[...], g_ref[...])

    return pl.pallas_call(
        body, name=name, out_shape=jax.ShapeDtypeStruct((x.shape[0], g.shape[1]), F32),
        in_specs=[pl.BlockSpec(memory_space=pltpu.VMEM)] * 2, out_specs=pl.BlockSpec(memory_space=pltpu.VMEM),
        compiler_params=pltpu.CompilerParams(vmem_limit_bytes=32 * MIB),
    )(x, g)


N_OFF = 2 * NA_ROWS - 1
TABLE_ROWS = 152


def _bias_c(rpb):
    table = jnp.zeros((TABLE_ROWS, LANES), F32).at[:N_HEADS_C * N_OFF, :2 * NA_COLS - 1].set(
        rpb.reshape(N_HEADS_C * N_OFF, 2 * NA_COLS - 1))
    tiles = _table_mm(table, _offset_onehot().T, "rpb_tiles")[:N_HEADS_C * N_OFF]
    tiles = tiles.reshape(N_HEADS_C, N_OFF, GRID_W, GRID_W)
    c = jnp.arange(GRID_W)
    col_start = jnp.clip(c - NA_COLS // 2, 0, GRID_W - NA_COLS)
    col_ok = (c[None, :] >= col_start[:, None]) & (c[None, :] < col_start[:, None] + NA_COLS)
    tiles = jnp.where(col_ok, tiles, NEG_INF)
    masked = jnp.full((N_HEADS_C, GRID_W, GRID_W), NEG_INF, F32)
    rows_q = QB // GRID_W
    variants = []
    for var in range(3):
        q_rows = []
        for rq in range(rows_q):
            r_l = rows_q * var + rq
            first = min(max(r_l - NA_ROWS // 2, 0), 3 * rows_q - NA_ROWS)
            q_rows.append(jnp.concatenate(
                [tiles[:, rk - r_l + NA_ROWS - 1] if first <= rk < first + NA_ROWS else masked
                 for rk in range(3 * rows_q)], axis=-1))
        variants.append(jnp.concatenate(q_rows, axis=-2))
    return jnp.stack(variants, axis=1)


def _rpb_grad(d_tiles):
    flat = jnp.zeros((TABLE_ROWS, GRID_W * GRID_W), F32).at[:N_HEADS_C * N_OFF].set(
        d_tiles.reshape(N_HEADS_C * N_OFF, GRID_W * GRID_W))
    out = _table_mm(flat, _offset_onehot(), "rpb_grad")
    return out[:N_HEADS_C * N_OFF, :2 * NA_COLS - 1].reshape(N_HEADS_C, N_OFF, 2 * NA_COLS - 1)


def _sink_lanes(sink):
    return jnp.repeat(sink.astype(F32), HEAD_DIM)[None, :]


def _attention_fwd(proj_r, sink_b, bias_a, bias_b, bias_c):
    no_sink_a = jnp.full((1, WIDTH_A), NEG_INF, F32)
    no_sink_c = jnp.full((1, WIDTH_C), NEG_INF, F32)
    oa = _attn_fwd(GROUP_A, proj_r, bias_a, no_sink_a, "attn_a_fwd")
    ob = _attn_fwd(GROUP_B, proj_r, bias_b, _sink_lanes(sink_b), "attn_b_fwd")
    oc = _attn_fwd(GROUP_C, proj_r, bias_c, no_sink_c, "attn_c_fwd")
    return oa, ob, oc


def _attention_bwd(proj_r, sink_b, bias_a, bias_b, bias_c, outs, d_outs, cos, sin):
    no_sink_a = jnp.full((1, WIDTH_A), NEG_INF, F32)
    no_sink_c = jnp.full((1, WIDTH_C), NEG_INF, F32)
    dqa, dka, dva, _ = _attn_bwd(GROUP_A, proj_r, bias_a, no_sink_a, outs[0], d_outs[0], "attn_a_bwd")
    dqb, dkb, dvb, dsink = _attn_bwd(GROUP_B, proj_r, bias_b, _sink_lanes(sink_b), outs[1], d_outs[1], "attn_b_bwd")
    dqc, dkc, dvc, _, d_tiles = _attn_bwd(GROUP_C, proj_r, bias_c, no_sink_c, outs[2], d_outs[2], "attn_c_bwd")
    d_proj = _rope_bwd((dqa, dka, dva, dqb, dkb, dvb, dqc, dkc, dvc), cos, sin, "rope_bwd")
    d_sink = dsink[:, 0, :].reshape(GROUP_B.pairs, 2, HEAD_DIM)[:, :, 0].reshape(N_HEADS_B)
    return d_proj, d_sink, _rpb_grad(d_tiles)


def _adamw(w, g, m, v, name):
    r, c = w.shape
    rows = r
    for cand in (512, 256, 128, 64, 32, 16, 8):
        if r % cand == 0 and cand * c * 4 <= MIB:
            rows = cand
            break
    spec = pl.BlockSpec((rows, c), lambda i: (i, 0))

    def body(w_ref, g_ref, m_ref, v_ref, d_ref, mo_ref, vo_ref):
        d_ref[...], mo_ref[...], vo_ref[...] = _adamw_step(w_ref[...], g_ref[...], m_ref[...], v_ref[...])

    return pl.pallas_call(
        body, name=name, grid=(r // rows,), in_specs=[spec] * 4, out_specs=[spec] * 3,
        out_shape=[jax.ShapeDtypeStruct((r, c), F32)] * 3, compiler_params=_params(("arbitrary",), 32),
    )(w, g, m, v)


def _adamw_step(w, grad, m, v):
    m_new = ADAM_B1 * m + (1.0 - ADAM_B1) * grad
    v_new = ADAM_B2 * v + (1.0 - ADAM_B2) * jnp.square(grad)
    m_hat = m_new / (1.0 - ADAM_B1 ** ADAM_STEP)
    v_hat = v_new / (1.0 - ADAM_B2 ** ADAM_STEP)
    return -ADAM_LR * (m_hat / (jnp.sqrt(v_hat) + ADAM_EPS) + ADAM_WD * w), m_new, v_new


def _adamw_layer(w, g, m, v, layer, prev, name):
    _, r, c = w.shape
    rows = next(cand for cand in (512, 256, 128, 64, 32, 16, 8) if r % cand == 0 and cand * c * 4 <= MIB)
    spec = pl.BlockSpec((None, rows, c), lambda i: (layer, i, 0))
    n_prev = 0 if prev is None else 4

    def body(w_ref, g_ref, m_ref, v_ref, *rest):
        go_ref, d_ref, mo_ref, vo_ref = rest[n_prev:]
        grad = g_ref[...]
        go_ref[...] = grad
        d_ref[...], mo_ref[...], vo_ref[...] = _adamw_step(w_ref[...], grad, m_ref[...], v_ref[...])

    return pl.pallas_call(
        body, name=name, grid=(r // rows,), in_specs=[spec] * 4 + [ANY_SPEC] * n_prev, out_specs=[spec] * 4,
        out_shape=[jax.ShapeDtypeStruct(w.shape, F32)] * 4,
        input_output_aliases={4 + i: i for i in range(n_prev)}, compiler_params=_params(("arbitrary",), 32),
    )(w, g, m, v, *(prev or ()))


def _layer_fwd(x0, p, weight, tabs):
    h1 = _rmsnorm_fwd(x0, p["ln_attn"], "ln_attn_fwd")
    proj = _mm_nn(h1, weight("w_in", h1), cols=True, tn=256, tk=D_MODEL, out_dtype=F32, name="mm_in")
    proj_r = _rope_fwd(proj, tabs["cos"], tabs["sin"], "rope_fwd")
    outs = _attention_fwd(proj_r, p["sink_b"], tabs["bias_a"], tabs["bias_b"], p["bias_c"])
    mixed = _mix_fwd(*outs, p["mix_gain"], "mix_fwd")
    x1 = _mm_nn(mixed, weight("w_out", mixed), cols=False, tn=256, tk=D_MODEL, out_dtype=F32, name="mm_out",
                residual=x0)
    h2 = _rmsnorm_fwd(x1, p["ln_ffn"], "ln_ffn_fwd")
    u0 = _mm_nn(h2, weight("w_up", h2), cols=True, tn=256, tk=D_MODEL, out_dtype=F32, name="mm_up", out_split=2)
    act = _convgate_fwd(u0, p["conv_w"], p["conv_b"], "convgate_fwd")
    x2 = _mm_nn(act, weight("w_down", act), cols=False, tn=256, tk=D_FF // 2, out_dtype=F32, name="mm_down",
                residual=x1)
    return x2, (x0, h1, proj_r, outs, mixed, x1, h2, u0, act)


def _layer_bwd(dx2, dx2_b, saved, p, big, tabs, send, after):
    x0, h1, proj_r, outs, mixed, x1, h2, u0, act = saved
    d_act = _mm_nt(dx2_b, big["w_down"], cols=False, to=512, tr=D_MODEL, out_dtype=F32, name="nt_down", after=after)
    g_down = _mm_tn(act, dx2_b, tk=D_FF // N_SHARDS, tn=512, shards=-N_SHARDS, name="tn_down")
    du0, d_conv_w, d_conv_b = _convgate_bwd(u0, p["conv_w"], p["conv_b"], d_act, "convgate_bwd")
    dh2 = _mm_nt(du0, big["w_up"], cols=True, to=512, tr=D_FF // 4, out_dtype=F32, name="nt_up")
    g_up = _mm_tn(h2, du0, tk=512, tn=D_FF // 4, shards=N_SHARDS, name="tn_up")
    token = send({"w_down": g_down, "w_up": g_up})
    dx1, dx1_b, d_ln_ffn = _rmsnorm_bwd(x1, p["ln_ffn"], dh2, dx2, "ln_ffn_bwd", after=[token])
    d_mixed = _mm_nt(dx1_b, big["w_out"], cols=False, to=512, tr=D_MODEL, out_dtype=F32, name="nt_out")
    g_out = _mm_tn(mixed, dx1_b, tk=D_MODEL // N_SHARDS, tn=512, shards=-N_SHARDS, name="tn_out")
    *d_outs, d_mix_gain = _mix_bwd(*outs, p["mix_gain"], d_mixed, "mix_bwd")
    d_proj, d_sink, d_rpb = _attention_bwd(proj_r, p["sink_b"], tabs["bias_a"], tabs["bias_b"], p["bias_c"], outs,
                                           d_outs, tabs["cos"], tabs["sin"])
    dh1 = _mm_nt(d_proj, big["w_in"], cols=True, to=512, tr=IN_COLS // N_SHARDS, out_dtype=F32, name="nt_in")
    g_in = _mm_tn(h1, d_proj, tk=512, tn=IN_COLS // N_SHARDS, shards=N_SHARDS, name="tn_in")
    dx0, dx0_b, d_ln_attn = _rmsnorm_bwd(x0, p["ln_attn"], dh1, dx1, "ln_attn_bwd")
    token = send({"w_out": g_out, "w_in": g_in})
    small = {"ln_attn": d_ln_attn, "sink_b": d_sink, "rpb_c": d_rpb, "mix_gain": d_mix_gain, "ln_ffn": d_ln_ffn,
             "conv_w": d_conv_w, "conv_b": d_conv_b}
    return dx0, dx0_b, small, token


HBM_SPEC = pl.BlockSpec(memory_space=pl.ANY)


def _place():
    x, y, c = lax.axis_index("x"), lax.axis_index("y"), lax.axis_index("c")
    chips = ((1 - x, y), (x, 1 - y), (1 - x, 1 - y))
    return x, y, c, chips


def _shard_index(px, py):
    return 2 * px + py


def _remote(src, dst, send_sem, recv_sem, to):
    return pltpu.make_async_remote_copy(src_ref=src, dst_ref=dst, send_sem=send_sem, recv_sem=recv_sem,
                                        device_id=to, device_id_type=MESH)


def _own_slot(w, layer, shard, name):
    _, r, c_dim = w.shape
    rows = r
    for cand in (512, 256, 128):
        if r % cand == 0 and cand * c_dim * 4 <= 2 * MIB:
            rows = cand
            break

    def body(s_ref, w_ref, o_ref):
        o_ref[...] = w_ref[...].astype(BF16)

    return pl.pallas_call(
        body, name=name,
        grid_spec=pltpu.PrefetchScalarGridSpec(
            num_scalar_prefetch=1, grid=(r // rows,),
            in_specs=[pl.BlockSpec((None, rows, c_dim), lambda i, s: (layer, i, 0))],
            out_specs=pl.BlockSpec((None, rows, c_dim), lambda i, s: (s[0], i, 0))),
        out_shape=jax.ShapeDtypeStruct((N_SHARDS, r, c_dim), BF16),
        compiler_params=_params(("arbitrary",), 32),
    )(shard.astype(jnp.int32).reshape(1), w)


def _gather_weights(bufs):
    n = len(bufs)

    def body(*refs):
        outs = refs[n:2 * n]
        send1, recv1, send2, recv2 = refs[2 * n:]
        x, y, c, chips = _place()
        me = _shard_index(x, y)
        sibling = (x, y, 1 - c)
        first = []
        for t in range(n):
            for j, (px, py) in enumerate(chips):
                mine = outs[t].at[c, me]
                cp = _remote(mine, mine, send1.at[t * 3 + j], recv1.at[t * 3 + j], (px, py, c))
                cp.start()
                first.append(cp)
        passed = []
        for t in range(n):
            for j, (px, py) in enumerate(chips):
                slot = outs[t].at[c, _shard_index(px, py)]
                _remote(slot, slot, send1.at[t * 3 + j], recv1.at[t * 3 + j], (px, py, c)).wait_recv()
                cp = _remote(slot, slot, send2.at[t * 3 + j], recv2.at[t * 3 + j], sibling)
                cp.start()
                passed.append(cp)
        for t in range(n):
            for j, (px, py) in enumerate(chips):
                slot = outs[t].at[1 - c, _shard_index(px, py)]
                _remote(slot, slot, send2.at[t * 3 + j], recv2.at[t * 3 + j], sibling).wait_recv()
        for cp in first + passed:
            cp.wait_send()

    return pl.pallas_call(
        body, name="gather_weights", in_specs=[HBM_SPEC] * n, out_specs=[HBM_SPEC] * n,
        out_shape=[jax.ShapeDtypeStruct(b.shape, b.dtype) for b in bufs],
        input_output_aliases={t: t for t in range(n)},
        scratch_shapes=[pltpu.SemaphoreType.DMA((n * 3,))] * 4,
    )(*bufs)


def _pair_exchange(bufs):
    n = len(bufs)

    def body(*refs):
        ins, outs = refs[:n], refs[n:2 * n]
        send, recv = refs[2 * n:]
        x, y, c, _ = _place()
        sibling = (x, y, 1 - c)
        cps = [_remote(ins[t].at[1 - c], outs[t], send.at[t], recv.at[t], sibling) for t in range(n)]
        for cp in cps:
            cp.start()
        for cp in cps:
            cp.wait()

    return pl.pallas_call(
        body, name="pair_exchange", in_specs=[HBM_SPEC] * n, out_specs=[HBM_SPEC] * n,
        out_shape=[jax.ShapeDtypeStruct(b.shape[1:], b.dtype) for b in bufs],
        scratch_shapes=[pltpu.SemaphoreType.DMA((n,))] * 2,
    )(*bufs)


def _pair_sum(own, other, name):
    _, s, r, c_dim = own.shape
    rows = min(r, LANES)
    per = r // rows
    layer = lax.axis_index("c").astype(jnp.int32).reshape(1)

    def body(layer_ref, a_ref, b_ref, o_ref):
        o_ref[...] = (a_ref[...] + b_ref[...]).astype(BF16)

    return pl.pallas_call(
        body, name=name,
        grid_spec=pltpu.PrefetchScalarGridSpec(
            num_scalar_prefetch=1, grid=(s * per,),
            in_specs=[pl.BlockSpec((None, None, rows, c_dim), lambda i, lay: (lay[0], i // per, i % per, 0)),
                      pl.BlockSpec((None, rows, c_dim), lambda i, lay: (i // per, i % per, 0))],
            out_specs=pl.BlockSpec((None, rows, c_dim), lambda i, lay: (i // per, i % per, 0))),
        out_shape=jax.ShapeDtypeStruct((s, r, c_dim), BF16), compiler_params=_params(("arbitrary",), 40),
    )(layer, own, other)


def _chip_exchange(bufs):
    n = len(bufs)

    def body(*refs):
        ins, outs = refs[:n], refs[n:2 * n]
        send, recv = refs[2 * n:]
        x, y, c, chips = _place()
        me = _shard_index(x, y)
        cps = []
        for t in range(n):
            for j, (px, py) in enumerate(chips):
                cp = _remote(ins[t].at[_shard_index(px, py)], outs[t].at[me], send.at[t * 3 + j], recv.at[t * 3 + j],
                             (px, py, c))
                cp.start()
                cps.append(cp)
        for t in range(n):
            for j, (px, py) in enumerate(chips):
                slot = outs[t].at[_shard_index(px, py)]
                _remote(slot, slot, send.at[t * 3 + j], recv.at[t * 3 + j], (px, py, c)).wait_recv()
        for cp in cps:
            cp.wait_send()

    return pl.pallas_call(
        body, name="chip_exchange", in_specs=[HBM_SPEC] * n, out_specs=[HBM_SPEC] * n,
        out_shape=[jax.ShapeDtypeStruct(b.shape, b.dtype) for b in bufs],
        scratch_shapes=[pltpu.SemaphoreType.DMA((n * 3,))] * 2,
    )(*bufs)


HBM_ONLY = pl.BlockSpec(memory_space=pltpu.HBM)
SEM_SPEC = pl.BlockSpec(memory_space=pltpu.SEMAPHORE)
DATAFLOW = pltpu.SideEffectType.DATAFLOW_SIDE_EFFECTING


def _in_hbm(a):
    return pltpu.with_memory_space_constraint(a, pltpu.HBM)


def _chip_exchange_start(bufs, name):
    n = len(bufs)

    def body(*refs):
        ins, lands = refs[:n], refs[n:2 * n]
        send, recv = refs[2 * n], refs[2 * n + 1]
        token = refs[-1]
        x, y, c, chips = _place()
        me = _shard_index(x, y)
        for t in range(n):
            for j, (px, py) in enumerate(chips):
                _remote(ins[t].at[_shard_index(px, py)], lands[t].at[me], send.at[t * 3 + j], recv.at[t * 3 + j],
                        (px, py, c)).start()
        token[...] = jnp.zeros_like(token)

    thru = [pltpu.HBM(b.shape, b.dtype) for b in bufs]
    res = pl.pallas_call(
        body, name=name,
        out_shape=(pltpu.SemaphoreType.DMA((n * 3,)), pltpu.SemaphoreType.DMA((n * 3,)), *thru, *thru,
                   jax.ShapeDtypeStruct((8, LANES), F32)),
        in_specs=[HBM_ONLY] * (2 * n),
        out_specs=(SEM_SPEC, SEM_SPEC, *([HBM_ONLY] * (2 * n)), pl.BlockSpec(memory_space=pltpu.VMEM)),
        input_output_aliases={i: 2 + i for i in range(2 * n)},
        compiler_params=pltpu.CompilerParams(has_side_effects=DATAFLOW),
    )(*[_in_hbm(b) for b in bufs], *[_in_hbm(lax.empty(b.shape, b.dtype)) for b in bufs])
    return res[0], res[1], list(res[2:2 + n]), list(res[2 + n:2 + 2 * n]), res[-1]


def _chip_exchange_wait(send, recv, bufs, lands, after, name):
    n = len(bufs)

    def body(*refs):
        ins, outs = refs[:n], refs[n:2 * n]
        send_ref, recv_ref = refs[2 * n], refs[2 * n + 1]
        x, y, c, chips = _place()
        for t in range(n):
            for j, (px, py) in enumerate(chips):
                sent = ins[t].at[_shard_index(px, py)]
                slot = outs[t].at[_shard_index(px, py)]
                cp = _remote(sent, slot, send_ref.at[t * 3 + j], recv_ref.at[t * 3 + j], (px, py, c))
                cp.wait_send()
                cp.wait_recv()

    thru = [pltpu.HBM(b.shape, b.dtype) for b in bufs]
    res = pl.pallas_call(
        body, name=name, out_shape=(*thru, *thru),
        in_specs=[HBM_ONLY] * (2 * n) + [SEM_SPEC, SEM_SPEC, pl.BlockSpec(memory_space=pl.ANY)],
        out_specs=[HBM_ONLY] * (2 * n),
        input_output_aliases={i: i for i in range(2 * n)},
        compiler_params=pltpu.CompilerParams(has_side_effects=DATAFLOW),
    )(*bufs, *lands, send, recv, after)
    return list(res[:n]), list(res[n:])


def _chip_sum(pair, landed, name):
    s, r, c_dim = pair.shape
    rows = min(r, LANES)
    shard = _shard_index(lax.axis_index("x"), lax.axis_index("y"))
    where = jnp.stack([shard, lax.axis_index("c")]).astype(jnp.int32)

    def landed_spec(k):
        return pl.BlockSpec((None, rows, c_dim), lambda i, w: (jnp.where(w[0] == k, (k + 1) % s, k), i, 0))

    def body(w_ref, own_ref, *rest):
        o_ref = rest[s]
        acc = None
        for k in range(s):
            term = jnp.where(w_ref[0] == k, own_ref[...], rest[k][...]).astype(F32)
            acc = term if acc is None else acc + term
        o_ref[...] = acc

    return pl.pallas_call(
        body, name=name,
        grid_spec=pltpu.PrefetchScalarGridSpec(
            num_scalar_prefetch=1, grid=(r // rows,),
            in_specs=[pl.BlockSpec((None, rows, c_dim), lambda i, w: (w[0], i, 0))] + [landed_spec(k) for k in range(s)],
            out_specs=pl.BlockSpec((None, rows, c_dim), lambda i, w: (w[1], i, 0))),
        out_shape=jax.ShapeDtypeStruct((DEPTH, r, c_dim), F32), compiler_params=_params(("arbitrary",), 40),
    )(where, pair, *([landed] * s))


def _sum_slots(buf, name):
    s, r, c_dim = buf.shape
    rows = min(r, LANES)

    def body(i_ref, o_ref):
        acc = i_ref[0].astype(F32)
        for k in range(1, s):
            acc = acc + i_ref[k].astype(F32)
        o_ref[...] = acc

    return pl.pallas_call(
        body, name=name, grid=(r // rows,),
        in_specs=[pl.BlockSpec((s, rows, c_dim), lambda i: (0, i, 0))],
        out_specs=pl.BlockSpec((rows, c_dim), lambda i: (i, 0)),
        out_shape=jax.ShapeDtypeStruct((r, c_dim), F32), compiler_params=_params(("arbitrary",), 40),
    )(buf)


def _pair_gather(bufs):
    n = len(bufs)

    def body(*refs):
        outs = refs[n:2 * n]
        send, recv = refs[2 * n:]
        x, y, c, _ = _place()
        sibling = (x, y, 1 - c)
        cps = [_remote(outs[t].at[c], outs[t].at[c], send.at[t], recv.at[t], sibling) for t in range(n)]
        for cp in cps:
            cp.start()
        for t in range(n):
            slot = outs[t].at[1 - c]
            _remote(slot, slot, send.at[t], recv.at[t], sibling).wait_recv()
        for cp in cps:
            cp.wait_send()

    return pl.pallas_call(
        body, name="pair_gather", in_specs=[HBM_SPEC] * n, out_specs=[HBM_SPEC] * n,
        out_shape=[jax.ShapeDtypeStruct(b.shape, b.dtype) for b in bufs],
        input_output_aliases={t: t for t in range(n)},
        scratch_shapes=[pltpu.SemaphoreType.DMA((n,))] * 2,
    )(*bufs)


N_DEV = 8


def _all_gather_small(vec, name, after=()):
    n_after = len(after)

    def body(v_ref, *rest):
        o_ref, send, recv, local_sem = rest[n_after:]
        x, y, c, _ = _place()
        me = 4 * x + 2 * y + c
        local = pltpu.make_async_copy(v_ref, o_ref.at[me], local_sem)
        local.start()
        flips = [(fx, fy, fc) for fx in (0, 1) for fy in (0, 1) for fc in (0, 1)][1:]
        peers = [((1 - x) if fx else x, (1 - y) if fy else y, (1 - c) if fc else c) for fx, fy, fc in flips]
        cps = [_remote(v_ref, o_ref.at[me], send.at[k], recv.at[k], peer) for k, peer in enumerate(peers)]
        for cp in cps:
            cp.start()
        for k, (px, py, pc) in enumerate(peers):
            slot = o_ref.at[4 * px + 2 * py + pc]
            _remote(slot, slot, send.at[k], recv.at[k], (px, py, pc)).wait_recv()
        for cp in cps:
            cp.wait_send()
        local.wait()

    return pl.pallas_call(
        body, name=name, in_specs=[HBM_SPEC] * (1 + n_after), out_specs=HBM_SPEC,
        out_shape=jax.ShapeDtypeStruct((N_DEV,) + vec.shape, vec.dtype),
        scratch_shapes=[pltpu.SemaphoreType.DMA((N_DEV - 1,))] * 2 + [pltpu.SemaphoreType.DMA(())],
    )(vec, *after)


def _half(ref, slot, c):
    half = ref.shape[1] // 2
    return ref.at[slot, pl.ds(pl.multiple_of(c * half, 8), half)]


def _gather_start(bufs, after, name):
    n = len(bufs)
    n_after = len(after)

    def body(*refs):
        ins = refs[:n]
        send, recv = refs[n + n_after], refs[n + n_after + 1]
        token = refs[-1]
        x, y, c, chips = _place()
        me = _shard_index(x, y)
        for t in range(n):
            for j, (px, py) in enumerate(chips):
                mine = _half(ins[t], me, c)
                _remote(mine, mine, send.at[t * 3 + j], recv.at[t * 3 + j], (px, py, c)).start()
        token[...] = jnp.zeros_like(token)

    thru = [pltpu.HBM(b.shape, b.dtype) for b in bufs]
    res = pl.pallas_call(
        body, name=name,
        out_shape=(pltpu.SemaphoreType.DMA((n * 3,)), pltpu.SemaphoreType.DMA((n * 3,)), *thru,
                   jax.ShapeDtypeStruct((8, LANES), F32)),
        in_specs=[HBM_ONLY] * n + [ANY_SPEC] * n_after,
        out_specs=(SEM_SPEC, SEM_SPEC, *([HBM_ONLY] * n), pl.BlockSpec(memory_space=pltpu.VMEM)),
        input_output_aliases={i: 2 + i for i in range(n)},
        compiler_params=pltpu.CompilerParams(has_side_effects=DATAFLOW),
    )(*[_in_hbm(b) for b in bufs], *after)
    return res[0], res[1], list(res[2:2 + n]), res[-1]


def _gather_wait(send, recv, bufs, after, name):
    n = len(bufs)

    def body(*refs):
        ins = refs[:n]
        send_ref, recv_ref = refs[n], refs[n + 1]
        x, y, c, chips = _place()
        me = _shard_index(x, y)
        for t in range(n):
            for j, (px, py) in enumerate(chips):
                cp = _remote(_half(ins[t], me, c), _half(ins[t], _shard_index(px, py), c), send_ref.at[t * 3 + j],
                             recv_ref.at[t * 3 + j], (px, py, c))
                cp.wait_send()
                cp.wait_recv()

    res = pl.pallas_call(
        body, name=name, out_shape=tuple(pltpu.HBM(b.shape, b.dtype) for b in bufs),
        in_specs=[HBM_ONLY] * n + [SEM_SPEC, SEM_SPEC] + [ANY_SPEC] * len(after), out_specs=[HBM_ONLY] * n,
        input_output_aliases={i: i for i in range(n)},
        compiler_params=pltpu.CompilerParams(has_side_effects=DATAFLOW),
    )(*bufs, send, recv, *after)
    return list(res)


def _gather_forward(bufs, name):
    n = len(bufs)

    def body(*refs):
        outs = refs[n:2 * n]
        send, recv = refs[2 * n:]
        x, y, c, chips = _place()
        sibling = (x, y, 1 - c)
        cps = []
        for t in range(n):
            for j, (px, py) in enumerate(chips):
                got = _half(outs[t], _shard_index(px, py), c)
                cp = _remote(got, got, send.at[t * 3 + j], recv.at[t * 3 + j], sibling)
                cp.start()
                cps.append(cp)
        for t in range(n):
            for j, (px, py) in enumerate(chips):
                theirs = _half(outs[t], _shard_index(px, py), 1 - c)
                _remote(theirs, theirs, send.at[t * 3 + j], recv.at[t * 3 + j], sibling).wait_recv()
        for cp in cps:
            cp.wait_send()

    return pl.pallas_call(
        body, name=name, in_specs=[HBM_SPEC] * n, out_specs=[HBM_SPEC] * n,
        out_shape=[jax.ShapeDtypeStruct(b.shape, b.dtype) for b in bufs],
        input_output_aliases={t: t for t in range(n)},
        scratch_shapes=[pltpu.SemaphoreType.DMA((n * 3,))] * 2,
    )(*bufs)


def _half_exchange(grads, name):
    n = len(grads)

    def body(*refs):
        ins, outs = refs[:n], refs[n:2 * n]
        send, recv = refs[2 * n:]
        x, y, c, _ = _place()
        sibling = (x, y, 1 - c)
        cps = []
        for t in range(n):
            half = ins[t].shape[1] // 2
            theirs = ins[t].at[:, pl.ds(pl.multiple_of((1 - c) * half, 8), half)]
            cps.append(_remote(theirs, outs[t], send.at[t], recv.at[t], sibling))
        for cp in cps:
            cp.start()
        for cp in cps:
            cp.wait()

    return pl.pallas_call(
        body, name=name, in_specs=[HBM_SPEC] * n, out_specs=[HBM_SPEC] * n,
        out_shape=[jax.ShapeDtypeStruct((g.shape[0], g.shape[1] // 2, g.shape[2]), g.dtype) for g in grads],
        scratch_shapes=[pltpu.SemaphoreType.DMA((n,))] * 2,
    )(*grads)


def _half_rows(half, c_dim):
    for cand in (512, 256, 128, 64):
        if half % cand == 0 and cand * c_dim * 2 <= MIB:
            return cand
    raise ValueError((half, c_dim))


def _core_index():
    return lax.axis_index("c").astype(jnp.int32).reshape(1)


def _half_sum(own, other, name):
    s, r, c_dim = own.shape
    rows = _half_rows(r // 2, c_dim)
    per = r // 2 // rows

    def body(c_ref, a_ref, b_ref, o_ref):
        o_ref[...] = (a_ref[...].astype(F32) + b_ref[...].astype(F32)).astype(BF16)

    return pl.pallas_call(
        body, name=name,
        grid_spec=pltpu.PrefetchScalarGridSpec(
            num_scalar_prefetch=1, grid=(s, per),
            in_specs=[pl.BlockSpec((None, rows, c_dim), lambda k, i, c: (k, c[0] * per + i, 0)),
                      pl.BlockSpec((None, rows, c_dim), lambda k, i, c: (k, i, 0))],
            out_specs=pl.BlockSpec((None, rows, c_dim), lambda k, i, c: (k, i, 0))),
        out_shape=jax.ShapeDtypeStruct((s, r // 2, c_dim), BF16), compiler_params=_params(("arbitrary", "arbitrary"), 32),
    )(_core_index(), own, other)


def _reduce_start(pairs, name):
    n = len(pairs)

    def body(*refs):
        ins, lands = refs[:n], refs[n:2 * n]
        send, recv = refs[2 * n], refs[2 * n + 1]
        token = refs[-1]
        x, y, c, chips = _place()
        me = _shard_index(x, y)
        for t in range(n):
            for j, (px, py) in enumerate(chips):
                _remote(ins[t].at[_shard_index(px, py)], lands[t].at[me], send.at[t * 3 + j], recv.at[t * 3 + j],
                        (px, py, c)).start()
        token[...] = jnp.zeros_like(token)

    thru = [pltpu.HBM(b.shape, b.dtype) for b in pairs]
    res = pl.pallas_call(
        body, name=name,
        out_shape=(pltpu.SemaphoreType.DMA((n * 3,)), pltpu.SemaphoreType.DMA((n * 3,)), *thru, *thru,
                   jax.ShapeDtypeStruct((8, LANES), F32)),
        in_specs=[HBM_ONLY] * (2 * n),
        out_specs=(SEM_SPEC, SEM_SPEC, *([HBM_ONLY] * (2 * n)), pl.BlockSpec(memory_space=pltpu.VMEM)),
        input_output_aliases={i: 2 + i for i in range(2 * n)},
        compiler_params=pltpu.CompilerParams(has_side_effects=DATAFLOW),
    )(*[_in_hbm(b) for b in pairs], *[_in_hbm(lax.empty(b.shape, b.dtype)) for b in pairs])
    return res[0], res[1], list(res[2:2 + n]), list(res[2 + n:2 + 2 * n]), res[-1]


def _reduce_wait(send, recv, pairs, lands, after, name):
    n = len(pairs)

    def body(*refs):
        ins, got = refs[:n], refs[n:2 * n]
        send_ref, recv_ref = refs[2 * n], refs[2 * n + 1]
        x, y, c, chips = _place()
        for t in range(n):
            for j, (px, py) in enumerate(chips):
                s = _shard_index(px, py)
                cp = _remote(ins[t].at[s], got[t].at[s], send_ref.at[t * 3 + j], recv_ref.at[t * 3 + j], (px, py, c))
                cp.wait_send()
                cp.wait_recv()

    thru = [pltpu.HBM(b.shape, b.dtype) for b in pairs]
    res = pl.pallas_call(
        body, name=name, out_shape=(*thru, *thru),
        in_specs=[HBM_ONLY] * (2 * n) + [SEM_SPEC, SEM_SPEC] + [ANY_SPEC] * len(after),
        out_specs=[HBM_ONLY] * (2 * n),
        input_output_aliases={i: i for i in range(2 * n)},
        compiler_params=pltpu.CompilerParams(has_side_effects=DATAFLOW),
    )(*pairs, *lands, send, recv, *after)
    return list(res[:n]), list(res[n:])


def _reduce_sum(pair, landed, layer, prev, name):
    s, half, c_dim = pair.shape
    rows = _half_rows(half, c_dim)
    per = half // rows
    shard = _shard_index(lax.axis_index("x"), lax.axis_index("y"))
    where = jnp.stack([shard, lax.axis_index("c")]).astype(jnp.int32)

    def landed_spec(k):
        return pl.BlockSpec((None, rows, c_dim), lambda i, w: (jnp.where(w[0] == k, (k + 1) % s, k), i, 0))

    def body(w_ref, own_ref, *rest):
        o_ref = rest[-1]
        acc = None
        for k in range(s):
            term = jnp.where(w_ref[0] == k, own_ref[...], rest[k][...]).astype(F32)
            acc = term if acc is None else acc + term
        o_ref[...] = acc

    args = [where, pair] + [landed] * s
    in_specs = [pl.BlockSpec((None, rows, c_dim), lambda i, w: (w[0], i, 0))] + [landed_spec(k) for k in range(s)]
    aliases = {}
    if prev is not None:
        args.append(prev)
        in_specs.append(ANY_SPEC)
        aliases = {len(args) - 1: 0}
    return pl.pallas_call(
        body, name=name,
        grid_spec=pltpu.PrefetchScalarGridSpec(
            num_scalar_prefetch=1, grid=(per,), in_specs=in_specs,
            out_specs=pl.BlockSpec((None, rows, c_dim), lambda i, w: (layer, w[1] * per + i, 0))),
        out_shape=jax.ShapeDtypeStruct((DEPTH, 2 * half, c_dim), F32), input_output_aliases=aliases,
        compiler_params=_params(("arbitrary",), 40),
    )(*args)


def _half_gather(bufs, layer, name):
    n = len(bufs)

    def body(*refs):
        outs = refs[n:2 * n]
        send, recv = refs[2 * n:]
        x, y, c, _ = _place()
        sibling = (x, y, 1 - c)

        def rows(t, which):
            half = outs[t].shape[1] // 2
            return outs[t].at[layer, pl.ds(pl.multiple_of(which * half, 8), half)]

        cps = [_remote(rows(t, c), rows(t, c), send.at[t], recv.at[t], sibling) for t in range(n)]
        for cp in cps:
            cp.start()
        for t in range(n):
            _remote(rows(t, 1 - c), rows(t, 1 - c), send.at[t], recv.at[t], sibling).wait_recv()
        for cp in cps:
            cp.wait_send()

    return pl.pallas_call(
        body, name=name, in_specs=[HBM_SPEC] * n, out_specs=[HBM_SPEC] * n,
        out_shape=[jax.ShapeDtypeStruct(b.shape, b.dtype) for b in bufs],
        input_output_aliases={t: t for t in range(n)},
        scratch_shapes=[pltpu.SemaphoreType.DMA((n,))] * 2,
    )(*bufs)


WEIGHT_NAMES = ("ln_attn", "w_in", "sink_b", "rpb_c", "mix_gain", "w_out", "ln_ffn", "w_up", "conv_w", "conv_b",
                "w_down", "ln_final")
BIG_NAMES = ("w_in", "w_out", "w_up", "w_down")
REPLICATED_NAMES = ("ln_attn", "sink_b", "rpb_c", "mix_gain", "ln_ffn", "conv_b", "ln_final")
PACK_TILE = 8 * LANES


def _pack(arrays, row_multiple):
    pieces = []
    for a in arrays:
        flat = a.reshape(-1)
        pieces.append(jnp.pad(flat, (0, (-flat.shape[0]) % PACK_TILE)))
    flat = jnp.concatenate(pieces)
    flat = jnp.pad(flat, (0, (-flat.shape[0]) % (row_multiple * LANES)))
    return flat.reshape(-1, LANES)


def _unpack(packed, shapes):
    flat = packed.reshape(-1)
    out, off = [], 0
    for shape in shapes:
        size = math.prod(shape)
        out.append(flat[off:off + size].reshape(shape))
        off += size + (-size) % PACK_TILE
    return out


def kernel(x, ln_attn, w_in, sink_b, rpb_c, mix_gain, w_out, ln_ffn, w_up, conv_w, conv_b, w_down, ln_final, loss_target, m_ln_attn, m_w_in, m_sink_b, m_rpb_c, m_mix_gain, m_w_out, m_ln_ffn, m_w_up, m_conv_w, m_conv_b, m_w_down, m_ln_final, v_ln_attn, v_w_in, v_sink_b, v_rpb_c, v_mix_gain, v_w_out, v_ln_ffn, v_w_up, v_conv_w, v_conv_b, v_w_down, v_ln_final):
    w = dict(ln_attn=ln_attn, w_in=w_in, sink_b=sink_b, rpb_c=rpb_c, mix_gain=mix_gain, w_out=w_out, ln_ffn=ln_ffn,
             w_up=w_up, conv_w=conv_w, conv_b=conv_b, w_down=w_down, ln_final=ln_final)
    m = dict(ln_attn=m_ln_attn, w_in=m_w_in, sink_b=m_sink_b, rpb_c=m_rpb_c, mix_gain=m_mix_gain, w_out=m_w_out,
             ln_ffn=m_ln_ffn, w_up=m_w_up, conv_w=m_conv_w, conv_b=m_conv_b, w_down=m_w_down, ln_final=m_ln_final)
    v = dict(ln_attn=v_ln_attn, w_in=v_w_in, sink_b=v_sink_b, rpb_c=v_rpb_c, mix_gain=v_mix_gain, w_out=v_w_out,
             ln_ffn=v_ln_ffn, w_up=v_w_up, conv_w=v_conv_w, conv_b=v_conv_b, w_down=v_w_down, ln_final=v_ln_final)
    shard = _shard_index(lax.axis_index("x"), lax.axis_index("y"))
    up_cols = w_up.shape[2]

    conv_slots = _all_gather_small(_pack([conv_w], 8), "gather_conv_w")
    conv_all = conv_slots[0::2].reshape(N_SHARDS, -1)[:, :conv_w.size].reshape((N_SHARDS,) + conv_w.shape)

    arrivals = []
    group_of = {}
    tokens = []
    for l, names in ((0, ("w_in",)), (0, ("w_out", "w_up", "w_down")), (1, BIG_NAMES)):
        bufs = [_own_slot(w[k], l, shard, "own_" + k) for k in names]
        send, recv, bufs, token = _gather_start(bufs, tokens[-1:] or [conv_slots], "gather_start_%d" % len(arrivals))
        tokens.append(token)
        for k in names:
            group_of[l, k] = len(arrivals)
        arrivals.append({"names": names, "send": send, "recv": recv, "bufs": bufs, "done": None})

    def gathered(l, name, after):
        idx = group_of[l, name]
        group = arrivals[idx]
        if group["done"] is None:
            bufs = _gather_wait(group["send"], group["recv"], group["bufs"], list(after) + tokens[-1:],
                                "gather_wait_%d" % idx)
            group["done"] = dict(zip(group["names"], _gather_forward(bufs, "gather_forward_%d" % idx)))
        buf = group["done"][name]
        return buf.reshape(1, -1, buf.shape[2]) if name in ("w_out", "w_down") else buf

    cos, sin = _rope_tables(SEQ)
    tabs = {"cos": cos, "sin": sin, "bias_a": _bias_a(), "bias_b": _bias_b()}
    layers = []
    for l in range(DEPTH):
        conv_w_l = conv_all[:, l].reshape(2, N_SHARDS // 2, 3, up_cols).transpose(0, 2, 1, 3).reshape(2, 3, D_FF)
        layers.append({"ln_attn": ln_attn[l][None], "sink_b": sink_b[l], "bias_c": _bias_c(rpb_c[l]),
                       "mix_gain": mix_gain[l][None], "ln_ffn": ln_ffn[l][None], "conv_w": conv_w_l,
                       "conv_b": conv_b[l].reshape(2, 1, D_FF)})

    act = x[0]
    saved = []
    for l in range(DEPTH):
        act, keep = _layer_fwd(act, layers[l], lambda name, after, l=l: gathered(l, name, [after]), tabs)
        saved.append(keep)
    loss_part, dx, dx_b, d_ln_final = _loss_head(act, ln_final[None], loss_target[0], "loss_head")
    loss = lax.psum(loss_part[0, 0], ("x", "y", "c"))

    reductions = []

    def sender(l):
        def send(partial):
            idx = len(reductions)
            names = tuple(partial)
            mine = [partial[k] for k in names]
            theirs = _half_exchange(mine, "half_exchange_%d" % idx)
            pairs = [_half_sum(a, b, "half_sum_" + k) for k, a, b in zip(names, mine, theirs)]
            send_sem, recv_sem, pairs, lands, token = _reduce_start(pairs, "reduce_start_%d" % idx)
            reductions.append({"layer": l, "names": names, "send": send_sem, "recv": recv_sem, "pairs": pairs,
                               "lands": lands})
            return token
        return send

    small = [None] * DEPTH
    after = []
    for l in reversed(range(DEPTH)):
        big = {k: gathered(l, k, []) for k in BIG_NAMES}
        dx, dx_b, small[l], token = _layer_bwd(dx, dx_b, saved[l], layers[l], big, tabs, sender(l), after)
        after = [token]

    grads, delta, new_m, new_v = {}, {}, {}, {}
    reduced = {}
    updated = dict.fromkeys(BIG_NAMES)
    for l in reversed(range(DEPTH)):
        for idx, group in enumerate(reductions):
            if group["layer"] != l:
                continue
            pairs, lands = _reduce_wait(group["send"], group["recv"], group["pairs"], group["lands"], after,
                                        "reduce_wait_%d" % idx)
            for k, pair, landed in zip(group["names"], pairs, lands):
                reduced[k] = _reduce_sum(pair, landed, l, reduced.get(k), "reduce_sum_" + k)
            after = [reduced[group["names"][-1]]]
        reduced = dict(zip(BIG_NAMES, _half_gather([reduced[k] for k in BIG_NAMES], l, "half_gather_%d" % l)))
        for k in BIG_NAMES:
            updated[k] = _adamw_layer(w[k], reduced[k], m[k], v[k], l, updated[k], "adamw_" + k)
        after = [updated[k][0] for k in BIG_NAMES]
    for k in BIG_NAMES:
        grads[k], delta[k], new_m[k], new_v[k] = updated[k]

    stacked = {k: jnp.stack([small[l][k] for l in range(DEPTH)]) for k in small[0]}
    part = {"ln_attn": stacked["ln_attn"][:, 0], "sink_b": stacked["sink_b"], "rpb_c": stacked["rpb_c"],
            "mix_gain": stacked["mix_gain"][:, 0], "ln_ffn": stacked["ln_ffn"][:, 0],
            "conv_b": stacked["conv_b"].reshape(DEPTH, 2 * D_FF), "ln_final": d_ln_final[0],
            "conv_w": stacked["conv_w"].transpose(0, 2, 1, 3).reshape(DEPTH, 3, 2 * D_FF)}
    names = REPLICATED_NAMES + ("conv_w",)
    total = _sum_slots(_all_gather_small(_pack([part[k] for k in names], 256), "gather_small_grads", after),
                       "sum_small")
    for k, g in zip(names, _unpack(total, [part[k].shape for k in names])):
        grads[k] = g
    grads["conv_w"] = lax.dynamic_slice_in_dim(grads["conv_w"], shard * up_cols, up_cols, axis=2)

    flat = (DEPTH * 3, up_cols)
    res = _adamw(conv_w.reshape(flat), grads["conv_w"].reshape(flat), m["conv_w"].reshape(flat),
                 v["conv_w"].reshape(flat), "adamw_conv_w")
    delta["conv_w"], new_m["conv_w"], new_v["conv_w"] = (r.reshape(conv_w.shape) for r in res)
    shapes = [w[k].shape for k in REPLICATED_NAMES]
    packed = [_pack([d[k] for k in REPLICATED_NAMES], 128) for d in (w, grads, m, v)]
    for d, res in zip((delta, new_m, new_v), _adamw(*packed, "adamw_small")):
        for k, r in zip(REPLICATED_NAMES, _unpack(res, shapes)):
            d[k] = r

    return (loss, dx[None], *[grads[k] for k in WEIGHT_NAMES], *[delta[k] for k in WEIGHT_NAMES],
            *[new_m[k] for k in WEIGHT_NAMES], *[new_v[k] for k in WEIGHT_NAMES])
```

```python
import functools
import math

import jax
import jax.numpy as jnp
from jax import lax
from jax.experimental import pallas as pl
from jax.experimental.pallas import tpu as pltpu

F32 = jnp.float32
BF16 = jnp.bfloat16
MESH = pl.DeviceIdType.MESH

D_MODEL = 2048
SEQ = 2048
DEPTH = 2
HEAD_DIM = 64
N_HEADS_A = 12
N_HEADS_B = 10
N_KV_B = 2
N_HEADS_C = 10
WINDOW_B = 128
GRID_W = 64
NA_ROWS = 8
NA_COLS = 16
WIDTH_A = N_HEADS_A * HEAD_DIM
WIDTH_B = N_HEADS_B * HEAD_DIM
WIDTH_C = N_HEADS_C * HEAD_DIM
IN_COLS = 5120
D_FF = 5632
ROPE_THETA = 10000.0
EPS = 1e-6
NEG_INF = -1e30
N_SHARDS = 4

ADAM_LR = 0.001
ADAM_B1 = 0.9
ADAM_B2 = 0.999
ADAM_EPS = 1e-08
ADAM_WD = 0.01
ADAM_STEP = 10

LANES = 128
QB = 256
NQB = SEQ // QB
ROWS = 256
MIB = 2 ** 20

A_BLK = (0, 6, 12)
B_BLK = (18, 23, 24)
C_BLK = (25, 30, 35)
ROPE_BLKS = tuple(range(0, 12)) + tuple(range(18, 24))
QSCALE_BLKS = tuple(range(0, 6)) + tuple(range(18, 23)) + tuple(range(25, 30))
N_PBLK = IN_COLS // LANES


def _params(sem, vmem_mib):
    return pltpu.CompilerParams(dimension_semantics=sem, vmem_limit_bytes=vmem_mib * MIB)


def _weight_spec(w, cols, t_in, t_out, transposed):
    s, r, c = w.shape
    if cols:
        per = c // t_out
        k_dim, n = r, s * c
        if transposed:
            index = lambda j, rr: (rr // per, j, rr % per)
        else:
            index = lambda j, kk: (j // per, kk, j % per)
    else:
        per = r // t_in
        k_dim, n = s * r, c
        if transposed:
            index = lambda j, rr: (j // per, j % per, rr)
        else:
            index = lambda j, kk: (kk // per, kk % per, j)
    return pl.BlockSpec((None, t_in, t_out), index), k_dim, n


def _mm_nn(a, w, *, cols, tn, tk, out_dtype, name, residual=None, out_split=1):
    m, k_dim = a.shape
    w_spec, k_w, n = _weight_spec(w, cols, tk, tn, False)
    assert k_w == k_dim
    nj, nk = n // tn, k_dim // tk
    in_specs = [pl.BlockSpec((m, tk), lambda j, k: (0, k)), w_spec]
    args = [a, w]
    if residual is not None:
        in_specs.append(pl.BlockSpec((m, tn), lambda j, k: (0, j)))
        args.append(residual)
    if out_split > 1:
        per_o = n // out_split // tn
        out_spec = pl.BlockSpec((None, m, tn), lambda j, k: (j // per_o, 0, j % per_o))
        out_shape = jax.ShapeDtypeStruct((out_split, m, n // out_split), out_dtype)
    else:
        out_spec = pl.BlockSpec((m, tn), lambda j, k: (0, j))
        out_shape = jax.ShapeDtypeStruct((m, n), out_dtype)

    def body(*refs):
        a_ref, w_ref = refs[0], refs[1]
        r_ref = refs[2] if residual is not None else None
        o_ref = refs[3] if residual is not None else refs[2]

        def finish(val):
            if r_ref is not None:
                val = r_ref[...] + val
            o_ref[...] = val.astype(o_ref.dtype)

        part = jnp.dot(a_ref[...], w_ref[...], preferred_element_type=F32)
        if nk == 1:
            finish(part)
        else:
            acc = refs[-1]
            kk = pl.program_id(1)

            @pl.when(kk == 0)
            def _():
                acc[...] = part

            @pl.when(kk > 0)
            def _():
                acc[...] += part

            @pl.when(kk == nk - 1)
            def _():
                finish(acc[...])

    return pl.pallas_call(
        body, name=name, grid=(nj, nk), in_specs=in_specs, out_specs=out_spec, out_shape=out_shape,
        scratch_shapes=[pltpu.VMEM((m, tn), F32)] if nk > 1 else [],
        compiler_params=_params(("arbitrary", "arbitrary"), 56),
    )(*args)


ANY_SPEC = pl.BlockSpec(memory_space=pl.ANY)


def _mm_nt(dy, w, *, cols, to, tr, out_dtype, name, after=()):
    if dy.ndim == 3:
        m = dy.shape[1]
        n = dy.shape[0] * dy.shape[2]
        per_d = dy.shape[2] // tr
        dy_spec = pl.BlockSpec((None, m, tr), lambda j, r: (r // per_d, 0, r % per_d))
    else:
        m, n = dy.shape
        dy_spec = pl.BlockSpec((m, tr), lambda j, r: (0, r))
    w_spec, k_dim, n_w = _weight_spec(w, cols, to, tr, True)
    assert n_w == n
    nj, nr = k_dim // to, n // tr

    n_after = len(after)

    def body(dy_ref, w_ref, *rest):
        o_ref = rest[n_after]
        part = lax.dot_general(dy_ref[...], w_ref[...], (((1,), (1,)), ((), ())), preferred_element_type=F32)
        if nr == 1:
            o_ref[...] = part.astype(o_ref.dtype)
        else:
            acc = rest[n_after + 1]
            rr = pl.program_id(1)

            @pl.when(rr == 0)
            def _():
                acc[...] = part

            @pl.when(rr > 0)
            def _():
                acc[...] += part

            @pl.when(rr == nr - 1)
            def _():
                o_ref[...] = acc[...].astype(o_ref.dtype)

    return pl.pallas_call(
        body, name=name, grid=(nj, nr), in_specs=[dy_spec, w_spec] + [ANY_SPEC] * n_after,
        out_specs=pl.BlockSpec((m, to), lambda j, r: (0, j)),
        out_shape=jax.ShapeDtypeStruct((m, k_dim), out_dtype),
        scratch_shapes=[pltpu.VMEM((m, to), F32)] if nr > 1 else [],
        compiler_params=_params(("arbitrary", "arbitrary"), 56),
    )(dy, w, *after)


def _mm_tn(x, dy, *, tk, tn, shards, name):
    m, k_dim = x.shape
    if dy.ndim == 3:
        n = dy.shape[0] * dy.shape[2]
        per_d = dy.shape[2] // tn
        dy_spec = pl.BlockSpec((None, m, tn), lambda i, j: (j // per_d, 0, j % per_d))
    else:
        n = dy.shape[1]
        dy_spec = pl.BlockSpec((m, tn), lambda i, j: (0, j))
    if shards > 0:
        per = n // shards // tn
        out_shape = jax.ShapeDtypeStruct((shards, k_dim, n // shards), BF16)
        out_spec = pl.BlockSpec((None, tk, tn), lambda i, j: (j // per, i, j % per))
    else:
        s = -shards
        per = k_dim // s // tk
        out_shape = jax.ShapeDtypeStruct((s, k_dim // s, n), BF16)
        out_spec = pl.BlockSpec((None, tk, tn), lambda i, j: (i // per, i % per, j))

    def body(x_ref, dy_ref, o_ref):
        o_ref[...] = lax.dot_general(x_ref[...], dy_ref[...], (((0,), (0,)), ((), ())),
                                     preferred_element_type=F32).astype(BF16)

    return pl.pallas_call(
        body, name=name, grid=(k_dim // tk, n // tn),
        in_specs=[pl.BlockSpec((m, tk), lambda i, j: (0, i)), dy_spec], out_specs=out_spec, out_shape=out_shape,
        compiler_params=_params(("arbitrary", "arbitrary"), 56),
    )(x, dy)


def _row_spec(width, rows=ROWS):
    return pl.BlockSpec((rows, width), lambda i: (i, 0))


def _vec_spec(width):
    return pl.BlockSpec((1, width), lambda i: (0, 0))


def _rms_stats(x):
    r = lax.rsqrt(jnp.mean(x * x, axis=-1, keepdims=True) + EPS)
    return r, x * r


def _rmsnorm_fwd(x, gain, name):
    t, d = x.shape

    def body(x_ref, g_ref, o_ref):
        _, n = _rms_stats(x_ref[...])
        o_ref[...] = (n * g_ref[...]).astype(BF16)

    return pl.pallas_call(
        body, name=name, grid=(t // ROWS,), in_specs=[_row_spec(d), _vec_spec(d)], out_specs=_row_spec(d),
        out_shape=jax.ShapeDtypeStruct((t, d), BF16), compiler_params=_params(("arbitrary",), 32),
    )(x, gain)


def _rmsnorm_bwd(x, gain, dh, dres, name, after=()):
    t, d = x.shape
    n_after = len(after)

    def body(x_ref, g_ref, dh_ref, dres_ref, *rest):
        dx_ref, dxb_ref, dg_ref = rest[n_after:]
        r, n = _rms_stats(x_ref[...])
        dh_v = dh_ref[...]
        dn = dh_v * g_ref[...]
        dx = dres_ref[...] + r * (dn - n * jnp.mean(dn * n, axis=-1, keepdims=True))
        dx_ref[...] = dx
        dxb_ref[...] = dx.astype(BF16)
        part = jnp.sum(dh_v * n, axis=0, keepdims=True)

        @pl.when(pl.program_id(0) == 0)
        def _():
            dg_ref[...] = part

        @pl.when(pl.program_id(0) > 0)
        def _():
            dg_ref[...] += part

    return pl.pallas_call(
        body, name=name, grid=(t // ROWS,),
        in_specs=[_row_spec(d), _vec_spec(d), _row_spec(d), _row_spec(d)] + [ANY_SPEC] * n_after,
        out_specs=[_row_spec(d), _row_spec(d), _vec_spec(d)],
        out_shape=[jax.ShapeDtypeStruct((t, d), F32), jax.ShapeDtypeStruct((t, d), BF16),
                   jax.ShapeDtypeStruct((1, d), F32)],
        compiler_params=_params(("arbitrary",), 40),
    )(x, gain, dh, dres, *after)


def _loss_head(x, gain, target, name):
    t, d = x.shape

    def body(x_ref, g_ref, t_ref, loss_ref, dx_ref, dxb_ref, dg_ref):
        r, n = _rms_stats(x_ref[...])
        g = g_ref[...]
        err = n * g - t_ref[...]
        dy = err * (1.0 / d)
        dn = dy * g
        dx = r * (dn - n * jnp.mean(dn * n, axis=-1, keepdims=True))
        dx_ref[...] = dx
        dxb_ref[...] = dx.astype(BF16)
        part = jnp.sum(dy * n, axis=0, keepdims=True)
        lpart = jnp.zeros((8, LANES), F32) + 0.5 * jnp.sum(jnp.mean(err * err, axis=-1, keepdims=True))

        @pl.when(pl.program_id(0) == 0)
        def _():
            dg_ref[...] = part
            loss_ref[...] = lpart

        @pl.when(pl.program_id(0) > 0)
        def _():
            dg_ref[...] += part
            loss_ref[...] += lpart

    return pl.pallas_call(
        body, name=name, grid=(t // ROWS,),
        in_specs=[_row_spec(d), _vec_spec(d), _row_spec(d)],
        out_specs=[pl.BlockSpec((8, LANES), lambda i: (0, 0)), _row_spec(d), _row_spec(d), _vec_spec(d)],
        out_shape=[jax.ShapeDtypeStruct((8, LANES), F32), jax.ShapeDtypeStruct((t, d), F32),
                   jax.ShapeDtypeStruct((t, d), BF16), jax.ShapeDtypeStruct((1, d), F32)],
        compiler_params=_params(("arbitrary",), 40),
    )(x, gain, target)


def _swap_halves(x):
    lane = lax.broadcasted_iota(jnp.int32, x.shape, 1)
    return jnp.where((lane % HEAD_DIM) < HEAD_DIM // 2, pltpu.roll(x, LANES - HEAD_DIM // 2, 1),
                     pltpu.roll(x, HEAD_DIM // 2, 1))


def _rope_tables(t):
    inv_freq = ROPE_THETA ** (-jnp.arange(0, HEAD_DIM, 2, dtype=F32) / HEAD_DIM)
    ang = jnp.arange(t, dtype=F32)[:, None] * inv_freq[None, :]
    cos = jnp.tile(jnp.cos(ang), (1, LANES // (HEAD_DIM // 2)))
    sin = jnp.tile(jnp.sin(ang), (1, LANES // (HEAD_DIM // 2)))
    lane = jnp.arange(LANES)[None, :]
    return cos, jnp.where((lane % HEAD_DIM) < HEAD_DIM // 2, -sin, sin)


def _rope_fwd(proj, cos, sin, name):
    t = proj.shape[0]
    scale = HEAD_DIM ** -0.5

    def body(p_ref, c_ref, s_ref, o_ref):
        cos_v, sin_v = c_ref[...], s_ref[...]
        for b in range(N_PBLK):
            cols = slice(b * LANES, (b + 1) * LANES)
            v = p_ref[:, cols]
            if b in ROPE_BLKS:
                v = v * cos_v + _swap_halves(v) * sin_v
            if b in QSCALE_BLKS:
                v = v * scale
            o_ref[:, cols] = v.astype(BF16)

    return pl.pallas_call(
        body, name=name, grid=(t // ROWS,),
        in_specs=[_row_spec(IN_COLS), _row_spec(LANES), _row_spec(LANES)], out_specs=_row_spec(IN_COLS),
        out_shape=jax.ShapeDtypeStruct((t, IN_COLS), BF16), compiler_params=_params(("arbitrary",), 40),
    )(proj, cos, sin)


def _rope_bwd(grads, cos, sin, name):
    t = grads[0].shape[0]
    scale = HEAD_DIM ** -0.5
    group = N_HEADS_B // N_KV_B

    def body(*refs):
        c_ref, s_ref, o_ref = refs[9], refs[10], refs[11]
        cos_v, sin_v = c_ref[...], s_ref[...]

        def kv_sum(ref):
            parts = []
            for g in range(N_KV_B):
                acc = ref[:, g * group * HEAD_DIM:(g * group + 1) * HEAD_DIM]
                for h in range(g * group + 1, (g + 1) * group):
                    acc = acc + ref[:, h * HEAD_DIM:(h + 1) * HEAD_DIM]
                parts.append(acc)
            return jnp.concatenate(parts, axis=1)

        def emit(b, v):
            if b in ROPE_BLKS:
                v = v * cos_v - _swap_halves(v) * sin_v
            if b in QSCALE_BLKS:
                v = v * scale
            o_ref[:, b * LANES:(b + 1) * LANES] = v.astype(BF16)

        starts = (A_BLK[0], A_BLK[1], A_BLK[2], B_BLK[0], None, None, C_BLK[0], C_BLK[1], C_BLK[2])
        for idx, start in enumerate(starts):
            if start is None:
                continue
            for j in range(refs[idx].shape[1] // LANES):
                emit(start + j, refs[idx][:, j * LANES:(j + 1) * LANES])
        emit(B_BLK[1], kv_sum(refs[4]))
        emit(B_BLK[2], kv_sum(refs[5]))

    return pl.pallas_call(
        body, name=name, grid=(t // ROWS,),
        in_specs=[_row_spec(g.shape[1]) for g in grads] + [_row_spec(LANES), _row_spec(LANES)],
        out_specs=_row_spec(IN_COLS),
        out_shape=jax.ShapeDtypeStruct((t, IN_COLS), BF16), compiler_params=_params(("arbitrary",), 40),
    )(*grads, cos, sin)


GROUP_COLS = ((0, WIDTH_A), (WIDTH_A, WIDTH_A + WIDTH_B), (WIDTH_A + WIDTH_B, D_MODEL))


def _mix_fwd(oa, ob, oc, gain, name):
    t = oa.shape[0]

    def body(a_ref, b_ref, c_ref, g_ref, o_ref):
        for ref, (lo, hi) in zip((a_ref, b_ref, c_ref), GROUP_COLS):
            _, n = _rms_stats(ref[...])
            o_ref[:, lo:hi] = (n * g_ref[:, lo:hi]).astype(BF16)

    return pl.pallas_call(
        body, name=name, grid=(t // ROWS,),
        in_specs=[_row_spec(WIDTH_A), _row_spec(WIDTH_B), _row_spec(WIDTH_C), _vec_spec(D_MODEL)],
        out_specs=_row_spec(D_MODEL),
        out_shape=jax.ShapeDtypeStruct((t, D_MODEL), BF16), compiler_params=_params(("arbitrary",), 32),
    )(oa, ob, oc, gain)


def _mix_bwd(oa, ob, oc, gain, dmixed, name):
    t = oa.shape[0]

    def body(a_ref, b_ref, c_ref, g_ref, dm_ref, da_ref, db_ref, dc_ref, dg_ref):
        first = pl.program_id(0) == 0
        for ref, dref, (lo, hi) in zip((a_ref, b_ref, c_ref), (da_ref, db_ref, dc_ref), GROUP_COLS):
            r, n = _rms_stats(ref[...])
            dm = dm_ref[:, lo:hi]
            dn = dm * g_ref[:, lo:hi]
            dref[...] = r * (dn - n * jnp.mean(dn * n, axis=-1, keepdims=True))
            part = jnp.sum(dm * n, axis=0, keepdims=True)

            @pl.when(first)
            def _():
                dg_ref[:, lo:hi] = part

            @pl.when(jnp.logical_not(first))
            def _():
                dg_ref[:, lo:hi] += part

    return pl.pallas_call(
        body, name=name, grid=(t // ROWS,),
        in_specs=[_row_spec(WIDTH_A), _row_spec(WIDTH_B), _row_spec(WIDTH_C), _vec_spec(D_MODEL), _row_spec(D_MODEL)],
        out_specs=[_row_spec(WIDTH_A), _row_spec(WIDTH_B), _row_spec(WIDTH_C), _vec_spec(D_MODEL)],
        out_shape=[jax.ShapeDtypeStruct((t, WIDTH_A), F32), jax.ShapeDtypeStruct((t, WIDTH_B), F32),
                   jax.ShapeDtypeStruct((t, WIDTH_C), F32), jax.ShapeDtypeStruct((1, D_MODEL), F32)],
        compiler_params=_params(("arbitrary",), 40),
    )(oa, ob, oc, gain, dmixed)


FF_COLS = 256


SUBLANES = 8
CHUNK = 128
HALO = SUBLANES


def _pad_rows(dst_ref, src_ref):
    t, cols = src_ref.shape
    dst_ref[0:HALO, :] = jnp.zeros((HALO, cols), F32)
    dst_ref[HALO:HALO + t, :] = src_ref[...]
    dst_ref[HALO + t:t + 2 * HALO, :] = jnp.zeros((HALO, cols), F32)


def _roll_rows(x, by):
    return pltpu.roll(x, by % x.shape[0], 0)


def _gate_val(pad_ref, r0, w_ref, b_ref):
    ext = [pad_ref[h, pl.ds(r0, CHUNK + 2 * HALO), :] for h in range(2)]
    before = [_roll_rows(e, 1) for e in ext]
    after = [_roll_rows(e, -1) for e in ext]
    gate, val = ((before[h] * w_ref[h, 0:1, :] + ext[h] * w_ref[h, 1:2, :]) + after[h] * w_ref[h, 2:3, :] + b_ref[h]
                 for h in range(2))
    return gate, val, ext, before, after


def _ff_specs(t):
    u_spec = pl.BlockSpec((2, t, FF_COLS), lambda j: (0, 0, j))
    w_spec = pl.BlockSpec((2, 3, FF_COLS), lambda j: (0, 0, j))
    b_spec = pl.BlockSpec((2, 1, FF_COLS), lambda j: (0, 0, j))
    return u_spec, w_spec, b_spec


def _convgate_fwd(u0, conv_w, conv_b, name):
    t = u0.shape[1]
    u_spec, w_spec, b_spec = _ff_specs(t)

    def body(u_ref, w_ref, b_ref, o_ref, pad_ref):
        for h in range(2):
            _pad_rows(pad_ref.at[h], u_ref.at[h])

        def chunk(ci, carry):
            r0 = pl.multiple_of(ci * CHUNK, CHUNK)
            gate, val, _, _, _ = _gate_val(pad_ref, r0, w_ref, b_ref)
            act = gate * jax.nn.sigmoid(gate) * val
            o_ref[pl.ds(r0, CHUNK), :] = act[HALO:HALO + CHUNK].astype(BF16)
            return carry

        lax.fori_loop(0, t // CHUNK, chunk, 0)

    return pl.pallas_call(
        body, name=name, grid=(D_FF // FF_COLS,), in_specs=[u_spec, w_spec, b_spec],
        out_specs=pl.BlockSpec((t, FF_COLS), lambda j: (0, j)),
        out_shape=jax.ShapeDtypeStruct((t, D_FF), BF16),
        scratch_shapes=[pltpu.VMEM((2, t + 2 * HALO, FF_COLS), F32)],
        compiler_params=_params(("arbitrary",), 48),
    )(u0, conv_w, conv_b)


def _convgate_bwd(u0, conv_w, conv_b, d_act, name):
    t = u0.shape[1]
    u_spec, w_spec, b_spec = _ff_specs(t)

    def body(u_ref, w_ref, b_ref, da_ref, du_ref, dw_ref, db_ref, pad_ref, da_pad_ref, sums_ref):
        for h in range(2):
            _pad_rows(pad_ref.at[h], u_ref.at[h])
        _pad_rows(da_pad_ref, da_ref)
        sums_ref[...] = jnp.zeros_like(sums_ref)
        inner = slice(HALO, HALO + CHUNK)

        def fold(x):
            return jnp.sum(x.reshape(CHUNK // SUBLANES, SUBLANES, x.shape[1]), axis=0)

        def chunk(ci, carry):
            r0 = pl.multiple_of(ci * CHUNK, CHUNK)
            gate, val, ext, before, after = _gate_val(pad_ref, r0, w_ref, b_ref)
            sig = jax.nn.sigmoid(gate)
            da = da_pad_ref[pl.ds(r0, CHUNK + 2 * HALO), :]
            d_half = (da * val * (sig * (1.0 + gate * (1.0 - sig))), da * (gate * sig))
            for h in range(2):
                du = d_half[h]
                for k, term in enumerate((du, du * before[h], du * ext[h], du * after[h])):
                    sums_ref[h, k] += fold(term[inner])
                du0 = (_roll_rows(du, -1) * w_ref[h, 0:1, :] + du * w_ref[h, 1:2, :]) + _roll_rows(du, 1) * w_ref[h, 2:3, :]
                du_ref[h, pl.ds(r0, CHUNK), :] = du0[inner].astype(BF16)
            return carry

        lax.fori_loop(0, t // CHUNK, chunk, 0)
        for h in range(2):
            db_ref[h] = jnp.sum(sums_ref[h, 0], axis=0, keepdims=True)
            for k in range(3):
                dw_ref[h, k:k + 1, :] = jnp.sum(sums_ref[h, k + 1], axis=0, keepdims=True)

    return pl.pallas_call(
        body, name=name, grid=(D_FF // FF_COLS,),
        in_specs=[u_spec, w_spec, b_spec, pl.BlockSpec((t, FF_COLS), lambda j: (0, j))],
        out_specs=[u_spec, w_spec, b_spec],
        out_shape=[jax.ShapeDtypeStruct((2, t, D_FF), BF16), jax.ShapeDtypeStruct((2, 3, D_FF), F32),
                   jax.ShapeDtypeStruct((2, 1, D_FF), F32)],
        scratch_shapes=[pltpu.VMEM((2, t + 2 * HALO, FF_COLS), F32), pltpu.VMEM((t + 2 * HALO, FF_COLS), F32),
                        pltpu.VMEM((2, 4, SUBLANES, FF_COLS), F32)],
        compiler_params=_params(("arbitrary",), 56),
    )(u0, conv_w, conv_b, d_act)


class _Group:
    def __init__(self, heads, blks, kv_rows, n_win, gqa, bias_per_head):
        self.heads = heads
        self.pairs = heads // 2
        self.q_blk, self.k_blk, self.v_blk = blks
        self.kv_rows = kv_rows
        self.n_win = n_win
        self.full = kv_rows == SEQ
        self.gqa = gqa
        self.bias_per_head = bias_per_head
        self.width = heads * HEAD_DIM
        self.keys = kv_rows * n_win


GROUP_A = _Group(N_HEADS_A, A_BLK, SEQ, 1, False, False)
GROUP_B = _Group(N_HEADS_B, B_BLK, WINDOW_B, 4, True, False)
GROUP_C = _Group(N_HEADS_C, C_BLK, QB, 3, False, True)


def _win_start(grp, i):
    return jnp.clip(i * (QB // grp.kv_rows) - 1, 0, SEQ // grp.kv_rows - grp.n_win)


def _win_variant(i):
    return jnp.minimum(i, 1) + (i == NQB - 1).astype(jnp.int32)


def _attn_in_specs(grp, t):
    q_spec = pl.BlockSpec((QB, LANES), lambda p, i: (i, grp.q_blk + p))

    def col(blk):
        return (lambda p: blk) if grp.gqa else (lambda p: blk + p)

    def kv_specs(blk):
        c = col(blk)
        if grp.full:
            return [pl.BlockSpec((t, LANES), lambda p, i: (0, c(p)))]
        return [pl.BlockSpec((grp.kv_rows, LANES),
                             functools.partial(lambda p, i, w: (_win_start(grp, i) + w, c(p)), w=w))
                for w in range(grp.n_win)]

    nwk = grp.keys
    if grp.bias_per_head:
        bias_spec = pl.BlockSpec((2, None, QB, nwk), lambda p, i: (p, _win_variant(i), 0, 0))
    elif grp.full:
        bias_spec = pl.BlockSpec((1, None, QB, nwk), lambda p, i: (0, i, 0, 0))
    else:
        bias_spec = pl.BlockSpec((1, None, QB, nwk), lambda p, i: (0, _win_variant(i), 0, 0))
    sink_spec = pl.BlockSpec((1, LANES), lambda p, i: (0, p))
    return q_spec, kv_specs(grp.k_blk), kv_specs(grp.v_blk), bias_spec, sink_spec


def _head_kv(grp, whole, e, p):
    lo, hi = whole[:, :HEAD_DIM], whole[:, HEAD_DIM:]
    if grp.gqa:
        return jnp.where(2 * p + e >= N_HEADS_B // N_KV_B, hi, lo)
    return hi if e else lo


def _softmax_parts(q, k, bias, sink):
    s = lax.dot_general(q, k, (((1,), (1,)), ((), ())), preferred_element_type=F32) + bias
    m = jnp.maximum(jnp.max(s, axis=-1, keepdims=True), sink)
    pe = jnp.exp(s - m)
    denom = jnp.sum(pe, axis=-1, keepdims=True) + jnp.exp(sink - m)
    return pe, m, 1.0 / denom


def _attn_fwd(grp, proj, bias, sink, name):
    t = proj.shape[0]
    q_spec, k_specs, v_specs, bias_spec, sink_spec = _attn_in_specs(grp, t)
    nkv = len(k_specs)

    def body(*refs):
        q_ref = refs[0]
        k_refs, v_refs = refs[1:1 + nkv], refs[1 + nkv:1 + 2 * nkv]
        bias_ref, sink_ref, o_ref = refs[1 + 2 * nkv:4 + 2 * nkv]
        p = pl.program_id(0)
        k_all = jnp.concatenate([r[...] for r in k_refs], axis=0)
        v_all = jnp.concatenate([r[...] for r in v_refs], axis=0)
        outs = []
        for e in range(2):
            q = q_ref[:, e * HEAD_DIM:(e + 1) * HEAD_DIM]
            k = _head_kv(grp, k_all, e, p)
            v = _head_kv(grp, v_all, e, p)
            snk = sink_ref[0:1, e * HEAD_DIM:e * HEAD_DIM + 1]
            pe, _, inv = _softmax_parts(q, k, bias_ref[e if grp.bias_per_head else 0], snk)
            outs.append(jnp.dot(pe.astype(BF16), v, preferred_element_type=F32) * inv)
        o_ref[...] = jnp.concatenate(outs, axis=1)

    return pl.pallas_call(
        body, name=name, grid=(grp.pairs, NQB),
        in_specs=[q_spec, *k_specs, *v_specs, bias_spec, sink_spec],
        out_specs=pl.BlockSpec((QB, LANES), lambda p, i: (i, p)),
        out_shape=jax.ShapeDtypeStruct((t, grp.width), F32),
        compiler_params=_params(("arbitrary", "arbitrary"), 48),
    )(proj, *([proj] * (2 * nkv)), bias, sink)


def _attn_bwd(grp, proj, bias, sink, out, d_out, name):
    t = proj.shape[0]
    q_spec, k_specs, v_specs, bias_spec, sink_spec = _attn_in_specs(grp, t)
    nkv = len(k_specs)
    n_off = 2 * NA_ROWS - 1
    rows_q = QB // GRID_W
    o_spec = pl.BlockSpec((QB, LANES), lambda p, i: (i, p))
    acc_spec = pl.BlockSpec((t, LANES), lambda p, i: (0, p))
    out_specs = [o_spec, acc_spec, acc_spec, pl.BlockSpec((None, 8, LANES), lambda p, i: (p, 0, 0))]
    out_shape = [jax.ShapeDtypeStruct((t, grp.width), F32)] * 3 + [jax.ShapeDtypeStruct((grp.pairs, 8, LANES), F32)]
    if grp.bias_per_head:
        out_specs.append(pl.BlockSpec((2, n_off, GRID_W, GRID_W), lambda p, i: (p, 0, 0, 0)))
        out_shape.append(jax.ShapeDtypeStruct((grp.heads, n_off, GRID_W, GRID_W), F32))

    def body(*refs):
        q_ref = refs[0]
        k_refs, v_refs = refs[1:1 + nkv], refs[1 + nkv:1 + 2 * nkv]
        bias_ref, sink_ref, o_ref, do_ref = refs[1 + 2 * nkv:5 + 2 * nkv]
        dq_ref, dk_ref, dv_ref, dsink_ref = refs[5 + 2 * nkv:9 + 2 * nkv]
        dbias_ref = refs[9 + 2 * nkv] if grp.bias_per_head else None
        p, i = pl.program_id(0), pl.program_id(1)

        @pl.when(i == 0)
        def _():
            dk_ref[...] = jnp.zeros_like(dk_ref)
            dv_ref[...] = jnp.zeros_like(dv_ref)
            dsink_ref[...] = jnp.zeros_like(dsink_ref)
            if dbias_ref is not None:
                dbias_ref[...] = jnp.zeros_like(dbias_ref)

        k_all = jnp.concatenate([r[...] for r in k_refs], axis=0)
        v_all = jnp.concatenate([r[...] for r in v_refs], axis=0)
        start = 0 if grp.full else _win_start(grp, i)
        dqs, dks, dvs, dsinks = [], [], [], []
        for e in range(2):
            cols = slice(e * HEAD_DIM, (e + 1) * HEAD_DIM)
            q = q_ref[:, cols]
            k = _head_kv(grp, k_all, e, p)
            v = _head_kv(grp, v_all, e, p)
            snk = sink_ref[0:1, e * HEAD_DIM:e * HEAD_DIM + 1]
            pe, m, inv = _softmax_parts(q, k, bias_ref[e if grp.bias_per_head else 0], snk)
            prob = pe * inv
            do = do_ref[:, cols]
            do_b = do.astype(BF16)
            delta = jnp.sum(do * o_ref[:, cols], axis=-1, keepdims=True)
            dp = lax.dot_general(do_b, v, (((1,), (1,)), ((), ())), preferred_element_type=F32)
            ds = prob * (dp - delta)
            ds_b = ds.astype(BF16)
            dqs.append(jnp.dot(ds_b, k, preferred_element_type=F32))
            dks.append(lax.dot_general(ds_b, q, (((0,), (0,)), ((), ())), preferred_element_type=F32))
            dvs.append(lax.dot_general(prob.astype(BF16), do_b, (((0,), (0,)), ((), ())), preferred_element_type=F32))
            dsinks.append(-jnp.sum(jnp.exp(snk - m) * inv * delta, axis=0, keepdims=True))
            if dbias_ref is not None:
                shift = (i * QB - start * grp.kv_rows) // GRID_W
                for rq in range(rows_q):
                    for rk in range(grp.keys // GRID_W):
                        off = jnp.clip(rk - rq + (NA_ROWS - 1) - shift, 0, n_off - 1)
                        dbias_ref[e, off] += ds[rq * GRID_W:(rq + 1) * GRID_W, rk * GRID_W:(rk + 1) * GRID_W]
        dq_ref[...] = jnp.concatenate(dqs, axis=1)
        rows = pl.ds(0, t) if grp.full else pl.ds(pl.multiple_of(start * grp.kv_rows, grp.kv_rows), grp.keys)
        dk_ref[rows, :] += jnp.concatenate(dks, axis=1)
        dv_ref[rows, :] += jnp.concatenate(dvs, axis=1)
        lane = lax.broadcasted_iota(jnp.int32, (8, LANES), 1)
        dsink_ref[...] += jnp.where(lane < HEAD_DIM, dsinks[0], dsinks[1])

    return pl.pallas_call(
        body, name=name, grid=(grp.pairs, NQB),
        in_specs=[q_spec, *k_specs, *v_specs, bias_spec, sink_spec, o_spec, o_spec],
        out_specs=out_specs, out_shape=out_shape,
        compiler_params=_params(("arbitrary", "arbitrary"), 56),
    )(proj, *([proj] * (2 * nkv)), bias, sink, out, d_out)


DILATED_CONFIGS = ((128, 1), (512, 4), (2048, 16))


def _bias_a():
    d = jnp.arange(SEQ)[None, :] - jnp.arange(SEQ)[:, None]
    mult = jnp.zeros((SEQ, SEQ), F32)
    for window, r in DILATED_CONFIGS:
        reach = (window // (2 * r)) * r
        mult = mult + ((d % r == 0) & (jnp.abs(d) <= reach)).astype(F32)
    return jnp.where(mult > 0, jnp.log(jnp.maximum(mult, 1.0)), NEG_INF).reshape(1, NQB, QB, SEQ)


def _bias_b():
    row = jnp.arange(QB)[None, :, None]
    col = jnp.arange(GROUP_B.keys)[None, None, :]
    var = jnp.arange(3)[:, None, None]
    d = col - (GROUP_B.kv_rows * var + row)
    return jnp.where(jnp.abs(d) <= WINDOW_B, 0.0, NEG_INF).astype(F32)[None]


def _offset_onehot():
    c = jnp.arange(GRID_W)[:, None, None]
    c2 = jnp.arange(GRID_W)[None, :, None]
    b = jnp.arange(LANES)[None, None, :]
    return (c2 - c + NA_COLS - 1 == b).astype(BF16).reshape(GRID_W * GRID_W, LANES)


def _split_dot(x, g):
    hi = x.astype(BF16)
    rest = x - hi.astype(F32)
    mid = rest.astype(BF16)
    lo = (rest - mid.astype(F32)).astype(BF16)
    return (jnp.dot(hi, g, preferred_element_type=F32) + jnp.dot(mid, g, preferred_element_type=F32)
            + jnp.dot(lo, g, preferred_element_type=F32))


def _table_mm(x, g, name):
    def body(x_ref, g_ref, o_ref):
        o_ref[...] = _split_dot(x_ref[...], g_ref[...])

    return pl.pallas_call(
        body, name=name, out_shape=jax.ShapeDtypeStruct((x.shape[0], g.shape[1]), F32),
        in_specs=[pl.BlockSpec(memory_space=pltpu.VMEM)] * 2, out_specs=pl.BlockSpec(memory_space=pltpu.VMEM),
        compiler_params=pltpu.CompilerParams(vmem_limit_bytes=32 * MIB),
    )(x, g)


N_OFF = 2 * NA_ROWS - 1
TABLE_ROWS = 152


def _bias_c(rpb):
    table = jnp.zeros((TABLE_ROWS, LANES), F32).at[:N_HEADS_C * N_OFF, :2 * NA_COLS - 1].set(
        rpb.reshape(N_HEADS_C * N_OFF, 2 * NA_COLS - 1))
    tiles = _table_mm(table, _offset_onehot().T, "rpb_tiles")[:N_HEADS_C * N_OFF]
    tiles = tiles.reshape(N_HEADS_C, N_OFF, GRID_W, GRID_W)
    c = jnp.arange(GRID_W)
    col_start = jnp.clip(c - NA_COLS // 2, 0, GRID_W - NA_COLS)
    col_ok = (c[None, :] >= col_start[:, None]) & (c[None, :] < col_start[:, None] + NA_COLS)
    tiles = jnp.where(col_ok, tiles, NEG_INF)
    masked = jnp.full((N_HEADS_C, GRID_W, GRID_W), NEG_INF, F32)
    rows_q = QB // GRID_W
    variants = []
    for var in range(3):
        q_rows = []
        for rq in range(rows_q):
            r_l = rows_q * var + rq
            first = min(max(r_l - NA_ROWS // 2, 0), 3 * rows_q - NA_ROWS)
            q_rows.append(jnp.concatenate(
                [tiles[:, rk - r_l + NA_ROWS - 1] if first <= rk < first + NA_ROWS else masked
                 for rk in range(3 * rows_q)], axis=-1))
        variants.append(jnp.concatenate(q_rows, axis=-2))
    return jnp.stack(variants, axis=1)


def _rpb_grad(d_tiles):
    flat = jnp.zeros((TABLE_ROWS, GRID_W * GRID_W), F32).at[:N_HEADS_C * N_OFF].set(
        d_tiles.reshape(N_HEADS_C * N_OFF, GRID_W * GRID_W))
    out = _table_mm(flat, _offset_onehot(), "rpb_grad")
    return out[:N_HEADS_C * N_OFF, :2 * NA_COLS - 1].reshape(N_HEADS_C, N_OFF, 2 * NA_COLS - 1)


def _sink_lanes(sink):
    return jnp.repeat(sink.astype(F32), HEAD_DIM)[None, :]


def _attention_fwd(proj_r, sink_b, bias_a, bias_b, bias_c):
    no_sink_a = jnp.full((1, WIDTH_A), NEG_INF, F32)
    no_sink_c = jnp.full((1, WIDTH_C), NEG_INF, F32)
    oa = _attn_fwd(GROUP_A, proj_r, bias_a, no_sink_a, "attn_a_fwd")
    ob = _attn_fwd(GROUP_B, proj_r, bias_b, _sink_lanes(sink_b), "attn_b_fwd")
    oc = _attn_fwd(GROUP_C, proj_r, bias_c, no_sink_c, "attn_c_fwd")
    return oa, ob, oc


def _attention_bwd(proj_r, sink_b, bias_a, bias_b, bias_c, outs, d_outs, cos, sin):
    no_sink_a = jnp.full((1, WIDTH_A), NEG_INF, F32)
    no_sink_c = jnp.full((1, WIDTH_C), NEG_INF, F32)
    dqa, dka, dva, _ = _attn_bwd(GROUP_A, proj_r, bias_a, no_sink_a, outs[0], d_outs[0], "attn_a_bwd")
    dqb, dkb, dvb, dsink = _attn_bwd(GROUP_B, proj_r, bias_b, _sink_lanes(sink_b), outs[1], d_outs[1], "attn_b_bwd")
    dqc, dkc, dvc, _, d_tiles = _attn_bwd(GROUP_C, proj_r, bias_c, no_sink_c, outs[2], d_outs[2], "attn_c_bwd")
    d_proj = _rope_bwd((dqa, dka, dva, dqb, dkb, dvb, dqc, dkc, dvc), cos, sin, "rope_bwd")
    d_sink = dsink[:, 0, :].reshape(GROUP_B.pairs, 2, HEAD_DIM)[:, :, 0].reshape(N_HEADS_B)
    return d_proj, d_sink, _rpb_grad(d_tiles)


def _adamw(w, g, m, v, name):
    r, c = w.shape
    rows = r
    for cand in (512, 256, 128, 64, 32, 16, 8):
        if r % cand == 0 and cand * c * 4 <= MIB:
            rows = cand
            break
    spec = pl.BlockSpec((rows, c), lambda i: (i, 0))

    def body(w_ref, g_ref, m_ref, v_ref, d_ref, mo_ref, vo_ref):
        d_ref[...], mo_ref[...], vo_ref[...] = _adamw_step(w_ref[...], g_ref[...], m_ref[...], v_ref[...])

    return pl.pallas_call(
        body, name=name, grid=(r // rows,), in_specs=[spec] * 4, out_specs=[spec] * 3,
        out_shape=[jax.ShapeDtypeStruct((r, c), F32)] * 3, compiler_params=_params(("arbitrary",), 32),
    )(w, g, m, v)


def _adamw_step(w, grad, m, v):
    m_new = ADAM_B1 * m + (1.0 - ADAM_B1) * grad
    v_new = ADAM_B2 * v + (1.0 - ADAM_B2) * jnp.square(grad)
    m_hat = m_new / (1.0 - ADAM_B1 ** ADAM_STEP)
    v_hat = v_new / (1.0 - ADAM_B2 ** ADAM_STEP)
    return -ADAM_LR * (m_hat / (jnp.sqrt(v_hat) + ADAM_EPS) + ADAM_WD * w), m_new, v_new


def _adamw_layer(w, g, m, v, layer, prev, name):
    _, r, c = w.shape
    rows = next(cand for cand in (512, 256, 128, 64, 32, 16, 8) if r % cand == 0 and cand * c * 4 <= MIB)
    spec = pl.BlockSpec((None, rows, c), lambda i: (layer, i, 0))
    n_prev = 0 if prev is None else 4

    def body(w_ref, g_ref, m_ref, v_ref, *rest):
        go_ref, d_ref, mo_ref, vo_ref = rest[n_prev:]
        grad = g_ref[...]
        go_ref[...] = grad
        d_ref[...], mo_ref[...], vo_ref[...] = _adamw_step(w_ref[...], grad, m_ref[...], v_ref[...])

    return pl.pallas_call(
        body, name=name, grid=(r // rows,), in_specs=[spec] * 4 + [ANY_SPEC] * n_prev, out_specs=[spec] * 4,
        out_shape=[jax.ShapeDtypeStruct(w.shape, F32)] * 4,
        input_output_aliases={4 + i: i for i in range(n_prev)}, compiler_params=_params(("arbitrary",), 32),
    )(w, g, m, v, *(prev or ()))


def _layer_fwd(x0, p, weight, tabs):
    h1 = _rmsnorm_fwd(x0, p["ln_attn"], "ln_attn_fwd")
    proj = _mm_nn(h1, weight("w_in", h1), cols=True, tn=256, tk=D_MODEL, out_dtype=F32, name="mm_in")
    proj_r = _rope_fwd(proj, tabs["cos"], tabs["sin"], "rope_fwd")
    outs = _attention_fwd(proj_r, p["sink_b"], tabs["bias_a"], tabs["bias_b"], p["bias_c"])
    mixed = _mix_fwd(*outs, p["mix_gain"], "mix_fwd")
    x1 = _mm_nn(mixed, weight("w_out", mixed), cols=False, tn=256, tk=D_MODEL, out_dtype=F32, name="mm_out",
                residual=x0)
    h2 = _rmsnorm_fwd(x1, p["ln_ffn"], "ln_ffn_fwd")
    u0 = _mm_nn(h2, weight("w_up", h2), cols=True, tn=256, tk=D_MODEL, out_dtype=F32, name="mm_up", out_split=2)
    act = _convgate_fwd(u0, p["conv_w"], p["conv_b"], "convgate_fwd")
    x2 = _mm_nn(act, weight("w_down", act), cols=False, tn=512, tk=D_FF // 2, out_dtype=F32, name="mm_down",
                residual=x1)
    return x2, (x0, h1, proj_r, outs, mixed, x1, h2, u0, act)


def _layer_bwd(dx2, dx2_b, saved, p, big, tabs, send, after):
    x0, h1, proj_r, outs, mixed, x1, h2, u0, act = saved
    d_act = _mm_nt(dx2_b, big["w_down"], cols=False, to=512, tr=D_MODEL, out_dtype=F32, name="nt_down", after=after)
    g_down = _mm_tn(act, dx2_b, tk=D_FF // N_SHARDS, tn=512, shards=-N_SHARDS, name="tn_down")
    du0, d_conv_w, d_conv_b = _convgate_bwd(u0, p["conv_w"], p["conv_b"], d_act, "convgate_bwd")
    dh2 = _mm_nt(du0, big["w_up"], cols=True, to=1024, tr=D_FF // 4, out_dtype=F32, name="nt_up")
    g_up = _mm_tn(h2, du0, tk=512, tn=D_FF // 4, shards=N_SHARDS, name="tn_up")
    token = send({"w_down": g_down, "w_up": g_up})
    dx1, dx1_b, d_ln_ffn = _rmsnorm_bwd(x1, p["ln_ffn"], dh2, dx2, "ln_ffn_bwd", after=[token])
    d_mixed = _mm_nt(dx1_b, big["w_out"], cols=False, to=512, tr=D_MODEL, out_dtype=F32, name="nt_out")
    g_out = _mm_tn(mixed, dx1_b, tk=D_MODEL // N_SHARDS, tn=512, shards=-N_SHARDS, name="tn_out")
    *d_outs, d_mix_gain = _mix_bwd(*outs, p["mix_gain"], d_mixed, "mix_bwd")
    d_proj, d_sink, d_rpb = _attention_bwd(proj_r, p["sink_b"], tabs["bias_a"], tabs["bias_b"], p["bias_c"], outs,
                                           d_outs, tabs["cos"], tabs["sin"])
    dh1 = _mm_nt(d_proj, big["w_in"], cols=True, to=1024, tr=IN_COLS // N_SHARDS, out_dtype=F32, name="nt_in")
    g_in = _mm_tn(h1, d_proj, tk=512, tn=IN_COLS // N_SHARDS, shards=N_SHARDS, name="tn_in")
    dx0, dx0_b, d_ln_attn = _rmsnorm_bwd(x0, p["ln_attn"], dh1, dx1, "ln_attn_bwd")
    token = send({"w_out": g_out, "w_in": g_in})
    small = {"ln_attn": d_ln_attn, "sink_b": d_sink, "rpb_c": d_rpb, "mix_gain": d_mix_gain, "ln_ffn": d_ln_ffn,
             "conv_w": d_conv_w, "conv_b": d_conv_b}
    return dx0, dx0_b, small, token


HBM_SPEC = pl.BlockSpec(memory_space=pl.ANY)


def _place():
    x, y, c = lax.axis_index("x"), lax.axis_index("y"), lax.axis_index("c")
    chips = ((1 - x, y), (x, 1 - y), (1 - x, 1 - y))
    return x, y, c, chips


def _shard_index(px, py):
    return 2 * px + py


def _remote(src, dst, send_sem, recv_sem, to):
    return pltpu.make_async_remote_copy(src_ref=src, dst_ref=dst, send_sem=send_sem, recv_sem=recv_sem,
                                        device_id=to, device_id_type=MESH)


def _own_slot(w, layer, shard, name):
    _, r, c_dim = w.shape
    rows = r
    for cand in (512, 256, 128):
        if r % cand == 0 and cand * c_dim * 4 <= 2 * MIB:
            rows = cand
            break

    def body(s_ref, w_ref, o_ref):
        o_ref[...] = w_ref[...].astype(BF16)

    return pl.pallas_call(
        body, name=name,
        grid_spec=pltpu.PrefetchScalarGridSpec(
            num_scalar_prefetch=1, grid=(r // rows,),
            in_specs=[pl.BlockSpec((None, rows, c_dim), lambda i, s: (layer, i, 0))],
            out_specs=pl.BlockSpec((None, rows, c_dim), lambda i, s: (s[0], i, 0))),
        out_shape=jax.ShapeDtypeStruct((N_SHARDS, r, c_dim), BF16),
        compiler_params=_params(("arbitrary",), 32),
    )(shard.astype(jnp.int32).reshape(1), w)


def _gather_weights(bufs):
    n = len(bufs)

    def body(*refs):
        outs = refs[n:2 * n]
        send1, recv1, send2, recv2 = refs[2 * n:]
        x, y, c, chips = _place()
        me = _shard_index(x, y)
        sibling = (x, y, 1 - c)
        first = []
        for t in range(n):
            for j, (px, py) in enumerate(chips):
                mine = outs[t].at[c, me]
                cp = _remote(mine, mine, send1.at[t * 3 + j], recv1.at[t * 3 + j], (px, py, c))
                cp.start()
                first.append(cp)
        passed = []
        for t in range(n):
            for j, (px, py) in enumerate(chips):
                slot = outs[t].at[c, _shard_index(px, py)]
                _remote(slot, slot, send1.at[t * 3 + j], recv1.at[t * 3 + j], (px, py, c)).wait_recv()
                cp = _remote(slot, slot, send2.at[t * 3 + j], recv2.at[t * 3 + j], sibling)
                cp.start()
                passed.append(cp)
        for t in range(n):
            for j, (px, py) in enumerate(chips):
                slot = outs[t].at[1 - c, _shard_index(px, py)]
                _remote(slot, slot, send2.at[t * 3 + j], recv2.at[t * 3 + j], sibling).wait_recv()
        for cp in first + passed:
            cp.wait_send()

    return pl.pallas_call(
        body, name="gather_weights", in_specs=[HBM_SPEC] * n, out_specs=[HBM_SPEC] * n,
        out_shape=[jax.ShapeDtypeStruct(b.shape, b.dtype) for b in bufs],
        input_output_aliases={t: t for t in range(n)},
        scratch_shapes=[pltpu.SemaphoreType.DMA((n * 3,))] * 4,
    )(*bufs)


def _pair_exchange(bufs):
    n = len(bufs)

    def body(*refs):
        ins, outs = refs[:n], refs[n:2 * n]
        send, recv = refs[2 * n:]
        x, y, c, _ = _place()
        sibling = (x, y, 1 - c)
        cps = [_remote(ins[t].at[1 - c], outs[t], send.at[t], recv.at[t], sibling) for t in range(n)]
        for cp in cps:
            cp.start()
        for cp in cps:
            cp.wait()

    return pl.pallas_call(
        body, name="pair_exchange", in_specs=[HBM_SPEC] * n, out_specs=[HBM_SPEC] * n,
        out_shape=[jax.ShapeDtypeStruct(b.shape[1:], b.dtype) for b in bufs],
        scratch_shapes=[pltpu.SemaphoreType.DMA((n,))] * 2,
    )(*bufs)


def _pair_sum(own, other, name):
    _, s, r, c_dim = own.shape
    rows = min(r, LANES)
    per = r // rows
    layer = lax.axis_index("c").astype(jnp.int32).reshape(1)

    def body(layer_ref, a_ref, b_ref, o_ref):
        o_ref[...] = (a_ref[...] + b_ref[...]).astype(BF16)

    return pl.pallas_call(
        body, name=name,
        grid_spec=pltpu.PrefetchScalarGridSpec(
            num_scalar_prefetch=1, grid=(s * per,),
            in_specs=[pl.BlockSpec((None, None, rows, c_dim), lambda i, lay: (lay[0], i // per, i % per, 0)),
                      pl.BlockSpec((None, rows, c_dim), lambda i, lay: (i // per, i % per, 0))],
            out_specs=pl.BlockSpec((None, rows, c_dim), lambda i, lay: (i // per, i % per, 0))),
        out_shape=jax.ShapeDtypeStruct((s, r, c_dim), BF16), compiler_params=_params(("arbitrary",), 40),
    )(layer, own, other)


def _chip_exchange(bufs):
    n = len(bufs)

    def body(*refs):
        ins, outs = refs[:n], refs[n:2 * n]
        send, recv = refs[2 * n:]
        x, y, c, chips = _place()
        me = _shard_index(x, y)
        cps = []
        for t in range(n):
            for j, (px, py) in enumerate(chips):
                cp = _remote(ins[t].at[_shard_index(px, py)], outs[t].at[me], send.at[t * 3 + j], recv.at[t * 3 + j],
                             (px, py, c))
                cp.start()
                cps.append(cp)
        for t in range(n):
            for j, (px, py) in enumerate(chips):
                slot = outs[t].at[_shard_index(px, py)]
                _remote(slot, slot, send.at[t * 3 + j], recv.at[t * 3 + j], (px, py, c)).wait_recv()
        for cp in cps:
            cp.wait_send()

    return pl.pallas_call(
        body, name="chip_exchange", in_specs=[HBM_SPEC] * n, out_specs=[HBM_SPEC] * n,
        out_shape=[jax.ShapeDtypeStruct(b.shape, b.dtype) for b in bufs],
        scratch_shapes=[pltpu.SemaphoreType.DMA((n * 3,))] * 2,
    )(*bufs)


HBM_ONLY = pl.BlockSpec(memory_space=pltpu.HBM)
SEM_SPEC = pl.BlockSpec(memory_space=pltpu.SEMAPHORE)
DATAFLOW = pltpu.SideEffectType.DATAFLOW_SIDE_EFFECTING


def _in_hbm(a):
    return pltpu.with_memory_space_constraint(a, pltpu.HBM)


def _chip_exchange_start(bufs, name):
    n = len(bufs)

    def body(*refs):
        ins, lands = refs[:n], refs[n:2 * n]
        send, recv = refs[2 * n], refs[2 * n + 1]
        token = refs[-1]
        x, y, c, chips = _place()
        me = _shard_index(x, y)
        for t in range(n):
            for j, (px, py) in enumerate(chips):
                _remote(ins[t].at[_shard_index(px, py)], lands[t].at[me], send.at[t * 3 + j], recv.at[t * 3 + j],
                        (px, py, c)).start()
        token[...] = jnp.zeros_like(token)

    thru = [pltpu.HBM(b.shape, b.dtype) for b in bufs]
    res = pl.pallas_call(
        body, name=name,
        out_shape=(pltpu.SemaphoreType.DMA((n * 3,)), pltpu.SemaphoreType.DMA((n * 3,)), *thru, *thru,
                   jax.ShapeDtypeStruct((8, LANES), F32)),
        in_specs=[HBM_ONLY] * (2 * n),
        out_specs=(SEM_SPEC, SEM_SPEC, *([HBM_ONLY] * (2 * n)), pl.BlockSpec(memory_space=pltpu.VMEM)),
        input_output_aliases={i: 2 + i for i in range(2 * n)},
        compiler_params=pltpu.CompilerParams(has_side_effects=DATAFLOW),
    )(*[_in_hbm(b) for b in bufs], *[_in_hbm(lax.empty(b.shape, b.dtype)) for b in bufs])
    return res[0], res[1], list(res[2:2 + n]), list(res[2 + n:2 + 2 * n]), res[-1]


def _chip_exchange_wait(send, recv, bufs, lands, after, name):
    n = len(bufs)

    def body(*refs):
        ins, outs = refs[:n], refs[n:2 * n]
        send_ref, recv_ref = refs[2 * n], refs[2 * n + 1]
        x, y, c, chips = _place()
        for t in range(n):
            for j, (px, py) in enumerate(chips):
                sent = ins[t].at[_shard_index(px, py)]
                slot = outs[t].at[_shard_index(px, py)]
                cp = _remote(sent, slot, send_ref.at[t * 3 + j], recv_ref.at[t * 3 + j], (px, py, c))
                cp.wait_send()
                cp.wait_recv()

    thru = [pltpu.HBM(b.shape, b.dtype) for b in bufs]
    res = pl.pallas_call(
        body, name=name, out_shape=(*thru, *thru),
        in_specs=[HBM_ONLY] * (2 * n) + [SEM_SPEC, SEM_SPEC, pl.BlockSpec(memory_space=pl.ANY)],
        out_specs=[HBM_ONLY] * (2 * n),
        input_output_aliases={i: i for i in range(2 * n)},
        compiler_params=pltpu.CompilerParams(has_side_effects=DATAFLOW),
    )(*bufs, *lands, send, recv, after)
    return list(res[:n]), list(res[n:])


def _chip_sum(pair, landed, name):
    s, r, c_dim = pair.shape
    rows = min(r, LANES)
    shard = _shard_index(lax.axis_index("x"), lax.axis_index("y"))
    where = jnp.stack([shard, lax.axis_index("c")]).astype(jnp.int32)

    def landed_spec(k):
        return pl.BlockSpec((None, rows, c_dim), lambda i, w: (jnp.where(w[0] == k, (k + 1) % s, k), i, 0))

    def body(w_ref, own_ref, *rest):
        o_ref = rest[s]
        acc = None
        for k in range(s):
            term = jnp.where(w_ref[0] == k, own_ref[...], rest[k][...]).astype(F32)
            acc = term if acc is None else acc + term
        o_ref[...] = acc

    return pl.pallas_call(
        body, name=name,
        grid_spec=pltpu.PrefetchScalarGridSpec(
            num_scalar_prefetch=1, grid=(r // rows,),
            in_specs=[pl.BlockSpec((None, rows, c_dim), lambda i, w: (w[0], i, 0))] + [landed_spec(k) for k in range(s)],
            out_specs=pl.BlockSpec((None, rows, c_dim), lambda i, w: (w[1], i, 0))),
        out_shape=jax.ShapeDtypeStruct((DEPTH, r, c_dim), F32), compiler_params=_params(("arbitrary",), 40),
    )(where, pair, *([landed] * s))


def _sum_slots(buf, name):
    s, r, c_dim = buf.shape
    rows = min(r, LANES)

    def body(i_ref, o_ref):
        acc = i_ref[0].astype(F32)
        for k in range(1, s):
            acc = acc + i_ref[k].astype(F32)
        o_ref[...] = acc

    return pl.pallas_call(
        body, name=name, grid=(r // rows,),
        in_specs=[pl.BlockSpec((s, rows, c_dim), lambda i: (0, i, 0))],
        out_specs=pl.BlockSpec((rows, c_dim), lambda i: (i, 0)),
        out_shape=jax.ShapeDtypeStruct((r, c_dim), F32), compiler_params=_params(("arbitrary",), 40),
    )(buf)


def _pair_gather(bufs):
    n = len(bufs)

    def body(*refs):
        outs = refs[n:2 * n]
        send, recv = refs[2 * n:]
        x, y, c, _ = _place()
        sibling = (x, y, 1 - c)
        cps = [_remote(outs[t].at[c], outs[t].at[c], send.at[t], recv.at[t], sibling) for t in range(n)]
        for cp in cps:
            cp.start()
        for t in range(n):
            slot = outs[t].at[1 - c]
            _remote(slot, slot, send.at[t], recv.at[t], sibling).wait_recv()
        for cp in cps:
            cp.wait_send()

    return pl.pallas_call(
        body, name="pair_gather", in_specs=[HBM_SPEC] * n, out_specs=[HBM_SPEC] * n,
        out_shape=[jax.ShapeDtypeStruct(b.shape, b.dtype) for b in bufs],
        input_output_aliases={t: t for t in range(n)},
        scratch_shapes=[pltpu.SemaphoreType.DMA((n,))] * 2,
    )(*bufs)


N_DEV = 8


def _all_gather_small(vec, name, after=()):
    n_after = len(after)

    def body(v_ref, *rest):
        o_ref, send, recv, local_sem = rest[n_after:]
        x, y, c, _ = _place()
        me = 4 * x + 2 * y + c
        local = pltpu.make_async_copy(v_ref, o_ref.at[me], local_sem)
        local.start()
        flips = [(fx, fy, fc) for fx in (0, 1) for fy in (0, 1) for fc in (0, 1)][1:]
        peers = [((1 - x) if fx else x, (1 - y) if fy else y, (1 - c) if fc else c) for fx, fy, fc in flips]
        cps = [_remote(v_ref, o_ref.at[me], send.at[k], recv.at[k], peer) for k, peer in enumerate(peers)]
        for cp in cps:
            cp.start()
        for k, (px, py, pc) in enumerate(peers):
            slot = o_ref.at[4 * px + 2 * py + pc]
            _remote(slot, slot, send.at[k], recv.at[k], (px, py, pc)).wait_recv()
        for cp in cps:
            cp.wait_send()
        local.wait()

    return pl.pallas_call(
        body, name=name, in_specs=[HBM_SPEC] * (1 + n_after), out_specs=HBM_SPEC,
        out_shape=jax.ShapeDtypeStruct((N_DEV,) + vec.shape, vec.dtype),
        scratch_shapes=[pltpu.SemaphoreType.DMA((N_DEV - 1,))] * 2 + [pltpu.SemaphoreType.DMA(())],
    )(vec, *after)


def _half(ref, slot, c):
    half = ref.shape[1] // 2
    return ref.at[slot, pl.ds(pl.multiple_of(c * half, 8), half)]


def _gather_start(bufs, after, name):
    n = len(bufs)
    n_after = len(after)

    def body(*refs):
        ins = refs[:n]
        send, recv = refs[n + n_after], refs[n + n_after + 1]
        token = refs[-1]
        x, y, c, chips = _place()
        me = _shard_index(x, y)
        for t in range(n):
            for j, (px, py) in enumerate(chips):
                mine = _half(ins[t], me, c)
                _remote(mine, mine, send.at[t * 3 + j], recv.at[t * 3 + j], (px, py, c)).start()
        token[...] = jnp.zeros_like(token)

    thru = [pltpu.HBM(b.shape, b.dtype) for b in bufs]
    res = pl.pallas_call(
        body, name=name,
        out_shape=(pltpu.SemaphoreType.DMA((n * 3,)), pltpu.SemaphoreType.DMA((n * 3,)), *thru,
                   jax.ShapeDtypeStruct((8, LANES), F32)),
        in_specs=[HBM_ONLY] * n + [ANY_SPEC] * n_after,
        out_specs=(SEM_SPEC, SEM_SPEC, *([HBM_ONLY] * n), pl.BlockSpec(memory_space=pltpu.VMEM)),
        input_output_aliases={i: 2 + i for i in range(n)},
        compiler_params=pltpu.CompilerParams(has_side_effects=DATAFLOW),
    )(*[_in_hbm(b) for b in bufs], *after)
    return res[0], res[1], list(res[2:2 + n]), res[-1]


def _gather_wait(send, recv, bufs, after, name):
    n = len(bufs)

    def body(*refs):
        ins = refs[:n]
        send_ref, recv_ref = refs[n], refs[n + 1]
        x, y, c, chips = _place()
        me = _shard_index(x, y)
        for t in range(n):
            for j, (px, py) in enumerate(chips):
                cp = _remote(_half(ins[t], me, c), _half(ins[t], _shard_index(px, py), c), send_ref.at[t * 3 + j],
                             recv_ref.at[t * 3 + j], (px, py, c))
                cp.wait_send()
                cp.wait_recv()

    res = pl.pallas_call(
        body, name=name, out_shape=tuple(pltpu.HBM(b.shape, b.dtype) for b in bufs),
        in_specs=[HBM_ONLY] * n + [SEM_SPEC, SEM_SPEC] + [ANY_SPEC] * len(after), out_specs=[HBM_ONLY] * n,
        input_output_aliases={i: i for i in range(n)},
        compiler_params=pltpu.CompilerParams(has_side_effects=DATAFLOW),
    )(*bufs, send, recv, *after)
    return list(res)


def _gather_forward(bufs, name):
    n = len(bufs)

    def body(*refs):
        outs = refs[n:2 * n]
        send, recv = refs[2 * n:]
        x, y, c, chips = _place()
        sibling = (x, y, 1 - c)
        cps = []
        for t in range(n):
            for j, (px, py) in enumerate(chips):
                got = _half(outs[t], _shard_index(px, py), c)
                cp = _remote(got, got, send.at[t * 3 + j], recv.at[t * 3 + j], sibling)
                cp.start()
                cps.append(cp)
        for t in range(n):
            for j, (px, py) in enumerate(chips):
                theirs = _half(outs[t], _shard_index(px, py), 1 - c)
                _remote(theirs, theirs, send.at[t * 3 + j], recv.at[t * 3 + j], sibling).wait_recv()
        for cp in cps:
            cp.wait_send()

    return pl.pallas_call(
        body, name=name, in_specs=[HBM_SPEC] * n, out_specs=[HBM_SPEC] * n,
        out_shape=[jax.ShapeDtypeStruct(b.shape, b.dtype) for b in bufs],
        input_output_aliases={t: t for t in range(n)},
        scratch_shapes=[pltpu.SemaphoreType.DMA((n * 3,))] * 2,
    )(*bufs)


def _half_exchange(grads, name):
    n = len(grads)

    def body(*refs):
        ins, outs = refs[:n], refs[n:2 * n]
        send, recv = refs[2 * n:]
        x, y, c, _ = _place()
        sibling = (x, y, 1 - c)
        cps = []
        for t in range(n):
            half = ins[t].shape[1] // 2
            theirs = ins[t].at[:, pl.ds(pl.multiple_of((1 - c) * half, 8), half)]
            cps.append(_remote(theirs, outs[t], send.at[t], recv.at[t], sibling))
        for cp in cps:
            cp.start()
        for cp in cps:
            cp.wait()

    return pl.pallas_call(
        body, name=name, in_specs=[HBM_SPEC] * n, out_specs=[HBM_SPEC] * n,
        out_shape=[jax.ShapeDtypeStruct((g.shape[0], g.shape[1] // 2, g.shape[2]), g.dtype) for g in grads],
        scratch_shapes=[pltpu.SemaphoreType.DMA((n,))] * 2,
    )(*grads)


def _half_rows(half, c_dim):
    for cand in (512, 256, 128, 64):
        if half % cand == 0 and cand * c_dim * 2 <= MIB:
            return cand
    raise ValueError((half, c_dim))


def _core_index():
    return lax.axis_index("c").astype(jnp.int32).reshape(1)


def _half_sum(own, other, name):
    s, r, c_dim = own.shape
    rows = _half_rows(r // 2, c_dim)
    per = r // 2 // rows

    def body(c_ref, a_ref, b_ref, o_ref):
        o_ref[...] = (a_ref[...].astype(F32) + b_ref[...].astype(F32)).astype(BF16)

    return pl.pallas_call(
        body, name=name,
        grid_spec=pltpu.PrefetchScalarGridSpec(
            num_scalar_prefetch=1, grid=(s, per),
            in_specs=[pl.BlockSpec((None, rows, c_dim), lambda k, i, c: (k, c[0] * per + i, 0)),
                      pl.BlockSpec((None, rows, c_dim), lambda k, i, c: (k, i, 0))],
            out_specs=pl.BlockSpec((None, rows, c_dim), lambda k, i, c: (k, i, 0))),
        out_shape=jax.ShapeDtypeStruct((s, r // 2, c_dim), BF16), compiler_params=_params(("arbitrary", "arbitrary"), 32),
    )(_core_index(), own, other)


def _reduce_start(pairs, name):
    n = len(pairs)

    def body(*refs):
        ins, lands = refs[:n], refs[n:2 * n]
        send, recv = refs[2 * n], refs[2 * n + 1]
        token = refs[-1]
        x, y, c, chips = _place()
        me = _shard_index(x, y)
        for t in range(n):
            for j, (px, py) in enumerate(chips):
                _remote(ins[t].at[_shard_index(px, py)], lands[t].at[me], send.at[t * 3 + j], recv.at[t * 3 + j],
                        (px, py, c)).start()
        token[...] = jnp.zeros_like(token)

    thru = [pltpu.HBM(b.shape, b.dtype) for b in pairs]
    res = pl.pallas_call(
        body, name=name,
        out_shape=(pltpu.SemaphoreType.DMA((n * 3,)), pltpu.SemaphoreType.DMA((n * 3,)), *thru, *thru,
                   jax.ShapeDtypeStruct((8, LANES), F32)),
        in_specs=[HBM_ONLY] * (2 * n),
        out_specs=(SEM_SPEC, SEM_SPEC, *([HBM_ONLY] * (2 * n)), pl.BlockSpec(memory_space=pltpu.VMEM)),
        input_output_aliases={i: 2 + i for i in range(2 * n)},
        compiler_params=pltpu.CompilerParams(has_side_effects=DATAFLOW),
    )(*[_in_hbm(b) for b in pairs], *[_in_hbm(lax.empty(b.shape, b.dtype)) for b in pairs])
    return res[0], res[1], list(res[2:2 + n]), list(res[2 + n:2 + 2 * n]), res[-1]


def _reduce_wait(send, recv, pairs, lands, after, name):
    n = len(pairs)

    def body(*refs):
        ins, got = refs[:n], refs[n:2 * n]
        send_ref, recv_ref = refs[2 * n], refs[2 * n + 1]
        x, y, c, chips = _place()
        for t in range(n):
            for j, (px, py) in enumerate(chips):
                s = _shard_index(px, py)
                cp = _remote(ins[t].at[s], got[t].at[s], send_ref.at[t * 3 + j], recv_ref.at[t * 3 + j], (px, py, c))
                cp.wait_send()
                cp.wait_recv()

    thru = [pltpu.HBM(b.shape, b.dtype) for b in pairs]
    res = pl.pallas_call(
        body, name=name, out_shape=(*thru, *thru),
        in_specs=[HBM_ONLY] * (2 * n) + [SEM_SPEC, SEM_SPEC] + [ANY_SPEC] * len(after),
        out_specs=[HBM_ONLY] * (2 * n),
        input_output_aliases={i: i for i in range(2 * n)},
        compiler_params=pltpu.CompilerParams(has_side_effects=DATAFLOW),
    )(*pairs, *lands, send, recv, *after)
    return list(res[:n]), list(res[n:])


def _reduce_sum(pair, landed, layer, prev, name):
    s, half, c_dim = pair.shape
    rows = _half_rows(half, c_dim)
    per = half // rows
    shard = _shard_index(lax.axis_index("x"), lax.axis_index("y"))
    where = jnp.stack([shard, lax.axis_index("c")]).astype(jnp.int32)

    def landed_spec(k):
        return pl.BlockSpec((None, rows, c_dim), lambda i, w: (jnp.where(w[0] == k, (k + 1) % s, k), i, 0))

    def body(w_ref, own_ref, *rest):
        o_ref = rest[-1]
        acc = None
        for k in range(s):
            term = jnp.where(w_ref[0] == k, own_ref[...], rest[k][...]).astype(F32)
            acc = term if acc is None else acc + term
        o_ref[...] = acc

    args = [where, pair] + [landed] * s
    in_specs = [pl.BlockSpec((None, rows, c_dim), lambda i, w: (w[0], i, 0))] + [landed_spec(k) for k in range(s)]
    aliases = {}
    if prev is not None:
        args.append(prev)
        in_specs.append(ANY_SPEC)
        aliases = {len(args) - 1: 0}
    return pl.pallas_call(
        body, name=name,
        grid_spec=pltpu.PrefetchScalarGridSpec(
            num_scalar_prefetch=1, grid=(per,), in_specs=in_specs,
            out_specs=pl.BlockSpec((None, rows, c_dim), lambda i, w: (layer, w[1] * per + i, 0))),
        out_shape=jax.ShapeDtypeStruct((DEPTH, 2 * half, c_dim), F32), input_output_aliases=aliases,
        compiler_params=_params(("arbitrary",), 40),
    )(*args)


def _half_gather(bufs, layer, name):
    n = len(bufs)

    def body(*refs):
        outs = refs[n:2 * n]
        send, recv = refs[2 * n:]
        x, y, c, _ = _place()
        sibling = (x, y, 1 - c)

        def rows(t, which):
            half = outs[t].shape[1] // 2
            return outs[t].at[layer, pl.ds(pl.multiple_of(which * half, 8), half)]

        cps = [_remote(rows(t, c), rows(t, c), send.at[t], recv.at[t], sibling) for t in range(n)]
        for cp in cps:
            cp.start()
        for t in range(n):
            _remote(rows(t, 1 - c), rows(t, 1 - c), send.at[t], recv.at[t], sibling).wait_recv()
        for cp in cps:
            cp.wait_send()

    return pl.pallas_call(
        body, name=name, in_specs=[HBM_SPEC] * n, out_specs=[HBM_SPEC] * n,
        out_shape=[jax.ShapeDtypeStruct(b.shape, b.dtype) for b in bufs],
        input_output_aliases={t: t for t in range(n)},
        scratch_shapes=[pltpu.SemaphoreType.DMA((n,))] * 2,
    )(*bufs)


WEIGHT_NAMES = ("ln_attn", "w_in", "sink_b", "rpb_c", "mix_gain", "w_out", "ln_ffn", "w_up", "conv_w", "conv_b",
                "w_down", "ln_final")
BIG_NAMES = ("w_in", "w_out", "w_up", "w_down")
REPLICATED_NAMES = ("ln_attn", "sink_b", "rpb_c", "mix_gain", "ln_ffn", "conv_b", "ln_final")
PACK_TILE = 8 * LANES


def _pack(arrays, row_multiple):
    pieces = []
    for a in arrays:
        flat = a.reshape(-1)
        pieces.append(jnp.pad(flat, (0, (-flat.shape[0]) % PACK_TILE)))
    flat = jnp.concatenate(pieces)
    flat = jnp.pad(flat, (0, (-flat.shape[0]) % (row_multiple * LANES)))
    return flat.reshape(-1, LANES)


def _unpack(packed, shapes):
    flat = packed.reshape(-1)
    out, off = [], 0
    for shape in shapes:
        size = math.prod(shape)
        out.append(flat[off:off + size].reshape(shape))
        off += size + (-size) % PACK_TILE
    return out


def kernel(x, ln_attn, w_in, sink_b, rpb_c, mix_gain, w_out, ln_ffn, w_up, conv_w, conv_b, w_down, ln_final, loss_target, m_ln_attn, m_w_in, m_sink_b, m_rpb_c, m_mix_gain, m_w_out, m_ln_ffn, m_w_up, m_conv_w, m_conv_b, m_w_down, m_ln_final, v_ln_attn, v_w_in, v_sink_b, v_rpb_c, v_mix_gain, v_w_out, v_ln_ffn, v_w_up, v_conv_w, v_conv_b, v_w_down, v_ln_final):
    w = dict(ln_attn=ln_attn, w_in=w_in, sink_b=sink_b, rpb_c=rpb_c, mix_gain=mix_gain, w_out=w_out, ln_ffn=ln_ffn,
             w_up=w_up, conv_w=conv_w, conv_b=conv_b, w_down=w_down, ln_final=ln_final)
    m = dict(ln_attn=m_ln_attn, w_in=m_w_in, sink_b=m_sink_b, rpb_c=m_rpb_c, mix_gain=m_mix_gain, w_out=m_w_out,
             ln_ffn=m_ln_ffn, w_up=m_w_up, conv_w=m_conv_w, conv_b=m_conv_b, w_down=m_w_down, ln_final=m_ln_final)
    v = dict(ln_attn=v_ln_attn, w_in=v_w_in, sink_b=v_sink_b, rpb_c=v_rpb_c, mix_gain=v_mix_gain, w_out=v_w_out,
             ln_ffn=v_ln_ffn, w_up=v_w_up, conv_w=v_conv_w, conv_b=v_conv_b, w_down=v_w_down, ln_final=v_ln_final)
    shard = _shard_index(lax.axis_index("x"), lax.axis_index("y"))
    up_cols = w_up.shape[2]

    conv_slots = _all_gather_small(_pack([conv_w], 8), "gather_conv_w")
    conv_all = conv_slots[0::2].reshape(N_SHARDS, -1)[:, :conv_w.size].reshape((N_SHARDS,) + conv_w.shape)

    arrivals = []
    group_of = {}
    tokens = []
    rest = ("w_out", "w_up", "w_down")
    for l, names in ((0, ("w_in",)), (0, rest), (1, ("w_in",)), (1, rest)):
        bufs = [_own_slot(w[k], l, shard, "own_" + k) for k in names]
        send, recv, bufs, token = _gather_start(bufs, tokens[-1:] or [conv_slots], "gather_start_%d" % len(arrivals))
        tokens.append(token)
        for k in names:
            group_of[l, k] = len(arrivals)
        arrivals.append({"names": names, "send": send, "recv": recv, "bufs": bufs, "done": None})

    def gathered(l, name, after):
        idx = group_of[l, name]
        group = arrivals[idx]
        if group["done"] is None:
            bufs = _gather_wait(group["send"], group["recv"], group["bufs"], list(after) + tokens[-1:],
                                "gather_wait_%d" % idx)
            group["done"] = dict(zip(group["names"], _gather_forward(bufs, "gather_forward_%d" % idx)))
        buf = group["done"][name]
        return buf.reshape(1, -1, buf.shape[2]) if name in ("w_out", "w_down") else buf

    cos, sin = _rope_tables(SEQ)
    tabs = {"cos": cos, "sin": sin, "bias_a": _bias_a(), "bias_b": _bias_b()}
    layers = []
    for l in range(DEPTH):
        conv_w_l = conv_all[:, l].reshape(2, N_SHARDS // 2, 3, up_cols).transpose(0, 2, 1, 3).reshape(2, 3, D_FF)
        layers.append({"ln_attn": ln_attn[l][None], "sink_b": sink_b[l], "bias_c": _bias_c(rpb_c[l]),
                       "mix_gain": mix_gain[l][None], "ln_ffn": ln_ffn[l][None], "conv_w": conv_w_l,
                       "conv_b": conv_b[l].reshape(2, 1, D_FF)})

    act = x[0]
    saved = []
    for l in range(DEPTH):
        act, keep = _layer_fwd(act, layers[l], lambda name, after, l=l: gathered(l, name, [after]), tabs)
        saved.append(keep)
    loss_part, dx, dx_b, d_ln_final = _loss_head(act, ln_final[None], loss_target[0], "loss_head")
    loss = lax.psum(loss_part[0, 0], ("x", "y", "c"))

    reductions = []

    def sender(l):
        def send(partial):
            idx = len(reductions)
            names = tuple(partial)
            mine = [partial[k] for k in names]
            theirs = _half_exchange(mine, "half_exchange_%d" % idx)
            pairs = [_half_sum(a, b, "half_sum_" + k) for k, a, b in zip(names, mine, theirs)]
            send_sem, recv_sem, pairs, lands, token = _reduce_start(pairs, "reduce_start_%d" % idx)
            reductions.append({"layer": l, "names": names, "send": send_sem, "recv": recv_sem, "pairs": pairs,
                               "lands": lands})
            return token
        return send

    small = [None] * DEPTH
    after = []
    for l in reversed(range(DEPTH)):
        big = {k: gathered(l, k, []) for k in BIG_NAMES}
        dx, dx_b, small[l], token = _layer_bwd(dx, dx_b, saved[l], layers[l], big, tabs, sender(l), after)
        after = [token]

    grads, delta, new_m, new_v = {}, {}, {}, {}
    reduced = {}
    updated = dict.fromkeys(BIG_NAMES)
    for l in reversed(range(DEPTH)):
        for idx, group in enumerate(reductions):
            if group["layer"] != l:
                continue
            pairs, lands = _reduce_wait(group["send"], group["recv"], group["pairs"], group["lands"], after,
                                        "reduce_wait_%d" % idx)
            for k, pair, landed in zip(group["names"], pairs, lands):
                reduced[k] = _reduce_sum(pair, landed, l, reduced.get(k), "reduce_sum_" + k)
            after = [reduced[group["names"][-1]]]
        reduced = dict(zip(BIG_NAMES, _half_gather([reduced[k] for k in BIG_NAMES], l, "half_gather_%d" % l)))
        for k in BIG_NAMES:
            updated[k] = _adamw_layer(w[k], reduced[k], m[k], v[k], l, updated[k], "adamw_" + k)
        after = [updated[k][0] for k in BIG_NAMES]
    for k in BIG_NAMES:
        grads[k], delta[k], new_m[k], new_v[k] = updated[k]

    stacked = {k: jnp.stack([small[l][k] for l in range(DEPTH)]) for k in small[0]}
    part = {"ln_attn": stacked["ln_attn"][:, 0], "sink_b": stacked["sink_b"], "rpb_c": stacked["rpb_c"],
            "mix_gain": stacked["mix_gain"][:, 0], "ln_ffn": stacked["ln_ffn"][:, 0],
            "conv_b": stacked["conv_b"].reshape(DEPTH, 2 * D_FF), "ln_final": d_ln_final[0],
            "conv_w": stacked["conv_w"].transpose(0, 2, 1, 3).reshape(DEPTH, 3, 2 * D_FF)}
    names = REPLICATED_NAMES + ("conv_w",)
    total = _sum_slots(_all_gather_small(_pack([part[k] for k in names], 256), "gather_small_grads", after),
                       "sum_small")
    for k, g in zip(names, _unpack(total, [part[k].shape for k in names])):
        grads[k] = g
    grads["conv_w"] = lax.dynamic_slice_in_dim(grads["conv_w"], shard * up_cols, up_cols, axis=2)

    flat = (DEPTH * 3, up_cols)
    res = _adamw(conv_w.reshape(flat), grads["conv_w"].reshape(flat), m["conv_w"].reshape(flat),
                 v["conv_w"].reshape(flat), "adamw_conv_w")
    delta["conv_w"], new_m["conv_w"], new_v["conv_w"] = (r.reshape(conv_w.shape) for r in res)
    shapes = [w[k].shape for k in REPLICATED_NAMES]
    packed = [_pack([d[k] for k in REPLICATED_NAMES], 128) for d in (w, grads, m, v)]
    for d, res in zip((delta, new_m, new_v), _adamw(*packed, "adamw_small")):
        for k, r in zip(REPLICATED_NAMES, _unpack(res, shapes)):
            d[k] = r

    return (loss, dx[None], *[grads[k] for k in WEIGHT_NAMES], *[delta[k] for k in WEIGHT_NAMES],
            *[new_m[k] for k in WEIGHT_NAMES], *[new_v[k] for k in WEIGHT_NAMES])
```

```python
import functools
import math

import jax
import jax.numpy as jnp
from jax import lax
from jax.experimental import pallas as pl
from jax.experimental.pallas import tpu as pltpu

F32 = jnp.float32
BF16 = jnp.bfloat16
MESH = pl.DeviceIdType.MESH

D_MODEL = 2048
SEQ = 2048
DEPTH = 2
HEAD_DIM = 64
N_HEADS_A = 12
N_HEADS_B = 10
N_KV_B = 2
N_HEADS_C = 10
WINDOW_B = 128
GRID_W = 64
NA_ROWS = 8
NA_COLS = 16
WIDTH_A = N_HEADS_A * HEAD_DIM
WIDTH_B = N_HEADS_B * HEAD_DIM
WIDTH_C = N_HEADS_C * HEAD_DIM
IN_COLS = 5120
D_FF = 5632
ROPE_THETA = 10000.0
EPS = 1e-6
NEG_INF = -1e30
N_SHARDS = 4

ADAM_LR = 0.001
ADAM_B1 = 0.9
ADAM_B2 = 0.999
ADAM_EPS = 1e-08
ADAM_WD = 0.01
ADAM_STEP = 10

LANES = 128
QB = 256
NQB = SEQ // QB
ROWS = 256
MIB = 2 ** 20

A_BLK = (0, 6, 12)
B_BLK = (18, 23, 24)
C_BLK = (25, 30, 35)
ROPE_BLKS = tuple(range(0, 12)) + tuple(range(18, 24))
QSCALE_BLKS = tuple(range(0, 6)) + tuple(range(18, 23)) + tuple(range(25, 30))
N_PBLK = IN_COLS // LANES


def _params(sem, vmem_mib):
    return pltpu.CompilerParams(dimension_semantics=sem, vmem_limit_bytes=vmem_mib * MIB)


def _weight_spec(w, cols, t_in, t_out, transposed):
    s, r, c = w.shape
    if cols:
        per = c // t_out
        k_dim, n = r, s * c
        if transposed:
            index = lambda j, rr: (rr // per, j, rr % per)
        else:
            index = lambda j, kk: (j // per, kk, j % per)
    else:
        per = r // t_in
        k_dim, n = s * r, c
        if transposed:
            index = lambda j, rr: (j // per, j % per, rr)
        else:
            index = lambda j, kk: (kk // per, kk % per, j)
    return pl.BlockSpec((None, t_in, t_out), index), k_dim, n


def _mm_nn(a, w, *, cols, tn, tk, out_dtype, name, residual=None, out_split=1):
    m, k_dim = a.shape
    w_spec, k_w, n = _weight_spec(w, cols, tk, tn, False)
    assert k_w == k_dim
    nj, nk = n // tn, k_dim // tk
    in_specs = [pl.BlockSpec((m, tk), lambda j, k: (0, k)), w_spec]
    args = [a, w]
    if residual is not None:
        in_specs.append(pl.BlockSpec((m, tn), lambda j, k: (0, j)))
        args.append(residual)
    if out_split > 1:
        per_o = n // out_split // tn
        out_spec = pl.BlockSpec((None, m, tn), lambda j, k: (j // per_o, 0, j % per_o))
        out_shape = jax.ShapeDtypeStruct((out_split, m, n // out_split), out_dtype)
    else:
        out_spec = pl.BlockSpec((m, tn), lambda j, k: (0, j))
        out_shape = jax.ShapeDtypeStruct((m, n), out_dtype)

    def body(*refs):
        a_ref, w_ref = refs[0], refs[1]
        r_ref = refs[2] if residual is not None else None
        o_ref = refs[3] if residual is not None else refs[2]

        def finish(val):
            if r_ref is not None:
                val = r_ref[...] + val
            o_ref[...] = val.astype(o_ref.dtype)

        part = jnp.dot(a_ref[...], w_ref[...], preferred_element_type=F32)
        if nk == 1:
            finish(part)
        else:
            acc = refs[-1]
            kk = pl.program_id(1)

            @pl.when(kk == 0)
            def _():
                acc[...] = part

            @pl.when(kk > 0)
            def _():
                acc[...] += part

            @pl.when(kk == nk - 1)
            def _():
                finish(acc[...])

    return pl.pallas_call(
        body, name=name, grid=(nj, nk), in_specs=in_specs, out_specs=out_spec, out_shape=out_shape,
        scratch_shapes=[pltpu.VMEM((m, tn), F32)] if nk > 1 else [],
        compiler_params=_params(("arbitrary", "arbitrary"), 56),
    )(*args)


ANY_SPEC = pl.BlockSpec(memory_space=pl.ANY)


def _mm_nt(dy, w, *, cols, to, tr, out_dtype, name, after=()):
    if dy.ndim == 3:
        m = dy.shape[1]
        n = dy.shape[0] * dy.shape[2]
        per_d = dy.shape[2] // tr
        dy_spec = pl.BlockSpec((None, m, tr), lambda j, r: (r // per_d, 0, r % per_d))
    else:
        m, n = dy.shape
        dy_spec = pl.BlockSpec((m, tr), lambda j, r: (0, r))
    w_spec, k_dim, n_w = _weight_spec(w, cols, to, tr, True)
    assert n_w == n
    nj, nr = k_dim // to, n // tr

    n_after = len(after)

    def body(dy_ref, w_ref, *rest):
        o_ref = rest[n_after]
        part = lax.dot_general(dy_ref[...], w_ref[...], (((1,), (1,)), ((), ())), preferred_element_type=F32)
        if nr == 1:
            o_ref[...] = part.astype(o_ref.dtype)
        else:
            acc = rest[n_after + 1]
            rr = pl.program_id(1)

            @pl.when(rr == 0)
            def _():
                acc[...] = part

            @pl.when(rr > 0)
            def _():
                acc[...] += part

            @pl.when(rr == nr - 1)
            def _():
                o_ref[...] = acc[...].astype(o_ref.dtype)

    return pl.pallas_call(
        body, name=name, grid=(nj, nr), in_specs=[dy_spec, w_spec] + [ANY_SPEC] * n_after,
        out_specs=pl.BlockSpec((m, to), lambda j, r: (0, j)),
        out_shape=jax.ShapeDtypeStruct((m, k_dim), out_dtype),
        scratch_shapes=[pltpu.VMEM((m, to), F32)] if nr > 1 else [],
        compiler_params=_params(("arbitrary", "arbitrary"), 56),
    )(dy, w, *after)


def _mm_tn(x, dy, *, tk, tn, shards, name):
    m, k_dim = x.shape
    if dy.ndim == 3:
        n = dy.shape[0] * dy.shape[2]
        per_d = dy.shape[2] // tn
        dy_spec = pl.BlockSpec((None, m, tn), lambda i, j: (j // per_d, 0, j % per_d))
    else:
        n = dy.shape[1]
        dy_spec = pl.BlockSpec((m, tn), lambda i, j: (0, j))
    if shards > 0:
        per = n // shards // tn
        out_shape = jax.ShapeDtypeStruct((shards, k_dim, n // shards), BF16)
        out_spec = pl.BlockSpec((None, tk, tn), lambda i, j: (j // per, i, j % per))
    else:
        s = -shards
        per = k_dim // s // tk
        out_shape = jax.ShapeDtypeStruct((s, k_dim // s, n), BF16)
        out_spec = pl.BlockSpec((None, tk, tn), lambda i, j: (i // per, i % per, j))

    def body(x_ref, dy_ref, o_ref):
        o_ref[...] = lax.dot_general(x_ref[...], dy_ref[...], (((0,), (0,)), ((), ())),
                                     preferred_element_type=F32).astype(BF16)

    return pl.pallas_call(
        body, name=name, grid=(k_dim // tk, n // tn),
        in_specs=[pl.BlockSpec((m, tk), lambda i, j: (0, i)), dy_spec], out_specs=out_spec, out_shape=out_shape,
        compiler_params=_params(("arbitrary", "arbitrary"), 56),
    )(x, dy)


def _row_spec(width, rows=ROWS):
    return pl.BlockSpec((rows, width), lambda i: (i, 0))


def _vec_spec(width):
    return pl.BlockSpec((1, width), lambda i: (0, 0))


def _rms_stats(x):
    r = lax.rsqrt(jnp.mean(x * x, axis=-1, keepdims=True) + EPS)
    return r, x * r


def _rmsnorm_fwd(x, gain, name):
    t, d = x.shape

    def body(x_ref, g_ref, o_ref):
        _, n = _rms_stats(x_ref[...])
        o_ref[...] = (n * g_ref[...]).astype(BF16)

    return pl.pallas_call(
        body, name=name, grid=(t // ROWS,), in_specs=[_row_spec(d), _vec_spec(d)], out_specs=_row_spec(d),
        out_shape=jax.ShapeDtypeStruct((t, d), BF16), compiler_params=_params(("arbitrary",), 32),
    )(x, gain)


def _rmsnorm_bwd(x, gain, dh, dres, name, after=()):
    t, d = x.shape
    n_after = len(after)

    def body(x_ref, g_ref, dh_ref, dres_ref, *rest):
        dx_ref, dxb_ref, dg_ref = rest[n_after:]
        r, n = _rms_stats(x_ref[...])
        dh_v = dh_ref[...]
        dn = dh_v * g_ref[...]
        dx = dres_ref[...] + r * (dn - n * jnp.mean(dn * n, axis=-1, keepdims=True))
        dx_ref[...] = dx
        dxb_ref[...] = dx.astype(BF16)
        part = jnp.sum(dh_v * n, axis=0, keepdims=True)

        @pl.when(pl.program_id(0) == 0)
        def _():
            dg_ref[...] = part

        @pl.when(pl.program_id(0) > 0)
        def _():
            dg_ref[...] += part

    return pl.pallas_call(
        body, name=name, grid=(t // ROWS,),
        in_specs=[_row_spec(d), _vec_spec(d), _row_spec(d), _row_spec(d)] + [ANY_SPEC] * n_after,
        out_specs=[_row_spec(d), _row_spec(d), _vec_spec(d)],
        out_shape=[jax.ShapeDtypeStruct((t, d), F32), jax.ShapeDtypeStruct((t, d), BF16),
                   jax.ShapeDtypeStruct((1, d), F32)],
        compiler_params=_params(("arbitrary",), 40),
    )(x, gain, dh, dres, *after)


def _loss_head(x, gain, target, name):
    t, d = x.shape

    def body(x_ref, g_ref, t_ref, loss_ref, dx_ref, dxb_ref, dg_ref):
        r, n = _rms_stats(x_ref[...])
        g = g_ref[...]
        err = n * g - t_ref[...]
        dy = err * (1.0 / d)
        dn = dy * g
        dx = r * (dn - n * jnp.mean(dn * n, axis=-1, keepdims=True))
        dx_ref[...] = dx
        dxb_ref[...] = dx.astype(BF16)
        part = jnp.sum(dy * n, axis=0, keepdims=True)
        lpart = jnp.zeros((8, LANES), F32) + 0.5 * jnp.sum(jnp.mean(err * err, axis=-1, keepdims=True))

        @pl.when(pl.program_id(0) == 0)
        def _():
            dg_ref[...] = part
            loss_ref[...] = lpart

        @pl.when(pl.program_id(0) > 0)
        def _():
            dg_ref[...] += part
            loss_ref[...] += lpart

    return pl.pallas_call(
        body, name=name, grid=(t // ROWS,),
        in_specs=[_row_spec(d), _vec_spec(d), _row_spec(d)],
        out_specs=[pl.BlockSpec((8, LANES), lambda i: (0, 0)), _row_spec(d), _row_spec(d), _vec_spec(d)],
        out_shape=[jax.ShapeDtypeStruct((8, LANES), F32), jax.ShapeDtypeStruct((t, d), F32),
                   jax.ShapeDtypeStruct((t, d), BF16), jax.ShapeDtypeStruct((1, d), F32)],
        compiler_params=_params(("arbitrary",), 40),
    )(x, gain, target)


def _swap_halves(x):
    lane = lax.broadcasted_iota(jnp.int32, x.shape, 1)
    return jnp.where((lane % HEAD_DIM) < HEAD_DIM // 2, pltpu.roll(x, LANES - HEAD_DIM // 2, 1),
                     pltpu.roll(x, HEAD_DIM // 2, 1))


def _rope_tables(t):
    inv_freq = ROPE_THETA ** (-jnp.arange(0, HEAD_DIM, 2, dtype=F32) / HEAD_DIM)
    ang = jnp.arange(t, dtype=F32)[:, None] * inv_freq[None, :]
    cos = jnp.tile(jnp.cos(ang), (1, LANES // (HEAD_DIM // 2)))
    sin = jnp.tile(jnp.sin(ang), (1, LANES // (HEAD_DIM // 2)))
    lane = jnp.arange(LANES)[None, :]
    return cos, jnp.where((lane % HEAD_DIM) < HEAD_DIM // 2, -sin, sin)


def _rope_fwd(proj, cos, sin, name):
    t = proj.shape[0]
    scale = HEAD_DIM ** -0.5

    def body(p_ref, c_ref, s_ref, o_ref):
        cos_v, sin_v = c_ref[...], s_ref[...]
        for b in range(N_PBLK):
            cols = slice(b * LANES, (b + 1) * LANES)
            v = p_ref[:, cols]
            if b in ROPE_BLKS:
                v = v * cos_v + _swap_halves(v) * sin_v
            if b in QSCALE_BLKS:
                v = v * scale
            o_ref[:, cols] = v.astype(BF16)

    return pl.pallas_call(
        body, name=name, grid=(t // ROWS,),
        in_specs=[_row_spec(IN_COLS), _row_spec(LANES), _row_spec(LANES)], out_specs=_row_spec(IN_COLS),
        out_shape=jax.ShapeDtypeStruct((t, IN_COLS), BF16), compiler_params=_params(("arbitrary",), 40),
    )(proj, cos, sin)


def _rope_bwd(grads, cos, sin, name):
    t = grads[0].shape[0]
    scale = HEAD_DIM ** -0.5
    group = N_HEADS_B // N_KV_B

    def body(*refs):
        c_ref, s_ref, o_ref = refs[9], refs[10], refs[11]
        cos_v, sin_v = c_ref[...], s_ref[...]

        def kv_sum(ref):
            parts = []
            for g in range(N_KV_B):
                acc = ref[:, g * group * HEAD_DIM:(g * group + 1) * HEAD_DIM]
                for h in range(g * group + 1, (g + 1) * group):
                    acc = acc + ref[:, h * HEAD_DIM:(h + 1) * HEAD_DIM]
                parts.append(acc)
            return jnp.concatenate(parts, axis=1)

        def emit(b, v):
            if b in ROPE_BLKS:
                v = v * cos_v - _swap_halves(v) * sin_v
            if b in QSCALE_BLKS:
                v = v * scale
            o_ref[:, b * LANES:(b + 1) * LANES] = v.astype(BF16)

        starts = (A_BLK[0], A_BLK[1], A_BLK[2], B_BLK[0], None, None, C_BLK[0], C_BLK[1], C_BLK[2])
        for idx, start in enumerate(starts):
            if start is None:
                continue
            for j in range(refs[idx].shape[1] // LANES):
                emit(start + j, refs[idx][:, j * LANES:(j + 1) * LANES])
        emit(B_BLK[1], kv_sum(refs[4]))
        emit(B_BLK[2], kv_sum(refs[5]))

    return pl.pallas_call(
        body, name=name, grid=(t // ROWS,),
        in_specs=[_row_spec(g.shape[1]) for g in grads] + [_row_spec(LANES), _row_spec(LANES)],
        out_specs=_row_spec(IN_COLS),
        out_shape=jax.ShapeDtypeStruct((t, IN_COLS), BF16), compiler_params=_params(("arbitrary",), 40),
    )(*grads, cos, sin)


GROUP_COLS = ((0, WIDTH_A), (WIDTH_A, WIDTH_A + WIDTH_B), (WIDTH_A + WIDTH_B, D_MODEL))


def _mix_fwd(oa, ob, oc, gain, name):
    t = oa.shape[0]

    def body(a_ref, b_ref, c_ref, g_ref, o_ref):
        for ref, (lo, hi) in zip((a_ref, b_ref, c_ref), GROUP_COLS):
            _, n = _rms_stats(ref[...])
            o_ref[:, lo:hi] = (n * g_ref[:, lo:hi]).astype(BF16)

    return pl.pallas_call(
        body, name=name, grid=(t // ROWS,),
        in_specs=[_row_spec(WIDTH_A), _row_spec(WIDTH_B), _row_spec(WIDTH_C), _vec_spec(D_MODEL)],
        out_specs=_row_spec(D_MODEL),
        out_shape=jax.ShapeDtypeStruct((t, D_MODEL), BF16), compiler_params=_params(("arbitrary",), 32),
    )(oa, ob, oc, gain)


def _mix_bwd(oa, ob, oc, gain, dmixed, name, after=()):
    t = oa.shape[0]
    n_after = len(after)

    def body(a_ref, b_ref, c_ref, g_ref, dm_ref, *rest):
        da_ref, db_ref, dc_ref, dg_ref = rest[n_after:]
        first = pl.program_id(0) == 0
        for ref, dref, (lo, hi) in zip((a_ref, b_ref, c_ref), (da_ref, db_ref, dc_ref), GROUP_COLS):
            r, n = _rms_stats(ref[...])
            dm = dm_ref[:, lo:hi]
            dn = dm * g_ref[:, lo:hi]
            dref[...] = r * (dn - n * jnp.mean(dn * n, axis=-1, keepdims=True))
            part = jnp.sum(dm * n, axis=0, keepdims=True)

            @pl.when(first)
            def _():
                dg_ref[:, lo:hi] = part

            @pl.when(jnp.logical_not(first))
            def _():
                dg_ref[:, lo:hi] += part

    return pl.pallas_call(
        body, name=name, grid=(t // ROWS,),
        in_specs=[_row_spec(WIDTH_A), _row_spec(WIDTH_B), _row_spec(WIDTH_C), _vec_spec(D_MODEL), _row_spec(D_MODEL)]
        + [ANY_SPEC] * n_after,
        out_specs=[_row_spec(WIDTH_A), _row_spec(WIDTH_B), _row_spec(WIDTH_C), _vec_spec(D_MODEL)],
        out_shape=[jax.ShapeDtypeStruct((t, WIDTH_A), F32), jax.ShapeDtypeStruct((t, WIDTH_B), F32),
                   jax.ShapeDtypeStruct((t, WIDTH_C), F32), jax.ShapeDtypeStruct((1, D_MODEL), F32)],
        compiler_params=_params(("arbitrary",), 40),
    )(oa, ob, oc, gain, dmixed, *after)


FF_COLS = 256


SUBLANES = 8
CHUNK = 128
HALO = SUBLANES


def _pad_rows(dst_ref, src_ref):
    t, cols = src_ref.shape
    dst_ref[0:HALO, :] = jnp.zeros((HALO, cols), F32)
    dst_ref[HALO:HALO + t, :] = src_ref[...]
    dst_ref[HALO + t:t + 2 * HALO, :] = jnp.zeros((HALO, cols), F32)


def _roll_rows(x, by):
    return pltpu.roll(x, by % x.shape[0], 0)


def _gate_val(pad_ref, r0, w_ref, b_ref):
    ext = [pad_ref[h, pl.ds(r0, CHUNK + 2 * HALO), :] for h in range(2)]
    before = [_roll_rows(e, 1) for e in ext]
    after = [_roll_rows(e, -1) for e in ext]
    gate, val = ((before[h] * w_ref[h, 0:1, :] + ext[h] * w_ref[h, 1:2, :]) + after[h] * w_ref[h, 2:3, :] + b_ref[h]
                 for h in range(2))
    return gate, val, ext, before, after


def _ff_specs(t):
    u_spec = pl.BlockSpec((2, t, FF_COLS), lambda j: (0, 0, j))
    w_spec = pl.BlockSpec((2, 3, FF_COLS), lambda j: (0, 0, j))
    b_spec = pl.BlockSpec((2, 1, FF_COLS), lambda j: (0, 0, j))
    return u_spec, w_spec, b_spec


def _convgate_fwd(u0, conv_w, conv_b, name):
    t = u0.shape[1]
    u_spec, w_spec, b_spec = _ff_specs(t)

    def body(u_ref, w_ref, b_ref, o_ref, pad_ref):
        for h in range(2):
            _pad_rows(pad_ref.at[h], u_ref.at[h])

        def chunk(ci, carry):
            r0 = pl.multiple_of(ci * CHUNK, CHUNK)
            gate, val, _, _, _ = _gate_val(pad_ref, r0, w_ref, b_ref)
            act = gate * jax.nn.sigmoid(gate) * val
            o_ref[pl.ds(r0, CHUNK), :] = act[HALO:HALO + CHUNK].astype(BF16)
            return carry

        lax.fori_loop(0, t // CHUNK, chunk, 0)

    return pl.pallas_call(
        body, name=name, grid=(D_FF // FF_COLS,), in_specs=[u_spec, w_spec, b_spec],
        out_specs=pl.BlockSpec((t, FF_COLS), lambda j: (0, j)),
        out_shape=jax.ShapeDtypeStruct((t, D_FF), BF16),
        scratch_shapes=[pltpu.VMEM((2, t + 2 * HALO, FF_COLS), F32)],
        compiler_params=_params(("arbitrary",), 48),
    )(u0, conv_w, conv_b)


def _convgate_bwd(u0, conv_w, conv_b, d_act, name):
    t = u0.shape[1]
    u_spec, w_spec, b_spec = _ff_specs(t)

    def body(u_ref, w_ref, b_ref, da_ref, du_ref, dw_ref, db_ref, pad_ref, da_pad_ref, sums_ref):
        for h in range(2):
            _pad_rows(pad_ref.at[h], u_ref.at[h])
        _pad_rows(da_pad_ref, da_ref)
        sums_ref[...] = jnp.zeros_like(sums_ref)
        inner = slice(HALO, HALO + CHUNK)

        def fold(x):
            return jnp.sum(x.reshape(CHUNK // SUBLANES, SUBLANES, x.shape[1]), axis=0)

        def chunk(ci, carry):
            r0 = pl.multiple_of(ci * CHUNK, CHUNK)
            gate, val, ext, before, after = _gate_val(pad_ref, r0, w_ref, b_ref)
            sig = jax.nn.sigmoid(gate)
            da = da_pad_ref[pl.ds(r0, CHUNK + 2 * HALO), :]
            d_half = (da * val * (sig * (1.0 + gate * (1.0 - sig))), da * (gate * sig))
            for h in range(2):
                du = d_half[h]
                for k, term in enumerate((du, du * before[h], du * ext[h], du * after[h])):
                    sums_ref[h, k] += fold(term[inner])
                du0 = (_roll_rows(du, -1) * w_ref[h, 0:1, :] + du * w_ref[h, 1:2, :]) + _roll_rows(du, 1) * w_ref[h, 2:3, :]
                du_ref[h, pl.ds(r0, CHUNK), :] = du0[inner].astype(BF16)
            return carry

        lax.fori_loop(0, t // CHUNK, chunk, 0)
        for h in range(2):
            db_ref[h] = jnp.sum(sums_ref[h, 0], axis=0, keepdims=True)
            for k in range(3):
                dw_ref[h, k:k + 1, :] = jnp.sum(sums_ref[h, k + 1], axis=0, keepdims=True)

    return pl.pallas_call(
        body, name=name, grid=(D_FF // FF_COLS,),
        in_specs=[u_spec, w_spec, b_spec, pl.BlockSpec((t, FF_COLS), lambda j: (0, j))],
        out_specs=[u_spec, w_spec, b_spec],
        out_shape=[jax.ShapeDtypeStruct((2, t, D_FF), BF16), jax.ShapeDtypeStruct((2, 3, D_FF), F32),
                   jax.ShapeDtypeStruct((2, 1, D_FF), F32)],
        scratch_shapes=[pltpu.VMEM((2, t + 2 * HALO, FF_COLS), F32), pltpu.VMEM((t + 2 * HALO, FF_COLS), F32),
                        pltpu.VMEM((2, 4, SUBLANES, FF_COLS), F32)],
        compiler_params=_params(("arbitrary",), 56),
    )(u0, conv_w, conv_b, d_act)


class _Group:
    def __init__(self, heads, blks, kv_rows, n_win, gqa, bias_per_head):
        self.heads = heads
        self.pairs = heads // 2
        self.q_blk, self.k_blk, self.v_blk = blks
        self.kv_rows = kv_rows
        self.n_win = n_win
        self.full = kv_rows == SEQ
        self.gqa = gqa
        self.bias_per_head = bias_per_head
        self.width = heads * HEAD_DIM
        self.keys = kv_rows * n_win


GROUP_A = _Group(N_HEADS_A, A_BLK, SEQ, 1, False, False)
GROUP_B = _Group(N_HEADS_B, B_BLK, WINDOW_B, 4, True, False)
GROUP_C = _Group(N_HEADS_C, C_BLK, QB, 3, False, True)


def _win_start(grp, i):
    return jnp.clip(i * (QB // grp.kv_rows) - 1, 0, SEQ // grp.kv_rows - grp.n_win)


def _win_variant(i):
    return jnp.minimum(i, 1) + (i == NQB - 1).astype(jnp.int32)


def _attn_in_specs(grp, t):
    q_spec = pl.BlockSpec((QB, LANES), lambda p, i: (i, grp.q_blk + p))

    def col(blk):
        return (lambda p: blk) if grp.gqa else (lambda p: blk + p)

    def kv_specs(blk):
        c = col(blk)
        if grp.full:
            return [pl.BlockSpec((t, LANES), lambda p, i: (0, c(p)))]
        return [pl.BlockSpec((grp.kv_rows, LANES),
                             functools.partial(lambda p, i, w: (_win_start(grp, i) + w, c(p)), w=w))
                for w in range(grp.n_win)]

    nwk = grp.keys
    if grp.bias_per_head:
        bias_spec = pl.BlockSpec((2, None, QB, nwk), lambda p, i: (p, _win_variant(i), 0, 0))
    elif grp.full:
        bias_spec = pl.BlockSpec((1, None, QB, nwk), lambda p, i: (0, i, 0, 0))
    else:
        bias_spec = pl.BlockSpec((1, None, QB, nwk), lambda p, i: (0, _win_variant(i), 0, 0))
    sink_spec = pl.BlockSpec((1, LANES), lambda p, i: (0, p))
    return q_spec, kv_specs(grp.k_blk), kv_specs(grp.v_blk), bias_spec, sink_spec


def _head_kv(grp, whole, e, p):
    lo, hi = whole[:, :HEAD_DIM], whole[:, HEAD_DIM:]
    if grp.gqa:
        return jnp.where(2 * p + e >= N_HEADS_B // N_KV_B, hi, lo)
    return hi if e else lo


def _softmax_parts(q, k, bias, sink):
    s = lax.dot_general(q, k, (((1,), (1,)), ((), ())), preferred_element_type=F32) + bias
    m = jnp.maximum(jnp.max(s, axis=-1, keepdims=True), sink)
    pe = jnp.exp(s - m)
    denom = jnp.sum(pe, axis=-1, keepdims=True) + jnp.exp(sink - m)
    return pe, m, 1.0 / denom


def _attn_fwd(grp, proj, bias, sink, name):
    t = proj.shape[0]
    q_spec, k_specs, v_specs, bias_spec, sink_spec = _attn_in_specs(grp, t)
    nkv = len(k_specs)

    def body(*refs):
        q_ref = refs[0]
        k_refs, v_refs = refs[1:1 + nkv], refs[1 + nkv:1 + 2 * nkv]
        bias_ref, sink_ref, o_ref = refs[1 + 2 * nkv:4 + 2 * nkv]
        p = pl.program_id(0)
        k_all = jnp.concatenate([r[...] for r in k_refs], axis=0)
        v_all = jnp.concatenate([r[...] for r in v_refs], axis=0)
        outs = []
        for e in range(2):
            q = q_ref[:, e * HEAD_DIM:(e + 1) * HEAD_DIM]
            k = _head_kv(grp, k_all, e, p)
            v = _head_kv(grp, v_all, e, p)
            snk = sink_ref[0:1, e * HEAD_DIM:e * HEAD_DIM + 1]
            pe, _, inv = _softmax_parts(q, k, bias_ref[e if grp.bias_per_head else 0], snk)
            outs.append(jnp.dot(pe.astype(BF16), v, preferred_element_type=F32) * inv)
        o_ref[...] = jnp.concatenate(outs, axis=1)

    return pl.pallas_call(
        body, name=name, grid=(grp.pairs, NQB),
        in_specs=[q_spec, *k_specs, *v_specs, bias_spec, sink_spec],
        out_specs=pl.BlockSpec((QB, LANES), lambda p, i: (i, p)),
        out_shape=jax.ShapeDtypeStruct((t, grp.width), F32),
        compiler_params=_params(("arbitrary", "arbitrary"), 48),
    )(proj, *([proj] * (2 * nkv)), bias, sink)


def _attn_bwd(grp, proj, bias, sink, out, d_out, name):
    t = proj.shape[0]
    q_spec, k_specs, v_specs, bias_spec, sink_spec = _attn_in_specs(grp, t)
    nkv = len(k_specs)
    n_off = 2 * NA_ROWS - 1
    rows_q = QB // GRID_W
    o_spec = pl.BlockSpec((QB, LANES), lambda p, i: (i, p))
    acc_spec = pl.BlockSpec((t, LANES), lambda p, i: (0, p))
    out_specs = [o_spec, acc_spec, acc_spec, pl.BlockSpec((None, 8, LANES), lambda p, i: (p, 0, 0))]
    out_shape = [jax.ShapeDtypeStruct((t, grp.width), F32)] * 3 + [jax.ShapeDtypeStruct((grp.pairs, 8, LANES), F32)]
    if grp.bias_per_head:
        out_specs.append(pl.BlockSpec((2, n_off, GRID_W, GRID_W), lambda p, i: (p, 0, 0, 0)))
        out_shape.append(jax.ShapeDtypeStruct((grp.heads, n_off, GRID_W, GRID_W), F32))

    def body(*refs):
        q_ref = refs[0]
        k_refs, v_refs = refs[1:1 + nkv], refs[1 + nkv:1 + 2 * nkv]
        bias_ref, sink_ref, o_ref, do_ref = refs[1 + 2 * nkv:5 + 2 * nkv]
        dq_ref, dk_ref, dv_ref, dsink_ref = refs[5 + 2 * nkv:9 + 2 * nkv]
        dbias_ref = refs[9 + 2 * nkv] if grp.bias_per_head else None
        p, i = pl.program_id(0), pl.program_id(1)

        @pl.when(i == 0)
        def _():
            dk_ref[...] = jnp.zeros_like(dk_ref)
            dv_ref[...] = jnp.zeros_like(dv_ref)
            dsink_ref[...] = jnp.zeros_like(dsink_ref)
            if dbias_ref is not None:
                dbias_ref[...] = jnp.zeros_like(dbias_ref)

        k_all = jnp.concatenate([r[...] for r in k_refs], axis=0)
        v_all = jnp.concatenate([r[...] for r in v_refs], axis=0)
        start = 0 if grp.full else _win_start(grp, i)
        dqs, dks, dvs, dsinks = [], [], [], []
        for e in range(2):
            cols = slice(e * HEAD_DIM, (e + 1) * HEAD_DIM)
            q = q_ref[:, cols]
            k = _head_kv(grp, k_all, e, p)
            v = _head_kv(grp, v_all, e, p)
            snk = sink_ref[0:1, e * HEAD_DIM:e * HEAD_DIM + 1]
            pe, m, inv = _softmax_parts(q, k, bias_ref[e if grp.bias_per_head else 0], snk)
            prob = pe * inv
            do = do_ref[:, cols]
            do_b = do.astype(BF16)
            delta = jnp.sum(do * o_ref[:, cols], axis=-1, keepdims=True)
            dp = lax.dot_general(do_b, v, (((1,), (1,)), ((), ())), preferred_element_type=F32)
            ds = prob * (dp - delta)
            ds_b = ds.astype(BF16)
            dqs.append(jnp.dot(ds_b, k, preferred_element_type=F32))
            dks.append(lax.dot_general(ds_b, q, (((0,), (0,)), ((), ())), preferred_element_type=F32))
            dvs.append(lax.dot_general(prob.astype(BF16), do_b, (((0,), (0,)), ((), ())), preferred_element_type=F32))
            dsinks.append(-jnp.sum(jnp.exp(snk - m) * inv * delta, axis=0, keepdims=True))
            if dbias_ref is not None:
                shift = (i * QB - start * grp.kv_rows) // GRID_W
                for rq in range(rows_q):
                    for rk in range(grp.keys // GRID_W):
                        off = jnp.clip(rk - rq + (NA_ROWS - 1) - shift, 0, n_off - 1)
                        dbias_ref[e, off] += ds[rq * GRID_W:(rq + 1) * GRID_W, rk * GRID_W:(rk + 1) * GRID_W]
        dq_ref[...] = jnp.concatenate(dqs, axis=1)
        rows = pl.ds(0, t) if grp.full else pl.ds(pl.multiple_of(start * grp.kv_rows, grp.kv_rows), grp.keys)
        dk_ref[rows, :] += jnp.concatenate(dks, axis=1)
        dv_ref[rows, :] += jnp.concatenate(dvs, axis=1)
        lane = lax.broadcasted_iota(jnp.int32, (8, LANES), 1)
        dsink_ref[...] += jnp.where(lane < HEAD_DIM, dsinks[0], dsinks[1])

    return pl.pallas_call(
        body, name=name, grid=(grp.pairs, NQB),
        in_specs=[q_spec, *k_specs, *v_specs, bias_spec, sink_spec, o_spec, o_spec],
        out_specs=out_specs, out_shape=out_shape,
        compiler_params=_params(("arbitrary", "arbitrary"), 56),
    )(proj, *([proj] * (2 * nkv)), bias, sink, out, d_out)


DILATED_CONFIGS = ((128, 1), (512, 4), (2048, 16))


def _bias_a():
    d = jnp.arange(SEQ)[None, :] - jnp.arange(SEQ)[:, None]
    mult = jnp.zeros((SEQ, SEQ), F32)
    for window, r in DILATED_CONFIGS:
        reach = (window // (2 * r)) * r
        mult = mult + ((d % r == 0) & (jnp.abs(d) <= reach)).astype(F32)
    return jnp.where(mult > 0, jnp.log(jnp.maximum(mult, 1.0)), NEG_INF).reshape(1, NQB, QB, SEQ)


def _bias_b():
    row = jnp.arange(QB)[None, :, None]
    col = jnp.arange(GROUP_B.keys)[None, None, :]
    var = jnp.arange(3)[:, None, None]
    d = col - (GROUP_B.kv_rows * var + row)
    return jnp.where(jnp.abs(d) <= WINDOW_B, 0.0, NEG_INF).astype(F32)[None]


def _offset_onehot():
    c = jnp.arange(GRID_W)[:, None, None]
    c2 = jnp.arange(GRID_W)[None, :, None]
    b = jnp.arange(LANES)[None, None, :]
    return (c2 - c + NA_COLS - 1 == b).astype(BF16).reshape(GRID_W * GRID_W, LANES)


def _split_dot(x, g):
    hi = x.astype(BF16)
    rest = x - hi.astype(F32)
    mid = rest.astype(BF16)
    lo = (rest - mid.astype(F32)).astype(BF16)
    return (jnp.dot(hi, g, preferred_element_type=F32) + jnp.dot(mid, g, preferred_element_type=F32)
            + jnp.dot(lo, g, preferred_element_type=F32))


def _table_mm(x, g, name):
    def body(x_ref, g_ref, o_ref):
        o_ref[...] = _split_dot(x_ref[...], g_ref[...])

    return pl.pallas_call(
        body, name=name, out_shape=jax.ShapeDtypeStruct((x.shape[0], g.shape[1]), F32),
        in_specs=[pl.BlockSpec(memory_space=pltpu.VMEM)] * 2, out_specs=pl.BlockSpec(memory_space=pltpu.VMEM),
        compiler_params=pltpu.CompilerParams(vmem_limit_bytes=32 * MIB),
    )(x, g)


N_OFF = 2 * NA_ROWS - 1
TABLE_ROWS = 152


def _bias_c(rpb):
    table = jnp.zeros((TABLE_ROWS, LANES), F32).at[:N_HEADS_C * N_OFF, :2 * NA_COLS - 1].set(
        rpb.reshape(N_HEADS_C * N_OFF, 2 * NA_COLS - 1))
    tiles = _table_mm(table, _offset_onehot().T, "rpb_tiles")[:N_HEADS_C * N_OFF]
    tiles = tiles.reshape(N_HEADS_C, N_OFF, GRID_W, GRID_W)
    c = jnp.arange(GRID_W)
    col_start = jnp.clip(c - NA_COLS // 2, 0, GRID_W - NA_COLS)
    col_ok = (c[None, :] >= col_start[:, None]) & (c[None, :] < col_start[:, None] + NA_COLS)
    tiles = jnp.where(col_ok, tiles, NEG_INF)
    rows_q = QB // GRID_W
    rows_k = GROUP_C.keys // GRID_W

    def body(t_ref, o_ref):
        for var in range(3):
            for rq in range(rows_q):
                r_l = rows_q * var + rq
                first = min(max(r_l - NA_ROWS // 2, 0), rows_k - NA_ROWS)
                for rk in range(rows_k):
                    if first <= rk < first + NA_ROWS:
                        tile = t_ref[rk - r_l + NA_ROWS - 1]
                    else:
                        tile = jnp.full((GRID_W, GRID_W), NEG_INF, F32)
                    o_ref[var, rq * GRID_W:(rq + 1) * GRID_W, rk * GRID_W:(rk + 1) * GRID_W] = tile

    return pl.pallas_call(
        body, name="bias_c", grid=(N_HEADS_C,),
        in_specs=[pl.BlockSpec((None, N_OFF, GRID_W, GRID_W), lambda h: (h, 0, 0, 0))],
        out_specs=pl.BlockSpec((None, 3, QB, GROUP_C.keys), lambda h: (h, 0, 0, 0)),
        out_shape=jax.ShapeDtypeStruct((N_HEADS_C, 3, QB, GROUP_C.keys), F32),
        compiler_params=_params(("arbitrary",), 32),
    )(tiles)


def _rpb_grad(d_tiles):
    flat = jnp.zeros((TABLE_ROWS, GRID_W * GRID_W), F32).at[:N_HEADS_C * N_OFF].set(
        d_tiles.reshape(N_HEADS_C * N_OFF, GRID_W * GRID_W))
    out = _table_mm(flat, _offset_onehot(), "rpb_grad")
    return out[:N_HEADS_C * N_OFF, :2 * NA_COLS - 1].reshape(N_HEADS_C, N_OFF, 2 * NA_COLS - 1)


def _sink_lanes(sink):
    return jnp.repeat(sink.astype(F32), HEAD_DIM)[None, :]


def _attention_fwd(proj_r, sink_b, bias_a, bias_b, bias_c):
    no_sink_a = jnp.full((1, WIDTH_A), NEG_INF, F32)
    no_sink_c = jnp.full((1, WIDTH_C), NEG_INF, F32)
    oa = _attn_fwd(GROUP_A, proj_r, bias_a, no_sink_a, "attn_a_fwd")
    ob = _attn_fwd(GROUP_B, proj_r, bias_b, _sink_lanes(sink_b), "attn_b_fwd")
    oc = _attn_fwd(GROUP_C, proj_r, bias_c, no_sink_c, "attn_c_fwd")
    return oa, ob, oc


def _attention_bwd(proj_r, sink_b, bias_a, bias_b, bias_c, outs, d_outs, cos, sin):
    no_sink_a = jnp.full((1, WIDTH_A), NEG_INF, F32)
    no_sink_c = jnp.full((1, WIDTH_C), NEG_INF, F32)
    dqa, dka, dva, _ = _attn_bwd(GROUP_A, proj_r, bias_a, no_sink_a, outs[0], d_outs[0], "attn_a_bwd")
    dqb, dkb, dvb, dsink = _attn_bwd(GROUP_B, proj_r, bias_b, _sink_lanes(sink_b), outs[1], d_outs[1], "attn_b_bwd")
    dqc, dkc, dvc, _, d_tiles = _attn_bwd(GROUP_C, proj_r, bias_c, no_sink_c, outs[2], d_outs[2], "attn_c_bwd")
    d_proj = _rope_bwd((dqa, dka, dva, dqb, dkb, dvb, dqc, dkc, dvc), cos, sin, "rope_bwd")
    d_sink = dsink[:, 0, :].reshape(GROUP_B.pairs, 2, HEAD_DIM)[:, :, 0].reshape(N_HEADS_B)
    return d_proj, d_sink, _rpb_grad(d_tiles)


def _adamw(w, g, m, v, name):
    r, c = w.shape
    rows = r
    for cand in (512, 256, 128, 64, 32, 16, 8):
        if r % cand == 0 and cand * c * 4 <= MIB:
            rows = cand
            break
    spec = pl.BlockSpec((rows, c), lambda i: (i, 0))

    def body(w_ref, g_ref, m_ref, v_ref, d_ref, mo_ref, vo_ref):
        d_ref[...], mo_ref[...], vo_ref[...] = _adamw_step(w_ref[...], g_ref[...], m_ref[...], v_ref[...])

    return pl.pallas_call(
        body, name=name, grid=(r // rows,), in_specs=[spec] * 4, out_specs=[spec] * 3,
        out_shape=[jax.ShapeDtypeStruct((r, c), F32)] * 3, compiler_params=_params(("arbitrary",), 32),
    )(w, g, m, v)


def _adamw_step(w, grad, m, v):
    m_new = ADAM_B1 * m + (1.0 - ADAM_B1) * grad
    v_new = ADAM_B2 * v + (1.0 - ADAM_B2) * jnp.square(grad)
    m_hat = m_new / (1.0 - ADAM_B1 ** ADAM_STEP)
    v_hat = v_new / (1.0 - ADAM_B2 ** ADAM_STEP)
    return -ADAM_LR * (m_hat / (jnp.sqrt(v_hat) + ADAM_EPS) + ADAM_WD * w), m_new, v_new


def _adamw_layer(w, g, m, v, layer, prev, name):
    _, r, c = w.shape
    rows = next(cand for cand in (512, 256, 128, 64, 32, 16, 8) if r % cand == 0 and cand * c * 4 <= MIB)
    spec = pl.BlockSpec((None, rows, c), lambda i: (layer, i, 0))
    n_prev = 0 if prev is None else 4

    def body(w_ref, g_ref, m_ref, v_ref, *rest):
        go_ref, d_ref, mo_ref, vo_ref = rest[n_prev:]
        grad = g_ref[...]
        go_ref[...] = grad
        d_ref[...], mo_ref[...], vo_ref[...] = _adamw_step(w_ref[...], grad, m_ref[...], v_ref[...])

    return pl.pallas_call(
        body, name=name, grid=(r // rows,), in_specs=[spec] * 4 + [ANY_SPEC] * n_prev, out_specs=[spec] * 4,
        out_shape=[jax.ShapeDtypeStruct(w.shape, F32)] * 4,
        input_output_aliases={4 + i: i for i in range(n_prev)}, compiler_params=_params(("arbitrary",), 32),
    )(w, g, m, v, *(prev or ()))


def _layer_fwd(x0, p, weight, tabs):
    h1 = _rmsnorm_fwd(x0, p["ln_attn"], "ln_attn_fwd")
    proj = _mm_nn(h1, weight("w_in", h1), cols=True, tn=256, tk=D_MODEL, out_dtype=F32, name="mm_in")
    proj_r = _rope_fwd(proj, tabs["cos"], tabs["sin"], "rope_fwd")
    outs = _attention_fwd(proj_r, p["sink_b"], tabs["bias_a"], tabs["bias_b"], p["bias_c"])
    mixed = _mix_fwd(*outs, p["mix_gain"], "mix_fwd")
    x1 = _mm_nn(mixed, weight("w_out", mixed), cols=False, tn=256, tk=D_MODEL, out_dtype=F32, name="mm_out",
                residual=x0)
    h2 = _rmsnorm_fwd(x1, p["ln_ffn"], "ln_ffn_fwd")
    u0 = _mm_nn(h2, weight("w_up", h2), cols=True, tn=256, tk=D_MODEL, out_dtype=F32, name="mm_up", out_split=2)
    act = _convgate_fwd(u0, p["conv_w"], p["conv_b"], "convgate_fwd")
    x2 = _mm_nn(act, weight("w_down", act), cols=False, tn=512, tk=D_FF // 2, out_dtype=F32, name="mm_down",
                residual=x1)
    return x2, (x0, h1, proj_r, outs, mixed, x1, h2, u0, act)


def _layer_bwd(dx2, dx2_b, saved, p, big, tabs, begin, finish, pending):
    x0, h1, proj_r, outs, mixed, x1, h2, u0, act = saved
    d_act = _mm_nt(dx2_b, big["w_down"], cols=False, to=512, tr=D_MODEL, out_dtype=F32, name="nt_down",
                   after=[pending[1]] if pending else [])
    g_down = _mm_tn(act, dx2_b, tk=D_FF // N_SHARDS, tn=512, shards=-N_SHARDS, name="tn_down")
    du0, d_conv_w, d_conv_b = _convgate_bwd(u0, p["conv_w"], p["conv_b"], d_act, "convgate_bwd")
    token = [finish(pending[0], [du0])] if pending else []
    dh2 = _mm_nt(du0, big["w_up"], cols=True, to=1024, tr=D_FF // 4, out_dtype=F32, name="nt_up", after=token)
    g_up = _mm_tn(h2, du0, tk=512, tn=D_FF // 4, shards=N_SHARDS, name="tn_up")
    first, token = begin({"w_down": g_down, "w_up": g_up})
    dx1, dx1_b, d_ln_ffn = _rmsnorm_bwd(x1, p["ln_ffn"], dh2, dx2, "ln_ffn_bwd", after=[token])
    d_mixed = _mm_nt(dx1_b, big["w_out"], cols=False, to=512, tr=D_MODEL, out_dtype=F32, name="nt_out")
    g_out = _mm_tn(mixed, dx1_b, tk=D_MODEL // N_SHARDS, tn=512, shards=-N_SHARDS, name="tn_out")
    token = finish(first, [g_out])
    *d_outs, d_mix_gain = _mix_bwd(*outs, p["mix_gain"], d_mixed, "mix_bwd", after=[token])
    d_proj, d_sink, d_rpb = _attention_bwd(proj_r, p["sink_b"], tabs["bias_a"], tabs["bias_b"], p["bias_c"], outs,
                                           d_outs, tabs["cos"], tabs["sin"])
    dh1 = _mm_nt(d_proj, big["w_in"], cols=True, to=1024, tr=IN_COLS // N_SHARDS, out_dtype=F32, name="nt_in")
    g_in = _mm_tn(h1, d_proj, tk=512, tn=IN_COLS // N_SHARDS, shards=N_SHARDS, name="tn_in")
    dx0, dx0_b, d_ln_attn = _rmsnorm_bwd(x0, p["ln_attn"], dh1, dx1, "ln_attn_bwd")
    small = {"ln_attn": d_ln_attn, "sink_b": d_sink, "rpb_c": d_rpb, "mix_gain": d_mix_gain, "ln_ffn": d_ln_ffn,
             "conv_w": d_conv_w, "conv_b": d_conv_b}
    return dx0, dx0_b, small, begin({"w_out": g_out, "w_in": g_in})


HBM_SPEC = pl.BlockSpec(memory_space=pl.ANY)


def _place():
    x, y, c = lax.axis_index("x"), lax.axis_index("y"), lax.axis_index("c")
    chips = ((1 - x, y), (x, 1 - y), (1 - x, 1 - y))
    return x, y, c, chips


def _shard_index(px, py):
    return 2 * px + py


def _remote(src, dst, send_sem, recv_sem, to):
    return pltpu.make_async_remote_copy(src_ref=src, dst_ref=dst, send_sem=send_sem, recv_sem=recv_sem,
                                        device_id=to, device_id_type=MESH)


def _own_slot(w, layer, shard, name):
    _, r, c_dim = w.shape
    rows = r
    for cand in (512, 256, 128):
        if r % cand == 0 and cand * c_dim * 4 <= 2 * MIB:
            rows = cand
            break

    def body(s_ref, w_ref, o_ref):
        o_ref[...] = w_ref[...].astype(BF16)

    return pl.pallas_call(
        body, name=name,
        grid_spec=pltpu.PrefetchScalarGridSpec(
            num_scalar_prefetch=1, grid=(r // rows,),
            in_specs=[pl.BlockSpec((None, rows, c_dim), lambda i, s: (layer, i, 0))],
            out_specs=pl.BlockSpec((None, rows, c_dim), lambda i, s: (s[0], i, 0))),
        out_shape=jax.ShapeDtypeStruct((N_SHARDS, r, c_dim), BF16),
        compiler_params=_params(("arbitrary",), 32),
    )(shard.astype(jnp.int32).reshape(1), w)


def _gather_weights(bufs):
    n = len(bufs)

    def body(*refs):
        outs = refs[n:2 * n]
        send1, recv1, send2, recv2 = refs[2 * n:]
        x, y, c, chips = _place()
        me = _shard_index(x, y)
        sibling = (x, y, 1 - c)
        first = []
        for t in range(n):
            for j, (px, py) in enumerate(chips):
                mine = outs[t].at[c, me]
                cp = _remote(mine, mine, send1.at[t * 3 + j], recv1.at[t * 3 + j], (px, py, c))
                cp.start()
                first.append(cp)
        passed = []
        for t in range(n):
            for j, (px, py) in enumerate(chips):
                slot = outs[t].at[c, _shard_index(px, py)]
                _remote(slot, slot, send1.at[t * 3 + j], recv1.at[t * 3 + j], (px, py, c)).wait_recv()
                cp = _remote(slot, slot, send2.at[t * 3 + j], recv2.at[t * 3 + j], sibling)
                cp.start()
                passed.append(cp)
        for t in range(n):
            for j, (px, py) in enumerate(chips):
                slot = outs[t].at[1 - c, _shard_index(px, py)]
                _remote(slot, slot, send2.at[t * 3 + j], recv2.at[t * 3 + j], sibling).wait_recv()
        for cp in first + passed:
            cp.wait_send()

    return pl.pallas_call(
        body, name="gather_weights", in_specs=[HBM_SPEC] * n, out_specs=[HBM_SPEC] * n,
        out_shape=[jax.ShapeDtypeStruct(b.shape, b.dtype) for b in bufs],
        input_output_aliases={t: t for t in range(n)},
        scratch_shapes=[pltpu.SemaphoreType.DMA((n * 3,))] * 4,
    )(*bufs)


def _pair_exchange(bufs):
    n = len(bufs)

    def body(*refs):
        ins, outs = refs[:n], refs[n:2 * n]
        send, recv = refs[2 * n:]
        x, y, c, _ = _place()
        sibling = (x, y, 1 - c)
        cps = [_remote(ins[t].at[1 - c], outs[t], send.at[t], recv.at[t], sibling) for t in range(n)]
        for cp in cps:
            cp.start()
        for cp in cps:
            cp.wait()

    return pl.pallas_call(
        body, name="pair_exchange", in_specs=[HBM_SPEC] * n, out_specs=[HBM_SPEC] * n,
        out_shape=[jax.ShapeDtypeStruct(b.shape[1:], b.dtype) for b in bufs],
        scratch_shapes=[pltpu.SemaphoreType.DMA((n,))] * 2,
    )(*bufs)


def _pair_sum(own, other, name):
    _, s, r, c_dim = own.shape
    rows = min(r, LANES)
    per = r // rows
    layer = lax.axis_index("c").astype(jnp.int32).reshape(1)

    def body(layer_ref, a_ref, b_ref, o_ref):
        o_ref[...] = (a_ref[...] + b_ref[...]).astype(BF16)

    return pl.pallas_call(
        body, name=name,
        grid_spec=pltpu.PrefetchScalarGridSpec(
            num_scalar_prefetch=1, grid=(s * per,),
            in_specs=[pl.BlockSpec((None, None, rows, c_dim), lambda i, lay: (lay[0], i // per, i % per, 0)),
                      pl.BlockSpec((None, rows, c_dim), lambda i, lay: (i // per, i % per, 0))],
            out_specs=pl.BlockSpec((None, rows, c_dim), lambda i, lay: (i // per, i % per, 0))),
        out_shape=jax.ShapeDtypeStruct((s, r, c_dim), BF16), compiler_params=_params(("arbitrary",), 40),
    )(layer, own, other)


def _chip_exchange(bufs):
    n = len(bufs)

    def body(*refs):
        ins, outs = refs[:n], refs[n:2 * n]
        send, recv = refs[2 * n:]
        x, y, c, chips = _place()
        me = _shard_index(x, y)
        cps = []
        for t in range(n):
            for j, (px, py) in enumerate(chips):
                cp = _remote(ins[t].at[_shard_index(px, py)], outs[t].at[me], send.at[t * 3 + j], recv.at[t * 3 + j],
                             (px, py, c))
                cp.start()
                cps.append(cp)
        for t in range(n):
            for j, (px, py) in enumerate(chips):
                slot = outs[t].at[_shard_index(px, py)]
                _remote(slot, slot, send.at[t * 3 + j], recv.at[t * 3 + j], (px, py, c)).wait_recv()
        for cp in cps:
            cp.wait_send()

    return pl.pallas_call(
        body, name="chip_exchange", in_specs=[HBM_SPEC] * n, out_specs=[HBM_SPEC] * n,
        out_shape=[jax.ShapeDtypeStruct(b.shape, b.dtype) for b in bufs],
        scratch_shapes=[pltpu.SemaphoreType.DMA((n * 3,))] * 2,
    )(*bufs)


HBM_ONLY = pl.BlockSpec(memory_space=pltpu.HBM)
SEM_SPEC = pl.BlockSpec(memory_space=pltpu.SEMAPHORE)
DATAFLOW = pltpu.SideEffectType.DATAFLOW_SIDE_EFFECTING


def _in_hbm(a):
    return pltpu.with_memory_space_constraint(a, pltpu.HBM)


def _chip_exchange_start(bufs, name):
    n = len(bufs)

    def body(*refs):
        ins, lands = refs[:n], refs[n:2 * n]
        send, recv = refs[2 * n], refs[2 * n + 1]
        token = refs[-1]
        x, y, c, chips = _place()
        me = _shard_index(x, y)
        for t in range(n):
            for j, (px, py) in enumerate(chips):
                _remote(ins[t].at[_shard_index(px, py)], lands[t].at[me], send.at[t * 3 + j], recv.at[t * 3 + j],
                        (px, py, c)).start()
        token[...] = jnp.zeros_like(token)

    thru = [pltpu.HBM(b.shape, b.dtype) for b in bufs]
    res = pl.pallas_call(
        body, name=name,
        out_shape=(pltpu.SemaphoreType.DMA((n * 3,)), pltpu.SemaphoreType.DMA((n * 3,)), *thru, *thru,
                   jax.ShapeDtypeStruct((8, LANES), F32)),
        in_specs=[HBM_ONLY] * (2 * n),
        out_specs=(SEM_SPEC, SEM_SPEC, *([HBM_ONLY] * (2 * n)), pl.BlockSpec(memory_space=pltpu.VMEM)),
        input_output_aliases={i: 2 + i for i in range(2 * n)},
        compiler_params=pltpu.CompilerParams(has_side_effects=DATAFLOW),
    )(*[_in_hbm(b) for b in bufs], *[_in_hbm(lax.empty(b.shape, b.dtype)) for b in bufs])
    return res[0], res[1], list(res[2:2 + n]), list(res[2 + n:2 + 2 * n]), res[-1]


def _chip_exchange_wait(send, recv, bufs, lands, after, name):
    n = len(bufs)

    def body(*refs):
        ins, outs = refs[:n], refs[n:2 * n]
        send_ref, recv_ref = refs[2 * n], refs[2 * n + 1]
        x, y, c, chips = _place()
        for t in range(n):
            for j, (px, py) in enumerate(chips):
                sent = ins[t].at[_shard_index(px, py)]
                slot = outs[t].at[_shard_index(px, py)]
                cp = _remote(sent, slot, send_ref.at[t * 3 + j], recv_ref.at[t * 3 + j], (px, py, c))
                cp.wait_send()
                cp.wait_recv()

    thru = [pltpu.HBM(b.shape, b.dtype) for b in bufs]
    res = pl.pallas_call(
        body, name=name, out_shape=(*thru, *thru),
        in_specs=[HBM_ONLY] * (2 * n) + [SEM_SPEC, SEM_SPEC, pl.BlockSpec(memory_space=pl.ANY)],
        out_specs=[HBM_ONLY] * (2 * n),
        input_output_aliases={i: i for i in range(2 * n)},
        compiler_params=pltpu.CompilerParams(has_side_effects=DATAFLOW),
    )(*bufs, *lands, send, recv, after)
    return list(res[:n]), list(res[n:])


def _chip_sum(pair, landed, name):
    s, r, c_dim = pair.shape
    rows = min(r, LANES)
    shard = _shard_index(lax.axis_index("x"), lax.axis_index("y"))
    where = jnp.stack([shard, lax.axis_index("c")]).astype(jnp.int32)

    def landed_spec(k):
        return pl.BlockSpec((None, rows, c_dim), lambda i, w: (jnp.where(w[0] == k, (k + 1) % s, k), i, 0))

    def body(w_ref, own_ref, *rest):
        o_ref = rest[s]
        acc = None
        for k in range(s):
            term = jnp.where(w_ref[0] == k, own_ref[...], rest[k][...]).astype(F32)
            acc = term if acc is None else acc + term
        o_ref[...] = acc

    return pl.pallas_call(
        body, name=name,
        grid_spec=pltpu.PrefetchScalarGridSpec(
            num_scalar_prefetch=1, grid=(r // rows,),
            in_specs=[pl.BlockSpec((None, rows, c_dim), lambda i, w: (w[0], i, 0))] + [landed_spec(k) for k in range(s)],
            out_specs=pl.BlockSpec((None, rows, c_dim), lambda i, w: (w[1], i, 0))),
        out_shape=jax.ShapeDtypeStruct((DEPTH, r, c_dim), F32), compiler_params=_params(("arbitrary",), 40),
    )(where, pair, *([landed] * s))


def _sum_slots(buf, name):
    s, r, c_dim = buf.shape
    rows = min(r, LANES)

    def body(i_ref, o_ref):
        acc = i_ref[0].astype(F32)
        for k in range(1, s):
            acc = acc + i_ref[k].astype(F32)
        o_ref[...] = acc

    return pl.pallas_call(
        body, name=name, grid=(r // rows,),
        in_specs=[pl.BlockSpec((s, rows, c_dim), lambda i: (0, i, 0))],
        out_specs=pl.BlockSpec((rows, c_dim), lambda i: (i, 0)),
        out_shape=jax.ShapeDtypeStruct((r, c_dim), F32), compiler_params=_params(("arbitrary",), 40),
    )(buf)


def _pair_gather(bufs):
    n = len(bufs)

    def body(*refs):
        outs = refs[n:2 * n]
        send, recv = refs[2 * n:]
        x, y, c, _ = _place()
        sibling = (x, y, 1 - c)
        cps = [_remote(outs[t].at[c], outs[t].at[c], send.at[t], recv.at[t], sibling) for t in range(n)]
        for cp in cps:
            cp.start()
        for t in range(n):
            slot = outs[t].at[1 - c]
            _remote(slot, slot, send.at[t], recv.at[t], sibling).wait_recv()
        for cp in cps:
            cp.wait_send()

    return pl.pallas_call(
        body, name="pair_gather", in_specs=[HBM_SPEC] * n, out_specs=[HBM_SPEC] * n,
        out_shape=[jax.ShapeDtypeStruct(b.shape, b.dtype) for b in bufs],
        input_output_aliases={t: t for t in range(n)},
        scratch_shapes=[pltpu.SemaphoreType.DMA((n,))] * 2,
    )(*bufs)


N_DEV = 8


def _all_gather_small(vec, name, after=()):
    n_after = len(after)

    def body(v_ref, *rest):
        o_ref, send, recv, local_sem = rest[n_after:]
        x, y, c, _ = _place()
        me = 4 * x + 2 * y + c
        local = pltpu.make_async_copy(v_ref, o_ref.at[me], local_sem)
        local.start()
        flips = [(fx, fy, fc) for fx in (0, 1) for fy in (0, 1) for fc in (0, 1)][1:]
        peers = [((1 - x) if fx else x, (1 - y) if fy else y, (1 - c) if fc else c) for fx, fy, fc in flips]
        cps = [_remote(v_ref, o_ref.at[me], send.at[k], recv.at[k], peer) for k, peer in enumerate(peers)]
        for cp in cps:
            cp.start()
        for k, (px, py, pc) in enumerate(peers):
            slot = o_ref.at[4 * px + 2 * py + pc]
            _remote(slot, slot, send.at[k], recv.at[k], (px, py, pc)).wait_recv()
        for cp in cps:
            cp.wait_send()
        local.wait()

    return pl.pallas_call(
        body, name=name, in_specs=[HBM_SPEC] * (1 + n_after), out_specs=HBM_SPEC,
        out_shape=jax.ShapeDtypeStruct((N_DEV,) + vec.shape, vec.dtype),
        scratch_shapes=[pltpu.SemaphoreType.DMA((N_DEV - 1,))] * 2 + [pltpu.SemaphoreType.DMA(())],
    )(vec, *after)


def _half(ref, slot, c):
    half = ref.shape[1] // 2
    return ref.at[slot, pl.ds(pl.multiple_of(c * half, 8), half)]


def _gather_start(bufs, after, name):
    n = len(bufs)
    n_after = len(after)

    def body(*refs):
        ins = refs[:n]
        send, recv = refs[n + n_after], refs[n + n_after + 1]
        token = refs[-1]
        x, y, c, chips = _place()
        me = _shard_index(x, y)
        for t in range(n):
            for j, (px, py) in enumerate(chips):
                mine = _half(ins[t], me, c)
                _remote(mine, mine, send.at[t * 3 + j], recv.at[t * 3 + j], (px, py, c)).start()
        token[...] = jnp.zeros_like(token)

    thru = [pltpu.HBM(b.shape, b.dtype) for b in bufs]
    res = pl.pallas_call(
        body, name=name,
        out_shape=(pltpu.SemaphoreType.DMA((n * 3,)), pltpu.SemaphoreType.DMA((n * 3,)), *thru,
                   jax.ShapeDtypeStruct((8, LANES), F32)),
        in_specs=[HBM_ONLY] * n + [ANY_SPEC] * n_after,
        out_specs=(SEM_SPEC, SEM_SPEC, *([HBM_ONLY] * n), pl.BlockSpec(memory_space=pltpu.VMEM)),
        input_output_aliases={i: 2 + i for i in range(n)},
        compiler_params=pltpu.CompilerParams(has_side_effects=DATAFLOW),
    )(*[_in_hbm(b) for b in bufs], *after)
    return res[0], res[1], list(res[2:2 + n]), res[-1]


def _gather_wait(send, recv, bufs, after, name):
    n = len(bufs)

    def body(*refs):
        ins = refs[:n]
        send_ref, recv_ref = refs[n], refs[n + 1]
        x, y, c, chips = _place()
        me = _shard_index(x, y)
        for t in range(n):
            for j, (px, py) in enumerate(chips):
                cp = _remote(_half(ins[t], me, c), _half(ins[t], _shard_index(px, py), c), send_ref.at[t * 3 + j],
                             recv_ref.at[t * 3 + j], (px, py, c))
                cp.wait_send()
                cp.wait_recv()

    res = pl.pallas_call(
        body, name=name, out_shape=tuple(pltpu.HBM(b.shape, b.dtype) for b in bufs),
        in_specs=[HBM_ONLY] * n + [SEM_SPEC, SEM_SPEC] + [ANY_SPEC] * len(after), out_specs=[HBM_ONLY] * n,
        input_output_aliases={i: i for i in range(n)},
        compiler_params=pltpu.CompilerParams(has_side_effects=DATAFLOW),
    )(*bufs, send, recv, *after)
    return list(res)


def _gather_forward(bufs, name):
    n = len(bufs)

    def body(*refs):
        outs = refs[n:2 * n]
        send, recv = refs[2 * n:]
        x, y, c, chips = _place()
        sibling = (x, y, 1 - c)
        cps = []
        for t in range(n):
            for j, (px, py) in enumerate(chips):
                got = _half(outs[t], _shard_index(px, py), c)
                cp = _remote(got, got, send.at[t * 3 + j], recv.at[t * 3 + j], sibling)
                cp.start()
                cps.append(cp)
        for t in range(n):
            for j, (px, py) in enumerate(chips):
                theirs = _half(outs[t], _shard_index(px, py), 1 - c)
                _remote(theirs, theirs, send.at[t * 3 + j], recv.at[t * 3 + j], sibling).wait_recv()
        for cp in cps:
            cp.wait_send()

    return pl.pallas_call(
        body, name=name, in_specs=[HBM_SPEC] * n, out_specs=[HBM_SPEC] * n,
        out_shape=[jax.ShapeDtypeStruct(b.shape, b.dtype) for b in bufs],
        input_output_aliases={t: t for t in range(n)},
        scratch_shapes=[pltpu.SemaphoreType.DMA((n * 3,))] * 2,
    )(*bufs)


def _sibling_rows(ref, c):
    half = ref.shape[1] // 2
    return ref.at[:, pl.ds(pl.multiple_of((1 - c) * half, 8), half)]


def _half_exchange_start(grads, name):
    n = len(grads)

    def body(*refs):
        ins, lands = refs[:n], refs[n:2 * n]
        send, recv = refs[2 * n], refs[2 * n + 1]
        token = refs[-1]
        x, y, c, _ = _place()
        for t in range(n):
            _remote(_sibling_rows(ins[t], c), lands[t], send.at[t], recv.at[t], (x, y, 1 - c)).start()
        token[...] = jnp.zeros_like(token)

    halves = [jax.ShapeDtypeStruct((g.shape[0], g.shape[1] // 2, g.shape[2]), g.dtype) for g in grads]
    res = pl.pallas_call(
        body, name=name,
        out_shape=(pltpu.SemaphoreType.DMA((n,)), pltpu.SemaphoreType.DMA((n,)),
                   *[pltpu.HBM(g.shape, g.dtype) for g in grads], *[pltpu.HBM(h.shape, h.dtype) for h in halves],
                   jax.ShapeDtypeStruct((8, LANES), F32)),
        in_specs=[HBM_ONLY] * (2 * n),
        out_specs=(SEM_SPEC, SEM_SPEC, *([HBM_ONLY] * (2 * n)), pl.BlockSpec(memory_space=pltpu.VMEM)),
        input_output_aliases={i: 2 + i for i in range(2 * n)},
        compiler_params=pltpu.CompilerParams(has_side_effects=DATAFLOW),
    )(*[_in_hbm(g) for g in grads], *[_in_hbm(lax.empty(h.shape, h.dtype)) for h in halves])
    return res[0], res[1], list(res[2:2 + n]), list(res[2 + n:2 + 2 * n]), res[-1]


def _half_exchange_wait(send, recv, grads, lands, after, name):
    n = len(grads)

    def body(*refs):
        ins, got = refs[:n], refs[n:2 * n]
        send_ref, recv_ref = refs[2 * n], refs[2 * n + 1]
        x, y, c, _ = _place()
        for t in range(n):
            cp = _remote(_sibling_rows(ins[t], c), got[t], send_ref.at[t], recv_ref.at[t], (x, y, 1 - c))
            cp.wait_send()
            cp.wait_recv()

    res = pl.pallas_call(
        body, name=name,
        out_shape=(*[pltpu.HBM(g.shape, g.dtype) for g in grads], *[pltpu.HBM(h.shape, h.dtype) for h in lands]),
        in_specs=[HBM_ONLY] * (2 * n) + [SEM_SPEC, SEM_SPEC] + [ANY_SPEC] * len(after),
        out_specs=[HBM_ONLY] * (2 * n),
        input_output_aliases={i: i for i in range(2 * n)},
        compiler_params=pltpu.CompilerParams(has_side_effects=DATAFLOW),
    )(*grads, *lands, send, recv, *after)
    return list(res[:n]), list(res[n:])


def _half_rows(half, c_dim):
    for cand in (512, 256, 128, 64):
        if half % cand == 0 and cand * c_dim * 2 <= MIB:
            return cand
    raise ValueError((half, c_dim))


def _core_index():
    return lax.axis_index("c").astype(jnp.int32).reshape(1)


def _half_sum(own, other, name):
    s, r, c_dim = own.shape
    rows = _half_rows(r // 2, c_dim)
    per = r // 2 // rows

    def body(c_ref, a_ref, b_ref, o_ref):
        o_ref[...] = (a_ref[...].astype(F32) + b_ref[...].astype(F32)).astype(BF16)

    return pl.pallas_call(
        body, name=name,
        grid_spec=pltpu.PrefetchScalarGridSpec(
            num_scalar_prefetch=1, grid=(s, per),
            in_specs=[pl.BlockSpec((None, rows, c_dim), lambda k, i, c: (k, c[0] * per + i, 0)),
                      pl.BlockSpec((None, rows, c_dim), lambda k, i, c: (k, i, 0))],
            out_specs=pl.BlockSpec((None, rows, c_dim), lambda k, i, c: (k, i, 0))),
        out_shape=jax.ShapeDtypeStruct((s, r // 2, c_dim), BF16), compiler_params=_params(("arbitrary", "arbitrary"), 32),
    )(_core_index(), own, other)


def _reduce_start(pairs, name):
    n = len(pairs)

    def body(*refs):
        ins, lands = refs[:n], refs[n:2 * n]
        send, recv = refs[2 * n], refs[2 * n + 1]
        token = refs[-1]
        x, y, c, chips = _place()
        me = _shard_index(x, y)
        for t in range(n):
            for j, (px, py) in enumerate(chips):
                _remote(ins[t].at[_shard_index(px, py)], lands[t].at[me], send.at[t * 3 + j], recv.at[t * 3 + j],
                        (px, py, c)).start()
        token[...] = jnp.zeros_like(token)

    thru = [pltpu.HBM(b.shape, b.dtype) for b in pairs]
    res = pl.pallas_call(
        body, name=name,
        out_shape=(pltpu.SemaphoreType.DMA((n * 3,)), pltpu.SemaphoreType.DMA((n * 3,)), *thru, *thru,
                   jax.ShapeDtypeStruct((8, LANES), F32)),
        in_specs=[HBM_ONLY] * (2 * n),
        out_specs=(SEM_SPEC, SEM_SPEC, *([HBM_ONLY] * (2 * n)), pl.BlockSpec(memory_space=pltpu.VMEM)),
        input_output_aliases={i: 2 + i for i in range(2 * n)},
        compiler_params=pltpu.CompilerParams(has_side_effects=DATAFLOW),
    )(*[_in_hbm(b) for b in pairs], *[_in_hbm(lax.empty(b.shape, b.dtype)) for b in pairs])
    return res[0], res[1], list(res[2:2 + n]), list(res[2 + n:2 + 2 * n]), res[-1]


def _reduce_wait(send, recv, pairs, lands, after, name):
    n = len(pairs)

    def body(*refs):
        ins, got = refs[:n], refs[n:2 * n]
        send_ref, recv_ref = refs[2 * n], refs[2 * n + 1]
        x, y, c, chips = _place()
        for t in range(n):
            for j, (px, py) in enumerate(chips):
                s = _shard_index(px, py)
                cp = _remote(ins[t].at[s], got[t].at[s], send_ref.at[t * 3 + j], recv_ref.at[t * 3 + j], (px, py, c))
                cp.wait_send()
                cp.wait_recv()

    thru = [pltpu.HBM(b.shape, b.dtype) for b in pairs]
    res = pl.pallas_call(
        body, name=name, out_shape=(*thru, *thru),
        in_specs=[HBM_ONLY] * (2 * n) + [SEM_SPEC, SEM_SPEC] + [ANY_SPEC] * len(after),
        out_specs=[HBM_ONLY] * (2 * n),
        input_output_aliases={i: i for i in range(2 * n)},
        compiler_params=pltpu.CompilerParams(has_side_effects=DATAFLOW),
    )(*pairs, *lands, send, recv, *after)
    return list(res[:n]), list(res[n:])


def _reduce_sum(pair, landed, layer, prev, name):
    s, half, c_dim = pair.shape
    rows = _half_rows(half, c_dim)
    per = half // rows
    shard = _shard_index(lax.axis_index("x"), lax.axis_index("y"))
    where = jnp.stack([shard, lax.axis_index("c")]).astype(jnp.int32)

    def landed_spec(k):
        return pl.BlockSpec((None, rows, c_dim), lambda i, w: (jnp.where(w[0] == k, (k + 1) % s, k), i, 0))

    def body(w_ref, own_ref, *rest):
        o_ref = rest[-1]
        acc = None
        for k in range(s):
            term = jnp.where(w_ref[0] == k, own_ref[...], rest[k][...]).astype(F32)
            acc = term if acc is None else acc + term
        o_ref[...] = acc

    args = [where, pair] + [landed] * s
    in_specs = [pl.BlockSpec((None, rows, c_dim), lambda i, w: (w[0], i, 0))] + [landed_spec(k) for k in range(s)]
    aliases = {}
    if prev is not None:
        args.append(prev)
        in_specs.append(ANY_SPEC)
        aliases = {len(args) - 1: 0}
    return pl.pallas_call(
        body, name=name,
        grid_spec=pltpu.PrefetchScalarGridSpec(
            num_scalar_prefetch=1, grid=(per,), in_specs=in_specs,
            out_specs=pl.BlockSpec((None, rows, c_dim), lambda i, w: (layer, w[1] * per + i, 0))),
        out_shape=jax.ShapeDtypeStruct((DEPTH, 2 * half, c_dim), F32), input_output_aliases=aliases,
        compiler_params=_params(("arbitrary",), 40),
    )(*args)


def _half_gather(bufs, layer, name):
    n = len(bufs)

    def body(*refs):
        outs = refs[n:2 * n]
        send, recv = refs[2 * n:]
        x, y, c, _ = _place()
        sibling = (x, y, 1 - c)

        def rows(t, which):
            half = outs[t].shape[1] // 2
            return outs[t].at[layer, pl.ds(pl.multiple_of(which * half, 8), half)]

        cps = [_remote(rows(t, c), rows(t, c), send.at[t], recv.at[t], sibling) for t in range(n)]
        for cp in cps:
            cp.start()
        for t in range(n):
            _remote(rows(t, 1 - c), rows(t, 1 - c), send.at[t], recv.at[t], sibling).wait_recv()
        for cp in cps:
            cp.wait_send()

    return pl.pallas_call(
        body, name=name, in_specs=[HBM_SPEC] * n, out_specs=[HBM_SPEC] * n,
        out_shape=[jax.ShapeDtypeStruct(b.shape, b.dtype) for b in bufs],
        input_output_aliases={t: t for t in range(n)},
        scratch_shapes=[pltpu.SemaphoreType.DMA((n,))] * 2,
    )(*bufs)


WEIGHT_NAMES = ("ln_attn", "w_in", "sink_b", "rpb_c", "mix_gain", "w_out", "ln_ffn", "w_up", "conv_w", "conv_b",
                "w_down", "ln_final")
BIG_NAMES = ("w_in", "w_out", "w_up", "w_down")
REPLICATED_NAMES = ("ln_attn", "sink_b", "rpb_c", "mix_gain", "ln_ffn", "conv_b", "ln_final")
PACK_TILE = 8 * LANES


def _pack(arrays, row_multiple):
    pieces = []
    for a in arrays:
        flat = a.reshape(-1)
        pieces.append(jnp.pad(flat, (0, (-flat.shape[0]) % PACK_TILE)))
    flat = jnp.concatenate(pieces)
    flat = jnp.pad(flat, (0, (-flat.shape[0]) % (row_multiple * LANES)))
    return flat.reshape(-1, LANES)


def _unpack(packed, shapes):
    flat = packed.reshape(-1)
    out, off = [], 0
    for shape in shapes:
        size = math.prod(shape)
        out.append(flat[off:off + size].reshape(shape))
        off += size + (-size) % PACK_TILE
    return out


def kernel(x, ln_attn, w_in, sink_b, rpb_c, mix_gain, w_out, ln_ffn, w_up, conv_w, conv_b, w_down, ln_final, loss_target, m_ln_attn, m_w_in, m_sink_b, m_rpb_c, m_mix_gain, m_w_out, m_ln_ffn, m_w_up, m_conv_w, m_conv_b, m_w_down, m_ln_final, v_ln_attn, v_w_in, v_sink_b, v_rpb_c, v_mix_gain, v_w_out, v_ln_ffn, v_w_up, v_conv_w, v_conv_b, v_w_down, v_ln_final):
    w = dict(ln_attn=ln_attn, w_in=w_in, sink_b=sink_b, rpb_c=rpb_c, mix_gain=mix_gain, w_out=w_out, ln_ffn=ln_ffn,
             w_up=w_up, conv_w=conv_w, conv_b=conv_b, w_down=w_down, ln_final=ln_final)
    m = dict(ln_attn=m_ln_attn, w_in=m_w_in, sink_b=m_sink_b, rpb_c=m_rpb_c, mix_gain=m_mix_gain, w_out=m_w_out,
             ln_ffn=m_ln_ffn, w_up=m_w_up, conv_w=m_conv_w, conv_b=m_conv_b, w_down=m_w_down, ln_final=m_ln_final)
    v = dict(ln_attn=v_ln_attn, w_in=v_w_in, sink_b=v_sink_b, rpb_c=v_rpb_c, mix_gain=v_mix_gain, w_out=v_w_out,
             ln_ffn=v_ln_ffn, w_up=v_w_up, conv_w=v_conv_w, conv_b=v_conv_b, w_down=v_w_down, ln_final=v_ln_final)
    shard = _shard_index(lax.axis_index("x"), lax.axis_index("y"))
    up_cols = w_up.shape[2]

    conv_slots = _all_gather_small(_pack([conv_w], 8), "gather_conv_w")
    conv_all = conv_slots[0::2].reshape(N_SHARDS, -1)[:, :conv_w.size].reshape((N_SHARDS,) + conv_w.shape)

    arrivals = []
    group_of = {}
    tokens = []
    rest = ("w_out", "w_up", "w_down")
    for l, names in ((0, ("w_in",)), (0, rest), (1, ("w_in",)), (1, rest)):
        bufs = [_own_slot(w[k], l, shard, "own_" + k) for k in names]
        send, recv, bufs, token = _gather_start(bufs, tokens[-1:] or [conv_slots], "gather_start_%d" % len(arrivals))
        tokens.append(token)
        for k in names:
            group_of[l, k] = len(arrivals)
        arrivals.append({"names": names, "send": send, "recv": recv, "bufs": bufs, "done": None})

    def gathered(l, name, after):
        idx = group_of[l, name]
        group = arrivals[idx]
        if group["done"] is None:
            bufs = _gather_wait(group["send"], group["recv"], group["bufs"], list(after) + tokens[-1:],
                                "gather_wait_%d" % idx)
            group["done"] = dict(zip(group["names"], _gather_forward(bufs, "gather_forward_%d" % idx)))
        buf = group["done"][name]
        return buf.reshape(1, -1, buf.shape[2]) if name in ("w_out", "w_down") else buf

    cos, sin = _rope_tables(SEQ)
    tabs = {"cos": cos, "sin": sin, "bias_a": _bias_a(), "bias_b": _bias_b()}
    layers = []
    for l in range(DEPTH):
        conv_w_l = conv_all[:, l].reshape(2, N_SHARDS // 2, 3, up_cols).transpose(0, 2, 1, 3).reshape(2, 3, D_FF)
        layers.append({"ln_attn": ln_attn[l][None], "sink_b": sink_b[l], "bias_c": _bias_c(rpb_c[l]),
                       "mix_gain": mix_gain[l][None], "ln_ffn": ln_ffn[l][None], "conv_w": conv_w_l,
                       "conv_b": conv_b[l].reshape(2, 1, D_FF)})

    act = x[0]
    saved = []
    for l in range(DEPTH):
        act, keep = _layer_fwd(act, layers[l], lambda name, after, l=l: gathered(l, name, [after]), tabs)
        saved.append(keep)
    loss_part, dx, dx_b, d_ln_final = _loss_head(act, ln_final[None], loss_target[0], "loss_head")
    loss = lax.psum(loss_part[0, 0], ("x", "y", "c"))

    reductions = []

    opened = [0]

    def begin(l, partial):
        idx = opened[0]
        opened[0] += 1
        names = tuple(partial)
        send_sem, recv_sem, mine, theirs, token = _half_exchange_start([partial[k] for k in names],
                                                                       "half_exchange_start_%d" % idx)
        return {"idx": idx, "layer": l, "names": names, "send": send_sem, "recv": recv_sem, "mine": mine,
                "theirs": theirs}, token

    def finish(handle, after):
        idx, names = handle["idx"], handle["names"]
        mine, theirs = _half_exchange_wait(handle["send"], handle["recv"], handle["mine"], handle["theirs"], after,
                                           "half_exchange_wait_%d" % idx)
        pairs = [_half_sum(a, b, "half_sum_" + k) for k, a, b in zip(names, mine, theirs)]
        send_sem, recv_sem, pairs, lands, token = _reduce_start(pairs, "reduce_start_%d" % idx)
        reductions.append({"layer": handle["layer"], "names": names, "send": send_sem, "recv": recv_sem,
                           "pairs": pairs, "lands": lands})
        return token

    small = [None] * DEPTH
    pending = None
    for l in reversed(range(DEPTH)):
        big = {k: gathered(l, k, []) for k in BIG_NAMES}
        dx, dx_b, small[l], pending = _layer_bwd(dx, dx_b, saved[l], layers[l], big, tabs,
                                                 functools.partial(begin, l), finish, pending)
    after = [finish(pending[0], [pending[1]])]

    grads, delta, new_m, new_v = {}, {}, {}, {}
    reduced = {}
    updated = dict.fromkeys(BIG_NAMES)
    for l in reversed(range(DEPTH)):
        for idx, group in enumerate(reductions):
            if group["layer"] != l:
                continue
            pairs, lands = _reduce_wait(group["send"], group["recv"], group["pairs"], group["lands"], after,
                                        "reduce_wait_%d" % idx)
            for k, pair, landed in zip(group["names"], pairs, lands):
                reduced[k] = _reduce_sum(pair, landed, l, reduced.get(k), "reduce_sum_" + k)
            after = [reduced[group["names"][-1]]]
        reduced = dict(zip(BIG_NAMES, _half_gather([reduced[k] for k in BIG_NAMES], l, "half_gather_%d" % l)))
        for k in BIG_NAMES:
            updated[k] = _adamw_layer(w[k], reduced[k], m[k], v[k], l, updated[k], "adamw_" + k)
        after = [updated[k][0] for k in BIG_NAMES]
    for k in BIG_NAMES:
        grads[k], delta[k], new_m[k], new_v[k] = updated[k]

    stacked = {k: jnp.stack([small[l][k] for l in range(DEPTH)]) for k in small[0]}
    part = {"ln_attn": stacked["ln_attn"][:, 0], "sink_b": stacked["sink_b"], "rpb_c": stacked["rpb_c"],
            "mix_gain": stacked["mix_gain"][:, 0], "ln_ffn": stacked["ln_ffn"][:, 0],
            "conv_b": stacked["conv_b"].reshape(DEPTH, 2 * D_FF), "ln_final": d_ln_final[0],
            "conv_w": stacked["conv_w"].transpose(0, 2, 1, 3).reshape(DEPTH, 3, 2 * D_FF)}
    names = REPLICATED_NAMES + ("conv_w",)
    total = _sum_slots(_all_gather_small(_pack([part[k] for k in names], 256), "gather_small_grads", after),
                       "sum_small")
    for k, g in zip(names, _unpack(total, [part[k].shape for k in names])):
        grads[k] = g
    grads["conv_w"] = lax.dynamic_slice_in_dim(grads["conv_w"], shard * up_cols, up_cols, axis=2)

    flat = (DEPTH * 3, up_cols)
    res = _adamw(conv_w.reshape(flat), grads["conv_w"].reshape(flat), m["conv_w"].reshape(flat),
                 v["conv_w"].reshape(flat), "adamw_conv_w")
    delta["conv_w"], new_m["conv_w"], new_v["conv_w"] = (r.reshape(conv_w.shape) for r in res)
    shapes = [w[k].shape for k in REPLICATED_NAMES]
    packed = [_pack([d[k] for k in REPLICATED_NAMES], 128) for d in (w, grads, m, v)]
    for d, res in zip((delta, new_m, new_v), _adamw(*packed, "adamw_small")):
        for k, r in zip(REPLICATED_NAMES, _unpack(res, shapes)):
            d[k] = r

    return (loss, dx[None], *[grads[k] for k in WEIGHT_NAMES], *[delta[k] for k in WEIGHT_NAMES],
            *[new_m[k] for k in WEIGHT_NAMES], *[new_v[k] for k in WEIGHT_NAMES])
```

```python
import functools
import math

import jax
import jax.numpy as jnp
from jax import lax
from jax.experimental import pallas as pl
from jax.experimental.pallas import tpu as pltpu

F32 = jnp.float32
BF16 = jnp.bfloat16
MESH = pl.DeviceIdType.MESH

D_MODEL = 2048
SEQ = 2048
DEPTH = 2
HEAD_DIM = 64
N_HEADS_A = 12
N_HEADS_B = 10
N_KV_B = 2
N_HEADS_C = 10
WINDOW_B = 128
GRID_W = 64
NA_ROWS = 8
NA_COLS = 16
WIDTH_A = N_HEADS_A * HEAD_DIM
WIDTH_B = N_HEADS_B * HEAD_DIM
WIDTH_C = N_HEADS_C * HEAD_DIM
IN_COLS = 5120
D_FF = 5632
ROPE_THETA = 10000.0
EPS = 1e-6
NEG_INF = -1e30
N_SHARDS = 4

ADAM_LR = 0.001
ADAM_B1 = 0.9
ADAM_B2 = 0.999
ADAM_EPS = 1e-08
ADAM_WD = 0.01
ADAM_STEP = 10

LANES = 128
QB = 256
NQB = SEQ // QB
ROWS = 256
MIB = 2 ** 20

A_BLK = (0, 6, 12)
B_BLK = (18, 23, 24)
C_BLK = (25, 30, 35)
ROPE_BLKS = tuple(range(0, 12)) + tuple(range(18, 24))
QSCALE_BLKS = tuple(range(0, 6)) + tuple(range(18, 23)) + tuple(range(25, 30))
N_PBLK = IN_COLS // LANES


def _params(sem, vmem_mib):
    return pltpu.CompilerParams(dimension_semantics=sem, vmem_limit_bytes=vmem_mib * MIB)


def _weight_spec(w, cols, t_in, t_out, transposed):
    s, r, c = w.shape
    if cols:
        per = c // t_out
        k_dim, n = r, s * c
        if transposed:
            index = lambda j, rr: (rr // per, j, rr % per)
        else:
            index = lambda j, kk: (j // per, kk, j % per)
    else:
        per = r // t_in
        k_dim, n = s * r, c
        if transposed:
            index = lambda j, rr: (j // per, j % per, rr)
        else:
            index = lambda j, kk: (kk // per, kk % per, j)
    return pl.BlockSpec((None, t_in, t_out), index), k_dim, n


def _mm_nn(a, w, *, cols, tn, tk, out_dtype, name, residual=None, out_split=1):
    m, k_dim = a.shape
    w_spec, k_w, n = _weight_spec(w, cols, tk, tn, False)
    assert k_w == k_dim
    nj, nk = n // tn, k_dim // tk
    in_specs = [pl.BlockSpec((m, tk), lambda j, k: (0, k)), w_spec]
    args = [a, w]
    if residual is not None:
        in_specs.append(pl.BlockSpec((m, tn), lambda j, k: (0, j)))
        args.append(residual)
    if out_split > 1:
        per_o = n // out_split // tn
        out_spec = pl.BlockSpec((None, m, tn), lambda j, k: (j // per_o, 0, j % per_o))
        out_shape = pltpu.HBM((out_split, m, n // out_split), out_dtype)
    else:
        out_spec = pl.BlockSpec((m, tn), lambda j, k: (0, j))
        out_shape = pltpu.HBM((m, n), out_dtype)

    def body(*refs):
        a_ref, w_ref = refs[0], refs[1]
        r_ref = refs[2] if residual is not None else None
        o_ref = refs[3] if residual is not None else refs[2]

        def finish(val):
            if r_ref is not None:
                val = r_ref[...] + val
            o_ref[...] = val.astype(o_ref.dtype)

        part = jnp.dot(a_ref[...], w_ref[...], preferred_element_type=F32)
        if nk == 1:
            finish(part)
        else:
            acc = refs[-1]
            kk = pl.program_id(1)

            @pl.when(kk == 0)
            def _():
                acc[...] = part

            @pl.when(kk > 0)
            def _():
                acc[...] += part

            @pl.when(kk == nk - 1)
            def _():
                finish(acc[...])

    return pl.pallas_call(
        body, name=name, grid=(nj, nk), in_specs=in_specs, out_specs=out_spec, out_shape=out_shape,
        scratch_shapes=[pltpu.VMEM((m, tn), F32)] if nk > 1 else [],
        compiler_params=_params(("arbitrary", "arbitrary"), 56),
    )(*[_in_hbm(a) for a in args])


ANY_SPEC = pl.BlockSpec(memory_space=pl.ANY)


def _mm_nt(dy, w, *, cols, to, tr, out_dtype, name, after=()):
    if dy.ndim == 3:
        m = dy.shape[1]
        n = dy.shape[0] * dy.shape[2]
        per_d = dy.shape[2] // tr
        dy_spec = pl.BlockSpec((None, m, tr), lambda j, r: (r // per_d, 0, r % per_d))
    else:
        m, n = dy.shape
        dy_spec = pl.BlockSpec((m, tr), lambda j, r: (0, r))
    w_spec, k_dim, n_w = _weight_spec(w, cols, to, tr, True)
    assert n_w == n
    nj, nr = k_dim // to, n // tr

    n_after = len(after)

    def body(dy_ref, w_ref, *rest):
        o_ref = rest[n_after]
        part = lax.dot_general(dy_ref[...], w_ref[...], (((1,), (1,)), ((), ())), preferred_element_type=F32)
        if nr == 1:
            o_ref[...] = part.astype(o_ref.dtype)
        else:
            acc = rest[n_after + 1]
            rr = pl.program_id(1)

            @pl.when(rr == 0)
            def _():
                acc[...] = part

            @pl.when(rr > 0)
            def _():
                acc[...] += part

            @pl.when(rr == nr - 1)
            def _():
                o_ref[...] = acc[...].astype(o_ref.dtype)

    return pl.pallas_call(
        body, name=name, grid=(nj, nr), in_specs=[dy_spec, w_spec] + [ANY_SPEC] * n_after,
        out_specs=pl.BlockSpec((m, to), lambda j, r: (0, j)),
        out_shape=pltpu.HBM((m, k_dim), out_dtype),
        scratch_shapes=[pltpu.VMEM((m, to), F32)] if nr > 1 else [],
        compiler_params=_params(("arbitrary", "arbitrary"), 56),
    )(_in_hbm(dy), _in_hbm(w), *after)


def _mm_tn(x, dy, *, tk, tn, shards, name):
    m, k_dim = x.shape
    if dy.ndim == 3:
        n = dy.shape[0] * dy.shape[2]
        per_d = dy.shape[2] // tn
        dy_spec = pl.BlockSpec((None, m, tn), lambda i, j: (j // per_d, 0, j % per_d))
    else:
        n = dy.shape[1]
        dy_spec = pl.BlockSpec((m, tn), lambda i, j: (0, j))
    if shards > 0:
        per = n // shards // tn
        out_shape = pltpu.HBM((shards, k_dim, n // shards), BF16)
        out_spec = pl.BlockSpec((None, tk, tn), lambda i, j: (j // per, i, j % per))
    else:
        s = -shards
        per = k_dim // s // tk
        out_shape = pltpu.HBM((s, k_dim // s, n), BF16)
        out_spec = pl.BlockSpec((None, tk, tn), lambda i, j: (i // per, i % per, j))

    def body(x_ref, dy_ref, o_ref):
        o_ref[...] = lax.dot_general(x_ref[...], dy_ref[...], (((0,), (0,)), ((), ())),
                                     preferred_element_type=F32).astype(BF16)

    return pl.pallas_call(
        body, name=name, grid=(k_dim // tk, n // tn),
        in_specs=[pl.BlockSpec((m, tk), lambda i, j: (0, i)), dy_spec], out_specs=out_spec, out_shape=out_shape,
        compiler_params=_params(("arbitrary", "arbitrary"), 56),
    )(_in_hbm(x), _in_hbm(dy))


def _row_spec(width, rows=ROWS):
    return pl.BlockSpec((rows, width), lambda i: (i, 0))


def _vec_spec(width):
    return pl.BlockSpec((1, width), lambda i: (0, 0))


def _rms_stats(x):
    r = lax.rsqrt(jnp.mean(x * x, axis=-1, keepdims=True) + EPS)
    return r, x * r


def _rmsnorm_fwd(x, gain, name):
    t, d = x.shape

    def body(x_ref, g_ref, o_ref):
        _, n = _rms_stats(x_ref[...])
        o_ref[...] = (n * g_ref[...]).astype(BF16)

    return pl.pallas_call(
        body, name=name, grid=(t // ROWS,), in_specs=[_row_spec(d), _vec_spec(d)], out_specs=_row_spec(d),
        out_shape=pltpu.HBM((t, d), BF16), compiler_params=_params(("arbitrary",), 32),
    )(_in_hbm(x), _in_hbm(gain))


def _rmsnorm_bwd(x, gain, dh, dres, name, after=()):
    t, d = x.shape
    n_after = len(after)

    def body(x_ref, g_ref, dh_ref, dres_ref, *rest):
        dx_ref, dxb_ref, dg_ref = rest[n_after:]
        r, n = _rms_stats(x_ref[...])
        dh_v = dh_ref[...]
        dn = dh_v * g_ref[...]
        dx = dres_ref[...] + r * (dn - n * jnp.mean(dn * n, axis=-1, keepdims=True))
        dx_ref[...] = dx
        dxb_ref[...] = dx.astype(BF16)
        part = jnp.sum(dh_v * n, axis=0, keepdims=True)

        @pl.when(pl.program_id(0) == 0)
        def _():
            dg_ref[...] = part

        @pl.when(pl.program_id(0) > 0)
        def _():
            dg_ref[...] += part

    return pl.pallas_call(
        body, name=name, grid=(t // ROWS,),
        in_specs=[_row_spec(d), _vec_spec(d), _row_spec(d), _row_spec(d)] + [ANY_SPEC] * n_after,
        out_specs=[_row_spec(d), _row_spec(d), _vec_spec(d)],
        out_shape=[pltpu.HBM((t, d), F32), pltpu.HBM((t, d), BF16), jax.ShapeDtypeStruct((1, d), F32)],
        compiler_params=_params(("arbitrary",), 40),
    )(_in_hbm(x), _in_hbm(gain), _in_hbm(dh), _in_hbm(dres), *after)


def _loss_head(x, gain, target, name):
    t, d = x.shape

    def body(x_ref, g_ref, t_ref, loss_ref, dx_ref, dxb_ref, dg_ref):
        r, n = _rms_stats(x_ref[...])
        g = g_ref[...]
        err = n * g - t_ref[...]
        dy = err * (1.0 / d)
        dn = dy * g
        dx = r * (dn - n * jnp.mean(dn * n, axis=-1, keepdims=True))
        dx_ref[...] = dx
        dxb_ref[...] = dx.astype(BF16)
        part = jnp.sum(dy * n, axis=0, keepdims=True)
        lpart = jnp.zeros((8, LANES), F32) + 0.5 * jnp.sum(jnp.mean(err * err, axis=-1, keepdims=True))

        @pl.when(pl.program_id(0) == 0)
        def _():
            dg_ref[...] = part
            loss_ref[...] = lpart

        @pl.when(pl.program_id(0) > 0)
        def _():
            dg_ref[...] += part
            loss_ref[...] += lpart

    return pl.pallas_call(
        body, name=name, grid=(t // ROWS,),
        in_specs=[_row_spec(d), _vec_spec(d), _row_spec(d)],
        out_specs=[pl.BlockSpec((8, LANES), lambda i: (0, 0)), _row_spec(d), _row_spec(d), _vec_spec(d)],
        out_shape=[jax.ShapeDtypeStruct((8, LANES), F32), pltpu.HBM((t, d), F32), pltpu.HBM((t, d), BF16),
                   jax.ShapeDtypeStruct((1, d), F32)],
        compiler_params=_params(("arbitrary",), 40),
    )(x, gain, target)


def _swap_halves(x):
    lane = lax.broadcasted_iota(jnp.int32, x.shape, 1)
    return jnp.where((lane % HEAD_DIM) < HEAD_DIM // 2, pltpu.roll(x, LANES - HEAD_DIM // 2, 1),
                     pltpu.roll(x, HEAD_DIM // 2, 1))


def _rope_tables(t):
    inv_freq = ROPE_THETA ** (-jnp.arange(0, HEAD_DIM, 2, dtype=F32) / HEAD_DIM)
    ang = jnp.arange(t, dtype=F32)[:, None] * inv_freq[None, :]
    cos = jnp.tile(jnp.cos(ang), (1, LANES // (HEAD_DIM // 2)))
    sin = jnp.tile(jnp.sin(ang), (1, LANES // (HEAD_DIM // 2)))
    lane = jnp.arange(LANES)[None, :]
    return cos, jnp.where((lane % HEAD_DIM) < HEAD_DIM // 2, -sin, sin)


def _rope_fwd(proj, cos, sin, name):
    t = proj.shape[0]
    scale = HEAD_DIM ** -0.5

    def body(p_ref, c_ref, s_ref, o_ref):
        cos_v, sin_v = c_ref[...], s_ref[...]
        for b in range(N_PBLK):
            cols = slice(b * LANES, (b + 1) * LANES)
            v = p_ref[:, cols]
            if b in ROPE_BLKS:
                v = v * cos_v + _swap_halves(v) * sin_v
            if b in QSCALE_BLKS:
                v = v * scale
            o_ref[:, cols] = v.astype(BF16)

    return pl.pallas_call(
        body, name=name, grid=(t // ROWS,),
        in_specs=[_row_spec(IN_COLS), _row_spec(LANES), _row_spec(LANES)], out_specs=_row_spec(IN_COLS),
        out_shape=pltpu.HBM((t, IN_COLS), BF16), compiler_params=_params(("arbitrary",), 40),
    )(_in_hbm(proj), _in_hbm(cos), _in_hbm(sin))


def _rope_bwd(grads, cos, sin, name):
    t = grads[0].shape[0]
    scale = HEAD_DIM ** -0.5
    group = N_HEADS_B // N_KV_B

    def body(*refs):
        c_ref, s_ref, o_ref = refs[9], refs[10], refs[11]
        cos_v, sin_v = c_ref[...], s_ref[...]

        def kv_sum(ref):
            parts = []
            for g in range(N_KV_B):
                acc = ref[:, g * group * HEAD_DIM:(g * group + 1) * HEAD_DIM]
                for h in range(g * group + 1, (g + 1) * group):
                    acc = acc + ref[:, h * HEAD_DIM:(h + 1) * HEAD_DIM]
                parts.append(acc)
            return jnp.concatenate(parts, axis=1)

        def emit(b, v):
            if b in ROPE_BLKS:
                v = v * cos_v - _swap_halves(v) * sin_v
            if b in QSCALE_BLKS:
                v = v * scale
            o_ref[:, b * LANES:(b + 1) * LANES] = v.astype(BF16)

        starts = (A_BLK[0], A_BLK[1], A_BLK[2], B_BLK[0], None, None, C_BLK[0], C_BLK[1], C_BLK[2])
        for idx, start in enumerate(starts):
            if start is None:
                continue
            for j in range(refs[idx].shape[1] // LANES):
                emit(start + j, refs[idx][:, j * LANES:(j + 1) * LANES])
        emit(B_BLK[1], kv_sum(refs[4]))
        emit(B_BLK[2], kv_sum(refs[5]))

    return pl.pallas_call(
        body, name=name, grid=(t // ROWS,),
        in_specs=[_row_spec(g.shape[1]) for g in grads] + [_row_spec(LANES), _row_spec(LANES)],
        out_specs=_row_spec(IN_COLS),
        out_shape=pltpu.HBM((t, IN_COLS), BF16), compiler_params=_params(("arbitrary",), 40),
    )(*[_in_hbm(g) for g in grads], _in_hbm(cos), _in_hbm(sin))


GROUP_COLS = ((0, WIDTH_A), (WIDTH_A, WIDTH_A + WIDTH_B), (WIDTH_A + WIDTH_B, D_MODEL))


def _mix_fwd(oa, ob, oc, gain, name):
    t = oa.shape[0]

    def body(a_ref, b_ref, c_ref, g_ref, o_ref):
        for ref, (lo, hi) in zip((a_ref, b_ref, c_ref), GROUP_COLS):
            _, n = _rms_stats(ref[...])
            o_ref[:, lo:hi] = (n * g_ref[:, lo:hi]).astype(BF16)

    return pl.pallas_call(
        body, name=name, grid=(t // ROWS,),
        in_specs=[_row_spec(WIDTH_A), _row_spec(WIDTH_B), _row_spec(WIDTH_C), _vec_spec(D_MODEL)],
        out_specs=_row_spec(D_MODEL),
        out_shape=pltpu.HBM((t, D_MODEL), BF16), compiler_params=_params(("arbitrary",), 32),
    )(_in_hbm(oa), _in_hbm(ob), _in_hbm(oc), _in_hbm(gain))


def _mix_bwd(oa, ob, oc, gain, dmixed, name, after=()):
    t = oa.shape[0]
    n_after = len(after)

    def body(a_ref, b_ref, c_ref, g_ref, dm_ref, *rest):
        da_ref, db_ref, dc_ref, dg_ref = rest[n_after:]
        first = pl.program_id(0) == 0
        for ref, dref, (lo, hi) in zip((a_ref, b_ref, c_ref), (da_ref, db_ref, dc_ref), GROUP_COLS):
            r, n = _rms_stats(ref[...])
            dm = dm_ref[:, lo:hi]
            dn = dm * g_ref[:, lo:hi]
            dref[...] = r * (dn - n * jnp.mean(dn * n, axis=-1, keepdims=True))
            part = jnp.sum(dm * n, axis=0, keepdims=True)

            @pl.when(first)
            def _():
                dg_ref[:, lo:hi] = part

            @pl.when(jnp.logical_not(first))
            def _():
                dg_ref[:, lo:hi] += part

    return pl.pallas_call(
        body, name=name, grid=(t // ROWS,),
        in_specs=[_row_spec(WIDTH_A), _row_spec(WIDTH_B), _row_spec(WIDTH_C), _vec_spec(D_MODEL), _row_spec(D_MODEL)]
        + [ANY_SPEC] * n_after,
        out_specs=[_row_spec(WIDTH_A), _row_spec(WIDTH_B), _row_spec(WIDTH_C), _vec_spec(D_MODEL)],
        out_shape=[pltpu.HBM((t, WIDTH_A), F32), pltpu.HBM((t, WIDTH_B), F32), pltpu.HBM((t, WIDTH_C), F32),
                   jax.ShapeDtypeStruct((1, D_MODEL), F32)],
        compiler_params=_params(("arbitrary",), 40),
    )(_in_hbm(oa), _in_hbm(ob), _in_hbm(oc), _in_hbm(gain), _in_hbm(dmixed), *after)


FF_COLS = 256


SUBLANES = 8
CHUNK = 128
HALO = SUBLANES


def _pad_rows(dst_ref, src_ref):
    t, cols = src_ref.shape
    dst_ref[0:HALO, :] = jnp.zeros((HALO, cols), F32)
    dst_ref[HALO:HALO + t, :] = src_ref[...]
    dst_ref[HALO + t:t + 2 * HALO, :] = jnp.zeros((HALO, cols), F32)


def _roll_rows(x, by):
    return pltpu.roll(x, by % x.shape[0], 0)


def _gate_val(pad_ref, r0, w_ref, b_ref):
    ext = [pad_ref[h, pl.ds(r0, CHUNK + 2 * HALO), :] for h in range(2)]
    before = [_roll_rows(e, 1) for e in ext]
    after = [_roll_rows(e, -1) for e in ext]
    gate, val = ((before[h] * w_ref[h, 0:1, :] + ext[h] * w_ref[h, 1:2, :]) + after[h] * w_ref[h, 2:3, :] + b_ref[h]
                 for h in range(2))
    return gate, val, ext, before, after


def _ff_specs(t):
    u_spec = pl.BlockSpec((2, t, FF_COLS), lambda j: (0, 0, j))
    w_spec = pl.BlockSpec((2, 3, FF_COLS), lambda j: (0, 0, j))
    b_spec = pl.BlockSpec((2, 1, FF_COLS), lambda j: (0, 0, j))
    return u_spec, w_spec, b_spec


def _convgate_fwd(u0, conv_w, conv_b, name):
    t = u0.shape[1]
    u_spec, w_spec, b_spec = _ff_specs(t)

    def body(u_ref, w_ref, b_ref, o_ref, pad_ref):
        for h in range(2):
            _pad_rows(pad_ref.at[h], u_ref.at[h])

        def chunk(ci, carry):
            r0 = pl.multiple_of(ci * CHUNK, CHUNK)
            gate, val, _, _, _ = _gate_val(pad_ref, r0, w_ref, b_ref)
            act = gate * jax.nn.sigmoid(gate) * val
            o_ref[pl.ds(r0, CHUNK), :] = act[HALO:HALO + CHUNK].astype(BF16)
            return carry

        lax.fori_loop(0, t // CHUNK, chunk, 0)

    return pl.pallas_call(
        body, name=name, grid=(D_FF // FF_COLS,), in_specs=[u_spec, w_spec, b_spec],
        out_specs=pl.BlockSpec((t, FF_COLS), lambda j: (0, j)),
        out_shape=pltpu.HBM((t, D_FF), BF16),
        scratch_shapes=[pltpu.VMEM((2, t + 2 * HALO, FF_COLS), F32)],
        compiler_params=_params(("arbitrary",), 48),
    )(_in_hbm(u0), conv_w, conv_b)


def _convgate_bwd(u0, conv_w, conv_b, d_act, name):
    t = u0.shape[1]
    u_spec, w_spec, b_spec = _ff_specs(t)

    def body(u_ref, w_ref, b_ref, da_ref, du_ref, dw_ref, db_ref, pad_ref, da_pad_ref, sums_ref):
        for h in range(2):
            _pad_rows(pad_ref.at[h], u_ref.at[h])
        _pad_rows(da_pad_ref, da_ref)
        sums_ref[...] = jnp.zeros_like(sums_ref)
        inner = slice(HALO, HALO + CHUNK)

        def fold(x):
            return jnp.sum(x.reshape(CHUNK // SUBLANES, SUBLANES, x.shape[1]), axis=0)

        def chunk(ci, carry):
            r0 = pl.multiple_of(ci * CHUNK, CHUNK)
            gate, val, ext, before, after = _gate_val(pad_ref, r0, w_ref, b_ref)
            sig = jax.nn.sigmoid(gate)
            da = da_pad_ref[pl.ds(r0, CHUNK + 2 * HALO), :]
            d_half = (da * val * (sig * (1.0 + gate * (1.0 - sig))), da * (gate * sig))
            for h in range(2):
                du = d_half[h]
                for k, term in enumerate((du, du * before[h], du * ext[h], du * after[h])):
                    sums_ref[h, k] += fold(term[inner])
                du0 = (_roll_rows(du, -1) * w_ref[h, 0:1, :] + du * w_ref[h, 1:2, :]) + _roll_rows(du, 1) * w_ref[h, 2:3, :]
                du_ref[h, pl.ds(r0, CHUNK), :] = du0[inner].astype(BF16)
            return carry

        lax.fori_loop(0, t // CHUNK, chunk, 0)
        for h in range(2):
            db_ref[h] = jnp.sum(sums_ref[h, 0], axis=0, keepdims=True)
            for k in range(3):
                dw_ref[h, k:k + 1, :] = jnp.sum(sums_ref[h, k + 1], axis=0, keepdims=True)

    return pl.pallas_call(
        body, name=name, grid=(D_FF // FF_COLS,),
        in_specs=[u_spec, w_spec, b_spec, pl.BlockSpec((t, FF_COLS), lambda j: (0, j))],
        out_specs=[u_spec, w_spec, b_spec],
        out_shape=[pltpu.HBM((2, t, D_FF), BF16), jax.ShapeDtypeStruct((2, 3, D_FF), F32),
                   jax.ShapeDtypeStruct((2, 1, D_FF), F32)],
        scratch_shapes=[pltpu.VMEM((2, t + 2 * HALO, FF_COLS), F32), pltpu.VMEM((t + 2 * HALO, FF_COLS), F32),
                        pltpu.VMEM((2, 4, SUBLANES, FF_COLS), F32)],
        compiler_params=_params(("arbitrary",), 56),
    )(_in_hbm(u0), conv_w, conv_b, _in_hbm(d_act))


class _Group:
    def __init__(self, heads, blks, kv_rows, n_win, gqa, bias_per_head):
        self.heads = heads
        self.pairs = heads // 2
        self.q_blk, self.k_blk, self.v_blk = blks
        self.kv_rows = kv_rows
        self.n_win = n_win
        self.full = kv_rows == SEQ
        self.gqa = gqa
        self.bias_per_head = bias_per_head
        self.width = heads * HEAD_DIM
        self.keys = kv_rows * n_win


GROUP_A = _Group(N_HEADS_A, A_BLK, SEQ, 1, False, False)
GROUP_B = _Group(N_HEADS_B, B_BLK, WINDOW_B, 4, True, False)
GROUP_C = _Group(N_HEADS_C, C_BLK, QB, 3, False, True)


def _win_start(grp, i):
    return jnp.clip(i * (QB // grp.kv_rows) - 1, 0, SEQ // grp.kv_rows - grp.n_win)


def _win_variant(i):
    return jnp.minimum(i, 1) + (i == NQB - 1).astype(jnp.int32)


def _attn_in_specs(grp, t):
    q_spec = pl.BlockSpec((QB, LANES), lambda p, i: (i, grp.q_blk + p))

    def col(blk):
        return (lambda p: blk) if grp.gqa else (lambda p: blk + p)

    def kv_specs(blk):
        c = col(blk)
        if grp.full:
            return [pl.BlockSpec((t, LANES), lambda p, i: (0, c(p)))]
        return [pl.BlockSpec((grp.kv_rows, LANES),
                             functools.partial(lambda p, i, w: (_win_start(grp, i) + w, c(p)), w=w))
                for w in range(grp.n_win)]

    nwk = grp.keys
    if grp.bias_per_head:
        bias_spec = pl.BlockSpec((2, None, QB, nwk), lambda p, i: (p, _win_variant(i), 0, 0))
    elif grp.full:
        bias_spec = pl.BlockSpec((1, None, QB, nwk), lambda p, i: (0, i, 0, 0))
    else:
        bias_spec = pl.BlockSpec((1, None, QB, nwk), lambda p, i: (0, _win_variant(i), 0, 0))
    sink_spec = pl.BlockSpec((1, LANES), lambda p, i: (0, p))
    return q_spec, kv_specs(grp.k_blk), kv_specs(grp.v_blk), bias_spec, sink_spec


def _head_kv(grp, whole, e, p):
    lo, hi = whole[:, :HEAD_DIM], whole[:, HEAD_DIM:]
    if grp.gqa:
        return jnp.where(2 * p + e >= N_HEADS_B // N_KV_B, hi, lo)
    return hi if e else lo


def _softmax_parts(q, k, bias, sink):
    s = lax.dot_general(q, k, (((1,), (1,)), ((), ())), preferred_element_type=F32) + bias
    m = jnp.maximum(jnp.max(s, axis=-1, keepdims=True), sink)
    pe = jnp.exp(s - m)
    denom = jnp.sum(pe, axis=-1, keepdims=True) + jnp.exp(sink - m)
    return pe, m, 1.0 / denom


def _attn_fwd(grp, proj, bias, sink, name):
    t = proj.shape[0]
    q_spec, k_specs, v_specs, bias_spec, sink_spec = _attn_in_specs(grp, t)
    nkv = len(k_specs)

    def body(*refs):
        q_ref = refs[0]
        k_refs, v_refs = refs[1:1 + nkv], refs[1 + nkv:1 + 2 * nkv]
        bias_ref, sink_ref, o_ref = refs[1 + 2 * nkv:4 + 2 * nkv]
        p = pl.program_id(0)
        k_all = jnp.concatenate([r[...] for r in k_refs], axis=0)
        v_all = jnp.concatenate([r[...] for r in v_refs], axis=0)
        outs = []
        for e in range(2):
            q = q_ref[:, e * HEAD_DIM:(e + 1) * HEAD_DIM]
            k = _head_kv(grp, k_all, e, p)
            v = _head_kv(grp, v_all, e, p)
            snk = sink_ref[0:1, e * HEAD_DIM:e * HEAD_DIM + 1]
            pe, _, inv = _softmax_parts(q, k, bias_ref[e if grp.bias_per_head else 0], snk)
            outs.append(jnp.dot(pe.astype(BF16), v, preferred_element_type=F32) * inv)
        o_ref[...] = jnp.concatenate(outs, axis=1)

    return pl.pallas_call(
        body, name=name, grid=(grp.pairs, NQB),
        in_specs=[q_spec, *k_specs, *v_specs, bias_spec, sink_spec],
        out_specs=pl.BlockSpec((QB, LANES), lambda p, i: (i, p)),
        out_shape=pltpu.HBM((t, grp.width), F32),
        compiler_params=_params(("arbitrary", "arbitrary"), 48),
    )(*([_in_hbm(proj)] * (1 + 2 * nkv)), _in_hbm(bias), sink)


def _attn_bwd(grp, proj, bias, sink, out, d_out, name):
    t = proj.shape[0]
    q_spec, k_specs, v_specs, bias_spec, sink_spec = _attn_in_specs(grp, t)
    nkv = len(k_specs)
    n_off = 2 * NA_ROWS - 1
    rows_q = QB // GRID_W
    o_spec = pl.BlockSpec((QB, LANES), lambda p, i: (i, p))
    acc_spec = pl.BlockSpec((t, LANES), lambda p, i: (0, p))
    out_specs = [o_spec, acc_spec, acc_spec, pl.BlockSpec((None, 8, LANES), lambda p, i: (p, 0, 0))]
    out_shape = [pltpu.HBM((t, grp.width), F32)] * 3 + [jax.ShapeDtypeStruct((grp.pairs, 8, LANES), F32)]
    if grp.bias_per_head:
        out_specs.append(pl.BlockSpec((2, n_off, GRID_W, GRID_W), lambda p, i: (p, 0, 0, 0)))
        out_shape.append(jax.ShapeDtypeStruct((grp.heads, n_off, GRID_W, GRID_W), F32))

    def body(*refs):
        q_ref = refs[0]
        k_refs, v_refs = refs[1:1 + nkv], refs[1 + nkv:1 + 2 * nkv]
        bias_ref, sink_ref, o_ref, do_ref = refs[1 + 2 * nkv:5 + 2 * nkv]
        dq_ref, dk_ref, dv_ref, dsink_ref = refs[5 + 2 * nkv:9 + 2 * nkv]
        dbias_ref = refs[9 + 2 * nkv] if grp.bias_per_head else None
        p, i = pl.program_id(0), pl.program_id(1)

        @pl.when(i == 0)
        def _():
            dk_ref[...] = jnp.zeros_like(dk_ref)
            dv_ref[...] = jnp.zeros_like(dv_ref)
            dsink_ref[...] = jnp.zeros_like(dsink_ref)
            if dbias_ref is not None:
                dbias_ref[...] = jnp.zeros_like(dbias_ref)

        k_all = jnp.concatenate([r[...] for r in k_refs], axis=0)
        v_all = jnp.concatenate([r[...] for r in v_refs], axis=0)
        start = 0 if grp.full else _win_start(grp, i)
        dqs, dks, dvs, dsinks = [], [], [], []
        for e in range(2):
            cols = slice(e * HEAD_DIM, (e + 1) * HEAD_DIM)
            q = q_ref[:, cols]
            k = _head_kv(grp, k_all, e, p)
            v = _head_kv(grp, v_all, e, p)
            snk = sink_ref[0:1, e * HEAD_DIM:e * HEAD_DIM + 1]
            pe, m, inv = _softmax_parts(q, k, bias_ref[e if grp.bias_per_head else 0], snk)
            prob = pe * inv
            do = do_ref[:, cols]
            do_b = do.astype(BF16)
            delta = jnp.sum(do * o_ref[:, cols], axis=-1, keepdims=True)
            dp = lax.dot_general(do_b, v, (((1,), (1,)), ((), ())), preferred_element_type=F32)
            ds = prob * (dp - delta)
            ds_b = ds.astype(BF16)
            dqs.append(jnp.dot(ds_b, k, preferred_element_type=F32))
            dks.append(lax.dot_general(ds_b, q, (((0,), (0,)), ((), ())), preferred_element_type=F32))
            dvs.append(lax.dot_general(prob.astype(BF16), do_b, (((0,), (0,)), ((), ())), preferred_element_type=F32))
            dsinks.append(-jnp.sum(jnp.exp(snk - m) * inv * delta, axis=0, keepdims=True))
            if dbias_ref is not None:
                shift = (i * QB - start * grp.kv_rows) // GRID_W
                for rq in range(rows_q):
                    for rk in range(grp.keys // GRID_W):
                        off = jnp.clip(rk - rq + (NA_ROWS - 1) - shift, 0, n_off - 1)
                        dbias_ref[e, off] += ds[rq * GRID_W:(rq + 1) * GRID_W, rk * GRID_W:(rk + 1) * GRID_W]
        dq_ref[...] = jnp.concatenate(dqs, axis=1)
        rows = pl.ds(0, t) if grp.full else pl.ds(pl.multiple_of(start * grp.kv_rows, grp.kv_rows), grp.keys)
        dk_ref[rows, :] += jnp.concatenate(dks, axis=1)
        dv_ref[rows, :] += jnp.concatenate(dvs, axis=1)
        lane = lax.broadcasted_iota(jnp.int32, (8, LANES), 1)
        dsink_ref[...] += jnp.where(lane < HEAD_DIM, dsinks[0], dsinks[1])

    return pl.pallas_call(
        body, name=name, grid=(grp.pairs, NQB),
        in_specs=[q_spec, *k_specs, *v_specs, bias_spec, sink_spec, o_spec, o_spec],
        out_specs=out_specs, out_shape=out_shape,
        compiler_params=_params(("arbitrary", "arbitrary"), 56),
    )(*([_in_hbm(proj)] * (1 + 2 * nkv)), _in_hbm(bias), sink, _in_hbm(out), _in_hbm(d_out))


DILATED_CONFIGS = ((128, 1), (512, 4), (2048, 16))


def _bias_a():
    d = jnp.arange(SEQ)[None, :] - jnp.arange(SEQ)[:, None]
    mult = jnp.zeros((SEQ, SEQ), F32)
    for window, r in DILATED_CONFIGS:
        reach = (window // (2 * r)) * r
        mult = mult + ((d % r == 0) & (jnp.abs(d) <= reach)).astype(F32)
    return jnp.where(mult > 0, jnp.log(jnp.maximum(mult, 1.0)), NEG_INF).reshape(1, NQB, QB, SEQ)


def _bias_b():
    row = jnp.arange(QB)[None, :, None]
    col = jnp.arange(GROUP_B.keys)[None, None, :]
    var = jnp.arange(3)[:, None, None]
    d = col - (GROUP_B.kv_rows * var + row)
    return jnp.where(jnp.abs(d) <= WINDOW_B, 0.0, NEG_INF).astype(F32)[None]


def _offset_onehot():
    c = jnp.arange(GRID_W)[:, None, None]
    c2 = jnp.arange(GRID_W)[None, :, None]
    b = jnp.arange(LANES)[None, None, :]
    return (c2 - c + NA_COLS - 1 == b).astype(BF16).reshape(GRID_W * GRID_W, LANES)


def _split_dot(x, g):
    hi = x.astype(BF16)
    rest = x - hi.astype(F32)
    mid = rest.astype(BF16)
    lo = (rest - mid.astype(F32)).astype(BF16)
    return (jnp.dot(hi, g, preferred_element_type=F32) + jnp.dot(mid, g, preferred_element_type=F32)
            + jnp.dot(lo, g, preferred_element_type=F32))


def _table_mm(x, g, name):
    def body(x_ref, g_ref, o_ref):
        o_ref[...] = _split_dot(x_ref[...], g_ref[...])

    return pl.pallas_call(
        body, name=name, out_shape=jax.ShapeDtypeStruct((x.shape[0], g.shape[1]), F32),
        in_specs=[pl.BlockSpec(memory_space=pltpu.VMEM)] * 2, out_specs=pl.BlockSpec(memory_space=pltpu.VMEM),
        compiler_params=pltpu.CompilerParams(vmem_limit_bytes=32 * MIB),
    )(x, g)


N_OFF = 2 * NA_ROWS - 1
TABLE_ROWS = 152


def _bias_c(rpb):
    table = jnp.zeros((TABLE_ROWS, LANES), F32).at[:N_HEADS_C * N_OFF, :2 * NA_COLS - 1].set(
        rpb.reshape(N_HEADS_C * N_OFF, 2 * NA_COLS - 1))
    tiles = _table_mm(table, _offset_onehot().T, "rpb_tiles")[:N_HEADS_C * N_OFF]
    tiles = tiles.reshape(N_HEADS_C, N_OFF, GRID_W, GRID_W)
    c = jnp.arange(GRID_W)
    col_start = jnp.clip(c - NA_COLS // 2, 0, GRID_W - NA_COLS)
    col_ok = (c[None, :] >= col_start[:, None]) & (c[None, :] < col_start[:, None] + NA_COLS)
    tiles = jnp.where(col_ok, tiles, NEG_INF)
    rows_q = QB // GRID_W
    rows_k = GROUP_C.keys // GRID_W

    def body(t_ref, o_ref):
        for var in range(3):
            for rq in range(rows_q):
                r_l = rows_q * var + rq
                first = min(max(r_l - NA_ROWS // 2, 0), rows_k - NA_ROWS)
                for rk in range(rows_k):
                    if first <= rk < first + NA_ROWS:
                        tile = t_ref[rk - r_l + NA_ROWS - 1]
                    else:
                        tile = jnp.full((GRID_W, GRID_W), NEG_INF, F32)
                    o_ref[var, rq * GRID_W:(rq + 1) * GRID_W, rk * GRID_W:(rk + 1) * GRID_W] = tile

    return pl.pallas_call(
        body, name="bias_c", grid=(N_HEADS_C,),
        in_specs=[pl.BlockSpec((None, N_OFF, GRID_W, GRID_W), lambda h: (h, 0, 0, 0))],
        out_specs=pl.BlockSpec((None, 3, QB, GROUP_C.keys), lambda h: (h, 0, 0, 0)),
        out_shape=jax.ShapeDtypeStruct((N_HEADS_C, 3, QB, GROUP_C.keys), F32),
        compiler_params=_params(("arbitrary",), 32),
    )(tiles)


def _rpb_grad(d_tiles):
    flat = jnp.zeros((TABLE_ROWS, GRID_W * GRID_W), F32).at[:N_HEADS_C * N_OFF].set(
        d_tiles.reshape(N_HEADS_C * N_OFF, GRID_W * GRID_W))
    out = _table_mm(flat, _offset_onehot(), "rpb_grad")
    return out[:N_HEADS_C * N_OFF, :2 * NA_COLS - 1].reshape(N_HEADS_C, N_OFF, 2 * NA_COLS - 1)


def _sink_lanes(sink):
    return jnp.repeat(sink.astype(F32), HEAD_DIM)[None, :]


def _attention_fwd(proj_r, sink_b, bias_a, bias_b, bias_c):
    no_sink_a = jnp.full((1, WIDTH_A), NEG_INF, F32)
    no_sink_c = jnp.full((1, WIDTH_C), NEG_INF, F32)
    oa = _attn_fwd(GROUP_A, proj_r, bias_a, no_sink_a, "attn_a_fwd")
    ob = _attn_fwd(GROUP_B, proj_r, bias_b, _sink_lanes(sink_b), "attn_b_fwd")
    oc = _attn_fwd(GROUP_C, proj_r, bias_c, no_sink_c, "attn_c_fwd")
    return oa, ob, oc


def _attention_bwd(proj_r, sink_b, bias_a, bias_b, bias_c, outs, d_outs, cos, sin):
    no_sink_a = jnp.full((1, WIDTH_A), NEG_INF, F32)
    no_sink_c = jnp.full((1, WIDTH_C), NEG_INF, F32)
    dqa, dka, dva, _ = _attn_bwd(GROUP_A, proj_r, bias_a, no_sink_a, outs[0], d_outs[0], "attn_a_bwd")
    dqb, dkb, dvb, dsink = _attn_bwd(GROUP_B, proj_r, bias_b, _sink_lanes(sink_b), outs[1], d_outs[1], "attn_b_bwd")
    dqc, dkc, dvc, _, d_tiles = _attn_bwd(GROUP_C, proj_r, bias_c, no_sink_c, outs[2], d_outs[2], "attn_c_bwd")
    d_proj = _rope_bwd((dqa, dka, dva, dqb, dkb, dvb, dqc, dkc, dvc), cos, sin, "rope_bwd")
    d_sink = dsink[:, 0, :].reshape(GROUP_B.pairs, 2, HEAD_DIM)[:, :, 0].reshape(N_HEADS_B)
    return d_proj, d_sink, _rpb_grad(d_tiles)


def _adamw(w, g, m, v, name):
    r, c = w.shape
    rows = r
    for cand in (512, 256, 128, 64, 32, 16, 8):
        if r % cand == 0 and cand * c * 4 <= MIB:
            rows = cand
            break
    spec = pl.BlockSpec((rows, c), lambda i: (i, 0))

    def body(w_ref, g_ref, m_ref, v_ref, d_ref, mo_ref, vo_ref):
        d_ref[...], mo_ref[...], vo_ref[...] = _adamw_step(w_ref[...], g_ref[...], m_ref[...], v_ref[...])

    return pl.pallas_call(
        body, name=name, grid=(r // rows,), in_specs=[spec] * 4, out_specs=[spec] * 3,
        out_shape=[jax.ShapeDtypeStruct((r, c), F32)] * 3, compiler_params=_params(("arbitrary",), 32),
    )(w, g, m, v)


def _adamw_step(w, grad, m, v):
    m_new = ADAM_B1 * m + (1.0 - ADAM_B1) * grad
    v_new = ADAM_B2 * v + (1.0 - ADAM_B2) * jnp.square(grad)
    m_hat = m_new / (1.0 - ADAM_B1 ** ADAM_STEP)
    v_hat = v_new / (1.0 - ADAM_B2 ** ADAM_STEP)
    return -ADAM_LR * (m_hat / (jnp.sqrt(v_hat) + ADAM_EPS) + ADAM_WD * w), m_new, v_new


def _adamw_layer(w, g, m, v, layer, prev, name):
    _, r, c = w.shape
    rows = next(cand for cand in (512, 256, 128, 64, 32, 16, 8) if r % cand == 0 and cand * c * 4 <= MIB)
    spec = pl.BlockSpec((None, rows, c), lambda i: (layer, i, 0))
    n_prev = 0 if prev is None else 4

    def body(w_ref, g_ref, m_ref, v_ref, *rest):
        go_ref, d_ref, mo_ref, vo_ref = rest[n_prev:]
        grad = g_ref[...]
        go_ref[...] = grad
        d_ref[...], mo_ref[...], vo_ref[...] = _adamw_step(w_ref[...], grad, m_ref[...], v_ref[...])

    return pl.pallas_call(
        body, name=name, grid=(r // rows,), in_specs=[spec] * 4 + [ANY_SPEC] * n_prev, out_specs=[spec] * 4,
        out_shape=[jax.ShapeDtypeStruct(w.shape, F32)] * 4,
        input_output_aliases={4 + i: i for i in range(n_prev)}, compiler_params=_params(("arbitrary",), 32),
    )(w, g, m, v, *(prev or ()))


def _layer_fwd(x0, p, weight, tabs):
    h1 = _rmsnorm_fwd(x0, p["ln_attn"], "ln_attn_fwd")
    proj = _mm_nn(h1, weight("w_in", h1), cols=True, tn=256, tk=D_MODEL, out_dtype=F32, name="mm_in")
    proj_r = _rope_fwd(proj, tabs["cos"], tabs["sin"], "rope_fwd")
    outs = _attention_fwd(proj_r, p["sink_b"], tabs["bias_a"], tabs["bias_b"], p["bias_c"])
    mixed = _mix_fwd(*outs, p["mix_gain"], "mix_fwd")
    x1 = _mm_nn(mixed, weight("w_out", mixed), cols=False, tn=256, tk=D_MODEL, out_dtype=F32, name="mm_out",
                residual=x0)
    h2 = _rmsnorm_fwd(x1, p["ln_ffn"], "ln_ffn_fwd")
    u0 = _mm_nn(h2, weight("w_up", h2), cols=True, tn=256, tk=D_MODEL, out_dtype=F32, name="mm_up", out_split=2)
    act = _convgate_fwd(u0, p["conv_w"], p["conv_b"], "convgate_fwd")
    x2 = _mm_nn(act, weight("w_down", act), cols=False, tn=512, tk=D_FF // 2, out_dtype=F32, name="mm_down",
                residual=x1)
    return x2, (x0, h1, proj_r, outs, mixed, x1, h2, u0, act)


def _layer_bwd(dx2, dx2_b, saved, p, big, tabs, begin, finish, pending):
    x0, h1, proj_r, outs, mixed, x1, h2, u0, act = saved
    d_act = _mm_nt(dx2_b, big["w_down"], cols=False, to=512, tr=D_MODEL, out_dtype=F32, name="nt_down",
                   after=[pending[1]] if pending else [])
    g_down = _mm_tn(act, dx2_b, tk=D_FF // N_SHARDS, tn=512, shards=-N_SHARDS, name="tn_down")
    du0, d_conv_w, d_conv_b = _convgate_bwd(u0, p["conv_w"], p["conv_b"], d_act, "convgate_bwd")
    token = [finish(pending[0], [du0])] if pending else []
    dh2 = _mm_nt(du0, big["w_up"], cols=True, to=1024, tr=D_FF // 4, out_dtype=F32, name="nt_up", after=token)
    g_up = _mm_tn(h2, du0, tk=512, tn=D_FF // 4, shards=N_SHARDS, name="tn_up")
    first, token = begin({"w_down": g_down, "w_up": g_up})
    dx1, dx1_b, d_ln_ffn = _rmsnorm_bwd(x1, p["ln_ffn"], dh2, dx2, "ln_ffn_bwd", after=[token])
    d_mixed = _mm_nt(dx1_b, big["w_out"], cols=False, to=512, tr=D_MODEL, out_dtype=F32, name="nt_out")
    g_out = _mm_tn(mixed, dx1_b, tk=D_MODEL // N_SHARDS, tn=512, shards=-N_SHARDS, name="tn_out")
    token = finish(first, [g_out])
    *d_outs, d_mix_gain = _mix_bwd(*outs, p["mix_gain"], d_mixed, "mix_bwd", after=[token])
    d_proj, d_sink, d_rpb = _attention_bwd(proj_r, p["sink_b"], tabs["bias_a"], tabs["bias_b"], p["bias_c"], outs,
                                           d_outs, tabs["cos"], tabs["sin"])
    dh1 = _mm_nt(d_proj, big["w_in"], cols=True, to=1024, tr=IN_COLS // N_SHARDS, out_dtype=F32, name="nt_in")
    g_in = _mm_tn(h1, d_proj, tk=512, tn=IN_COLS // N_SHARDS, shards=N_SHARDS, name="tn_in")
    dx0, dx0_b, d_ln_attn = _rmsnorm_bwd(x0, p["ln_attn"], dh1, dx1, "ln_attn_bwd")
    small = {"ln_attn": d_ln_attn, "sink_b": d_sink, "rpb_c": d_rpb, "mix_gain": d_mix_gain, "ln_ffn": d_ln_ffn,
             "conv_w": d_conv_w, "conv_b": d_conv_b}
    return dx0, dx0_b, small, begin({"w_out": g_out, "w_in": g_in})


HBM_SPEC = pl.BlockSpec(memory_space=pl.ANY)


def _place():
    x, y, c = lax.axis_index("x"), lax.axis_index("y"), lax.axis_index("c")
    chips = ((1 - x, y), (x, 1 - y), (1 - x, 1 - y))
    return x, y, c, chips


def _shard_index(px, py):
    return 2 * px + py


def _remote(src, dst, send_sem, recv_sem, to):
    return pltpu.make_async_remote_copy(src_ref=src, dst_ref=dst, send_sem=send_sem, recv_sem=recv_sem,
                                        device_id=to, device_id_type=MESH)


def _own_slot(w, layer, shard, name):
    _, r, c_dim = w.shape
    rows = r
    for cand in (512, 256, 128):
        if r % cand == 0 and cand * c_dim * 4 <= 2 * MIB:
            rows = cand
            break

    def body(s_ref, w_ref, o_ref):
        o_ref[...] = w_ref[...].astype(BF16)

    return pl.pallas_call(
        body, name=name,
        grid_spec=pltpu.PrefetchScalarGridSpec(
            num_scalar_prefetch=1, grid=(r // rows,),
            in_specs=[pl.BlockSpec((None, rows, c_dim), lambda i, s: (layer, i, 0))],
            out_specs=pl.BlockSpec((None, rows, c_dim), lambda i, s: (s[0], i, 0))),
        out_shape=jax.ShapeDtypeStruct((N_SHARDS, r, c_dim), BF16),
        compiler_params=_params(("arbitrary",), 32),
    )(shard.astype(jnp.int32).reshape(1), w)


def _gather_weights(bufs):
    n = len(bufs)

    def body(*refs):
        outs = refs[n:2 * n]
        send1, recv1, send2, recv2 = refs[2 * n:]
        x, y, c, chips = _place()
        me = _shard_index(x, y)
        sibling = (x, y, 1 - c)
        first = []
        for t in range(n):
            for j, (px, py) in enumerate(chips):
                mine = outs[t].at[c, me]
                cp = _remote(mine, mine, send1.at[t * 3 + j], recv1.at[t * 3 + j], (px, py, c))
                cp.start()
                first.append(cp)
        passed = []
        for t in range(n):
            for j, (px, py) in enumerate(chips):
                slot = outs[t].at[c, _shard_index(px, py)]
                _remote(slot, slot, send1.at[t * 3 + j], recv1.at[t * 3 + j], (px, py, c)).wait_recv()
                cp = _remote(slot, slot, send2.at[t * 3 + j], recv2.at[t * 3 + j], sibling)
                cp.start()
                passed.append(cp)
        for t in range(n):
            for j, (px, py) in enumerate(chips):
                slot = outs[t].at[1 - c, _shard_index(px, py)]
                _remote(slot, slot, send2.at[t * 3 + j], recv2.at[t * 3 + j], sibling).wait_recv()
        for cp in first + passed:
            cp.wait_send()

    return pl.pallas_call(
        body, name="gather_weights", in_specs=[HBM_SPEC] * n, out_specs=[HBM_SPEC] * n,
        out_shape=[jax.ShapeDtypeStruct(b.shape, b.dtype) for b in bufs],
        input_output_aliases={t: t for t in range(n)},
        scratch_shapes=[pltpu.SemaphoreType.DMA((n * 3,))] * 4,
    )(*bufs)


def _pair_exchange(bufs):
    n = len(bufs)

    def body(*refs):
        ins, outs = refs[:n], refs[n:2 * n]
        send, recv = refs[2 * n:]
        x, y, c, _ = _place()
        sibling = (x, y, 1 - c)
        cps = [_remote(ins[t].at[1 - c], outs[t], send.at[t], recv.at[t], sibling) for t in range(n)]
        for cp in cps:
            cp.start()
        for cp in cps:
            cp.wait()

    return pl.pallas_call(
        body, name="pair_exchange", in_specs=[HBM_SPEC] * n, out_specs=[HBM_SPEC] * n,
        out_shape=[jax.ShapeDtypeStruct(b.shape[1:], b.dtype) for b in bufs],
        scratch_shapes=[pltpu.SemaphoreType.DMA((n,))] * 2,
    )(*bufs)


def _pair_sum(own, other, name):
    _, s, r, c_dim = own.shape
    rows = min(r, LANES)
    per = r // rows
    layer = lax.axis_index("c").astype(jnp.int32).reshape(1)

    def body(layer_ref, a_ref, b_ref, o_ref):
        o_ref[...] = (a_ref[...] + b_ref[...]).astype(BF16)

    return pl.pallas_call(
        body, name=name,
        grid_spec=pltpu.PrefetchScalarGridSpec(
            num_scalar_prefetch=1, grid=(s * per,),
            in_specs=[pl.BlockSpec((None, None, rows, c_dim), lambda i, lay: (lay[0], i // per, i % per, 0)),
                      pl.BlockSpec((None, rows, c_dim), lambda i, lay: (i // per, i % per, 0))],
            out_specs=pl.BlockSpec((None, rows, c_dim), lambda i, lay: (i // per, i % per, 0))),
        out_shape=jax.ShapeDtypeStruct((s, r, c_dim), BF16), compiler_params=_params(("arbitrary",), 40),
    )(layer, own, other)


def _chip_exchange(bufs):
    n = len(bufs)

    def body(*refs):
        ins, outs = refs[:n], refs[n:2 * n]
        send, recv = refs[2 * n:]
        x, y, c, chips = _place()
        me = _shard_index(x, y)
        cps = []
        for t in range(n):
            for j, (px, py) in enumerate(chips):
                cp = _remote(ins[t].at[_shard_index(px, py)], outs[t].at[me], send.at[t * 3 + j], recv.at[t * 3 + j],
                             (px, py, c))
                cp.start()
                cps.append(cp)
        for t in range(n):
            for j, (px, py) in enumerate(chips):
                slot = outs[t].at[_shard_index(px, py)]
                _remote(slot, slot, send.at[t * 3 + j], recv.at[t * 3 + j], (px, py, c)).wait_recv()
        for cp in cps:
            cp.wait_send()

    return pl.pallas_call(
        body, name="chip_exchange", in_specs=[HBM_SPEC] * n, out_specs=[HBM_SPEC] * n,
        out_shape=[jax.ShapeDtypeStruct(b.shape, b.dtype) for b in bufs],
        scratch_shapes=[pltpu.SemaphoreType.DMA((n * 3,))] * 2,
    )(*bufs)


HBM_ONLY = pl.BlockSpec(memory_space=pltpu.HBM)
SEM_SPEC = pl.BlockSpec(memory_space=pltpu.SEMAPHORE)
DATAFLOW = pltpu.SideEffectType.DATAFLOW_SIDE_EFFECTING


def _in_hbm(a):
    return pltpu.with_memory_space_constraint(a, pltpu.HBM)


def _chip_exchange_start(bufs, name):
    n = len(bufs)

    def body(*refs):
        ins, lands = refs[:n], refs[n:2 * n]
        send, recv = refs[2 * n], refs[2 * n + 1]
        token = refs[-1]
        x, y, c, chips = _place()
        me = _shard_index(x, y)
        for t in range(n):
            for j, (px, py) in enumerate(chips):
                _remote(ins[t].at[_shard_index(px, py)], lands[t].at[me], send.at[t * 3 + j], recv.at[t * 3 + j],
                        (px, py, c)).start()
        token[...] = jnp.zeros_like(token)

    thru = [pltpu.HBM(b.shape, b.dtype) for b in bufs]
    res = pl.pallas_call(
        body, name=name,
        out_shape=(pltpu.SemaphoreType.DMA((n * 3,)), pltpu.SemaphoreType.DMA((n * 3,)), *thru, *thru,
                   jax.ShapeDtypeStruct((8, LANES), F32)),
        in_specs=[HBM_ONLY] * (2 * n),
        out_specs=(SEM_SPEC, SEM_SPEC, *([HBM_ONLY] * (2 * n)), pl.BlockSpec(memory_space=pltpu.VMEM)),
        input_output_aliases={i: 2 + i for i in range(2 * n)},
        compiler_params=pltpu.CompilerParams(has_side_effects=DATAFLOW),
    )(*[_in_hbm(b) for b in bufs], *[_in_hbm(lax.empty(b.shape, b.dtype)) for b in bufs])
    return res[0], res[1], list(res[2:2 + n]), list(res[2 + n:2 + 2 * n]), res[-1]


def _chip_exchange_wait(send, recv, bufs, lands, after, name):
    n = len(bufs)

    def body(*refs):
        ins, outs = refs[:n], refs[n:2 * n]
        send_ref, recv_ref = refs[2 * n], refs[2 * n + 1]
        x, y, c, chips = _place()
        for t in range(n):
            for j, (px, py) in enumerate(chips):
                sent = ins[t].at[_shard_index(px, py)]
                slot = outs[t].at[_shard_index(px, py)]
                cp = _remote(sent, slot, send_ref.at[t * 3 + j], recv_ref.at[t * 3 + j], (px, py, c))
                cp.wait_send()
                cp.wait_recv()

    thru = [pltpu.HBM(b.shape, b.dtype) for b in bufs]
    res = pl.pallas_call(
        body, name=name, out_shape=(*thru, *thru),
        in_specs=[HBM_ONLY] * (2 * n) + [SEM_SPEC, SEM_SPEC, pl.BlockSpec(memory_space=pl.ANY)],
        out_specs=[HBM_ONLY] * (2 * n),
        input_output_aliases={i: i for i in range(2 * n)},
        compiler_params=pltpu.CompilerParams(has_side_effects=DATAFLOW),
    )(*bufs, *lands, send, recv, after)
    return list(res[:n]), list(res[n:])


def _chip_sum(pair, landed, name):
    s, r, c_dim = pair.shape
    rows = min(r, LANES)
    shard = _shard_index(lax.axis_index("x"), lax.axis_index("y"))
    where = jnp.stack([shard, lax.axis_index("c")]).astype(jnp.int32)

    def landed_spec(k):
        return pl.BlockSpec((None, rows, c_dim), lambda i, w: (jnp.where(w[0] == k, (k + 1) % s, k), i, 0))

    def body(w_ref, own_ref, *rest):
        o_ref = rest[s]
        acc = None
        for k in range(s):
            term = jnp.where(w_ref[0] == k, own_ref[...], rest[k][...]).astype(F32)
            acc = term if acc is None else acc + term
        o_ref[...] = acc

    return pl.pallas_call(
        body, name=name,
        grid_spec=pltpu.PrefetchScalarGridSpec(
            num_scalar_prefetch=1, grid=(r // rows,),
            in_specs=[pl.BlockSpec((None, rows, c_dim), lambda i, w: (w[0], i, 0))] + [landed_spec(k) for k in range(s)],
            out_specs=pl.BlockSpec((None, rows, c_dim), lambda i, w: (w[1], i, 0))),
        out_shape=jax.ShapeDtypeStruct((DEPTH, r, c_dim), F32), compiler_params=_params(("arbitrary",), 40),
    )(where, pair, *([landed] * s))


def _sum_slots(buf, name):
    s, r, c_dim = buf.shape
    rows = min(r, LANES)

    def body(i_ref, o_ref):
        acc = i_ref[0].astype(F32)
        for k in range(1, s):
            acc = acc + i_ref[k].astype(F32)
        o_ref[...] = acc

    return pl.pallas_call(
        body, name=name, grid=(r // rows,),
        in_specs=[pl.BlockSpec((s, rows, c_dim), lambda i: (0, i, 0))],
        out_specs=pl.BlockSpec((rows, c_dim), lambda i: (i, 0)),
        out_shape=jax.ShapeDtypeStruct((r, c_dim), F32), compiler_params=_params(("arbitrary",), 40),
    )(buf)


def _pair_gather(bufs):
    n = len(bufs)

    def body(*refs):
        outs = refs[n:2 * n]
        send, recv = refs[2 * n:]
        x, y, c, _ = _place()
        sibling = (x, y, 1 - c)
        cps = [_remote(outs[t].at[c], outs[t].at[c], send.at[t], recv.at[t], sibling) for t in range(n)]
        for cp in cps:
            cp.start()
        for t in range(n):
            slot = outs[t].at[1 - c]
            _remote(slot, slot, send.at[t], recv.at[t], sibling).wait_recv()
        for cp in cps:
            cp.wait_send()

    return pl.pallas_call(
        body, name="pair_gather", in_specs=[HBM_SPEC] * n, out_specs=[HBM_SPEC] * n,
        out_shape=[jax.ShapeDtypeStruct(b.shape, b.dtype) for b in bufs],
        input_output_aliases={t: t for t in range(n)},
        scratch_shapes=[pltpu.SemaphoreType.DMA((n,))] * 2,
    )(*bufs)


N_DEV = 8


def _all_gather_small(vec, name, after=()):
    n_after = len(after)

    def body(v_ref, *rest):
        o_ref, send, recv, local_sem = rest[n_after:]
        x, y, c, _ = _place()
        me = 4 * x + 2 * y + c
        local = pltpu.make_async_copy(v_ref, o_ref.at[me], local_sem)
        local.start()
        flips = [(fx, fy, fc) for fx in (0, 1) for fy in (0, 1) for fc in (0, 1)][1:]
        peers = [((1 - x) if fx else x, (1 - y) if fy else y, (1 - c) if fc else c) for fx, fy, fc in flips]
        cps = [_remote(v_ref, o_ref.at[me], send.at[k], recv.at[k], peer) for k, peer in enumerate(peers)]
        for cp in cps:
            cp.start()
        for k, (px, py, pc) in enumerate(peers):
            slot = o_ref.at[4 * px + 2 * py + pc]
            _remote(slot, slot, send.at[k], recv.at[k], (px, py, pc)).wait_recv()
        for cp in cps:
            cp.wait_send()
        local.wait()

    return pl.pallas_call(
        body, name=name, in_specs=[HBM_SPEC] * (1 + n_after), out_specs=HBM_SPEC,
        out_shape=jax.ShapeDtypeStruct((N_DEV,) + vec.shape, vec.dtype),
        scratch_shapes=[pltpu.SemaphoreType.DMA((N_DEV - 1,))] * 2 + [pltpu.SemaphoreType.DMA(())],
    )(vec, *after)


def _peers(x, y, c):
    flips = [(fx, fy, fc) for fx in (0, 1) for fy in (0, 1) for fc in (0, 1)][1:]
    return [((1 - x) if fx else x, (1 - y) if fy else y, (1 - c) if fc else c) for fx, fy, fc in flips]


def _small_start(vec, after, name):
    n_after = len(after)

    def body(v_ref, slots_ref, *rest):
        send, recv = rest[n_after], rest[n_after + 1]
        token = rest[-1]
        x, y, c, _ = _place()
        me = 4 * x + 2 * y + c
        for k, peer in enumerate(_peers(x, y, c)):
            _remote(v_ref, slots_ref.at[me], send.at[k], recv.at[k], peer).start()
        token[...] = jnp.zeros_like(token)

    slots = jax.ShapeDtypeStruct((N_DEV,) + vec.shape, vec.dtype)
    res = pl.pallas_call(
        body, name=name,
        out_shape=(pltpu.SemaphoreType.DMA((N_DEV - 1,)), pltpu.SemaphoreType.DMA((N_DEV - 1,)),
                   pltpu.HBM(vec.shape, vec.dtype), pltpu.HBM(slots.shape, slots.dtype),
                   jax.ShapeDtypeStruct((8, LANES), F32)),
        in_specs=[HBM_ONLY, HBM_ONLY] + [ANY_SPEC] * n_after,
        out_specs=(SEM_SPEC, SEM_SPEC, HBM_ONLY, HBM_ONLY, pl.BlockSpec(memory_space=pltpu.VMEM)),
        input_output_aliases={0: 2, 1: 3},
        compiler_params=pltpu.CompilerParams(has_side_effects=DATAFLOW),
    )(_in_hbm(vec), _in_hbm(lax.empty(slots.shape, slots.dtype)), *after)
    return res


def _small_wait(send, recv, vec, slots, after, name):
    def body(v_ref, slots_ref, send_ref, recv_ref, *rest):
        x, y, c, _ = _place()
        for k, (px, py, pc) in enumerate(_peers(x, y, c)):
            cp = _remote(v_ref, slots_ref.at[4 * px + 2 * py + pc], send_ref.at[k], recv_ref.at[k], (px, py, pc))
            cp.wait_send()
            cp.wait_recv()

    return pl.pallas_call(
        body, name=name, out_shape=(pltpu.HBM(vec.shape, vec.dtype), pltpu.HBM(slots.shape, slots.dtype)),
        in_specs=[HBM_ONLY, HBM_ONLY, SEM_SPEC, SEM_SPEC] + [ANY_SPEC] * len(after), out_specs=[HBM_ONLY, HBM_ONLY],
        input_output_aliases={0: 0, 1: 1},
        compiler_params=pltpu.CompilerParams(has_side_effects=DATAFLOW),
    )(vec, slots, send, recv, *after)


def _small_sum(vec, slots, name):
    rows = vec.shape[0]
    blk = min(rows, 256)
    x, y, c = lax.axis_index("x"), lax.axis_index("y"), lax.axis_index("c")
    me = (4 * x + 2 * y + c).astype(jnp.int32).reshape(1)

    def slot_spec(k):
        return pl.BlockSpec((None, blk, LANES), lambda i, w: (jnp.where(w[0] == k, (k + 1) % N_DEV, k), i, 0))

    def body(w_ref, v_ref, *rest):
        o_ref = rest[-1]
        acc = None
        for k in range(N_DEV):
            term = jnp.where(w_ref[0] == k, v_ref[...], rest[k][...])
            acc = term if acc is None else acc + term
        o_ref[...] = acc

    return pl.pallas_call(
        body, name=name,
        grid_spec=pltpu.PrefetchScalarGridSpec(
            num_scalar_prefetch=1, grid=(rows // blk,),
            in_specs=[pl.BlockSpec((blk, LANES), lambda i, w: (i, 0))] + [slot_spec(k) for k in range(N_DEV)],
            out_specs=pl.BlockSpec((blk, LANES), lambda i, w: (i, 0))),
        out_shape=jax.ShapeDtypeStruct(vec.shape, F32), compiler_params=_params(("arbitrary",), 32),
    )(me, vec, *([slots] * N_DEV))


def _half(ref, slot, c):
    half = ref.shape[1] // 2
    return ref.at[slot, pl.ds(pl.multiple_of(c * half, 8), half)]


def _gather_start(bufs, after, name):
    n = len(bufs)
    n_after = len(after)

    def body(*refs):
        ins = refs[:n]
        send, recv = refs[n + n_after], refs[n + n_after + 1]
        token = refs[-1]
        x, y, c, chips = _place()
        me = _shard_index(x, y)
        for t in range(n):
            for j, (px, py) in enumerate(chips):
                mine = _half(ins[t], me, c)
                _remote(mine, mine, send.at[t * 3 + j], recv.at[t * 3 + j], (px, py, c)).start()
        token[...] = jnp.zeros_like(token)

    thru = [pltpu.HBM(b.shape, b.dtype) for b in bufs]
    res = pl.pallas_call(
        body, name=name,
        out_shape=(pltpu.SemaphoreType.DMA((n * 3,)), pltpu.SemaphoreType.DMA((n * 3,)), *thru,
                   jax.ShapeDtypeStruct((8, LANES), F32)),
        in_specs=[HBM_ONLY] * n + [ANY_SPEC] * n_after,
        out_specs=(SEM_SPEC, SEM_SPEC, *([HBM_ONLY] * n), pl.BlockSpec(memory_space=pltpu.VMEM)),
        input_output_aliases={i: 2 + i for i in range(n)},
        compiler_params=pltpu.CompilerParams(has_side_effects=DATAFLOW),
    )(*[_in_hbm(b) for b in bufs], *after)
    return res[0], res[1], list(res[2:2 + n]), res[-1]


def _gather_wait(send, recv, bufs, after, name):
    n = len(bufs)

    def body(*refs):
        ins = refs[:n]
        send_ref, recv_ref = refs[n], refs[n + 1]
        x, y, c, chips = _place()
        me = _shard_index(x, y)
        for t in range(n):
            for j, (px, py) in enumerate(chips):
                cp = _remote(_half(ins[t], me, c), _half(ins[t], _shard_index(px, py), c), send_ref.at[t * 3 + j],
                             recv_ref.at[t * 3 + j], (px, py, c))
                cp.wait_send()
                cp.wait_recv()

    res = pl.pallas_call(
        body, name=name, out_shape=tuple(pltpu.HBM(b.shape, b.dtype) for b in bufs),
        in_specs=[HBM_ONLY] * n + [SEM_SPEC, SEM_SPEC] + [ANY_SPEC] * len(after), out_specs=[HBM_ONLY] * n,
        input_output_aliases={i: i for i in range(n)},
        compiler_params=pltpu.CompilerParams(has_side_effects=DATAFLOW),
    )(*bufs, send, recv, *after)
    return list(res)


def _gather_forward(bufs, name):
    n = len(bufs)

    def body(*refs):
        outs = refs[n:2 * n]
        send, recv = refs[2 * n:]
        x, y, c, chips = _place()
        sibling = (x, y, 1 - c)
        cps = []
        for t in range(n):
            for j, (px, py) in enumerate(chips):
                got = _half(outs[t], _shard_index(px, py), c)
                cp = _remote(got, got, send.at[t * 3 + j], recv.at[t * 3 + j], sibling)
                cp.start()
                cps.append(cp)
        for t in range(n):
            for j, (px, py) in enumerate(chips):
                theirs = _half(outs[t], _shard_index(px, py), 1 - c)
                _remote(theirs, theirs, send.at[t * 3 + j], recv.at[t * 3 + j], sibling).wait_recv()
        for cp in cps:
            cp.wait_send()

    return pl.pallas_call(
        body, name=name, in_specs=[HBM_SPEC] * n, out_specs=[HBM_SPEC] * n,
        out_shape=[jax.ShapeDtypeStruct(b.shape, b.dtype) for b in bufs],
        input_output_aliases={t: t for t in range(n)},
        scratch_shapes=[pltpu.SemaphoreType.DMA((n * 3,))] * 2,
    )(*bufs)


def _sibling_rows(ref, c):
    half = ref.shape[1] // 2
    return ref.at[:, pl.ds(pl.multiple_of((1 - c) * half, 8), half)]


def _half_exchange_start(grads, name):
    n = len(grads)

    def body(*refs):
        ins, lands = refs[:n], refs[n:2 * n]
        send, recv = refs[2 * n], refs[2 * n + 1]
        token = refs[-1]
        x, y, c, _ = _place()
        for t in range(n):
            _remote(_sibling_rows(ins[t], c), lands[t], send.at[t], recv.at[t], (x, y, 1 - c)).start()
        token[...] = jnp.zeros_like(token)

    halves = [jax.ShapeDtypeStruct((g.shape[0], g.shape[1] // 2, g.shape[2]), g.dtype) for g in grads]
    res = pl.pallas_call(
        body, name=name,
        out_shape=(pltpu.SemaphoreType.DMA((n,)), pltpu.SemaphoreType.DMA((n,)),
                   *[pltpu.HBM(g.shape, g.dtype) for g in grads], *[pltpu.HBM(h.shape, h.dtype) for h in halves],
                   jax.ShapeDtypeStruct((8, LANES), F32)),
        in_specs=[HBM_ONLY] * (2 * n),
        out_specs=(SEM_SPEC, SEM_SPEC, *([HBM_ONLY] * (2 * n)), pl.BlockSpec(memory_space=pltpu.VMEM)),
        input_output_aliases={i: 2 + i for i in range(2 * n)},
        compiler_params=pltpu.CompilerParams(has_side_effects=DATAFLOW),
    )(*[_in_hbm(g) for g in grads], *[_in_hbm(lax.empty(h.shape, h.dtype)) for h in halves])
    return res[0], res[1], list(res[2:2 + n]), list(res[2 + n:2 + 2 * n]), res[-1]


def _half_exchange_wait(send, recv, grads, lands, after, name):
    n = len(grads)

    def body(*refs):
        ins, got = refs[:n], refs[n:2 * n]
        send_ref, recv_ref = refs[2 * n], refs[2 * n + 1]
        x, y, c, _ = _place()
        for t in range(n):
            cp = _remote(_sibling_rows(ins[t], c), got[t], send_ref.at[t], recv_ref.at[t], (x, y, 1 - c))
            cp.wait_send()
            cp.wait_recv()

    res = pl.pallas_call(
        body, name=name,
        out_shape=(*[pltpu.HBM(g.shape, g.dtype) for g in grads], *[pltpu.HBM(h.shape, h.dtype) for h in lands]),
        in_specs=[HBM_ONLY] * (2 * n) + [SEM_SPEC, SEM_SPEC] + [ANY_SPEC] * len(after),
        out_specs=[HBM_ONLY] * (2 * n),
        input_output_aliases={i: i for i in range(2 * n)},
        compiler_params=pltpu.CompilerParams(has_side_effects=DATAFLOW),
    )(*grads, *lands, send, recv, *after)
    return list(res[:n]), list(res[n:])


def _half_rows(half, c_dim):
    for cand in (512, 256, 128, 64):
        if half % cand == 0 and cand * c_dim * 2 <= 2 * MIB:
            return cand
    raise ValueError((half, c_dim))


def _core_index():
    return lax.axis_index("c").astype(jnp.int32).reshape(1)


def _half_sum(own, other, name):
    s, r, c_dim = own.shape
    rows = _half_rows(r // 2, c_dim)
    per = r // 2 // rows

    def body(c_ref, a_ref, b_ref, o_ref):
        o_ref[...] = (a_ref[...].astype(F32) + b_ref[...].astype(F32)).astype(BF16)

    return pl.pallas_call(
        body, name=name,
        grid_spec=pltpu.PrefetchScalarGridSpec(
            num_scalar_prefetch=1, grid=(s, per),
            in_specs=[pl.BlockSpec((None, rows, c_dim), lambda k, i, c: (k, c[0] * per + i, 0)),
                      pl.BlockSpec((None, rows, c_dim), lambda k, i, c: (k, i, 0))],
            out_specs=pl.BlockSpec((None, rows, c_dim), lambda k, i, c: (k, i, 0))),
        out_shape=pltpu.HBM((s, r // 2, c_dim), BF16), compiler_params=_params(("arbitrary", "arbitrary"), 32),
    )(_core_index(), own, other)


def _reduce_start(pairs, name):
    n = len(pairs)

    def body(*refs):
        ins, lands = refs[:n], refs[n:2 * n]
        send, recv = refs[2 * n], refs[2 * n + 1]
        token = refs[-1]
        x, y, c, chips = _place()
        me = _shard_index(x, y)
        for t in range(n):
            for j, (px, py) in enumerate(chips):
                _remote(ins[t].at[_shard_index(px, py)], lands[t].at[me], send.at[t * 3 + j], recv.at[t * 3 + j],
                        (px, py, c)).start()
        token[...] = jnp.zeros_like(token)

    thru = [pltpu.HBM(b.shape, b.dtype) for b in pairs]
    res = pl.pallas_call(
        body, name=name,
        out_shape=(pltpu.SemaphoreType.DMA((n * 3,)), pltpu.SemaphoreType.DMA((n * 3,)), *thru, *thru,
                   jax.ShapeDtypeStruct((8, LANES), F32)),
        in_specs=[HBM_ONLY] * (2 * n),
        out_specs=(SEM_SPEC, SEM_SPEC, *([HBM_ONLY] * (2 * n)), pl.BlockSpec(memory_space=pltpu.VMEM)),
        input_output_aliases={i: 2 + i for i in range(2 * n)},
        compiler_params=pltpu.CompilerParams(has_side_effects=DATAFLOW),
    )(*[_in_hbm(b) for b in pairs], *[_in_hbm(lax.empty(b.shape, b.dtype)) for b in pairs])
    return res[0], res[1], list(res[2:2 + n]), list(res[2 + n:2 + 2 * n]), res[-1]


def _reduce_wait(send, recv, pairs, lands, after, name):
    n = len(pairs)

    def body(*refs):
        ins, got = refs[:n], refs[n:2 * n]
        send_ref, recv_ref = refs[2 * n], refs[2 * n + 1]
        x, y, c, chips = _place()
        for t in range(n):
            for j, (px, py) in enumerate(chips):
                s = _shard_index(px, py)
                cp = _remote(ins[t].at[s], got[t].at[s], send_ref.at[t * 3 + j], recv_ref.at[t * 3 + j], (px, py, c))
                cp.wait_send()
                cp.wait_recv()

    thru = [pltpu.HBM(b.shape, b.dtype) for b in pairs]
    res = pl.pallas_call(
        body, name=name, out_shape=(*thru, *thru),
        in_specs=[HBM_ONLY] * (2 * n) + [SEM_SPEC, SEM_SPEC] + [ANY_SPEC] * len(after),
        out_specs=[HBM_ONLY] * (2 * n),
        input_output_aliases={i: i for i in range(2 * n)},
        compiler_params=pltpu.CompilerParams(has_side_effects=DATAFLOW),
    )(*pairs, *lands, send, recv, *after)
    return list(res[:n]), list(res[n:])


def _reduce_sum(pair, landed, layer, prev, name):
    s, half, c_dim = pair.shape
    rows = _half_rows(half, c_dim)
    per = half // rows
    shard = _shard_index(lax.axis_index("x"), lax.axis_index("y"))
    where = jnp.stack([shard, lax.axis_index("c")]).astype(jnp.int32)

    def landed_spec(k):
        return pl.BlockSpec((None, rows, c_dim), lambda i, w: (jnp.where(w[0] == k, (k + 1) % s, k), i, 0))

    def body(w_ref, own_ref, *rest):
        o_ref = rest[-1]
        acc = None
        for k in range(s):
            term = jnp.where(w_ref[0] == k, own_ref[...], rest[k][...]).astype(F32)
            acc = term if acc is None else acc + term
        o_ref[...] = acc

    args = [where, pair] + [landed] * s
    in_specs = [pl.BlockSpec((None, rows, c_dim), lambda i, w: (w[0], i, 0))] + [landed_spec(k) for k in range(s)]
    aliases = {}
    if prev is not None:
        args.append(prev)
        in_specs.append(ANY_SPEC)
        aliases = {len(args) - 1: 0}
    return pl.pallas_call(
        body, name=name,
        grid_spec=pltpu.PrefetchScalarGridSpec(
            num_scalar_prefetch=1, grid=(per,), in_specs=in_specs,
            out_specs=pl.BlockSpec((None, rows, c_dim), lambda i, w: (layer, w[1] * per + i, 0))),
        out_shape=jax.ShapeDtypeStruct((DEPTH, 2 * half, c_dim), F32), input_output_aliases=aliases,
        compiler_params=_params(("arbitrary",), 40),
    )(*args)


def _half_gather(bufs, layer, name):
    n = len(bufs)

    def body(*refs):
        outs = refs[n:2 * n]
        send, recv = refs[2 * n:]
        x, y, c, _ = _place()
        sibling = (x, y, 1 - c)

        def rows(t, which):
            half = outs[t].shape[1] // 2
            return outs[t].at[layer, pl.ds(pl.multiple_of(which * half, 8), half)]

        cps = [_remote(rows(t, c), rows(t, c), send.at[t], recv.at[t], sibling) for t in range(n)]
        for cp in cps:
            cp.start()
        for t in range(n):
            _remote(rows(t, 1 - c), rows(t, 1 - c), send.at[t], recv.at[t], sibling).wait_recv()
        for cp in cps:
            cp.wait_send()

    return pl.pallas_call(
        body, name=name, in_specs=[HBM_SPEC] * n, out_specs=[HBM_SPEC] * n,
        out_shape=[jax.ShapeDtypeStruct(b.shape, b.dtype) for b in bufs],
        input_output_aliases={t: t for t in range(n)},
        scratch_shapes=[pltpu.SemaphoreType.DMA((n,))] * 2,
    )(*bufs)


WEIGHT_NAMES = ("ln_attn", "w_in", "sink_b", "rpb_c", "mix_gain", "w_out", "ln_ffn", "w_up", "conv_w", "conv_b",
                "w_down", "ln_final")
BIG_NAMES = ("w_in", "w_out", "w_up", "w_down")
REPLICATED_NAMES = ("ln_attn", "sink_b", "rpb_c", "mix_gain", "ln_ffn", "conv_b", "ln_final")
PACK_TILE = 8 * LANES


def _pack(arrays, row_multiple):
    pieces = []
    for a in arrays:
        flat = a.reshape(-1)
        pieces.append(jnp.pad(flat, (0, (-flat.shape[0]) % PACK_TILE)))
    flat = jnp.concatenate(pieces)
    flat = jnp.pad(flat, (0, (-flat.shape[0]) % (row_multiple * LANES)))
    return flat.reshape(-1, LANES)


def _unpack(packed, shapes):
    flat = packed.reshape(-1)
    out, off = [], 0
    for shape in shapes:
        size = math.prod(shape)
        out.append(flat[off:off + size].reshape(shape))
        off += size + (-size) % PACK_TILE
    return out


def kernel(x, ln_attn, w_in, sink_b, rpb_c, mix_gain, w_out, ln_ffn, w_up, conv_w, conv_b, w_down, ln_final, loss_target, m_ln_attn, m_w_in, m_sink_b, m_rpb_c, m_mix_gain, m_w_out, m_ln_ffn, m_w_up, m_conv_w, m_conv_b, m_w_down, m_ln_final, v_ln_attn, v_w_in, v_sink_b, v_rpb_c, v_mix_gain, v_w_out, v_ln_ffn, v_w_up, v_conv_w, v_conv_b, v_w_down, v_ln_final):
    w = dict(ln_attn=ln_attn, w_in=w_in, sink_b=sink_b, rpb_c=rpb_c, mix_gain=mix_gain, w_out=w_out, ln_ffn=ln_ffn,
             w_up=w_up, conv_w=conv_w, conv_b=conv_b, w_down=w_down, ln_final=ln_final)
    m = dict(ln_attn=m_ln_attn, w_in=m_w_in, sink_b=m_sink_b, rpb_c=m_rpb_c, mix_gain=m_mix_gain, w_out=m_w_out,
             ln_ffn=m_ln_ffn, w_up=m_w_up, conv_w=m_conv_w, conv_b=m_conv_b, w_down=m_w_down, ln_final=m_ln_final)
    v = dict(ln_attn=v_ln_attn, w_in=v_w_in, sink_b=v_sink_b, rpb_c=v_rpb_c, mix_gain=v_mix_gain, w_out=v_w_out,
             ln_ffn=v_ln_ffn, w_up=v_w_up, conv_w=v_conv_w, conv_b=v_conv_b, w_down=v_w_down, ln_final=v_ln_final)
    shard = _shard_index(lax.axis_index("x"), lax.axis_index("y"))
    up_cols = w_up.shape[2]

    conv_slots = _all_gather_small(_pack([conv_w], 8), "gather_conv_w")
    conv_all = conv_slots[0::2].reshape(N_SHARDS, -1)[:, :conv_w.size].reshape((N_SHARDS,) + conv_w.shape)

    arrivals = []
    group_of = {}
    tokens = []
    rest = ("w_out", "w_up", "w_down")
    for l, names in ((0, ("w_in",)), (0, rest), (1, ("w_in",)), (1, rest)):
        bufs = [_own_slot(w[k], l, shard, "own_" + k) for k in names]
        send, recv, bufs, token = _gather_start(bufs, tokens[-1:] or [conv_slots], "gather_start_%d" % len(arrivals))
        tokens.append(token)
        for k in names:
            group_of[l, k] = len(arrivals)
        arrivals.append({"names": names, "send": send, "recv": recv, "bufs": bufs, "done": None})

    def gathered(l, name, after):
        idx = group_of[l, name]
        group = arrivals[idx]
        if group["done"] is None:
            bufs = _gather_wait(group["send"], group["recv"], group["bufs"], list(after) + tokens[-1:],
                                "gather_wait_%d" % idx)
            group["done"] = dict(zip(group["names"], _gather_forward(bufs, "gather_forward_%d" % idx)))
        buf = group["done"][name]
        return buf.reshape(1, -1, buf.shape[2]) if name in ("w_out", "w_down") else buf

    cos, sin = _rope_tables(SEQ)
    tabs = {"cos": cos, "sin": sin, "bias_a": _bias_a(), "bias_b": _bias_b()}
    layers = []
    for l in range(DEPTH):
        conv_w_l = conv_all[:, l].reshape(2, N_SHARDS // 2, 3, up_cols).transpose(0, 2, 1, 3).reshape(2, 3, D_FF)
        layers.append({"ln_attn": ln_attn[l][None], "sink_b": sink_b[l], "bias_c": _bias_c(rpb_c[l]),
                       "mix_gain": mix_gain[l][None], "ln_ffn": ln_ffn[l][None], "conv_w": conv_w_l,
                       "conv_b": conv_b[l].reshape(2, 1, D_FF)})

    act = x[0]
    saved = []
    for l in range(DEPTH):
        act, keep = _layer_fwd(act, layers[l], lambda name, after, l=l: gathered(l, name, [after]), tabs)
        saved.append(keep)
    loss_part, dx, dx_b, d_ln_final = _loss_head(act, ln_final[None], loss_target[0], "loss_head")
    loss = lax.psum(loss_part[0, 0], ("x", "y", "c"))

    reductions = []

    opened = [0]

    def begin(l, partial):
        idx = opened[0]
        opened[0] += 1
        names = tuple(partial)
        send_sem, recv_sem, mine, theirs, token = _half_exchange_start([partial[k] for k in names],
                                                                       "half_exchange_start_%d" % idx)
        return {"idx": idx, "layer": l, "names": names, "send": send_sem, "recv": recv_sem, "mine": mine,
                "theirs": theirs}, token

    def finish(handle, after):
        idx, names = handle["idx"], handle["names"]
        mine, theirs = _half_exchange_wait(handle["send"], handle["recv"], handle["mine"], handle["theirs"], after,
                                           "half_exchange_wait_%d" % idx)
        pairs = [_half_sum(a, b, "half_sum_" + k) for k, a, b in zip(names, mine, theirs)]
        send_sem, recv_sem, pairs, lands, token = _reduce_start(pairs, "reduce_start_%d" % idx)
        reductions.append({"layer": handle["layer"], "names": names, "send": send_sem, "recv": recv_sem,
                           "pairs": pairs, "lands": lands})
        return token

    small = [None] * DEPTH
    pending = None
    for l in reversed(range(DEPTH)):
        big = {k: gathered(l, k, []) for k in BIG_NAMES}
        dx, dx_b, small[l], pending = _layer_bwd(dx, dx_b, saved[l], layers[l], big, tabs,
                                                 functools.partial(begin, l), finish, pending)
    after = [finish(pending[0], [pending[1]])]

    stacked = {k: jnp.stack([small[l][k] for l in range(DEPTH)]) for k in small[0]}
    part = {"ln_attn": stacked["ln_attn"][:, 0], "sink_b": stacked["sink_b"], "rpb_c": stacked["rpb_c"],
            "mix_gain": stacked["mix_gain"][:, 0], "ln_ffn": stacked["ln_ffn"][:, 0],
            "conv_b": stacked["conv_b"].reshape(DEPTH, 2 * D_FF), "ln_final": d_ln_final[0],
            "conv_w": stacked["conv_w"].transpose(0, 2, 1, 3).reshape(DEPTH, 3, 2 * D_FF)}
    small_names = REPLICATED_NAMES + ("conv_w",)
    small_send, small_recv, small_vec, small_slots, token = _small_start(
        _pack([part[k] for k in small_names], 256), after, "small_grads_start")
    after = [token]

    grads, delta, new_m, new_v = {}, {}, {}, {}
    reduced = {}
    updated = dict.fromkeys(BIG_NAMES)
    for l in reversed(range(DEPTH)):
        for idx, group in enumerate(reductions):
            if group["layer"] != l:
                continue
            pairs, lands = _reduce_wait(group["send"], group["recv"], group["pairs"], group["lands"], after,
                                        "reduce_wait_%d" % idx)
            for k, pair, landed in zip(group["names"], pairs, lands):
                reduced[k] = _reduce_sum(pair, landed, l, reduced.get(k), "reduce_sum_" + k)
            after = [reduced[group["names"][-1]]]
        reduced = dict(zip(BIG_NAMES, _half_gather([reduced[k] for k in BIG_NAMES], l, "half_gather_%d" % l)))
        for k in BIG_NAMES:
            updated[k] = _adamw_layer(w[k], reduced[k], m[k], v[k], l, updated[k], "adamw_" + k)
        after = [updated[k][0] for k in BIG_NAMES]
    for k in BIG_NAMES:
        grads[k], delta[k], new_m[k], new_v[k] = updated[k]

    small_vec, small_slots = _small_wait(small_send, small_recv, small_vec, small_slots, after, "small_grads_wait")
    total = _small_sum(small_vec, small_slots, "small_grads_sum")
    for k, g in zip(small_names, _unpack(total, [part[k].shape for k in small_names])):
        grads[k] = g
    grads["conv_w"] = lax.dynamic_slice_in_dim(grads["conv_w"], shard * up_cols, up_cols, axis=2)

    flat = (DEPTH * 3, up_cols)
    res = _adamw(conv_w.reshape(flat), grads["conv_w"].reshape(flat), m["conv_w"].reshape(flat),
                 v["conv_w"].reshape(flat), "adamw_conv_w")
    delta["conv_w"], new_m["conv_w"], new_v["conv_w"] = (r.reshape(conv_w.shape) for r in res)
    shapes = [w[k].shape for k in REPLICATED_NAMES]
    packed = [_pack([d[k] for k in REPLICATED_NAMES], 128) for d in (w, grads, m, v)]
    for d, res in zip((delta, new_m, new_v), _adamw(*packed, "adamw_small")):
        for k, r in zip(REPLICATED_NAMES, _unpack(res, shapes)):
            d[k] = r

    return (loss, dx[None], *[grads[k] for k in WEIGHT_NAMES], *[delta[k] for k in WEIGHT_NAMES],
            *[new_m[k] for k in WEIGHT_NAMES], *[new_v[k] for k in WEIGHT_NAMES])
```

```python
import functools
import math

import jax
import jax.numpy as jnp
from jax import lax
from jax.experimental import pallas as pl
from jax.experimental.pallas import tpu as pltpu

F32 = jnp.float32
BF16 = jnp.bfloat16
MESH = pl.DeviceIdType.MESH

D_MODEL = 2048
SEQ = 2048
DEPTH = 2
HEAD_DIM = 64
N_HEADS_A = 12
N_HEADS_B = 10
N_KV_B = 2
N_HEADS_C = 10
WINDOW_B = 128
GRID_W = 64
NA_ROWS = 8
NA_COLS = 16
WIDTH_A = N_HEADS_A * HEAD_DIM
WIDTH_B = N_HEADS_B * HEAD_DIM
WIDTH_C = N_HEADS_C * HEAD_DIM
IN_COLS = 5120
D_FF = 5632
ROPE_THETA = 10000.0
EPS = 1e-6
NEG_INF = -1e30
N_SHARDS = 4

ADAM_LR = 0.001
ADAM_B1 = 0.9
ADAM_B2 = 0.999
ADAM_EPS = 1e-08
ADAM_WD = 0.01
ADAM_STEP = 10

LANES = 128
QB = 256
NQB = SEQ // QB
ROWS = 256
MIB = 2 ** 20

A_BLK = (0, 6, 12)
B_BLK = (18, 23, 24)
C_BLK = (25, 30, 35)
ROPE_BLKS = tuple(range(0, 12)) + tuple(range(18, 24))
QSCALE_BLKS = tuple(range(0, 6)) + tuple(range(18, 23)) + tuple(range(25, 30))
N_PBLK = IN_COLS // LANES


def _params(sem, vmem_mib):
    return pltpu.CompilerParams(dimension_semantics=sem, vmem_limit_bytes=vmem_mib * MIB)


def _weight_spec(w, cols, t_in, t_out, transposed):
    s, r, c = w.shape
    if cols:
        per = c // t_out
        k_dim, n = r, s * c
        if transposed:
            index = lambda j, rr: (rr // per, j, rr % per)
        else:
            index = lambda j, kk: (j // per, kk, j % per)
    else:
        per = r // t_in
        k_dim, n = s * r, c
        if transposed:
            index = lambda j, rr: (j // per, j % per, rr)
        else:
            index = lambda j, kk: (kk // per, kk % per, j)
    return pl.BlockSpec((None, t_in, t_out), index), k_dim, n


def _mm_nn(a, w, *, cols, tn, tk, out_dtype, name, residual=None, out_split=1):
    m, k_dim = a.shape
    w_spec, k_w, n = _weight_spec(w, cols, tk, tn, False)
    assert k_w == k_dim
    nj, nk = n // tn, k_dim // tk
    in_specs = [pl.BlockSpec((m, tk), lambda j, k: (0, k)), w_spec]
    args = [a, w]
    if residual is not None:
        in_specs.append(pl.BlockSpec((m, tn), lambda j, k: (0, j)))
        args.append(residual)
    if out_split > 1:
        per_o = n // out_split // tn
        out_spec = pl.BlockSpec((None, m, tn), lambda j, k: (j // per_o, 0, j % per_o))
        out_shape = pltpu.HBM((out_split, m, n // out_split), out_dtype)
    else:
        out_spec = pl.BlockSpec((m, tn), lambda j, k: (0, j))
        out_shape = pltpu.HBM((m, n), out_dtype)

    def body(*refs):
        a_ref, w_ref = refs[0], refs[1]
        r_ref = refs[2] if residual is not None else None
        o_ref = refs[3] if residual is not None else refs[2]

        def finish(val):
            if r_ref is not None:
                val = r_ref[...] + val
            o_ref[...] = val.astype(o_ref.dtype)

        part = jnp.dot(a_ref[...], w_ref[...], preferred_element_type=F32)
        if nk == 1:
            finish(part)
        else:
            acc = refs[-1]
            kk = pl.program_id(1)

            @pl.when(kk == 0)
            def _():
                acc[...] = part

            @pl.when(kk > 0)
            def _():
                acc[...] += part

            @pl.when(kk == nk - 1)
            def _():
                finish(acc[...])

    return pl.pallas_call(
        body, name=name, grid=(nj, nk), in_specs=in_specs, out_specs=out_spec, out_shape=out_shape,
        scratch_shapes=[pltpu.VMEM((m, tn), F32)] if nk > 1 else [],
        compiler_params=_params(("arbitrary", "arbitrary"), 56),
    )(*[_in_hbm(a) for a in args])


ANY_SPEC = pl.BlockSpec(memory_space=pl.ANY)


def _mm_nt(dy, w, *, cols, to, tr, out_dtype, name, after=()):
    if dy.ndim == 3:
        m = dy.shape[1]
        n = dy.shape[0] * dy.shape[2]
        per_d = dy.shape[2] // tr
        dy_spec = pl.BlockSpec((None, m, tr), lambda j, r: (r // per_d, 0, r % per_d))
    else:
        m, n = dy.shape
        dy_spec = pl.BlockSpec((m, tr), lambda j, r: (0, r))
    w_spec, k_dim, n_w = _weight_spec(w, cols, to, tr, True)
    assert n_w == n
    nj, nr = k_dim // to, n // tr

    n_after = len(after)

    def body(dy_ref, w_ref, *rest):
        o_ref = rest[n_after]
        part = lax.dot_general(dy_ref[...], w_ref[...], (((1,), (1,)), ((), ())), preferred_element_type=F32)
        if nr == 1:
            o_ref[...] = part.astype(o_ref.dtype)
        else:
            acc = rest[n_after + 1]
            rr = pl.program_id(1)

            @pl.when(rr == 0)
            def _():
                acc[...] = part

            @pl.when(rr > 0)
            def _():
                acc[...] += part

            @pl.when(rr == nr - 1)
            def _():
                o_ref[...] = acc[...].astype(o_ref.dtype)

    return pl.pallas_call(
        body, name=name, grid=(nj, nr), in_specs=[dy_spec, w_spec] + [ANY_SPEC] * n_after,
        out_specs=pl.BlockSpec((m, to), lambda j, r: (0, j)),
        out_shape=pltpu.HBM((m, k_dim), out_dtype),
        scratch_shapes=[pltpu.VMEM((m, to), F32)] if nr > 1 else [],
        compiler_params=_params(("arbitrary", "arbitrary"), 56),
    )(_in_hbm(dy), _in_hbm(w), *after)


def _mm_tn(x, dy, *, tk, tn, shards, name):
    m, k_dim = x.shape
    if dy.ndim == 3:
        n = dy.shape[0] * dy.shape[2]
        per_d = dy.shape[2] // tn
        dy_spec = pl.BlockSpec((None, m, tn), lambda i, j: (j // per_d, 0, j % per_d))
    else:
        n = dy.shape[1]
        dy_spec = pl.BlockSpec((m, tn), lambda i, j: (0, j))
    if shards > 0:
        per = n // shards // tn
        out_shape = pltpu.HBM((shards, k_dim, n // shards), BF16)
        out_spec = pl.BlockSpec((None, tk, tn), lambda i, j: (j // per, i, j % per))
    else:
        s = -shards
        per = k_dim // s // tk
        out_shape = pltpu.HBM((s, k_dim // s, n), BF16)
        out_spec = pl.BlockSpec((None, tk, tn), lambda i, j: (i // per, i % per, j))

    def body(x_ref, dy_ref, o_ref):
        o_ref[...] = lax.dot_general(x_ref[...], dy_ref[...], (((0,), (0,)), ((), ())),
                                     preferred_element_type=F32).astype(BF16)

    return pl.pallas_call(
        body, name=name, grid=(k_dim // tk, n // tn),
        in_specs=[pl.BlockSpec((m, tk), lambda i, j: (0, i)), dy_spec], out_specs=out_spec, out_shape=out_shape,
        compiler_params=_params(("arbitrary", "arbitrary"), 56),
    )(_in_hbm(x), _in_hbm(dy))


def _row_spec(width, rows=ROWS):
    return pl.BlockSpec((rows, width), lambda i: (i, 0))


def _vec_spec(width):
    return pl.BlockSpec((1, width), lambda i: (0, 0))


def _rms_stats(x):
    r = lax.rsqrt(jnp.mean(x * x, axis=-1, keepdims=True) + EPS)
    return r, x * r


def _rmsnorm_fwd(x, gain, name):
    t, d = x.shape

    def body(x_ref, g_ref, o_ref):
        _, n = _rms_stats(x_ref[...])
        o_ref[...] = (n * g_ref[...]).astype(BF16)

    return pl.pallas_call(
        body, name=name, grid=(t // ROWS,), in_specs=[_row_spec(d), _vec_spec(d)], out_specs=_row_spec(d),
        out_shape=pltpu.HBM((t, d), BF16), compiler_params=_params(("arbitrary",), 32),
    )(_in_hbm(x), _in_hbm(gain))


def _rmsnorm_bwd(x, gain, dh, dres, name, after=()):
    t, d = x.shape
    n_after = len(after)

    def body(x_ref, g_ref, dh_ref, dres_ref, *rest):
        dx_ref, dxb_ref, dg_ref = rest[n_after:]
        r, n = _rms_stats(x_ref[...])
        dh_v = dh_ref[...]
        dn = dh_v * g_ref[...]
        dx = dres_ref[...] + r * (dn - n * jnp.mean(dn * n, axis=-1, keepdims=True))
        dx_ref[...] = dx
        dxb_ref[...] = dx.astype(BF16)
        part = jnp.sum(dh_v * n, axis=0, keepdims=True)

        @pl.when(pl.program_id(0) == 0)
        def _():
            dg_ref[...] = part

        @pl.when(pl.program_id(0) > 0)
        def _():
            dg_ref[...] += part

    return pl.pallas_call(
        body, name=name, grid=(t // ROWS,),
        in_specs=[_row_spec(d), _vec_spec(d), _row_spec(d), _row_spec(d)] + [ANY_SPEC] * n_after,
        out_specs=[_row_spec(d), _row_spec(d), _vec_spec(d)],
        out_shape=[pltpu.HBM((t, d), F32), pltpu.HBM((t, d), BF16), jax.ShapeDtypeStruct((1, d), F32)],
        compiler_params=_params(("arbitrary",), 40),
    )(_in_hbm(x), _in_hbm(gain), _in_hbm(dh), _in_hbm(dres), *after)


def _loss_head(x, gain, target, name):
    t, d = x.shape

    def body(x_ref, g_ref, t_ref, loss_ref, dx_ref, dxb_ref, dg_ref):
        r, n = _rms_stats(x_ref[...])
        g = g_ref[...]
        err = n * g - t_ref[...]
        dy = err * (1.0 / d)
        dn = dy * g
        dx = r * (dn - n * jnp.mean(dn * n, axis=-1, keepdims=True))
        dx_ref[...] = dx
        dxb_ref[...] = dx.astype(BF16)
        part = jnp.sum(dy * n, axis=0, keepdims=True)
        lpart = jnp.zeros((8, LANES), F32) + 0.5 * jnp.sum(jnp.mean(err * err, axis=-1, keepdims=True))

        @pl.when(pl.program_id(0) == 0)
        def _():
            dg_ref[...] = part
            loss_ref[...] = lpart

        @pl.when(pl.program_id(0) > 0)
        def _():
            dg_ref[...] += part
            loss_ref[...] += lpart

    return pl.pallas_call(
        body, name=name, grid=(t // ROWS,),
        in_specs=[_row_spec(d), _vec_spec(d), _row_spec(d)],
        out_specs=[pl.BlockSpec((8, LANES), lambda i: (0, 0)), _row_spec(d), _row_spec(d), _vec_spec(d)],
        out_shape=[jax.ShapeDtypeStruct((8, LANES), F32), pltpu.HBM((t, d), F32), pltpu.HBM((t, d), BF16),
                   jax.ShapeDtypeStruct((1, d), F32)],
        compiler_params=_params(("arbitrary",), 40),
    )(x, gain, target)


def _swap_halves(x):
    lane = lax.broadcasted_iota(jnp.int32, x.shape, 1)
    return jnp.where((lane % HEAD_DIM) < HEAD_DIM // 2, pltpu.roll(x, LANES - HEAD_DIM // 2, 1),
                     pltpu.roll(x, HEAD_DIM // 2, 1))


def _rope_tables(t):
    inv_freq = ROPE_THETA ** (-jnp.arange(0, HEAD_DIM, 2, dtype=F32) / HEAD_DIM)
    ang = jnp.arange(t, dtype=F32)[:, None] * inv_freq[None, :]
    cos = jnp.tile(jnp.cos(ang), (1, LANES // (HEAD_DIM // 2)))
    sin = jnp.tile(jnp.sin(ang), (1, LANES // (HEAD_DIM // 2)))
    lane = jnp.arange(LANES)[None, :]
    return cos, jnp.where((lane % HEAD_DIM) < HEAD_DIM // 2, -sin, sin)


def _rope_fwd(proj, cos, sin, name):
    t = proj.shape[0]
    scale = HEAD_DIM ** -0.5

    def body(p_ref, c_ref, s_ref, o_ref):
        cos_v, sin_v = c_ref[...], s_ref[...]
        for b in range(N_PBLK):
            cols = slice(b * LANES, (b + 1) * LANES)
            v = p_ref[:, cols]
            if b in ROPE_BLKS:
                v = v * cos_v + _swap_halves(v) * sin_v
            if b in QSCALE_BLKS:
                v = v * scale
            o_ref[:, cols] = v.astype(BF16)

    return pl.pallas_call(
        body, name=name, grid=(t // ROWS,),
        in_specs=[_row_spec(IN_COLS), _row_spec(LANES), _row_spec(LANES)], out_specs=_row_spec(IN_COLS),
        out_shape=pltpu.HBM((t, IN_COLS), BF16), compiler_params=_params(("arbitrary",), 40),
    )(_in_hbm(proj), _in_hbm(cos), _in_hbm(sin))


def _rope_bwd(grads, cos, sin, name):
    t = grads[0].shape[0]
    scale = HEAD_DIM ** -0.5
    group = N_HEADS_B // N_KV_B

    def body(*refs):
        c_ref, s_ref, o_ref = refs[9], refs[10], refs[11]
        cos_v, sin_v = c_ref[...], s_ref[...]

        def kv_sum(ref):
            parts = []
            for g in range(N_KV_B):
                acc = ref[:, g * group * HEAD_DIM:(g * group + 1) * HEAD_DIM]
                for h in range(g * group + 1, (g + 1) * group):
                    acc = acc + ref[:, h * HEAD_DIM:(h + 1) * HEAD_DIM]
                parts.append(acc)
            return jnp.concatenate(parts, axis=1)

        def emit(b, v):
            if b in ROPE_BLKS:
                v = v * cos_v - _swap_halves(v) * sin_v
            if b in QSCALE_BLKS:
                v = v * scale
            o_ref[:, b * LANES:(b + 1) * LANES] = v.astype(BF16)

        starts = (A_BLK[0], A_BLK[1], A_BLK[2], B_BLK[0], None, None, C_BLK[0], C_BLK[1], C_BLK[2])
        for idx, start in enumerate(starts):
            if start is None:
                continue
            for j in range(refs[idx].shape[1] // LANES):
                emit(start + j, refs[idx][:, j * LANES:(j + 1) * LANES])
        emit(B_BLK[1], kv_sum(refs[4]))
        emit(B_BLK[2], kv_sum(refs[5]))

    return pl.pallas_call(
        body, name=name, grid=(t // ROWS,),
        in_specs=[_row_spec(g.shape[1]) for g in grads] + [_row_spec(LANES), _row_spec(LANES)],
        out_specs=_row_spec(IN_COLS),
        out_shape=pltpu.HBM((t, IN_COLS), BF16), compiler_params=_params(("arbitrary",), 40),
    )(*[_in_hbm(g) for g in grads], _in_hbm(cos), _in_hbm(sin))


GROUP_COLS = ((0, WIDTH_A), (WIDTH_A, WIDTH_A + WIDTH_B), (WIDTH_A + WIDTH_B, D_MODEL))


def _mix_fwd(oa, ob, oc, gain, name):
    t = oa.shape[0]

    def body(a_ref, b_ref, c_ref, g_ref, o_ref):
        for ref, (lo, hi) in zip((a_ref, b_ref, c_ref), GROUP_COLS):
            _, n = _rms_stats(ref[...])
            o_ref[:, lo:hi] = (n * g_ref[:, lo:hi]).astype(BF16)

    return pl.pallas_call(
        body, name=name, grid=(t // ROWS,),
        in_specs=[_row_spec(WIDTH_A), _row_spec(WIDTH_B), _row_spec(WIDTH_C), _vec_spec(D_MODEL)],
        out_specs=_row_spec(D_MODEL),
        out_shape=pltpu.HBM((t, D_MODEL), BF16), compiler_params=_params(("arbitrary",), 32),
    )(_in_hbm(oa), _in_hbm(ob), _in_hbm(oc), _in_hbm(gain))


def _mix_bwd(oa, ob, oc, gain, dmixed, name, after=()):
    t = oa.shape[0]
    n_after = len(after)

    def body(a_ref, b_ref, c_ref, g_ref, dm_ref, *rest):
        da_ref, db_ref, dc_ref, dg_ref = rest[n_after:]
        first = pl.program_id(0) == 0
        for ref, dref, (lo, hi) in zip((a_ref, b_ref, c_ref), (da_ref, db_ref, dc_ref), GROUP_COLS):
            r, n = _rms_stats(ref[...])
            dm = dm_ref[:, lo:hi]
            dn = dm * g_ref[:, lo:hi]
            dref[...] = r * (dn - n * jnp.mean(dn * n, axis=-1, keepdims=True))
            part = jnp.sum(dm * n, axis=0, keepdims=True)

            @pl.when(first)
            def _():
                dg_ref[:, lo:hi] = part

            @pl.when(jnp.logical_not(first))
            def _():
                dg_ref[:, lo:hi] += part

    return pl.pallas_call(
        body, name=name, grid=(t // ROWS,),
        in_specs=[_row_spec(WIDTH_A), _row_spec(WIDTH_B), _row_spec(WIDTH_C), _vec_spec(D_MODEL), _row_spec(D_MODEL)]
        + [ANY_SPEC] * n_after,
        out_specs=[_row_spec(WIDTH_A), _row_spec(WIDTH_B), _row_spec(WIDTH_C), _vec_spec(D_MODEL)],
        out_shape=[pltpu.HBM((t, WIDTH_A), F32), pltpu.HBM((t, WIDTH_B), F32), pltpu.HBM((t, WIDTH_C), F32),
                   jax.ShapeDtypeStruct((1, D_MODEL), F32)],
        compiler_params=_params(("arbitrary",), 40),
    )(_in_hbm(oa), _in_hbm(ob), _in_hbm(oc), _in_hbm(gain), _in_hbm(dmixed), *after)


FF_COLS = 256


SUBLANES = 8
CHUNK = 128
HALO = SUBLANES


def _pad_rows(dst_ref, src_ref):
    t, cols = src_ref.shape
    dst_ref[0:HALO, :] = jnp.zeros((HALO, cols), F32)
    dst_ref[HALO:HALO + t, :] = src_ref[...]
    dst_ref[HALO + t:t + 2 * HALO, :] = jnp.zeros((HALO, cols), F32)


def _roll_rows(x, by):
    return pltpu.roll(x, by % x.shape[0], 0)


def _gate_val(pad_ref, r0, w_ref, b_ref):
    ext = [pad_ref[h, pl.ds(r0, CHUNK + 2 * HALO), :] for h in range(2)]
    before = [_roll_rows(e, 1) for e in ext]
    after = [_roll_rows(e, -1) for e in ext]
    gate, val = ((before[h] * w_ref[h, 0:1, :] + ext[h] * w_ref[h, 1:2, :]) + after[h] * w_ref[h, 2:3, :] + b_ref[h]
                 for h in range(2))
    return gate, val, ext, before, after


def _ff_specs(t):
    u_spec = pl.BlockSpec((2, t, FF_COLS), lambda j: (0, 0, j))
    w_spec = pl.BlockSpec((2, 3, FF_COLS), lambda j: (0, 0, j))
    b_spec = pl.BlockSpec((2, 1, FF_COLS), lambda j: (0, 0, j))
    return u_spec, w_spec, b_spec


def _convgate_fwd(u0, conv_w, conv_b, name):
    t = u0.shape[1]
    u_spec, w_spec, b_spec = _ff_specs(t)

    def body(u_ref, w_ref, b_ref, o_ref, pad_ref):
        for h in range(2):
            _pad_rows(pad_ref.at[h], u_ref.at[h])

        def chunk(ci, carry):
            r0 = pl.multiple_of(ci * CHUNK, CHUNK)
            gate, val, _, _, _ = _gate_val(pad_ref, r0, w_ref, b_ref)
            act = gate * jax.nn.sigmoid(gate) * val
            o_ref[pl.ds(r0, CHUNK), :] = act[HALO:HALO + CHUNK].astype(BF16)
            return carry

        lax.fori_loop(0, t // CHUNK, chunk, 0)

    return pl.pallas_call(
        body, name=name, grid=(D_FF // FF_COLS,), in_specs=[u_spec, w_spec, b_spec],
        out_specs=pl.BlockSpec((t, FF_COLS), lambda j: (0, j)),
        out_shape=pltpu.HBM((t, D_FF), BF16),
        scratch_shapes=[pltpu.VMEM((2, t + 2 * HALO, FF_COLS), F32)],
        compiler_params=_params(("arbitrary",), 48),
    )(_in_hbm(u0), conv_w, conv_b)


def _convgate_bwd(u0, conv_w, conv_b, d_act, name):
    t = u0.shape[1]
    u_spec, w_spec, b_spec = _ff_specs(t)

    def body(u_ref, w_ref, b_ref, da_ref, du_ref, dw_ref, db_ref, pad_ref, da_pad_ref, sums_ref):
        for h in range(2):
            _pad_rows(pad_ref.at[h], u_ref.at[h])
        _pad_rows(da_pad_ref, da_ref)
        sums_ref[...] = jnp.zeros_like(sums_ref)
        inner = slice(HALO, HALO + CHUNK)

        def fold(x):
            return jnp.sum(x.reshape(CHUNK // SUBLANES, SUBLANES, x.shape[1]), axis=0)

        def chunk(ci, carry):
            r0 = pl.multiple_of(ci * CHUNK, CHUNK)
            gate, val, ext, before, after = _gate_val(pad_ref, r0, w_ref, b_ref)
            sig = jax.nn.sigmoid(gate)
            da = da_pad_ref[pl.ds(r0, CHUNK + 2 * HALO), :]
            d_half = (da * val * (sig * (1.0 + gate * (1.0 - sig))), da * (gate * sig))
            for h in range(2):
                du = d_half[h]
                for k, term in enumerate((du, du * before[h], du * ext[h], du * after[h])):
                    sums_ref[h, k] += fold(term[inner])
                du0 = (_roll_rows(du, -1) * w_ref[h, 0:1, :] + du * w_ref[h, 1:2, :]) + _roll_rows(du, 1) * w_ref[h, 2:3, :]
                du_ref[h, pl.ds(r0, CHUNK), :] = du0[inner].astype(BF16)
            return carry

        lax.fori_loop(0, t // CHUNK, chunk, 0)
        for h in range(2):
            db_ref[h] = jnp.sum(sums_ref[h, 0], axis=0, keepdims=True)
            for k in range(3):
                dw_ref[h, k:k + 1, :] = jnp.sum(sums_ref[h, k + 1], axis=0, keepdims=True)

    return pl.pallas_call(
        body, name=name, grid=(D_FF // FF_COLS,),
        in_specs=[u_spec, w_spec, b_spec, pl.BlockSpec((t, FF_COLS), lambda j: (0, j))],
        out_specs=[u_spec, w_spec, b_spec],
        out_shape=[pltpu.HBM((2, t, D_FF), BF16), jax.ShapeDtypeStruct((2, 3, D_FF), F32),
                   jax.ShapeDtypeStruct((2, 1, D_FF), F32)],
        scratch_shapes=[pltpu.VMEM((2, t + 2 * HALO, FF_COLS), F32), pltpu.VMEM((t + 2 * HALO, FF_COLS), F32),
                        pltpu.VMEM((2, 4, SUBLANES, FF_COLS), F32)],
        compiler_params=_params(("arbitrary",), 56),
    )(_in_hbm(u0), conv_w, conv_b, _in_hbm(d_act))


class _Group:
    def __init__(self, heads, blks, kv_rows, n_win, gqa, bias_per_head):
        self.heads = heads
        self.pairs = heads // 2
        self.q_blk, self.k_blk, self.v_blk = blks
        self.kv_rows = kv_rows
        self.n_win = n_win
        self.full = kv_rows == SEQ
        self.gqa = gqa
        self.bias_per_head = bias_per_head
        self.width = heads * HEAD_DIM
        self.keys = kv_rows * n_win


GROUP_A = _Group(N_HEADS_A, A_BLK, SEQ, 1, False, False)
GROUP_B = _Group(N_HEADS_B, B_BLK, WINDOW_B, 4, True, False)
GROUP_C = _Group(N_HEADS_C, C_BLK, QB, 3, False, True)


def _win_start(grp, i):
    return jnp.clip(i * (QB // grp.kv_rows) - 1, 0, SEQ // grp.kv_rows - grp.n_win)


def _win_variant(i):
    return jnp.minimum(i, 1) + (i == NQB - 1).astype(jnp.int32)


def _attn_in_specs(grp, t):
    q_spec = pl.BlockSpec((QB, LANES), lambda p, i: (i, grp.q_blk + p))

    def col(blk):
        return (lambda p: blk) if grp.gqa else (lambda p: blk + p)

    def kv_specs(blk):
        c = col(blk)
        if grp.full:
            return [pl.BlockSpec((t, LANES), lambda p, i: (0, c(p)))]
        return [pl.BlockSpec((grp.kv_rows, LANES),
                             functools.partial(lambda p, i, w: (_win_start(grp, i) + w, c(p)), w=w))
                for w in range(grp.n_win)]

    nwk = grp.keys
    if grp.bias_per_head:
        bias_spec = pl.BlockSpec((2, None, QB, nwk), lambda p, i: (p, _win_variant(i), 0, 0))
    elif grp.full:
        bias_spec = pl.BlockSpec((1, None, QB, nwk), lambda p, i: (0, i, 0, 0))
    else:
        bias_spec = pl.BlockSpec((1, None, QB, nwk), lambda p, i: (0, _win_variant(i), 0, 0))
    sink_spec = pl.BlockSpec((1, LANES), lambda p, i: (0, p))
    return q_spec, kv_specs(grp.k_blk), kv_specs(grp.v_blk), bias_spec, sink_spec


def _head_kv(grp, whole, e, p):
    lo, hi = whole[:, :HEAD_DIM], whole[:, HEAD_DIM:]
    if grp.gqa:
        return jnp.where(2 * p + e >= N_HEADS_B // N_KV_B, hi, lo)
    return hi if e else lo


def _softmax_parts(q, k, bias, sink):
    s = lax.dot_general(q, k, (((1,), (1,)), ((), ())), preferred_element_type=F32) + bias
    m = jnp.maximum(jnp.max(s, axis=-1, keepdims=True), sink)
    pe = jnp.exp(s - m)
    denom = jnp.sum(pe, axis=-1, keepdims=True) + jnp.exp(sink - m)
    return pe, m, 1.0 / denom


def _attn_fwd(grp, proj, bias, sink, name):
    t = proj.shape[0]
    q_spec, k_specs, v_specs, bias_spec, sink_spec = _attn_in_specs(grp, t)
    nkv = len(k_specs)

    def body(*refs):
        q_ref = refs[0]
        k_refs, v_refs = refs[1:1 + nkv], refs[1 + nkv:1 + 2 * nkv]
        bias_ref, sink_ref, o_ref = refs[1 + 2 * nkv:4 + 2 * nkv]
        p = pl.program_id(0)
        k_all = jnp.concatenate([r[...] for r in k_refs], axis=0)
        v_all = jnp.concatenate([r[...] for r in v_refs], axis=0)
        outs = []
        for e in range(2):
            q = q_ref[:, e * HEAD_DIM:(e + 1) * HEAD_DIM]
            k = _head_kv(grp, k_all, e, p)
            v = _head_kv(grp, v_all, e, p)
            snk = sink_ref[0:1, e * HEAD_DIM:e * HEAD_DIM + 1]
            pe, _, inv = _softmax_parts(q, k, bias_ref[e if grp.bias_per_head else 0], snk)
            outs.append(jnp.dot(pe.astype(BF16), v, preferred_element_type=F32) * inv)
        o_ref[...] = jnp.concatenate(outs, axis=1)

    return pl.pallas_call(
        body, name=name, grid=(grp.pairs, NQB),
        in_specs=[q_spec, *k_specs, *v_specs, bias_spec, sink_spec],
        out_specs=pl.BlockSpec((QB, LANES), lambda p, i: (i, p)),
        out_shape=pltpu.HBM((t, grp.width), F32),
        compiler_params=_params(("arbitrary", "arbitrary"), 48),
    )(*([_in_hbm(proj)] * (1 + 2 * nkv)), _in_hbm(bias), sink)


def _attn_bwd(grp, proj, bias, sink, out, d_out, name):
    t = proj.shape[0]
    q_spec, k_specs, v_specs, bias_spec, sink_spec = _attn_in_specs(grp, t)
    nkv = len(k_specs)
    n_off = 2 * NA_ROWS - 1
    rows_q = QB // GRID_W
    wide = grp.keys > 2 * QB
    o_spec = pl.BlockSpec((QB, LANES), lambda p, i: (i, p))
    acc_spec = pl.BlockSpec((t, LANES), lambda p, i: (0, p))
    out_specs = [o_spec, acc_spec, acc_spec, pl.BlockSpec((None, 8, LANES), lambda p, i: (p, 0, 0))]
    out_shape = [pltpu.HBM((t, grp.width), F32)] * 3 + [jax.ShapeDtypeStruct((grp.pairs, 8, LANES), F32)]
    if grp.bias_per_head:
        out_specs.append(pl.BlockSpec((2, n_off, GRID_W, GRID_W), lambda p, i: (p, 0, 0, 0)))
        out_shape.append(jax.ShapeDtypeStruct((grp.heads, n_off, GRID_W, GRID_W), F32))

    def body(*refs):
        q_ref = refs[0]
        k_refs, v_refs = refs[1:1 + nkv], refs[1 + nkv:1 + 2 * nkv]
        bias_ref, sink_ref, o_ref, do_ref = refs[1 + 2 * nkv:5 + 2 * nkv]
        dq_ref, dk_ref, dv_ref, dsink_ref = refs[5 + 2 * nkv:9 + 2 * nkv]
        dbias_ref = refs[9 + 2 * nkv] if grp.bias_per_head else None
        p, i = pl.program_id(0), pl.program_id(1)

        @pl.when(i == 0)
        def _():
            dk_ref[...] = jnp.zeros_like(dk_ref)
            dv_ref[...] = jnp.zeros_like(dv_ref)
            dsink_ref[...] = jnp.zeros_like(dsink_ref)
            if dbias_ref is not None:
                dbias_ref[...] = jnp.zeros_like(dbias_ref)

        k_all = jnp.concatenate([r[...] for r in k_refs], axis=0)
        v_all = jnp.concatenate([r[...] for r in v_refs], axis=0)
        start = 0 if grp.full else _win_start(grp, i)
        dqs, dks, dvs, dsinks = [], [], [], []
        for e in range(2):
            cols = slice(e * HEAD_DIM, (e + 1) * HEAD_DIM)
            q = q_ref[:, cols]
            k = _head_kv(grp, k_all, e, p)
            v = _head_kv(grp, v_all, e, p)
            snk = sink_ref[0:1, e * HEAD_DIM:e * HEAD_DIM + 1]
            pe, m, inv = _softmax_parts(q, k, bias_ref[e if grp.bias_per_head else 0], snk)
            prob = pe * inv
            do = do_ref[:, cols]
            do_b = do.astype(BF16)
            delta = jnp.sum(do * o_ref[:, cols], axis=-1, keepdims=True)
            dp = lax.dot_general(do_b, v, (((1,), (1,)), ((), ())), preferred_element_type=F32)
            ds = prob * (dp - delta)
            ds_b = ds.astype(BF16)
            dqs.append(jnp.dot(ds_b, k, preferred_element_type=F32))
            if wide:
                dks.append(lax.dot_general(q, ds_b, (((0,), (0,)), ((), ())), preferred_element_type=F32))
                dvs.append(lax.dot_general(do_b, prob.astype(BF16), (((0,), (0,)), ((), ())),
                                           preferred_element_type=F32))
            else:
                dks.append(lax.dot_general(ds_b, q, (((0,), (0,)), ((), ())), preferred_element_type=F32))
                dvs.append(lax.dot_general(prob.astype(BF16), do_b, (((0,), (0,)), ((), ())),
                                           preferred_element_type=F32))
            dsinks.append(-jnp.sum(jnp.exp(snk - m) * inv * delta, axis=0, keepdims=True))
            if dbias_ref is not None:
                shift = (i * QB - start * grp.kv_rows) // GRID_W
                for rq in range(rows_q):
                    for rk in range(grp.keys // GRID_W):
                        off = jnp.clip(rk - rq + (NA_ROWS - 1) - shift, 0, n_off - 1)
                        dbias_ref[e, off] += ds[rq * GRID_W:(rq + 1) * GRID_W, rk * GRID_W:(rk + 1) * GRID_W]
        dq_ref[...] = jnp.concatenate(dqs, axis=1)
        rows = pl.ds(0, t) if grp.full else pl.ds(pl.multiple_of(start * grp.kv_rows, grp.kv_rows), grp.keys)
        if wide:
            dk_ref[rows, :] += jnp.concatenate(dks, axis=0).T
            dv_ref[rows, :] += jnp.concatenate(dvs, axis=0).T
        else:
            dk_ref[rows, :] += jnp.concatenate(dks, axis=1)
            dv_ref[rows, :] += jnp.concatenate(dvs, axis=1)
        lane = lax.broadcasted_iota(jnp.int32, (8, LANES), 1)
        dsink_ref[...] += jnp.where(lane < HEAD_DIM, dsinks[0], dsinks[1])

    return pl.pallas_call(
        body, name=name, grid=(grp.pairs, NQB),
        in_specs=[q_spec, *k_specs, *v_specs, bias_spec, sink_spec, o_spec, o_spec],
        out_specs=out_specs, out_shape=out_shape,
        compiler_params=_params(("arbitrary", "arbitrary"), 56),
    )(*([_in_hbm(proj)] * (1 + 2 * nkv)), _in_hbm(bias), sink, _in_hbm(out), _in_hbm(d_out))


DILATED_CONFIGS = ((128, 1), (512, 4), (2048, 16))


def _bias_a():
    d = jnp.arange(SEQ)[None, :] - jnp.arange(SEQ)[:, None]
    mult = jnp.zeros((SEQ, SEQ), F32)
    for window, r in DILATED_CONFIGS:
        reach = (window // (2 * r)) * r
        mult = mult + ((d % r == 0) & (jnp.abs(d) <= reach)).astype(F32)
    return jnp.where(mult > 0, jnp.log(jnp.maximum(mult, 1.0)), NEG_INF).reshape(1, NQB, QB, SEQ)


def _bias_b():
    row = jnp.arange(QB)[None, :, None]
    col = jnp.arange(GROUP_B.keys)[None, None, :]
    var = jnp.arange(3)[:, None, None]
    d = col - (GROUP_B.kv_rows * var + row)
    return jnp.where(jnp.abs(d) <= WINDOW_B, 0.0, NEG_INF).astype(F32)[None]


def _offset_onehot():
    c = jnp.arange(GRID_W)[:, None, None]
    c2 = jnp.arange(GRID_W)[None, :, None]
    b = jnp.arange(LANES)[None, None, :]
    return (c2 - c + NA_COLS - 1 == b).astype(BF16).reshape(GRID_W * GRID_W, LANES)


def _split_dot(x, g):
    hi = x.astype(BF16)
    rest = x - hi.astype(F32)
    mid = rest.astype(BF16)
    lo = (rest - mid.astype(F32)).astype(BF16)
    return (jnp.dot(hi, g, preferred_element_type=F32) + jnp.dot(mid, g, preferred_element_type=F32)
            + jnp.dot(lo, g, preferred_element_type=F32))


def _table_mm(x, g, name):
    def body(x_ref, g_ref, o_ref):
        o_ref[...] = _split_dot(x_ref[...], g_ref[...])

    return pl.pallas_call(
        body, name=name, out_shape=jax.ShapeDtypeStruct((x.shape[0], g.shape[1]), F32),
        in_specs=[pl.BlockSpec(memory_space=pltpu.VMEM)] * 2, out_specs=pl.BlockSpec(memory_space=pltpu.VMEM),
        compiler_params=pltpu.CompilerParams(vmem_limit_bytes=32 * MIB),
    )(x, g)


N_OFF = 2 * NA_ROWS - 1
TABLE_ROWS = 152


def _bias_c(rpb):
    table = jnp.zeros((TABLE_ROWS, LANES), F32).at[:N_HEADS_C * N_OFF, :2 * NA_COLS - 1].set(
        rpb.reshape(N_HEADS_C * N_OFF, 2 * NA_COLS - 1))
    tiles = _table_mm(table, _offset_onehot().T, "rpb_tiles")[:N_HEADS_C * N_OFF]
    tiles = tiles.reshape(N_HEADS_C, N_OFF, GRID_W, GRID_W)
    c = jnp.arange(GRID_W)
    col_start = jnp.clip(c - NA_COLS // 2, 0, GRID_W - NA_COLS)
    col_ok = (c[None, :] >= col_start[:, None]) & (c[None, :] < col_start[:, None] + NA_COLS)
    tiles = jnp.where(col_ok, tiles, NEG_INF)
    rows_q = QB // GRID_W
    rows_k = GROUP_C.keys // GRID_W

    def body(t_ref, o_ref):
        for var in range(3):
            for rq in range(rows_q):
                r_l = rows_q * var + rq
                first = min(max(r_l - NA_ROWS // 2, 0), rows_k - NA_ROWS)
                for rk in range(rows_k):
                    if first <= rk < first + NA_ROWS:
                        tile = t_ref[rk - r_l + NA_ROWS - 1]
                    else:
                        tile = jnp.full((GRID_W, GRID_W), NEG_INF, F32)
                    o_ref[var, rq * GRID_W:(rq + 1) * GRID_W, rk * GRID_W:(rk + 1) * GRID_W] = tile

    return pl.pallas_call(
        body, name="bias_c", grid=(N_HEADS_C,),
        in_specs=[pl.BlockSpec((None, N_OFF, GRID_W, GRID_W), lambda h: (h, 0, 0, 0))],
        out_specs=pl.BlockSpec((None, 3, QB, GROUP_C.keys), lambda h: (h, 0, 0, 0)),
        out_shape=jax.ShapeDtypeStruct((N_HEADS_C, 3, QB, GROUP_C.keys), F32),
        compiler_params=_params(("arbitrary",), 32),
    )(tiles)


def _rpb_grad(d_tiles):
    flat = jnp.zeros((TABLE_ROWS, GRID_W * GRID_W), F32).at[:N_HEADS_C * N_OFF].set(
        d_tiles.reshape(N_HEADS_C * N_OFF, GRID_W * GRID_W))
    out = _table_mm(flat, _offset_onehot(), "rpb_grad")
    return out[:N_HEADS_C * N_OFF, :2 * NA_COLS - 1].reshape(N_HEADS_C, N_OFF, 2 * NA_COLS - 1)


def _sink_lanes(sink):
    return jnp.repeat(sink.astype(F32), HEAD_DIM)[None, :]


def _attention_fwd(proj_r, sink_b, bias_a, bias_b, bias_c):
    no_sink_a = jnp.full((1, WIDTH_A), NEG_INF, F32)
    no_sink_c = jnp.full((1, WIDTH_C), NEG_INF, F32)
    oa = _attn_fwd(GROUP_A, proj_r, bias_a, no_sink_a, "attn_a_fwd")
    ob = _attn_fwd(GROUP_B, proj_r, bias_b, _sink_lanes(sink_b), "attn_b_fwd")
    oc = _attn_fwd(GROUP_C, proj_r, bias_c, no_sink_c, "attn_c_fwd")
    return oa, ob, oc


def _attention_bwd(proj_r, sink_b, bias_a, bias_b, bias_c, outs, d_outs, cos, sin):
    no_sink_a = jnp.full((1, WIDTH_A), NEG_INF, F32)
    no_sink_c = jnp.full((1, WIDTH_C), NEG_INF, F32)
    dqa, dka, dva, _ = _attn_bwd(GROUP_A, proj_r, bias_a, no_sink_a, outs[0], d_outs[0], "attn_a_bwd")
    dqb, dkb, dvb, dsink = _attn_bwd(GROUP_B, proj_r, bias_b, _sink_lanes(sink_b), outs[1], d_outs[1], "attn_b_bwd")
    dqc, dkc, dvc, _, d_tiles = _attn_bwd(GROUP_C, proj_r, bias_c, no_sink_c, outs[2], d_outs[2], "attn_c_bwd")
    d_proj = _rope_bwd((dqa, dka, dva, dqb, dkb, dvb, dqc, dkc, dvc), cos, sin, "rope_bwd")
    d_sink = dsink[:, 0, :].reshape(GROUP_B.pairs, 2, HEAD_DIM)[:, :, 0].reshape(N_HEADS_B)
    return d_proj, d_sink, _rpb_grad(d_tiles)


def _adamw(w, g, m, v, name):
    r, c = w.shape
    rows = r
    for cand in (512, 256, 128, 64, 32, 16, 8):
        if r % cand == 0 and cand * c * 4 <= MIB:
            rows = cand
            break
    spec = pl.BlockSpec((rows, c), lambda i: (i, 0))

    def body(w_ref, g_ref, m_ref, v_ref, d_ref, mo_ref, vo_ref):
        d_ref[...], mo_ref[...], vo_ref[...] = _adamw_step(w_ref[...], g_ref[...], m_ref[...], v_ref[...])

    return pl.pallas_call(
        body, name=name, grid=(r // rows,), in_specs=[spec] * 4, out_specs=[spec] * 3,
        out_shape=[jax.ShapeDtypeStruct((r, c), F32)] * 3, compiler_params=_params(("arbitrary",), 32),
    )(w, g, m, v)


def _adamw_step(w, grad, m, v):
    m_new = ADAM_B1 * m + (1.0 - ADAM_B1) * grad
    v_new = ADAM_B2 * v + (1.0 - ADAM_B2) * jnp.square(grad)
    m_hat = m_new / (1.0 - ADAM_B1 ** ADAM_STEP)
    v_hat = v_new / (1.0 - ADAM_B2 ** ADAM_STEP)
    return -ADAM_LR * (m_hat / (jnp.sqrt(v_hat) + ADAM_EPS) + ADAM_WD * w), m_new, v_new


def _adamw_layer(w, g, m, v, layer, prev, name):
    _, r, c = w.shape
    rows = next(cand for cand in (512, 256, 128, 64, 32, 16, 8) if r % cand == 0 and cand * c * 4 <= MIB)
    spec = pl.BlockSpec((None, rows, c), lambda i: (layer, i, 0))
    n_prev = 0 if prev is None else 4

    def body(w_ref, g_ref, m_ref, v_ref, *rest):
        go_ref, d_ref, mo_ref, vo_ref = rest[n_prev:]
        grad = g_ref[...]
        go_ref[...] = grad
        d_ref[...], mo_ref[...], vo_ref[...] = _adamw_step(w_ref[...], grad, m_ref[...], v_ref[...])

    return pl.pallas_call(
        body, name=name, grid=(r // rows,), in_specs=[spec] * 4 + [ANY_SPEC] * n_prev, out_specs=[spec] * 4,
        out_shape=[jax.ShapeDtypeStruct(w.shape, F32)] * 4,
        input_output_aliases={4 + i: i for i in range(n_prev)}, compiler_params=_params(("arbitrary",), 32),
    )(w, g, m, v, *(prev or ()))


def _layer_fwd(x0, p, weight, tabs):
    h1 = _rmsnorm_fwd(x0, p["ln_attn"], "ln_attn_fwd")
    proj = _mm_nn(h1, weight("w_in", h1), cols=True, tn=256, tk=D_MODEL, out_dtype=F32, name="mm_in")
    proj_r = _rope_fwd(proj, tabs["cos"], tabs["sin"], "rope_fwd")
    outs = _attention_fwd(proj_r, p["sink_b"], tabs["bias_a"], tabs["bias_b"], p["bias_c"])
    mixed = _mix_fwd(*outs, p["mix_gain"], "mix_fwd")
    x1 = _mm_nn(mixed, weight("w_out", mixed), cols=False, tn=256, tk=D_MODEL, out_dtype=F32, name="mm_out",
                residual=x0)
    h2 = _rmsnorm_fwd(x1, p["ln_ffn"], "ln_ffn_fwd")
    u0 = _mm_nn(h2, weight("w_up", h2), cols=True, tn=256, tk=D_MODEL, out_dtype=F32, name="mm_up", out_split=2)
    act = _convgate_fwd(u0, p["conv_w"], p["conv_b"], "convgate_fwd")
    x2 = _mm_nn(act, weight("w_down", act), cols=False, tn=512, tk=D_FF // 2, out_dtype=F32, name="mm_down",
                residual=x1)
    return x2, (x0, h1, proj_r, outs, mixed, x1, h2, u0, act)


def _layer_bwd(dx2, dx2_b, saved, p, big, tabs, begin, finish, pending):
    x0, h1, proj_r, outs, mixed, x1, h2, u0, act = saved
    d_act = _mm_nt(dx2_b, big["w_down"], cols=False, to=512, tr=D_MODEL, out_dtype=F32, name="nt_down",
                   after=[pending[1]] if pending else [])
    g_down = _mm_tn(act, dx2_b, tk=D_FF // N_SHARDS, tn=512, shards=-N_SHARDS, name="tn_down")
    du0, d_conv_w, d_conv_b = _convgate_bwd(u0, p["conv_w"], p["conv_b"], d_act, "convgate_bwd")
    token = [finish(pending[0], [du0])] if pending else []
    dh2 = _mm_nt(du0, big["w_up"], cols=True, to=1024, tr=D_FF // 4, out_dtype=F32, name="nt_up", after=token)
    g_up = _mm_tn(h2, du0, tk=512, tn=D_FF // 4, shards=N_SHARDS, name="tn_up")
    first, token = begin({"w_down": g_down, "w_up": g_up})
    dx1, dx1_b, d_ln_ffn = _rmsnorm_bwd(x1, p["ln_ffn"], dh2, dx2, "ln_ffn_bwd", after=[token])
    d_mixed = _mm_nt(dx1_b, big["w_out"], cols=False, to=512, tr=D_MODEL, out_dtype=F32, name="nt_out")
    g_out = _mm_tn(mixed, dx1_b, tk=D_MODEL // N_SHARDS, tn=512, shards=-N_SHARDS, name="tn_out")
    token = finish(first, [g_out])
    *d_outs, d_mix_gain = _mix_bwd(*outs, p["mix_gain"], d_mixed, "mix_bwd", after=[token])
    d_proj, d_sink, d_rpb = _attention_bwd(proj_r, p["sink_b"], tabs["bias_a"], tabs["bias_b"], p["bias_c"], outs,
                                           d_outs, tabs["cos"], tabs["sin"])
    dh1 = _mm_nt(d_proj, big["w_in"], cols=True, to=1024, tr=IN_COLS // N_SHARDS, out_dtype=F32, name="nt_in")
    g_in = _mm_tn(h1, d_proj, tk=512, tn=IN_COLS // N_SHARDS, shards=N_SHARDS, name="tn_in")
    dx0, dx0_b, d_ln_attn = _rmsnorm_bwd(x0, p["ln_attn"], dh1, dx1, "ln_attn_bwd")
    small = {"ln_attn": d_ln_attn, "sink_b": d_sink, "rpb_c": d_rpb, "mix_gain": d_mix_gain, "ln_ffn": d_ln_ffn,
             "conv_w": d_conv_w, "conv_b": d_conv_b}
    return dx0, dx0_b, small, begin({"w_out": g_out, "w_in": g_in})


HBM_SPEC = pl.BlockSpec(memory_space=pl.ANY)


def _place():
    x, y, c = lax.axis_index("x"), lax.axis_index("y"), lax.axis_index("c")
    chips = ((1 - x, y), (x, 1 - y), (1 - x, 1 - y))
    return x, y, c, chips


def _shard_index(px, py):
    return 2 * px + py


def _remote(src, dst, send_sem, recv_sem, to):
    return pltpu.make_async_remote_copy(src_ref=src, dst_ref=dst, send_sem=send_sem, recv_sem=recv_sem,
                                        device_id=to, device_id_type=MESH)


def _own_slot(w, layer, shard, name):
    _, r, c_dim = w.shape
    rows = r
    for cand in (512, 256, 128):
        if r % cand == 0 and cand * c_dim * 4 <= 2 * MIB:
            rows = cand
            break

    def body(s_ref, w_ref, o_ref):
        o_ref[...] = w_ref[...].astype(BF16)

    return pl.pallas_call(
        body, name=name,
        grid_spec=pltpu.PrefetchScalarGridSpec(
            num_scalar_prefetch=1, grid=(r // rows,),
            in_specs=[pl.BlockSpec((None, rows, c_dim), lambda i, s: (layer, i, 0))],
            out_specs=pl.BlockSpec((None, rows, c_dim), lambda i, s: (s[0], i, 0))),
        out_shape=jax.ShapeDtypeStruct((N_SHARDS, r, c_dim), BF16),
        compiler_params=_params(("arbitrary",), 32),
    )(shard.astype(jnp.int32).reshape(1), w)


def _gather_weights(bufs):
    n = len(bufs)

    def body(*refs):
        outs = refs[n:2 * n]
        send1, recv1, send2, recv2 = refs[2 * n:]
        x, y, c, chips = _place()
        me = _shard_index(x, y)
        sibling = (x, y, 1 - c)
        first = []
        for t in range(n):
            for j, (px, py) in enumerate(chips):
                mine = outs[t].at[c, me]
                cp = _remote(mine, mine, send1.at[t * 3 + j], recv1.at[t * 3 + j], (px, py, c))
                cp.start()
                first.append(cp)
        passed = []
        for t in range(n):
            for j, (px, py) in enumerate(chips):
                slot = outs[t].at[c, _shard_index(px, py)]
                _remote(slot, slot, send1.at[t * 3 + j], recv1.at[t * 3 + j], (px, py, c)).wait_recv()
                cp = _remote(slot, slot, send2.at[t * 3 + j], recv2.at[t * 3 + j], sibling)
                cp.start()
                passed.append(cp)
        for t in range(n):
            for j, (px, py) in enumerate(chips):
                slot = outs[t].at[1 - c, _shard_index(px, py)]
                _remote(slot, slot, send2.at[t * 3 + j], recv2.at[t * 3 + j], sibling).wait_recv()
        for cp in first + passed:
            cp.wait_send()

    return pl.pallas_call(
        body, name="gather_weights", in_specs=[HBM_SPEC] * n, out_specs=[HBM_SPEC] * n,
        out_shape=[jax.ShapeDtypeStruct(b.shape, b.dtype) for b in bufs],
        input_output_aliases={t: t for t in range(n)},
        scratch_shapes=[pltpu.SemaphoreType.DMA((n * 3,))] * 4,
    )(*bufs)


def _pair_exchange(bufs):
    n = len(bufs)

    def body(*refs):
        ins, outs = refs[:n], refs[n:2 * n]
        send, recv = refs[2 * n:]
        x, y, c, _ = _place()
        sibling = (x, y, 1 - c)
        cps = [_remote(ins[t].at[1 - c], outs[t], send.at[t], recv.at[t], sibling) for t in range(n)]
        for cp in cps:
            cp.start()
        for cp in cps:
            cp.wait()

    return pl.pallas_call(
        body, name="pair_exchange", in_specs=[HBM_SPEC] * n, out_specs=[HBM_SPEC] * n,
        out_shape=[jax.ShapeDtypeStruct(b.shape[1:], b.dtype) for b in bufs],
        scratch_shapes=[pltpu.SemaphoreType.DMA((n,))] * 2,
    )(*bufs)


def _pair_sum(own, other, name):
    _, s, r, c_dim = own.shape
    rows = min(r, LANES)
    per = r // rows
    layer = lax.axis_index("c").astype(jnp.int32).reshape(1)

    def body(layer_ref, a_ref, b_ref, o_ref):
        o_ref[...] = (a_ref[...] + b_ref[...]).astype(BF16)

    return pl.pallas_call(
        body, name=name,
        grid_spec=pltpu.PrefetchScalarGridSpec(
            num_scalar_prefetch=1, grid=(s * per,),
            in_specs=[pl.BlockSpec((None, None, rows, c_dim), lambda i, lay: (lay[0], i // per, i % per, 0)),
                      pl.BlockSpec((None, rows, c_dim), lambda i, lay: (i // per, i % per, 0))],
            out_specs=pl.BlockSpec((None, rows, c_dim), lambda i, lay: (i // per, i % per, 0))),
        out_shape=jax.ShapeDtypeStruct((s, r, c_dim), BF16), compiler_params=_params(("arbitrary",), 40),
    )(layer, own, other)


def _chip_exchange(bufs):
    n = len(bufs)

    def body(*refs):
        ins, outs = refs[:n], refs[n:2 * n]
        send, recv = refs[2 * n:]
        x, y, c, chips = _place()
        me = _shard_index(x, y)
        cps = []
        for t in range(n):
            for j, (px, py) in enumerate(chips):
                cp = _remote(ins[t].at[_shard_index(px, py)], outs[t].at[me], send.at[t * 3 + j], recv.at[t * 3 + j],
                             (px, py, c))
                cp.start()
                cps.append(cp)
        for t in range(n):
            for j, (px, py) in enumerate(chips):
                slot = outs[t].at[_shard_index(px, py)]
                _remote(slot, slot, send.at[t * 3 + j], recv.at[t * 3 + j], (px, py, c)).wait_recv()
        for cp in cps:
            cp.wait_send()

    return pl.pallas_call(
        body, name="chip_exchange", in_specs=[HBM_SPEC] * n, out_specs=[HBM_SPEC] * n,
        out_shape=[jax.ShapeDtypeStruct(b.shape, b.dtype) for b in bufs],
        scratch_shapes=[pltpu.SemaphoreType.DMA((n * 3,))] * 2,
    )(*bufs)


HBM_ONLY = pl.BlockSpec(memory_space=pltpu.HBM)
SEM_SPEC = pl.BlockSpec(memory_space=pltpu.SEMAPHORE)
DATAFLOW = pltpu.SideEffectType.DATAFLOW_SIDE_EFFECTING


def _in_hbm(a):
    return pltpu.with_memory_space_constraint(a, pltpu.HBM)


def _chip_exchange_start(bufs, name):
    n = len(bufs)

    def body(*refs):
        ins, lands = refs[:n], refs[n:2 * n]
        send, recv = refs[2 * n], refs[2 * n + 1]
        token = refs[-1]
        x, y, c, chips = _place()
        me = _shard_index(x, y)
        for t in range(n):
            for j, (px, py) in enumerate(chips):
                _remote(ins[t].at[_shard_index(px, py)], lands[t].at[me], send.at[t * 3 + j], recv.at[t * 3 + j],
                        (px, py, c)).start()
        token[...] = jnp.zeros_like(token)

    thru = [pltpu.HBM(b.shape, b.dtype) for b in bufs]
    res = pl.pallas_call(
        body, name=name,
        out_shape=(pltpu.SemaphoreType.DMA((n * 3,)), pltpu.SemaphoreType.DMA((n * 3,)), *thru, *thru,
                   jax.ShapeDtypeStruct((8, LANES), F32)),
        in_specs=[HBM_ONLY] * (2 * n),
        out_specs=(SEM_SPEC, SEM_SPEC, *([HBM_ONLY] * (2 * n)), pl.BlockSpec(memory_space=pltpu.VMEM)),
        input_output_aliases={i: 2 + i for i in range(2 * n)},
        compiler_params=pltpu.CompilerParams(has_side_effects=DATAFLOW),
    )(*[_in_hbm(b) for b in bufs], *[_in_hbm(lax.empty(b.shape, b.dtype)) for b in bufs])
    return res[0], res[1], list(res[2:2 + n]), list(res[2 + n:2 + 2 * n]), res[-1]


def _chip_exchange_wait(send, recv, bufs, lands, after, name):
    n = len(bufs)

    def body(*refs):
        ins, outs = refs[:n], refs[n:2 * n]
        send_ref, recv_ref = refs[2 * n], refs[2 * n + 1]
        x, y, c, chips = _place()
        for t in range(n):
            for j, (px, py) in enumerate(chips):
                sent = ins[t].at[_shard_index(px, py)]
                slot = outs[t].at[_shard_index(px, py)]
                cp = _remote(sent, slot, send_ref.at[t * 3 + j], recv_ref.at[t * 3 + j], (px, py, c))
                cp.wait_send()
                cp.wait_recv()

    thru = [pltpu.HBM(b.shape, b.dtype) for b in bufs]
    res = pl.pallas_call(
        body, name=name, out_shape=(*thru, *thru),
        in_specs=[HBM_ONLY] * (2 * n) + [SEM_SPEC, SEM_SPEC, pl.BlockSpec(memory_space=pl.ANY)],
        out_specs=[HBM_ONLY] * (2 * n),
        input_output_aliases={i: i for i in range(2 * n)},
        compiler_params=pltpu.CompilerParams(has_side_effects=DATAFLOW),
    )(*bufs, *lands, send, recv, after)
    return list(res[:n]), list(res[n:])


def _chip_sum(pair, landed, name):
    s, r, c_dim = pair.shape
    rows = min(r, LANES)
    shard = _shard_index(lax.axis_index("x"), lax.axis_index("y"))
    where = jnp.stack([shard, lax.axis_index("c")]).astype(jnp.int32)

    def landed_spec(k):
        return pl.BlockSpec((None, rows, c_dim), lambda i, w: (jnp.where(w[0] == k, (k + 1) % s, k), i, 0))

    def body(w_ref, own_ref, *rest):
        o_ref = rest[s]
        acc = None
        for k in range(s):
            term = jnp.where(w_ref[0] == k, own_ref[...], rest[k][...]).astype(F32)
            acc = term if acc is None else acc + term
        o_ref[...] = acc

    return pl.pallas_call(
        body, name=name,
        grid_spec=pltpu.PrefetchScalarGridSpec(
            num_scalar_prefetch=1, grid=(r // rows,),
            in_specs=[pl.BlockSpec((None, rows, c_dim), lambda i, w: (w[0], i, 0))] + [landed_spec(k) for k in range(s)],
            out_specs=pl.BlockSpec((None, rows, c_dim), lambda i, w: (w[1], i, 0))),
        out_shape=jax.ShapeDtypeStruct((DEPTH, r, c_dim), F32), compiler_params=_params(("arbitrary",), 40),
    )(where, pair, *([landed] * s))


def _sum_slots(buf, name):
    s, r, c_dim = buf.shape
    rows = min(r, LANES)

    def body(i_ref, o_ref):
        acc = i_ref[0].astype(F32)
        for k in range(1, s):
            acc = acc + i_ref[k].astype(F32)
        o_ref[...] = acc

    return pl.pallas_call(
        body, name=name, grid=(r // rows,),
        in_specs=[pl.BlockSpec((s, rows, c_dim), lambda i: (0, i, 0))],
        out_specs=pl.BlockSpec((rows, c_dim), lambda i: (i, 0)),
        out_shape=jax.ShapeDtypeStruct((r, c_dim), F32), compiler_params=_params(("arbitrary",), 40),
    )(buf)


def _pair_gather(bufs):
    n = len(bufs)

    def body(*refs):
        outs = refs[n:2 * n]
        send, recv = refs[2 * n:]
        x, y, c, _ = _place()
        sibling = (x, y, 1 - c)
        cps = [_remote(outs[t].at[c], outs[t].at[c], send.at[t], recv.at[t], sibling) for t in range(n)]
        for cp in cps:
            cp.start()
        for t in range(n):
            slot = outs[t].at[1 - c]
            _remote(slot, slot, send.at[t], recv.at[t], sibling).wait_recv()
        for cp in cps:
            cp.wait_send()

    return pl.pallas_call(
        body, name="pair_gather", in_specs=[HBM_SPEC] * n, out_specs=[HBM_SPEC] * n,
        out_shape=[jax.ShapeDtypeStruct(b.shape, b.dtype) for b in bufs],
        input_output_aliases={t: t for t in range(n)},
        scratch_shapes=[pltpu.SemaphoreType.DMA((n,))] * 2,
    )(*bufs)


N_DEV = 8


def _all_gather_small(vec, name, after=()):
    n_after = len(after)

    def body(v_ref, *rest):
        o_ref, send, recv, local_sem = rest[n_after:]
        x, y, c, _ = _place()
        me = 4 * x + 2 * y + c
        local = pltpu.make_async_copy(v_ref, o_ref.at[me], local_sem)
        local.start()
        flips = [(fx, fy, fc) for fx in (0, 1) for fy in (0, 1) for fc in (0, 1)][1:]
        peers = [((1 - x) if fx else x, (1 - y) if fy else y, (1 - c) if fc else c) for fx, fy, fc in flips]
        cps = [_remote(v_ref, o_ref.at[me], send.at[k], recv.at[k], peer) for k, peer in enumerate(peers)]
        for cp in cps:
            cp.start()
        for k, (px, py, pc) in enumerate(peers):
            slot = o_ref.at[4 * px + 2 * py + pc]
            _remote(slot, slot, send.at[k], recv.at[k], (px, py, pc)).wait_recv()
        for cp in cps:
            cp.wait_send()
        local.wait()

    return pl.pallas_call(
        body, name=name, in_specs=[HBM_SPEC] * (1 + n_after), out_specs=HBM_SPEC,
        out_shape=jax.ShapeDtypeStruct((N_DEV,) + vec.shape, vec.dtype),
        scratch_shapes=[pltpu.SemaphoreType.DMA((N_DEV - 1,))] * 2 + [pltpu.SemaphoreType.DMA(())],
    )(vec, *after)


def _peers(x, y, c):
    flips = [(fx, fy, fc) for fx in (0, 1) for fy in (0, 1) for fc in (0, 1)][1:]
    return [((1 - x) if fx else x, (1 - y) if fy else y, (1 - c) if fc else c) for fx, fy, fc in flips]


def _small_start(vec, after, name):
    n_after = len(after)

    def body(v_ref, slots_ref, *rest):
        send, recv = rest[n_after], rest[n_after + 1]
        token = rest[-1]
        x, y, c, _ = _place()
        me = 4 * x + 2 * y + c
        for k, peer in enumerate(_peers(x, y, c)):
            _remote(v_ref, slots_ref.at[me], send.at[k], recv.at[k], peer).start()
        token[...] = jnp.zeros_like(token)

    slots = jax.ShapeDtypeStruct((N_DEV,) + vec.shape, vec.dtype)
    res = pl.pallas_call(
        body, name=name,
        out_shape=(pltpu.SemaphoreType.DMA((N_DEV - 1,)), pltpu.SemaphoreType.DMA((N_DEV - 1,)),
                   pltpu.HBM(vec.shape, vec.dtype), pltpu.HBM(slots.shape, slots.dtype),
                   jax.ShapeDtypeStruct((8, LANES), F32)),
        in_specs=[HBM_ONLY, HBM_ONLY] + [ANY_SPEC] * n_after,
        out_specs=(SEM_SPEC, SEM_SPEC, HBM_ONLY, HBM_ONLY, pl.BlockSpec(memory_space=pltpu.VMEM)),
        input_output_aliases={0: 2, 1: 3},
        compiler_params=pltpu.CompilerParams(has_side_effects=DATAFLOW),
    )(_in_hbm(vec), _in_hbm(lax.empty(slots.shape, slots.dtype)), *after)
    return res


def _small_wait(send, recv, vec, slots, after, name):
    def body(v_ref, slots_ref, send_ref, recv_ref, *rest):
        x, y, c, _ = _place()
        for k, (px, py, pc) in enumerate(_peers(x, y, c)):
            cp = _remote(v_ref, slots_ref.at[4 * px + 2 * py + pc], send_ref.at[k], recv_ref.at[k], (px, py, pc))
            cp.wait_send()
            cp.wait_recv()

    return pl.pallas_call(
        body, name=name, out_shape=(pltpu.HBM(vec.shape, vec.dtype), pltpu.HBM(slots.shape, slots.dtype)),
        in_specs=[HBM_ONLY, HBM_ONLY, SEM_SPEC, SEM_SPEC] + [ANY_SPEC] * len(after), out_specs=[HBM_ONLY, HBM_ONLY],
        input_output_aliases={0: 0, 1: 1},
        compiler_params=pltpu.CompilerParams(has_side_effects=DATAFLOW),
    )(vec, slots, send, recv, *after)


def _small_sum(vec, slots, name):
    rows = vec.shape[0]
    blk = min(rows, 256)
    x, y, c = lax.axis_index("x"), lax.axis_index("y"), lax.axis_index("c")
    me = (4 * x + 2 * y + c).astype(jnp.int32).reshape(1)

    def slot_spec(k):
        return pl.BlockSpec((None, blk, LANES), lambda i, w: (jnp.where(w[0] == k, (k + 1) % N_DEV, k), i, 0))

    def body(w_ref, v_ref, *rest):
        o_ref = rest[-1]
        acc = None
        for k in range(N_DEV):
            term = jnp.where(w_ref[0] == k, v_ref[...], rest[k][...])
            acc = term if acc is None else acc + term
        o_ref[...] = acc

    return pl.pallas_call(
        body, name=name,
        grid_spec=pltpu.PrefetchScalarGridSpec(
            num_scalar_prefetch=1, grid=(rows // blk,),
            in_specs=[pl.BlockSpec((blk, LANES), lambda i, w: (i, 0))] + [slot_spec(k) for k in range(N_DEV)],
            out_specs=pl.BlockSpec((blk, LANES), lambda i, w: (i, 0))),
        out_shape=jax.ShapeDtypeStruct(vec.shape, F32), compiler_params=_params(("arbitrary",), 32),
    )(me, vec, *([slots] * N_DEV))


def _half(ref, slot, c):
    half = ref.shape[1] // 2
    return ref.at[slot, pl.ds(pl.multiple_of(c * half, 8), half)]


def _gather_start(bufs, after, name):
    n = len(bufs)
    n_after = len(after)

    def body(*refs):
        ins = refs[:n]
        send, recv = refs[n + n_after], refs[n + n_after + 1]
        token = refs[-1]
        x, y, c, chips = _place()
        me = _shard_index(x, y)
        for t in range(n):
            for j, (px, py) in enumerate(chips):
                mine = _half(ins[t], me, c)
                _remote(mine, mine, send.at[t * 3 + j], recv.at[t * 3 + j], (px, py, c)).start()
        token[...] = jnp.zeros_like(token)

    thru = [pltpu.HBM(b.shape, b.dtype) for b in bufs]
    res = pl.pallas_call(
        body, name=name,
        out_shape=(pltpu.SemaphoreType.DMA((n * 3,)), pltpu.SemaphoreType.DMA((n * 3,)), *thru,
                   jax.ShapeDtypeStruct((8, LANES), F32)),
        in_specs=[HBM_ONLY] * n + [ANY_SPEC] * n_after,
        out_specs=(SEM_SPEC, SEM_SPEC, *([HBM_ONLY] * n), pl.BlockSpec(memory_space=pltpu.VMEM)),
        input_output_aliases={i: 2 + i for i in range(n)},
        compiler_params=pltpu.CompilerParams(has_side_effects=DATAFLOW),
    )(*[_in_hbm(b) for b in bufs], *after)
    return res[0], res[1], list(res[2:2 + n]), res[-1]


def _gather_wait(send, recv, bufs, after, name):
    n = len(bufs)

    def body(*refs):
        ins = refs[:n]
        send_ref, recv_ref = refs[n], refs[n + 1]
        x, y, c, chips = _place()
        me = _shard_index(x, y)
        for t in range(n):
            for j, (px, py) in enumerate(chips):
                cp = _remote(_half(ins[t], me, c), _half(ins[t], _shard_index(px, py), c), send_ref.at[t * 3 + j],
                             recv_ref.at[t * 3 + j], (px, py, c))
                cp.wait_send()
                cp.wait_recv()

    res = pl.pallas_call(
        body, name=name, out_shape=tuple(pltpu.HBM(b.shape, b.dtype) for b in bufs),
        in_specs=[HBM_ONLY] * n + [SEM_SPEC, SEM_SPEC] + [ANY_SPEC] * len(after), out_specs=[HBM_ONLY] * n,
        input_output_aliases={i: i for i in range(n)},
        compiler_params=pltpu.CompilerParams(has_side_effects=DATAFLOW),
    )(*bufs, send, recv, *after)
    return list(res)


def _gather_forward(bufs, name):
    n = len(bufs)

    def body(*refs):
        outs = refs[n:2 * n]
        send, recv = refs[2 * n:]
        x, y, c, chips = _place()
        sibling = (x, y, 1 - c)
        cps = []
        for t in range(n):
            for j, (px, py) in enumerate(chips):
                got = _half(outs[t], _shard_index(px, py), c)
                cp = _remote(got, got, send.at[t * 3 + j], recv.at[t * 3 + j], sibling)
                cp.start()
                cps.append(cp)
        for t in range(n):
            for j, (px, py) in enumerate(chips):
                theirs = _half(outs[t], _shard_index(px, py), 1 - c)
                _remote(theirs, theirs, send.at[t * 3 + j], recv.at[t * 3 + j], sibling).wait_recv()
        for cp in cps:
            cp.wait_send()

    return pl.pallas_call(
        body, name=name, in_specs=[HBM_SPEC] * n, out_specs=[HBM_SPEC] * n,
        out_shape=[jax.ShapeDtypeStruct(b.shape, b.dtype) for b in bufs],
        input_output_aliases={t: t for t in range(n)},
        scratch_shapes=[pltpu.SemaphoreType.DMA((n * 3,))] * 2,
    )(*bufs)


def _sibling_rows(ref, c):
    half = ref.shape[1] // 2
    return ref.at[:, pl.ds(pl.multiple_of((1 - c) * half, 8), half)]


def _half_exchange_start(grads, name):
    n = len(grads)

    def body(*refs):
        ins, lands = refs[:n], refs[n:2 * n]
        send, recv = refs[2 * n], refs[2 * n + 1]
        token = refs[-1]
        x, y, c, _ = _place()
        for t in range(n):
            _remote(_sibling_rows(ins[t], c), lands[t], send.at[t], recv.at[t], (x, y, 1 - c)).start()
        token[...] = jnp.zeros_like(token)

    halves = [jax.ShapeDtypeStruct((g.shape[0], g.shape[1] // 2, g.shape[2]), g.dtype) for g in grads]
    res = pl.pallas_call(
        body, name=name,
        out_shape=(pltpu.SemaphoreType.DMA((n,)), pltpu.SemaphoreType.DMA((n,)),
                   *[pltpu.HBM(g.shape, g.dtype) for g in grads], *[pltpu.HBM(h.shape, h.dtype) for h in halves],
                   jax.ShapeDtypeStruct((8, LANES), F32)),
        in_specs=[HBM_ONLY] * (2 * n),
        out_specs=(SEM_SPEC, SEM_SPEC, *([HBM_ONLY] * (2 * n)), pl.BlockSpec(memory_space=pltpu.VMEM)),
        input_output_aliases={i: 2 + i for i in range(2 * n)},
        compiler_params=pltpu.CompilerParams(has_side_effects=DATAFLOW),
    )(*[_in_hbm(g) for g in grads], *[_in_hbm(lax.empty(h.shape, h.dtype)) for h in halves])
    return res[0], res[1], list(res[2:2 + n]), list(res[2 + n:2 + 2 * n]), res[-1]


def _half_exchange_wait(send, recv, grads, lands, after, name):
    n = len(grads)

    def body(*refs):
        ins, got = refs[:n], refs[n:2 * n]
        send_ref, recv_ref = refs[2 * n], refs[2 * n + 1]
        x, y, c, _ = _place()
        for t in range(n):
            cp = _remote(_sibling_rows(ins[t], c), got[t], send_ref.at[t], recv_ref.at[t], (x, y, 1 - c))
            cp.wait_send()
            cp.wait_recv()

    res = pl.pallas_call(
        body, name=name,
        out_shape=(*[pltpu.HBM(g.shape, g.dtype) for g in grads], *[pltpu.HBM(h.shape, h.dtype) for h in lands]),
        in_specs=[HBM_ONLY] * (2 * n) + [SEM_SPEC, SEM_SPEC] + [ANY_SPEC] * len(after),
        out_specs=[HBM_ONLY] * (2 * n),
        input_output_aliases={i: i for i in range(2 * n)},
        compiler_params=pltpu.CompilerParams(has_side_effects=DATAFLOW),
    )(*grads, *lands, send, recv, *after)
    return list(res[:n]), list(res[n:])


def _half_rows(half, c_dim):
    for cand in (512, 256, 128, 64):
        if half % cand == 0 and cand * c_dim * 2 <= 2 * MIB:
            return cand
    raise ValueError((half, c_dim))


def _core_index():
    return lax.axis_index("c").astype(jnp.int32).reshape(1)


def _half_sum(own, other, name):
    s, r, c_dim = own.shape
    rows = _half_rows(r // 2, c_dim)
    per = r // 2 // rows

    def body(c_ref, a_ref, b_ref, o_ref):
        o_ref[...] = (a_ref[...].astype(F32) + b_ref[...].astype(F32)).astype(BF16)

    return pl.pallas_call(
        body, name=name,
        grid_spec=pltpu.PrefetchScalarGridSpec(
            num_scalar_prefetch=1, grid=(s, per),
            in_specs=[pl.BlockSpec((None, rows, c_dim), lambda k, i, c: (k, c[0] * per + i, 0)),
                      pl.BlockSpec((None, rows, c_dim), lambda k, i, c: (k, i, 0))],
            out_specs=pl.BlockSpec((None, rows, c_dim), lambda k, i, c: (k, i, 0))),
        out_shape=pltpu.HBM((s, r // 2, c_dim), BF16), compiler_params=_params(("arbitrary", "arbitrary"), 32),
    )(_core_index(), own, other)


def _reduce_start(pairs, name):
    n = len(pairs)

    def body(*refs):
        ins, lands = refs[:n], refs[n:2 * n]
        send, recv = refs[2 * n], refs[2 * n + 1]
        token = refs[-1]
        x, y, c, chips = _place()
        me = _shard_index(x, y)
        for t in range(n):
            for j, (px, py) in enumerate(chips):
                _remote(ins[t].at[_shard_index(px, py)], lands[t].at[me], send.at[t * 3 + j], recv.at[t * 3 + j],
                        (px, py, c)).start()
        token[...] = jnp.zeros_like(token)

    thru = [pltpu.HBM(b.shape, b.dtype) for b in pairs]
    res = pl.pallas_call(
        body, name=name,
        out_shape=(pltpu.SemaphoreType.DMA((n * 3,)), pltpu.SemaphoreType.DMA((n * 3,)), *thru, *thru,
                   jax.ShapeDtypeStruct((8, LANES), F32)),
        in_specs=[HBM_ONLY] * (2 * n),
        out_specs=(SEM_SPEC, SEM_SPEC, *([HBM_ONLY] * (2 * n)), pl.BlockSpec(memory_space=pltpu.VMEM)),
        input_output_aliases={i: 2 + i for i in range(2 * n)},
        compiler_params=pltpu.CompilerParams(has_side_effects=DATAFLOW),
    )(*[_in_hbm(b) for b in pairs], *[_in_hbm(lax.empty(b.shape, b.dtype)) for b in pairs])
    return res[0], res[1], list(res[2:2 + n]), list(res[2 + n:2 + 2 * n]), res[-1]


def _reduce_wait(send, recv, pairs, lands, after, name):
    n = len(pairs)

    def body(*refs):
        ins, got = refs[:n], refs[n:2 * n]
        send_ref, recv_ref = refs[2 * n], refs[2 * n + 1]
        x, y, c, chips = _place()
        for t in range(n):
            for j, (px, py) in enumerate(chips):
                s = _shard_index(px, py)
                cp = _remote(ins[t].at[s], got[t].at[s], send_ref.at[t * 3 + j], recv_ref.at[t * 3 + j], (px, py, c))
                cp.wait_send()
                cp.wait_recv()

    thru = [pltpu.HBM(b.shape, b.dtype) for b in pairs]
    res = pl.pallas_call(
        body, name=name, out_shape=(*thru, *thru),
        in_specs=[HBM_ONLY] * (2 * n) + [SEM_SPEC, SEM_SPEC] + [ANY_SPEC] * len(after),
        out_specs=[HBM_ONLY] * (2 * n),
        input_output_aliases={i: i for i in range(2 * n)},
        compiler_params=pltpu.CompilerParams(has_side_effects=DATAFLOW),
    )(*pairs, *lands, send, recv, *after)
    return list(res[:n]), list(res[n:])


def _reduce_sum(pair, landed, layer, prev, name):
    s, half, c_dim = pair.shape
    rows = _half_rows(half, c_dim)
    per = half // rows
    shard = _shard_index(lax.axis_index("x"), lax.axis_index("y"))
    where = jnp.stack([shard, lax.axis_index("c")]).astype(jnp.int32)

    def landed_spec(k):
        return pl.BlockSpec((None, rows, c_dim), lambda i, w: (jnp.where(w[0] == k, (k + 1) % s, k), i, 0))

    def body(w_ref, own_ref, *rest):
        o_ref = rest[-1]
        acc = None
        for k in range(s):
            term = jnp.where(w_ref[0] == k, own_ref[...], rest[k][...]).astype(F32)
            acc = term if acc is None else acc + term
        o_ref[...] = acc

    args = [where, pair] + [landed] * s
    in_specs = [pl.BlockSpec((None, rows, c_dim), lambda i, w: (w[0], i, 0))] + [landed_spec(k) for k in range(s)]
    aliases = {}
    if prev is not None:
        args.append(prev)
        in_specs.append(ANY_SPEC)
        aliases = {len(args) - 1: 0}
    return pl.pallas_call(
        body, name=name,
        grid_spec=pltpu.PrefetchScalarGridSpec(
            num_scalar_prefetch=1, grid=(per,), in_specs=in_specs,
            out_specs=pl.BlockSpec((None, rows, c_dim), lambda i, w: (layer, w[1] * per + i, 0))),
        out_shape=jax.ShapeDtypeStruct((DEPTH, 2 * half, c_dim), F32), input_output_aliases=aliases,
        compiler_params=_params(("arbitrary",), 40),
    )(*args)


def _half_gather(bufs, layer, name):
    n = len(bufs)

    def body(*refs):
        outs = refs[n:2 * n]
        send, recv = refs[2 * n:]
        x, y, c, _ = _place()
        sibling = (x, y, 1 - c)

        def rows(t, which):
            half = outs[t].shape[1] // 2
            return outs[t].at[layer, pl.ds(pl.multiple_of(which * half, 8), half)]

        cps = [_remote(rows(t, c), rows(t, c), send.at[t], recv.at[t], sibling) for t in range(n)]
        for cp in cps:
            cp.start()
        for t in range(n):
            _remote(rows(t, 1 - c), rows(t, 1 - c), send.at[t], recv.at[t], sibling).wait_recv()
        for cp in cps:
            cp.wait_send()

    return pl.pallas_call(
        body, name=name, in_specs=[HBM_SPEC] * n, out_specs=[HBM_SPEC] * n,
        out_shape=[jax.ShapeDtypeStruct(b.shape, b.dtype) for b in bufs],
        input_output_aliases={t: t for t in range(n)},
        scratch_shapes=[pltpu.SemaphoreType.DMA((n,))] * 2,
    )(*bufs)


WEIGHT_NAMES = ("ln_attn", "w_in", "sink_b", "rpb_c", "mix_gain", "w_out", "ln_ffn", "w_up", "conv_w", "conv_b",
                "w_down", "ln_final")
BIG_NAMES = ("w_in", "w_out", "w_up", "w_down")
REPLICATED_NAMES = ("ln_attn", "sink_b", "rpb_c", "mix_gain", "ln_ffn", "conv_b", "ln_final")
PACK_TILE = 8 * LANES


def _pack(arrays, row_multiple):
    pieces = []
    for a in arrays:
        flat = a.reshape(-1)
        pieces.append(jnp.pad(flat, (0, (-flat.shape[0]) % PACK_TILE)))
    flat = jnp.concatenate(pieces)
    flat = jnp.pad(flat, (0, (-flat.shape[0]) % (row_multiple * LANES)))
    return flat.reshape(-1, LANES)


def _unpack(packed, shapes):
    flat = packed.reshape(-1)
    out, off = [], 0
    for shape in shapes:
        size = math.prod(shape)
        out.append(flat[off:off + size].reshape(shape))
        off += size + (-size) % PACK_TILE
    return out


def kernel(x, ln_attn, w_in, sink_b, rpb_c, mix_gain, w_out, ln_ffn, w_up, conv_w, conv_b, w_down, ln_final, loss_target, m_ln_attn, m_w_in, m_sink_b, m_rpb_c, m_mix_gain, m_w_out, m_ln_ffn, m_w_up, m_conv_w, m_conv_b, m_w_down, m_ln_final, v_ln_attn, v_w_in, v_sink_b, v_rpb_c, v_mix_gain, v_w_out, v_ln_ffn, v_w_up, v_conv_w, v_conv_b, v_w_down, v_ln_final):
    w = dict(ln_attn=ln_attn, w_in=w_in, sink_b=sink_b, rpb_c=rpb_c, mix_gain=mix_gain, w_out=w_out, ln_ffn=ln_ffn,
             w_up=w_up, conv_w=conv_w, conv_b=conv_b, w_down=w_down, ln_final=ln_final)
    m = dict(ln_attn=m_ln_attn, w_in=m_w_in, sink_b=m_sink_b, rpb_c=m_rpb_c, mix_gain=m_mix_gain, w_out=m_w_out,
             ln_ffn=m_ln_ffn, w_up=m_w_up, conv_w=m_conv_w, conv_b=m_conv_b, w_down=m_w_down, ln_final=m_ln_final)
    v = dict(ln_attn=v_ln_attn, w_in=v_w_in, sink_b=v_sink_b, rpb_c=v_rpb_c, mix_gain=v_mix_gain, w_out=v_w_out,
             ln_ffn=v_ln_ffn, w_up=v_w_up, conv_w=v_conv_w, conv_b=v_conv_b, w_down=v_w_down, ln_final=v_ln_final)
    shard = _shard_index(lax.axis_index("x"), lax.axis_index("y"))
    up_cols = w_up.shape[2]

    conv_slots = _all_gather_small(_pack([conv_w], 8), "gather_conv_w")
    conv_all = conv_slots[0::2].reshape(N_SHARDS, -1)[:, :conv_w.size].reshape((N_SHARDS,) + conv_w.shape)

    arrivals = []
    group_of = {}
    tokens = []
    rest = ("w_out", "w_up", "w_down")
    for l, names in ((0, ("w_in",)), (0, rest), (1, ("w_in",)), (1, rest)):
        bufs = [_own_slot(w[k], l, shard, "own_" + k) for k in names]
        send, recv, bufs, token = _gather_start(bufs, tokens[-1:] or [conv_slots], "gather_start_%d" % len(arrivals))
        tokens.append(token)
        for k in names:
            group_of[l, k] = len(arrivals)
        arrivals.append({"names": names, "send": send, "recv": recv, "bufs": bufs, "done": None})

    def gathered(l, name, after):
        idx = group_of[l, name]
        group = arrivals[idx]
        if group["done"] is None:
            bufs = _gather_wait(group["send"], group["recv"], group["bufs"], list(after) + tokens[-1:],
                                "gather_wait_%d" % idx)
            group["done"] = dict(zip(group["names"], _gather_forward(bufs, "gather_forward_%d" % idx)))
        buf = group["done"][name]
        return buf.reshape(1, -1, buf.shape[2]) if name in ("w_out", "w_down") else buf

    cos, sin = _rope_tables(SEQ)
    tabs = {"cos": cos, "sin": sin, "bias_a": _bias_a(), "bias_b": _bias_b()}
    layers = []
    for l in range(DEPTH):
        conv_w_l = conv_all[:, l].reshape(2, N_SHARDS // 2, 3, up_cols).transpose(0, 2, 1, 3).reshape(2, 3, D_FF)
        layers.append({"ln_attn": ln_attn[l][None], "sink_b": sink_b[l], "bias_c": _bias_c(rpb_c[l]),
                       "mix_gain": mix_gain[l][None], "ln_ffn": ln_ffn[l][None], "conv_w": conv_w_l,
                       "conv_b": conv_b[l].reshape(2, 1, D_FF)})

    act = x[0]
    saved = []
    for l in range(DEPTH):
        act, keep = _layer_fwd(act, layers[l], lambda name, after, l=l: gathered(l, name, [after]), tabs)
        saved.append(keep)
    loss_part, dx, dx_b, d_ln_final = _loss_head(act, ln_final[None], loss_target[0], "loss_head")
    loss = lax.psum(loss_part[0, 0], ("x", "y", "c"))

    reductions = []

    opened = [0]

    def begin(l, partial):
        idx = opened[0]
        opened[0] += 1
        names = tuple(partial)
        send_sem, recv_sem, mine, theirs, token = _half_exchange_start([partial[k] for k in names],
                                                                       "half_exchange_start_%d" % idx)
        return {"idx": idx, "layer": l, "names": names, "send": send_sem, "recv": recv_sem, "mine": mine,
                "theirs": theirs}, token

    def finish(handle, after):
        idx, names = handle["idx"], handle["names"]
        mine, theirs = _half_exchange_wait(handle["send"], handle["recv"], handle["mine"], handle["theirs"], after,
                                           "half_exchange_wait_%d" % idx)
        pairs = [_half_sum(a, b, "half_sum_" + k) for k, a, b in zip(names, mine, theirs)]
        send_sem, recv_sem, pairs, lands, token = _reduce_start(pairs, "reduce_start_%d" % idx)
        reductions.append({"layer": handle["layer"], "names": names, "send": send_sem, "recv": recv_sem,
                           "pairs": pairs, "lands": lands})
        return token

    small = [None] * DEPTH
    pending = None
    for l in reversed(range(DEPTH)):
        big = {k: gathered(l, k, []) for k in BIG_NAMES}
        dx, dx_b, small[l], pending = _layer_bwd(dx, dx_b, saved[l], layers[l], big, tabs,
                                                 functools.partial(begin, l), finish, pending)
    after = [finish(pending[0], [pending[1]])]

    stacked = {k: jnp.stack([small[l][k] for l in range(DEPTH)]) for k in small[0]}
    part = {"ln_attn": stacked["ln_attn"][:, 0], "sink_b": stacked["sink_b"], "rpb_c": stacked["rpb_c"],
            "mix_gain": stacked["mix_gain"][:, 0], "ln_ffn": stacked["ln_ffn"][:, 0],
            "conv_b": stacked["conv_b"].reshape(DEPTH, 2 * D_FF), "ln_final": d_ln_final[0],
            "conv_w": stacked["conv_w"].transpose(0, 2, 1, 3).reshape(DEPTH, 3, 2 * D_FF)}
    small_names = REPLICATED_NAMES + ("conv_w",)
    small_send, small_recv, small_vec, small_slots, token = _small_start(
        _pack([part[k] for k in small_names], 256), after, "small_grads_start")
    after = [token]

    grads, delta, new_m, new_v = {}, {}, {}, {}
    reduced = {}
    updated = dict.fromkeys(BIG_NAMES)
    for l in reversed(range(DEPTH)):
        for idx, group in enumerate(reductions):
            if group["layer"] != l:
                continue
            pairs, lands = _reduce_wait(group["send"], group["recv"], group["pairs"], group["lands"], after,
                                        "reduce_wait_%d" % idx)
            for k, pair, landed in zip(group["names"], pairs, lands):
                reduced[k] = _reduce_sum(pair, landed, l, reduced.get(k), "reduce_sum_" + k)
            after = [reduced[group["names"][-1]]]
        reduced = dict(zip(BIG_NAMES, _half_gather([reduced[k] for k in BIG_NAMES], l, "half_gather_%d" % l)))
        for k in BIG_NAMES:
            updated[k] = _adamw_layer(w[k], reduced[k], m[k], v[k], l, updated[k], "adamw_" + k)
        after = [updated[k][0] for k in BIG_NAMES]
    for k in BIG_NAMES:
        grads[k], delta[k], new_m[k], new_v[k] = updated[k]

    small_vec, small_slots = _small_wait(small_send, small_recv, small_vec, small_slots, after, "small_grads_wait")
    total = _small_sum(small_vec, small_slots, "small_grads_sum")
    for k, g in zip(small_names, _unpack(total, [part[k].shape for k in small_names])):
        grads[k] = g
    grads["conv_w"] = lax.dynamic_slice_in_dim(grads["conv_w"], shard * up_cols, up_cols, axis=2)

    flat = (DEPTH * 3, up_cols)
    res = _adamw(conv_w.reshape(flat), grads["conv_w"].reshape(flat), m["conv_w"].reshape(flat),
                 v["conv_w"].reshape(flat), "adamw_conv_w")
    delta["conv_w"], new_m["conv_w"], new_v["conv_w"] = (r.reshape(conv_w.shape) for r in res)
    shapes = [w[k].shape for k in REPLICATED_NAMES]
    packed = [_pack([d[k] for k in REPLICATED_NAMES], 128) for d in (w, grads, m, v)]
    for d, res in zip((delta, new_m, new_v), _adamw(*packed, "adamw_small")):
        for k, r in zip(REPLICATED_NAMES, _unpack(res, shapes)):
            d[k] = r

    return (loss, dx[None], *[grads[k] for k in WEIGHT_NAMES], *[delta[k] for k in WEIGHT_NAMES],
            *[new_m[k] for k in WEIGHT_NAMES], *[new_v[k] for k in WEIGHT_NAMES])
```

```python
import functools
import math

import jax
import jax.numpy as jnp
from jax import lax
from jax.experimental import pallas as pl
from jax.experimental.pallas import tpu as pltpu

F32 = jnp.float32
BF16 = jnp.bfloat16
MESH = pl.DeviceIdType.MESH

D_MODEL = 2048
SEQ = 2048
DEPTH = 2
HEAD_DIM = 64
N_HEADS_A = 12
N_HEADS_B = 10
N_KV_B = 2
N_HEADS_C = 10
WINDOW_B = 128
GRID_W = 64
NA_ROWS = 8
NA_COLS = 16
WIDTH_A = N_HEADS_A * HEAD_DIM
WIDTH_B = N_HEADS_B * HEAD_DIM
WIDTH_C = N_HEADS_C * HEAD_DIM
IN_COLS = 5120
D_FF = 5632
ROPE_THETA = 10000.0
EPS = 1e-6
NEG_INF = -1e30
N_SHARDS = 4

ADAM_LR = 0.001
ADAM_B1 = 0.9
ADAM_B2 = 0.999
ADAM_EPS = 1e-08
ADAM_WD = 0.01
ADAM_STEP = 10

LANES = 128
QB = 256
NQB = SEQ // QB
ROWS = 256
MIB = 2 ** 20

A_BLK = (0, 6, 12)
B_BLK = (18, 23, 24)
C_BLK = (25, 30, 35)
ROPE_BLKS = tuple(range(0, 12)) + tuple(range(18, 24))
QSCALE_BLKS = tuple(range(0, 6)) + tuple(range(18, 23)) + tuple(range(25, 30))
N_PBLK = IN_COLS // LANES


def _params(sem, vmem_mib):
    return pltpu.CompilerParams(dimension_semantics=sem, vmem_limit_bytes=vmem_mib * MIB)


def _weight_spec(w, cols, t_in, t_out, transposed):
    s, r, c = w.shape
    if cols:
        per = c // t_out
        k_dim, n = r, s * c
        if transposed:
            index = lambda j, rr: (rr // per, j, rr % per)
        else:
            index = lambda j, kk: (j // per, kk, j % per)
    else:
        per = r // t_in
        k_dim, n = s * r, c
        if transposed:
            index = lambda j, rr: (j // per, j % per, rr)
        else:
            index = lambda j, kk: (kk // per, kk % per, j)
    return pl.BlockSpec((None, t_in, t_out), index), k_dim, n


def _mm_nn(a, w, *, cols, tn, tk, out_dtype, name, residual=None, out_split=1):
    m, k_dim = a.shape
    w_spec, k_w, n = _weight_spec(w, cols, tk, tn, False)
    assert k_w == k_dim
    nj, nk = n // tn, k_dim // tk
    in_specs = [pl.BlockSpec((m, tk), lambda j, k: (0, k)), w_spec]
    args = [a, w]
    if residual is not None:
        in_specs.append(pl.BlockSpec((m, tn), lambda j, k: (0, j)))
        args.append(residual)
    if out_split > 1:
        per_o = n // out_split // tn
        out_spec = pl.BlockSpec((None, m, tn), lambda j, k: (j // per_o, 0, j % per_o))
        out_shape = pltpu.HBM((out_split, m, n // out_split), out_dtype)
    else:
        out_spec = pl.BlockSpec((m, tn), lambda j, k: (0, j))
        out_shape = pltpu.HBM((m, n), out_dtype)

    def body(*refs):
        a_ref, w_ref = refs[0], refs[1]
        r_ref = refs[2] if residual is not None else None
        o_ref = refs[3] if residual is not None else refs[2]

        def finish(val):
            if r_ref is not None:
                val = r_ref[...] + val
            o_ref[...] = val.astype(o_ref.dtype)

        part = jnp.dot(a_ref[...], w_ref[...], preferred_element_type=F32)
        if nk == 1:
            finish(part)
        else:
            acc = refs[-1]
            kk = pl.program_id(1)

            @pl.when(kk == 0)
            def _():
                acc[...] = part

            @pl.when(kk > 0)
            def _():
                acc[...] += part

            @pl.when(kk == nk - 1)
            def _():
                finish(acc[...])

    return pl.pallas_call(
        body, name=name, grid=(nj, nk), in_specs=in_specs, out_specs=out_spec, out_shape=out_shape,
        scratch_shapes=[pltpu.VMEM((m, tn), F32)] if nk > 1 else [],
        compiler_params=_params(("arbitrary", "arbitrary"), 56),
    )(*[_in_hbm(a) for a in args])


ANY_SPEC = pl.BlockSpec(memory_space=pl.ANY)


def _mm_nt(dy, w, *, cols, to, tr, out_dtype, name, after=()):
    if dy.ndim == 3:
        m = dy.shape[1]
        n = dy.shape[0] * dy.shape[2]
        per_d = dy.shape[2] // tr
        dy_spec = pl.BlockSpec((None, m, tr), lambda j, r: (r // per_d, 0, r % per_d))
    else:
        m, n = dy.shape
        dy_spec = pl.BlockSpec((m, tr), lambda j, r: (0, r))
    w_spec, k_dim, n_w = _weight_spec(w, cols, to, tr, True)
    assert n_w == n
    nj, nr = k_dim // to, n // tr

    n_after = len(after)

    def body(dy_ref, w_ref, *rest):
        o_ref = rest[n_after]
        part = lax.dot_general(dy_ref[...], w_ref[...], (((1,), (1,)), ((), ())), preferred_element_type=F32)
        if nr == 1:
            o_ref[...] = part.astype(o_ref.dtype)
        else:
            acc = rest[n_after + 1]
            rr = pl.program_id(1)

            @pl.when(rr == 0)
            def _():
                acc[...] = part

            @pl.when(rr > 0)
            def _():
                acc[...] += part

            @pl.when(rr == nr - 1)
            def _():
                o_ref[...] = acc[...].astype(o_ref.dtype)

    return pl.pallas_call(
        body, name=name, grid=(nj, nr), in_specs=[dy_spec, w_spec] + [ANY_SPEC] * n_after,
        out_specs=pl.BlockSpec((m, to), lambda j, r: (0, j)),
        out_shape=pltpu.HBM((m, k_dim), out_dtype),
        scratch_shapes=[pltpu.VMEM((m, to), F32)] if nr > 1 else [],
        compiler_params=_params(("arbitrary", "arbitrary"), 56),
    )(_in_hbm(dy), _in_hbm(w), *after)


def _mm_tn(x, dy, *, tk, tn, shards, name):
    m, k_dim = x.shape
    if dy.ndim == 3:
        n = dy.shape[0] * dy.shape[2]
        per_d = dy.shape[2] // tn
        dy_spec = pl.BlockSpec((None, m, tn), lambda i, j: (j // per_d, 0, j % per_d))
    else:
        n = dy.shape[1]
        dy_spec = pl.BlockSpec((m, tn), lambda i, j: (0, j))
    if shards > 0:
        per = n // shards // tn
        out_shape = pltpu.HBM((shards, k_dim, n // shards), BF16)
        out_spec = pl.BlockSpec((None, tk, tn), lambda i, j: (j // per, i, j % per))
    else:
        s = -shards
        per = k_dim // s // tk
        out_shape = pltpu.HBM((s, k_dim // s, n), BF16)
        out_spec = pl.BlockSpec((None, tk, tn), lambda i, j: (i // per, i % per, j))

    def body(x_ref, dy_ref, o_ref):
        o_ref[...] = lax.dot_general(x_ref[...], dy_ref[...], (((0,), (0,)), ((), ())),
                                     preferred_element_type=F32).astype(BF16)

    return pl.pallas_call(
        body, name=name, grid=(k_dim // tk, n // tn),
        in_specs=[pl.BlockSpec((m, tk), lambda i, j: (0, i)), dy_spec], out_specs=out_spec, out_shape=out_shape,
        compiler_params=_params(("arbitrary", "arbitrary"), 56),
    )(_in_hbm(x), _in_hbm(dy))


def _row_spec(width, rows=ROWS):
    return pl.BlockSpec((rows, width), lambda i: (i, 0))


def _vec_spec(width):
    return pl.BlockSpec((1, width), lambda i: (0, 0))


def _rms_stats(x):
    r = lax.rsqrt(jnp.mean(x * x, axis=-1, keepdims=True) + EPS)
    return r, x * r


def _rmsnorm_fwd(x, gain, name):
    t, d = x.shape

    def body(x_ref, g_ref, o_ref):
        _, n = _rms_stats(x_ref[...])
        o_ref[...] = (n * g_ref[...]).astype(BF16)

    return pl.pallas_call(
        body, name=name, grid=(t // ROWS,), in_specs=[_row_spec(d), _vec_spec(d)], out_specs=_row_spec(d),
        out_shape=pltpu.HBM((t, d), BF16), compiler_params=_params(("arbitrary",), 32),
    )(_in_hbm(x), _in_hbm(gain))


def _rmsnorm_bwd(x, gain, dh, dres, name, after=()):
    t, d = x.shape
    n_after = len(after)

    def body(x_ref, g_ref, dh_ref, dres_ref, *rest):
        dx_ref, dxb_ref, dg_ref = rest[n_after:]
        r, n = _rms_stats(x_ref[...])
        dh_v = dh_ref[...]
        dn = dh_v * g_ref[...]
        dx = dres_ref[...] + r * (dn - n * jnp.mean(dn * n, axis=-1, keepdims=True))
        dx_ref[...] = dx
        dxb_ref[...] = dx.astype(BF16)
        part = jnp.sum(dh_v * n, axis=0, keepdims=True)

        @pl.when(pl.program_id(0) == 0)
        def _():
            dg_ref[...] = part

        @pl.when(pl.program_id(0) > 0)
        def _():
            dg_ref[...] += part

    return pl.pallas_call(
        body, name=name, grid=(t // ROWS,),
        in_specs=[_row_spec(d), _vec_spec(d), _row_spec(d), _row_spec(d)] + [ANY_SPEC] * n_after,
        out_specs=[_row_spec(d), _row_spec(d), _vec_spec(d)],
        out_shape=[pltpu.HBM((t, d), F32), pltpu.HBM((t, d), BF16), jax.ShapeDtypeStruct((1, d), F32)],
        compiler_params=_params(("arbitrary",), 40),
    )(_in_hbm(x), _in_hbm(gain), _in_hbm(dh), _in_hbm(dres), *after)


def _loss_head(x, gain, target, name):
    t, d = x.shape

    def body(x_ref, g_ref, t_ref, loss_ref, dx_ref, dxb_ref, dg_ref):
        r, n = _rms_stats(x_ref[...])
        g = g_ref[...]
        err = n * g - t_ref[...]
        dy = err * (1.0 / d)
        dn = dy * g
        dx = r * (dn - n * jnp.mean(dn * n, axis=-1, keepdims=True))
        dx_ref[...] = dx
        dxb_ref[...] = dx.astype(BF16)
        part = jnp.sum(dy * n, axis=0, keepdims=True)
        lpart = jnp.zeros((8, LANES), F32) + 0.5 * jnp.sum(jnp.mean(err * err, axis=-1, keepdims=True))

        @pl.when(pl.program_id(0) == 0)
        def _():
            dg_ref[...] = part
            loss_ref[...] = lpart

        @pl.when(pl.program_id(0) > 0)
        def _():
            dg_ref[...] += part
            loss_ref[...] += lpart

    return pl.pallas_call(
        body, name=name, grid=(t // ROWS,),
        in_specs=[_row_spec(d), _vec_spec(d), _row_spec(d)],
        out_specs=[pl.BlockSpec((8, LANES), lambda i: (0, 0)), _row_spec(d), _row_spec(d), _vec_spec(d)],
        out_shape=[jax.ShapeDtypeStruct((8, LANES), F32), pltpu.HBM((t, d), F32), pltpu.HBM((t, d), BF16),
                   jax.ShapeDtypeStruct((1, d), F32)],
        compiler_params=_params(("arbitrary",), 40),
    )(x, gain, target)


def _swap_halves(x):
    lane = lax.broadcasted_iota(jnp.int32, x.shape, 1)
    return jnp.where((lane % HEAD_DIM) < HEAD_DIM // 2, pltpu.roll(x, LANES - HEAD_DIM // 2, 1),
                     pltpu.roll(x, HEAD_DIM // 2, 1))


def _rope_tables(t):
    inv_freq = ROPE_THETA ** (-jnp.arange(0, HEAD_DIM, 2, dtype=F32) / HEAD_DIM)
    ang = jnp.arange(t, dtype=F32)[:, None] * inv_freq[None, :]
    cos = jnp.tile(jnp.cos(ang), (1, LANES // (HEAD_DIM // 2)))
    sin = jnp.tile(jnp.sin(ang), (1, LANES // (HEAD_DIM // 2)))
    lane = jnp.arange(LANES)[None, :]
    return cos, jnp.where((lane % HEAD_DIM) < HEAD_DIM // 2, -sin, sin)


def _rope_fwd(proj, cos, sin, name):
    t = proj.shape[0]
    scale = HEAD_DIM ** -0.5

    def body(p_ref, c_ref, s_ref, o_ref):
        cos_v, sin_v = c_ref[...], s_ref[...]
        for b in range(N_PBLK):
            cols = slice(b * LANES, (b + 1) * LANES)
            v = p_ref[:, cols]
            if b in ROPE_BLKS:
                v = v * cos_v + _swap_halves(v) * sin_v
            if b in QSCALE_BLKS:
                v = v * scale
            o_ref[:, cols] = v.astype(BF16)

    return pl.pallas_call(
        body, name=name, grid=(t // ROWS,),
        in_specs=[_row_spec(IN_COLS), _row_spec(LANES), _row_spec(LANES)], out_specs=_row_spec(IN_COLS),
        out_shape=pltpu.HBM((t, IN_COLS), BF16), compiler_params=_params(("arbitrary",), 40),
    )(_in_hbm(proj), _in_hbm(cos), _in_hbm(sin))


def _rope_bwd(grads, cos, sin, name):
    t = grads[0].shape[0]
    scale = HEAD_DIM ** -0.5
    group = N_HEADS_B // N_KV_B

    def body(*refs):
        c_ref, s_ref, o_ref = refs[9], refs[10], refs[11]
        cos_v, sin_v = c_ref[...], s_ref[...]

        def kv_sum(ref):
            parts = []
            for g in range(N_KV_B):
                acc = ref[:, g * group * HEAD_DIM:(g * group + 1) * HEAD_DIM]
                for h in range(g * group + 1, (g + 1) * group):
                    acc = acc + ref[:, h * HEAD_DIM:(h + 1) * HEAD_DIM]
                parts.append(acc)
            return jnp.concatenate(parts, axis=1)

        def emit(b, v):
            if b in ROPE_BLKS:
                v = v * cos_v - _swap_halves(v) * sin_v
            if b in QSCALE_BLKS:
                v = v * scale
            o_ref[:, b * LANES:(b + 1) * LANES] = v.astype(BF16)

        starts = (A_BLK[0], A_BLK[1], A_BLK[2], B_BLK[0], None, None, C_BLK[0], C_BLK[1], C_BLK[2])
        for idx, start in enumerate(starts):
            if start is None:
                continue
            for j in range(refs[idx].shape[1] // LANES):
                emit(start + j, refs[idx][:, j * LANES:(j + 1) * LANES])
        emit(B_BLK[1], kv_sum(refs[4]))
        emit(B_BLK[2], kv_sum(refs[5]))

    return pl.pallas_call(
        body, name=name, grid=(t // ROWS,),
        in_specs=[_row_spec(g.shape[1]) for g in grads] + [_row_spec(LANES), _row_spec(LANES)],
        out_specs=_row_spec(IN_COLS),
        out_shape=pltpu.HBM((t, IN_COLS), BF16), compiler_params=_params(("arbitrary",), 40),
    )(*[_in_hbm(g) for g in grads], _in_hbm(cos), _in_hbm(sin))


GROUP_COLS = ((0, WIDTH_A), (WIDTH_A, WIDTH_A + WIDTH_B), (WIDTH_A + WIDTH_B, D_MODEL))


def _mix_fwd(oa, ob, oc, gain, name):
    t = oa.shape[0]

    def body(a_ref, b_ref, c_ref, g_ref, o_ref):
        for ref, (lo, hi) in zip((a_ref, b_ref, c_ref), GROUP_COLS):
            _, n = _rms_stats(ref[...])
            o_ref[:, lo:hi] = (n * g_ref[:, lo:hi]).astype(BF16)

    return pl.pallas_call(
        body, name=name, grid=(t // ROWS,),
        in_specs=[_row_spec(WIDTH_A), _row_spec(WIDTH_B), _row_spec(WIDTH_C), _vec_spec(D_MODEL)],
        out_specs=_row_spec(D_MODEL),
        out_shape=pltpu.HBM((t, D_MODEL), BF16), compiler_params=_params(("arbitrary",), 32),
    )(_in_hbm(oa), _in_hbm(ob), _in_hbm(oc), _in_hbm(gain))


def _mix_bwd(oa, ob, oc, gain, dmixed, name, after=()):
    t = oa.shape[0]
    n_after = len(after)

    def body(a_ref, b_ref, c_ref, g_ref, dm_ref, *rest):
        da_ref, db_ref, dc_ref, dg_ref = rest[n_after:]
        first = pl.program_id(0) == 0
        for ref, dref, (lo, hi) in zip((a_ref, b_ref, c_ref), (da_ref, db_ref, dc_ref), GROUP_COLS):
            r, n = _rms_stats(ref[...])
            dm = dm_ref[:, lo:hi]
            dn = dm * g_ref[:, lo:hi]
            dref[...] = r * (dn - n * jnp.mean(dn * n, axis=-1, keepdims=True))
            part = jnp.sum(dm * n, axis=0, keepdims=True)

            @pl.when(first)
            def _():
                dg_ref[:, lo:hi] = part

            @pl.when(jnp.logical_not(first))
            def _():
                dg_ref[:, lo:hi] += part

    return pl.pallas_call(
        body, name=name, grid=(t // ROWS,),
        in_specs=[_row_spec(WIDTH_A), _row_spec(WIDTH_B), _row_spec(WIDTH_C), _vec_spec(D_MODEL), _row_spec(D_MODEL)]
        + [ANY_SPEC] * n_after,
        out_specs=[_row_spec(WIDTH_A), _row_spec(WIDTH_B), _row_spec(WIDTH_C), _vec_spec(D_MODEL)],
        out_shape=[pltpu.HBM((t, WIDTH_A), F32), pltpu.HBM((t, WIDTH_B), F32), pltpu.HBM((t, WIDTH_C), F32),
                   jax.ShapeDtypeStruct((1, D_MODEL), F32)],
        compiler_params=_params(("arbitrary",), 40),
    )(_in_hbm(oa), _in_hbm(ob), _in_hbm(oc), _in_hbm(gain), _in_hbm(dmixed), *after)


FF_COLS = 256


SUBLANES = 8
CHUNK = 128
HALO = SUBLANES


def _pad_rows(dst_ref, src_ref):
    t, cols = src_ref.shape
    dst_ref[0:HALO, :] = jnp.zeros((HALO, cols), F32)
    dst_ref[HALO:HALO + t, :] = src_ref[...]
    dst_ref[HALO + t:t + 2 * HALO, :] = jnp.zeros((HALO, cols), F32)


def _roll_rows(x, by):
    return pltpu.roll(x, by % x.shape[0], 0)


def _gate_val(pad_ref, r0, w_ref, b_ref):
    ext = [pad_ref[h, pl.ds(r0, CHUNK + 2 * HALO), :] for h in range(2)]
    before = [_roll_rows(e, 1) for e in ext]
    after = [_roll_rows(e, -1) for e in ext]
    gate, val = ((before[h] * w_ref[h, 0:1, :] + ext[h] * w_ref[h, 1:2, :]) + after[h] * w_ref[h, 2:3, :] + b_ref[h]
                 for h in range(2))
    return gate, val, ext, before, after


def _ff_specs(t):
    u_spec = pl.BlockSpec((2, t, FF_COLS), lambda j: (0, 0, j))
    w_spec = pl.BlockSpec((2, 3, FF_COLS), lambda j: (0, 0, j))
    b_spec = pl.BlockSpec((2, 1, FF_COLS), lambda j: (0, 0, j))
    return u_spec, w_spec, b_spec


def _convgate_fwd(u0, conv_w, conv_b, name):
    t = u0.shape[1]
    u_spec, w_spec, b_spec = _ff_specs(t)

    def body(u_ref, w_ref, b_ref, o_ref, pad_ref):
        for h in range(2):
            _pad_rows(pad_ref.at[h], u_ref.at[h])

        def chunk(ci, carry):
            r0 = pl.multiple_of(ci * CHUNK, CHUNK)
            gate, val, _, _, _ = _gate_val(pad_ref, r0, w_ref, b_ref)
            act = gate * jax.nn.sigmoid(gate) * val
            o_ref[pl.ds(r0, CHUNK), :] = act[HALO:HALO + CHUNK].astype(BF16)
            return carry

        lax.fori_loop(0, t // CHUNK, chunk, 0)

    return pl.pallas_call(
        body, name=name, grid=(D_FF // FF_COLS,), in_specs=[u_spec, w_spec, b_spec],
        out_specs=pl.BlockSpec((t, FF_COLS), lambda j: (0, j)),
        out_shape=pltpu.HBM((t, D_FF), BF16),
        scratch_shapes=[pltpu.VMEM((2, t + 2 * HALO, FF_COLS), F32)],
        compiler_params=_params(("arbitrary",), 48),
    )(_in_hbm(u0), conv_w, conv_b)


def _convgate_bwd(u0, conv_w, conv_b, d_act, name):
    t = u0.shape[1]
    u_spec, w_spec, b_spec = _ff_specs(t)

    def body(u_ref, w_ref, b_ref, da_ref, du_ref, dw_ref, db_ref, pad_ref, da_pad_ref, sums_ref):
        for h in range(2):
            _pad_rows(pad_ref.at[h], u_ref.at[h])
        _pad_rows(da_pad_ref, da_ref)
        sums_ref[...] = jnp.zeros_like(sums_ref)
        inner = slice(HALO, HALO + CHUNK)

        def fold(x):
            return jnp.sum(x.reshape(CHUNK // SUBLANES, SUBLANES, x.shape[1]), axis=0)

        def chunk(ci, carry):
            r0 = pl.multiple_of(ci * CHUNK, CHUNK)
            gate, val, ext, before, after = _gate_val(pad_ref, r0, w_ref, b_ref)
            sig = jax.nn.sigmoid(gate)
            da = da_pad_ref[pl.ds(r0, CHUNK + 2 * HALO), :]
            d_half = (da * val * (sig * (1.0 + gate * (1.0 - sig))), da * (gate * sig))
            for h in range(2):
                du = d_half[h]
                for k, term in enumerate((du, du * before[h], du * ext[h], du * after[h])):
                    sums_ref[h, k] += fold(term[inner])
                du0 = (_roll_rows(du, -1) * w_ref[h, 0:1, :] + du * w_ref[h, 1:2, :]) + _roll_rows(du, 1) * w_ref[h, 2:3, :]
                du_ref[h, pl.ds(r0, CHUNK), :] = du0[inner].astype(BF16)
            return carry

        lax.fori_loop(0, t // CHUNK, chunk, 0)
        for h in range(2):
            db_ref[h] = jnp.sum(sums_ref[h, 0], axis=0, keepdims=True)
            for k in range(3):
                dw_ref[h, k:k + 1, :] = jnp.sum(sums_ref[h, k + 1], axis=0, keepdims=True)

    return pl.pallas_call(
        body, name=name, grid=(D_FF // FF_COLS,),
        in_specs=[u_spec, w_spec, b_spec, pl.BlockSpec((t, FF_COLS), lambda j: (0, j))],
        out_specs=[u_spec, w_spec, b_spec],
        out_shape=[pltpu.HBM((2, t, D_FF), BF16), jax.ShapeDtypeStruct((2, 3, D_FF), F32),
                   jax.ShapeDtypeStruct((2, 1, D_FF), F32)],
        scratch_shapes=[pltpu.VMEM((2, t + 2 * HALO, FF_COLS), F32), pltpu.VMEM((t + 2 * HALO, FF_COLS), F32),
                        pltpu.VMEM((2, 4, SUBLANES, FF_COLS), F32)],
        compiler_params=_params(("arbitrary",), 56),
    )(_in_hbm(u0), conv_w, conv_b, _in_hbm(d_act))


class _Group:
    def __init__(self, heads, blks, kv_rows, n_win, gqa, bias_per_head):
        self.heads = heads
        self.pairs = heads // 2
        self.q_blk, self.k_blk, self.v_blk = blks
        self.kv_rows = kv_rows
        self.n_win = n_win
        self.full = kv_rows == SEQ
        self.gqa = gqa
        self.bias_per_head = bias_per_head
        self.width = heads * HEAD_DIM
        self.keys = kv_rows * n_win


GROUP_A = _Group(N_HEADS_A, A_BLK, SEQ, 1, False, False)
GROUP_B = _Group(N_HEADS_B, B_BLK, WINDOW_B, 4, True, False)
GROUP_C = _Group(N_HEADS_C, C_BLK, QB, 3, False, True)


def _win_start(grp, i):
    return jnp.clip(i * (QB // grp.kv_rows) - 1, 0, SEQ // grp.kv_rows - grp.n_win)


def _win_variant(i):
    return jnp.minimum(i, 1) + (i == NQB - 1).astype(jnp.int32)


def _attn_in_specs(grp, t):
    q_spec = pl.BlockSpec((QB, LANES), lambda p, i: (i, grp.q_blk + p))

    def col(blk):
        return (lambda p: blk) if grp.gqa else (lambda p: blk + p)

    def kv_specs(blk):
        c = col(blk)
        if grp.full:
            return [pl.BlockSpec((t, LANES), lambda p, i: (0, c(p)))]
        return [pl.BlockSpec((grp.kv_rows, LANES),
                             functools.partial(lambda p, i, w: (_win_start(grp, i) + w, c(p)), w=w))
                for w in range(grp.n_win)]

    nwk = grp.keys
    if grp.bias_per_head:
        bias_spec = pl.BlockSpec((2, None, QB, nwk), lambda p, i: (p, _win_variant(i), 0, 0))
    elif grp.full:
        bias_spec = pl.BlockSpec((1, None, QB, nwk), lambda p, i: (0, i, 0, 0))
    else:
        bias_spec = pl.BlockSpec((1, None, QB, nwk), lambda p, i: (0, _win_variant(i), 0, 0))
    sink_spec = pl.BlockSpec((1, LANES), lambda p, i: (0, p))
    return q_spec, kv_specs(grp.k_blk), kv_specs(grp.v_blk), bias_spec, sink_spec


def _head_kv(grp, whole, e, p):
    lo, hi = whole[:, :HEAD_DIM], whole[:, HEAD_DIM:]
    if grp.gqa:
        return jnp.where(2 * p + e >= N_HEADS_B // N_KV_B, hi, lo)
    return hi if e else lo


def _softmax_parts(q, k, bias, sink):
    s = lax.dot_general(q, k, (((1,), (1,)), ((), ())), preferred_element_type=F32) + bias
    m = jnp.maximum(jnp.max(s, axis=-1, keepdims=True), sink)
    pe = jnp.exp(s - m)
    denom = jnp.sum(pe, axis=-1, keepdims=True) + jnp.exp(sink - m)
    return pe, m, 1.0 / denom


def _attn_fwd(grp, proj, bias, sink, name):
    t = proj.shape[0]
    q_spec, k_specs, v_specs, bias_spec, sink_spec = _attn_in_specs(grp, t)
    nkv = len(k_specs)

    def body(*refs):
        q_ref = refs[0]
        k_refs, v_refs = refs[1:1 + nkv], refs[1 + nkv:1 + 2 * nkv]
        bias_ref, sink_ref, o_ref = refs[1 + 2 * nkv:4 + 2 * nkv]
        p = pl.program_id(0)
        k_all = jnp.concatenate([r[...] for r in k_refs], axis=0)
        v_all = jnp.concatenate([r[...] for r in v_refs], axis=0)
        outs = []
        for e in range(2):
            q = q_ref[:, e * HEAD_DIM:(e + 1) * HEAD_DIM]
            k = _head_kv(grp, k_all, e, p)
            v = _head_kv(grp, v_all, e, p)
            snk = sink_ref[0:1, e * HEAD_DIM:e * HEAD_DIM + 1]
            pe, _, inv = _softmax_parts(q, k, bias_ref[e if grp.bias_per_head else 0], snk)
            outs.append(jnp.dot(pe.astype(BF16), v, preferred_element_type=F32) * inv)
        o_ref[...] = jnp.concatenate(outs, axis=1)

    return pl.pallas_call(
        body, name=name, grid=(grp.pairs, NQB),
        in_specs=[q_spec, *k_specs, *v_specs, bias_spec, sink_spec],
        out_specs=pl.BlockSpec((QB, LANES), lambda p, i: (i, p)),
        out_shape=pltpu.HBM((t, grp.width), F32),
        compiler_params=_params(("arbitrary", "arbitrary"), 48),
    )(*([_in_hbm(proj)] * (1 + 2 * nkv)), _in_hbm(bias), sink)


def _attn_bwd(grp, proj, bias, sink, out, d_out, name):
    t = proj.shape[0]
    q_spec, k_specs, v_specs, bias_spec, sink_spec = _attn_in_specs(grp, t)
    nkv = len(k_specs)
    n_off = 2 * NA_ROWS - 1
    rows_q = QB // GRID_W
    wide = grp.keys > 2 * QB
    o_spec = pl.BlockSpec((QB, LANES), lambda p, i: (i, p))
    acc_spec = pl.BlockSpec((t, LANES), lambda p, i: (0, p))
    out_specs = [o_spec, acc_spec, acc_spec, pl.BlockSpec((None, 8, LANES), lambda p, i: (p, 0, 0))]
    out_shape = [pltpu.HBM((t, grp.width), F32)] * 3 + [jax.ShapeDtypeStruct((grp.pairs, 8, LANES), F32)]
    if grp.bias_per_head:
        out_specs.append(pl.BlockSpec((2, n_off, GRID_W, GRID_W), lambda p, i: (p, 0, 0, 0)))
        out_shape.append(jax.ShapeDtypeStruct((grp.heads, n_off, GRID_W, GRID_W), F32))

    def body(*refs):
        q_ref = refs[0]
        k_refs, v_refs = refs[1:1 + nkv], refs[1 + nkv:1 + 2 * nkv]
        bias_ref, sink_ref, o_ref, do_ref = refs[1 + 2 * nkv:5 + 2 * nkv]
        dq_ref, dk_ref, dv_ref, dsink_ref = refs[5 + 2 * nkv:9 + 2 * nkv]
        dbias_ref = refs[9 + 2 * nkv] if grp.bias_per_head else None
        p, i = pl.program_id(0), pl.program_id(1)

        @pl.when(i == 0)
        def _():
            dk_ref[...] = jnp.zeros_like(dk_ref)
            dv_ref[...] = jnp.zeros_like(dv_ref)
            dsink_ref[...] = jnp.zeros_like(dsink_ref)
            if dbias_ref is not None:
                dbias_ref[...] = jnp.zeros_like(dbias_ref)

        k_all = jnp.concatenate([r[...] for r in k_refs], axis=0)
        v_all = jnp.concatenate([r[...] for r in v_refs], axis=0)
        start = 0 if grp.full else _win_start(grp, i)
        dqs, dks, dvs, dsinks = [], [], [], []
        for e in range(2):
            cols = slice(e * HEAD_DIM, (e + 1) * HEAD_DIM)
            q = q_ref[:, cols]
            k = _head_kv(grp, k_all, e, p)
            v = _head_kv(grp, v_all, e, p)
            snk = sink_ref[0:1, e * HEAD_DIM:e * HEAD_DIM + 1]
            pe, m, inv = _softmax_parts(q, k, bias_ref[e if grp.bias_per_head else 0], snk)
            prob = pe * inv
            do = do_ref[:, cols]
            do_b = do.astype(BF16)
            delta = jnp.sum(do * o_ref[:, cols], axis=-1, keepdims=True)
            dp = lax.dot_general(do_b, v, (((1,), (1,)), ((), ())), preferred_element_type=F32)
            ds = prob * (dp - delta)
            ds_b = ds.astype(BF16)
            dqs.append(jnp.dot(ds_b, k, preferred_element_type=F32))
            if wide:
                dks.append(lax.dot_general(q, ds_b, (((0,), (0,)), ((), ())), preferred_element_type=F32))
                dvs.append(lax.dot_general(do_b, prob.astype(BF16), (((0,), (0,)), ((), ())),
                                           preferred_element_type=F32))
            else:
                dks.append(lax.dot_general(ds_b, q, (((0,), (0,)), ((), ())), preferred_element_type=F32))
                dvs.append(lax.dot_general(prob.astype(BF16), do_b, (((0,), (0,)), ((), ())),
                                           preferred_element_type=F32))
            dsinks.append(-jnp.sum(jnp.exp(snk - m) * inv * delta, axis=0, keepdims=True))
            if dbias_ref is not None:
                shift = (i * QB - start * grp.kv_rows) // GRID_W
                for rq in range(rows_q):
                    for rk in range(grp.keys // GRID_W):
                        off = jnp.clip(rk - rq + (NA_ROWS - 1) - shift, 0, n_off - 1)
                        dbias_ref[e, off] += ds[rq * GRID_W:(rq + 1) * GRID_W, rk * GRID_W:(rk + 1) * GRID_W]
        dq_ref[...] = jnp.concatenate(dqs, axis=1)
        rows = pl.ds(0, t) if grp.full else pl.ds(pl.multiple_of(start * grp.kv_rows, grp.kv_rows), grp.keys)
        if wide:
            dk_ref[rows, :] += jnp.concatenate(dks, axis=0).T
            dv_ref[rows, :] += jnp.concatenate(dvs, axis=0).T
        else:
            dk_ref[rows, :] += jnp.concatenate(dks, axis=1)
            dv_ref[rows, :] += jnp.concatenate(dvs, axis=1)
        lane = lax.broadcasted_iota(jnp.int32, (8, LANES), 1)
        dsink_ref[...] += jnp.where(lane < HEAD_DIM, dsinks[0], dsinks[1])

    return pl.pallas_call(
        body, name=name, grid=(grp.pairs, NQB),
        in_specs=[q_spec, *k_specs, *v_specs, bias_spec, sink_spec, o_spec, o_spec],
        out_specs=out_specs, out_shape=out_shape,
        compiler_params=_params(("arbitrary", "arbitrary"), 56),
    )(*([_in_hbm(proj)] * (1 + 2 * nkv)), _in_hbm(bias), sink, _in_hbm(out), _in_hbm(d_out))


DILATED_CONFIGS = ((128, 1), (512, 4), (2048, 16))


def _bias_a():
    d = jnp.arange(SEQ)[None, :] - jnp.arange(SEQ)[:, None]
    mult = jnp.zeros((SEQ, SEQ), F32)
    for window, r in DILATED_CONFIGS:
        reach = (window // (2 * r)) * r
        mult = mult + ((d % r == 0) & (jnp.abs(d) <= reach)).astype(F32)
    return jnp.where(mult > 0, jnp.log(jnp.maximum(mult, 1.0)), NEG_INF).reshape(1, NQB, QB, SEQ)


def _bias_b():
    row = jnp.arange(QB)[None, :, None]
    col = jnp.arange(GROUP_B.keys)[None, None, :]
    var = jnp.arange(3)[:, None, None]
    d = col - (GROUP_B.kv_rows * var + row)
    return jnp.where(jnp.abs(d) <= WINDOW_B, 0.0, NEG_INF).astype(F32)[None]


def _offset_onehot():
    c = jnp.arange(GRID_W)[:, None, None]
    c2 = jnp.arange(GRID_W)[None, :, None]
    b = jnp.arange(LANES)[None, None, :]
    return (c2 - c + NA_COLS - 1 == b).astype(BF16).reshape(GRID_W * GRID_W, LANES)


def _split_dot(x, g):
    hi = x.astype(BF16)
    rest = x - hi.astype(F32)
    mid = rest.astype(BF16)
    lo = (rest - mid.astype(F32)).astype(BF16)
    return (jnp.dot(hi, g, preferred_element_type=F32) + jnp.dot(mid, g, preferred_element_type=F32)
            + jnp.dot(lo, g, preferred_element_type=F32))


def _table_mm(x, g, name):
    def body(x_ref, g_ref, o_ref):
        o_ref[...] = _split_dot(x_ref[...], g_ref[...])

    return pl.pallas_call(
        body, name=name, out_shape=jax.ShapeDtypeStruct((x.shape[0], g.shape[1]), F32),
        in_specs=[pl.BlockSpec(memory_space=pltpu.VMEM)] * 2, out_specs=pl.BlockSpec(memory_space=pltpu.VMEM),
        compiler_params=pltpu.CompilerParams(vmem_limit_bytes=32 * MIB),
    )(x, g)


N_OFF = 2 * NA_ROWS - 1
TABLE_ROWS = 152


def _bias_c(rpb):
    table = jnp.zeros((TABLE_ROWS, LANES), F32).at[:N_HEADS_C * N_OFF, :2 * NA_COLS - 1].set(
        rpb.reshape(N_HEADS_C * N_OFF, 2 * NA_COLS - 1))
    tiles = _table_mm(table, _offset_onehot().T, "rpb_tiles")[:N_HEADS_C * N_OFF]
    tiles = tiles.reshape(N_HEADS_C, N_OFF, GRID_W, GRID_W)
    c = jnp.arange(GRID_W)
    col_start = jnp.clip(c - NA_COLS // 2, 0, GRID_W - NA_COLS)
    col_ok = (c[None, :] >= col_start[:, None]) & (c[None, :] < col_start[:, None] + NA_COLS)
    tiles = jnp.where(col_ok, tiles, NEG_INF)
    rows_q = QB // GRID_W
    rows_k = GROUP_C.keys // GRID_W

    def body(t_ref, o_ref):
        for var in range(3):
            for rq in range(rows_q):
                r_l = rows_q * var + rq
                first = min(max(r_l - NA_ROWS // 2, 0), rows_k - NA_ROWS)
                for rk in range(rows_k):
                    if first <= rk < first + NA_ROWS:
                        tile = t_ref[rk - r_l + NA_ROWS - 1]
                    else:
                        tile = jnp.full((GRID_W, GRID_W), NEG_INF, F32)
                    o_ref[var, rq * GRID_W:(rq + 1) * GRID_W, rk * GRID_W:(rk + 1) * GRID_W] = tile

    return pl.pallas_call(
        body, name="bias_c", grid=(N_HEADS_C,),
        in_specs=[pl.BlockSpec((None, N_OFF, GRID_W, GRID_W), lambda h: (h, 0, 0, 0))],
        out_specs=pl.BlockSpec((None, 3, QB, GROUP_C.keys), lambda h: (h, 0, 0, 0)),
        out_shape=jax.ShapeDtypeStruct((N_HEADS_C, 3, QB, GROUP_C.keys), F32),
        compiler_params=_params(("arbitrary",), 32),
    )(tiles)


def _rpb_grad(d_tiles):
    flat = jnp.zeros((TABLE_ROWS, GRID_W * GRID_W), F32).at[:N_HEADS_C * N_OFF].set(
        d_tiles.reshape(N_HEADS_C * N_OFF, GRID_W * GRID_W))
    out = _table_mm(flat, _offset_onehot(), "rpb_grad")
    return out[:N_HEADS_C * N_OFF, :2 * NA_COLS - 1].reshape(N_HEADS_C, N_OFF, 2 * NA_COLS - 1)


def _sink_lanes(sink):
    return jnp.repeat(sink.astype(F32), HEAD_DIM)[None, :]


def _attention_fwd(proj_r, sink_b, bias_a, bias_b, bias_c):
    no_sink_a = jnp.full((1, WIDTH_A), NEG_INF, F32)
    no_sink_c = jnp.full((1, WIDTH_C), NEG_INF, F32)
    oa = _attn_fwd(GROUP_A, proj_r, bias_a, no_sink_a, "attn_a_fwd")
    ob = _attn_fwd(GROUP_B, proj_r, bias_b, _sink_lanes(sink_b), "attn_b_fwd")
    oc = _attn_fwd(GROUP_C, proj_r, bias_c, no_sink_c, "attn_c_fwd")
    return oa, ob, oc


def _attention_bwd(proj_r, sink_b, bias_a, bias_b, bias_c, outs, d_outs, cos, sin):
    no_sink_a = jnp.full((1, WIDTH_A), NEG_INF, F32)
    no_sink_c = jnp.full((1, WIDTH_C), NEG_INF, F32)
    dqa, dka, dva, _ = _attn_bwd(GROUP_A, proj_r, bias_a, no_sink_a, outs[0], d_outs[0], "attn_a_bwd")
    dqb, dkb, dvb, dsink = _attn_bwd(GROUP_B, proj_r, bias_b, _sink_lanes(sink_b), outs[1], d_outs[1], "attn_b_bwd")
    dqc, dkc, dvc, _, d_tiles = _attn_bwd(GROUP_C, proj_r, bias_c, no_sink_c, outs[2], d_outs[2], "attn_c_bwd")
    d_proj = _rope_bwd((dqa, dka, dva, dqb, dkb, dvb, dqc, dkc, dvc), cos, sin, "rope_bwd")
    d_sink = dsink[:, 0, :].reshape(GROUP_B.pairs, 2, HEAD_DIM)[:, :, 0].reshape(N_HEADS_B)
    return d_proj, d_sink, _rpb_grad(d_tiles)


def _adamw(w, g, m, v, name):
    r, c = w.shape
    rows = r
    for cand in (512, 256, 128, 64, 32, 16, 8):
        if r % cand == 0 and cand * c * 4 <= MIB:
            rows = cand
            break
    spec = pl.BlockSpec((rows, c), lambda i: (i, 0))

    def body(w_ref, g_ref, m_ref, v_ref, d_ref, mo_ref, vo_ref):
        d_ref[...], mo_ref[...], vo_ref[...] = _adamw_step(w_ref[...], g_ref[...], m_ref[...], v_ref[...])

    return pl.pallas_call(
        body, name=name, grid=(r // rows,), in_specs=[spec] * 4, out_specs=[spec] * 3,
        out_shape=[jax.ShapeDtypeStruct((r, c), F32)] * 3, compiler_params=_params(("arbitrary",), 32),
    )(w, g, m, v)


def _adamw_step(w, grad, m, v):
    m_new = ADAM_B1 * m + (1.0 - ADAM_B1) * grad
    v_new = ADAM_B2 * v + (1.0 - ADAM_B2) * jnp.square(grad)
    m_hat = m_new / (1.0 - ADAM_B1 ** ADAM_STEP)
    v_hat = v_new / (1.0 - ADAM_B2 ** ADAM_STEP)
    return -ADAM_LR * (m_hat / (jnp.sqrt(v_hat) + ADAM_EPS) + ADAM_WD * w), m_new, v_new


def _adamw_layer(w, g, m, v, layer, prev, name):
    _, r, c = w.shape
    rows = next(cand for cand in (512, 256, 128, 64, 32, 16, 8) if r % cand == 0 and cand * c * 4 <= 2 * MIB)
    spec = pl.BlockSpec((None, rows, c), lambda i: (layer, i, 0))
    n_prev = 0 if prev is None else 4

    def body(w_ref, g_ref, m_ref, v_ref, *rest):
        go_ref, d_ref, mo_ref, vo_ref = rest[n_prev:]
        grad = g_ref[...]
        go_ref[...] = grad
        d_ref[...], mo_ref[...], vo_ref[...] = _adamw_step(w_ref[...], grad, m_ref[...], v_ref[...])

    return pl.pallas_call(
        body, name=name, grid=(r // rows,), in_specs=[spec] * 4 + [ANY_SPEC] * n_prev, out_specs=[spec] * 4,
        out_shape=[jax.ShapeDtypeStruct(w.shape, F32)] * 4,
        input_output_aliases={4 + i: i for i in range(n_prev)}, compiler_params=_params(("arbitrary",), 48),
    )(w, g, m, v, *(prev or ()))


def _layer_fwd(x0, p, weight, tabs):
    h1 = _rmsnorm_fwd(x0, p["ln_attn"], "ln_attn_fwd")
    proj = _mm_nn(h1, weight("w_in", h1), cols=True, tn=256, tk=D_MODEL, out_dtype=F32, name="mm_in")
    proj_r = _rope_fwd(proj, tabs["cos"], tabs["sin"], "rope_fwd")
    outs = _attention_fwd(proj_r, p["sink_b"], tabs["bias_a"], tabs["bias_b"], p["bias_c"])
    mixed = _mix_fwd(*outs, p["mix_gain"], "mix_fwd")
    x1 = _mm_nn(mixed, weight("w_out", mixed), cols=False, tn=256, tk=D_MODEL, out_dtype=F32, name="mm_out",
                residual=x0)
    h2 = _rmsnorm_fwd(x1, p["ln_ffn"], "ln_ffn_fwd")
    u0 = _mm_nn(h2, weight("w_up", h2), cols=True, tn=256, tk=D_MODEL, out_dtype=F32, name="mm_up", out_split=2)
    act = _convgate_fwd(u0, p["conv_w"], p["conv_b"], "convgate_fwd")
    x2 = _mm_nn(act, weight("w_down", act), cols=False, tn=512, tk=D_FF // 2, out_dtype=F32, name="mm_down",
                residual=x1)
    return x2, (x0, h1, proj_r, outs, mixed, x1, h2, u0, act)


def _layer_bwd(dx2, dx2_b, saved, p, big, tabs, begin, finish, pending):
    x0, h1, proj_r, outs, mixed, x1, h2, u0, act = saved
    d_act = _mm_nt(dx2_b, big["w_down"], cols=False, to=512, tr=D_MODEL, out_dtype=F32, name="nt_down",
                   after=[pending[1]] if pending else [])
    g_down = _mm_tn(act, dx2_b, tk=D_FF // N_SHARDS, tn=512, shards=-N_SHARDS, name="tn_down")
    du0, d_conv_w, d_conv_b = _convgate_bwd(u0, p["conv_w"], p["conv_b"], d_act, "convgate_bwd")
    token = [finish(pending[0], [du0])] if pending else []
    dh2 = _mm_nt(du0, big["w_up"], cols=True, to=1024, tr=D_FF // 4, out_dtype=F32, name="nt_up", after=token)
    g_up = _mm_tn(h2, du0, tk=512, tn=D_FF // 4, shards=N_SHARDS, name="tn_up")
    first, token = begin({"w_down": g_down, "w_up": g_up})
    dx1, dx1_b, d_ln_ffn = _rmsnorm_bwd(x1, p["ln_ffn"], dh2, dx2, "ln_ffn_bwd", after=[token])
    d_mixed = _mm_nt(dx1_b, big["w_out"], cols=False, to=512, tr=D_MODEL, out_dtype=F32, name="nt_out")
    g_out = _mm_tn(mixed, dx1_b, tk=D_MODEL // N_SHARDS, tn=512, shards=-N_SHARDS, name="tn_out")
    token = finish(first, [g_out])
    *d_outs, d_mix_gain = _mix_bwd(*outs, p["mix_gain"], d_mixed, "mix_bwd", after=[token])
    d_proj, d_sink, d_rpb = _attention_bwd(proj_r, p["sink_b"], tabs["bias_a"], tabs["bias_b"], p["bias_c"], outs,
                                           d_outs, tabs["cos"], tabs["sin"])
    dh1 = _mm_nt(d_proj, big["w_in"], cols=True, to=1024, tr=IN_COLS // N_SHARDS, out_dtype=F32, name="nt_in")
    g_in = _mm_tn(h1, d_proj, tk=512, tn=IN_COLS // N_SHARDS, shards=N_SHARDS, name="tn_in")
    dx0, dx0_b, d_ln_attn = _rmsnorm_bwd(x0, p["ln_attn"], dh1, dx1, "ln_attn_bwd")
    small = {"ln_attn": d_ln_attn, "sink_b": d_sink, "rpb_c": d_rpb, "mix_gain": d_mix_gain, "ln_ffn": d_ln_ffn,
             "conv_w": d_conv_w, "conv_b": d_conv_b}
    return dx0, dx0_b, small, begin({"w_out": g_out, "w_in": g_in})


HBM_SPEC = pl.BlockSpec(memory_space=pl.ANY)


def _place():
    x, y, c = lax.axis_index("x"), lax.axis_index("y"), lax.axis_index("c")
    chips = ((1 - x, y), (x, 1 - y), (1 - x, 1 - y))
    return x, y, c, chips


def _shard_index(px, py):
    return 2 * px + py


def _remote(src, dst, send_sem, recv_sem, to):
    return pltpu.make_async_remote_copy(src_ref=src, dst_ref=dst, send_sem=send_sem, recv_sem=recv_sem,
                                        device_id=to, device_id_type=MESH)


def _own_slot(w, layer, shard, name):
    _, r, c_dim = w.shape
    rows = r
    for cand in (512, 256, 128):
        if r % cand == 0 and cand * c_dim * 4 <= 2 * MIB:
            rows = cand
            break

    def body(s_ref, w_ref, o_ref):
        o_ref[...] = w_ref[...].astype(BF16)

    return pl.pallas_call(
        body, name=name,
        grid_spec=pltpu.PrefetchScalarGridSpec(
            num_scalar_prefetch=1, grid=(r // rows,),
            in_specs=[pl.BlockSpec((None, rows, c_dim), lambda i, s: (layer, i, 0))],
            out_specs=pl.BlockSpec((None, rows, c_dim), lambda i, s: (s[0], i, 0))),
        out_shape=jax.ShapeDtypeStruct((N_SHARDS, r, c_dim), BF16),
        compiler_params=_params(("arbitrary",), 32),
    )(shard.astype(jnp.int32).reshape(1), w)


HBM_ONLY = pl.BlockSpec(memory_space=pltpu.HBM)
SEM_SPEC = pl.BlockSpec(memory_space=pltpu.SEMAPHORE)
DATAFLOW = pltpu.SideEffectType.DATAFLOW_SIDE_EFFECTING


def _in_hbm(a):
    return pltpu.with_memory_space_constraint(a, pltpu.HBM)


N_DEV = 8


def _all_gather_small(vec, name, after=()):
    n_after = len(after)

    def body(v_ref, *rest):
        o_ref, send, recv, local_sem = rest[n_after:]
        x, y, c, _ = _place()
        me = 4 * x + 2 * y + c
        local = pltpu.make_async_copy(v_ref, o_ref.at[me], local_sem)
        local.start()
        flips = [(fx, fy, fc) for fx in (0, 1) for fy in (0, 1) for fc in (0, 1)][1:]
        peers = [((1 - x) if fx else x, (1 - y) if fy else y, (1 - c) if fc else c) for fx, fy, fc in flips]
        cps = [_remote(v_ref, o_ref.at[me], send.at[k], recv.at[k], peer) for k, peer in enumerate(peers)]
        for cp in cps:
            cp.start()
        for k, (px, py, pc) in enumerate(peers):
            slot = o_ref.at[4 * px + 2 * py + pc]
            _remote(slot, slot, send.at[k], recv.at[k], (px, py, pc)).wait_recv()
        for cp in cps:
            cp.wait_send()
        local.wait()

    return pl.pallas_call(
        body, name=name, in_specs=[HBM_SPEC] * (1 + n_after), out_specs=HBM_SPEC,
        out_shape=jax.ShapeDtypeStruct((N_DEV,) + vec.shape, vec.dtype),
        scratch_shapes=[pltpu.SemaphoreType.DMA((N_DEV - 1,))] * 2 + [pltpu.SemaphoreType.DMA(())],
    )(vec, *after)


def _peers(x, y, c):
    flips = [(fx, fy, fc) for fx in (0, 1) for fy in (0, 1) for fc in (0, 1)][1:]
    return [((1 - x) if fx else x, (1 - y) if fy else y, (1 - c) if fc else c) for fx, fy, fc in flips]


def _small_start(vec, after, name):
    n_after = len(after)

    def body(v_ref, slots_ref, *rest):
        send, recv = rest[n_after], rest[n_after + 1]
        token = rest[-1]
        x, y, c, _ = _place()
        me = 4 * x + 2 * y + c
        for k, peer in enumerate(_peers(x, y, c)):
            _remote(v_ref, slots_ref.at[me], send.at[k], recv.at[k], peer).start()
        token[...] = jnp.zeros_like(token)

    slots = jax.ShapeDtypeStruct((N_DEV,) + vec.shape, vec.dtype)
    res = pl.pallas_call(
        body, name=name,
        out_shape=(pltpu.SemaphoreType.DMA((N_DEV - 1,)), pltpu.SemaphoreType.DMA((N_DEV - 1,)),
                   pltpu.HBM(vec.shape, vec.dtype), pltpu.HBM(slots.shape, slots.dtype),
                   jax.ShapeDtypeStruct((8, LANES), F32)),
        in_specs=[HBM_ONLY, HBM_ONLY] + [ANY_SPEC] * n_after,
        out_specs=(SEM_SPEC, SEM_SPEC, HBM_ONLY, HBM_ONLY, pl.BlockSpec(memory_space=pltpu.VMEM)),
        input_output_aliases={0: 2, 1: 3},
        compiler_params=pltpu.CompilerParams(has_side_effects=DATAFLOW),
    )(_in_hbm(vec), _in_hbm(lax.empty(slots.shape, slots.dtype)), *after)
    return res


def _small_wait(send, recv, vec, slots, after, name):
    def body(v_ref, slots_ref, send_ref, recv_ref, *rest):
        x, y, c, _ = _place()
        for k, (px, py, pc) in enumerate(_peers(x, y, c)):
            cp = _remote(v_ref, slots_ref.at[4 * px + 2 * py + pc], send_ref.at[k], recv_ref.at[k], (px, py, pc))
            cp.wait_send()
            cp.wait_recv()

    return pl.pallas_call(
        body, name=name, out_shape=(pltpu.HBM(vec.shape, vec.dtype), pltpu.HBM(slots.shape, slots.dtype)),
        in_specs=[HBM_ONLY, HBM_ONLY, SEM_SPEC, SEM_SPEC] + [ANY_SPEC] * len(after), out_specs=[HBM_ONLY, HBM_ONLY],
        input_output_aliases={0: 0, 1: 1},
        compiler_params=pltpu.CompilerParams(has_side_effects=DATAFLOW),
    )(vec, slots, send, recv, *after)


def _small_sum(vec, slots, name):
    rows = vec.shape[0]
    blk = min(rows, 256)
    x, y, c = lax.axis_index("x"), lax.axis_index("y"), lax.axis_index("c")
    me = (4 * x + 2 * y + c).astype(jnp.int32).reshape(1)

    def slot_spec(k):
        return pl.BlockSpec((None, blk, LANES), lambda i, w: (jnp.where(w[0] == k, (k + 1) % N_DEV, k), i, 0))

    def body(w_ref, v_ref, *rest):
        o_ref = rest[-1]
        acc = None
        for k in range(N_DEV):
            term = jnp.where(w_ref[0] == k, v_ref[...], rest[k][...])
            acc = term if acc is None else acc + term
        o_ref[...] = acc

    return pl.pallas_call(
        body, name=name,
        grid_spec=pltpu.PrefetchScalarGridSpec(
            num_scalar_prefetch=1, grid=(rows // blk,),
            in_specs=[pl.BlockSpec((blk, LANES), lambda i, w: (i, 0))] + [slot_spec(k) for k in range(N_DEV)],
            out_specs=pl.BlockSpec((blk, LANES), lambda i, w: (i, 0))),
        out_shape=jax.ShapeDtypeStruct(vec.shape, F32), compiler_params=_params(("arbitrary",), 32),
    )(me, vec, *([slots] * N_DEV))


def _half(ref, slot, c):
    half = ref.shape[1] // 2
    return ref.at[slot, pl.ds(pl.multiple_of(c * half, 8), half)]


def _gather_start(bufs, after, name):
    n = len(bufs)
    n_after = len(after)

    def body(*refs):
        ins = refs[:n]
        send, recv = refs[n + n_after], refs[n + n_after + 1]
        token = refs[-1]
        x, y, c, chips = _place()
        me = _shard_index(x, y)
        for t in range(n):
            for j, (px, py) in enumerate(chips):
                mine = _half(ins[t], me, c)
                _remote(mine, mine, send.at[t * 3 + j], recv.at[t * 3 + j], (px, py, c)).start()
        token[...] = jnp.zeros_like(token)

    thru = [pltpu.HBM(b.shape, b.dtype) for b in bufs]
    res = pl.pallas_call(
        body, name=name,
        out_shape=(pltpu.SemaphoreType.DMA((n * 3,)), pltpu.SemaphoreType.DMA((n * 3,)), *thru,
                   jax.ShapeDtypeStruct((8, LANES), F32)),
        in_specs=[HBM_ONLY] * n + [ANY_SPEC] * n_after,
        out_specs=(SEM_SPEC, SEM_SPEC, *([HBM_ONLY] * n), pl.BlockSpec(memory_space=pltpu.VMEM)),
        input_output_aliases={i: 2 + i for i in range(n)},
        compiler_params=pltpu.CompilerParams(has_side_effects=DATAFLOW),
    )(*[_in_hbm(b) for b in bufs], *after)
    return res[0], res[1], list(res[2:2 + n]), res[-1]


def _gather_wait(send, recv, bufs, after, name):
    n = len(bufs)

    def body(*refs):
        ins = refs[:n]
        send_ref, recv_ref = refs[n], refs[n + 1]
        x, y, c, chips = _place()
        me = _shard_index(x, y)
        for t in range(n):
            for j, (px, py) in enumerate(chips):
                cp = _remote(_half(ins[t], me, c), _half(ins[t], _shard_index(px, py), c), send_ref.at[t * 3 + j],
                             recv_ref.at[t * 3 + j], (px, py, c))
                cp.wait_send()
                cp.wait_recv()

    res = pl.pallas_call(
        body, name=name, out_shape=tuple(pltpu.HBM(b.shape, b.dtype) for b in bufs),
        in_specs=[HBM_ONLY] * n + [SEM_SPEC, SEM_SPEC] + [ANY_SPEC] * len(after), out_specs=[HBM_ONLY] * n,
        input_output_aliases={i: i for i in range(n)},
        compiler_params=pltpu.CompilerParams(has_side_effects=DATAFLOW),
    )(*bufs, send, recv, *after)
    return list(res)


def _gather_forward(bufs, name):
    n = len(bufs)

    def body(*refs):
        outs = refs[n:2 * n]
        send, recv = refs[2 * n:]
        x, y, c, chips = _place()
        sibling = (x, y, 1 - c)
        cps = []
        for t in range(n):
            for j, (px, py) in enumerate(chips):
                got = _half(outs[t], _shard_index(px, py), c)
                cp = _remote(got, got, send.at[t * 3 + j], recv.at[t * 3 + j], sibling)
                cp.start()
                cps.append(cp)
        for t in range(n):
            for j, (px, py) in enumerate(chips):
                theirs = _half(outs[t], _shard_index(px, py), 1 - c)
                _remote(theirs, theirs, send.at[t * 3 + j], recv.at[t * 3 + j], sibling).wait_recv()
        for cp in cps:
            cp.wait_send()

    return pl.pallas_call(
        body, name=name, in_specs=[HBM_SPEC] * n, out_specs=[HBM_SPEC] * n,
        out_shape=[jax.ShapeDtypeStruct(b.shape, b.dtype) for b in bufs],
        input_output_aliases={t: t for t in range(n)},
        scratch_shapes=[pltpu.SemaphoreType.DMA((n * 3,))] * 2,
    )(*bufs)


def _sibling_rows(ref, c):
    half = ref.shape[1] // 2
    return ref.at[:, pl.ds(pl.multiple_of((1 - c) * half, 8), half)]


def _half_exchange_start(grads, name):
    n = len(grads)

    def body(*refs):
        ins, lands = refs[:n], refs[n:2 * n]
        send, recv = refs[2 * n], refs[2 * n + 1]
        token = refs[-1]
        x, y, c, _ = _place()
        for t in range(n):
            _remote(_sibling_rows(ins[t], c), lands[t], send.at[t], recv.at[t], (x, y, 1 - c)).start()
        token[...] = jnp.zeros_like(token)

    halves = [jax.ShapeDtypeStruct((g.shape[0], g.shape[1] // 2, g.shape[2]), g.dtype) for g in grads]
    res = pl.pallas_call(
        body, name=name,
        out_shape=(pltpu.SemaphoreType.DMA((n,)), pltpu.SemaphoreType.DMA((n,)),
                   *[pltpu.HBM(g.shape, g.dtype) for g in grads], *[pltpu.HBM(h.shape, h.dtype) for h in halves],
                   jax.ShapeDtypeStruct((8, LANES), F32)),
        in_specs=[HBM_ONLY] * (2 * n),
        out_specs=(SEM_SPEC, SEM_SPEC, *([HBM_ONLY] * (2 * n)), pl.BlockSpec(memory_space=pltpu.VMEM)),
        input_output_aliases={i: 2 + i for i in range(2 * n)},
        compiler_params=pltpu.CompilerParams(has_side_effects=DATAFLOW),
    )(*[_in_hbm(g) for g in grads], *[_in_hbm(lax.empty(h.shape, h.dtype)) for h in halves])
    return res[0], res[1], list(res[2:2 + n]), list(res[2 + n:2 + 2 * n]), res[-1]


def _half_exchange_wait(send, recv, grads, lands, after, name):
    n = len(grads)

    def body(*refs):
        ins, got = refs[:n], refs[n:2 * n]
        send_ref, recv_ref = refs[2 * n], refs[2 * n + 1]
        x, y, c, _ = _place()
        for t in range(n):
            cp = _remote(_sibling_rows(ins[t], c), got[t], send_ref.at[t], recv_ref.at[t], (x, y, 1 - c))
            cp.wait_send()
            cp.wait_recv()

    res = pl.pallas_call(
        body, name=name,
        out_shape=(*[pltpu.HBM(g.shape, g.dtype) for g in grads], *[pltpu.HBM(h.shape, h.dtype) for h in lands]),
        in_specs=[HBM_ONLY] * (2 * n) + [SEM_SPEC, SEM_SPEC] + [ANY_SPEC] * len(after),
        out_specs=[HBM_ONLY] * (2 * n),
        input_output_aliases={i: i for i in range(2 * n)},
        compiler_params=pltpu.CompilerParams(has_side_effects=DATAFLOW),
    )(*grads, *lands, send, recv, *after)
    return list(res[:n]), list(res[n:])


def _half_rows(half, c_dim):
    for cand in (512, 256, 128, 64):
        if half % cand == 0 and cand * c_dim * 2 <= 2 * MIB:
            return cand
    raise ValueError((half, c_dim))


def _core_index():
    return lax.axis_index("c").astype(jnp.int32).reshape(1)


def _half_sum(own, other, name):
    s, r, c_dim = own.shape
    rows = _half_rows(r // 2, c_dim)
    per = r // 2 // rows

    def body(c_ref, a_ref, b_ref, o_ref):
        o_ref[...] = (a_ref[...].astype(F32) + b_ref[...].astype(F32)).astype(BF16)

    return pl.pallas_call(
        body, name=name,
        grid_spec=pltpu.PrefetchScalarGridSpec(
            num_scalar_prefetch=1, grid=(s, per),
            in_specs=[pl.BlockSpec((None, rows, c_dim), lambda k, i, c: (k, c[0] * per + i, 0)),
                      pl.BlockSpec((None, rows, c_dim), lambda k, i, c: (k, i, 0))],
            out_specs=pl.BlockSpec((None, rows, c_dim), lambda k, i, c: (k, i, 0))),
        out_shape=pltpu.HBM((s, r // 2, c_dim), BF16), compiler_params=_params(("arbitrary", "arbitrary"), 32),
    )(_core_index(), own, other)


def _reduce_start(pairs, name):
    n = len(pairs)

    def body(*refs):
        ins, lands = refs[:n], refs[n:2 * n]
        send, recv = refs[2 * n], refs[2 * n + 1]
        token = refs[-1]
        x, y, c, chips = _place()
        me = _shard_index(x, y)
        for t in range(n):
            for j, (px, py) in enumerate(chips):
                _remote(ins[t].at[_shard_index(px, py)], lands[t].at[me], send.at[t * 3 + j], recv.at[t * 3 + j],
                        (px, py, c)).start()
        token[...] = jnp.zeros_like(token)

    thru = [pltpu.HBM(b.shape, b.dtype) for b in pairs]
    res = pl.pallas_call(
        body, name=name,
        out_shape=(pltpu.SemaphoreType.DMA((n * 3,)), pltpu.SemaphoreType.DMA((n * 3,)), *thru, *thru,
                   jax.ShapeDtypeStruct((8, LANES), F32)),
        in_specs=[HBM_ONLY] * (2 * n),
        out_specs=(SEM_SPEC, SEM_SPEC, *([HBM_ONLY] * (2 * n)), pl.BlockSpec(memory_space=pltpu.VMEM)),
        input_output_aliases={i: 2 + i for i in range(2 * n)},
        compiler_params=pltpu.CompilerParams(has_side_effects=DATAFLOW),
    )(*[_in_hbm(b) for b in pairs], *[_in_hbm(lax.empty(b.shape, b.dtype)) for b in pairs])
    return res[0], res[1], list(res[2:2 + n]), list(res[2 + n:2 + 2 * n]), res[-1]


def _reduce_wait(send, recv, pairs, lands, after, name):
    n = len(pairs)

    def body(*refs):
        ins, got = refs[:n], refs[n:2 * n]
        send_ref, recv_ref = refs[2 * n], refs[2 * n + 1]
        x, y, c, chips = _place()
        for t in range(n):
            for j, (px, py) in enumerate(chips):
                s = _shard_index(px, py)
                cp = _remote(ins[t].at[s], got[t].at[s], send_ref.at[t * 3 + j], recv_ref.at[t * 3 + j], (px, py, c))
                cp.wait_send()
                cp.wait_recv()

    thru = [pltpu.HBM(b.shape, b.dtype) for b in pairs]
    res = pl.pallas_call(
        body, name=name, out_shape=(*thru, *thru),
        in_specs=[HBM_ONLY] * (2 * n) + [SEM_SPEC, SEM_SPEC] + [ANY_SPEC] * len(after),
        out_specs=[HBM_ONLY] * (2 * n),
        input_output_aliases={i: i for i in range(2 * n)},
        compiler_params=pltpu.CompilerParams(has_side_effects=DATAFLOW),
    )(*pairs, *lands, send, recv, *after)
    return list(res[:n]), list(res[n:])


def _reduce_sum(pair, landed, layer, prev, name):
    s, half, c_dim = pair.shape
    rows = _half_rows(half, c_dim)
    per = half // rows
    shard = _shard_index(lax.axis_index("x"), lax.axis_index("y"))
    where = jnp.stack([shard, lax.axis_index("c")]).astype(jnp.int32)

    def landed_spec(k):
        return pl.BlockSpec((None, rows, c_dim), lambda i, w: (jnp.where(w[0] == k, (k + 1) % s, k), i, 0))

    def body(w_ref, own_ref, *rest):
        o_ref = rest[-1]
        acc = None
        for k in range(s):
            term = jnp.where(w_ref[0] == k, own_ref[...], rest[k][...]).astype(F32)
            acc = term if acc is None else acc + term
        o_ref[...] = acc

    args = [where, pair] + [landed] * s
    in_specs = [pl.BlockSpec((None, rows, c_dim), lambda i, w: (w[0], i, 0))] + [landed_spec(k) for k in range(s)]
    aliases = {}
    if prev is not None:
        args.append(prev)
        in_specs.append(ANY_SPEC)
        aliases = {len(args) - 1: 0}
    return pl.pallas_call(
        body, name=name,
        grid_spec=pltpu.PrefetchScalarGridSpec(
            num_scalar_prefetch=1, grid=(per,), in_specs=in_specs,
            out_specs=pl.BlockSpec((None, rows, c_dim), lambda i, w: (layer, w[1] * per + i, 0))),
        out_shape=jax.ShapeDtypeStruct((DEPTH, 2 * half, c_dim), F32), input_output_aliases=aliases,
        compiler_params=_params(("arbitrary",), 40),
    )(*args)


def _half_gather(bufs, layer, name):
    n = len(bufs)

    def body(*refs):
        outs = refs[n:2 * n]
        send, recv = refs[2 * n:]
        x, y, c, _ = _place()
        sibling = (x, y, 1 - c)

        def rows(t, which):
            half = outs[t].shape[1] // 2
            return outs[t].at[layer, pl.ds(pl.multiple_of(which * half, 8), half)]

        cps = [_remote(rows(t, c), rows(t, c), send.at[t], recv.at[t], sibling) for t in range(n)]
        for cp in cps:
            cp.start()
        for t in range(n):
            _remote(rows(t, 1 - c), rows(t, 1 - c), send.at[t], recv.at[t], sibling).wait_recv()
        for cp in cps:
            cp.wait_send()

    return pl.pallas_call(
        body, name=name, in_specs=[HBM_SPEC] * n, out_specs=[HBM_SPEC] * n,
        out_shape=[jax.ShapeDtypeStruct(b.shape, b.dtype) for b in bufs],
        input_output_aliases={t: t for t in range(n)},
        scratch_shapes=[pltpu.SemaphoreType.DMA((n,))] * 2,
    )(*bufs)


WEIGHT_NAMES = ("ln_attn", "w_in", "sink_b", "rpb_c", "mix_gain", "w_out", "ln_ffn", "w_up", "conv_w", "conv_b",
                "w_down", "ln_final")
BIG_NAMES = ("w_in", "w_out", "w_up", "w_down")
REPLICATED_NAMES = ("ln_attn", "sink_b", "rpb_c", "mix_gain", "ln_ffn", "conv_b", "ln_final")
PACK_TILE = 8 * LANES


def _pack(arrays, row_multiple):
    pieces = []
    for a in arrays:
        flat = a.reshape(-1)
        pieces.append(jnp.pad(flat, (0, (-flat.shape[0]) % PACK_TILE)))
    flat = jnp.concatenate(pieces)
    flat = jnp.pad(flat, (0, (-flat.shape[0]) % (row_multiple * LANES)))
    return flat.reshape(-1, LANES)


def _unpack(packed, shapes):
    flat = packed.reshape(-1)
    out, off = [], 0
    for shape in shapes:
        size = math.prod(shape)
        out.append(flat[off:off + size].reshape(shape))
        off += size + (-size) % PACK_TILE
    return out


def kernel(x, ln_attn, w_in, sink_b, rpb_c, mix_gain, w_out, ln_ffn, w_up, conv_w, conv_b, w_down, ln_final, loss_target, m_ln_attn, m_w_in, m_sink_b, m_rpb_c, m_mix_gain, m_w_out, m_ln_ffn, m_w_up, m_conv_w, m_conv_b, m_w_down, m_ln_final, v_ln_attn, v_w_in, v_sink_b, v_rpb_c, v_mix_gain, v_w_out, v_ln_ffn, v_w_up, v_conv_w, v_conv_b, v_w_down, v_ln_final):
    w = dict(ln_attn=ln_attn, w_in=w_in, sink_b=sink_b, rpb_c=rpb_c, mix_gain=mix_gain, w_out=w_out, ln_ffn=ln_ffn,
             w_up=w_up, conv_w=conv_w, conv_b=conv_b, w_down=w_down, ln_final=ln_final)
    m = dict(ln_attn=m_ln_attn, w_in=m_w_in, sink_b=m_sink_b, rpb_c=m_rpb_c, mix_gain=m_mix_gain, w_out=m_w_out,
             ln_ffn=m_ln_ffn, w_up=m_w_up, conv_w=m_conv_w, conv_b=m_conv_b, w_down=m_w_down, ln_final=m_ln_final)
    v = dict(ln_attn=v_ln_attn, w_in=v_w_in, sink_b=v_sink_b, rpb_c=v_rpb_c, mix_gain=v_mix_gain, w_out=v_w_out,
             ln_ffn=v_ln_ffn, w_up=v_w_up, conv_w=v_conv_w, conv_b=v_conv_b, w_down=v_w_down, ln_final=v_ln_final)
    shard = _shard_index(lax.axis_index("x"), lax.axis_index("y"))
    up_cols = w_up.shape[2]

    conv_slots = _all_gather_small(_pack([conv_w], 8), "gather_conv_w")
    conv_all = conv_slots[0::2].reshape(N_SHARDS, -1)[:, :conv_w.size].reshape((N_SHARDS,) + conv_w.shape)

    arrivals = []
    group_of = {}
    tokens = []
    rest = ("w_out", "w_up", "w_down")
    for l, names in ((0, ("w_in",)), (0, rest), (1, ("w_in",)), (1, rest)):
        bufs = [_own_slot(w[k], l, shard, "own_" + k) for k in names]
        send, recv, bufs, token = _gather_start(bufs, tokens[-1:] or [conv_slots], "gather_start_%d" % len(arrivals))
        tokens.append(token)
        for k in names:
            group_of[l, k] = len(arrivals)
        arrivals.append({"names": names, "send": send, "recv": recv, "bufs": bufs, "done": None})

    def gathered(l, name, after):
        idx = group_of[l, name]
        group = arrivals[idx]
        if group["done"] is None:
            bufs = _gather_wait(group["send"], group["recv"], group["bufs"], list(after) + tokens[-1:],
                                "gather_wait_%d" % idx)
            done = dict(zip(group["names"], _gather_forward(bufs, "gather_forward_%d" % idx)))
            for k in ("w_out", "w_down"):
                if k in done:
                    done[k] = done[k].reshape(1, -1, done[k].shape[2])
            group["done"] = done
        return group["done"][name]

    cos, sin = _rope_tables(SEQ)
    tabs = {"cos": cos, "sin": sin, "bias_a": _bias_a(), "bias_b": _bias_b()}
    layers = []
    for l in range(DEPTH):
        conv_w_l = conv_all[:, l].reshape(2, N_SHARDS // 2, 3, up_cols).transpose(0, 2, 1, 3).reshape(2, 3, D_FF)
        layers.append({"ln_attn": ln_attn[l][None], "sink_b": sink_b[l], "bias_c": _bias_c(rpb_c[l]),
                       "mix_gain": mix_gain[l][None], "ln_ffn": ln_ffn[l][None], "conv_w": conv_w_l,
                       "conv_b": conv_b[l].reshape(2, 1, D_FF)})

    act = x[0]
    saved = []
    for l in range(DEPTH):
        act, keep = _layer_fwd(act, layers[l], lambda name, after, l=l: gathered(l, name, [after]), tabs)
        saved.append(keep)
    loss_part, dx, dx_b, d_ln_final = _loss_head(act, ln_final[None], loss_target[0], "loss_head")
    loss = lax.psum(loss_part[0, 0], ("x", "y", "c"))

    reductions = []

    opened = [0]

    def begin(l, partial):
        idx = opened[0]
        opened[0] += 1
        names = tuple(partial)
        send_sem, recv_sem, mine, theirs, token = _half_exchange_start([partial[k] for k in names],
                                                                       "half_exchange_start_%d" % idx)
        return {"idx": idx, "layer": l, "names": names, "send": send_sem, "recv": recv_sem, "mine": mine,
                "theirs": theirs}, token

    def finish(handle, after):
        idx, names = handle["idx"], handle["names"]
        mine, theirs = _half_exchange_wait(handle["send"], handle["recv"], handle["mine"], handle["theirs"], after,
                                           "half_exchange_wait_%d" % idx)
        pairs = [_half_sum(a, b, "half_sum_" + k) for k, a, b in zip(names, mine, theirs)]
        send_sem, recv_sem, pairs, lands, token = _reduce_start(pairs, "reduce_start_%d" % idx)
        reductions.append({"layer": handle["layer"], "names": names, "send": send_sem, "recv": recv_sem,
                           "pairs": pairs, "lands": lands})
        return token

    small = [None] * DEPTH
    pending = None
    for l in reversed(range(DEPTH)):
        big = {k: gathered(l, k, []) for k in BIG_NAMES}
        dx, dx_b, small[l], pending = _layer_bwd(dx, dx_b, saved[l], layers[l], big, tabs,
                                                 functools.partial(begin, l), finish, pending)
    after = [finish(pending[0], [pending[1]])]

    stacked = {k: jnp.stack([small[l][k] for l in range(DEPTH)]) for k in small[0]}
    part = {"ln_attn": stacked["ln_attn"][:, 0], "sink_b": stacked["sink_b"], "rpb_c": stacked["rpb_c"],
            "mix_gain": stacked["mix_gain"][:, 0], "ln_ffn": stacked["ln_ffn"][:, 0],
            "conv_b": stacked["conv_b"].reshape(DEPTH, 2 * D_FF), "ln_final": d_ln_final[0],
            "conv_w": stacked["conv_w"].transpose(0, 2, 1, 3).reshape(DEPTH, 3, 2 * D_FF)}
    small_names = REPLICATED_NAMES + ("conv_w",)
    small_send, small_recv, small_vec, small_slots, token = _small_start(
        _pack([part[k] for k in small_names], 256), after, "small_grads_start")
    after = [token]

    grads, delta, new_m, new_v = {}, {}, {}, {}
    reduced = {}
    updated = dict.fromkeys(BIG_NAMES)
    for l in reversed(range(DEPTH)):
        for idx, group in enumerate(reductions):
            if group["layer"] != l:
                continue
            pairs, lands = _reduce_wait(group["send"], group["recv"], group["pairs"], group["lands"], after,
                                        "reduce_wait_%d" % idx)
            for k, pair, landed in zip(group["names"], pairs, lands):
                reduced[k] = _reduce_sum(pair, landed, l, reduced.get(k), "reduce_sum_" + k)
            after = [reduced[group["names"][-1]]]
        reduced = dict(zip(BIG_NAMES, _half_gather([reduced[k] for k in BIG_NAMES], l, "half_gather_%d" % l)))
        for k in BIG_NAMES:
            updated[k] = _adamw_layer(w[k], reduced[k], m[k], v[k], l, updated[k], "adamw_" + k)
        after = [updated[k][0] for k in BIG_NAMES]
    for k in BIG_NAMES:
        grads[k], delta[k], new_m[k], new_v[k] = updated[k]

    small_vec, small_slots = _small_wait(small_send, small_recv, small_vec, small_slots, after, "small_grads_wait")
    total = _small_sum(small_vec, small_slots, "small_grads_sum")
    for k, g in zip(small_names, _unpack(total, [part[k].shape for k in small_names])):
        grads[k] = g
    grads["conv_w"] = lax.dynamic_slice_in_dim(grads["conv_w"], shard * up_cols, up_cols, axis=2)

    flat = (DEPTH * 3, up_cols)
    res = _adamw(conv_w.reshape(flat), grads["conv_w"].reshape(flat), m["conv_w"].reshape(flat),
                 v["conv_w"].reshape(flat), "adamw_conv_w")
    delta["conv_w"], new_m["conv_w"], new_v["conv_w"] = (r.reshape(conv_w.shape) for r in res)
    shapes = [w[k].shape for k in REPLICATED_NAMES]
    packed = [_pack([d[k] for k in REPLICATED_NAMES], 128) for d in (w, grads, m, v)]
    for d, res in zip((delta, new_m, new_v), _adamw(*packed, "adamw_small")):
        for k, r in zip(REPLICATED_NAMES, _unpack(res, shapes)):
            d[k] = r

    return (loss, dx[None], *[grads[k] for k in WEIGHT_NAMES], *[delta[k] for k in WEIGHT_NAMES],
            *[new_m[k] for k in WEIGHT_NAMES], *[new_v[k] for k in WEIGHT_NAMES])
```

```python
import functools
import math

import jax
import jax.numpy as jnp
from jax import lax
from jax.experimental import pallas as pl
from jax.experimental.pallas import tpu as pltpu

F32 = jnp.float32
BF16 = jnp.bfloat16
MESH = pl.DeviceIdType.MESH

D_MODEL = 2048
SEQ = 2048
DEPTH = 2
HEAD_DIM = 64
N_HEADS_A = 12
N_HEADS_B = 10
N_KV_B = 2
N_HEADS_C = 10
WINDOW_B = 128
GRID_W = 64
NA_ROWS = 8
NA_COLS = 16
WIDTH_A = N_HEADS_A * HEAD_DIM
WIDTH_B = N_HEADS_B * HEAD_DIM
WIDTH_C = N_HEADS_C * HEAD_DIM
IN_COLS = 5120
D_FF = 5632
ROPE_THETA = 10000.0
EPS = 1e-6
NEG_INF = -1e30
N_SHARDS = 4

ADAM_LR = 0.001
ADAM_B1 = 0.9
ADAM_B2 = 0.999
ADAM_EPS = 1e-08
ADAM_WD = 0.01
ADAM_STEP = 10

LANES = 128
QB = 256
NQB = SEQ // QB
ROWS = 256
MIB = 2 ** 20

A_BLK = (0, 6, 12)
B_BLK = (18, 23, 24)
C_BLK = (25, 30, 35)
ROPE_BLKS = tuple(range(0, 12)) + tuple(range(18, 24))
QSCALE_BLKS = tuple(range(0, 6)) + tuple(range(18, 23)) + tuple(range(25, 30))
N_PBLK = IN_COLS // LANES


def _params(sem, vmem_mib):
    return pltpu.CompilerParams(dimension_semantics=sem, vmem_limit_bytes=vmem_mib * MIB)


def _weight_spec(w, cols, t_in, t_out, transposed):
    s, r, c = w.shape
    if cols:
        per = c // t_out
        k_dim, n = r, s * c
        if transposed:
            index = lambda j, rr: (rr // per, j, rr % per)
        else:
            index = lambda j, kk: (j // per, kk, j % per)
    else:
        per = r // t_in
        k_dim, n = s * r, c
        if transposed:
            index = lambda j, rr: (j // per, j % per, rr)
        else:
            index = lambda j, kk: (kk // per, kk % per, j)
    return pl.BlockSpec((None, t_in, t_out), index), k_dim, n


def _mm_nn(a, w, *, cols, tn, tk, out_dtype, name, residual=None, out_split=1):
    m, k_dim = a.shape
    w_spec, k_w, n = _weight_spec(w, cols, tk, tn, False)
    assert k_w == k_dim
    nj, nk = n // tn, k_dim // tk
    in_specs = [pl.BlockSpec((m, tk), lambda j, k: (0, k)), w_spec]
    args = [a, w]
    if residual is not None:
        in_specs.append(pl.BlockSpec((m, tn), lambda j, k: (0, j)))
        args.append(residual)
    if out_split > 1:
        per_o = n // out_split // tn
        out_spec = pl.BlockSpec((None, m, tn), lambda j, k: (j // per_o, 0, j % per_o))
        out_shape = pltpu.HBM((out_split, m, n // out_split), out_dtype)
    else:
        out_spec = pl.BlockSpec((m, tn), lambda j, k: (0, j))
        out_shape = pltpu.HBM((m, n), out_dtype)

    def body(*refs):
        a_ref, w_ref = refs[0], refs[1]
        r_ref = refs[2] if residual is not None else None
        o_ref = refs[3] if residual is not None else refs[2]

        def finish(val):
            if r_ref is not None:
                val = r_ref[...] + val
            o_ref[...] = val.astype(o_ref.dtype)

        part = jnp.dot(a_ref[...], w_ref[...], preferred_element_type=F32)
        if nk == 1:
            finish(part)
        else:
            acc = refs[-1]
            kk = pl.program_id(1)

            @pl.when(kk == 0)
            def _():
                acc[...] = part

            @pl.when(kk > 0)
            def _():
                acc[...] += part

            @pl.when(kk == nk - 1)
            def _():
                finish(acc[...])

    return pl.pallas_call(
        body, name=name, grid=(nj, nk), in_specs=in_specs, out_specs=out_spec, out_shape=out_shape,
        scratch_shapes=[pltpu.VMEM((m, tn), F32)] if nk > 1 else [],
        compiler_params=_params(("arbitrary", "arbitrary"), 56),
    )(*[_in_hbm(a) for a in args])


ANY_SPEC = pl.BlockSpec(memory_space=pl.ANY)


def _mm_nt(dy, w, *, cols, to, tr, out_dtype, name, after=()):
    if dy.ndim == 3:
        m = dy.shape[1]
        n = dy.shape[0] * dy.shape[2]
        per_d = dy.shape[2] // tr
        dy_spec = pl.BlockSpec((None, m, tr), lambda j, r: (r // per_d, 0, r % per_d))
    else:
        m, n = dy.shape
        dy_spec = pl.BlockSpec((m, tr), lambda j, r: (0, r))
    w_spec, k_dim, n_w = _weight_spec(w, cols, to, tr, True)
    assert n_w == n
    nj, nr = k_dim // to, n // tr

    n_after = len(after)

    def body(dy_ref, w_ref, *rest):
        o_ref = rest[n_after]
        part = lax.dot_general(dy_ref[...], w_ref[...], (((1,), (1,)), ((), ())), preferred_element_type=F32)
        if nr == 1:
            o_ref[...] = part.astype(o_ref.dtype)
        else:
            acc = rest[n_after + 1]
            rr = pl.program_id(1)

            @pl.when(rr == 0)
            def _():
                acc[...] = part

            @pl.when(rr > 0)
            def _():
                acc[...] += part

            @pl.when(rr == nr - 1)
            def _():
                o_ref[...] = acc[...].astype(o_ref.dtype)

    return pl.pallas_call(
        body, name=name, grid=(nj, nr), in_specs=[dy_spec, w_spec] + [ANY_SPEC] * n_after,
        out_specs=pl.BlockSpec((m, to), lambda j, r: (0, j)),
        out_shape=pltpu.HBM((m, k_dim), out_dtype),
        scratch_shapes=[pltpu.VMEM((m, to), F32)] if nr > 1 else [],
        compiler_params=_params(("arbitrary", "arbitrary"), 56),
    )(_in_hbm(dy), _in_hbm(w), *after)


def _mm_tn(x, dy, *, tk, tn, shards, name):
    m, k_dim = x.shape
    if dy.ndim == 3:
        n = dy.shape[0] * dy.shape[2]
        per_d = dy.shape[2] // tn
        dy_spec = pl.BlockSpec((None, m, tn), lambda i, j: (j // per_d, 0, j % per_d))
    else:
        n = dy.shape[1]
        dy_spec = pl.BlockSpec((m, tn), lambda i, j: (0, j))
    if shards > 0:
        per = n // shards // tn
        out_shape = pltpu.HBM((shards, k_dim, n // shards), BF16)
        out_spec = pl.BlockSpec((None, tk, tn), lambda i, j: (j // per, i, j % per))
    else:
        s = -shards
        per = k_dim // s // tk
        out_shape = pltpu.HBM((s, k_dim // s, n), BF16)
        out_spec = pl.BlockSpec((None, tk, tn), lambda i, j: (i // per, i % per, j))

    def body(x_ref, dy_ref, o_ref):
        o_ref[...] = lax.dot_general(x_ref[...], dy_ref[...], (((0,), (0,)), ((), ())),
                                     preferred_element_type=F32).astype(BF16)

    return pl.pallas_call(
        body, name=name, grid=(k_dim // tk, n // tn),
        in_specs=[pl.BlockSpec((m, tk), lambda i, j: (0, i)), dy_spec], out_specs=out_spec, out_shape=out_shape,
        compiler_params=_params(("arbitrary", "arbitrary"), 56),
    )(_in_hbm(x), _in_hbm(dy))


def _row_spec(width, rows=ROWS):
    return pl.BlockSpec((rows, width), lambda i: (i, 0))


def _vec_spec(width):
    return pl.BlockSpec((1, width), lambda i: (0, 0))


def _rms_stats(x):
    r = lax.rsqrt(jnp.mean(x * x, axis=-1, keepdims=True) + EPS)
    return r, x * r


def _rmsnorm_fwd(x, gain, name):
    t, d = x.shape

    def body(x_ref, g_ref, o_ref):
        _, n = _rms_stats(x_ref[...])
        o_ref[...] = (n * g_ref[...]).astype(BF16)

    return pl.pallas_call(
        body, name=name, grid=(t // ROWS,), in_specs=[_row_spec(d), _vec_spec(d)], out_specs=_row_spec(d),
        out_shape=pltpu.HBM((t, d), BF16), compiler_params=_params(("arbitrary",), 32),
    )(_in_hbm(x), _in_hbm(gain))


def _rmsnorm_bwd(x, gain, dh, dres, name, after=()):
    t, d = x.shape
    n_after = len(after)

    def body(x_ref, g_ref, dh_ref, dres_ref, *rest):
        dx_ref, dxb_ref, dg_ref = rest[n_after:]
        r, n = _rms_stats(x_ref[...])
        dh_v = dh_ref[...]
        dn = dh_v * g_ref[...]
        dx = dres_ref[...] + r * (dn - n * jnp.mean(dn * n, axis=-1, keepdims=True))
        dx_ref[...] = dx
        dxb_ref[...] = dx.astype(BF16)
        part = jnp.sum(dh_v * n, axis=0, keepdims=True)

        @pl.when(pl.program_id(0) == 0)
        def _():
            dg_ref[...] = part

        @pl.when(pl.program_id(0) > 0)
        def _():
            dg_ref[...] += part

    return pl.pallas_call(
        body, name=name, grid=(t // ROWS,),
        in_specs=[_row_spec(d), _vec_spec(d), _row_spec(d), _row_spec(d)] + [ANY_SPEC] * n_after,
        out_specs=[_row_spec(d), _row_spec(d), _vec_spec(d)],
        out_shape=[pltpu.HBM((t, d), F32), pltpu.HBM((t, d), BF16), jax.ShapeDtypeStruct((1, d), F32)],
        compiler_params=_params(("arbitrary",), 40),
    )(_in_hbm(x), _in_hbm(gain), _in_hbm(dh), _in_hbm(dres), *after)


def _loss_head(x, gain, target, name):
    t, d = x.shape

    def body(x_ref, g_ref, t_ref, loss_ref, dx_ref, dxb_ref, dg_ref):
        r, n = _rms_stats(x_ref[...])
        g = g_ref[...]
        err = n * g - t_ref[...]
        dy = err * (1.0 / d)
        dn = dy * g
        dx = r * (dn - n * jnp.mean(dn * n, axis=-1, keepdims=True))
        dx_ref[...] = dx
        dxb_ref[...] = dx.astype(BF16)
        part = jnp.sum(dy * n, axis=0, keepdims=True)
        lpart = jnp.zeros((8, LANES), F32) + 0.5 * jnp.sum(jnp.mean(err * err, axis=-1, keepdims=True))

        @pl.when(pl.program_id(0) == 0)
        def _():
            dg_ref[...] = part
            loss_ref[...] = lpart

        @pl.when(pl.program_id(0) > 0)
        def _():
            dg_ref[...] += part
            loss_ref[...] += lpart

    return pl.pallas_call(
        body, name=name, grid=(t // ROWS,),
        in_specs=[_row_spec(d), _vec_spec(d), _row_spec(d)],
        out_specs=[pl.BlockSpec((8, LANES), lambda i: (0, 0)), _row_spec(d), _row_spec(d), _vec_spec(d)],
        out_shape=[jax.ShapeDtypeStruct((8, LANES), F32), pltpu.HBM((t, d), F32), pltpu.HBM((t, d), BF16),
                   jax.ShapeDtypeStruct((1, d), F32)],
        compiler_params=_params(("arbitrary",), 40),
    )(x, gain, target)


def _swap_halves(x):
    lane = lax.broadcasted_iota(jnp.int32, x.shape, 1)
    return jnp.where((lane % HEAD_DIM) < HEAD_DIM // 2, pltpu.roll(x, LANES - HEAD_DIM // 2, 1),
                     pltpu.roll(x, HEAD_DIM // 2, 1))


def _rope_tables(t):
    inv_freq = ROPE_THETA ** (-jnp.arange(0, HEAD_DIM, 2, dtype=F32) / HEAD_DIM)
    ang = jnp.arange(t, dtype=F32)[:, None] * inv_freq[None, :]
    cos = jnp.tile(jnp.cos(ang), (1, LANES // (HEAD_DIM // 2)))
    sin = jnp.tile(jnp.sin(ang), (1, LANES // (HEAD_DIM // 2)))
    lane = jnp.arange(LANES)[None, :]
    return cos, jnp.where((lane % HEAD_DIM) < HEAD_DIM // 2, -sin, sin)


def _rope_fwd(proj, cos, sin, name):
    t = proj.shape[0]
    scale = HEAD_DIM ** -0.5

    def body(p_ref, c_ref, s_ref, o_ref):
        cos_v, sin_v = c_ref[...], s_ref[...]
        for b in range(N_PBLK):
            cols = slice(b * LANES, (b + 1) * LANES)
            v = p_ref[:, cols]
            if b in ROPE_BLKS:
                v = v * cos_v + _swap_halves(v) * sin_v
            if b in QSCALE_BLKS:
                v = v * scale
            o_ref[:, cols] = v.astype(BF16)

    return pl.pallas_call(
        body, name=name, grid=(t // ROWS,),
        in_specs=[_row_spec(IN_COLS), _row_spec(LANES), _row_spec(LANES)], out_specs=_row_spec(IN_COLS),
        out_shape=pltpu.HBM((t, IN_COLS), BF16), compiler_params=_params(("arbitrary",), 40),
    )(_in_hbm(proj), _in_hbm(cos), _in_hbm(sin))


def _rope_bwd(grads, cos, sin, name):
    t = grads[0].shape[0]
    scale = HEAD_DIM ** -0.5
    group = N_HEADS_B // N_KV_B

    def body(*refs):
        c_ref, s_ref, o_ref = refs[9], refs[10], refs[11]
        cos_v, sin_v = c_ref[...], s_ref[...]

        def kv_sum(ref):
            parts = []
            for g in range(N_KV_B):
                acc = ref[:, g * group * HEAD_DIM:(g * group + 1) * HEAD_DIM]
                for h in range(g * group + 1, (g + 1) * group):
                    acc = acc + ref[:, h * HEAD_DIM:(h + 1) * HEAD_DIM]
                parts.append(acc)
            return jnp.concatenate(parts, axis=1)

        def emit(b, v):
            if b in ROPE_BLKS:
                v = v * cos_v - _swap_halves(v) * sin_v
            if b in QSCALE_BLKS:
                v = v * scale
            o_ref[:, b * LANES:(b + 1) * LANES] = v.astype(BF16)

        starts = (A_BLK[0], A_BLK[1], A_BLK[2], B_BLK[0], None, None, C_BLK[0], C_BLK[1], C_BLK[2])
        for idx, start in enumerate(starts):
            if start is None:
                continue
            for j in range(refs[idx].shape[1] // LANES):
                emit(start + j, refs[idx][:, j * LANES:(j + 1) * LANES])
        emit(B_BLK[1], kv_sum(refs[4]))
        emit(B_BLK[2], kv_sum(refs[5]))

    return pl.pallas_call(
        body, name=name, grid=(t // ROWS,),
        in_specs=[_row_spec(g.shape[1]) for g in grads] + [_row_spec(LANES), _row_spec(LANES)],
        out_specs=_row_spec(IN_COLS),
        out_shape=pltpu.HBM((t, IN_COLS), BF16), compiler_params=_params(("arbitrary",), 40),
    )(*[_in_hbm(g) for g in grads], _in_hbm(cos), _in_hbm(sin))


GROUP_COLS = ((0, WIDTH_A), (WIDTH_A, WIDTH_A + WIDTH_B), (WIDTH_A + WIDTH_B, D_MODEL))


def _mix_fwd(oa, ob, oc, gain, name):
    t = oa.shape[0]

    def body(a_ref, b_ref, c_ref, g_ref, o_ref):
        for ref, (lo, hi) in zip((a_ref, b_ref, c_ref), GROUP_COLS):
            _, n = _rms_stats(ref[...])
            o_ref[:, lo:hi] = (n * g_ref[:, lo:hi]).astype(BF16)

    return pl.pallas_call(
        body, name=name, grid=(t // ROWS,),
        in_specs=[_row_spec(WIDTH_A), _row_spec(WIDTH_B), _row_spec(WIDTH_C), _vec_spec(D_MODEL)],
        out_specs=_row_spec(D_MODEL),
        out_shape=pltpu.HBM((t, D_MODEL), BF16), compiler_params=_params(("arbitrary",), 32),
    )(_in_hbm(oa), _in_hbm(ob), _in_hbm(oc), _in_hbm(gain))


def _mix_bwd(oa, ob, oc, gain, dmixed, name, after=()):
    t = oa.shape[0]
    n_after = len(after)

    def body(a_ref, b_ref, c_ref, g_ref, dm_ref, *rest):
        da_ref, db_ref, dc_ref, dg_ref = rest[n_after:]
        first = pl.program_id(0) == 0
        for ref, dref, (lo, hi) in zip((a_ref, b_ref, c_ref), (da_ref, db_ref, dc_ref), GROUP_COLS):
            r, n = _rms_stats(ref[...])
            dm = dm_ref[:, lo:hi]
            dn = dm * g_ref[:, lo:hi]
            dref[...] = r * (dn - n * jnp.mean(dn * n, axis=-1, keepdims=True))
            part = jnp.sum(dm * n, axis=0, keepdims=True)

            @pl.when(first)
            def _():
                dg_ref[:, lo:hi] = part

            @pl.when(jnp.logical_not(first))
            def _():
                dg_ref[:, lo:hi] += part

    return pl.pallas_call(
        body, name=name, grid=(t // ROWS,),
        in_specs=[_row_spec(WIDTH_A), _row_spec(WIDTH_B), _row_spec(WIDTH_C), _vec_spec(D_MODEL), _row_spec(D_MODEL)]
        + [ANY_SPEC] * n_after,
        out_specs=[_row_spec(WIDTH_A), _row_spec(WIDTH_B), _row_spec(WIDTH_C), _vec_spec(D_MODEL)],
        out_shape=[pltpu.HBM((t, WIDTH_A), F32), pltpu.HBM((t, WIDTH_B), F32), pltpu.HBM((t, WIDTH_C), F32),
                   jax.ShapeDtypeStruct((1, D_MODEL), F32)],
        compiler_params=_params(("arbitrary",), 40),
    )(_in_hbm(oa), _in_hbm(ob), _in_hbm(oc), _in_hbm(gain), _in_hbm(dmixed), *after)


FF_COLS = 256


SUBLANES = 8
CHUNK = 128
HALO = SUBLANES


def _pad_rows(dst_ref, src_ref):
    t, cols = src_ref.shape
    dst_ref[0:HALO, :] = jnp.zeros((HALO, cols), F32)
    dst_ref[HALO:HALO + t, :] = src_ref[...]
    dst_ref[HALO + t:t + 2 * HALO, :] = jnp.zeros((HALO, cols), F32)


def _roll_rows(x, by):
    return pltpu.roll(x, by % x.shape[0], 0)


def _gate_val(pad_ref, r0, w_ref, b_ref):
    ext = [pad_ref[h, pl.ds(r0, CHUNK + 2 * HALO), :] for h in range(2)]
    before = [_roll_rows(e, 1) for e in ext]
    after = [_roll_rows(e, -1) for e in ext]
    gate, val = ((before[h] * w_ref[h, 0:1, :] + ext[h] * w_ref[h, 1:2, :]) + after[h] * w_ref[h, 2:3, :] + b_ref[h]
                 for h in range(2))
    return gate, val, ext, before, after


def _ff_specs(t):
    u_spec = pl.BlockSpec((2, t, FF_COLS), lambda j: (0, 0, j))
    w_spec = pl.BlockSpec((2, 3, FF_COLS), lambda j: (0, 0, j))
    b_spec = pl.BlockSpec((2, 1, FF_COLS), lambda j: (0, 0, j))
    return u_spec, w_spec, b_spec


def _convgate_fwd(u0, conv_w, conv_b, name):
    t = u0.shape[1]
    u_spec, w_spec, b_spec = _ff_specs(t)

    def body(u_ref, w_ref, b_ref, o_ref, pad_ref):
        for h in range(2):
            _pad_rows(pad_ref.at[h], u_ref.at[h])

        def chunk(ci, carry):
            r0 = pl.multiple_of(ci * CHUNK, CHUNK)
            gate, val, _, _, _ = _gate_val(pad_ref, r0, w_ref, b_ref)
            act = gate * jax.nn.sigmoid(gate) * val
            o_ref[pl.ds(r0, CHUNK), :] = act[HALO:HALO + CHUNK].astype(BF16)
            return carry

        lax.fori_loop(0, t // CHUNK, chunk, 0)

    return pl.pallas_call(
        body, name=name, grid=(D_FF // FF_COLS,), in_specs=[u_spec, w_spec, b_spec],
        out_specs=pl.BlockSpec((t, FF_COLS), lambda j: (0, j)),
        out_shape=pltpu.HBM((t, D_FF), BF16),
        scratch_shapes=[pltpu.VMEM((2, t + 2 * HALO, FF_COLS), F32)],
        compiler_params=_params(("arbitrary",), 48),
    )(_in_hbm(u0), conv_w, conv_b)


def _convgate_bwd(u0, conv_w, conv_b, d_act, name):
    t = u0.shape[1]
    u_spec, w_spec, b_spec = _ff_specs(t)

    def body(u_ref, w_ref, b_ref, da_ref, du_ref, dw_ref, db_ref, pad_ref, da_pad_ref, sums_ref):
        for h in range(2):
            _pad_rows(pad_ref.at[h], u_ref.at[h])
        _pad_rows(da_pad_ref, da_ref)
        sums_ref[...] = jnp.zeros_like(sums_ref)
        inner = slice(HALO, HALO + CHUNK)

        def fold(x):
            return jnp.sum(x.reshape(CHUNK // SUBLANES, SUBLANES, x.shape[1]), axis=0)

        def chunk(ci, carry):
            r0 = pl.multiple_of(ci * CHUNK, CHUNK)
            gate, val, ext, before, after = _gate_val(pad_ref, r0, w_ref, b_ref)
            sig = jax.nn.sigmoid(gate)
            da = da_pad_ref[pl.ds(r0, CHUNK + 2 * HALO), :]
            d_half = (da * val * (sig * (1.0 + gate * (1.0 - sig))), da * (gate * sig))
            for h in range(2):
                du = d_half[h]
                for k, term in enumerate((du, du * before[h], du * ext[h], du * after[h])):
                    sums_ref[h, k] += fold(term[inner])
                du0 = (_roll_rows(du, -1) * w_ref[h, 0:1, :] + du * w_ref[h, 1:2, :]) + _roll_rows(du, 1) * w_ref[h, 2:3, :]
                du_ref[h, pl.ds(r0, CHUNK), :] = du0[inner].astype(BF16)
            return carry

        lax.fori_loop(0, t // CHUNK, chunk, 0)
        for h in range(2):
            db_ref[h] = jnp.sum(sums_ref[h, 0], axis=0, keepdims=True)
            for k in range(3):
                dw_ref[h, k:k + 1, :] = jnp.sum(sums_ref[h, k + 1], axis=0, keepdims=True)

    return pl.pallas_call(
        body, name=name, grid=(D_FF // FF_COLS,),
        in_specs=[u_spec, w_spec, b_spec, pl.BlockSpec((t, FF_COLS), lambda j: (0, j))],
        out_specs=[u_spec, w_spec, b_spec],
        out_shape=[pltpu.HBM((2, t, D_FF), BF16), jax.ShapeDtypeStruct((2, 3, D_FF), F32),
                   jax.ShapeDtypeStruct((2, 1, D_FF), F32)],
        scratch_shapes=[pltpu.VMEM((2, t + 2 * HALO, FF_COLS), F32), pltpu.VMEM((t + 2 * HALO, FF_COLS), F32),
                        pltpu.VMEM((2, 4, SUBLANES, FF_COLS), F32)],
        compiler_params=_params(("arbitrary",), 56),
    )(_in_hbm(u0), conv_w, conv_b, _in_hbm(d_act))


class _Group:
    def __init__(self, heads, blks, kv_rows, n_win, gqa, bias_per_head):
        self.heads = heads
        self.pairs = heads // 2
        self.q_blk, self.k_blk, self.v_blk = blks
        self.kv_rows = kv_rows
        self.n_win = n_win
        self.full = kv_rows == SEQ
        self.gqa = gqa
        self.bias_per_head = bias_per_head
        self.width = heads * HEAD_DIM
        self.keys = kv_rows * n_win


GROUP_A = _Group(N_HEADS_A, A_BLK, SEQ, 1, False, False)
GROUP_B = _Group(N_HEADS_B, B_BLK, WINDOW_B, 4, True, False)
GROUP_C = _Group(N_HEADS_C, C_BLK, QB, 3, False, True)


def _win_start(grp, i):
    return jnp.clip(i * (QB // grp.kv_rows) - 1, 0, SEQ // grp.kv_rows - grp.n_win)


def _win_variant(i):
    return jnp.minimum(i, 1) + (i == NQB - 1).astype(jnp.int32)


def _attn_in_specs(grp, t):
    q_spec = pl.BlockSpec((QB, LANES), lambda p, i: (i, grp.q_blk + p))

    def col(blk):
        return (lambda p: blk) if grp.gqa else (lambda p: blk + p)

    def kv_specs(blk):
        c = col(blk)
        if grp.full:
            return [pl.BlockSpec((t, LANES), lambda p, i: (0, c(p)))]
        return [pl.BlockSpec((grp.kv_rows, LANES),
                             functools.partial(lambda p, i, w: (_win_start(grp, i) + w, c(p)), w=w))
                for w in range(grp.n_win)]

    nwk = grp.keys
    if grp.bias_per_head:
        bias_spec = pl.BlockSpec((2, None, QB, nwk), lambda p, i: (p, _win_variant(i), 0, 0))
    elif grp.full:
        bias_spec = pl.BlockSpec((1, None, QB, nwk), lambda p, i: (0, i, 0, 0))
    else:
        bias_spec = pl.BlockSpec((1, None, QB, nwk), lambda p, i: (0, _win_variant(i), 0, 0))
    sink_spec = pl.BlockSpec((1, LANES), lambda p, i: (0, p))
    return q_spec, kv_specs(grp.k_blk), kv_specs(grp.v_blk), bias_spec, sink_spec


def _head_kv(grp, whole, e, p):
    lo, hi = whole[:, :HEAD_DIM], whole[:, HEAD_DIM:]
    if grp.gqa:
        return jnp.where(2 * p + e >= N_HEADS_B // N_KV_B, hi, lo)
    return hi if e else lo


def _softmax_parts(q, k, bias, sink):
    s = lax.dot_general(q, k, (((1,), (1,)), ((), ())), preferred_element_type=F32) + bias
    m = jnp.maximum(jnp.max(s, axis=-1, keepdims=True), sink)
    pe = jnp.exp(s - m)
    denom = jnp.sum(pe, axis=-1, keepdims=True) + jnp.exp(sink - m)
    return pe, m, 1.0 / denom


def _attn_fwd(grp, proj, bias, sink, name):
    t = proj.shape[0]
    q_spec, k_specs, v_specs, bias_spec, sink_spec = _attn_in_specs(grp, t)
    nkv = len(k_specs)

    def body(*refs):
        q_ref = refs[0]
        k_refs, v_refs = refs[1:1 + nkv], refs[1 + nkv:1 + 2 * nkv]
        bias_ref, sink_ref, o_ref = refs[1 + 2 * nkv:4 + 2 * nkv]
        p = pl.program_id(0)
        k_all = jnp.concatenate([r[...] for r in k_refs], axis=0)
        v_all = jnp.concatenate([r[...] for r in v_refs], axis=0)
        outs = []
        for e in range(2):
            q = q_ref[:, e * HEAD_DIM:(e + 1) * HEAD_DIM]
            k = _head_kv(grp, k_all, e, p)
            v = _head_kv(grp, v_all, e, p)
            snk = sink_ref[0:1, e * HEAD_DIM:e * HEAD_DIM + 1]
            pe, _, inv = _softmax_parts(q, k, bias_ref[e if grp.bias_per_head else 0], snk)
            outs.append(jnp.dot(pe.astype(BF16), v, preferred_element_type=F32) * inv)
        o_ref[...] = jnp.concatenate(outs, axis=1)

    return pl.pallas_call(
        body, name=name, grid=(grp.pairs, NQB),
        in_specs=[q_spec, *k_specs, *v_specs, bias_spec, sink_spec],
        out_specs=pl.BlockSpec((QB, LANES), lambda p, i: (i, p)),
        out_shape=pltpu.HBM((t, grp.width), F32),
        compiler_params=_params(("arbitrary", "arbitrary"), 48),
    )(*([_in_hbm(proj)] * (1 + 2 * nkv)), _in_hbm(bias), sink)


def _attn_bwd(grp, proj, bias, sink, out, d_out, name):
    t = proj.shape[0]
    q_spec, k_specs, v_specs, bias_spec, sink_spec = _attn_in_specs(grp, t)
    nkv = len(k_specs)
    n_off = 2 * NA_ROWS - 1
    rows_q = QB // GRID_W
    wide = grp.keys > 2 * QB
    o_spec = pl.BlockSpec((QB, LANES), lambda p, i: (i, p))
    acc_spec = pl.BlockSpec((t, LANES), lambda p, i: (0, p))
    out_specs = [o_spec, acc_spec, acc_spec, pl.BlockSpec((None, 8, LANES), lambda p, i: (p, 0, 0))]
    out_shape = [pltpu.HBM((t, grp.width), F32)] * 3 + [jax.ShapeDtypeStruct((grp.pairs, 8, LANES), F32)]
    if grp.bias_per_head:
        out_specs.append(pl.BlockSpec((2, n_off, GRID_W, GRID_W), lambda p, i: (p, 0, 0, 0)))
        out_shape.append(jax.ShapeDtypeStruct((grp.heads, n_off, GRID_W, GRID_W), F32))

    def body(*refs):
        q_ref = refs[0]
        k_refs, v_refs = refs[1:1 + nkv], refs[1 + nkv:1 + 2 * nkv]
        bias_ref, sink_ref, o_ref, do_ref = refs[1 + 2 * nkv:5 + 2 * nkv]
        dq_ref, dk_ref, dv_ref, dsink_ref = refs[5 + 2 * nkv:9 + 2 * nkv]
        dbias_ref = refs[9 + 2 * nkv] if grp.bias_per_head else None
        p, i = pl.program_id(0), pl.program_id(1)

        @pl.when(i == 0)
        def _():
            dk_ref[...] = jnp.zeros_like(dk_ref)
            dv_ref[...] = jnp.zeros_like(dv_ref)
            dsink_ref[...] = jnp.zeros_like(dsink_ref)
            if dbias_ref is not None:
                dbias_ref[...] = jnp.zeros_like(dbias_ref)

        k_all = jnp.concatenate([r[...] for r in k_refs], axis=0)
        v_all = jnp.concatenate([r[...] for r in v_refs], axis=0)
        start = 0 if grp.full else _win_start(grp, i)
        dqs, dks, dvs, dsinks = [], [], [], []
        for e in range(2):
            cols = slice(e * HEAD_DIM, (e + 1) * HEAD_DIM)
            q = q_ref[:, cols]
            k = _head_kv(grp, k_all, e, p)
            v = _head_kv(grp, v_all, e, p)
            snk = sink_ref[0:1, e * HEAD_DIM:e * HEAD_DIM + 1]
            pe, m, inv = _softmax_parts(q, k, bias_ref[e if grp.bias_per_head else 0], snk)
            prob = pe * inv
            do = do_ref[:, cols]
            do_b = do.astype(BF16)
            delta = jnp.sum(do * o_ref[:, cols], axis=-1, keepdims=True)
            dp = lax.dot_general(do_b, v, (((1,), (1,)), ((), ())), preferred_element_type=F32)
            ds = prob * (dp - delta)
            ds_b = ds.astype(BF16)
            dqs.append(jnp.dot(ds_b, k, preferred_element_type=F32))
            if wide:
                dks.append(lax.dot_general(q, ds_b, (((0,), (0,)), ((), ())), preferred_element_type=F32))
                dvs.append(lax.dot_general(do_b, prob.astype(BF16), (((0,), (0,)), ((), ())),
                                           preferred_element_type=F32))
            else:
                dks.append(lax.dot_general(ds_b, q, (((0,), (0,)), ((), ())), preferred_element_type=F32))
                dvs.append(lax.dot_general(prob.astype(BF16), do_b, (((0,), (0,)), ((), ())),
                                           preferred_element_type=F32))
            dsinks.append(-jnp.sum(jnp.exp(snk - m) * inv * delta, axis=0, keepdims=True))
            if dbias_ref is not None:
                shift = (i * QB - start * grp.kv_rows) // GRID_W
                for rq in range(rows_q):
                    for rk in range(grp.keys // GRID_W):
                        off = jnp.clip(rk - rq + (NA_ROWS - 1) - shift, 0, n_off - 1)
                        dbias_ref[e, off] += ds[rq * GRID_W:(rq + 1) * GRID_W, rk * GRID_W:(rk + 1) * GRID_W]
        dq_ref[...] = jnp.concatenate(dqs, axis=1)
        rows = pl.ds(0, t) if grp.full else pl.ds(pl.multiple_of(start * grp.kv_rows, grp.kv_rows), grp.keys)
        if wide:
            dk_ref[rows, :] += jnp.concatenate(dks, axis=0).T
            dv_ref[rows, :] += jnp.concatenate(dvs, axis=0).T
        else:
            dk_ref[rows, :] += jnp.concatenate(dks, axis=1)
            dv_ref[rows, :] += jnp.concatenate(dvs, axis=1)
        lane = lax.broadcasted_iota(jnp.int32, (8, LANES), 1)
        dsink_ref[...] += jnp.where(lane < HEAD_DIM, dsinks[0], dsinks[1])

    return pl.pallas_call(
        body, name=name, grid=(grp.pairs, NQB),
        in_specs=[q_spec, *k_specs, *v_specs, bias_spec, sink_spec, o_spec, o_spec],
        out_specs=out_specs, out_shape=out_shape,
        compiler_params=_params(("arbitrary", "arbitrary"), 56),
    )(*([_in_hbm(proj)] * (1 + 2 * nkv)), _in_hbm(bias), sink, _in_hbm(out), _in_hbm(d_out))


DILATED_CONFIGS = ((128, 1), (512, 4), (2048, 16))


def _bias_a():
    d = jnp.arange(SEQ)[None, :] - jnp.arange(SEQ)[:, None]
    mult = jnp.zeros((SEQ, SEQ), F32)
    for window, r in DILATED_CONFIGS:
        reach = (window // (2 * r)) * r
        mult = mult + ((d % r == 0) & (jnp.abs(d) <= reach)).astype(F32)
    return jnp.where(mult > 0, jnp.log(jnp.maximum(mult, 1.0)), NEG_INF).reshape(1, NQB, QB, SEQ)


def _bias_b():
    row = jnp.arange(QB)[None, :, None]
    col = jnp.arange(GROUP_B.keys)[None, None, :]
    var = jnp.arange(3)[:, None, None]
    d = col - (GROUP_B.kv_rows * var + row)
    return jnp.where(jnp.abs(d) <= WINDOW_B, 0.0, NEG_INF).astype(F32)[None]


def _offset_onehot():
    c = jnp.arange(GRID_W)[:, None, None]
    c2 = jnp.arange(GRID_W)[None, :, None]
    b = jnp.arange(LANES)[None, None, :]
    return (c2 - c + NA_COLS - 1 == b).astype(BF16).reshape(GRID_W * GRID_W, LANES)


def _split_dot(x, g):
    hi = x.astype(BF16)
    rest = x - hi.astype(F32)
    mid = rest.astype(BF16)
    lo = (rest - mid.astype(F32)).astype(BF16)
    return (jnp.dot(hi, g, preferred_element_type=F32) + jnp.dot(mid, g, preferred_element_type=F32)
            + jnp.dot(lo, g, preferred_element_type=F32))


def _table_mm(x, g, name):
    def body(x_ref, g_ref, o_ref):
        o_ref[...] = _split_dot(x_ref[...], g_ref[...])

    return pl.pallas_call(
        body, name=name, out_shape=jax.ShapeDtypeStruct((x.shape[0], g.shape[1]), F32),
        in_specs=[pl.BlockSpec(memory_space=pltpu.VMEM)] * 2, out_specs=pl.BlockSpec(memory_space=pltpu.VMEM),
        compiler_params=pltpu.CompilerParams(vmem_limit_bytes=32 * MIB),
    )(x, g)


N_OFF = 2 * NA_ROWS - 1
TABLE_ROWS = 152


def _bias_c(rpb):
    table = jnp.zeros((TABLE_ROWS, LANES), F32).at[:N_HEADS_C * N_OFF, :2 * NA_COLS - 1].set(
        rpb.reshape(N_HEADS_C * N_OFF, 2 * NA_COLS - 1))
    tiles = _table_mm(table, _offset_onehot().T, "rpb_tiles")[:N_HEADS_C * N_OFF]
    tiles = tiles.reshape(N_HEADS_C, N_OFF, GRID_W, GRID_W)
    c = jnp.arange(GRID_W)
    col_start = jnp.clip(c - NA_COLS // 2, 0, GRID_W - NA_COLS)
    col_ok = (c[None, :] >= col_start[:, None]) & (c[None, :] < col_start[:, None] + NA_COLS)
    tiles = jnp.where(col_ok, tiles, NEG_INF)
    rows_q = QB // GRID_W
    rows_k = GROUP_C.keys // GRID_W

    def body(t_ref, o_ref):
        for var in range(3):
            for rq in range(rows_q):
                r_l = rows_q * var + rq
                first = min(max(r_l - NA_ROWS // 2, 0), rows_k - NA_ROWS)
                for rk in range(rows_k):
                    if first <= rk < first + NA_ROWS:
                        tile = t_ref[rk - r_l + NA_ROWS - 1]
                    else:
                        tile = jnp.full((GRID_W, GRID_W), NEG_INF, F32)
                    o_ref[var, rq * GRID_W:(rq + 1) * GRID_W, rk * GRID_W:(rk + 1) * GRID_W] = tile

    return pl.pallas_call(
        body, name="bias_c", grid=(N_HEADS_C,),
        in_specs=[pl.BlockSpec((None, N_OFF, GRID_W, GRID_W), lambda h: (h, 0, 0, 0))],
        out_specs=pl.BlockSpec((None, 3, QB, GROUP_C.keys), lambda h: (h, 0, 0, 0)),
        out_shape=jax.ShapeDtypeStruct((N_HEADS_C, 3, QB, GROUP_C.keys), F32),
        compiler_params=_params(("arbitrary",), 32),
    )(tiles)


def _rpb_grad(d_tiles):
    flat = jnp.zeros((TABLE_ROWS, GRID_W * GRID_W), F32).at[:N_HEADS_C * N_OFF].set(
        d_tiles.reshape(N_HEADS_C * N_OFF, GRID_W * GRID_W))
    out = _table_mm(flat, _offset_onehot(), "rpb_grad")
    return out[:N_HEADS_C * N_OFF, :2 * NA_COLS - 1].reshape(N_HEADS_C, N_OFF, 2 * NA_COLS - 1)


def _sink_lanes(sink):
    return jnp.repeat(sink.astype(F32), HEAD_DIM)[None, :]


def _attention_fwd(proj_r, sink_b, bias_a, bias_b, bias_c):
    no_sink_a = jnp.full((1, WIDTH_A), NEG_INF, F32)
    no_sink_c = jnp.full((1, WIDTH_C), NEG_INF, F32)
    oa = _attn_fwd(GROUP_A, proj_r, bias_a, no_sink_a, "attn_a_fwd")
    ob = _attn_fwd(GROUP_B, proj_r, bias_b, _sink_lanes(sink_b), "attn_b_fwd")
    oc = _attn_fwd(GROUP_C, proj_r, bias_c, no_sink_c, "attn_c_fwd")
    return oa, ob, oc


def _attention_bwd(proj_r, sink_b, bias_a, bias_b, bias_c, outs, d_outs, cos, sin):
    no_sink_a = jnp.full((1, WIDTH_A), NEG_INF, F32)
    no_sink_c = jnp.full((1, WIDTH_C), NEG_INF, F32)
    dqa, dka, dva, _ = _attn_bwd(GROUP_A, proj_r, bias_a, no_sink_a, outs[0], d_outs[0], "attn_a_bwd")
    dqb, dkb, dvb, dsink = _attn_bwd(GROUP_B, proj_r, bias_b, _sink_lanes(sink_b), outs[1], d_outs[1], "attn_b_bwd")
    dqc, dkc, dvc, _, d_tiles = _attn_bwd(GROUP_C, proj_r, bias_c, no_sink_c, outs[2], d_outs[2], "attn_c_bwd")
    d_proj = _rope_bwd((dqa, dka, dva, dqb, dkb, dvb, dqc, dkc, dvc), cos, sin, "rope_bwd")
    d_sink = dsink[:, 0, :].reshape(GROUP_B.pairs, 2, HEAD_DIM)[:, :, 0].reshape(N_HEADS_B)
    return d_proj, d_sink, _rpb_grad(d_tiles)


def _adamw(w, g, m, v, name):
    r, c = w.shape
    rows = r
    for cand in (512, 256, 128, 64, 32, 16, 8):
        if r % cand == 0 and cand * c * 4 <= MIB:
            rows = cand
            break
    spec = pl.BlockSpec((rows, c), lambda i: (i, 0))

    def body(w_ref, g_ref, m_ref, v_ref, d_ref, mo_ref, vo_ref):
        d_ref[...], mo_ref[...], vo_ref[...] = _adamw_step(w_ref[...], g_ref[...], m_ref[...], v_ref[...])

    return pl.pallas_call(
        body, name=name, grid=(r // rows,), in_specs=[spec] * 4, out_specs=[spec] * 3,
        out_shape=[jax.ShapeDtypeStruct((r, c), F32)] * 3, compiler_params=_params(("arbitrary",), 32),
    )(w, g, m, v)


def _adamw_step(w, grad, m, v):
    m_new = ADAM_B1 * m + (1.0 - ADAM_B1) * grad
    v_new = ADAM_B2 * v + (1.0 - ADAM_B2) * jnp.square(grad)
    m_hat = m_new / (1.0 - ADAM_B1 ** ADAM_STEP)
    v_hat = v_new / (1.0 - ADAM_B2 ** ADAM_STEP)
    return -ADAM_LR * (m_hat / (jnp.sqrt(v_hat) + ADAM_EPS) + ADAM_WD * w), m_new, v_new


def _adamw_layer(w, g, m, v, layer, prev, name):
    _, r, c = w.shape
    rows = next(cand for cand in (512, 256, 128, 64, 32, 16, 8) if r % cand == 0 and cand * c * 4 <= 2 * MIB)
    spec = pl.BlockSpec((None, rows, c), lambda i: (layer, i, 0))
    g_spec = pl.BlockSpec((rows, c), lambda i: (i, 0))
    n_prev = 0 if prev is None else 4

    def body(w_ref, g_ref, m_ref, v_ref, *rest):
        go_ref, d_ref, mo_ref, vo_ref = rest[n_prev:]
        grad = g_ref[...]
        go_ref[...] = grad
        d_ref[...], mo_ref[...], vo_ref[...] = _adamw_step(w_ref[...], grad, m_ref[...], v_ref[...])

    return pl.pallas_call(
        body, name=name, grid=(r // rows,), in_specs=[spec, g_spec, spec, spec] + [ANY_SPEC] * n_prev,
        out_specs=[spec] * 4,
        out_shape=[jax.ShapeDtypeStruct(w.shape, F32)] * 4,
        input_output_aliases={4 + i: i for i in range(n_prev)}, compiler_params=_params(("arbitrary",), 48),
    )(w, g, m, v, *(prev or ()))


def _layer_fwd(x0, p, weight, tabs):
    h1 = _rmsnorm_fwd(x0, p["ln_attn"], "ln_attn_fwd")
    proj = _mm_nn(h1, weight("w_in", h1), cols=True, tn=256, tk=D_MODEL, out_dtype=F32, name="mm_in")
    proj_r = _rope_fwd(proj, tabs["cos"], tabs["sin"], "rope_fwd")
    outs = _attention_fwd(proj_r, p["sink_b"], tabs["bias_a"], tabs["bias_b"], p["bias_c"])
    mixed = _mix_fwd(*outs, p["mix_gain"], "mix_fwd")
    x1 = _mm_nn(mixed, weight("w_out", mixed), cols=False, tn=256, tk=D_MODEL, out_dtype=F32, name="mm_out",
                residual=x0)
    h2 = _rmsnorm_fwd(x1, p["ln_ffn"], "ln_ffn_fwd")
    u0 = _mm_nn(h2, weight("w_up", h2), cols=True, tn=256, tk=D_MODEL, out_dtype=F32, name="mm_up", out_split=2)
    act = _convgate_fwd(u0, p["conv_w"], p["conv_b"], "convgate_fwd")
    x2 = _mm_nn(act, weight("w_down", act), cols=False, tn=512, tk=D_FF // 2, out_dtype=F32, name="mm_down",
                residual=x1)
    return x2, (x0, h1, proj_r, outs, mixed, x1, h2, u0, act)


def _layer_bwd(dx2, dx2_b, saved, p, big, tabs, begin, finish, pending):
    x0, h1, proj_r, outs, mixed, x1, h2, u0, act = saved
    d_act = _mm_nt(dx2_b, big["w_down"], cols=False, to=512, tr=D_MODEL, out_dtype=F32, name="nt_down",
                   after=[pending[1]] if pending else [])
    g_down = _mm_tn(act, dx2_b, tk=D_FF // N_SHARDS, tn=512, shards=-N_SHARDS, name="tn_down")
    du0, d_conv_w, d_conv_b = _convgate_bwd(u0, p["conv_w"], p["conv_b"], d_act, "convgate_bwd")
    token = [finish(pending[0], [du0])] if pending else []
    dh2 = _mm_nt(du0, big["w_up"], cols=True, to=1024, tr=D_FF // 4, out_dtype=F32, name="nt_up", after=token)
    g_up = _mm_tn(h2, du0, tk=512, tn=D_FF // 4, shards=N_SHARDS, name="tn_up")
    first, token = begin({"w_down": g_down, "w_up": g_up})
    dx1, dx1_b, d_ln_ffn = _rmsnorm_bwd(x1, p["ln_ffn"], dh2, dx2, "ln_ffn_bwd", after=[token])
    d_mixed = _mm_nt(dx1_b, big["w_out"], cols=False, to=512, tr=D_MODEL, out_dtype=F32, name="nt_out")
    g_out = _mm_tn(mixed, dx1_b, tk=D_MODEL // N_SHARDS, tn=512, shards=-N_SHARDS, name="tn_out")
    token = finish(first, [g_out])
    *d_outs, d_mix_gain = _mix_bwd(*outs, p["mix_gain"], d_mixed, "mix_bwd", after=[token])
    d_proj, d_sink, d_rpb = _attention_bwd(proj_r, p["sink_b"], tabs["bias_a"], tabs["bias_b"], p["bias_c"], outs,
                                           d_outs, tabs["cos"], tabs["sin"])
    dh1 = _mm_nt(d_proj, big["w_in"], cols=True, to=1024, tr=IN_COLS // N_SHARDS, out_dtype=F32, name="nt_in")
    g_in = _mm_tn(h1, d_proj, tk=512, tn=IN_COLS // N_SHARDS, shards=N_SHARDS, name="tn_in")
    dx0, dx0_b, d_ln_attn = _rmsnorm_bwd(x0, p["ln_attn"], dh1, dx1, "ln_attn_bwd")
    small = {"ln_attn": d_ln_attn, "sink_b": d_sink, "rpb_c": d_rpb, "mix_gain": d_mix_gain, "ln_ffn": d_ln_ffn,
             "conv_w": d_conv_w, "conv_b": d_conv_b}
    return dx0, dx0_b, small, begin({"w_out": g_out, "w_in": g_in})


HBM_SPEC = pl.BlockSpec(memory_space=pl.ANY)


def _place():
    x, y, c = lax.axis_index("x"), lax.axis_index("y"), lax.axis_index("c")
    chips = ((1 - x, y), (x, 1 - y), (1 - x, 1 - y))
    return x, y, c, chips


def _shard_index(px, py):
    return 2 * px + py


def _remote(src, dst, send_sem, recv_sem, to):
    return pltpu.make_async_remote_copy(src_ref=src, dst_ref=dst, send_sem=send_sem, recv_sem=recv_sem,
                                        device_id=to, device_id_type=MESH)


def _own_slot(w, layer, shard, name):
    _, r, c_dim = w.shape
    rows = r
    for cand in (512, 256, 128):
        if r % cand == 0 and cand * c_dim * 4 <= 2 * MIB:
            rows = cand
            break

    def body(s_ref, w_ref, o_ref):
        o_ref[...] = w_ref[...].astype(BF16)

    return pl.pallas_call(
        body, name=name,
        grid_spec=pltpu.PrefetchScalarGridSpec(
            num_scalar_prefetch=1, grid=(r // rows,),
            in_specs=[pl.BlockSpec((None, rows, c_dim), lambda i, s: (layer, i, 0))],
            out_specs=pl.BlockSpec((None, rows, c_dim), lambda i, s: (s[0], i, 0))),
        out_shape=jax.ShapeDtypeStruct((N_SHARDS, r, c_dim), BF16),
        compiler_params=_params(("arbitrary",), 32),
    )(shard.astype(jnp.int32).reshape(1), w)


HBM_ONLY = pl.BlockSpec(memory_space=pltpu.HBM)
SEM_SPEC = pl.BlockSpec(memory_space=pltpu.SEMAPHORE)
DATAFLOW = pltpu.SideEffectType.DATAFLOW_SIDE_EFFECTING


def _in_hbm(a):
    return pltpu.with_memory_space_constraint(a, pltpu.HBM)


N_DEV = 8


def _all_gather_small(vec, name, after=()):
    n_after = len(after)

    def body(v_ref, *rest):
        o_ref, send, recv, local_sem = rest[n_after:]
        x, y, c, _ = _place()
        me = 4 * x + 2 * y + c
        local = pltpu.make_async_copy(v_ref, o_ref.at[me], local_sem)
        local.start()
        flips = [(fx, fy, fc) for fx in (0, 1) for fy in (0, 1) for fc in (0, 1)][1:]
        peers = [((1 - x) if fx else x, (1 - y) if fy else y, (1 - c) if fc else c) for fx, fy, fc in flips]
        cps = [_remote(v_ref, o_ref.at[me], send.at[k], recv.at[k], peer) for k, peer in enumerate(peers)]
        for cp in cps:
            cp.start()
        for k, (px, py, pc) in enumerate(peers):
            slot = o_ref.at[4 * px + 2 * py + pc]
            _remote(slot, slot, send.at[k], recv.at[k], (px, py, pc)).wait_recv()
        for cp in cps:
            cp.wait_send()
        local.wait()

    return pl.pallas_call(
        body, name=name, in_specs=[HBM_SPEC] * (1 + n_after), out_specs=HBM_SPEC,
        out_shape=jax.ShapeDtypeStruct((N_DEV,) + vec.shape, vec.dtype),
        scratch_shapes=[pltpu.SemaphoreType.DMA((N_DEV - 1,))] * 2 + [pltpu.SemaphoreType.DMA(())],
    )(vec, *after)


def _peers(x, y, c):
    flips = [(fx, fy, fc) for fx in (0, 1) for fy in (0, 1) for fc in (0, 1)][1:]
    return [((1 - x) if fx else x, (1 - y) if fy else y, (1 - c) if fc else c) for fx, fy, fc in flips]


def _small_start(vec, after, name):
    n_after = len(after)

    def body(v_ref, slots_ref, *rest):
        send, recv = rest[n_after], rest[n_after + 1]
        token = rest[-1]
        x, y, c, _ = _place()
        me = 4 * x + 2 * y + c
        for k, peer in enumerate(_peers(x, y, c)):
            _remote(v_ref, slots_ref.at[me], send.at[k], recv.at[k], peer).start()
        token[...] = jnp.zeros_like(token)

    slots = jax.ShapeDtypeStruct((N_DEV,) + vec.shape, vec.dtype)
    res = pl.pallas_call(
        body, name=name,
        out_shape=(pltpu.SemaphoreType.DMA((N_DEV - 1,)), pltpu.SemaphoreType.DMA((N_DEV - 1,)),
                   pltpu.HBM(vec.shape, vec.dtype), pltpu.HBM(slots.shape, slots.dtype),
                   jax.ShapeDtypeStruct((8, LANES), F32)),
        in_specs=[HBM_ONLY, HBM_ONLY] + [ANY_SPEC] * n_after,
        out_specs=(SEM_SPEC, SEM_SPEC, HBM_ONLY, HBM_ONLY, pl.BlockSpec(memory_space=pltpu.VMEM)),
        input_output_aliases={0: 2, 1: 3},
        compiler_params=pltpu.CompilerParams(has_side_effects=DATAFLOW),
    )(_in_hbm(vec), _in_hbm(lax.empty(slots.shape, slots.dtype)), *after)
    return res


def _small_wait(send, recv, vec, slots, after, name):
    def body(v_ref, slots_ref, send_ref, recv_ref, *rest):
        x, y, c, _ = _place()
        for k, (px, py, pc) in enumerate(_peers(x, y, c)):
            cp = _remote(v_ref, slots_ref.at[4 * px + 2 * py + pc], send_ref.at[k], recv_ref.at[k], (px, py, pc))
            cp.wait_send()
            cp.wait_recv()

    return pl.pallas_call(
        body, name=name, out_shape=(pltpu.HBM(vec.shape, vec.dtype), pltpu.HBM(slots.shape, slots.dtype)),
        in_specs=[HBM_ONLY, HBM_ONLY, SEM_SPEC, SEM_SPEC] + [ANY_SPEC] * len(after), out_specs=[HBM_ONLY, HBM_ONLY],
        input_output_aliases={0: 0, 1: 1},
        compiler_params=pltpu.CompilerParams(has_side_effects=DATAFLOW),
    )(vec, slots, send, recv, *after)


def _small_sum(vec, slots, name):
    rows = vec.shape[0]
    blk = min(rows, 256)
    x, y, c = lax.axis_index("x"), lax.axis_index("y"), lax.axis_index("c")
    me = (4 * x + 2 * y + c).astype(jnp.int32).reshape(1)

    def slot_spec(k):
        return pl.BlockSpec((None, blk, LANES), lambda i, w: (jnp.where(w[0] == k, (k + 1) % N_DEV, k), i, 0))

    def body(w_ref, v_ref, *rest):
        o_ref = rest[-1]
        acc = None
        for k in range(N_DEV):
            term = jnp.where(w_ref[0] == k, v_ref[...], rest[k][...])
            acc = term if acc is None else acc + term
        o_ref[...] = acc

    return pl.pallas_call(
        body, name=name,
        grid_spec=pltpu.PrefetchScalarGridSpec(
            num_scalar_prefetch=1, grid=(rows // blk,),
            in_specs=[pl.BlockSpec((blk, LANES), lambda i, w: (i, 0))] + [slot_spec(k) for k in range(N_DEV)],
            out_specs=pl.BlockSpec((blk, LANES), lambda i, w: (i, 0))),
        out_shape=jax.ShapeDtypeStruct(vec.shape, F32), compiler_params=_params(("arbitrary",), 32),
    )(me, vec, *([slots] * N_DEV))


def _half(ref, slot, c):
    half = ref.shape[1] // 2
    return ref.at[slot, pl.ds(pl.multiple_of(c * half, 8), half)]


def _gather_start(bufs, after, name):
    n = len(bufs)
    n_after = len(after)

    def body(*refs):
        ins = refs[:n]
        send, recv = refs[n + n_after], refs[n + n_after + 1]
        token = refs[-1]
        x, y, c, chips = _place()
        me = _shard_index(x, y)
        for t in range(n):
            for j, (px, py) in enumerate(chips):
                mine = _half(ins[t], me, c)
                _remote(mine, mine, send.at[t * 3 + j], recv.at[t * 3 + j], (px, py, c)).start()
        token[...] = jnp.zeros_like(token)

    thru = [pltpu.HBM(b.shape, b.dtype) for b in bufs]
    res = pl.pallas_call(
        body, name=name,
        out_shape=(pltpu.SemaphoreType.DMA((n * 3,)), pltpu.SemaphoreType.DMA((n * 3,)), *thru,
                   jax.ShapeDtypeStruct((8, LANES), F32)),
        in_specs=[HBM_ONLY] * n + [ANY_SPEC] * n_after,
        out_specs=(SEM_SPEC, SEM_SPEC, *([HBM_ONLY] * n), pl.BlockSpec(memory_space=pltpu.VMEM)),
        input_output_aliases={i: 2 + i for i in range(n)},
        compiler_params=pltpu.CompilerParams(has_side_effects=DATAFLOW),
    )(*[_in_hbm(b) for b in bufs], *after)
    return res[0], res[1], list(res[2:2 + n]), res[-1]


def _gather_wait(send, recv, bufs, after, name):
    n = len(bufs)

    def body(*refs):
        ins = refs[:n]
        send_ref, recv_ref = refs[n], refs[n + 1]
        x, y, c, chips = _place()
        me = _shard_index(x, y)
        for t in range(n):
            for j, (px, py) in enumerate(chips):
                cp = _remote(_half(ins[t], me, c), _half(ins[t], _shard_index(px, py), c), send_ref.at[t * 3 + j],
                             recv_ref.at[t * 3 + j], (px, py, c))
                cp.wait_send()
                cp.wait_recv()

    res = pl.pallas_call(
        body, name=name, out_shape=tuple(pltpu.HBM(b.shape, b.dtype) for b in bufs),
        in_specs=[HBM_ONLY] * n + [SEM_SPEC, SEM_SPEC] + [ANY_SPEC] * len(after), out_specs=[HBM_ONLY] * n,
        input_output_aliases={i: i for i in range(n)},
        compiler_params=pltpu.CompilerParams(has_side_effects=DATAFLOW),
    )(*bufs, send, recv, *after)
    return list(res)


def _gather_forward(bufs, name):
    n = len(bufs)

    def body(*refs):
        outs = refs[n:2 * n]
        send, recv = refs[2 * n:]
        x, y, c, chips = _place()
        sibling = (x, y, 1 - c)
        cps = []
        for t in range(n):
            for j, (px, py) in enumerate(chips):
                got = _half(outs[t], _shard_index(px, py), c)
                cp = _remote(got, got, send.at[t * 3 + j], recv.at[t * 3 + j], sibling)
                cp.start()
                cps.append(cp)
        for t in range(n):
            for j, (px, py) in enumerate(chips):
                theirs = _half(outs[t], _shard_index(px, py), 1 - c)
                _remote(theirs, theirs, send.at[t * 3 + j], recv.at[t * 3 + j], sibling).wait_recv()
        for cp in cps:
            cp.wait_send()

    return pl.pallas_call(
        body, name=name, in_specs=[HBM_SPEC] * n, out_specs=[HBM_SPEC] * n,
        out_shape=[jax.ShapeDtypeStruct(b.shape, b.dtype) for b in bufs],
        input_output_aliases={t: t for t in range(n)},
        scratch_shapes=[pltpu.SemaphoreType.DMA((n * 3,))] * 2,
    )(*bufs)


def _sibling_rows(ref, c):
    half = ref.shape[1] // 2
    return ref.at[:, pl.ds(pl.multiple_of((1 - c) * half, 8), half)]


def _half_exchange_start(grads, name):
    n = len(grads)

    def body(*refs):
        ins, lands = refs[:n], refs[n:2 * n]
        send, recv = refs[2 * n], refs[2 * n + 1]
        token = refs[-1]
        x, y, c, _ = _place()
        for t in range(n):
            _remote(_sibling_rows(ins[t], c), lands[t], send.at[t], recv.at[t], (x, y, 1 - c)).start()
        token[...] = jnp.zeros_like(token)

    halves = [jax.ShapeDtypeStruct((g.shape[0], g.shape[1] // 2, g.shape[2]), g.dtype) for g in grads]
    res = pl.pallas_call(
        body, name=name,
        out_shape=(pltpu.SemaphoreType.DMA((n,)), pltpu.SemaphoreType.DMA((n,)),
                   *[pltpu.HBM(g.shape, g.dtype) for g in grads], *[pltpu.HBM(h.shape, h.dtype) for h in halves],
                   jax.ShapeDtypeStruct((8, LANES), F32)),
        in_specs=[HBM_ONLY] * (2 * n),
        out_specs=(SEM_SPEC, SEM_SPEC, *([HBM_ONLY] * (2 * n)), pl.BlockSpec(memory_space=pltpu.VMEM)),
        input_output_aliases={i: 2 + i for i in range(2 * n)},
        compiler_params=pltpu.CompilerParams(has_side_effects=DATAFLOW),
    )(*[_in_hbm(g) for g in grads], *[_in_hbm(lax.empty(h.shape, h.dtype)) for h in halves])
    return res[0], res[1], list(res[2:2 + n]), list(res[2 + n:2 + 2 * n]), res[-1]


def _half_exchange_wait(send, recv, grads, lands, after, name):
    n = len(grads)

    def body(*refs):
        ins, got = refs[:n], refs[n:2 * n]
        send_ref, recv_ref = refs[2 * n], refs[2 * n + 1]
        x, y, c, _ = _place()
        for t in range(n):
            cp = _remote(_sibling_rows(ins[t], c), got[t], send_ref.at[t], recv_ref.at[t], (x, y, 1 - c))
            cp.wait_send()
            cp.wait_recv()

    res = pl.pallas_call(
        body, name=name,
        out_shape=(*[pltpu.HBM(g.shape, g.dtype) for g in grads], *[pltpu.HBM(h.shape, h.dtype) for h in lands]),
        in_specs=[HBM_ONLY] * (2 * n) + [SEM_SPEC, SEM_SPEC] + [ANY_SPEC] * len(after),
        out_specs=[HBM_ONLY] * (2 * n),
        input_output_aliases={i: i for i in range(2 * n)},
        compiler_params=pltpu.CompilerParams(has_side_effects=DATAFLOW),
    )(*grads, *lands, send, recv, *after)
    return list(res[:n]), list(res[n:])


def _half_rows(half, c_dim):
    for cand in (512, 256, 128, 64):
        if half % cand == 0 and cand * c_dim * 2 <= 2 * MIB:
            return cand
    raise ValueError((half, c_dim))


def _core_index():
    return lax.axis_index("c").astype(jnp.int32).reshape(1)


def _half_sum(own, other, name):
    s, r, c_dim = own.shape
    rows = _half_rows(r // 2, c_dim)
    per = r // 2 // rows

    def body(c_ref, a_ref, b_ref, o_ref):
        o_ref[...] = (a_ref[...].astype(F32) + b_ref[...].astype(F32)).astype(BF16)

    return pl.pallas_call(
        body, name=name,
        grid_spec=pltpu.PrefetchScalarGridSpec(
            num_scalar_prefetch=1, grid=(s, per),
            in_specs=[pl.BlockSpec((None, rows, c_dim), lambda k, i, c: (k, c[0] * per + i, 0)),
                      pl.BlockSpec((None, rows, c_dim), lambda k, i, c: (k, i, 0))],
            out_specs=pl.BlockSpec((None, rows, c_dim), lambda k, i, c: (k, i, 0))),
        out_shape=pltpu.HBM((s, r // 2, c_dim), BF16), compiler_params=_params(("arbitrary", "arbitrary"), 32),
    )(_core_index(), own, other)


def _reduce_start(pairs, name):
    n = len(pairs)

    def body(*refs):
        ins, lands = refs[:n], refs[n:2 * n]
        send, recv = refs[2 * n], refs[2 * n + 1]
        token = refs[-1]
        x, y, c, chips = _place()
        me = _shard_index(x, y)
        for t in range(n):
            for j, (px, py) in enumerate(chips):
                _remote(ins[t].at[_shard_index(px, py)], lands[t].at[me], send.at[t * 3 + j], recv.at[t * 3 + j],
                        (px, py, c)).start()
        token[...] = jnp.zeros_like(token)

    thru = [pltpu.HBM(b.shape, b.dtype) for b in pairs]
    res = pl.pallas_call(
        body, name=name,
        out_shape=(pltpu.SemaphoreType.DMA((n * 3,)), pltpu.SemaphoreType.DMA((n * 3,)), *thru, *thru,
                   jax.ShapeDtypeStruct((8, LANES), F32)),
        in_specs=[HBM_ONLY] * (2 * n),
        out_specs=(SEM_SPEC, SEM_SPEC, *([HBM_ONLY] * (2 * n)), pl.BlockSpec(memory_space=pltpu.VMEM)),
        input_output_aliases={i: 2 + i for i in range(2 * n)},
        compiler_params=pltpu.CompilerParams(has_side_effects=DATAFLOW),
    )(*[_in_hbm(b) for b in pairs], *[_in_hbm(lax.empty(b.shape, b.dtype)) for b in pairs])
    return res[0], res[1], list(res[2:2 + n]), list(res[2 + n:2 + 2 * n]), res[-1]


def _reduce_wait(send, recv, pairs, lands, after, name):
    n = len(pairs)

    def body(*refs):
        ins, got = refs[:n], refs[n:2 * n]
        send_ref, recv_ref = refs[2 * n], refs[2 * n + 1]
        x, y, c, chips = _place()
        for t in range(n):
            for j, (px, py) in enumerate(chips):
                s = _shard_index(px, py)
                cp = _remote(ins[t].at[s], got[t].at[s], send_ref.at[t * 3 + j], recv_ref.at[t * 3 + j], (px, py, c))
                cp.wait_send()
                cp.wait_recv()

    thru = [pltpu.HBM(b.shape, b.dtype) for b in pairs]
    res = pl.pallas_call(
        body, name=name, out_shape=(*thru, *thru),
        in_specs=[HBM_ONLY] * (2 * n) + [SEM_SPEC, SEM_SPEC] + [ANY_SPEC] * len(after),
        out_specs=[HBM_ONLY] * (2 * n),
        input_output_aliases={i: i for i in range(2 * n)},
        compiler_params=pltpu.CompilerParams(has_side_effects=DATAFLOW),
    )(*pairs, *lands, send, recv, *after)
    return list(res[:n]), list(res[n:])


def _reduce_sum(pair, landed, name):
    s, half, c_dim = pair.shape
    rows = _half_rows(half, c_dim)
    per = half // rows
    shard = _shard_index(lax.axis_index("x"), lax.axis_index("y"))
    where = jnp.stack([shard, lax.axis_index("c")]).astype(jnp.int32)

    def landed_spec(k):
        return pl.BlockSpec((None, rows, c_dim), lambda i, w: (jnp.where(w[0] == k, (k + 1) % s, k), i, 0))

    def body(w_ref, own_ref, *rest):
        o_ref = rest[-1]
        acc = None
        for k in range(s):
            term = jnp.where(w_ref[0] == k, own_ref[...], rest[k][...]).astype(F32)
            acc = term if acc is None else acc + term
        o_ref[...] = acc

    return pl.pallas_call(
        body, name=name,
        grid_spec=pltpu.PrefetchScalarGridSpec(
            num_scalar_prefetch=1, grid=(per,),
            in_specs=[pl.BlockSpec((None, rows, c_dim), lambda i, w: (w[0], i, 0))] + [landed_spec(k) for k in range(s)],
            out_specs=pl.BlockSpec((rows, c_dim), lambda i, w: (w[1] * per + i, 0))),
        out_shape=pltpu.HBM((2 * half, c_dim), F32), compiler_params=_params(("arbitrary",), 40),
    )(where, pair, *([landed] * s))


def _my_rows(ref, c):
    half = ref.shape[0] // 2
    return ref.at[pl.ds(pl.multiple_of(c * half, 8), half)]


def _half_gather_start(bufs, name):
    n = len(bufs)

    def body(*refs):
        ins = refs[:n]
        send, recv = refs[n], refs[n + 1]
        token = refs[-1]
        x, y, c, _ = _place()
        for t in range(n):
            mine = _my_rows(ins[t], c)
            _remote(mine, mine, send.at[t], recv.at[t], (x, y, 1 - c)).start()
        token[...] = jnp.zeros_like(token)

    res = pl.pallas_call(
        body, name=name,
        out_shape=(pltpu.SemaphoreType.DMA((n,)), pltpu.SemaphoreType.DMA((n,)),
                   *[pltpu.HBM(b.shape, b.dtype) for b in bufs], jax.ShapeDtypeStruct((8, LANES), F32)),
        in_specs=[HBM_ONLY] * n,
        out_specs=(SEM_SPEC, SEM_SPEC, *([HBM_ONLY] * n), pl.BlockSpec(memory_space=pltpu.VMEM)),
        input_output_aliases={i: 2 + i for i in range(n)},
        compiler_params=pltpu.CompilerParams(has_side_effects=DATAFLOW),
    )(*[_in_hbm(b) for b in bufs])
    return res[0], res[1], list(res[2:2 + n]), res[-1]


def _half_gather_wait(send, recv, bufs, after, name):
    n = len(bufs)

    def body(*refs):
        ins = refs[:n]
        send_ref, recv_ref = refs[n], refs[n + 1]
        x, y, c, _ = _place()
        for t in range(n):
            cp = _remote(_my_rows(ins[t], c), _my_rows(ins[t], 1 - c), send_ref.at[t], recv_ref.at[t], (x, y, 1 - c))
            cp.wait_send()
            cp.wait_recv()

    res = pl.pallas_call(
        body, name=name, out_shape=tuple(pltpu.HBM(b.shape, b.dtype) for b in bufs),
        in_specs=[HBM_ONLY] * n + [SEM_SPEC, SEM_SPEC] + [ANY_SPEC] * len(after), out_specs=[HBM_ONLY] * n,
        input_output_aliases={i: i for i in range(n)},
        compiler_params=pltpu.CompilerParams(has_side_effects=DATAFLOW),
    )(*bufs, send, recv, *after)
    return list(res)


WEIGHT_NAMES = ("ln_attn", "w_in", "sink_b", "rpb_c", "mix_gain", "w_out", "ln_ffn", "w_up", "conv_w", "conv_b",
                "w_down", "ln_final")
BIG_NAMES = ("w_in", "w_out", "w_up", "w_down")
REPLICATED_NAMES = ("ln_attn", "sink_b", "rpb_c", "mix_gain", "ln_ffn", "conv_b", "ln_final")
PACK_TILE = 8 * LANES


def _pack(arrays, row_multiple):
    pieces = []
    for a in arrays:
        flat = a.reshape(-1)
        pieces.append(jnp.pad(flat, (0, (-flat.shape[0]) % PACK_TILE)))
    flat = jnp.concatenate(pieces)
    flat = jnp.pad(flat, (0, (-flat.shape[0]) % (row_multiple * LANES)))
    return flat.reshape(-1, LANES)


def _unpack(packed, shapes):
    flat = packed.reshape(-1)
    out, off = [], 0
    for shape in shapes:
        size = math.prod(shape)
        out.append(flat[off:off + size].reshape(shape))
        off += size + (-size) % PACK_TILE
    return out


def kernel(x, ln_attn, w_in, sink_b, rpb_c, mix_gain, w_out, ln_ffn, w_up, conv_w, conv_b, w_down, ln_final, loss_target, m_ln_attn, m_w_in, m_sink_b, m_rpb_c, m_mix_gain, m_w_out, m_ln_ffn, m_w_up, m_conv_w, m_conv_b, m_w_down, m_ln_final, v_ln_attn, v_w_in, v_sink_b, v_rpb_c, v_mix_gain, v_w_out, v_ln_ffn, v_w_up, v_conv_w, v_conv_b, v_w_down, v_ln_final):
    w = dict(ln_attn=ln_attn, w_in=w_in, sink_b=sink_b, rpb_c=rpb_c, mix_gain=mix_gain, w_out=w_out, ln_ffn=ln_ffn,
             w_up=w_up, conv_w=conv_w, conv_b=conv_b, w_down=w_down, ln_final=ln_final)
    m = dict(ln_attn=m_ln_attn, w_in=m_w_in, sink_b=m_sink_b, rpb_c=m_rpb_c, mix_gain=m_mix_gain, w_out=m_w_out,
             ln_ffn=m_ln_ffn, w_up=m_w_up, conv_w=m_conv_w, conv_b=m_conv_b, w_down=m_w_down, ln_final=m_ln_final)
    v = dict(ln_attn=v_ln_attn, w_in=v_w_in, sink_b=v_sink_b, rpb_c=v_rpb_c, mix_gain=v_mix_gain, w_out=v_w_out,
             ln_ffn=v_ln_ffn, w_up=v_w_up, conv_w=v_conv_w, conv_b=v_conv_b, w_down=v_w_down, ln_final=v_ln_final)
    shard = _shard_index(lax.axis_index("x"), lax.axis_index("y"))
    up_cols = w_up.shape[2]

    conv_slots = _all_gather_small(_pack([conv_w], 8), "gather_conv_w")
    conv_all = conv_slots[0::2].reshape(N_SHARDS, -1)[:, :conv_w.size].reshape((N_SHARDS,) + conv_w.shape)

    arrivals = []
    group_of = {}
    tokens = []
    rest = ("w_out", "w_up", "w_down")
    for l, names in ((0, ("w_in",)), (0, rest), (1, ("w_in",)), (1, rest)):
        bufs = [_own_slot(w[k], l, shard, "own_" + k) for k in names]
        send, recv, bufs, token = _gather_start(bufs, tokens[-1:] or [conv_slots], "gather_start_%d" % len(arrivals))
        tokens.append(token)
        for k in names:
            group_of[l, k] = len(arrivals)
        arrivals.append({"names": names, "send": send, "recv": recv, "bufs": bufs, "done": None})

    def gathered(l, name, after):
        idx = group_of[l, name]
        group = arrivals[idx]
        if group["done"] is None:
            bufs = _gather_wait(group["send"], group["recv"], group["bufs"], list(after) + tokens[-1:],
                                "gather_wait_%d" % idx)
            done = dict(zip(group["names"], _gather_forward(bufs, "gather_forward_%d" % idx)))
            for k in ("w_out", "w_down"):
                if k in done:
                    done[k] = done[k].reshape(1, -1, done[k].shape[2])
            group["done"] = done
        return group["done"][name]

    cos, sin = _rope_tables(SEQ)
    tabs = {"cos": cos, "sin": sin, "bias_a": _bias_a(), "bias_b": _bias_b()}
    layers = []
    for l in range(DEPTH):
        conv_w_l = conv_all[:, l].reshape(2, N_SHARDS // 2, 3, up_cols).transpose(0, 2, 1, 3).reshape(2, 3, D_FF)
        layers.append({"ln_attn": ln_attn[l][None], "sink_b": sink_b[l], "bias_c": _bias_c(rpb_c[l]),
                       "mix_gain": mix_gain[l][None], "ln_ffn": ln_ffn[l][None], "conv_w": conv_w_l,
                       "conv_b": conv_b[l].reshape(2, 1, D_FF)})

    act = x[0]
    saved = []
    for l in range(DEPTH):
        act, keep = _layer_fwd(act, layers[l], lambda name, after, l=l: gathered(l, name, [after]), tabs)
        saved.append(keep)
    loss_part, dx, dx_b, d_ln_final = _loss_head(act, ln_final[None], loss_target[0], "loss_head")
    loss = lax.psum(loss_part[0, 0], ("x", "y", "c"))

    reductions = []

    opened = [0]

    def begin(l, partial):
        idx = opened[0]
        opened[0] += 1
        names = tuple(partial)
        send_sem, recv_sem, mine, theirs, token = _half_exchange_start([partial[k] for k in names],
                                                                       "half_exchange_start_%d" % idx)
        return {"idx": idx, "layer": l, "names": names, "send": send_sem, "recv": recv_sem, "mine": mine,
                "theirs": theirs}, token

    def finish(handle, after):
        idx, names = handle["idx"], handle["names"]
        mine, theirs = _half_exchange_wait(handle["send"], handle["recv"], handle["mine"], handle["theirs"], after,
                                           "half_exchange_wait_%d" % idx)
        pairs = [_half_sum(a, b, "half_sum_" + k) for k, a, b in zip(names, mine, theirs)]
        send_sem, recv_sem, pairs, lands, token = _reduce_start(pairs, "reduce_start_%d" % idx)
        reductions.append({"layer": handle["layer"], "names": names, "send": send_sem, "recv": recv_sem,
                           "pairs": pairs, "lands": lands})
        return token

    small = [None] * DEPTH
    pending = None
    for l in reversed(range(DEPTH)):
        big = {k: gathered(l, k, []) for k in BIG_NAMES}
        dx, dx_b, small[l], pending = _layer_bwd(dx, dx_b, saved[l], layers[l], big, tabs,
                                                 functools.partial(begin, l), finish, pending)
    after = [finish(pending[0], [pending[1]])]

    stacked = {k: jnp.stack([small[l][k] for l in range(DEPTH)]) for k in small[0]}
    part = {"ln_attn": stacked["ln_attn"][:, 0], "sink_b": stacked["sink_b"], "rpb_c": stacked["rpb_c"],
            "mix_gain": stacked["mix_gain"][:, 0], "ln_ffn": stacked["ln_ffn"][:, 0],
            "conv_b": stacked["conv_b"].reshape(DEPTH, 2 * D_FF), "ln_final": d_ln_final[0],
            "conv_w": stacked["conv_w"].transpose(0, 2, 1, 3).reshape(DEPTH, 3, 2 * D_FF)}
    small_names = REPLICATED_NAMES + ("conv_w",)
    small_send, small_recv, small_vec, small_slots, token = _small_start(
        _pack([part[k] for k in small_names], 256), after, "small_grads_start")
    after = [token]

    grads, delta, new_m, new_v = {}, {}, {}, {}
    updated = dict.fromkeys(BIG_NAMES)

    def arrive(idx, after):
        group = reductions[idx]
        pairs, lands = _reduce_wait(group["send"], group["recv"], group["pairs"], group["lands"], after,
                                    "reduce_wait_%d" % idx)
        halves = [_reduce_sum(pair, landed, "reduce_sum_" + k) for k, pair, landed in zip(group["names"], pairs, lands)]
        send_sem, recv_sem, halves, token = _half_gather_start(halves, "half_gather_start_%d" % idx)
        return {"idx": idx, "send": send_sem, "recv": recv_sem, "bufs": halves, "names": group["names"],
                "layer": group["layer"]}, [token]

    def update(swap, after):
        whole = _half_gather_wait(swap["send"], swap["recv"], swap["bufs"], after,
                                  "half_gather_wait_%d" % swap["idx"])
        for k, g in zip(swap["names"], whole):
            updated[k] = _adamw_layer(w[k], g, m[k], v[k], swap["layer"], updated[k], "adamw_" + k)
        return [updated[k][0] for k in swap["names"]]

    swaps = []
    for idx in range(len(reductions) - 1):
        swap, after = arrive(idx, after)
        swaps.append(swap)
    for swap in swaps[:2]:
        after = update(swap, after)
    swap, after = arrive(len(reductions) - 1, after)
    for swap in swaps[2:] + [swap]:
        after = update(swap, after)
    for k in BIG_NAMES:
        grads[k], delta[k], new_m[k], new_v[k] = updated[k]

    small_vec, small_slots = _small_wait(small_send, small_recv, small_vec, small_slots, after, "small_grads_wait")
    total = _small_sum(small_vec, small_slots, "small_grads_sum")
    for k, g in zip(small_names, _unpack(total, [part[k].shape for k in small_names])):
        grads[k] = g
    grads["conv_w"] = lax.dynamic_slice_in_dim(grads["conv_w"], shard * up_cols, up_cols, axis=2)

    flat = (DEPTH * 3, up_cols)
    res = _adamw(conv_w.reshape(flat), grads["conv_w"].reshape(flat), m["conv_w"].reshape(flat),
                 v["conv_w"].reshape(flat), "adamw_conv_w")
    delta["conv_w"], new_m["conv_w"], new_v["conv_w"] = (r.reshape(conv_w.shape) for r in res)
    shapes = [w[k].shape for k in REPLICATED_NAMES]
    packed = [_pack([d[k] for k in REPLICATED_NAMES], 128) for d in (w, grads, m, v)]
    for d, res in zip((delta, new_m, new_v), _adamw(*packed, "adamw_small")):
        for k, r in zip(REPLICATED_NAMES, _unpack(res, shapes)):
            d[k] = r

    return (loss, dx[None], *[grads[k] for k in WEIGHT_NAMES], *[delta[k] for k in WEIGHT_NAMES],
            *[new_m[k] for k in WEIGHT_NAMES], *[new_v[k] for k in WEIGHT_NAMES])
```

```python
import functools
import math

import jax
import jax.numpy as jnp
from jax import lax
from jax.experimental import pallas as pl
from jax.experimental.pallas import tpu as pltpu

F32 = jnp.float32
BF16 = jnp.bfloat16
MESH = pl.DeviceIdType.MESH

D_MODEL = 2048
SEQ = 2048
DEPTH = 2
HEAD_DIM = 64
N_HEADS_A = 12
N_HEADS_B = 10
N_KV_B = 2
N_HEADS_C = 10
WINDOW_B = 128
GRID_W = 64
NA_ROWS = 8
NA_COLS = 16
WIDTH_A = N_HEADS_A * HEAD_DIM
WIDTH_B = N_HEADS_B * HEAD_DIM
WIDTH_C = N_HEADS_C * HEAD_DIM
IN_COLS = 5120
D_FF = 5632
ROPE_THETA = 10000.0
EPS = 1e-6
NEG_INF = -1e30
N_SHARDS = 4

ADAM_LR = 0.001
ADAM_B1 = 0.9
ADAM_B2 = 0.999
ADAM_EPS = 1e-08
ADAM_WD = 0.01
ADAM_STEP = 10

LANES = 128
QB = 256
NQB = SEQ // QB
ROWS = 256
MIB = 2 ** 20

A_BLK = (0, 6, 12)
B_BLK = (18, 23, 24)
C_BLK = (25, 30, 35)
ROPE_BLKS = tuple(range(0, 12)) + tuple(range(18, 24))
QSCALE_BLKS = tuple(range(0, 6)) + tuple(range(18, 23)) + tuple(range(25, 30))
N_PBLK = IN_COLS // LANES


def _params(sem, vmem_mib):
    return pltpu.CompilerParams(dimension_semantics=sem, vmem_limit_bytes=vmem_mib * MIB)


def _weight_spec(w, cols, t_in, t_out, transposed):
    s, r, c = w.shape
    if cols:
        per = c // t_out
        k_dim, n = r, s * c
        if transposed:
            index = lambda j, rr: (rr // per, j, rr % per)
        else:
            index = lambda j, kk: (j // per, kk, j % per)
    else:
        per = r // t_in
        k_dim, n = s * r, c
        if transposed:
            index = lambda j, rr: (j // per, j % per, rr)
        else:
            index = lambda j, kk: (kk // per, kk % per, j)
    return pl.BlockSpec((None, t_in, t_out), index), k_dim, n


def _mm_nn(a, w, *, cols, tn, tk, out_dtype, name, residual=None, out_split=1):
    m, k_dim = a.shape
    w_spec, k_w, n = _weight_spec(w, cols, tk, tn, False)
    assert k_w == k_dim
    nj, nk = n // tn, k_dim // tk
    in_specs = [pl.BlockSpec((m, tk), lambda j, k: (0, k)), w_spec]
    args = [a, w]
    if residual is not None:
        in_specs.append(pl.BlockSpec((m, tn), lambda j, k: (0, j)))
        args.append(residual)
    if out_split > 1:
        per_o = n // out_split // tn
        out_spec = pl.BlockSpec((None, m, tn), lambda j, k: (j // per_o, 0, j % per_o))
        out_shape = pltpu.HBM((out_split, m, n // out_split), out_dtype)
    else:
        out_spec = pl.BlockSpec((m, tn), lambda j, k: (0, j))
        out_shape = pltpu.HBM((m, n), out_dtype)

    def body(*refs):
        a_ref, w_ref = refs[0], refs[1]
        r_ref = refs[2] if residual is not None else None
        o_ref = refs[3] if residual is not None else refs[2]

        def finish(val):
            if r_ref is not None:
                val = r_ref[...] + val
            o_ref[...] = val.astype(o_ref.dtype)

        part = jnp.dot(a_ref[...], w_ref[...], preferred_element_type=F32)
        if nk == 1:
            finish(part)
        else:
            acc = refs[-1]
            kk = pl.program_id(1)

            @pl.when(kk == 0)
            def _():
                acc[...] = part

            @pl.when(kk > 0)
            def _():
                acc[...] += part

            @pl.when(kk == nk - 1)
            def _():
                finish(acc[...])

    return pl.pallas_call(
        body, name=name, grid=(nj, nk), in_specs=in_specs, out_specs=out_spec, out_shape=out_shape,
        scratch_shapes=[pltpu.VMEM((m, tn), F32)] if nk > 1 else [],
        compiler_params=_params(("arbitrary", "arbitrary"), 56),
    )(*[_in_hbm(a) for a in args])


ANY_SPEC = pl.BlockSpec(memory_space=pl.ANY)


def _mm_nt(dy, w, *, cols, to, tr, out_dtype, name, after=()):
    if dy.ndim == 3:
        m = dy.shape[1]
        n = dy.shape[0] * dy.shape[2]
        per_d = dy.shape[2] // tr
        dy_spec = pl.BlockSpec((None, m, tr), lambda j, r: (r // per_d, 0, r % per_d))
    else:
        m, n = dy.shape
        dy_spec = pl.BlockSpec((m, tr), lambda j, r: (0, r))
    w_spec, k_dim, n_w = _weight_spec(w, cols, to, tr, True)
    assert n_w == n
    nj, nr = k_dim // to, n // tr

    n_after = len(after)

    def body(dy_ref, w_ref, *rest):
        o_ref = rest[n_after]
        part = lax.dot_general(dy_ref[...], w_ref[...], (((1,), (1,)), ((), ())), preferred_element_type=F32)
        if nr == 1:
            o_ref[...] = part.astype(o_ref.dtype)
        else:
            acc = rest[n_after + 1]
            rr = pl.program_id(1)

            @pl.when(rr == 0)
            def _():
                acc[...] = part

            @pl.when(rr > 0)
            def _():
                acc[...] += part

            @pl.when(rr == nr - 1)
            def _():
                o_ref[...] = acc[...].astype(o_ref.dtype)

    return pl.pallas_call(
        body, name=name, grid=(nj, nr), in_specs=[dy_spec, w_spec] + [ANY_SPEC] * n_after,
        out_specs=pl.BlockSpec((m, to), lambda j, r: (0, j)),
        out_shape=pltpu.HBM((m, k_dim), out_dtype),
        scratch_shapes=[pltpu.VMEM((m, to), F32)] if nr > 1 else [],
        compiler_params=_params(("arbitrary", "arbitrary"), 56),
    )(_in_hbm(dy), _in_hbm(w), *after)


def _mm_tn(x, dy, *, tk, tn, shards, name):
    m, k_dim = x.shape
    if dy.ndim == 3:
        n = dy.shape[0] * dy.shape[2]
        per_d = dy.shape[2] // tn
        dy_spec = pl.BlockSpec((None, m, tn), lambda i, j: (j // per_d, 0, j % per_d))
    else:
        n = dy.shape[1]
        dy_spec = pl.BlockSpec((m, tn), lambda i, j: (0, j))
    if shards > 0:
        per = n // shards // tn
        out_shape = pltpu.HBM((shards, k_dim, n // shards), BF16)
        out_spec = pl.BlockSpec((None, tk, tn), lambda i, j: (j // per, i, j % per))
    else:
        s = -shards
        per = k_dim // s // tk
        out_shape = pltpu.HBM((s, k_dim // s, n), BF16)
        out_spec = pl.BlockSpec((None, tk, tn), lambda i, j: (i // per, i % per, j))

    def body(x_ref, dy_ref, o_ref):
        o_ref[...] = lax.dot_general(x_ref[...], dy_ref[...], (((0,), (0,)), ((), ())),
                                     preferred_element_type=F32).astype(BF16)

    return pl.pallas_call(
        body, name=name, grid=(k_dim // tk, n // tn),
        in_specs=[pl.BlockSpec((m, tk), lambda i, j: (0, i)), dy_spec], out_specs=out_spec, out_shape=out_shape,
        compiler_params=_params(("arbitrary", "arbitrary"), 56),
    )(_in_hbm(x), _in_hbm(dy))


def _row_spec(width, rows=ROWS):
    return pl.BlockSpec((rows, width), lambda i: (i, 0))


def _vec_spec(width):
    return pl.BlockSpec((1, width), lambda i: (0, 0))


def _rms_stats(x):
    r = lax.rsqrt(jnp.mean(x * x, axis=-1, keepdims=True) + EPS)
    return r, x * r


def _rmsnorm_fwd(x, gain, name):
    t, d = x.shape

    def body(x_ref, g_ref, o_ref):
        _, n = _rms_stats(x_ref[...])
        o_ref[...] = (n * g_ref[...]).astype(BF16)

    return pl.pallas_call(
        body, name=name, grid=(t // ROWS,), in_specs=[_row_spec(d), _vec_spec(d)], out_specs=_row_spec(d),
        out_shape=pltpu.HBM((t, d), BF16), compiler_params=_params(("arbitrary",), 32),
    )(_in_hbm(x), _in_hbm(gain))


def _rmsnorm_bwd(x, gain, dh, dres, name, after=()):
    t, d = x.shape
    n_after = len(after)

    def body(x_ref, g_ref, dh_ref, dres_ref, *rest):
        dx_ref, dxb_ref, dg_ref = rest[n_after:]
        r, n = _rms_stats(x_ref[...])
        dh_v = dh_ref[...]
        dn = dh_v * g_ref[...]
        dx = dres_ref[...] + r * (dn - n * jnp.mean(dn * n, axis=-1, keepdims=True))
        dx_ref[...] = dx
        dxb_ref[...] = dx.astype(BF16)
        part = jnp.sum(dh_v * n, axis=0, keepdims=True)

        @pl.when(pl.program_id(0) == 0)
        def _():
            dg_ref[...] = part

        @pl.when(pl.program_id(0) > 0)
        def _():
            dg_ref[...] += part

    return pl.pallas_call(
        body, name=name, grid=(t // ROWS,),
        in_specs=[_row_spec(d), _vec_spec(d), _row_spec(d), _row_spec(d)] + [ANY_SPEC] * n_after,
        out_specs=[_row_spec(d), _row_spec(d), _vec_spec(d)],
        out_shape=[pltpu.HBM((t, d), F32), pltpu.HBM((t, d), BF16), jax.ShapeDtypeStruct((1, d), F32)],
        compiler_params=_params(("arbitrary",), 40),
    )(_in_hbm(x), _in_hbm(gain), _in_hbm(dh), _in_hbm(dres), *after)


def _loss_head(x, gain, target, name):
    t, d = x.shape

    def body(x_ref, g_ref, t_ref, loss_ref, dx_ref, dxb_ref, dg_ref):
        r, n = _rms_stats(x_ref[...])
        g = g_ref[...]
        err = n * g - t_ref[...]
        dy = err * (1.0 / d)
        dn = dy * g
        dx = r * (dn - n * jnp.mean(dn * n, axis=-1, keepdims=True))
        dx_ref[...] = dx
        dxb_ref[...] = dx.astype(BF16)
        part = jnp.sum(dy * n, axis=0, keepdims=True)
        lpart = jnp.zeros((8, LANES), F32) + 0.5 * jnp.sum(jnp.mean(err * err, axis=-1, keepdims=True))

        @pl.when(pl.program_id(0) == 0)
        def _():
            dg_ref[...] = part
            loss_ref[...] = lpart

        @pl.when(pl.program_id(0) > 0)
        def _():
            dg_ref[...] += part
            loss_ref[...] += lpart

    return pl.pallas_call(
        body, name=name, grid=(t // ROWS,),
        in_specs=[_row_spec(d), _vec_spec(d), _row_spec(d)],
        out_specs=[pl.BlockSpec((8, LANES), lambda i: (0, 0)), _row_spec(d), _row_spec(d), _vec_spec(d)],
        out_shape=[jax.ShapeDtypeStruct((8, LANES), F32), pltpu.HBM((t, d), F32), pltpu.HBM((t, d), BF16),
                   jax.ShapeDtypeStruct((1, d), F32)],
        compiler_params=_params(("arbitrary",), 40),
    )(x, gain, target)


def _swap_halves(x):
    lane = lax.broadcasted_iota(jnp.int32, x.shape, 1)
    return jnp.where((lane % HEAD_DIM) < HEAD_DIM // 2, pltpu.roll(x, LANES - HEAD_DIM // 2, 1),
                     pltpu.roll(x, HEAD_DIM // 2, 1))


def _rope_tables(t):
    inv_freq = ROPE_THETA ** (-jnp.arange(0, HEAD_DIM, 2, dtype=F32) / HEAD_DIM)
    ang = jnp.arange(t, dtype=F32)[:, None] * inv_freq[None, :]
    cos = jnp.tile(jnp.cos(ang), (1, LANES // (HEAD_DIM // 2)))
    sin = jnp.tile(jnp.sin(ang), (1, LANES // (HEAD_DIM // 2)))
    lane = jnp.arange(LANES)[None, :]
    return cos, jnp.where((lane % HEAD_DIM) < HEAD_DIM // 2, -sin, sin)


def _rope_fwd(proj, cos, sin, name):
    t = proj.shape[0]
    scale = HEAD_DIM ** -0.5

    def body(p_ref, c_ref, s_ref, o_ref):
        cos_v, sin_v = c_ref[...], s_ref[...]
        for b in range(N_PBLK):
            cols = slice(b * LANES, (b + 1) * LANES)
            v = p_ref[:, cols]
            if b in ROPE_BLKS:
                v = v * cos_v + _swap_halves(v) * sin_v
            if b in QSCALE_BLKS:
                v = v * scale
            o_ref[:, cols] = v.astype(BF16)

    return pl.pallas_call(
        body, name=name, grid=(t // ROWS,),
        in_specs=[_row_spec(IN_COLS), _row_spec(LANES), _row_spec(LANES)], out_specs=_row_spec(IN_COLS),
        out_shape=pltpu.HBM((t, IN_COLS), BF16), compiler_params=_params(("arbitrary",), 40),
    )(_in_hbm(proj), _in_hbm(cos), _in_hbm(sin))


def _rope_bwd(grads, cos, sin, name):
    t = grads[0].shape[0]
    scale = HEAD_DIM ** -0.5
    group = N_HEADS_B // N_KV_B

    def body(*refs):
        c_ref, s_ref, o_ref = refs[9], refs[10], refs[11]
        cos_v, sin_v = c_ref[...], s_ref[...]

        def kv_sum(ref):
            parts = []
            for g in range(N_KV_B):
                acc = ref[:, g * group * HEAD_DIM:(g * group + 1) * HEAD_DIM]
                for h in range(g * group + 1, (g + 1) * group):
                    acc = acc + ref[:, h * HEAD_DIM:(h + 1) * HEAD_DIM]
                parts.append(acc)
            return jnp.concatenate(parts, axis=1)

        def emit(b, v):
            if b in ROPE_BLKS:
                v = v * cos_v - _swap_halves(v) * sin_v
            if b in QSCALE_BLKS:
                v = v * scale
            o_ref[:, b * LANES:(b + 1) * LANES] = v.astype(BF16)

        starts = (A_BLK[0], A_BLK[1], A_BLK[2], B_BLK[0], None, None, C_BLK[0], C_BLK[1], C_BLK[2])
        for idx, start in enumerate(starts):
            if start is None:
                continue
            for j in range(refs[idx].shape[1] // LANES):
                emit(start + j, refs[idx][:, j * LANES:(j + 1) * LANES])
        emit(B_BLK[1], kv_sum(refs[4]))
        emit(B_BLK[2], kv_sum(refs[5]))

    return pl.pallas_call(
        body, name=name, grid=(t // ROWS,),
        in_specs=[_row_spec(g.shape[1]) for g in grads] + [_row_spec(LANES), _row_spec(LANES)],
        out_specs=_row_spec(IN_COLS),
        out_shape=pltpu.HBM((t, IN_COLS), BF16), compiler_params=_params(("arbitrary",), 40),
    )(*[_in_hbm(g) for g in grads], _in_hbm(cos), _in_hbm(sin))


GROUP_COLS = ((0, WIDTH_A), (WIDTH_A, WIDTH_A + WIDTH_B), (WIDTH_A + WIDTH_B, D_MODEL))


def _mix_fwd(oa, ob, oc, gain, name):
    t = oa.shape[0]

    def body(a_ref, b_ref, c_ref, g_ref, o_ref):
        for ref, (lo, hi) in zip((a_ref, b_ref, c_ref), GROUP_COLS):
            _, n = _rms_stats(ref[...])
            o_ref[:, lo:hi] = (n * g_ref[:, lo:hi]).astype(BF16)

    return pl.pallas_call(
        body, name=name, grid=(t // ROWS,),
        in_specs=[_row_spec(WIDTH_A), _row_spec(WIDTH_B), _row_spec(WIDTH_C), _vec_spec(D_MODEL)],
        out_specs=_row_spec(D_MODEL),
        out_shape=pltpu.HBM((t, D_MODEL), BF16), compiler_params=_params(("arbitrary",), 32),
    )(_in_hbm(oa), _in_hbm(ob), _in_hbm(oc), _in_hbm(gain))


def _mix_bwd(oa, ob, oc, gain, dmixed, name, after=()):
    t = oa.shape[0]
    n_after = len(after)

    def body(a_ref, b_ref, c_ref, g_ref, dm_ref, *rest):
        da_ref, db_ref, dc_ref, dg_ref = rest[n_after:]
        first = pl.program_id(0) == 0
        for ref, dref, (lo, hi) in zip((a_ref, b_ref, c_ref), (da_ref, db_ref, dc_ref), GROUP_COLS):
            r, n = _rms_stats(ref[...])
            dm = dm_ref[:, lo:hi]
            dn = dm * g_ref[:, lo:hi]
            dref[...] = r * (dn - n * jnp.mean(dn * n, axis=-1, keepdims=True))
            part = jnp.sum(dm * n, axis=0, keepdims=True)

            @pl.when(first)
            def _():
                dg_ref[:, lo:hi] = part

            @pl.when(jnp.logical_not(first))
            def _():
                dg_ref[:, lo:hi] += part

    return pl.pallas_call(
        body, name=name, grid=(t // ROWS,),
        in_specs=[_row_spec(WIDTH_A), _row_spec(WIDTH_B), _row_spec(WIDTH_C), _vec_spec(D_MODEL), _row_spec(D_MODEL)]
        + [ANY_SPEC] * n_after,
        out_specs=[_row_spec(WIDTH_A), _row_spec(WIDTH_B), _row_spec(WIDTH_C), _vec_spec(D_MODEL)],
        out_shape=[pltpu.HBM((t, WIDTH_A), F32), pltpu.HBM((t, WIDTH_B), F32), pltpu.HBM((t, WIDTH_C), F32),
                   jax.ShapeDtypeStruct((1, D_MODEL), F32)],
        compiler_params=_params(("arbitrary",), 40),
    )(_in_hbm(oa), _in_hbm(ob), _in_hbm(oc), _in_hbm(gain), _in_hbm(dmixed), *after)


FF_COLS = 256


SUBLANES = 8
CHUNK = 128
HALO = SUBLANES


def _pad_rows(dst_ref, src_ref):
    t, cols = src_ref.shape
    dst_ref[0:HALO, :] = jnp.zeros((HALO, cols), F32)
    dst_ref[HALO:HALO + t, :] = src_ref[...]
    dst_ref[HALO + t:t + 2 * HALO, :] = jnp.zeros((HALO, cols), F32)


def _roll_rows(x, by):
    return pltpu.roll(x, by % x.shape[0], 0)


def _gate_val(pad_ref, r0, w_ref, b_ref):
    ext = [pad_ref[h, pl.ds(r0, CHUNK + 2 * HALO), :] for h in range(2)]
    before = [_roll_rows(e, 1) for e in ext]
    after = [_roll_rows(e, -1) for e in ext]
    gate, val = ((before[h] * w_ref[h, 0:1, :] + ext[h] * w_ref[h, 1:2, :]) + after[h] * w_ref[h, 2:3, :] + b_ref[h]
                 for h in range(2))
    return gate, val, ext, before, after


def _ff_specs(t):
    u_spec = pl.BlockSpec((2, t, FF_COLS), lambda j: (0, 0, j))
    w_spec = pl.BlockSpec((2, 3, FF_COLS), lambda j: (0, 0, j))
    b_spec = pl.BlockSpec((2, 1, FF_COLS), lambda j: (0, 0, j))
    return u_spec, w_spec, b_spec


def _convgate_fwd(u0, conv_w, conv_b, name):
    t = u0.shape[1]
    u_spec, w_spec, b_spec = _ff_specs(t)

    def body(u_ref, w_ref, b_ref, o_ref, pad_ref):
        for h in range(2):
            _pad_rows(pad_ref.at[h], u_ref.at[h])

        def chunk(ci, carry):
            r0 = pl.multiple_of(ci * CHUNK, CHUNK)
            gate, val, _, _, _ = _gate_val(pad_ref, r0, w_ref, b_ref)
            act = gate * jax.nn.sigmoid(gate) * val
            o_ref[pl.ds(r0, CHUNK), :] = act[HALO:HALO + CHUNK].astype(BF16)
            return carry

        lax.fori_loop(0, t // CHUNK, chunk, 0)

    return pl.pallas_call(
        body, name=name, grid=(D_FF // FF_COLS,), in_specs=[u_spec, w_spec, b_spec],
        out_specs=pl.BlockSpec((t, FF_COLS), lambda j: (0, j)),
        out_shape=pltpu.HBM((t, D_FF), BF16),
        scratch_shapes=[pltpu.VMEM((2, t + 2 * HALO, FF_COLS), F32)],
        compiler_params=_params(("arbitrary",), 48),
    )(_in_hbm(u0), conv_w, conv_b)


def _convgate_bwd(u0, conv_w, conv_b, d_act, name):
    t = u0.shape[1]
    u_spec, w_spec, b_spec = _ff_specs(t)

    def body(u_ref, w_ref, b_ref, da_ref, du_ref, dw_ref, db_ref, pad_ref, da_pad_ref, sums_ref):
        for h in range(2):
            _pad_rows(pad_ref.at[h], u_ref.at[h])
        _pad_rows(da_pad_ref, da_ref)
        sums_ref[...] = jnp.zeros_like(sums_ref)
        inner = slice(HALO, HALO + CHUNK)

        def fold(x):
            return jnp.sum(x.reshape(CHUNK // SUBLANES, SUBLANES, x.shape[1]), axis=0)

        def chunk(ci, carry):
            r0 = pl.multiple_of(ci * CHUNK, CHUNK)
            gate, val, ext, before, after = _gate_val(pad_ref, r0, w_ref, b_ref)
            sig = jax.nn.sigmoid(gate)
            da = da_pad_ref[pl.ds(r0, CHUNK + 2 * HALO), :]
            d_half = (da * val * (sig * (1.0 + gate * (1.0 - sig))), da * (gate * sig))
            for h in range(2):
                du = d_half[h]
                for k, term in enumerate((du, du * before[h], du * ext[h], du * after[h])):
                    sums_ref[h, k] += fold(term[inner])
                du0 = (_roll_rows(du, -1) * w_ref[h, 0:1, :] + du * w_ref[h, 1:2, :]) + _roll_rows(du, 1) * w_ref[h, 2:3, :]
                du_ref[h, pl.ds(r0, CHUNK), :] = du0[inner].astype(BF16)
            return carry

        lax.fori_loop(0, t // CHUNK, chunk, 0)
        for h in range(2):
            db_ref[h] = jnp.sum(sums_ref[h, 0], axis=0, keepdims=True)
            for k in range(3):
                dw_ref[h, k:k + 1, :] = jnp.sum(sums_ref[h, k + 1], axis=0, keepdims=True)

    return pl.pallas_call(
        body, name=name, grid=(D_FF // FF_COLS,),
        in_specs=[u_spec, w_spec, b_spec, pl.BlockSpec((t, FF_COLS), lambda j: (0, j))],
        out_specs=[u_spec, w_spec, b_spec],
        out_shape=[pltpu.HBM((2, t, D_FF), BF16), jax.ShapeDtypeStruct((2, 3, D_FF), F32),
                   jax.ShapeDtypeStruct((2, 1, D_FF), F32)],
        scratch_shapes=[pltpu.VMEM((2, t + 2 * HALO, FF_COLS), F32), pltpu.VMEM((t + 2 * HALO, FF_COLS), F32),
                        pltpu.VMEM((2, 4, SUBLANES, FF_COLS), F32)],
        compiler_params=_params(("arbitrary",), 56),
    )(_in_hbm(u0), conv_w, conv_b, _in_hbm(d_act))


class _Group:
    def __init__(self, heads, blks, kv_rows, n_win, gqa, bias_per_head):
        self.heads = heads
        self.pairs = heads // 2
        self.q_blk, self.k_blk, self.v_blk = blks
        self.kv_rows = kv_rows
        self.n_win = n_win
        self.full = kv_rows == SEQ
        self.gqa = gqa
        self.bias_per_head = bias_per_head
        self.width = heads * HEAD_DIM
        self.keys = kv_rows * n_win


GROUP_A = _Group(N_HEADS_A, A_BLK, SEQ, 1, False, False)
GROUP_B = _Group(N_HEADS_B, B_BLK, WINDOW_B, 4, True, False)
GROUP_C = _Group(N_HEADS_C, C_BLK, QB, 3, False, True)


def _win_start(grp, i):
    return jnp.clip(i * (QB // grp.kv_rows) - 1, 0, SEQ // grp.kv_rows - grp.n_win)


def _win_variant(i):
    return jnp.minimum(i, 1) + (i == NQB - 1).astype(jnp.int32)


def _attn_in_specs(grp, t):
    q_spec = pl.BlockSpec((QB, LANES), lambda p, i: (i, grp.q_blk + p))

    def col(blk):
        return (lambda p: blk) if grp.gqa else (lambda p: blk + p)

    def kv_specs(blk):
        c = col(blk)
        if grp.full:
            return [pl.BlockSpec((t, LANES), lambda p, i: (0, c(p)))]
        return [pl.BlockSpec((grp.kv_rows, LANES),
                             functools.partial(lambda p, i, w: (_win_start(grp, i) + w, c(p)), w=w))
                for w in range(grp.n_win)]

    nwk = grp.keys
    if grp.bias_per_head:
        bias_spec = pl.BlockSpec((2, None, QB, nwk), lambda p, i: (p, _win_variant(i), 0, 0))
    elif grp.full:
        bias_spec = pl.BlockSpec((1, None, QB, nwk), lambda p, i: (0, i, 0, 0))
    else:
        bias_spec = pl.BlockSpec((1, None, QB, nwk), lambda p, i: (0, _win_variant(i), 0, 0))
    sink_spec = pl.BlockSpec((1, LANES), lambda p, i: (0, p))
    return q_spec, kv_specs(grp.k_blk), kv_specs(grp.v_blk), bias_spec, sink_spec


def _head_kv(grp, whole, e, p):
    lo, hi = whole[:, :HEAD_DIM], whole[:, HEAD_DIM:]
    if grp.gqa:
        return jnp.where(2 * p + e >= N_HEADS_B // N_KV_B, hi, lo)
    return hi if e else lo


def _softmax_parts(q, k, bias, sink):
    s = lax.dot_general(q, k, (((1,), (1,)), ((), ())), preferred_element_type=F32) + bias
    m = jnp.maximum(jnp.max(s, axis=-1, keepdims=True), sink)
    pe = jnp.exp(s - m)
    denom = jnp.sum(pe, axis=-1, keepdims=True) + jnp.exp(sink - m)
    return pe, m, 1.0 / denom


def _attn_fwd(grp, proj, bias, sink, name):
    t = proj.shape[0]
    q_spec, k_specs, v_specs, bias_spec, sink_spec = _attn_in_specs(grp, t)
    nkv = len(k_specs)

    def body(*refs):
        q_ref = refs[0]
        k_refs, v_refs = refs[1:1 + nkv], refs[1 + nkv:1 + 2 * nkv]
        bias_ref, sink_ref, o_ref = refs[1 + 2 * nkv:4 + 2 * nkv]
        p = pl.program_id(0)
        k_all = jnp.concatenate([r[...] for r in k_refs], axis=0)
        v_all = jnp.concatenate([r[...] for r in v_refs], axis=0)
        outs = []
        for e in range(2):
            q = q_ref[:, e * HEAD_DIM:(e + 1) * HEAD_DIM]
            k = _head_kv(grp, k_all, e, p)
            v = _head_kv(grp, v_all, e, p)
            snk = sink_ref[0:1, e * HEAD_DIM:e * HEAD_DIM + 1]
            pe, _, inv = _softmax_parts(q, k, bias_ref[e if grp.bias_per_head else 0], snk)
            outs.append(jnp.dot(pe.astype(BF16), v, preferred_element_type=F32) * inv)
        o_ref[...] = jnp.concatenate(outs, axis=1)

    return pl.pallas_call(
        body, name=name, grid=(grp.pairs, NQB),
        in_specs=[q_spec, *k_specs, *v_specs, bias_spec, sink_spec],
        out_specs=pl.BlockSpec((QB, LANES), lambda p, i: (i, p)),
        out_shape=pltpu.HBM((t, grp.width), F32),
        compiler_params=_params(("arbitrary", "arbitrary"), 48),
    )(*([_in_hbm(proj)] * (1 + 2 * nkv)), _in_hbm(bias), sink)


def _attn_bwd(grp, proj, bias, sink, out, d_out, name):
    t = proj.shape[0]
    q_spec, k_specs, v_specs, bias_spec, sink_spec = _attn_in_specs(grp, t)
    nkv = len(k_specs)
    n_off = 2 * NA_ROWS - 1
    rows_q = QB // GRID_W
    wide = grp.keys > 2 * QB
    o_spec = pl.BlockSpec((QB, LANES), lambda p, i: (i, p))
    acc_spec = pl.BlockSpec((t, LANES), lambda p, i: (0, p))
    out_specs = [o_spec, acc_spec, acc_spec, pl.BlockSpec((None, 8, LANES), lambda p, i: (p, 0, 0))]
    out_shape = [pltpu.HBM((t, grp.width), F32)] * 3 + [jax.ShapeDtypeStruct((grp.pairs, 8, LANES), F32)]
    if grp.bias_per_head:
        out_specs.append(pl.BlockSpec((2, n_off, GRID_W, GRID_W), lambda p, i: (p, 0, 0, 0)))
        out_shape.append(jax.ShapeDtypeStruct((grp.heads, n_off, GRID_W, GRID_W), F32))

    def body(*refs):
        q_ref = refs[0]
        k_refs, v_refs = refs[1:1 + nkv], refs[1 + nkv:1 + 2 * nkv]
        bias_ref, sink_ref, o_ref, do_ref = refs[1 + 2 * nkv:5 + 2 * nkv]
        dq_ref, dk_ref, dv_ref, dsink_ref = refs[5 + 2 * nkv:9 + 2 * nkv]
        dbias_ref = refs[9 + 2 * nkv] if grp.bias_per_head else None
        p, i = pl.program_id(0), pl.program_id(1)

        @pl.when(i == 0)
        def _():
            dk_ref[...] = jnp.zeros_like(dk_ref)
            dv_ref[...] = jnp.zeros_like(dv_ref)
            dsink_ref[...] = jnp.zeros_like(dsink_ref)
            if dbias_ref is not None:
                dbias_ref[...] = jnp.zeros_like(dbias_ref)

        k_all = jnp.concatenate([r[...] for r in k_refs], axis=0)
        v_all = jnp.concatenate([r[...] for r in v_refs], axis=0)
        start = 0 if grp.full else _win_start(grp, i)
        dqs, dks, dvs, dsinks = [], [], [], []
        for e in range(2):
            cols = slice(e * HEAD_DIM, (e + 1) * HEAD_DIM)
            q = q_ref[:, cols]
            k = _head_kv(grp, k_all, e, p)
            v = _head_kv(grp, v_all, e, p)
            snk = sink_ref[0:1, e * HEAD_DIM:e * HEAD_DIM + 1]
            pe, m, inv = _softmax_parts(q, k, bias_ref[e if grp.bias_per_head else 0], snk)
            prob = pe * inv
            do = do_ref[:, cols]
            do_b = do.astype(BF16)
            pe_b = prob.astype(BF16)
            delta = jnp.sum(do * o_ref[:, cols], axis=-1, keepdims=True)
            dp = lax.dot_general(do_b, v, (((1,), (1,)), ((), ())), preferred_element_type=F32)
            ds = prob * (dp - delta)
            ds_b = ds.astype(BF16)
            dqs.append(jnp.dot(ds_b, k, preferred_element_type=F32))
            if wide:
                dks.append(lax.dot_general(q, ds_b, (((0,), (0,)), ((), ())), preferred_element_type=F32))
                dvs.append(lax.dot_general(do_b, pe_b, (((0,), (0,)), ((), ())), preferred_element_type=F32))
            else:
                dks.append(lax.dot_general(ds_b, q, (((0,), (0,)), ((), ())), preferred_element_type=F32))
                dvs.append(lax.dot_general(pe_b, do_b, (((0,), (0,)), ((), ())), preferred_element_type=F32))
            dsinks.append(-jnp.sum(jnp.exp(snk - m) * inv * delta, axis=0, keepdims=True))
            if dbias_ref is not None:
                shift = (i * QB - start * grp.kv_rows) // GRID_W
                for rq in range(rows_q):
                    for rk in range(grp.keys // GRID_W):
                        off = jnp.clip(rk - rq + (NA_ROWS - 1) - shift, 0, n_off - 1)
                        dbias_ref[e, off] += ds[rq * GRID_W:(rq + 1) * GRID_W, rk * GRID_W:(rk + 1) * GRID_W]
        dq_ref[...] = jnp.concatenate(dqs, axis=1)
        rows = pl.ds(0, t) if grp.full else pl.ds(pl.multiple_of(start * grp.kv_rows, grp.kv_rows), grp.keys)
        if wide:
            dk_ref[rows, :] += jnp.concatenate(dks, axis=0).T
            dv_ref[rows, :] += jnp.concatenate(dvs, axis=0).T
        else:
            dk_ref[rows, :] += jnp.concatenate(dks, axis=1)
            dv_ref[rows, :] += jnp.concatenate(dvs, axis=1)
        lane = lax.broadcasted_iota(jnp.int32, (8, LANES), 1)
        dsink_ref[...] += jnp.where(lane < HEAD_DIM, dsinks[0], dsinks[1])

    return pl.pallas_call(
        body, name=name, grid=(grp.pairs, NQB),
        in_specs=[q_spec, *k_specs, *v_specs, bias_spec, sink_spec, o_spec, o_spec],
        out_specs=out_specs, out_shape=out_shape,
        compiler_params=_params(("arbitrary", "arbitrary"), 56),
    )(*([_in_hbm(proj)] * (1 + 2 * nkv)), _in_hbm(bias), sink, _in_hbm(out), _in_hbm(d_out))


DILATED_CONFIGS = ((128, 1), (512, 4), (2048, 16))


def _bias_a():
    d = jnp.arange(SEQ)[None, :] - jnp.arange(SEQ)[:, None]
    mult = jnp.zeros((SEQ, SEQ), F32)
    for window, r in DILATED_CONFIGS:
        reach = (window // (2 * r)) * r
        mult = mult + ((d % r == 0) & (jnp.abs(d) <= reach)).astype(F32)
    return jnp.where(mult > 0, jnp.log(jnp.maximum(mult, 1.0)), NEG_INF).reshape(1, NQB, QB, SEQ)


def _bias_b():
    row = jnp.arange(QB)[None, :, None]
    col = jnp.arange(GROUP_B.keys)[None, None, :]
    var = jnp.arange(3)[:, None, None]
    d = col - (GROUP_B.kv_rows * var + row)
    return jnp.where(jnp.abs(d) <= WINDOW_B, 0.0, NEG_INF).astype(F32)[None]


def _offset_onehot():
    c = jnp.arange(GRID_W)[:, None, None]
    c2 = jnp.arange(GRID_W)[None, :, None]
    b = jnp.arange(LANES)[None, None, :]
    return (c2 - c + NA_COLS - 1 == b).astype(BF16).reshape(GRID_W * GRID_W, LANES)


def _split_dot(x, g):
    hi = x.astype(BF16)
    rest = x - hi.astype(F32)
    mid = rest.astype(BF16)
    lo = (rest - mid.astype(F32)).astype(BF16)
    return (jnp.dot(hi, g, preferred_element_type=F32) + jnp.dot(mid, g, preferred_element_type=F32)
            + jnp.dot(lo, g, preferred_element_type=F32))


def _table_mm(x, g, name):
    def body(x_ref, g_ref, o_ref):
        o_ref[...] = _split_dot(x_ref[...], g_ref[...])

    return pl.pallas_call(
        body, name=name, out_shape=jax.ShapeDtypeStruct((x.shape[0], g.shape[1]), F32),
        in_specs=[pl.BlockSpec(memory_space=pltpu.VMEM)] * 2, out_specs=pl.BlockSpec(memory_space=pltpu.VMEM),
        compiler_params=pltpu.CompilerParams(vmem_limit_bytes=32 * MIB),
    )(x, g)


N_OFF = 2 * NA_ROWS - 1
TABLE_ROWS = 152


def _bias_c(rpb):
    table = jnp.zeros((TABLE_ROWS, LANES), F32).at[:N_HEADS_C * N_OFF, :2 * NA_COLS - 1].set(
        rpb.reshape(N_HEADS_C * N_OFF, 2 * NA_COLS - 1))
    tiles = _table_mm(table, _offset_onehot().T, "rpb_tiles")[:N_HEADS_C * N_OFF]
    tiles = tiles.reshape(N_HEADS_C, N_OFF, GRID_W, GRID_W)
    c = jnp.arange(GRID_W)
    col_start = jnp.clip(c - NA_COLS // 2, 0, GRID_W - NA_COLS)
    col_ok = (c[None, :] >= col_start[:, None]) & (c[None, :] < col_start[:, None] + NA_COLS)
    tiles = jnp.where(col_ok, tiles, NEG_INF)
    rows_q = QB // GRID_W
    rows_k = GROUP_C.keys // GRID_W

    def body(t_ref, o_ref):
        for var in range(3):
            for rq in range(rows_q):
                r_l = rows_q * var + rq
                first = min(max(r_l - NA_ROWS // 2, 0), rows_k - NA_ROWS)
                for rk in range(rows_k):
                    if first <= rk < first + NA_ROWS:
                        tile = t_ref[rk - r_l + NA_ROWS - 1]
                    else:
                        tile = jnp.full((GRID_W, GRID_W), NEG_INF, F32)
                    o_ref[var, rq * GRID_W:(rq + 1) * GRID_W, rk * GRID_W:(rk + 1) * GRID_W] = tile

    return pl.pallas_call(
        body, name="bias_c", grid=(N_HEADS_C,),
        in_specs=[pl.BlockSpec((None, N_OFF, GRID_W, GRID_W), lambda h: (h, 0, 0, 0))],
        out_specs=pl.BlockSpec((None, 3, QB, GROUP_C.keys), lambda h: (h, 0, 0, 0)),
        out_shape=jax.ShapeDtypeStruct((N_HEADS_C, 3, QB, GROUP_C.keys), F32),
        compiler_params=_params(("arbitrary",), 32),
    )(tiles)


def _rpb_grad(d_tiles):
    flat = jnp.zeros((TABLE_ROWS, GRID_W * GRID_W), F32).at[:N_HEADS_C * N_OFF].set(
        d_tiles.reshape(N_HEADS_C * N_OFF, GRID_W * GRID_W))
    out = _table_mm(flat, _offset_onehot(), "rpb_grad")
    return out[:N_HEADS_C * N_OFF, :2 * NA_COLS - 1].reshape(N_HEADS_C, N_OFF, 2 * NA_COLS - 1)


def _sink_lanes(sink):
    return jnp.repeat(sink.astype(F32), HEAD_DIM)[None, :]


def _attention_fwd(proj_r, sink_b, bias_a, bias_b, bias_c):
    no_sink_a = jnp.full((1, WIDTH_A), NEG_INF, F32)
    no_sink_c = jnp.full((1, WIDTH_C), NEG_INF, F32)
    oa = _attn_fwd(GROUP_A, proj_r, bias_a, no_sink_a, "attn_a_fwd")
    ob = _attn_fwd(GROUP_B, proj_r, bias_b, _sink_lanes(sink_b), "attn_b_fwd")
    oc = _attn_fwd(GROUP_C, proj_r, bias_c, no_sink_c, "attn_c_fwd")
    return oa, ob, oc


def _attention_bwd(proj_r, sink_b, bias_a, bias_b, bias_c, outs, d_outs, cos, sin):
    no_sink_a = jnp.full((1, WIDTH_A), NEG_INF, F32)
    no_sink_c = jnp.full((1, WIDTH_C), NEG_INF, F32)
    dqa, dka, dva, _ = _attn_bwd(GROUP_A, proj_r, bias_a, no_sink_a, outs[0], d_outs[0], "attn_a_bwd")
    dqb, dkb, dvb, dsink = _attn_bwd(GROUP_B, proj_r, bias_b, _sink_lanes(sink_b), outs[1], d_outs[1], "attn_b_bwd")
    dqc, dkc, dvc, _, d_tiles = _attn_bwd(GROUP_C, proj_r, bias_c, no_sink_c, outs[2], d_outs[2], "attn_c_bwd")
    d_proj = _rope_bwd((dqa, dka, dva, dqb, dkb, dvb, dqc, dkc, dvc), cos, sin, "rope_bwd")
    d_sink = dsink[:, 0, :].reshape(GROUP_B.pairs, 2, HEAD_DIM)[:, :, 0].reshape(N_HEADS_B)
    return d_proj, d_sink, _rpb_grad(d_tiles)


def _adamw(w, g, m, v, name):
    r, c = w.shape
    rows = r
    for cand in (512, 256, 128, 64, 32, 16, 8):
        if r % cand == 0 and cand * c * 4 <= MIB:
            rows = cand
            break
    spec = pl.BlockSpec((rows, c), lambda i: (i, 0))

    def body(w_ref, g_ref, m_ref, v_ref, d_ref, mo_ref, vo_ref):
        d_ref[...], mo_ref[...], vo_ref[...] = _adamw_step(w_ref[...], g_ref[...], m_ref[...], v_ref[...])

    return pl.pallas_call(
        body, name=name, grid=(r // rows,), in_specs=[spec] * 4, out_specs=[spec] * 3,
        out_shape=[jax.ShapeDtypeStruct((r, c), F32)] * 3, compiler_params=_params(("arbitrary",), 32),
    )(w, g, m, v)


def _adamw_step(w, grad, m, v):
    m_new = ADAM_B1 * m + (1.0 - ADAM_B1) * grad
    v_new = ADAM_B2 * v + (1.0 - ADAM_B2) * jnp.square(grad)
    m_hat = m_new / (1.0 - ADAM_B1 ** ADAM_STEP)
    v_hat = v_new / (1.0 - ADAM_B2 ** ADAM_STEP)
    return -ADAM_LR * (m_hat / (jnp.sqrt(v_hat) + ADAM_EPS) + ADAM_WD * w), m_new, v_new


def _adamw_layer(w, g, m, v, layer, prev, name):
    _, r, c = w.shape
    rows = next(cand for cand in (512, 256, 128, 64, 32, 16, 8) if r % cand == 0 and cand * c * 4 <= 2 * MIB)
    spec = pl.BlockSpec((None, rows, c), lambda i: (layer, i, 0))
    g_spec = pl.BlockSpec((rows, c), lambda i: (i, 0))
    n_prev = 0 if prev is None else 4

    def body(w_ref, g_ref, m_ref, v_ref, *rest):
        go_ref, d_ref, mo_ref, vo_ref = rest[n_prev:]
        grad = g_ref[...]
        go_ref[...] = grad
        d_ref[...], mo_ref[...], vo_ref[...] = _adamw_step(w_ref[...], grad, m_ref[...], v_ref[...])

    return pl.pallas_call(
        body, name=name, grid=(r // rows,), in_specs=[spec, g_spec, spec, spec] + [ANY_SPEC] * n_prev,
        out_specs=[spec] * 4,
        out_shape=[jax.ShapeDtypeStruct(w.shape, F32)] * 4,
        input_output_aliases={4 + i: i for i in range(n_prev)}, compiler_params=_params(("arbitrary",), 48),
    )(w, g, m, v, *(prev or ()))


def _layer_fwd(x0, p, weight, tabs):
    h1 = _rmsnorm_fwd(x0, p["ln_attn"], "ln_attn_fwd")
    proj = _mm_nn(h1, weight("w_in", h1), cols=True, tn=256, tk=D_MODEL, out_dtype=F32, name="mm_in")
    proj_r = _rope_fwd(proj, tabs["cos"], tabs["sin"], "rope_fwd")
    outs = _attention_fwd(proj_r, p["sink_b"], tabs["bias_a"], tabs["bias_b"], p["bias_c"])
    mixed = _mix_fwd(*outs, p["mix_gain"], "mix_fwd")
    x1 = _mm_nn(mixed, weight("w_out", mixed), cols=False, tn=256, tk=D_MODEL, out_dtype=F32, name="mm_out",
                residual=x0)
    h2 = _rmsnorm_fwd(x1, p["ln_ffn"], "ln_ffn_fwd")
    u0 = _mm_nn(h2, weight("w_up", h2), cols=True, tn=256, tk=D_MODEL, out_dtype=F32, name="mm_up", out_split=2)
    act = _convgate_fwd(u0, p["conv_w"], p["conv_b"], "convgate_fwd")
    x2 = _mm_nn(act, weight("w_down", act), cols=False, tn=512, tk=D_FF // 2, out_dtype=F32, name="mm_down",
                residual=x1)
    return x2, (x0, h1, proj_r, outs, mixed, x1, h2, u0, act)


def _layer_bwd(dx2, dx2_b, saved, p, big, tabs, begin, finish, pending):
    x0, h1, proj_r, outs, mixed, x1, h2, u0, act = saved
    d_act = _mm_nt(dx2_b, big["w_down"], cols=False, to=512, tr=D_MODEL, out_dtype=F32, name="nt_down",
                   after=[pending[1]] if pending else [])
    g_down = _mm_tn(act, dx2_b, tk=D_FF // N_SHARDS, tn=512, shards=-N_SHARDS, name="tn_down")
    du0, d_conv_w, d_conv_b = _convgate_bwd(u0, p["conv_w"], p["conv_b"], d_act, "convgate_bwd")
    token = [finish(pending[0], [du0])] if pending else []
    dh2 = _mm_nt(du0, big["w_up"], cols=True, to=1024, tr=D_FF // 4, out_dtype=F32, name="nt_up", after=token)
    g_up = _mm_tn(h2, du0, tk=512, tn=D_FF // 4, shards=N_SHARDS, name="tn_up")
    first, token = begin({"w_down": g_down, "w_up": g_up})
    dx1, dx1_b, d_ln_ffn = _rmsnorm_bwd(x1, p["ln_ffn"], dh2, dx2, "ln_ffn_bwd", after=[token])
    d_mixed = _mm_nt(dx1_b, big["w_out"], cols=False, to=512, tr=D_MODEL, out_dtype=F32, name="nt_out")
    g_out = _mm_tn(mixed, dx1_b, tk=D_MODEL // N_SHARDS, tn=512, shards=-N_SHARDS, name="tn_out")
    token = finish(first, [g_out])
    *d_outs, d_mix_gain = _mix_bwd(*outs, p["mix_gain"], d_mixed, "mix_bwd", after=[token])
    d_proj, d_sink, d_rpb = _attention_bwd(proj_r, p["sink_b"], tabs["bias_a"], tabs["bias_b"], p["bias_c"], outs,
                                           d_outs, tabs["cos"], tabs["sin"])
    dh1 = _mm_nt(d_proj, big["w_in"], cols=True, to=1024, tr=IN_COLS // N_SHARDS, out_dtype=F32, name="nt_in")
    g_in = _mm_tn(h1, d_proj, tk=512, tn=IN_COLS // N_SHARDS, shards=N_SHARDS, name="tn_in")
    dx0, dx0_b, d_ln_attn = _rmsnorm_bwd(x0, p["ln_attn"], dh1, dx1, "ln_attn_bwd")
    small = {"ln_attn": d_ln_attn, "sink_b": d_sink, "rpb_c": d_rpb, "mix_gain": d_mix_gain, "ln_ffn": d_ln_ffn,
             "conv_w": d_conv_w, "conv_b": d_conv_b}
    return dx0, dx0_b, small, begin({"w_out": g_out, "w_in": g_in})


HBM_SPEC = pl.BlockSpec(memory_space=pl.ANY)


def _place():
    x, y, c = lax.axis_index("x"), lax.axis_index("y"), lax.axis_index("c")
    chips = ((1 - x, y), (x, 1 - y), (1 - x, 1 - y))
    return x, y, c, chips


def _shard_index(px, py):
    return 2 * px + py


def _remote(src, dst, send_sem, recv_sem, to):
    return pltpu.make_async_remote_copy(src_ref=src, dst_ref=dst, send_sem=send_sem, recv_sem=recv_sem,
                                        device_id=to, device_id_type=MESH)


def _own_slot(w, layer, shard, name):
    _, r, c_dim = w.shape
    rows = r
    for cand in (512, 256, 128):
        if r % cand == 0 and cand * c_dim * 4 <= 2 * MIB:
            rows = cand
            break

    def body(s_ref, w_ref, o_ref):
        o_ref[...] = w_ref[...].astype(BF16)

    return pl.pallas_call(
        body, name=name,
        grid_spec=pltpu.PrefetchScalarGridSpec(
            num_scalar_prefetch=1, grid=(r // rows,),
            in_specs=[pl.BlockSpec((None, rows, c_dim), lambda i, s: (layer, i, 0))],
            out_specs=pl.BlockSpec((None, rows, c_dim), lambda i, s: (s[0], i, 0))),
        out_shape=jax.ShapeDtypeStruct((N_SHARDS, r, c_dim), BF16),
        compiler_params=_params(("arbitrary",), 32),
    )(shard.astype(jnp.int32).reshape(1), w)


HBM_ONLY = pl.BlockSpec(memory_space=pltpu.HBM)
SEM_SPEC = pl.BlockSpec(memory_space=pltpu.SEMAPHORE)
DATAFLOW = pltpu.SideEffectType.DATAFLOW_SIDE_EFFECTING


def _in_hbm(a):
    return pltpu.with_memory_space_constraint(a, pltpu.HBM)


N_DEV = 8


def _all_gather_small(vec, name, after=()):
    n_after = len(after)

    def body(v_ref, *rest):
        o_ref, send, recv, local_sem = rest[n_after:]
        x, y, c, _ = _place()
        me = 4 * x + 2 * y + c
        local = pltpu.make_async_copy(v_ref, o_ref.at[me], local_sem)
        local.start()
        flips = [(fx, fy, fc) for fx in (0, 1) for fy in (0, 1) for fc in (0, 1)][1:]
        peers = [((1 - x) if fx else x, (1 - y) if fy else y, (1 - c) if fc else c) for fx, fy, fc in flips]
        cps = [_remote(v_ref, o_ref.at[me], send.at[k], recv.at[k], peer) for k, peer in enumerate(peers)]
        for cp in cps:
            cp.start()
        for k, (px, py, pc) in enumerate(peers):
            slot = o_ref.at[4 * px + 2 * py + pc]
            _remote(slot, slot, send.at[k], recv.at[k], (px, py, pc)).wait_recv()
        for cp in cps:
            cp.wait_send()
        local.wait()

    return pl.pallas_call(
        body, name=name, in_specs=[HBM_SPEC] * (1 + n_after), out_specs=HBM_SPEC,
        out_shape=jax.ShapeDtypeStruct((N_DEV,) + vec.shape, vec.dtype),
        scratch_shapes=[pltpu.SemaphoreType.DMA((N_DEV - 1,))] * 2 + [pltpu.SemaphoreType.DMA(())],
    )(vec, *after)


def _peers(x, y, c):
    flips = [(fx, fy, fc) for fx in (0, 1) for fy in (0, 1) for fc in (0, 1)][1:]
    return [((1 - x) if fx else x, (1 - y) if fy else y, (1 - c) if fc else c) for fx, fy, fc in flips]


def _small_start(vec, after, name):
    n_after = len(after)

    def body(v_ref, slots_ref, *rest):
        send, recv = rest[n_after], rest[n_after + 1]
        token = rest[-1]
        x, y, c, _ = _place()
        me = 4 * x + 2 * y + c
        for k, peer in enumerate(_peers(x, y, c)):
            _remote(v_ref, slots_ref.at[me], send.at[k], recv.at[k], peer).start()
        token[...] = jnp.zeros_like(token)

    slots = jax.ShapeDtypeStruct((N_DEV,) + vec.shape, vec.dtype)
    res = pl.pallas_call(
        body, name=name,
        out_shape=(pltpu.SemaphoreType.DMA((N_DEV - 1,)), pltpu.SemaphoreType.DMA((N_DEV - 1,)),
                   pltpu.HBM(vec.shape, vec.dtype), pltpu.HBM(slots.shape, slots.dtype),
                   jax.ShapeDtypeStruct((8, LANES), F32)),
        in_specs=[HBM_ONLY, HBM_ONLY] + [ANY_SPEC] * n_after,
        out_specs=(SEM_SPEC, SEM_SPEC, HBM_ONLY, HBM_ONLY, pl.BlockSpec(memory_space=pltpu.VMEM)),
        input_output_aliases={0: 2, 1: 3},
        compiler_params=pltpu.CompilerParams(has_side_effects=DATAFLOW),
    )(_in_hbm(vec), _in_hbm(lax.empty(slots.shape, slots.dtype)), *after)
    return res


def _small_wait(send, recv, vec, slots, after, name):
    def body(v_ref, slots_ref, send_ref, recv_ref, *rest):
        x, y, c, _ = _place()
        for k, (px, py, pc) in enumerate(_peers(x, y, c)):
            cp = _remote(v_ref, slots_ref.at[4 * px + 2 * py + pc], send_ref.at[k], recv_ref.at[k], (px, py, pc))
            cp.wait_send()
            cp.wait_recv()

    return pl.pallas_call(
        body, name=name, out_shape=(pltpu.HBM(vec.shape, vec.dtype), pltpu.HBM(slots.shape, slots.dtype)),
        in_specs=[HBM_ONLY, HBM_ONLY, SEM_SPEC, SEM_SPEC] + [ANY_SPEC] * len(after), out_specs=[HBM_ONLY, HBM_ONLY],
        input_output_aliases={0: 0, 1: 1},
        compiler_params=pltpu.CompilerParams(has_side_effects=DATAFLOW),
    )(vec, slots, send, recv, *after)


def _small_sum(vec, slots, name):
    rows = vec.shape[0]
    blk = min(rows, 256)
    x, y, c = lax.axis_index("x"), lax.axis_index("y"), lax.axis_index("c")
    me = (4 * x + 2 * y + c).astype(jnp.int32).reshape(1)

    def slot_spec(k):
        return pl.BlockSpec((None, blk, LANES), lambda i, w: (jnp.where(w[0] == k, (k + 1) % N_DEV, k), i, 0))

    def body(w_ref, v_ref, *rest):
        o_ref = rest[-1]
        acc = None
        for k in range(N_DEV):
            term = jnp.where(w_ref[0] == k, v_ref[...], rest[k][...])
            acc = term if acc is None else acc + term
        o_ref[...] = acc

    return pl.pallas_call(
        body, name=name,
        grid_spec=pltpu.PrefetchScalarGridSpec(
            num_scalar_prefetch=1, grid=(rows // blk,),
            in_specs=[pl.BlockSpec((blk, LANES), lambda i, w: (i, 0))] + [slot_spec(k) for k in range(N_DEV)],
            out_specs=pl.BlockSpec((blk, LANES), lambda i, w: (i, 0))),
        out_shape=jax.ShapeDtypeStruct(vec.shape, F32), compiler_params=_params(("arbitrary",), 32),
    )(me, vec, *([slots] * N_DEV))


def _half(ref, slot, c):
    half = ref.shape[1] // 2
    return ref.at[slot, pl.ds(pl.multiple_of(c * half, 8), half)]


def _gather_start(bufs, after, name):
    n = len(bufs)
    n_after = len(after)

    def body(*refs):
        ins = refs[:n]
        send, recv = refs[n + n_after], refs[n + n_after + 1]
        token = refs[-1]
        x, y, c, chips = _place()
        me = _shard_index(x, y)
        for t in range(n):
            for j, (px, py) in enumerate(chips):
                mine = _half(ins[t], me, c)
                _remote(mine, mine, send.at[t * 3 + j], recv.at[t * 3 + j], (px, py, c)).start()
        token[...] = jnp.zeros_like(token)

    thru = [pltpu.HBM(b.shape, b.dtype) for b in bufs]
    res = pl.pallas_call(
        body, name=name,
        out_shape=(pltpu.SemaphoreType.DMA((n * 3,)), pltpu.SemaphoreType.DMA((n * 3,)), *thru,
                   jax.ShapeDtypeStruct((8, LANES), F32)),
        in_specs=[HBM_ONLY] * n + [ANY_SPEC] * n_after,
        out_specs=(SEM_SPEC, SEM_SPEC, *([HBM_ONLY] * n), pl.BlockSpec(memory_space=pltpu.VMEM)),
        input_output_aliases={i: 2 + i for i in range(n)},
        compiler_params=pltpu.CompilerParams(has_side_effects=DATAFLOW),
    )(*[_in_hbm(b) for b in bufs], *after)
    return res[0], res[1], list(res[2:2 + n]), res[-1]


def _gather_wait(send, recv, bufs, after, name):
    n = len(bufs)

    def body(*refs):
        ins = refs[:n]
        send_ref, recv_ref = refs[n], refs[n + 1]
        x, y, c, chips = _place()
        me = _shard_index(x, y)
        for t in range(n):
            for j, (px, py) in enumerate(chips):
                cp = _remote(_half(ins[t], me, c), _half(ins[t], _shard_index(px, py), c), send_ref.at[t * 3 + j],
                             recv_ref.at[t * 3 + j], (px, py, c))
                cp.wait_send()
                cp.wait_recv()

    res = pl.pallas_call(
        body, name=name, out_shape=tuple(pltpu.HBM(b.shape, b.dtype) for b in bufs),
        in_specs=[HBM_ONLY] * n + [SEM_SPEC, SEM_SPEC] + [ANY_SPEC] * len(after), out_specs=[HBM_ONLY] * n,
        input_output_aliases={i: i for i in range(n)},
        compiler_params=pltpu.CompilerParams(has_side_effects=DATAFLOW),
    )(*bufs, send, recv, *after)
    return list(res)


def _gather_forward(bufs, name):
    n = len(bufs)

    def body(*refs):
        outs = refs[n:2 * n]
        send, recv = refs[2 * n:]
        x, y, c, chips = _place()
        sibling = (x, y, 1 - c)
        cps = []
        for t in range(n):
            for j, (px, py) in enumerate(chips):
                got = _half(outs[t], _shard_index(px, py), c)
                cp = _remote(got, got, send.at[t * 3 + j], recv.at[t * 3 + j], sibling)
                cp.start()
                cps.append(cp)
        for t in range(n):
            for j, (px, py) in enumerate(chips):
                theirs = _half(outs[t], _shard_index(px, py), 1 - c)
                _remote(theirs, theirs, send.at[t * 3 + j], recv.at[t * 3 + j], sibling).wait_recv()
        for cp in cps:
            cp.wait_send()

    return pl.pallas_call(
        body, name=name, in_specs=[HBM_SPEC] * n, out_specs=[HBM_SPEC] * n,
        out_shape=[jax.ShapeDtypeStruct(b.shape, b.dtype) for b in bufs],
        input_output_aliases={t: t for t in range(n)},
        scratch_shapes=[pltpu.SemaphoreType.DMA((n * 3,))] * 2,
    )(*bufs)


def _gather_forward_start(bufs, carry, name):
    n = len(bufs)

    def body(*refs):
        ins = refs[:n]
        send, recv = refs[n + 1], refs[n + 2]
        x, y, c, chips = _place()
        for t in range(n):
            for j, (px, py) in enumerate(chips):
                got = _half(ins[t], _shard_index(px, py), c)
                _remote(got, got, send.at[t * 3 + j], recv.at[t * 3 + j], (x, y, 1 - c)).start()

    res = pl.pallas_call(
        body, name=name,
        out_shape=(pltpu.SemaphoreType.DMA((n * 3,)), pltpu.SemaphoreType.DMA((n * 3,)),
                   *[pltpu.HBM(b.shape, b.dtype) for b in bufs], pltpu.HBM(carry.shape, carry.dtype)),
        in_specs=[HBM_ONLY] * (n + 1),
        out_specs=(SEM_SPEC, SEM_SPEC, *([HBM_ONLY] * (n + 1))),
        input_output_aliases={i: 2 + i for i in range(n + 1)},
        compiler_params=pltpu.CompilerParams(has_side_effects=DATAFLOW),
    )(*[_in_hbm(b) for b in bufs], _in_hbm(carry))
    return res[0], res[1], list(res[2:2 + n]), res[-1]


def _gather_forward_wait(send, recv, bufs, after, name):
    n = len(bufs)

    def body(*refs):
        ins = refs[:n]
        send_ref, recv_ref = refs[n], refs[n + 1]
        x, y, c, chips = _place()
        for t in range(n):
            for j, (px, py) in enumerate(chips):
                s = _shard_index(px, py)
                cp = _remote(_half(ins[t], s, c), _half(ins[t], s, 1 - c), send_ref.at[t * 3 + j],
                             recv_ref.at[t * 3 + j], (x, y, 1 - c))
                cp.wait_send()
                cp.wait_recv()

    res = pl.pallas_call(
        body, name=name, out_shape=tuple(pltpu.HBM(b.shape, b.dtype) for b in bufs),
        in_specs=[HBM_ONLY] * n + [SEM_SPEC, SEM_SPEC] + [ANY_SPEC] * len(after), out_specs=[HBM_ONLY] * n,
        input_output_aliases={i: i for i in range(n)},
        compiler_params=pltpu.CompilerParams(has_side_effects=DATAFLOW),
    )(*bufs, send, recv, *after)
    return list(res)


def _sibling_rows(ref, c):
    half = ref.shape[1] // 2
    return ref.at[:, pl.ds(pl.multiple_of((1 - c) * half, 8), half)]


def _half_exchange_start(grads, name):
    n = len(grads)

    def body(*refs):
        ins, lands = refs[:n], refs[n:2 * n]
        send, recv = refs[2 * n], refs[2 * n + 1]
        token = refs[-1]
        x, y, c, _ = _place()
        for t in range(n):
            _remote(_sibling_rows(ins[t], c), lands[t], send.at[t], recv.at[t], (x, y, 1 - c)).start()
        token[...] = jnp.zeros_like(token)

    halves = [jax.ShapeDtypeStruct((g.shape[0], g.shape[1] // 2, g.shape[2]), g.dtype) for g in grads]
    res = pl.pallas_call(
        body, name=name,
        out_shape=(pltpu.SemaphoreType.DMA((n,)), pltpu.SemaphoreType.DMA((n,)),
                   *[pltpu.HBM(g.shape, g.dtype) for g in grads], *[pltpu.HBM(h.shape, h.dtype) for h in halves],
                   jax.ShapeDtypeStruct((8, LANES), F32)),
        in_specs=[HBM_ONLY] * (2 * n),
        out_specs=(SEM_SPEC, SEM_SPEC, *([HBM_ONLY] * (2 * n)), pl.BlockSpec(memory_space=pltpu.VMEM)),
        input_output_aliases={i: 2 + i for i in range(2 * n)},
        compiler_params=pltpu.CompilerParams(has_side_effects=DATAFLOW),
    )(*[_in_hbm(g) for g in grads], *[_in_hbm(lax.empty(h.shape, h.dtype)) for h in halves])
    return res[0], res[1], list(res[2:2 + n]), list(res[2 + n:2 + 2 * n]), res[-1]


def _half_exchange_wait(send, recv, grads, lands, after, name):
    n = len(grads)

    def body(*refs):
        ins, got = refs[:n], refs[n:2 * n]
        send_ref, recv_ref = refs[2 * n], refs[2 * n + 1]
        x, y, c, _ = _place()
        for t in range(n):
            cp = _remote(_sibling_rows(ins[t], c), got[t], send_ref.at[t], recv_ref.at[t], (x, y, 1 - c))
            cp.wait_send()
            cp.wait_recv()

    res = pl.pallas_call(
        body, name=name,
        out_shape=(*[pltpu.HBM(g.shape, g.dtype) for g in grads], *[pltpu.HBM(h.shape, h.dtype) for h in lands]),
        in_specs=[HBM_ONLY] * (2 * n) + [SEM_SPEC, SEM_SPEC] + [ANY_SPEC] * len(after),
        out_specs=[HBM_ONLY] * (2 * n),
        input_output_aliases={i: i for i in range(2 * n)},
        compiler_params=pltpu.CompilerParams(has_side_effects=DATAFLOW),
    )(*grads, *lands, send, recv, *after)
    return list(res[:n]), list(res[n:])


def _half_rows(half, c_dim):
    for cand in (512, 256, 128, 64):
        if half % cand == 0 and cand * c_dim * 2 <= 2 * MIB:
            return cand
    raise ValueError((half, c_dim))


def _core_index():
    return lax.axis_index("c").astype(jnp.int32).reshape(1)


def _half_sum(own, other, name):
    s, r, c_dim = own.shape
    rows = _half_rows(r // 2, c_dim)
    per = r // 2 // rows

    def body(c_ref, a_ref, b_ref, o_ref):
        o_ref[...] = (a_ref[...].astype(F32) + b_ref[...].astype(F32)).astype(BF16)

    return pl.pallas_call(
        body, name=name,
        grid_spec=pltpu.PrefetchScalarGridSpec(
            num_scalar_prefetch=1, grid=(s, per),
            in_specs=[pl.BlockSpec((None, rows, c_dim), lambda k, i, c: (k, c[0] * per + i, 0)),
                      pl.BlockSpec((None, rows, c_dim), lambda k, i, c: (k, i, 0))],
            out_specs=pl.BlockSpec((None, rows, c_dim), lambda k, i, c: (k, i, 0))),
        out_shape=pltpu.HBM((s, r // 2, c_dim), BF16), compiler_params=_params(("arbitrary", "arbitrary"), 32),
    )(_core_index(), own, other)


def _reduce_start(pairs, name):
    n = len(pairs)

    def body(*refs):
        ins, lands = refs[:n], refs[n:2 * n]
        send, recv = refs[2 * n], refs[2 * n + 1]
        token = refs[-1]
        x, y, c, chips = _place()
        me = _shard_index(x, y)
        for t in range(n):
            for j, (px, py) in enumerate(chips):
                _remote(ins[t].at[_shard_index(px, py)], lands[t].at[me], send.at[t * 3 + j], recv.at[t * 3 + j],
                        (px, py, c)).start()
        token[...] = jnp.zeros_like(token)

    thru = [pltpu.HBM(b.shape, b.dtype) for b in pairs]
    res = pl.pallas_call(
        body, name=name,
        out_shape=(pltpu.SemaphoreType.DMA((n * 3,)), pltpu.SemaphoreType.DMA((n * 3,)), *thru, *thru,
                   jax.ShapeDtypeStruct((8, LANES), F32)),
        in_specs=[HBM_ONLY] * (2 * n),
        out_specs=(SEM_SPEC, SEM_SPEC, *([HBM_ONLY] * (2 * n)), pl.BlockSpec(memory_space=pltpu.VMEM)),
        input_output_aliases={i: 2 + i for i in range(2 * n)},
        compiler_params=pltpu.CompilerParams(has_side_effects=DATAFLOW),
    )(*[_in_hbm(b) for b in pairs], *[_in_hbm(lax.empty(b.shape, b.dtype)) for b in pairs])
    return res[0], res[1], list(res[2:2 + n]), list(res[2 + n:2 + 2 * n]), res[-1]


def _reduce_wait(send, recv, pairs, lands, after, name):
    n = len(pairs)

    def body(*refs):
        ins, got = refs[:n], refs[n:2 * n]
        send_ref, recv_ref = refs[2 * n], refs[2 * n + 1]
        x, y, c, chips = _place()
        for t in range(n):
            for j, (px, py) in enumerate(chips):
                s = _shard_index(px, py)
                cp = _remote(ins[t].at[s], got[t].at[s], send_ref.at[t * 3 + j], recv_ref.at[t * 3 + j], (px, py, c))
                cp.wait_send()
                cp.wait_recv()

    thru = [pltpu.HBM(b.shape, b.dtype) for b in pairs]
    res = pl.pallas_call(
        body, name=name, out_shape=(*thru, *thru),
        in_specs=[HBM_ONLY] * (2 * n) + [SEM_SPEC, SEM_SPEC] + [ANY_SPEC] * len(after),
        out_specs=[HBM_ONLY] * (2 * n),
        input_output_aliases={i: i for i in range(2 * n)},
        compiler_params=pltpu.CompilerParams(has_side_effects=DATAFLOW),
    )(*pairs, *lands, send, recv, *after)
    return list(res[:n]), list(res[n:])


def _reduce_sum(pair, landed, name):
    s, half, c_dim = pair.shape
    rows = _half_rows(half, c_dim)
    per = half // rows
    shard = _shard_index(lax.axis_index("x"), lax.axis_index("y"))
    where = jnp.stack([shard, lax.axis_index("c")]).astype(jnp.int32)

    def landed_spec(k):
        return pl.BlockSpec((None, rows, c_dim), lambda i, w: (jnp.where(w[0] == k, (k + 1) % s, k), i, 0))

    def body(w_ref, own_ref, *rest):
        o_ref = rest[-1]
        acc = None
        for k in range(s):
            term = jnp.where(w_ref[0] == k, own_ref[...], rest[k][...]).astype(F32)
            acc = term if acc is None else acc + term
        o_ref[...] = acc

    return pl.pallas_call(
        body, name=name,
        grid_spec=pltpu.PrefetchScalarGridSpec(
            num_scalar_prefetch=1, grid=(per,),
            in_specs=[pl.BlockSpec((None, rows, c_dim), lambda i, w: (w[0], i, 0))] + [landed_spec(k) for k in range(s)],
            out_specs=pl.BlockSpec((rows, c_dim), lambda i, w: (w[1] * per + i, 0))),
        out_shape=pltpu.HBM((2 * half, c_dim), F32), compiler_params=_params(("arbitrary",), 40),
    )(where, pair, *([landed] * s))


def _my_rows(ref, c):
    half = ref.shape[0] // 2
    return ref.at[pl.ds(pl.multiple_of(c * half, 8), half)]


def _half_gather_start(bufs, name):
    n = len(bufs)

    def body(*refs):
        ins = refs[:n]
        send, recv = refs[n], refs[n + 1]
        token = refs[-1]
        x, y, c, _ = _place()
        for t in range(n):
            mine = _my_rows(ins[t], c)
            _remote(mine, mine, send.at[t], recv.at[t], (x, y, 1 - c)).start()
        token[...] = jnp.zeros_like(token)

    res = pl.pallas_call(
        body, name=name,
        out_shape=(pltpu.SemaphoreType.DMA((n,)), pltpu.SemaphoreType.DMA((n,)),
                   *[pltpu.HBM(b.shape, b.dtype) for b in bufs], jax.ShapeDtypeStruct((8, LANES), F32)),
        in_specs=[HBM_ONLY] * n,
        out_specs=(SEM_SPEC, SEM_SPEC, *([HBM_ONLY] * n), pl.BlockSpec(memory_space=pltpu.VMEM)),
        input_output_aliases={i: 2 + i for i in range(n)},
        compiler_params=pltpu.CompilerParams(has_side_effects=DATAFLOW),
    )(*[_in_hbm(b) for b in bufs])
    return res[0], res[1], list(res[2:2 + n]), res[-1]


def _half_gather_wait(send, recv, bufs, after, name):
    n = len(bufs)

    def body(*refs):
        ins = refs[:n]
        send_ref, recv_ref = refs[n], refs[n + 1]
        x, y, c, _ = _place()
        for t in range(n):
            cp = _remote(_my_rows(ins[t], c), _my_rows(ins[t], 1 - c), send_ref.at[t], recv_ref.at[t], (x, y, 1 - c))
            cp.wait_send()
            cp.wait_recv()

    res = pl.pallas_call(
        body, name=name, out_shape=tuple(pltpu.HBM(b.shape, b.dtype) for b in bufs),
        in_specs=[HBM_ONLY] * n + [SEM_SPEC, SEM_SPEC] + [ANY_SPEC] * len(after), out_specs=[HBM_ONLY] * n,
        input_output_aliases={i: i for i in range(n)},
        compiler_params=pltpu.CompilerParams(has_side_effects=DATAFLOW),
    )(*bufs, send, recv, *after)
    return list(res)


WEIGHT_NAMES = ("ln_attn", "w_in", "sink_b", "rpb_c", "mix_gain", "w_out", "ln_ffn", "w_up", "conv_w", "conv_b",
                "w_down", "ln_final")
BIG_NAMES = ("w_in", "w_out", "w_up", "w_down")
REPLICATED_NAMES = ("ln_attn", "sink_b", "rpb_c", "mix_gain", "ln_ffn", "conv_b", "ln_final")
PACK_TILE = 8 * LANES


def _pack(arrays, row_multiple):
    pieces = []
    for a in arrays:
        flat = a.reshape(-1)
        pieces.append(jnp.pad(flat, (0, (-flat.shape[0]) % PACK_TILE)))
    flat = jnp.concatenate(pieces)
    flat = jnp.pad(flat, (0, (-flat.shape[0]) % (row_multiple * LANES)))
    return flat.reshape(-1, LANES)


def _unpack(packed, shapes):
    flat = packed.reshape(-1)
    out, off = [], 0
    for shape in shapes:
        size = math.prod(shape)
        out.append(flat[off:off + size].reshape(shape))
        off += size + (-size) % PACK_TILE
    return out


def kernel(x, ln_attn, w_in, sink_b, rpb_c, mix_gain, w_out, ln_ffn, w_up, conv_w, conv_b, w_down, ln_final, loss_target, m_ln_attn, m_w_in, m_sink_b, m_rpb_c, m_mix_gain, m_w_out, m_ln_ffn, m_w_up, m_conv_w, m_conv_b, m_w_down, m_ln_final, v_ln_attn, v_w_in, v_sink_b, v_rpb_c, v_mix_gain, v_w_out, v_ln_ffn, v_w_up, v_conv_w, v_conv_b, v_w_down, v_ln_final):
    w = dict(ln_attn=ln_attn, w_in=w_in, sink_b=sink_b, rpb_c=rpb_c, mix_gain=mix_gain, w_out=w_out, ln_ffn=ln_ffn,
             w_up=w_up, conv_w=conv_w, conv_b=conv_b, w_down=w_down, ln_final=ln_final)
    m = dict(ln_attn=m_ln_attn, w_in=m_w_in, sink_b=m_sink_b, rpb_c=m_rpb_c, mix_gain=m_mix_gain, w_out=m_w_out,
             ln_ffn=m_ln_ffn, w_up=m_w_up, conv_w=m_conv_w, conv_b=m_conv_b, w_down=m_w_down, ln_final=m_ln_final)
    v = dict(ln_attn=v_ln_attn, w_in=v_w_in, sink_b=v_sink_b, rpb_c=v_rpb_c, mix_gain=v_mix_gain, w_out=v_w_out,
             ln_ffn=v_ln_ffn, w_up=v_w_up, conv_w=v_conv_w, conv_b=v_conv_b, w_down=v_w_down, ln_final=v_ln_final)
    shard = _shard_index(lax.axis_index("x"), lax.axis_index("y"))
    up_cols = w_up.shape[2]

    conv_slots = _all_gather_small(_pack([conv_w], 8), "gather_conv_w")
    conv_all = conv_slots[0::2].reshape(N_SHARDS, -1)[:, :conv_w.size].reshape((N_SHARDS,) + conv_w.shape)

    arrivals = []
    group_of = {}
    tokens = []
    rest = ("w_out", "w_up", "w_down")
    for l, names in ((0, ("w_in",)), (0, rest), (1, ("w_in",)), (1, rest)):
        bufs = [_own_slot(w[k], l, shard, "own_" + k) for k in names]
        send, recv, bufs, token = _gather_start(bufs, tokens[-1:] or [conv_slots], "gather_start_%d" % len(arrivals))
        tokens.append(token)
        for k in names:
            group_of[l, k] = len(arrivals)
        arrivals.append({"names": names, "send": send, "recv": recv, "bufs": bufs, "done": None})

    def gathered(l, name, after):
        idx = group_of[l, name]
        group = arrivals[idx]

        def whole(k, buf):
            return buf.reshape(1, -1, buf.shape[2]) if k in ("w_out", "w_down") else buf

        if group["done"] is None:
            bufs = _gather_wait(group["send"], group["recv"], group["bufs"], list(after) + tokens[-1:],
                                "gather_wait_%d" % idx)
            first = _gather_forward(bufs[:1], "gather_forward_%d" % idx)[0]
            if len(bufs) > 1:
                send, recv, rest, first = _gather_forward_start(bufs[1:], first, "gather_forward_start_%d" % idx)
                group["rest"] = (send, recv, rest)
            group["done"] = {group["names"][0]: whole(group["names"][0], first)}
        if name not in group["done"]:
            send, recv, rest = group["rest"]
            rest = _gather_forward_wait(send, recv, rest, list(after), "gather_forward_wait_%d" % idx)
            group["done"].update({k: whole(k, buf) for k, buf in zip(group["names"][1:], rest)})
        return group["done"][name]

    cos, sin = _rope_tables(SEQ)
    tabs = {"cos": cos, "sin": sin, "bias_a": _bias_a(), "bias_b": _bias_b()}
    layers = []
    for l in range(DEPTH):
        conv_w_l = conv_all[:, l].reshape(2, N_SHARDS // 2, 3, up_cols).transpose(0, 2, 1, 3).reshape(2, 3, D_FF)
        layers.append({"ln_attn": ln_attn[l][None], "sink_b": sink_b[l], "bias_c": _bias_c(rpb_c[l]),
                       "mix_gain": mix_gain[l][None], "ln_ffn": ln_ffn[l][None], "conv_w": conv_w_l,
                       "conv_b": conv_b[l].reshape(2, 1, D_FF)})

    act = x[0]
    saved = []
    for l in range(DEPTH):
        act, keep = _layer_fwd(act, layers[l], lambda name, after, l=l: gathered(l, name, [after]), tabs)
        saved.append(keep)
    loss_part, dx, dx_b, d_ln_final = _loss_head(act, ln_final[None], loss_target[0], "loss_head")
    loss = lax.psum(loss_part[0, 0], ("x", "y", "c"))

    reductions = []

    opened = [0]

    def begin(l, partial):
        idx = opened[0]
        opened[0] += 1
        names = tuple(partial)
        send_sem, recv_sem, mine, theirs, token = _half_exchange_start([partial[k] for k in names],
                                                                       "half_exchange_start_%d" % idx)
        return {"idx": idx, "layer": l, "names": names, "send": send_sem, "recv": recv_sem, "mine": mine,
                "theirs": theirs}, token

    def finish(handle, after):
        idx, names = handle["idx"], handle["names"]
        mine, theirs = _half_exchange_wait(handle["send"], handle["recv"], handle["mine"], handle["theirs"], after,
                                           "half_exchange_wait_%d" % idx)
        pairs = [_half_sum(a, b, "half_sum_" + k) for k, a, b in zip(names, mine, theirs)]
        send_sem, recv_sem, pairs, lands, token = _reduce_start(pairs, "reduce_start_%d" % idx)
        reductions.append({"layer": handle["layer"], "names": names, "send": send_sem, "recv": recv_sem,
                           "pairs": pairs, "lands": lands})
        return token

    small = [None] * DEPTH
    pending = None
    for l in reversed(range(DEPTH)):
        big = {k: gathered(l, k, []) for k in BIG_NAMES}
        dx, dx_b, small[l], pending = _layer_bwd(dx, dx_b, saved[l], layers[l], big, tabs,
                                                 functools.partial(begin, l), finish, pending)
    after = [finish(pending[0], [pending[1]])]

    stacked = {k: jnp.stack([small[l][k] for l in range(DEPTH)]) for k in small[0]}
    part = {"ln_attn": stacked["ln_attn"][:, 0], "sink_b": stacked["sink_b"], "rpb_c": stacked["rpb_c"],
            "mix_gain": stacked["mix_gain"][:, 0], "ln_ffn": stacked["ln_ffn"][:, 0],
            "conv_b": stacked["conv_b"].reshape(DEPTH, 2 * D_FF), "ln_final": d_ln_final[0],
            "conv_w": stacked["conv_w"].transpose(0, 2, 1, 3).reshape(DEPTH, 3, 2 * D_FF)}
    small_names = REPLICATED_NAMES + ("conv_w",)
    small_send, small_recv, small_vec, small_slots, token = _small_start(
        _pack([part[k] for k in small_names], 256), after, "small_grads_start")
    after = [token]

    grads, delta, new_m, new_v = {}, {}, {}, {}
    updated = dict.fromkeys(BIG_NAMES)

    def arrive(idx, after):
        group = reductions[idx]
        pairs, lands = _reduce_wait(group["send"], group["recv"], group["pairs"], group["lands"], after,
                                    "reduce_wait_%d" % idx)
        halves = [_reduce_sum(pair, landed, "reduce_sum_" + k) for k, pair, landed in zip(group["names"], pairs, lands)]
        send_sem, recv_sem, halves, token = _half_gather_start(halves, "half_gather_start_%d" % idx)
        return {"idx": idx, "send": send_sem, "recv": recv_sem, "bufs": halves, "names": group["names"],
                "layer": group["layer"]}, [token]

    def update(swap, after):
        whole = _half_gather_wait(swap["send"], swap["recv"], swap["bufs"], after,
                                  "half_gather_wait_%d" % swap["idx"])
        for k, g in zip(swap["names"], whole):
            updated[k] = _adamw_layer(w[k], g, m[k], v[k], swap["layer"], updated[k], "adamw_" + k)
        return [updated[k][0] for k in swap["names"]]

    swaps = []
    for idx in range(len(reductions) - 1):
        swap, after = arrive(idx, after)
        swaps.append(swap)
    for swap in swaps[:2]:
        after = update(swap, after)
    swap, after = arrive(len(reductions) - 1, after)
    for swap in swaps[2:] + [swap]:
        after = update(swap, after)
    for k in BIG_NAMES:
        grads[k], delta[k], new_m[k], new_v[k] = updated[k]

    small_vec, small_slots = _small_wait(small_send, small_recv, small_vec, small_slots, after, "small_grads_wait")
    total = _small_sum(small_vec, small_slots, "small_grads_sum")
    for k, g in zip(small_names, _unpack(total, [part[k].shape for k in small_names])):
        grads[k] = g
    grads["conv_w"] = lax.dynamic_slice_in_dim(grads["conv_w"], shard * up_cols, up_cols, axis=2)

    flat = (DEPTH * 3, up_cols)
    res = _adamw(conv_w.reshape(flat), grads["conv_w"].reshape(flat), m["conv_w"].reshape(flat),
                 v["conv_w"].reshape(flat), "adamw_conv_w")
    delta["conv_w"], new_m["conv_w"], new_v["conv_w"] = (r.reshape(conv_w.shape) for r in res)
    shapes = [w[k].shape for k in REPLICATED_NAMES]
    packed = [_pack([d[k] for k in REPLICATED_NAMES], 128) for d in (w, grads, m, v)]
    for d, res in zip((delta, new_m, new_v), _adamw(*packed, "adamw_small")):
        for k, r in zip(REPLICATED_NAMES, _unpack(res, shapes)):
            d[k] = r

    return (loss, dx[None], *[grads[k] for k in WEIGHT_NAMES], *[delta[k] for k in WEIGHT_NAMES],
            *[new_m[k] for k in WEIGHT_NAMES], *[new_v[k] for k in WEIGHT_NAMES])
```

```python
import functools
import math

import jax
import jax.numpy as jnp
from jax import lax
from jax.experimental import pallas as pl
from jax.experimental.pallas import tpu as pltpu

F32 = jnp.float32
BF16 = jnp.bfloat16
MESH = pl.DeviceIdType.MESH

D_MODEL = 2048
SEQ = 2048
DEPTH = 2
HEAD_DIM = 64
N_HEADS_A = 12
N_HEADS_B = 10
N_KV_B = 2
N_HEADS_C = 10
WINDOW_B = 128
GRID_W = 64
NA_ROWS = 8
NA_COLS = 16
WIDTH_A = N_HEADS_A * HEAD_DIM
WIDTH_B = N_HEADS_B * HEAD_DIM
WIDTH_C = N_HEADS_C * HEAD_DIM
IN_COLS = 5120
D_FF = 5632
ROPE_THETA = 10000.0
EPS = 1e-6
NEG_INF = -1e30
N_SHARDS = 4

ADAM_LR = 0.001
ADAM_B1 = 0.9
ADAM_B2 = 0.999
ADAM_EPS = 1e-08
ADAM_WD = 0.01
ADAM_STEP = 10

LANES = 128
QB = 256
NQB = SEQ // QB
ROWS = 256
MIB = 2 ** 20

A_BLK = (0, 6, 12)
B_BLK = (18, 23, 24)
C_BLK = (25, 30, 35)
ROPE_BLKS = tuple(range(0, 12)) + tuple(range(18, 24))
QSCALE_BLKS = tuple(range(0, 6)) + tuple(range(18, 23)) + tuple(range(25, 30))
N_PBLK = IN_COLS // LANES


def _params(sem, vmem_mib):
    return pltpu.CompilerParams(dimension_semantics=sem, vmem_limit_bytes=vmem_mib * MIB)


def _weight_spec(w, cols, t_in, t_out, transposed):
    s, r, c = w.shape
    if cols:
        per = c // t_out
        k_dim, n = r, s * c
        if transposed:
            index = lambda j, rr: (rr // per, j, rr % per)
        else:
            index = lambda j, kk: (j // per, kk, j % per)
    else:
        per = r // t_in
        k_dim, n = s * r, c
        if transposed:
            index = lambda j, rr: (j // per, j % per, rr)
        else:
            index = lambda j, kk: (kk // per, kk % per, j)
    return pl.BlockSpec((None, t_in, t_out), index), k_dim, n


def _mm_nn(a, w, *, cols, tn, tk, out_dtype, name, residual=None, out_split=1):
    m, k_dim = a.shape
    w_spec, k_w, n = _weight_spec(w, cols, tk, tn, False)
    assert k_w == k_dim
    nj, nk = n // tn, k_dim // tk
    in_specs = [pl.BlockSpec((m, tk), lambda j, k: (0, k)), w_spec]
    args = [a, w]
    if residual is not None:
        in_specs.append(pl.BlockSpec((m, tn), lambda j, k: (0, j)))
        args.append(residual)
    if out_split > 1:
        per_o = n // out_split // tn
        out_spec = pl.BlockSpec((None, m, tn), lambda j, k: (j // per_o, 0, j % per_o))
        out_shape = pltpu.HBM((out_split, m, n // out_split), out_dtype)
    else:
        out_spec = pl.BlockSpec((m, tn), lambda j, k: (0, j))
        out_shape = pltpu.HBM((m, n), out_dtype)

    def body(*refs):
        a_ref, w_ref = refs[0], refs[1]
        r_ref = refs[2] if residual is not None else None
        o_ref = refs[3] if residual is not None else refs[2]

        def finish(val):
            if r_ref is not None:
                val = r_ref[...] + val
            o_ref[...] = val.astype(o_ref.dtype)

        part = jnp.dot(a_ref[...], w_ref[...], preferred_element_type=F32)
        if nk == 1:
            finish(part)
        else:
            acc = refs[-1]
            kk = pl.program_id(1)

            @pl.when(kk == 0)
            def _():
                acc[...] = part

            @pl.when(kk > 0)
            def _():
                acc[...] += part

            @pl.when(kk == nk - 1)
            def _():
                finish(acc[...])

    return pl.pallas_call(
        body, name=name, grid=(nj, nk), in_specs=in_specs, out_specs=out_spec, out_shape=out_shape,
        scratch_shapes=[pltpu.VMEM((m, tn), F32)] if nk > 1 else [],
        compiler_params=_params(("arbitrary", "arbitrary"), 56),
    )(*[_in_hbm(a) for a in args])


ANY_SPEC = pl.BlockSpec(memory_space=pl.ANY)


def _mm_nt(dy, w, *, cols, to, tr, out_dtype, name, after=()):
    if dy.ndim == 3:
        m = dy.shape[1]
        n = dy.shape[0] * dy.shape[2]
        per_d = dy.shape[2] // tr
        dy_spec = pl.BlockSpec((None, m, tr), lambda j, r: (r // per_d, 0, r % per_d))
    else:
        m, n = dy.shape
        dy_spec = pl.BlockSpec((m, tr), lambda j, r: (0, r))
    w_spec, k_dim, n_w = _weight_spec(w, cols, to, tr, True)
    assert n_w == n
    nj, nr = k_dim // to, n // tr

    n_after = len(after)

    def body(dy_ref, w_ref, *rest):
        o_ref = rest[n_after]
        part = lax.dot_general(dy_ref[...], w_ref[...], (((1,), (1,)), ((), ())), preferred_element_type=F32)
        if nr == 1:
            o_ref[...] = part.astype(o_ref.dtype)
        else:
            acc = rest[n_after + 1]
            rr = pl.program_id(1)

            @pl.when(rr == 0)
            def _():
                acc[...] = part

            @pl.when(rr > 0)
            def _():
                acc[...] += part

            @pl.when(rr == nr - 1)
            def _():
                o_ref[...] = acc[...].astype(o_ref.dtype)

    return pl.pallas_call(
        body, name=name, grid=(nj, nr), in_specs=[dy_spec, w_spec] + [ANY_SPEC] * n_after,
        out_specs=pl.BlockSpec((m, to), lambda j, r: (0, j)),
        out_shape=pltpu.HBM((m, k_dim), out_dtype),
        scratch_shapes=[pltpu.VMEM((m, to), F32)] if nr > 1 else [],
        compiler_params=_params(("arbitrary", "arbitrary"), 56),
    )(_in_hbm(dy), _in_hbm(w), *after)


def _mm_tn(x, dy, *, tk, tn, shards, name):
    m, k_dim = x.shape
    if dy.ndim == 3:
        n = dy.shape[0] * dy.shape[2]
        per_d = dy.shape[2] // tn
        dy_spec = pl.BlockSpec((None, m, tn), lambda i, j: (j // per_d, 0, j % per_d))
    else:
        n = dy.shape[1]
        dy_spec = pl.BlockSpec((m, tn), lambda i, j: (0, j))
    if shards > 0:
        per = n // shards // tn
        out_shape = pltpu.HBM((shards, k_dim, n // shards), BF16)
        out_spec = pl.BlockSpec((None, tk, tn), lambda i, j: (j // per, i, j % per))
    else:
        s = -shards
        per = k_dim // s // tk
        out_shape = pltpu.HBM((s, k_dim // s, n), BF16)
        out_spec = pl.BlockSpec((None, tk, tn), lambda i, j: (i // per, i % per, j))

    def body(x_ref, dy_ref, o_ref):
        o_ref[...] = lax.dot_general(x_ref[...], dy_ref[...], (((0,), (0,)), ((), ())),
                                     preferred_element_type=F32).astype(BF16)

    return pl.pallas_call(
        body, name=name, grid=(k_dim // tk, n // tn),
        in_specs=[pl.BlockSpec((m, tk), lambda i, j: (0, i)), dy_spec], out_specs=out_spec, out_shape=out_shape,
        compiler_params=_params(("arbitrary", "arbitrary"), 56),
    )(_in_hbm(x), _in_hbm(dy))


def _row_spec(width, rows=ROWS):
    return pl.BlockSpec((rows, width), lambda i: (i, 0))


def _vec_spec(width):
    return pl.BlockSpec((1, width), lambda i: (0, 0))


def _rms_stats(x):
    r = lax.rsqrt(jnp.mean(x * x, axis=-1, keepdims=True) + EPS)
    return r, x * r


def _rmsnorm_fwd(x, gain, name):
    t, d = x.shape

    def body(x_ref, g_ref, o_ref):
        _, n = _rms_stats(x_ref[...])
        o_ref[...] = (n * g_ref[...]).astype(BF16)

    return pl.pallas_call(
        body, name=name, grid=(t // ROWS,), in_specs=[_row_spec(d), _vec_spec(d)], out_specs=_row_spec(d),
        out_shape=pltpu.HBM((t, d), BF16), compiler_params=_params(("arbitrary",), 32),
    )(_in_hbm(x), _in_hbm(gain))


def _rmsnorm_bwd(x, gain, dh, dres, name, after=()):
    t, d = x.shape
    n_after = len(after)

    def body(x_ref, g_ref, dh_ref, dres_ref, *rest):
        dx_ref, dxb_ref, dg_ref = rest[n_after:]
        r, n = _rms_stats(x_ref[...])
        dh_v = dh_ref[...]
        dn = dh_v * g_ref[...]
        dx = dres_ref[...] + r * (dn - n * jnp.mean(dn * n, axis=-1, keepdims=True))
        dx_ref[...] = dx
        dxb_ref[...] = dx.astype(BF16)
        part = jnp.sum(dh_v * n, axis=0, keepdims=True)

        @pl.when(pl.program_id(0) == 0)
        def _():
            dg_ref[...] = part

        @pl.when(pl.program_id(0) > 0)
        def _():
            dg_ref[...] += part

    return pl.pallas_call(
        body, name=name, grid=(t // ROWS,),
        in_specs=[_row_spec(d), _vec_spec(d), _row_spec(d), _row_spec(d)] + [ANY_SPEC] * n_after,
        out_specs=[_row_spec(d), _row_spec(d), _vec_spec(d)],
        out_shape=[pltpu.HBM((t, d), F32), pltpu.HBM((t, d), BF16), jax.ShapeDtypeStruct((1, d), F32)],
        compiler_params=_params(("arbitrary",), 40),
    )(_in_hbm(x), _in_hbm(gain), _in_hbm(dh), _in_hbm(dres), *after)


def _loss_head(x, gain, target, name):
    t, d = x.shape

    def body(x_ref, g_ref, t_ref, loss_ref, dx_ref, dxb_ref, dg_ref):
        r, n = _rms_stats(x_ref[...])
        g = g_ref[...]
        err = n * g - t_ref[...]
        dy = err * (1.0 / d)
        dn = dy * g
        dx = r * (dn - n * jnp.mean(dn * n, axis=-1, keepdims=True))
        dx_ref[...] = dx
        dxb_ref[...] = dx.astype(BF16)
        part = jnp.sum(dy * n, axis=0, keepdims=True)
        lpart = jnp.zeros((8, LANES), F32) + 0.5 * jnp.sum(jnp.mean(err * err, axis=-1, keepdims=True))

        @pl.when(pl.program_id(0) == 0)
        def _():
            dg_ref[...] = part
            loss_ref[...] = lpart

        @pl.when(pl.program_id(0) > 0)
        def _():
            dg_ref[...] += part
            loss_ref[...] += lpart

    return pl.pallas_call(
        body, name=name, grid=(t // ROWS,),
        in_specs=[_row_spec(d), _vec_spec(d), _row_spec(d)],
        out_specs=[pl.BlockSpec((8, LANES), lambda i: (0, 0)), _row_spec(d), _row_spec(d), _vec_spec(d)],
        out_shape=[jax.ShapeDtypeStruct((8, LANES), F32), pltpu.HBM((t, d), F32), pltpu.HBM((t, d), BF16),
                   jax.ShapeDtypeStruct((1, d), F32)],
        compiler_params=_params(("arbitrary",), 40),
    )(x, gain, target)


def _swap_halves(x):
    lane = lax.broadcasted_iota(jnp.int32, x.shape, 1)
    return jnp.where((lane % HEAD_DIM) < HEAD_DIM // 2, pltpu.roll(x, LANES - HEAD_DIM // 2, 1),
                     pltpu.roll(x, HEAD_DIM // 2, 1))


def _rope_tables(t):
    inv_freq = ROPE_THETA ** (-jnp.arange(0, HEAD_DIM, 2, dtype=F32) / HEAD_DIM)
    ang = jnp.arange(t, dtype=F32)[:, None] * inv_freq[None, :]
    cos = jnp.tile(jnp.cos(ang), (1, LANES // (HEAD_DIM // 2)))
    sin = jnp.tile(jnp.sin(ang), (1, LANES // (HEAD_DIM // 2)))
    lane = jnp.arange(LANES)[None, :]
    return cos, jnp.where((lane % HEAD_DIM) < HEAD_DIM // 2, -sin, sin)


def _rope_fwd(proj, cos, sin, name):
    t = proj.shape[0]
    scale = HEAD_DIM ** -0.5

    def body(p_ref, c_ref, s_ref, o_ref):
        cos_v, sin_v = c_ref[...], s_ref[...]
        for b in range(N_PBLK):
            cols = slice(b * LANES, (b + 1) * LANES)
            v = p_ref[:, cols]
            if b in ROPE_BLKS:
                v = v * cos_v + _swap_halves(v) * sin_v
            if b in QSCALE_BLKS:
                v = v * scale
            o_ref[:, cols] = v.astype(BF16)

    return pl.pallas_call(
        body, name=name, grid=(t // ROWS,),
        in_specs=[_row_spec(IN_COLS), _row_spec(LANES), _row_spec(LANES)], out_specs=_row_spec(IN_COLS),
        out_shape=pltpu.HBM((t, IN_COLS), BF16), compiler_params=_params(("arbitrary",), 40),
    )(_in_hbm(proj), _in_hbm(cos), _in_hbm(sin))


def _rope_bwd(grads, cos, sin, name):
    t = grads[0].shape[0]
    scale = HEAD_DIM ** -0.5
    group = N_HEADS_B // N_KV_B

    def body(*refs):
        c_ref, s_ref, o_ref = refs[9], refs[10], refs[11]
        cos_v, sin_v = c_ref[...], s_ref[...]

        def kv_sum(ref):
            parts = []
            for g in range(N_KV_B):
                acc = ref[:, g * group * HEAD_DIM:(g * group + 1) * HEAD_DIM]
                for h in range(g * group + 1, (g + 1) * group):
                    acc = acc + ref[:, h * HEAD_DIM:(h + 1) * HEAD_DIM]
                parts.append(acc)
            return jnp.concatenate(parts, axis=1)

        def emit(b, v):
            if b in ROPE_BLKS:
                v = v * cos_v - _swap_halves(v) * sin_v
            if b in QSCALE_BLKS:
                v = v * scale
            o_ref[:, b * LANES:(b + 1) * LANES] = v.astype(BF16)

        starts = (A_BLK[0], A_BLK[1], A_BLK[2], B_BLK[0], None, None, C_BLK[0], C_BLK[1], C_BLK[2])
        for idx, start in enumerate(starts):
            if start is None:
                continue
            for j in range(refs[idx].shape[1] // LANES):
                emit(start + j, refs[idx][:, j * LANES:(j + 1) * LANES])
        emit(B_BLK[1], kv_sum(refs[4]))
        emit(B_BLK[2], kv_sum(refs[5]))

    return pl.pallas_call(
        body, name=name, grid=(t // ROWS,),
        in_specs=[_row_spec(g.shape[1]) for g in grads] + [_row_spec(LANES), _row_spec(LANES)],
        out_specs=_row_spec(IN_COLS),
        out_shape=pltpu.HBM((t, IN_COLS), BF16), compiler_params=_params(("arbitrary",), 40),
    )(*[_in_hbm(g) for g in grads], _in_hbm(cos), _in_hbm(sin))


GROUP_COLS = ((0, WIDTH_A), (WIDTH_A, WIDTH_A + WIDTH_B), (WIDTH_A + WIDTH_B, D_MODEL))


def _mix_fwd(oa, ob, oc, gain, name):
    t = oa.shape[0]

    def body(a_ref, b_ref, c_ref, g_ref, o_ref):
        for ref, (lo, hi) in zip((a_ref, b_ref, c_ref), GROUP_COLS):
            _, n = _rms_stats(ref[...])
            o_ref[:, lo:hi] = (n * g_ref[:, lo:hi]).astype(BF16)

    return pl.pallas_call(
        body, name=name, grid=(t // ROWS,),
        in_specs=[_row_spec(WIDTH_A), _row_spec(WIDTH_B), _row_spec(WIDTH_C), _vec_spec(D_MODEL)],
        out_specs=_row_spec(D_MODEL),
        out_shape=pltpu.HBM((t, D_MODEL), BF16), compiler_params=_params(("arbitrary",), 32),
    )(_in_hbm(oa), _in_hbm(ob), _in_hbm(oc), _in_hbm(gain))


def _mix_bwd(oa, ob, oc, gain, dmixed, name, after=()):
    t = oa.shape[0]
    n_after = len(after)

    def body(a_ref, b_ref, c_ref, g_ref, dm_ref, *rest):
        da_ref, db_ref, dc_ref, dg_ref = rest[n_after:]
        first = pl.program_id(0) == 0
        for ref, dref, (lo, hi) in zip((a_ref, b_ref, c_ref), (da_ref, db_ref, dc_ref), GROUP_COLS):
            r, n = _rms_stats(ref[...])
            dm = dm_ref[:, lo:hi]
            dn = dm * g_ref[:, lo:hi]
            dref[...] = r * (dn - n * jnp.mean(dn * n, axis=-1, keepdims=True))
            part = jnp.sum(dm * n, axis=0, keepdims=True)

            @pl.when(first)
            def _():
                dg_ref[:, lo:hi] = part

            @pl.when(jnp.logical_not(first))
            def _():
                dg_ref[:, lo:hi] += part

    return pl.pallas_call(
        body, name=name, grid=(t // ROWS,),
        in_specs=[_row_spec(WIDTH_A), _row_spec(WIDTH_B), _row_spec(WIDTH_C), _vec_spec(D_MODEL), _row_spec(D_MODEL)]
        + [ANY_SPEC] * n_after,
        out_specs=[_row_spec(WIDTH_A), _row_spec(WIDTH_B), _row_spec(WIDTH_C), _vec_spec(D_MODEL)],
        out_shape=[pltpu.HBM((t, WIDTH_A), F32), pltpu.HBM((t, WIDTH_B), F32), pltpu.HBM((t, WIDTH_C), F32),
                   jax.ShapeDtypeStruct((1, D_MODEL), F32)],
        compiler_params=_params(("arbitrary",), 40),
    )(_in_hbm(oa), _in_hbm(ob), _in_hbm(oc), _in_hbm(gain), _in_hbm(dmixed), *after)


FF_COLS = 256


SUBLANES = 8
CHUNK_FWD = 256
CHUNK_BWD = 128
HALO = SUBLANES


def _ext_rows(ref, r0, chunk, where):
    t, cols = ref.shape
    zeros = jnp.zeros((HALO, cols), F32)
    if where == "first":
        return jnp.concatenate([zeros, ref[0:chunk + HALO, :]], axis=0)
    if where == "last":
        return jnp.concatenate([ref[t - chunk - HALO:t, :], zeros], axis=0)
    return ref[pl.ds(pl.multiple_of(r0 - HALO, HALO), chunk + 2 * HALO), :]


def _for_chunks(t, chunk, fn):
    fn(0, "first")

    def mid(ci, carry):
        fn(pl.multiple_of(ci * chunk, chunk), "mid")
        return carry

    lax.fori_loop(1, t // chunk - 1, mid, 0)
    fn(t - chunk, "last")


def _roll_rows(x, by):
    return pltpu.roll(x, by % x.shape[0], 0)


def _gate_val(u_ref, r0, chunk, where, w_ref, b_ref):
    ext = [_ext_rows(u_ref.at[h], r0, chunk, where) for h in range(2)]
    before = [_roll_rows(e, 1) for e in ext]
    after = [_roll_rows(e, -1) for e in ext]
    gate, val = ((before[h] * w_ref[h, 0:1, :] + ext[h] * w_ref[h, 1:2, :]) + after[h] * w_ref[h, 2:3, :] + b_ref[h]
                 for h in range(2))
    return gate, val, ext, before, after


def _ff_specs(t):
    u_spec = pl.BlockSpec((2, t, FF_COLS), lambda j: (0, 0, j))
    w_spec = pl.BlockSpec((2, 3, FF_COLS), lambda j: (0, 0, j))
    b_spec = pl.BlockSpec((2, 1, FF_COLS), lambda j: (0, 0, j))
    return u_spec, w_spec, b_spec


def _convgate_fwd(u0, conv_w, conv_b, name):
    t = u0.shape[1]
    u_spec, w_spec, b_spec = _ff_specs(t)

    def body(u_ref, w_ref, b_ref, o_ref):
        def chunk(r0, where):
            gate, val, _, _, _ = _gate_val(u_ref, r0, CHUNK_FWD, where, w_ref, b_ref)
            act = gate * jax.nn.sigmoid(gate) * val
            o_ref[pl.ds(r0, CHUNK_FWD), :] = act[HALO:HALO + CHUNK_FWD].astype(BF16)

        _for_chunks(t, CHUNK_FWD, chunk)

    return pl.pallas_call(
        body, name=name, grid=(D_FF // FF_COLS,), in_specs=[u_spec, w_spec, b_spec],
        out_specs=pl.BlockSpec((t, FF_COLS), lambda j: (0, j)),
        out_shape=pltpu.HBM((t, D_FF), BF16), compiler_params=_params(("arbitrary",), 48),
    )(_in_hbm(u0), conv_w, conv_b)


def _convgate_bwd(u0, conv_w, conv_b, d_act, name):
    t = u0.shape[1]
    u_spec, w_spec, b_spec = _ff_specs(t)

    def body(u_ref, w_ref, b_ref, da_ref, du_ref, dw_ref, db_ref, sums_ref):
        sums_ref[...] = jnp.zeros_like(sums_ref)
        inner = slice(HALO, HALO + CHUNK_BWD)

        def fold(x):
            return jnp.sum(x.reshape(CHUNK_BWD // SUBLANES, SUBLANES, x.shape[1]), axis=0)

        def chunk(r0, where):
            gate, val, ext, before, after = _gate_val(u_ref, r0, CHUNK_BWD, where, w_ref, b_ref)
            sig = jax.nn.sigmoid(gate)
            da = _ext_rows(da_ref, r0, CHUNK_BWD, where)
            d_half = (da * val * (sig * (1.0 + gate * (1.0 - sig))), da * (gate * sig))
            for h in range(2):
                du = d_half[h]
                for k, term in enumerate((du, du * before[h], du * ext[h], du * after[h])):
                    sums_ref[h, k] += fold(term[inner])
                du0 = (_roll_rows(du, -1) * w_ref[h, 0:1, :] + du * w_ref[h, 1:2, :]) + _roll_rows(du, 1) * w_ref[h, 2:3, :]
                du_ref[h, pl.ds(r0, CHUNK_BWD), :] = du0[inner].astype(BF16)

        _for_chunks(t, CHUNK_BWD, chunk)
        for h in range(2):
            db_ref[h] = jnp.sum(sums_ref[h, 0], axis=0, keepdims=True)
            for k in range(3):
                dw_ref[h, k:k + 1, :] = jnp.sum(sums_ref[h, k + 1], axis=0, keepdims=True)

    return pl.pallas_call(
        body, name=name, grid=(D_FF // FF_COLS,),
        in_specs=[u_spec, w_spec, b_spec, pl.BlockSpec((t, FF_COLS), lambda j: (0, j))],
        out_specs=[u_spec, w_spec, b_spec],
        out_shape=[pltpu.HBM((2, t, D_FF), BF16), jax.ShapeDtypeStruct((2, 3, D_FF), F32),
                   jax.ShapeDtypeStruct((2, 1, D_FF), F32)],
        scratch_shapes=[pltpu.VMEM((2, 4, SUBLANES, FF_COLS), F32)],
        compiler_params=_params(("arbitrary",), 56),
    )(_in_hbm(u0), conv_w, conv_b, _in_hbm(d_act))


class _Group:
    def __init__(self, heads, blks, kv_rows, n_win, gqa, bias_per_head):
        self.heads = heads
        self.pairs = heads // 2
        self.q_blk, self.k_blk, self.v_blk = blks
        self.kv_rows = kv_rows
        self.n_win = n_win
        self.full = kv_rows == SEQ
        self.gqa = gqa
        self.bias_per_head = bias_per_head
        self.width = heads * HEAD_DIM
        self.keys = kv_rows * n_win


GROUP_A = _Group(N_HEADS_A, A_BLK, SEQ, 1, False, False)
GROUP_B = _Group(N_HEADS_B, B_BLK, WINDOW_B, 4, True, False)
GROUP_C = _Group(N_HEADS_C, C_BLK, QB, 3, False, True)


def _win_start(grp, i):
    return jnp.clip(i * (QB // grp.kv_rows) - 1, 0, SEQ // grp.kv_rows - grp.n_win)


def _win_variant(i):
    return jnp.minimum(i, 1) + (i == NQB - 1).astype(jnp.int32)


def _attn_in_specs(grp, t):
    q_spec = pl.BlockSpec((QB, LANES), lambda p, i: (i, grp.q_blk + p))

    def col(blk):
        return (lambda p: blk) if grp.gqa else (lambda p: blk + p)

    def kv_specs(blk):
        c = col(blk)
        if grp.full:
            return [pl.BlockSpec((t, LANES), lambda p, i: (0, c(p)))]
        return [pl.BlockSpec((grp.kv_rows, LANES),
                             functools.partial(lambda p, i, w: (_win_start(grp, i) + w, c(p)), w=w))
                for w in range(grp.n_win)]

    nwk = grp.keys
    if grp.bias_per_head:
        bias_spec = pl.BlockSpec((2, None, QB, nwk), lambda p, i: (p, _win_variant(i), 0, 0))
    elif grp.full:
        bias_spec = pl.BlockSpec((1, None, QB, nwk), lambda p, i: (0, i, 0, 0))
    else:
        bias_spec = pl.BlockSpec((1, None, QB, nwk), lambda p, i: (0, _win_variant(i), 0, 0))
    sink_spec = pl.BlockSpec((1, LANES), lambda p, i: (0, p))
    return q_spec, kv_specs(grp.k_blk), kv_specs(grp.v_blk), bias_spec, sink_spec


def _head_kv(grp, whole, e, p):
    lo, hi = whole[:, :HEAD_DIM], whole[:, HEAD_DIM:]
    if grp.gqa:
        return jnp.where(2 * p + e >= N_HEADS_B // N_KV_B, hi, lo)
    return hi if e else lo


def _softmax_parts(q, k, bias, sink):
    s = lax.dot_general(q, k, (((1,), (1,)), ((), ())), preferred_element_type=F32) + bias
    m = jnp.maximum(jnp.max(s, axis=-1, keepdims=True), sink)
    pe = jnp.exp(s - m)
    denom = jnp.sum(pe, axis=-1, keepdims=True) + jnp.exp(sink - m)
    return pe, m, 1.0 / denom


def _attn_fwd(grp, proj, bias, sink, name):
    t = proj.shape[0]
    q_spec, k_specs, v_specs, bias_spec, sink_spec = _attn_in_specs(grp, t)
    nkv = len(k_specs)

    def body(*refs):
        q_ref = refs[0]
        k_refs, v_refs = refs[1:1 + nkv], refs[1 + nkv:1 + 2 * nkv]
        bias_ref, sink_ref, o_ref = refs[1 + 2 * nkv:4 + 2 * nkv]
        p = pl.program_id(0)
        k_all = jnp.concatenate([r[...] for r in k_refs], axis=0)
        v_all = jnp.concatenate([r[...] for r in v_refs], axis=0)
        outs = []
        for e in range(2):
            q = q_ref[:, e * HEAD_DIM:(e + 1) * HEAD_DIM]
            k = _head_kv(grp, k_all, e, p)
            v = _head_kv(grp, v_all, e, p)
            snk = sink_ref[0:1, e * HEAD_DIM:e * HEAD_DIM + 1]
            pe, _, inv = _softmax_parts(q, k, bias_ref[e if grp.bias_per_head else 0], snk)
            outs.append(jnp.dot(pe.astype(BF16), v, preferred_element_type=F32) * inv)
        o_ref[...] = jnp.concatenate(outs, axis=1)

    return pl.pallas_call(
        body, name=name, grid=(grp.pairs, NQB),
        in_specs=[q_spec, *k_specs, *v_specs, bias_spec, sink_spec],
        out_specs=pl.BlockSpec((QB, LANES), lambda p, i: (i, p)),
        out_shape=pltpu.HBM((t, grp.width), F32),
        compiler_params=_params(("arbitrary", "arbitrary"), 48),
    )(*([_in_hbm(proj)] * (1 + 2 * nkv)), _in_hbm(bias), sink)


def _attn_bwd(grp, proj, bias, sink, out, d_out, name):
    t = proj.shape[0]
    q_spec, k_specs, v_specs, bias_spec, sink_spec = _attn_in_specs(grp, t)
    nkv = len(k_specs)
    n_off = 2 * NA_ROWS - 1
    rows_q = QB // GRID_W
    wide = grp.keys > 2 * QB
    o_spec = pl.BlockSpec((QB, LANES), lambda p, i: (i, p))
    acc_spec = pl.BlockSpec((t, LANES), lambda p, i: (0, p))
    out_specs = [o_spec, acc_spec, acc_spec, pl.BlockSpec((None, 8, LANES), lambda p, i: (p, 0, 0))]
    out_shape = [pltpu.HBM((t, grp.width), F32)] * 3 + [jax.ShapeDtypeStruct((grp.pairs, 8, LANES), F32)]
    if grp.bias_per_head:
        out_specs.append(pl.BlockSpec((2, n_off, GRID_W, GRID_W), lambda p, i: (p, 0, 0, 0)))
        out_shape.append(jax.ShapeDtypeStruct((grp.heads, n_off, GRID_W, GRID_W), F32))

    def body(*refs):
        q_ref = refs[0]
        k_refs, v_refs = refs[1:1 + nkv], refs[1 + nkv:1 + 2 * nkv]
        bias_ref, sink_ref, o_ref, do_ref = refs[1 + 2 * nkv:5 + 2 * nkv]
        dq_ref, dk_ref, dv_ref, dsink_ref = refs[5 + 2 * nkv:9 + 2 * nkv]
        dbias_ref = refs[9 + 2 * nkv] if grp.bias_per_head else None
        p, i = pl.program_id(0), pl.program_id(1)

        @pl.when(i == 0)
        def _():
            dk_ref[...] = jnp.zeros_like(dk_ref)
            dv_ref[...] = jnp.zeros_like(dv_ref)
            dsink_ref[...] = jnp.zeros_like(dsink_ref)
            if dbias_ref is not None:
                dbias_ref[...] = jnp.zeros_like(dbias_ref)

        k_all = jnp.concatenate([r[...] for r in k_refs], axis=0)
        v_all = jnp.concatenate([r[...] for r in v_refs], axis=0)
        start = 0 if grp.full else _win_start(grp, i)
        dqs, dks, dvs, dsinks = [], [], [], []
        for e in range(2):
            cols = slice(e * HEAD_DIM, (e + 1) * HEAD_DIM)
            q = q_ref[:, cols]
            k = _head_kv(grp, k_all, e, p)
            v = _head_kv(grp, v_all, e, p)
            snk = sink_ref[0:1, e * HEAD_DIM:e * HEAD_DIM + 1]
            pe, m, inv = _softmax_parts(q, k, bias_ref[e if grp.bias_per_head else 0], snk)
            prob = pe * inv
            do = do_ref[:, cols]
            do_b = do.astype(BF16)
            pe_b = prob.astype(BF16)
            delta = jnp.sum(do * o_ref[:, cols], axis=-1, keepdims=True)
            dp = lax.dot_general(do_b, v, (((1,), (1,)), ((), ())), preferred_element_type=F32)
            ds = prob * (dp - delta)
            ds_b = ds.astype(BF16)
            dqs.append(jnp.dot(ds_b, k, preferred_element_type=F32))
            if wide:
                dks.append(lax.dot_general(q, ds_b, (((0,), (0,)), ((), ())), preferred_element_type=F32))
                dvs.append(lax.dot_general(do_b, pe_b, (((0,), (0,)), ((), ())), preferred_element_type=F32))
            else:
                dks.append(lax.dot_general(ds_b, q, (((0,), (0,)), ((), ())), preferred_element_type=F32))
                dvs.append(lax.dot_general(pe_b, do_b, (((0,), (0,)), ((), ())), preferred_element_type=F32))
            dsinks.append(-jnp.sum(jnp.exp(snk - m) * inv * delta, axis=0, keepdims=True))
            if dbias_ref is not None:
                shift = (i * QB - start * grp.kv_rows) // GRID_W
                for rq in range(rows_q):
                    for rk in range(grp.keys // GRID_W):
                        off = jnp.clip(rk - rq + (NA_ROWS - 1) - shift, 0, n_off - 1)
                        dbias_ref[e, off] += ds[rq * GRID_W:(rq + 1) * GRID_W, rk * GRID_W:(rk + 1) * GRID_W]
        dq_ref[...] = jnp.concatenate(dqs, axis=1)
        rows = pl.ds(0, t) if grp.full else pl.ds(pl.multiple_of(start * grp.kv_rows, grp.kv_rows), grp.keys)
        if wide:
            dk_ref[rows, :] += jnp.concatenate(dks, axis=0).T
            dv_ref[rows, :] += jnp.concatenate(dvs, axis=0).T
        else:
            dk_ref[rows, :] += jnp.concatenate(dks, axis=1)
            dv_ref[rows, :] += jnp.concatenate(dvs, axis=1)
        lane = lax.broadcasted_iota(jnp.int32, (8, LANES), 1)
        dsink_ref[...] += jnp.where(lane < HEAD_DIM, dsinks[0], dsinks[1])

    return pl.pallas_call(
        body, name=name, grid=(grp.pairs, NQB),
        in_specs=[q_spec, *k_specs, *v_specs, bias_spec, sink_spec, o_spec, o_spec],
        out_specs=out_specs, out_shape=out_shape,
        compiler_params=_params(("arbitrary", "arbitrary"), 56),
    )(*([_in_hbm(proj)] * (1 + 2 * nkv)), _in_hbm(bias), sink, _in_hbm(out), _in_hbm(d_out))


DILATED_CONFIGS = ((128, 1), (512, 4), (2048, 16))


def _bias_a():
    d = jnp.arange(SEQ)[None, :] - jnp.arange(SEQ)[:, None]
    mult = jnp.zeros((SEQ, SEQ), F32)
    for window, r in DILATED_CONFIGS:
        reach = (window // (2 * r)) * r
        mult = mult + ((d % r == 0) & (jnp.abs(d) <= reach)).astype(F32)
    return jnp.where(mult > 0, jnp.log(jnp.maximum(mult, 1.0)), NEG_INF).reshape(1, NQB, QB, SEQ)


def _bias_b():
    row = jnp.arange(QB)[None, :, None]
    col = jnp.arange(GROUP_B.keys)[None, None, :]
    var = jnp.arange(3)[:, None, None]
    d = col - (GROUP_B.kv_rows * var + row)
    return jnp.where(jnp.abs(d) <= WINDOW_B, 0.0, NEG_INF).astype(F32)[None]


def _offset_onehot():
    c = jnp.arange(GRID_W)[:, None, None]
    c2 = jnp.arange(GRID_W)[None, :, None]
    b = jnp.arange(LANES)[None, None, :]
    return (c2 - c + NA_COLS - 1 == b).astype(BF16).reshape(GRID_W * GRID_W, LANES)


def _split_dot(x, g):
    hi = x.astype(BF16)
    rest = x - hi.astype(F32)
    mid = rest.astype(BF16)
    lo = (rest - mid.astype(F32)).astype(BF16)
    return (jnp.dot(hi, g, preferred_element_type=F32) + jnp.dot(mid, g, preferred_element_type=F32)
            + jnp.dot(lo, g, preferred_element_type=F32))


def _table_mm(x, g, name):
    def body(x_ref, g_ref, o_ref):
        o_ref[...] = _split_dot(x_ref[...], g_ref[...])

    return pl.pallas_call(
        body, name=name, out_shape=jax.ShapeDtypeStruct((x.shape[0], g.shape[1]), F32),
        in_specs=[pl.BlockSpec(memory_space=pltpu.VMEM)] * 2, out_specs=pl.BlockSpec(memory_space=pltpu.VMEM),
        compiler_params=pltpu.CompilerParams(vmem_limit_bytes=32 * MIB),
    )(x, g)


N_OFF = 2 * NA_ROWS - 1
TABLE_ROWS = 152


def _bias_c(rpb):
    table = jnp.zeros((TABLE_ROWS, LANES), F32).at[:N_HEADS_C * N_OFF, :2 * NA_COLS - 1].set(
        rpb.reshape(N_HEADS_C * N_OFF, 2 * NA_COLS - 1))
    tiles = _table_mm(table, _offset_onehot().T, "rpb_tiles")[:N_HEADS_C * N_OFF]
    tiles = tiles.reshape(N_HEADS_C, N_OFF, GRID_W, GRID_W)
    c = jnp.arange(GRID_W)
    col_start = jnp.clip(c - NA_COLS // 2, 0, GRID_W - NA_COLS)
    col_ok = (c[None, :] >= col_start[:, None]) & (c[None, :] < col_start[:, None] + NA_COLS)
    tiles = jnp.where(col_ok, tiles, NEG_INF)
    rows_q = QB // GRID_W
    rows_k = GROUP_C.keys // GRID_W

    def body(t_ref, o_ref):
        for var in range(3):
            for rq in range(rows_q):
                r_l = rows_q * var + rq
                first = min(max(r_l - NA_ROWS // 2, 0), rows_k - NA_ROWS)
                for rk in range(rows_k):
                    if first <= rk < first + NA_ROWS:
                        tile = t_ref[rk - r_l + NA_ROWS - 1]
                    else:
                        tile = jnp.full((GRID_W, GRID_W), NEG_INF, F32)
                    o_ref[var, rq * GRID_W:(rq + 1) * GRID_W, rk * GRID_W:(rk + 1) * GRID_W] = tile

    return pl.pallas_call(
        body, name="bias_c", grid=(N_HEADS_C,),
        in_specs=[pl.BlockSpec((None, N_OFF, GRID_W, GRID_W), lambda h: (h, 0, 0, 0))],
        out_specs=pl.BlockSpec((None, 3, QB, GROUP_C.keys), lambda h: (h, 0, 0, 0)),
        out_shape=jax.ShapeDtypeStruct((N_HEADS_C, 3, QB, GROUP_C.keys), F32),
        compiler_params=_params(("arbitrary",), 32),
    )(tiles)


def _rpb_grad(d_tiles):
    flat = jnp.zeros((TABLE_ROWS, GRID_W * GRID_W), F32).at[:N_HEADS_C * N_OFF].set(
        d_tiles.reshape(N_HEADS_C * N_OFF, GRID_W * GRID_W))
    out = _table_mm(flat, _offset_onehot(), "rpb_grad")
    return out[:N_HEADS_C * N_OFF, :2 * NA_COLS - 1].reshape(N_HEADS_C, N_OFF, 2 * NA_COLS - 1)


def _sink_lanes(sink):
    return jnp.repeat(sink.astype(F32), HEAD_DIM)[None, :]


def _attention_fwd(proj_r, sink_b, bias_a, bias_b, bias_c):
    no_sink_a = jnp.full((1, WIDTH_A), NEG_INF, F32)
    no_sink_c = jnp.full((1, WIDTH_C), NEG_INF, F32)
    oa = _attn_fwd(GROUP_A, proj_r, bias_a, no_sink_a, "attn_a_fwd")
    ob = _attn_fwd(GROUP_B, proj_r, bias_b, _sink_lanes(sink_b), "attn_b_fwd")
    oc = _attn_fwd(GROUP_C, proj_r, bias_c, no_sink_c, "attn_c_fwd")
    return oa, ob, oc


def _attention_bwd(proj_r, sink_b, bias_a, bias_b, bias_c, outs, d_outs, cos, sin):
    no_sink_a = jnp.full((1, WIDTH_A), NEG_INF, F32)
    no_sink_c = jnp.full((1, WIDTH_C), NEG_INF, F32)
    dqa, dka, dva, _ = _attn_bwd(GROUP_A, proj_r, bias_a, no_sink_a, outs[0], d_outs[0], "attn_a_bwd")
    dqb, dkb, dvb, dsink = _attn_bwd(GROUP_B, proj_r, bias_b, _sink_lanes(sink_b), outs[1], d_outs[1], "attn_b_bwd")
    dqc, dkc, dvc, _, d_tiles = _attn_bwd(GROUP_C, proj_r, bias_c, no_sink_c, outs[2], d_outs[2], "attn_c_bwd")
    d_proj = _rope_bwd((dqa, dka, dva, dqb, dkb, dvb, dqc, dkc, dvc), cos, sin, "rope_bwd")
    d_sink = dsink[:, 0, :].reshape(GROUP_B.pairs, 2, HEAD_DIM)[:, :, 0].reshape(N_HEADS_B)
    return d_proj, d_sink, _rpb_grad(d_tiles)


def _adamw(w, g, m, v, name):
    r, c = w.shape
    rows = r
    for cand in (512, 256, 128, 64, 32, 16, 8):
        if r % cand == 0 and cand * c * 4 <= MIB:
            rows = cand
            break
    spec = pl.BlockSpec((rows, c), lambda i: (i, 0))

    def body(w_ref, g_ref, m_ref, v_ref, d_ref, mo_ref, vo_ref):
        d_ref[...], mo_ref[...], vo_ref[...] = _adamw_step(w_ref[...], g_ref[...], m_ref[...], v_ref[...])

    return pl.pallas_call(
        body, name=name, grid=(r // rows,), in_specs=[spec] * 4, out_specs=[spec] * 3,
        out_shape=[jax.ShapeDtypeStruct((r, c), F32)] * 3, compiler_params=_params(("arbitrary",), 32),
    )(w, g, m, v)


def _adamw_step(w, grad, m, v):
    m_new = ADAM_B1 * m + (1.0 - ADAM_B1) * grad
    v_new = ADAM_B2 * v + (1.0 - ADAM_B2) * jnp.square(grad)
    m_hat = m_new / (1.0 - ADAM_B1 ** ADAM_STEP)
    v_hat = v_new / (1.0 - ADAM_B2 ** ADAM_STEP)
    return -ADAM_LR * (m_hat / (jnp.sqrt(v_hat) + ADAM_EPS) + ADAM_WD * w), m_new, v_new


def _adamw_layer(w, g, m, v, layer, prev, name):
    _, r, c = w.shape
    rows = next(cand for cand in (512, 256, 128, 64, 32, 16, 8) if r % cand == 0 and cand * c * 4 <= 2 * MIB)
    spec = pl.BlockSpec((None, rows, c), lambda i: (layer, i, 0))
    g_spec = pl.BlockSpec((rows, c), lambda i: (i, 0))
    n_prev = 0 if prev is None else 4

    def body(w_ref, g_ref, m_ref, v_ref, *rest):
        go_ref, d_ref, mo_ref, vo_ref = rest[n_prev:]
        grad = g_ref[...]
        go_ref[...] = grad
        d_ref[...], mo_ref[...], vo_ref[...] = _adamw_step(w_ref[...], grad, m_ref[...], v_ref[...])

    return pl.pallas_call(
        body, name=name, grid=(r // rows,), in_specs=[spec, g_spec, spec, spec] + [ANY_SPEC] * n_prev,
        out_specs=[spec] * 4,
        out_shape=[jax.ShapeDtypeStruct(w.shape, F32)] * 4,
        input_output_aliases={4 + i: i for i in range(n_prev)}, compiler_params=_params(("arbitrary",), 48),
    )(w, g, m, v, *(prev or ()))


def _layer_fwd(x0, p, weight, tabs):
    h1 = _rmsnorm_fwd(x0, p["ln_attn"], "ln_attn_fwd")
    proj = _mm_nn(h1, weight("w_in", h1), cols=True, tn=256, tk=D_MODEL, out_dtype=F32, name="mm_in")
    proj_r = _rope_fwd(proj, tabs["cos"], tabs["sin"], "rope_fwd")
    outs = _attention_fwd(proj_r, p["sink_b"], tabs["bias_a"], tabs["bias_b"], p["bias_c"])
    mixed = _mix_fwd(*outs, p["mix_gain"], "mix_fwd")
    x1 = _mm_nn(mixed, weight("w_out", mixed), cols=False, tn=256, tk=D_MODEL, out_dtype=F32, name="mm_out",
                residual=x0)
    h2 = _rmsnorm_fwd(x1, p["ln_ffn"], "ln_ffn_fwd")
    u0 = _mm_nn(h2, weight("w_up", h2), cols=True, tn=256, tk=D_MODEL, out_dtype=F32, name="mm_up", out_split=2)
    act = _convgate_fwd(u0, p["conv_w"], p["conv_b"], "convgate_fwd")
    x2 = _mm_nn(act, weight("w_down", act), cols=False, tn=512, tk=D_FF // 2, out_dtype=F32, name="mm_down",
                residual=x1)
    return x2, (x0, h1, proj_r, outs, mixed, x1, h2, u0, act)


def _layer_bwd(dx2, dx2_b, saved, p, big, tabs, begin, finish, pending):
    x0, h1, proj_r, outs, mixed, x1, h2, u0, act = saved
    d_act = _mm_nt(dx2_b, big["w_down"], cols=False, to=512, tr=D_MODEL, out_dtype=F32, name="nt_down",
                   after=[pending[1]] if pending else [])
    g_down = _mm_tn(act, dx2_b, tk=D_FF // N_SHARDS, tn=512, shards=-N_SHARDS, name="tn_down")
    du0, d_conv_w, d_conv_b = _convgate_bwd(u0, p["conv_w"], p["conv_b"], d_act, "convgate_bwd")
    token = [finish(pending[0], [du0])] if pending else []
    dh2 = _mm_nt(du0, big["w_up"], cols=True, to=1024, tr=D_FF // 4, out_dtype=F32, name="nt_up", after=token)
    g_up = _mm_tn(h2, du0, tk=512, tn=D_FF // 4, shards=N_SHARDS, name="tn_up")
    first, token = begin({"w_down": g_down, "w_up": g_up})
    dx1, dx1_b, d_ln_ffn = _rmsnorm_bwd(x1, p["ln_ffn"], dh2, dx2, "ln_ffn_bwd", after=[token])
    d_mixed = _mm_nt(dx1_b, big["w_out"], cols=False, to=512, tr=D_MODEL, out_dtype=F32, name="nt_out")
    g_out = _mm_tn(mixed, dx1_b, tk=D_MODEL // N_SHARDS, tn=512, shards=-N_SHARDS, name="tn_out")
    token = finish(first, [g_out])
    *d_outs, d_mix_gain = _mix_bwd(*outs, p["mix_gain"], d_mixed, "mix_bwd", after=[token])
    d_proj, d_sink, d_rpb = _attention_bwd(proj_r, p["sink_b"], tabs["bias_a"], tabs["bias_b"], p["bias_c"], outs,
                                           d_outs, tabs["cos"], tabs["sin"])
    dh1 = _mm_nt(d_proj, big["w_in"], cols=True, to=1024, tr=IN_COLS // N_SHARDS, out_dtype=F32, name="nt_in")
    g_in = _mm_tn(h1, d_proj, tk=512, tn=IN_COLS // N_SHARDS, shards=N_SHARDS, name="tn_in")
    dx0, dx0_b, d_ln_attn = _rmsnorm_bwd(x0, p["ln_attn"], dh1, dx1, "ln_attn_bwd")
    small = {"ln_attn": d_ln_attn, "sink_b": d_sink, "rpb_c": d_rpb, "mix_gain": d_mix_gain, "ln_ffn": d_ln_ffn,
             "conv_w": d_conv_w, "conv_b": d_conv_b}
    return dx0, dx0_b, small, begin({"w_out": g_out, "w_in": g_in})


HBM_SPEC = pl.BlockSpec(memory_space=pl.ANY)


def _place():
    x, y, c = lax.axis_index("x"), lax.axis_index("y"), lax.axis_index("c")
    chips = ((1 - x, y), (x, 1 - y), (1 - x, 1 - y))
    return x, y, c, chips


def _shard_index(px, py):
    return 2 * px + py


def _remote(src, dst, send_sem, recv_sem, to):
    return pltpu.make_async_remote_copy(src_ref=src, dst_ref=dst, send_sem=send_sem, recv_sem=recv_sem,
                                        device_id=to, device_id_type=MESH)


def _own_slot(w, layer, shard, name):
    _, r, c_dim = w.shape
    rows = r
    for cand in (512, 256, 128):
        if r % cand == 0 and cand * c_dim * 4 <= 2 * MIB:
            rows = cand
            break

    def body(s_ref, w_ref, o_ref):
        o_ref[...] = w_ref[...].astype(BF16)

    return pl.pallas_call(
        body, name=name,
        grid_spec=pltpu.PrefetchScalarGridSpec(
            num_scalar_prefetch=1, grid=(r // rows,),
            in_specs=[pl.BlockSpec((None, rows, c_dim), lambda i, s: (layer, i, 0))],
            out_specs=pl.BlockSpec((None, rows, c_dim), lambda i, s: (s[0], i, 0))),
        out_shape=jax.ShapeDtypeStruct((N_SHARDS, r, c_dim), BF16),
        compiler_params=_params(("arbitrary",), 32),
    )(shard.astype(jnp.int32).reshape(1), w)


HBM_ONLY = pl.BlockSpec(memory_space=pltpu.HBM)
SEM_SPEC = pl.BlockSpec(memory_space=pltpu.SEMAPHORE)
DATAFLOW = pltpu.SideEffectType.DATAFLOW_SIDE_EFFECTING


def _in_hbm(a):
    return pltpu.with_memory_space_constraint(a, pltpu.HBM)


N_DEV = 8


def _all_gather_small(vec, name, after=()):
    n_after = len(after)

    def body(v_ref, *rest):
        o_ref, send, recv, local_sem = rest[n_after:]
        x, y, c, _ = _place()
        me = 4 * x + 2 * y + c
        local = pltpu.make_async_copy(v_ref, o_ref.at[me], local_sem)
        local.start()
        flips = [(fx, fy, fc) for fx in (0, 1) for fy in (0, 1) for fc in (0, 1)][1:]
        peers = [((1 - x) if fx else x, (1 - y) if fy else y, (1 - c) if fc else c) for fx, fy, fc in flips]
        cps = [_remote(v_ref, o_ref.at[me], send.at[k], recv.at[k], peer) for k, peer in enumerate(peers)]
        for cp in cps:
            cp.start()
        for k, (px, py, pc) in enumerate(peers):
            slot = o_ref.at[4 * px + 2 * py + pc]
            _remote(slot, slot, send.at[k], recv.at[k], (px, py, pc)).wait_recv()
        for cp in cps:
            cp.wait_send()
        local.wait()

    return pl.pallas_call(
        body, name=name, in_specs=[HBM_SPEC] * (1 + n_after), out_specs=HBM_SPEC,
        out_shape=jax.ShapeDtypeStruct((N_DEV,) + vec.shape, vec.dtype),
        scratch_shapes=[pltpu.SemaphoreType.DMA((N_DEV - 1,))] * 2 + [pltpu.SemaphoreType.DMA(())],
    )(vec, *after)


def _peers(x, y, c):
    flips = [(fx, fy, fc) for fx in (0, 1) for fy in (0, 1) for fc in (0, 1)][1:]
    return [((1 - x) if fx else x, (1 - y) if fy else y, (1 - c) if fc else c) for fx, fy, fc in flips]


def _small_start(vec, after, name):
    n_after = len(after)

    def body(v_ref, slots_ref, *rest):
        send, recv = rest[n_after], rest[n_after + 1]
        token = rest[-1]
        x, y, c, _ = _place()
        me = 4 * x + 2 * y + c
        for k, peer in enumerate(_peers(x, y, c)):
            _remote(v_ref, slots_ref.at[me], send.at[k], recv.at[k], peer).start()
        token[...] = jnp.zeros_like(token)

    slots = jax.ShapeDtypeStruct((N_DEV,) + vec.shape, vec.dtype)
    res = pl.pallas_call(
        body, name=name,
        out_shape=(pltpu.SemaphoreType.DMA((N_DEV - 1,)), pltpu.SemaphoreType.DMA((N_DEV - 1,)),
                   pltpu.HBM(vec.shape, vec.dtype), pltpu.HBM(slots.shape, slots.dtype),
                   jax.ShapeDtypeStruct((8, LANES), F32)),
        in_specs=[HBM_ONLY, HBM_ONLY] + [ANY_SPEC] * n_after,
        out_specs=(SEM_SPEC, SEM_SPEC, HBM_ONLY, HBM_ONLY, pl.BlockSpec(memory_space=pltpu.VMEM)),
        input_output_aliases={0: 2, 1: 3},
        compiler_params=pltpu.CompilerParams(has_side_effects=DATAFLOW),
    )(_in_hbm(vec), _in_hbm(lax.empty(slots.shape, slots.dtype)), *after)
    return res


def _small_wait(send, recv, vec, slots, after, name):
    def body(v_ref, slots_ref, send_ref, recv_ref, *rest):
        x, y, c, _ = _place()
        for k, (px, py, pc) in enumerate(_peers(x, y, c)):
            cp = _remote(v_ref, slots_ref.at[4 * px + 2 * py + pc], send_ref.at[k], recv_ref.at[k], (px, py, pc))
            cp.wait_send()
            cp.wait_recv()

    return pl.pallas_call(
        body, name=name, out_shape=(pltpu.HBM(vec.shape, vec.dtype), pltpu.HBM(slots.shape, slots.dtype)),
        in_specs=[HBM_ONLY, HBM_ONLY, SEM_SPEC, SEM_SPEC] + [ANY_SPEC] * len(after), out_specs=[HBM_ONLY, HBM_ONLY],
        input_output_aliases={0: 0, 1: 1},
        compiler_params=pltpu.CompilerParams(has_side_effects=DATAFLOW),
    )(vec, slots, send, recv, *after)


def _small_sum(vec, slots, name):
    rows = vec.shape[0]
    blk = min(rows, 256)
    x, y, c = lax.axis_index("x"), lax.axis_index("y"), lax.axis_index("c")
    me = (4 * x + 2 * y + c).astype(jnp.int32).reshape(1)

    def slot_spec(k):
        return pl.BlockSpec((None, blk, LANES), lambda i, w: (jnp.where(w[0] == k, (k + 1) % N_DEV, k), i, 0))

    def body(w_ref, v_ref, *rest):
        o_ref = rest[-1]
        acc = None
        for k in range(N_DEV):
            term = jnp.where(w_ref[0] == k, v_ref[...], rest[k][...])
            acc = term if acc is None else acc + term
        o_ref[...] = acc

    return pl.pallas_call(
        body, name=name,
        grid_spec=pltpu.PrefetchScalarGridSpec(
            num_scalar_prefetch=1, grid=(rows // blk,),
            in_specs=[pl.BlockSpec((blk, LANES), lambda i, w: (i, 0))] + [slot_spec(k) for k in range(N_DEV)],
            out_specs=pl.BlockSpec((blk, LANES), lambda i, w: (i, 0))),
        out_shape=jax.ShapeDtypeStruct(vec.shape, F32), compiler_params=_params(("arbitrary",), 32),
    )(me, vec, *([slots] * N_DEV))


def _half(ref, slot, c):
    half = ref.shape[1] // 2
    return ref.at[slot, pl.ds(pl.multiple_of(c * half, 8), half)]


def _gather_start(bufs, after, name):
    n = len(bufs)
    n_after = len(after)

    def body(*refs):
        ins = refs[:n]
        send, recv = refs[n + n_after], refs[n + n_after + 1]
        token = refs[-1]
        x, y, c, chips = _place()
        me = _shard_index(x, y)
        for t in range(n):
            for j, (px, py) in enumerate(chips):
                mine = _half(ins[t], me, c)
                _remote(mine, mine, send.at[t * 3 + j], recv.at[t * 3 + j], (px, py, c)).start()
        token[...] = jnp.zeros_like(token)

    thru = [pltpu.HBM(b.shape, b.dtype) for b in bufs]
    res = pl.pallas_call(
        body, name=name,
        out_shape=(pltpu.SemaphoreType.DMA((n * 3,)), pltpu.SemaphoreType.DMA((n * 3,)), *thru,
                   jax.ShapeDtypeStruct((8, LANES), F32)),
        in_specs=[HBM_ONLY] * n + [ANY_SPEC] * n_after,
        out_specs=(SEM_SPEC, SEM_SPEC, *([HBM_ONLY] * n), pl.BlockSpec(memory_space=pltpu.VMEM)),
        input_output_aliases={i: 2 + i for i in range(n)},
        compiler_params=pltpu.CompilerParams(has_side_effects=DATAFLOW),
    )(*[_in_hbm(b) for b in bufs], *after)
    return res[0], res[1], list(res[2:2 + n]), res[-1]


def _gather_wait(send, recv, bufs, after, name):
    n = len(bufs)

    def body(*refs):
        ins = refs[:n]
        send_ref, recv_ref = refs[n], refs[n + 1]
        x, y, c, chips = _place()
        me = _shard_index(x, y)
        for t in range(n):
            for j, (px, py) in enumerate(chips):
                cp = _remote(_half(ins[t], me, c), _half(ins[t], _shard_index(px, py), c), send_ref.at[t * 3 + j],
                             recv_ref.at[t * 3 + j], (px, py, c))
                cp.wait_send()
                cp.wait_recv()

    res = pl.pallas_call(
        body, name=name, out_shape=tuple(pltpu.HBM(b.shape, b.dtype) for b in bufs),
        in_specs=[HBM_ONLY] * n + [SEM_SPEC, SEM_SPEC] + [ANY_SPEC] * len(after), out_specs=[HBM_ONLY] * n,
        input_output_aliases={i: i for i in range(n)},
        compiler_params=pltpu.CompilerParams(has_side_effects=DATAFLOW),
    )(*bufs, send, recv, *after)
    return list(res)


def _gather_forward(bufs, name):
    n = len(bufs)

    def body(*refs):
        outs = refs[n:2 * n]
        send, recv = refs[2 * n:]
        x, y, c, chips = _place()
        sibling = (x, y, 1 - c)
        cps = []
        for t in range(n):
            for j, (px, py) in enumerate(chips):
                got = _half(outs[t], _shard_index(px, py), c)
                cp = _remote(got, got, send.at[t * 3 + j], recv.at[t * 3 + j], sibling)
                cp.start()
                cps.append(cp)
        for t in range(n):
            for j, (px, py) in enumerate(chips):
                theirs = _half(outs[t], _shard_index(px, py), 1 - c)
                _remote(theirs, theirs, send.at[t * 3 + j], recv.at[t * 3 + j], sibling).wait_recv()
        for cp in cps:
            cp.wait_send()

    return pl.pallas_call(
        body, name=name, in_specs=[HBM_SPEC] * n, out_specs=[HBM_SPEC] * n,
        out_shape=[jax.ShapeDtypeStruct(b.shape, b.dtype) for b in bufs],
        input_output_aliases={t: t for t in range(n)},
        scratch_shapes=[pltpu.SemaphoreType.DMA((n * 3,))] * 2,
    )(*bufs)


def _gather_forward_start(bufs, carry, name):
    n = len(bufs)

    def body(*refs):
        ins = refs[:n]
        send, recv = refs[n + 1], refs[n + 2]
        x, y, c, chips = _place()
        for t in range(n):
            for j, (px, py) in enumerate(chips):
                got = _half(ins[t], _shard_index(px, py), c)
                _remote(got, got, send.at[t * 3 + j], recv.at[t * 3 + j], (x, y, 1 - c)).start()

    res = pl.pallas_call(
        body, name=name,
        out_shape=(pltpu.SemaphoreType.DMA((n * 3,)), pltpu.SemaphoreType.DMA((n * 3,)),
                   *[pltpu.HBM(b.shape, b.dtype) for b in bufs], pltpu.HBM(carry.shape, carry.dtype)),
        in_specs=[HBM_ONLY] * (n + 1),
        out_specs=(SEM_SPEC, SEM_SPEC, *([HBM_ONLY] * (n + 1))),
        input_output_aliases={i: 2 + i for i in range(n + 1)},
        compiler_params=pltpu.CompilerParams(has_side_effects=DATAFLOW),
    )(*[_in_hbm(b) for b in bufs], _in_hbm(carry))
    return res[0], res[1], list(res[2:2 + n]), res[-1]


def _gather_forward_wait(send, recv, bufs, after, name):
    n = len(bufs)

    def body(*refs):
        ins = refs[:n]
        send_ref, recv_ref = refs[n], refs[n + 1]
        x, y, c, chips = _place()
        for t in range(n):
            for j, (px, py) in enumerate(chips):
                s = _shard_index(px, py)
                cp = _remote(_half(ins[t], s, c), _half(ins[t], s, 1 - c), send_ref.at[t * 3 + j],
                             recv_ref.at[t * 3 + j], (x, y, 1 - c))
                cp.wait_send()
                cp.wait_recv()

    res = pl.pallas_call(
        body, name=name, out_shape=tuple(pltpu.HBM(b.shape, b.dtype) for b in bufs),
        in_specs=[HBM_ONLY] * n + [SEM_SPEC, SEM_SPEC] + [ANY_SPEC] * len(after), out_specs=[HBM_ONLY] * n,
        input_output_aliases={i: i for i in range(n)},
        compiler_params=pltpu.CompilerParams(has_side_effects=DATAFLOW),
    )(*bufs, send, recv, *after)
    return list(res)


def _sibling_rows(ref, c):
    half = ref.shape[1] // 2
    return ref.at[:, pl.ds(pl.multiple_of((1 - c) * half, 8), half)]


def _half_exchange_start(grads, name):
    n = len(grads)

    def body(*refs):
        ins, lands = refs[:n], refs[n:2 * n]
        send, recv = refs[2 * n], refs[2 * n + 1]
        token = refs[-1]
        x, y, c, _ = _place()
        for t in range(n):
            _remote(_sibling_rows(ins[t], c), lands[t], send.at[t], recv.at[t], (x, y, 1 - c)).start()
        token[...] = jnp.zeros_like(token)

    halves = [jax.ShapeDtypeStruct((g.shape[0], g.shape[1] // 2, g.shape[2]), g.dtype) for g in grads]
    res = pl.pallas_call(
        body, name=name,
        out_shape=(pltpu.SemaphoreType.DMA((n,)), pltpu.SemaphoreType.DMA((n,)),
                   *[pltpu.HBM(g.shape, g.dtype) for g in grads], *[pltpu.HBM(h.shape, h.dtype) for h in halves],
                   jax.ShapeDtypeStruct((8, LANES), F32)),
        in_specs=[HBM_ONLY] * (2 * n),
        out_specs=(SEM_SPEC, SEM_SPEC, *([HBM_ONLY] * (2 * n)), pl.BlockSpec(memory_space=pltpu.VMEM)),
        input_output_aliases={i: 2 + i for i in range(2 * n)},
        compiler_params=pltpu.CompilerParams(has_side_effects=DATAFLOW),
    )(*[_in_hbm(g) for g in grads], *[_in_hbm(lax.empty(h.shape, h.dtype)) for h in halves])
    return res[0], res[1], list(res[2:2 + n]), list(res[2 + n:2 + 2 * n]), res[-1]


def _half_exchange_wait(send, recv, grads, lands, after, name):
    n = len(grads)

    def body(*refs):
        ins, got = refs[:n], refs[n:2 * n]
        send_ref, recv_ref = refs[2 * n], refs[2 * n + 1]
        x, y, c, _ = _place()
        for t in range(n):
            cp = _remote(_sibling_rows(ins[t], c), got[t], send_ref.at[t], recv_ref.at[t], (x, y, 1 - c))
            cp.wait_send()
            cp.wait_recv()

    res = pl.pallas_call(
        body, name=name,
        out_shape=(*[pltpu.HBM(g.shape, g.dtype) for g in grads], *[pltpu.HBM(h.shape, h.dtype) for h in lands]),
        in_specs=[HBM_ONLY] * (2 * n) + [SEM_SPEC, SEM_SPEC] + [ANY_SPEC] * len(after),
        out_specs=[HBM_ONLY] * (2 * n),
        input_output_aliases={i: i for i in range(2 * n)},
        compiler_params=pltpu.CompilerParams(has_side_effects=DATAFLOW),
    )(*grads, *lands, send, recv, *after)
    return list(res[:n]), list(res[n:])


def _half_rows(half, c_dim):
    for cand in (512, 256, 128, 64):
        if half % cand == 0 and cand * c_dim * 2 <= 2 * MIB:
            return cand
    raise ValueError((half, c_dim))


def _core_index():
    return lax.axis_index("c").astype(jnp.int32).reshape(1)


def _half_sum(own, other, name):
    s, r, c_dim = own.shape
    rows = _half_rows(r // 2, c_dim)
    per = r // 2 // rows

    def body(c_ref, a_ref, b_ref, o_ref):
        o_ref[...] = (a_ref[...].astype(F32) + b_ref[...].astype(F32)).astype(BF16)

    return pl.pallas_call(
        body, name=name,
        grid_spec=pltpu.PrefetchScalarGridSpec(
            num_scalar_prefetch=1, grid=(s, per),
            in_specs=[pl.BlockSpec((None, rows, c_dim), lambda k, i, c: (k, c[0] * per + i, 0)),
                      pl.BlockSpec((None, rows, c_dim), lambda k, i, c: (k, i, 0))],
            out_specs=pl.BlockSpec((None, rows, c_dim), lambda k, i, c: (k, i, 0))),
        out_shape=pltpu.HBM((s, r // 2, c_dim), BF16), compiler_params=_params(("arbitrary", "arbitrary"), 32),
    )(_core_index(), own, other)


def _reduce_start(pairs, name):
    n = len(pairs)

    def body(*refs):
        ins, lands = refs[:n], refs[n:2 * n]
        send, recv = refs[2 * n], refs[2 * n + 1]
        token = refs[-1]
        x, y, c, chips = _place()
        me = _shard_index(x, y)
        for t in range(n):
            for j, (px, py) in enumerate(chips):
                _remote(ins[t].at[_shard_index(px, py)], lands[t].at[me], send.at[t * 3 + j], recv.at[t * 3 + j],
                        (px, py, c)).start()
        token[...] = jnp.zeros_like(token)

    thru = [pltpu.HBM(b.shape, b.dtype) for b in pairs]
    res = pl.pallas_call(
        body, name=name,
        out_shape=(pltpu.SemaphoreType.DMA((n * 3,)), pltpu.SemaphoreType.DMA((n * 3,)), *thru, *thru,
                   jax.ShapeDtypeStruct((8, LANES), F32)),
        in_specs=[HBM_ONLY] * (2 * n),
        out_specs=(SEM_SPEC, SEM_SPEC, *([HBM_ONLY] * (2 * n)), pl.BlockSpec(memory_space=pltpu.VMEM)),
        input_output_aliases={i: 2 + i for i in range(2 * n)},
        compiler_params=pltpu.CompilerParams(has_side_effects=DATAFLOW),
    )(*[_in_hbm(b) for b in pairs], *[_in_hbm(lax.empty(b.shape, b.dtype)) for b in pairs])
    return res[0], res[1], list(res[2:2 + n]), list(res[2 + n:2 + 2 * n]), res[-1]


def _reduce_wait(send, recv, pairs, lands, after, name):
    n = len(pairs)

    def body(*refs):
        ins, got = refs[:n], refs[n:2 * n]
        send_ref, recv_ref = refs[2 * n], refs[2 * n + 1]
        x, y, c, chips = _place()
        for t in range(n):
            for j, (px, py) in enumerate(chips):
                s = _shard_index(px, py)
                cp = _remote(ins[t].at[s], got[t].at[s], send_ref.at[t * 3 + j], recv_ref.at[t * 3 + j], (px, py, c))
                cp.wait_send()
                cp.wait_recv()

    thru = [pltpu.HBM(b.shape, b.dtype) for b in pairs]
    res = pl.pallas_call(
        body, name=name, out_shape=(*thru, *thru),
        in_specs=[HBM_ONLY] * (2 * n) + [SEM_SPEC, SEM_SPEC] + [ANY_SPEC] * len(after),
        out_specs=[HBM_ONLY] * (2 * n),
        input_output_aliases={i: i for i in range(2 * n)},
        compiler_params=pltpu.CompilerParams(has_side_effects=DATAFLOW),
    )(*pairs, *lands, send, recv, *after)
    return list(res[:n]), list(res[n:])


def _reduce_sum(pair, landed, name):
    s, half, c_dim = pair.shape
    rows = _half_rows(half, c_dim)
    per = half // rows
    shard = _shard_index(lax.axis_index("x"), lax.axis_index("y"))
    where = jnp.stack([shard, lax.axis_index("c")]).astype(jnp.int32)

    def landed_spec(k):
        return pl.BlockSpec((None, rows, c_dim), lambda i, w: (jnp.where(w[0] == k, (k + 1) % s, k), i, 0))

    def body(w_ref, own_ref, *rest):
        o_ref = rest[-1]
        acc = None
        for k in range(s):
            term = jnp.where(w_ref[0] == k, own_ref[...], rest[k][...]).astype(F32)
            acc = term if acc is None else acc + term
        o_ref[...] = acc

    return pl.pallas_call(
        body, name=name,
        grid_spec=pltpu.PrefetchScalarGridSpec(
            num_scalar_prefetch=1, grid=(per,),
            in_specs=[pl.BlockSpec((None, rows, c_dim), lambda i, w: (w[0], i, 0))] + [landed_spec(k) for k in range(s)],
            out_specs=pl.BlockSpec((rows, c_dim), lambda i, w: (w[1] * per + i, 0))),
        out_shape=pltpu.HBM((2 * half, c_dim), F32), compiler_params=_params(("arbitrary",), 40),
    )(where, pair, *([landed] * s))


def _my_rows(ref, c):
    half = ref.shape[0] // 2
    return ref.at[pl.ds(pl.multiple_of(c * half, 8), half)]


def _half_gather_start(bufs, name):
    n = len(bufs)

    def body(*refs):
        ins = refs[:n]
        send, recv = refs[n], refs[n + 1]
        token = refs[-1]
        x, y, c, _ = _place()
        for t in range(n):
            mine = _my_rows(ins[t], c)
            _remote(mine, mine, send.at[t], recv.at[t], (x, y, 1 - c)).start()
        token[...] = jnp.zeros_like(token)

    res = pl.pallas_call(
        body, name=name,
        out_shape=(pltpu.SemaphoreType.DMA((n,)), pltpu.SemaphoreType.DMA((n,)),
                   *[pltpu.HBM(b.shape, b.dtype) for b in bufs], jax.ShapeDtypeStruct((8, LANES), F32)),
        in_specs=[HBM_ONLY] * n,
        out_specs=(SEM_SPEC, SEM_SPEC, *([HBM_ONLY] * n), pl.BlockSpec(memory_space=pltpu.VMEM)),
        input_output_aliases={i: 2 + i for i in range(n)},
        compiler_params=pltpu.CompilerParams(has_side_effects=DATAFLOW),
    )(*[_in_hbm(b) for b in bufs])
    return res[0], res[1], list(res[2:2 + n]), res[-1]


def _half_gather_wait(send, recv, bufs, after, name):
    n = len(bufs)

    def body(*refs):
        ins = refs[:n]
        send_ref, recv_ref = refs[n], refs[n + 1]
        x, y, c, _ = _place()
        for t in range(n):
            cp = _remote(_my_rows(ins[t], c), _my_rows(ins[t], 1 - c), send_ref.at[t], recv_ref.at[t], (x, y, 1 - c))
            cp.wait_send()
            cp.wait_recv()

    res = pl.pallas_call(
        body, name=name, out_shape=tuple(pltpu.HBM(b.shape, b.dtype) for b in bufs),
        in_specs=[HBM_ONLY] * n + [SEM_SPEC, SEM_SPEC] + [ANY_SPEC] * len(after), out_specs=[HBM_ONLY] * n,
        input_output_aliases={i: i for i in range(n)},
        compiler_params=pltpu.CompilerParams(has_side_effects=DATAFLOW),
    )(*bufs, send, recv, *after)
    return list(res)


WEIGHT_NAMES = ("ln_attn", "w_in", "sink_b", "rpb_c", "mix_gain", "w_out", "ln_ffn", "w_up", "conv_w", "conv_b",
                "w_down", "ln_final")
BIG_NAMES = ("w_in", "w_out", "w_up", "w_down")
REPLICATED_NAMES = ("ln_attn", "sink_b", "rpb_c", "mix_gain", "ln_ffn", "conv_b", "ln_final")
PACK_TILE = 8 * LANES


def _pack(arrays, row_multiple):
    pieces = []
    for a in arrays:
        flat = a.reshape(-1)
        pieces.append(jnp.pad(flat, (0, (-flat.shape[0]) % PACK_TILE)))
    flat = jnp.concatenate(pieces)
    flat = jnp.pad(flat, (0, (-flat.shape[0]) % (row_multiple * LANES)))
    return flat.reshape(-1, LANES)


def _unpack(packed, shapes):
    flat = packed.reshape(-1)
    out, off = [], 0
    for shape in shapes:
        size = math.prod(shape)
        out.append(flat[off:off + size].reshape(shape))
        off += size + (-size) % PACK_TILE
    return out


def kernel(x, ln_attn, w_in, sink_b, rpb_c, mix_gain, w_out, ln_ffn, w_up, conv_w, conv_b, w_down, ln_final, loss_target, m_ln_attn, m_w_in, m_sink_b, m_rpb_c, m_mix_gain, m_w_out, m_ln_ffn, m_w_up, m_conv_w, m_conv_b, m_w_down, m_ln_final, v_ln_attn, v_w_in, v_sink_b, v_rpb_c, v_mix_gain, v_w_out, v_ln_ffn, v_w_up, v_conv_w, v_conv_b, v_w_down, v_ln_final):
    w = dict(ln_attn=ln_attn, w_in=w_in, sink_b=sink_b, rpb_c=rpb_c, mix_gain=mix_gain, w_out=w_out, ln_ffn=ln_ffn,
             w_up=w_up, conv_w=conv_w, conv_b=conv_b, w_down=w_down, ln_final=ln_final)
    m = dict(ln_attn=m_ln_attn, w_in=m_w_in, sink_b=m_sink_b, rpb_c=m_rpb_c, mix_gain=m_mix_gain, w_out=m_w_out,
             ln_ffn=m_ln_ffn, w_up=m_w_up, conv_w=m_conv_w, conv_b=m_conv_b, w_down=m_w_down, ln_final=m_ln_final)
    v = dict(ln_attn=v_ln_attn, w_in=v_w_in, sink_b=v_sink_b, rpb_c=v_rpb_c, mix_gain=v_mix_gain, w_out=v_w_out,
             ln_ffn=v_ln_ffn, w_up=v_w_up, conv_w=v_conv_w, conv_b=v_conv_b, w_down=v_w_down, ln_final=v_ln_final)
    shard = _shard_index(lax.axis_index("x"), lax.axis_index("y"))
    up_cols = w_up.shape[2]

    conv_slots = _all_gather_small(_pack([conv_w], 8), "gather_conv_w")
    conv_all = conv_slots[0::2].reshape(N_SHARDS, -1)[:, :conv_w.size].reshape((N_SHARDS,) + conv_w.shape)

    arrivals = []
    group_of = {}
    tokens = []
    rest = ("w_out", "w_up", "w_down")
    for l, names in ((0, ("w_in",)), (0, rest), (1, ("w_in",)), (1, rest)):
        bufs = [_own_slot(w[k], l, shard, "own_" + k) for k in names]
        send, recv, bufs, token = _gather_start(bufs, tokens[-1:] or [conv_slots], "gather_start_%d" % len(arrivals))
        tokens.append(token)
        for k in names:
            group_of[l, k] = len(arrivals)
        arrivals.append({"names": names, "send": send, "recv": recv, "bufs": bufs, "done": None})

    def gathered(l, name, after):
        idx = group_of[l, name]
        group = arrivals[idx]

        def whole(k, buf):
            return buf.reshape(1, -1, buf.shape[2]) if k in ("w_out", "w_down") else buf

        if group["done"] is None:
            bufs = _gather_wait(group["send"], group["recv"], group["bufs"], list(after) + tokens[-1:],
                                "gather_wait_%d" % idx)
            first = _gather_forward(bufs[:1], "gather_forward_%d" % idx)[0]
            if len(bufs) > 1:
                send, recv, rest, first = _gather_forward_start(bufs[1:], first, "gather_forward_start_%d" % idx)
                group["rest"] = (send, recv, rest)
            group["done"] = {group["names"][0]: whole(group["names"][0], first)}
        if name not in group["done"]:
            send, recv, rest = group["rest"]
            rest = _gather_forward_wait(send, recv, rest, list(after), "gather_forward_wait_%d" % idx)
            group["done"].update({k: whole(k, buf) for k, buf in zip(group["names"][1:], rest)})
        return group["done"][name]

    cos, sin = _rope_tables(SEQ)
    tabs = {"cos": cos, "sin": sin, "bias_a": _bias_a(), "bias_b": _bias_b()}
    layers = []
    for l in range(DEPTH):
        conv_w_l = conv_all[:, l].reshape(2, N_SHARDS // 2, 3, up_cols).transpose(0, 2, 1, 3).reshape(2, 3, D_FF)
        layers.append({"ln_attn": ln_attn[l][None], "sink_b": sink_b[l], "bias_c": _bias_c(rpb_c[l]),
                       "mix_gain": mix_gain[l][None], "ln_ffn": ln_ffn[l][None], "conv_w": conv_w_l,
                       "conv_b": conv_b[l].reshape(2, 1, D_FF)})

    act = x[0]
    saved = []
    for l in range(DEPTH):
        act, keep = _layer_fwd(act, layers[l], lambda name, after, l=l: gathered(l, name, [after]), tabs)
        saved.append(keep)
    loss_part, dx, dx_b, d_ln_final = _loss_head(act, ln_final[None], loss_target[0], "loss_head")
    loss = lax.psum(loss_part[0, 0], ("x", "y", "c"))

    reductions = []

    opened = [0]

    def begin(l, partial):
        idx = opened[0]
        opened[0] += 1
        names = tuple(partial)
        send_sem, recv_sem, mine, theirs, token = _half_exchange_start([partial[k] for k in names],
                                                                       "half_exchange_start_%d" % idx)
        return {"idx": idx, "layer": l, "names": names, "send": send_sem, "recv": recv_sem, "mine": mine,
                "theirs": theirs}, token

    def finish(handle, after):
        idx, names = handle["idx"], handle["names"]
        mine, theirs = _half_exchange_wait(handle["send"], handle["recv"], handle["mine"], handle["theirs"], after,
                                           "half_exchange_wait_%d" % idx)
        pairs = [_half_sum(a, b, "half_sum_" + k) for k, a, b in zip(names, mine, theirs)]
        send_sem, recv_sem, pairs, lands, token = _reduce_start(pairs, "reduce_start_%d" % idx)
        reductions.append({"layer": handle["layer"], "names": names, "send": send_sem, "recv": recv_sem,
                           "pairs": pairs, "lands": lands})
        return token

    small = [None] * DEPTH
    pending = None
    for l in reversed(range(DEPTH)):
        big = {k: gathered(l, k, []) for k in BIG_NAMES}
        dx, dx_b, small[l], pending = _layer_bwd(dx, dx_b, saved[l], layers[l], big, tabs,
                                                 functools.partial(begin, l), finish, pending)
    after = [finish(pending[0], [pending[1]])]

    stacked = {k: jnp.stack([small[l][k] for l in range(DEPTH)]) for k in small[0]}
    part = {"ln_attn": stacked["ln_attn"][:, 0], "sink_b": stacked["sink_b"], "rpb_c": stacked["rpb_c"],
            "mix_gain": stacked["mix_gain"][:, 0], "ln_ffn": stacked["ln_ffn"][:, 0],
            "conv_b": stacked["conv_b"].reshape(DEPTH, 2 * D_FF), "ln_final": d_ln_final[0],
            "conv_w": stacked["conv_w"].transpose(0, 2, 1, 3).reshape(DEPTH, 3, 2 * D_FF)}
    small_names = REPLICATED_NAMES + ("conv_w",)
    small_send, small_recv, small_vec, small_slots, token = _small_start(
        _pack([part[k] for k in small_names], 256), after, "small_grads_start")
    after = [token]

    grads, delta, new_m, new_v = {}, {}, {}, {}
    updated = dict.fromkeys(BIG_NAMES)

    def arrive(idx, after):
        group = reductions[idx]
        pairs, lands = _reduce_wait(group["send"], group["recv"], group["pairs"], group["lands"], after,
                                    "reduce_wait_%d" % idx)
        halves = [_reduce_sum(pair, landed, "reduce_sum_" + k) for k, pair, landed in zip(group["names"], pairs, lands)]
        send_sem, recv_sem, halves, token = _half_gather_start(halves, "half_gather_start_%d" % idx)
        return {"idx": idx, "send": send_sem, "recv": recv_sem, "bufs": halves, "names": group["names"],
                "layer": group["layer"]}, [token]

    def update(swap, after):
        whole = _half_gather_wait(swap["send"], swap["recv"], swap["bufs"], after,
                                  "half_gather_wait_%d" % swap["idx"])
        for k, g in zip(swap["names"], whole):
            updated[k] = _adamw_layer(w[k], g, m[k], v[k], swap["layer"], updated[k], "adamw_" + k)
        return [updated[k][0] for k in swap["names"]]

    swaps = []
    for idx in range(len(reductions) - 1):
        swap, after = arrive(idx, after)
        swaps.append(swap)
    for swap in swaps[:2]:
        after = update(swap, after)
    swap, after = arrive(len(reductions) - 1, after)
    for swap in swaps[2:] + [swap]:
        after = update(swap, after)
    for k in BIG_NAMES:
        grads[k], delta[k], new_m[k], new_v[k] = updated[k]

    small_vec, small_slots = _small_wait(small_send, small_recv, small_vec, small_slots, after, "small_grads_wait")
    total = _small_sum(small_vec, small_slots, "small_grads_sum")
    for k, g in zip(small_names, _unpack(total, [part[k].shape for k in small_names])):
        grads[k] = g
    grads["conv_w"] = lax.dynamic_slice_in_dim(grads["conv_w"], shard * up_cols, up_cols, axis=2)

    flat = (DEPTH * 3, up_cols)
    res = _adamw(conv_w.reshape(flat), grads["conv_w"].reshape(flat), m["conv_w"].reshape(flat),
                 v["conv_w"].reshape(flat), "adamw_conv_w")
    delta["conv_w"], new_m["conv_w"], new_v["conv_w"] = (r.reshape(conv_w.shape) for r in res)
    shapes = [w[k].shape for k in REPLICATED_NAMES]
    packed = [_pack([d[k] for k in REPLICATED_NAMES], 128) for d in (w, grads, m, v)]
    for d, res in zip((delta, new_m, new_v), _adamw(*packed, "adamw_small")):
        for k, r in zip(REPLICATED_NAMES, _unpack(res, shapes)):
            d[k] = r

    return (loss, dx[None], *[grads[k] for k in WEIGHT_NAMES], *[delta[k] for k in WEIGHT_NAMES],
            *[new_m[k] for k in WEIGHT_NAMES], *[new_v[k] for k in WEIGHT_NAMES])
```

```python
import functools
import math

import jax
import jax.numpy as jnp
from jax import lax
from jax.experimental import pallas as pl
from jax.experimental.pallas import tpu as pltpu

F32 = jnp.float32
BF16 = jnp.bfloat16
MESH = pl.DeviceIdType.MESH

D_MODEL = 2048
SEQ = 2048
DEPTH = 2
HEAD_DIM = 64
N_HEADS_A = 12
N_HEADS_B = 10
N_KV_B = 2
N_HEADS_C = 10
WINDOW_B = 128
GRID_W = 64
NA_ROWS = 8
NA_COLS = 16
WIDTH_A = N_HEADS_A * HEAD_DIM
WIDTH_B = N_HEADS_B * HEAD_DIM
WIDTH_C = N_HEADS_C * HEAD_DIM
IN_COLS = 5120
D_FF = 5632
ROPE_THETA = 10000.0
EPS = 1e-6
NEG_INF = -1e30
N_SHARDS = 4

ADAM_LR = 0.001
ADAM_B1 = 0.9
ADAM_B2 = 0.999
ADAM_EPS = 1e-08
ADAM_WD = 0.01
ADAM_STEP = 10

LANES = 128
QB = 256
NQB = SEQ // QB
ROWS = 256
MIB = 2 ** 20

A_BLK = (0, 6, 12)
B_BLK = (18, 23, 24)
C_BLK = (25, 30, 35)
ROPE_BLKS = tuple(range(0, 12)) + tuple(range(18, 24))
QSCALE_BLKS = tuple(range(0, 6)) + tuple(range(18, 23)) + tuple(range(25, 30))
N_PBLK = IN_COLS // LANES


def _params(sem, vmem_mib):
    return pltpu.CompilerParams(dimension_semantics=sem, vmem_limit_bytes=vmem_mib * MIB)


def _weight_spec(w, cols, t_in, t_out, transposed):
    s, r, c = w.shape
    if cols:
        per = c // t_out
        k_dim, n = r, s * c
        if transposed:
            index = lambda j, rr: (rr // per, j, rr % per)
        else:
            index = lambda j, kk: (j // per, kk, j % per)
    else:
        per = r // t_in
        k_dim, n = s * r, c
        if transposed:
            index = lambda j, rr: (j // per, j % per, rr)
        else:
            index = lambda j, kk: (kk // per, kk % per, j)
    return pl.BlockSpec((None, t_in, t_out), index), k_dim, n


def _mm_nn(a, w, *, cols, tn, tk, out_dtype, name, residual=None, out_split=1):
    m, k_dim = a.shape
    w_spec, k_w, n = _weight_spec(w, cols, tk, tn, False)
    assert k_w == k_dim
    nj, nk = n // tn, k_dim // tk
    in_specs = [pl.BlockSpec((m, tk), lambda j, k: (0, k)), w_spec]
    args = [a, w]
    if residual is not None:
        in_specs.append(pl.BlockSpec((m, tn), lambda j, k: (0, j)))
        args.append(residual)
    if out_split > 1:
        per_o = n // out_split // tn
        out_spec = pl.BlockSpec((None, m, tn), lambda j, k: (j // per_o, 0, j % per_o))
        out_shape = pltpu.HBM((out_split, m, n // out_split), out_dtype)
    else:
        out_spec = pl.BlockSpec((m, tn), lambda j, k: (0, j))
        out_shape = pltpu.HBM((m, n), out_dtype)

    def body(*refs):
        a_ref, w_ref = refs[0], refs[1]
        r_ref = refs[2] if residual is not None else None
        o_ref = refs[3] if residual is not None else refs[2]

        def finish(val):
            if r_ref is not None:
                val = r_ref[...] + val
            o_ref[...] = val.astype(o_ref.dtype)

        part = jnp.dot(a_ref[...], w_ref[...], preferred_element_type=F32)
        if nk == 1:
            finish(part)
        else:
            acc = refs[-1]
            kk = pl.program_id(1)

            @pl.when(kk == 0)
            def _():
                acc[...] = part

            @pl.when(kk > 0)
            def _():
                acc[...] += part

            @pl.when(kk == nk - 1)
            def _():
                finish(acc[...])

    return pl.pallas_call(
        body, name=name, grid=(nj, nk), in_specs=in_specs, out_specs=out_spec, out_shape=out_shape,
        scratch_shapes=[pltpu.VMEM((m, tn), F32)] if nk > 1 else [],
        compiler_params=_params(("arbitrary", "arbitrary"), 56),
    )(*[_in_hbm(a) for a in args])


ANY_SPEC = pl.BlockSpec(memory_space=pl.ANY)


def _mm_nt(dy, w, *, cols, to, tr, out_dtype, name, after=()):
    if dy.ndim == 3:
        m = dy.shape[1]
        n = dy.shape[0] * dy.shape[2]
        per_d = dy.shape[2] // tr
        dy_spec = pl.BlockSpec((None, m, tr), lambda j, r: (r // per_d, 0, r % per_d))
    else:
        m, n = dy.shape
        dy_spec = pl.BlockSpec((m, tr), lambda j, r: (0, r))
    w_spec, k_dim, n_w = _weight_spec(w, cols, to, tr, True)
    assert n_w == n
    nj, nr = k_dim // to, n // tr

    n_after = len(after)

    def body(dy_ref, w_ref, *rest):
        o_ref = rest[n_after]
        part = lax.dot_general(dy_ref[...], w_ref[...], (((1,), (1,)), ((), ())), preferred_element_type=F32)
        if nr == 1:
            o_ref[...] = part.astype(o_ref.dtype)
        else:
            acc = rest[n_after + 1]
            rr = pl.program_id(1)

            @pl.when(rr == 0)
            def _():
                acc[...] = part

            @pl.when(rr > 0)
            def _():
                acc[...] += part

            @pl.when(rr == nr - 1)
            def _():
                o_ref[...] = acc[...].astype(o_ref.dtype)

    return pl.pallas_call(
        body, name=name, grid=(nj, nr), in_specs=[dy_spec, w_spec] + [ANY_SPEC] * n_after,
        out_specs=pl.BlockSpec((m, to), lambda j, r: (0, j)),
        out_shape=pltpu.HBM((m, k_dim), out_dtype),
        scratch_shapes=[pltpu.VMEM((m, to), F32)] if nr > 1 else [],
        compiler_params=_params(("arbitrary", "arbitrary"), 56),
    )(_in_hbm(dy), _in_hbm(w), *after)


def _mm_tn(x, dy, *, tk, tn, shards, name):
    m, k_dim = x.shape
    if dy.ndim == 3:
        n = dy.shape[0] * dy.shape[2]
        per_d = dy.shape[2] // tn
        dy_spec = pl.BlockSpec((None, m, tn), lambda i, j: (j // per_d, 0, j % per_d))
    else:
        n = dy.shape[1]
        dy_spec = pl.BlockSpec((m, tn), lambda i, j: (0, j))
    if shards > 0:
        per = n // shards // tn
        out_shape = pltpu.HBM((shards, k_dim, n // shards), BF16)
        out_spec = pl.BlockSpec((None, tk, tn), lambda i, j: (j // per, i, j % per))
    else:
        s = -shards
        per = k_dim // s // tk
        out_shape = pltpu.HBM((s, k_dim // s, n), BF16)
        out_spec = pl.BlockSpec((None, tk, tn), lambda i, j: (i // per, i % per, j))

    def body(x_ref, dy_ref, o_ref):
        o_ref[...] = lax.dot_general(x_ref[...], dy_ref[...], (((0,), (0,)), ((), ())),
                                     preferred_element_type=F32).astype(BF16)

    return pl.pallas_call(
        body, name=name, grid=(k_dim // tk, n // tn),
        in_specs=[pl.BlockSpec((m, tk), lambda i, j: (0, i)), dy_spec], out_specs=out_spec, out_shape=out_shape,
        compiler_params=_params(("arbitrary", "arbitrary"), 56),
    )(_in_hbm(x), _in_hbm(dy))


def _row_spec(width, rows=ROWS):
    return pl.BlockSpec((rows, width), lambda i: (i, 0))


def _vec_spec(width):
    return pl.BlockSpec((1, width), lambda i: (0, 0))


def _rms_stats(x):
    r = lax.rsqrt(jnp.mean(x * x, axis=-1, keepdims=True) + EPS)
    return r, x * r


def _rmsnorm_fwd(x, gain, name):
    t, d = x.shape

    def body(x_ref, g_ref, o_ref):
        _, n = _rms_stats(x_ref[...])
        o_ref[...] = (n * g_ref[...]).astype(BF16)

    return pl.pallas_call(
        body, name=name, grid=(t // ROWS,), in_specs=[_row_spec(d), _vec_spec(d)], out_specs=_row_spec(d),
        out_shape=pltpu.HBM((t, d), BF16), compiler_params=_params(("arbitrary",), 32),
    )(_in_hbm(x), _in_hbm(gain))


def _rmsnorm_bwd(x, gain, dh, dres, name, after=()):
    t, d = x.shape
    n_after = len(after)

    def body(x_ref, g_ref, dh_ref, dres_ref, *rest):
        dx_ref, dxb_ref, dg_ref = rest[n_after:]
        r, n = _rms_stats(x_ref[...])
        dh_v = dh_ref[...]
        dn = dh_v * g_ref[...]
        dx = dres_ref[...] + r * (dn - n * jnp.mean(dn * n, axis=-1, keepdims=True))
        dx_ref[...] = dx
        dxb_ref[...] = dx.astype(BF16)
        part = jnp.sum(dh_v * n, axis=0, keepdims=True)

        @pl.when(pl.program_id(0) == 0)
        def _():
            dg_ref[...] = part

        @pl.when(pl.program_id(0) > 0)
        def _():
            dg_ref[...] += part

    return pl.pallas_call(
        body, name=name, grid=(t // ROWS,),
        in_specs=[_row_spec(d), _vec_spec(d), _row_spec(d), _row_spec(d)] + [ANY_SPEC] * n_after,
        out_specs=[_row_spec(d), _row_spec(d), _vec_spec(d)],
        out_shape=[pltpu.HBM((t, d), F32), pltpu.HBM((t, d), BF16), jax.ShapeDtypeStruct((1, d), F32)],
        compiler_params=_params(("arbitrary",), 40),
    )(_in_hbm(x), _in_hbm(gain), _in_hbm(dh), _in_hbm(dres), *after)


def _loss_head(x, gain, target, name):
    t, d = x.shape

    def body(x_ref, g_ref, t_ref, loss_ref, dx_ref, dxb_ref, dg_ref):
        r, n = _rms_stats(x_ref[...])
        g = g_ref[...]
        err = n * g - t_ref[...]
        dy = err * (1.0 / d)
        dn = dy * g
        dx = r * (dn - n * jnp.mean(dn * n, axis=-1, keepdims=True))
        dx_ref[...] = dx
        dxb_ref[...] = dx.astype(BF16)
        part = jnp.sum(dy * n, axis=0, keepdims=True)
        lpart = jnp.zeros((8, LANES), F32) + 0.5 * jnp.sum(jnp.mean(err * err, axis=-1, keepdims=True))

        @pl.when(pl.program_id(0) == 0)
        def _():
            dg_ref[...] = part
            loss_ref[...] = lpart

        @pl.when(pl.program_id(0) > 0)
        def _():
            dg_ref[...] += part
            loss_ref[...] += lpart

    return pl.pallas_call(
        body, name=name, grid=(t // ROWS,),
        in_specs=[_row_spec(d), _vec_spec(d), _row_spec(d)],
        out_specs=[pl.BlockSpec((8, LANES), lambda i: (0, 0)), _row_spec(d), _row_spec(d), _vec_spec(d)],
        out_shape=[jax.ShapeDtypeStruct((8, LANES), F32), pltpu.HBM((t, d), F32), pltpu.HBM((t, d), BF16),
                   jax.ShapeDtypeStruct((1, d), F32)],
        compiler_params=_params(("arbitrary",), 40),
    )(x, gain, target)


def _swap_halves(x):
    lane = lax.broadcasted_iota(jnp.int32, x.shape, 1)
    return jnp.where((lane % HEAD_DIM) < HEAD_DIM // 2, pltpu.roll(x, LANES - HEAD_DIM // 2, 1),
                     pltpu.roll(x, HEAD_DIM // 2, 1))


def _rope_tables(t):
    inv_freq = ROPE_THETA ** (-jnp.arange(0, HEAD_DIM, 2, dtype=F32) / HEAD_DIM)
    ang = jnp.arange(t, dtype=F32)[:, None] * inv_freq[None, :]
    cos = jnp.tile(jnp.cos(ang), (1, LANES // (HEAD_DIM // 2)))
    sin = jnp.tile(jnp.sin(ang), (1, LANES // (HEAD_DIM // 2)))
    lane = jnp.arange(LANES)[None, :]
    return cos, jnp.where((lane % HEAD_DIM) < HEAD_DIM // 2, -sin, sin)


def _rope_fwd(proj, cos, sin, name):
    t = proj.shape[0]
    scale = HEAD_DIM ** -0.5

    def body(p_ref, c_ref, s_ref, o_ref):
        cos_v, sin_v = c_ref[...], s_ref[...]
        for b in range(N_PBLK):
            cols = slice(b * LANES, (b + 1) * LANES)
            v = p_ref[:, cols]
            if b in ROPE_BLKS:
                v = v * cos_v + _swap_halves(v) * sin_v
            if b in QSCALE_BLKS:
                v = v * scale
            o_ref[:, cols] = v.astype(BF16)

    return pl.pallas_call(
        body, name=name, grid=(t // ROWS,),
        in_specs=[_row_spec(IN_COLS), _row_spec(LANES), _row_spec(LANES)], out_specs=_row_spec(IN_COLS),
        out_shape=pltpu.HBM((t, IN_COLS), BF16), compiler_params=_params(("arbitrary",), 40),
    )(_in_hbm(proj), _in_hbm(cos), _in_hbm(sin))


def _rope_bwd(grads, cos, sin, name):
    t = grads[0].shape[0]
    scale = HEAD_DIM ** -0.5
    group = N_HEADS_B // N_KV_B

    def body(*refs):
        c_ref, s_ref, o_ref = refs[9], refs[10], refs[11]
        cos_v, sin_v = c_ref[...], s_ref[...]

        def kv_sum(ref):
            parts = []
            for g in range(N_KV_B):
                acc = ref[:, g * group * HEAD_DIM:(g * group + 1) * HEAD_DIM]
                for h in range(g * group + 1, (g + 1) * group):
                    acc = acc + ref[:, h * HEAD_DIM:(h + 1) * HEAD_DIM]
                parts.append(acc)
            return jnp.concatenate(parts, axis=1)

        def emit(b, v):
            if b in ROPE_BLKS:
                v = v * cos_v - _swap_halves(v) * sin_v
            if b in QSCALE_BLKS:
                v = v * scale
            o_ref[:, b * LANES:(b + 1) * LANES] = v.astype(BF16)

        starts = (A_BLK[0], A_BLK[1], A_BLK[2], B_BLK[0], None, None, C_BLK[0], C_BLK[1], C_BLK[2])
        for idx, start in enumerate(starts):
            if start is None:
                continue
            for j in range(refs[idx].shape[1] // LANES):
                emit(start + j, refs[idx][:, j * LANES:(j + 1) * LANES])
        emit(B_BLK[1], kv_sum(refs[4]))
        emit(B_BLK[2], kv_sum(refs[5]))

    return pl.pallas_call(
        body, name=name, grid=(t // ROWS,),
        in_specs=[_row_spec(g.shape[1]) for g in grads] + [_row_spec(LANES), _row_spec(LANES)],
        out_specs=_row_spec(IN_COLS),
        out_shape=pltpu.HBM((t, IN_COLS), BF16), compiler_params=_params(("arbitrary",), 40),
    )(*[_in_hbm(g) for g in grads], _in_hbm(cos), _in_hbm(sin))


GROUP_COLS = ((0, WIDTH_A), (WIDTH_A, WIDTH_A + WIDTH_B), (WIDTH_A + WIDTH_B, D_MODEL))


def _mix_fwd(oa, ob, oc, gain, name):
    t = oa.shape[0]

    def body(a_ref, b_ref, c_ref, g_ref, o_ref):
        for ref, (lo, hi) in zip((a_ref, b_ref, c_ref), GROUP_COLS):
            _, n = _rms_stats(ref[...])
            o_ref[:, lo:hi] = (n * g_ref[:, lo:hi]).astype(BF16)

    return pl.pallas_call(
        body, name=name, grid=(t // ROWS,),
        in_specs=[_row_spec(WIDTH_A), _row_spec(WIDTH_B), _row_spec(WIDTH_C), _vec_spec(D_MODEL)],
        out_specs=_row_spec(D_MODEL),
        out_shape=pltpu.HBM((t, D_MODEL), BF16), compiler_params=_params(("arbitrary",), 32),
    )(_in_hbm(oa), _in_hbm(ob), _in_hbm(oc), _in_hbm(gain))


def _mix_bwd(oa, ob, oc, gain, dmixed, name, after=()):
    t = oa.shape[0]
    n_after = len(after)

    def body(a_ref, b_ref, c_ref, g_ref, dm_ref, *rest):
        da_ref, db_ref, dc_ref, dg_ref = rest[n_after:]
        first = pl.program_id(0) == 0
        for ref, dref, (lo, hi) in zip((a_ref, b_ref, c_ref), (da_ref, db_ref, dc_ref), GROUP_COLS):
            r, n = _rms_stats(ref[...])
            dm = dm_ref[:, lo:hi]
            dn = dm * g_ref[:, lo:hi]
            dref[...] = r * (dn - n * jnp.mean(dn * n, axis=-1, keepdims=True))
            part = jnp.sum(dm * n, axis=0, keepdims=True)

            @pl.when(first)
            def _():
                dg_ref[:, lo:hi] = part

            @pl.when(jnp.logical_not(first))
            def _():
                dg_ref[:, lo:hi] += part

    return pl.pallas_call(
        body, name=name, grid=(t // ROWS,),
        in_specs=[_row_spec(WIDTH_A), _row_spec(WIDTH_B), _row_spec(WIDTH_C), _vec_spec(D_MODEL), _row_spec(D_MODEL)]
        + [ANY_SPEC] * n_after,
        out_specs=[_row_spec(WIDTH_A), _row_spec(WIDTH_B), _row_spec(WIDTH_C), _vec_spec(D_MODEL)],
        out_shape=[pltpu.HBM((t, WIDTH_A), F32), pltpu.HBM((t, WIDTH_B), F32), pltpu.HBM((t, WIDTH_C), F32),
                   jax.ShapeDtypeStruct((1, D_MODEL), F32)],
        compiler_params=_params(("arbitrary",), 40),
    )(_in_hbm(oa), _in_hbm(ob), _in_hbm(oc), _in_hbm(gain), _in_hbm(dmixed), *after)


FF_COLS = 256


SUBLANES = 8
CHUNK_FWD = 256
CHUNK_BWD = 128
HALO = SUBLANES


def _ext_rows(ref, r0, chunk, where):
    t, cols = ref.shape
    zeros = jnp.zeros((HALO, cols), F32)
    if where == "first":
        return jnp.concatenate([zeros, ref[0:chunk + HALO, :]], axis=0)
    if where == "last":
        return jnp.concatenate([ref[t - chunk - HALO:t, :], zeros], axis=0)
    return ref[pl.ds(pl.multiple_of(r0 - HALO, HALO), chunk + 2 * HALO), :]


def _for_chunks(t, chunk, fn):
    fn(0, "first")

    def mid(ci, carry):
        fn(pl.multiple_of(ci * chunk, chunk), "mid")
        return carry

    lax.fori_loop(1, t // chunk - 1, mid, 0)
    fn(t - chunk, "last")


def _roll_rows(x, by):
    return pltpu.roll(x, by % x.shape[0], 0)


def _gate_val(u_ref, r0, chunk, where, w_ref, b_ref):
    ext = [_ext_rows(u_ref.at[h], r0, chunk, where) for h in range(2)]
    before = [_roll_rows(e, 1) for e in ext]
    after = [_roll_rows(e, -1) for e in ext]
    gate, val = ((before[h] * w_ref[h, 0:1, :] + ext[h] * w_ref[h, 1:2, :]) + after[h] * w_ref[h, 2:3, :] + b_ref[h]
                 for h in range(2))
    return gate, val, ext, before, after


def _ff_specs(t):
    u_spec = pl.BlockSpec((2, t, FF_COLS), lambda j: (0, 0, j))
    w_spec = pl.BlockSpec((2, 3, FF_COLS), lambda j: (0, 0, j))
    b_spec = pl.BlockSpec((2, 1, FF_COLS), lambda j: (0, 0, j))
    return u_spec, w_spec, b_spec


def _convgate_fwd(u0, conv_w, conv_b, name):
    t = u0.shape[1]
    u_spec, w_spec, b_spec = _ff_specs(t)

    def body(u_ref, w_ref, b_ref, o_ref):
        def chunk(r0, where):
            gate, val, _, _, _ = _gate_val(u_ref, r0, CHUNK_FWD, where, w_ref, b_ref)
            act = gate * jax.nn.sigmoid(gate) * val
            o_ref[pl.ds(r0, CHUNK_FWD), :] = act[HALO:HALO + CHUNK_FWD].astype(BF16)

        _for_chunks(t, CHUNK_FWD, chunk)

    return pl.pallas_call(
        body, name=name, grid=(D_FF // FF_COLS,), in_specs=[u_spec, w_spec, b_spec],
        out_specs=pl.BlockSpec((t, FF_COLS), lambda j: (0, j)),
        out_shape=pltpu.HBM((t, D_FF), BF16), compiler_params=_params(("arbitrary",), 48),
    )(_in_hbm(u0), conv_w, conv_b)


def _convgate_bwd(u0, conv_w, conv_b, d_act, name):
    t = u0.shape[1]
    u_spec, w_spec, b_spec = _ff_specs(t)

    def body(u_ref, w_ref, b_ref, da_ref, du_ref, dw_ref, db_ref, sums_ref):
        sums_ref[...] = jnp.zeros_like(sums_ref)
        inner = slice(HALO, HALO + CHUNK_BWD)

        def fold(x):
            return jnp.sum(x.reshape(CHUNK_BWD // SUBLANES, SUBLANES, x.shape[1]), axis=0)

        def chunk(r0, where):
            gate, val, ext, before, after = _gate_val(u_ref, r0, CHUNK_BWD, where, w_ref, b_ref)
            sig = jax.nn.sigmoid(gate)
            da = _ext_rows(da_ref, r0, CHUNK_BWD, where)
            d_half = (da * val * (sig * (1.0 + gate * (1.0 - sig))), da * (gate * sig))
            for h in range(2):
                du = d_half[h]
                for k, term in enumerate((du, du * before[h], du * ext[h], du * after[h])):
                    sums_ref[h, k] += fold(term[inner])
                du0 = (_roll_rows(du, -1) * w_ref[h, 0:1, :] + du * w_ref[h, 1:2, :]) + _roll_rows(du, 1) * w_ref[h, 2:3, :]
                du_ref[h, pl.ds(r0, CHUNK_BWD), :] = du0[inner].astype(BF16)

        _for_chunks(t, CHUNK_BWD, chunk)
        for h in range(2):
            db_ref[h] = jnp.sum(sums_ref[h, 0], axis=0, keepdims=True)
            for k in range(3):
                dw_ref[h, k:k + 1, :] = jnp.sum(sums_ref[h, k + 1], axis=0, keepdims=True)

    return pl.pallas_call(
        body, name=name, grid=(D_FF // FF_COLS,),
        in_specs=[u_spec, w_spec, b_spec, pl.BlockSpec((t, FF_COLS), lambda j: (0, j))],
        out_specs=[u_spec, w_spec, b_spec],
        out_shape=[pltpu.HBM((2, t, D_FF), BF16), jax.ShapeDtypeStruct((2, 3, D_FF), F32),
                   jax.ShapeDtypeStruct((2, 1, D_FF), F32)],
        scratch_shapes=[pltpu.VMEM((2, 4, SUBLANES, FF_COLS), F32)],
        compiler_params=_params(("arbitrary",), 56),
    )(_in_hbm(u0), conv_w, conv_b, _in_hbm(d_act))


class _Group:
    def __init__(self, heads, blks, kv_rows, n_win, gqa, bias_per_head):
        self.heads = heads
        self.pairs = heads // 2
        self.q_blk, self.k_blk, self.v_blk = blks
        self.kv_rows = kv_rows
        self.n_win = n_win
        self.full = kv_rows == SEQ
        self.gqa = gqa
        self.bias_per_head = bias_per_head
        self.width = heads * HEAD_DIM
        self.keys = kv_rows * n_win


GROUP_A = _Group(N_HEADS_A, A_BLK, SEQ, 1, False, False)
GROUP_B = _Group(N_HEADS_B, B_BLK, WINDOW_B, 4, True, False)
GROUP_C = _Group(N_HEADS_C, C_BLK, QB, 3, False, True)


def _win_start(grp, i):
    return jnp.clip(i * (QB // grp.kv_rows) - 1, 0, SEQ // grp.kv_rows - grp.n_win)


def _win_variant(i):
    return jnp.minimum(i, 1) + (i == NQB - 1).astype(jnp.int32)


def _attn_in_specs(grp, t):
    q_spec = pl.BlockSpec((QB, LANES), lambda p, i: (i, grp.q_blk + p))

    def col(blk):
        return (lambda p: blk) if grp.gqa else (lambda p: blk + p)

    def kv_specs(blk):
        c = col(blk)
        if grp.full:
            return [pl.BlockSpec((t, LANES), lambda p, i: (0, c(p)))]
        return [pl.BlockSpec((grp.kv_rows, LANES),
                             functools.partial(lambda p, i, w: (_win_start(grp, i) + w, c(p)), w=w))
                for w in range(grp.n_win)]

    nwk = grp.keys
    if grp.bias_per_head:
        bias_spec = pl.BlockSpec((2, None, QB, nwk), lambda p, i: (p, _win_variant(i), 0, 0))
    elif grp.full:
        bias_spec = pl.BlockSpec((1, None, QB, nwk), lambda p, i: (0, i, 0, 0))
    else:
        bias_spec = pl.BlockSpec((1, None, QB, nwk), lambda p, i: (0, _win_variant(i), 0, 0))
    sink_spec = pl.BlockSpec((1, LANES), lambda p, i: (0, p))
    return q_spec, kv_specs(grp.k_blk), kv_specs(grp.v_blk), bias_spec, sink_spec


def _head_kv(grp, whole, e, p):
    lo, hi = whole[:, :HEAD_DIM], whole[:, HEAD_DIM:]
    if grp.gqa:
        return jnp.where(2 * p + e >= N_HEADS_B // N_KV_B, hi, lo)
    return hi if e else lo


def _softmax_parts(q, k, bias, sink):
    s = lax.dot_general(q, k, (((1,), (1,)), ((), ())), preferred_element_type=F32) + bias
    m = jnp.maximum(jnp.max(s, axis=-1, keepdims=True), sink)
    pe = jnp.exp(s - m)
    denom = jnp.sum(pe, axis=-1, keepdims=True) + jnp.exp(sink - m)
    return pe, m, 1.0 / denom


def _attn_fwd(grp, proj, bias, sink, name):
    t = proj.shape[0]
    q_spec, k_specs, v_specs, bias_spec, sink_spec = _attn_in_specs(grp, t)
    nkv = len(k_specs)

    def body(*refs):
        q_ref = refs[0]
        k_refs, v_refs = refs[1:1 + nkv], refs[1 + nkv:1 + 2 * nkv]
        bias_ref, sink_ref, o_ref = refs[1 + 2 * nkv:4 + 2 * nkv]
        p = pl.program_id(0)
        k_all = jnp.concatenate([r[...] for r in k_refs], axis=0)
        v_all = jnp.concatenate([r[...] for r in v_refs], axis=0)
        outs = []
        for e in range(2):
            q = q_ref[:, e * HEAD_DIM:(e + 1) * HEAD_DIM]
            k = _head_kv(grp, k_all, e, p)
            v = _head_kv(grp, v_all, e, p)
            snk = sink_ref[0:1, e * HEAD_DIM:e * HEAD_DIM + 1]
            pe, _, inv = _softmax_parts(q, k, bias_ref[e if grp.bias_per_head else 0], snk)
            outs.append(jnp.dot(pe.astype(BF16), v, preferred_element_type=F32) * inv)
        o_ref[...] = jnp.concatenate(outs, axis=1)

    return pl.pallas_call(
        body, name=name, grid=(grp.pairs, NQB),
        in_specs=[q_spec, *k_specs, *v_specs, bias_spec, sink_spec],
        out_specs=pl.BlockSpec((QB, LANES), lambda p, i: (i, p)),
        out_shape=pltpu.HBM((t, grp.width), F32),
        compiler_params=_params(("arbitrary", "arbitrary"), 48),
    )(*([_in_hbm(proj)] * (1 + 2 * nkv)), _in_hbm(bias), sink)


def _attn_bwd(grp, proj, bias, sink, out, d_out, name):
    t = proj.shape[0]
    q_spec, k_specs, v_specs, bias_spec, sink_spec = _attn_in_specs(grp, t)
    nkv = len(k_specs)
    n_off = 2 * NA_ROWS - 1
    rows_q = QB // GRID_W
    wide = grp.keys > 2 * QB
    o_spec = pl.BlockSpec((QB, LANES), lambda p, i: (i, p))
    acc_spec = pl.BlockSpec((t, LANES), lambda p, i: (0, p))
    out_specs = [o_spec, acc_spec, acc_spec, pl.BlockSpec((None, 8, LANES), lambda p, i: (p, 0, 0))]
    out_shape = [pltpu.HBM((t, grp.width), F32)] * 3 + [jax.ShapeDtypeStruct((grp.pairs, 8, LANES), F32)]
    if grp.bias_per_head:
        out_specs.append(pl.BlockSpec((2, n_off, GRID_W, GRID_W), lambda p, i: (p, 0, 0, 0)))
        out_shape.append(jax.ShapeDtypeStruct((grp.heads, n_off, GRID_W, GRID_W), F32))

    def body(*refs):
        q_ref = refs[0]
        k_refs, v_refs = refs[1:1 + nkv], refs[1 + nkv:1 + 2 * nkv]
        bias_ref, sink_ref, o_ref, do_ref = refs[1 + 2 * nkv:5 + 2 * nkv]
        dq_ref, dk_ref, dv_ref, dsink_ref = refs[5 + 2 * nkv:9 + 2 * nkv]
        dbias_ref = refs[9 + 2 * nkv] if grp.bias_per_head else None
        p, i = pl.program_id(0), pl.program_id(1)

        @pl.when(i == 0)
        def _():
            dk_ref[...] = jnp.zeros_like(dk_ref)
            dv_ref[...] = jnp.zeros_like(dv_ref)
            dsink_ref[...] = jnp.zeros_like(dsink_ref)
            if dbias_ref is not None:
                dbias_ref[...] = jnp.zeros_like(dbias_ref)

        k_all = jnp.concatenate([r[...] for r in k_refs], axis=0)
        v_all = jnp.concatenate([r[...] for r in v_refs], axis=0)
        start = 0 if grp.full else _win_start(grp, i)
        dqs, dks, dvs, dsinks = [], [], [], []
        for e in range(2):
            cols = slice(e * HEAD_DIM, (e + 1) * HEAD_DIM)
            q = q_ref[:, cols]
            k = _head_kv(grp, k_all, e, p)
            v = _head_kv(grp, v_all, e, p)
            snk = sink_ref[0:1, e * HEAD_DIM:e * HEAD_DIM + 1]
            pe, m, inv = _softmax_parts(q, k, bias_ref[e if grp.bias_per_head else 0], snk)
            prob = pe * inv
            do = do_ref[:, cols]
            do_b = do.astype(BF16)
            pe_b = prob.astype(BF16)
            delta = jnp.sum(do * o_ref[:, cols], axis=-1, keepdims=True)
            dp = lax.dot_general(do_b, v, (((1,), (1,)), ((), ())), preferred_element_type=F32)
            ds = prob * (dp - delta)
            ds_b = ds.astype(BF16)
            dqs.append(jnp.dot(ds_b, k, preferred_element_type=F32))
            if wide:
                dks.append(lax.dot_general(q, ds_b, (((0,), (0,)), ((), ())), preferred_element_type=F32))
                dvs.append(lax.dot_general(do_b, pe_b, (((0,), (0,)), ((), ())), preferred_element_type=F32))
            else:
                dks.append(lax.dot_general(ds_b, q, (((0,), (0,)), ((), ())), preferred_element_type=F32))
                dvs.append(lax.dot_general(pe_b, do_b, (((0,), (0,)), ((), ())), preferred_element_type=F32))
            dsinks.append(-jnp.sum(jnp.exp(snk - m) * inv * delta, axis=0, keepdims=True))
            if dbias_ref is not None:
                shift = (i * QB - start * grp.kv_rows) // GRID_W
                for rq in range(rows_q):
                    for rk in range(grp.keys // GRID_W):
                        off = jnp.clip(rk - rq + (NA_ROWS - 1) - shift, 0, n_off - 1)
                        dbias_ref[e, off] += ds[rq * GRID_W:(rq + 1) * GRID_W, rk * GRID_W:(rk + 1) * GRID_W]
        dq_ref[...] = jnp.concatenate(dqs, axis=1)
        rows = pl.ds(0, t) if grp.full else pl.ds(pl.multiple_of(start * grp.kv_rows, grp.kv_rows), grp.keys)
        if wide:
            dk_ref[rows, :] += jnp.concatenate(dks, axis=0).T
            dv_ref[rows, :] += jnp.concatenate(dvs, axis=0).T
        else:
            dk_ref[rows, :] += jnp.concatenate(dks, axis=1)
            dv_ref[rows, :] += jnp.concatenate(dvs, axis=1)
        lane = lax.broadcasted_iota(jnp.int32, (8, LANES), 1)
        dsink_ref[...] += jnp.where(lane < HEAD_DIM, dsinks[0], dsinks[1])

    return pl.pallas_call(
        body, name=name, grid=(grp.pairs, NQB),
        in_specs=[q_spec, *k_specs, *v_specs, bias_spec, sink_spec, o_spec, o_spec],
        out_specs=out_specs, out_shape=out_shape,
        compiler_params=_params(("arbitrary", "arbitrary"), 56),
    )(*([_in_hbm(proj)] * (1 + 2 * nkv)), _in_hbm(bias), sink, _in_hbm(out), _in_hbm(d_out))


DILATED_CONFIGS = ((128, 1), (512, 4), (2048, 16))


def _bias_a():
    d = jnp.arange(SEQ)[None, :] - jnp.arange(SEQ)[:, None]
    mult = jnp.zeros((SEQ, SEQ), F32)
    for window, r in DILATED_CONFIGS:
        reach = (window // (2 * r)) * r
        mult = mult + ((d % r == 0) & (jnp.abs(d) <= reach)).astype(F32)
    return jnp.where(mult > 0, jnp.log(jnp.maximum(mult, 1.0)), NEG_INF).reshape(1, NQB, QB, SEQ)


def _bias_b():
    row = jnp.arange(QB)[None, :, None]
    col = jnp.arange(GROUP_B.keys)[None, None, :]
    var = jnp.arange(3)[:, None, None]
    d = col - (GROUP_B.kv_rows * var + row)
    return jnp.where(jnp.abs(d) <= WINDOW_B, 0.0, NEG_INF).astype(F32)[None]


def _offset_onehot():
    c = jnp.arange(GRID_W)[:, None, None]
    c2 = jnp.arange(GRID_W)[None, :, None]
    b = jnp.arange(LANES)[None, None, :]
    return (c2 - c + NA_COLS - 1 == b).astype(BF16).reshape(GRID_W * GRID_W, LANES)


def _split_dot(x, g):
    hi = x.astype(BF16)
    rest = x - hi.astype(F32)
    mid = rest.astype(BF16)
    lo = (rest - mid.astype(F32)).astype(BF16)
    return (jnp.dot(hi, g, preferred_element_type=F32) + jnp.dot(mid, g, preferred_element_type=F32)
            + jnp.dot(lo, g, preferred_element_type=F32))


def _table_mm(x, g, name):
    def body(x_ref, g_ref, o_ref):
        o_ref[...] = _split_dot(x_ref[...], g_ref[...])

    return pl.pallas_call(
        body, name=name, out_shape=jax.ShapeDtypeStruct((x.shape[0], g.shape[1]), F32),
        in_specs=[pl.BlockSpec(memory_space=pltpu.VMEM)] * 2, out_specs=pl.BlockSpec(memory_space=pltpu.VMEM),
        compiler_params=pltpu.CompilerParams(vmem_limit_bytes=32 * MIB),
    )(x, g)


N_OFF = 2 * NA_ROWS - 1
TABLE_ROWS = 152


def _bias_c(rpb):
    table = jnp.zeros((TABLE_ROWS, LANES), F32).at[:N_HEADS_C * N_OFF, :2 * NA_COLS - 1].set(
        rpb.reshape(N_HEADS_C * N_OFF, 2 * NA_COLS - 1))
    tiles = _table_mm(table, _offset_onehot().T, "rpb_tiles")[:N_HEADS_C * N_OFF]
    tiles = tiles.reshape(N_HEADS_C, N_OFF, GRID_W, GRID_W)
    c = jnp.arange(GRID_W)
    col_start = jnp.clip(c - NA_COLS // 2, 0, GRID_W - NA_COLS)
    col_ok = (c[None, :] >= col_start[:, None]) & (c[None, :] < col_start[:, None] + NA_COLS)
    tiles = jnp.where(col_ok, tiles, NEG_INF)
    rows_q = QB // GRID_W
    rows_k = GROUP_C.keys // GRID_W

    def body(t_ref, o_ref):
        for var in range(3):
            for rq in range(rows_q):
                r_l = rows_q * var + rq
                first = min(max(r_l - NA_ROWS // 2, 0), rows_k - NA_ROWS)
                for rk in range(rows_k):
                    if first <= rk < first + NA_ROWS:
                        tile = t_ref[rk - r_l + NA_ROWS - 1]
                    else:
                        tile = jnp.full((GRID_W, GRID_W), NEG_INF, F32)
                    o_ref[var, rq * GRID_W:(rq + 1) * GRID_W, rk * GRID_W:(rk + 1) * GRID_W] = tile

    return pl.pallas_call(
        body, name="bias_c", grid=(N_HEADS_C,),
        in_specs=[pl.BlockSpec((None, N_OFF, GRID_W, GRID_W), lambda h: (h, 0, 0, 0))],
        out_specs=pl.BlockSpec((None, 3, QB, GROUP_C.keys), lambda h: (h, 0, 0, 0)),
        out_shape=jax.ShapeDtypeStruct((N_HEADS_C, 3, QB, GROUP_C.keys), F32),
        compiler_params=_params(("arbitrary",), 32),
    )(tiles)


def _rpb_grad(d_tiles):
    flat = jnp.zeros((TABLE_ROWS, GRID_W * GRID_W), F32).at[:N_HEADS_C * N_OFF].set(
        d_tiles.reshape(N_HEADS_C * N_OFF, GRID_W * GRID_W))
    out = _table_mm(flat, _offset_onehot(), "rpb_grad")
    return out[:N_HEADS_C * N_OFF, :2 * NA_COLS - 1].reshape(N_HEADS_C, N_OFF, 2 * NA_COLS - 1)


def _sink_lanes(sink):
    return jnp.repeat(sink.astype(F32), HEAD_DIM)[None, :]


def _attention_fwd(proj_r, sink_b, bias_a, bias_b, bias_c):
    no_sink_a = jnp.full((1, WIDTH_A), NEG_INF, F32)
    no_sink_c = jnp.full((1, WIDTH_C), NEG_INF, F32)
    oa = _attn_fwd(GROUP_A, proj_r, bias_a, no_sink_a, "attn_a_fwd")
    ob = _attn_fwd(GROUP_B, proj_r, bias_b, _sink_lanes(sink_b), "attn_b_fwd")
    oc = _attn_fwd(GROUP_C, proj_r, bias_c, no_sink_c, "attn_c_fwd")
    return oa, ob, oc


def _attention_bwd(proj_r, sink_b, bias_a, bias_b, bias_c, outs, d_outs, cos, sin):
    no_sink_a = jnp.full((1, WIDTH_A), NEG_INF, F32)
    no_sink_c = jnp.full((1, WIDTH_C), NEG_INF, F32)
    dqa, dka, dva, _ = _attn_bwd(GROUP_A, proj_r, bias_a, no_sink_a, outs[0], d_outs[0], "attn_a_bwd")
    dqb, dkb, dvb, dsink = _attn_bwd(GROUP_B, proj_r, bias_b, _sink_lanes(sink_b), outs[1], d_outs[1], "attn_b_bwd")
    dqc, dkc, dvc, _, d_tiles = _attn_bwd(GROUP_C, proj_r, bias_c, no_sink_c, outs[2], d_outs[2], "attn_c_bwd")
    d_proj = _rope_bwd((dqa, dka, dva, dqb, dkb, dvb, dqc, dkc, dvc), cos, sin, "rope_bwd")
    d_sink = dsink[:, 0, :].reshape(GROUP_B.pairs, 2, HEAD_DIM)[:, :, 0].reshape(N_HEADS_B)
    return d_proj, d_sink, _rpb_grad(d_tiles)


def _adamw(w, g, m, v, name):
    r, c = w.shape
    rows = r
    for cand in (512, 256, 128, 64, 32, 16, 8):
        if r % cand == 0 and cand * c * 4 <= MIB:
            rows = cand
            break
    spec = pl.BlockSpec((rows, c), lambda i: (i, 0))

    def body(w_ref, g_ref, m_ref, v_ref, d_ref, mo_ref, vo_ref):
        d_ref[...], mo_ref[...], vo_ref[...] = _adamw_step(w_ref[...], g_ref[...], m_ref[...], v_ref[...])

    return pl.pallas_call(
        body, name=name, grid=(r // rows,), in_specs=[spec] * 4, out_specs=[spec] * 3,
        out_shape=[jax.ShapeDtypeStruct((r, c), F32)] * 3, compiler_params=_params(("arbitrary",), 32),
    )(w, g, m, v)


def _adamw_step(w, grad, m, v):
    m_new = ADAM_B1 * m + (1.0 - ADAM_B1) * grad
    v_new = ADAM_B2 * v + (1.0 - ADAM_B2) * jnp.square(grad)
    m_hat = m_new / (1.0 - ADAM_B1 ** ADAM_STEP)
    v_hat = v_new / (1.0 - ADAM_B2 ** ADAM_STEP)
    return -ADAM_LR * (m_hat / (jnp.sqrt(v_hat) + ADAM_EPS) + ADAM_WD * w), m_new, v_new


def _adamw_layer(w, g, m, v, layer, prev, name):
    _, r, c = w.shape
    rows = next(cand for cand in (512, 256, 128, 64, 32, 16, 8) if r % cand == 0 and cand * c * 4 <= 2 * MIB)
    spec = pl.BlockSpec((None, rows, c), lambda i: (layer, i, 0))
    g_spec = pl.BlockSpec((rows, c), lambda i: (i, 0))
    n_prev = 0 if prev is None else 4

    def body(w_ref, g_ref, m_ref, v_ref, *rest):
        go_ref, d_ref, mo_ref, vo_ref = rest[n_prev:]
        grad = g_ref[...]
        go_ref[...] = grad
        d_ref[...], mo_ref[...], vo_ref[...] = _adamw_step(w_ref[...], grad, m_ref[...], v_ref[...])

    return pl.pallas_call(
        body, name=name, grid=(r // rows,), in_specs=[spec, g_spec, spec, spec] + [ANY_SPEC] * n_prev,
        out_specs=[spec] * 4,
        out_shape=[jax.ShapeDtypeStruct(w.shape, F32)] * 4,
        input_output_aliases={4 + i: i for i in range(n_prev)}, compiler_params=_params(("arbitrary",), 48),
    )(w, g, m, v, *(prev or ()))


def _layer_fwd(x0, p, weight, tabs):
    h1 = _rmsnorm_fwd(x0, p["ln_attn"], "ln_attn_fwd")
    proj = _mm_nn(h1, weight("w_in", h1), cols=True, tn=256, tk=D_MODEL, out_dtype=F32, name="mm_in")
    proj_r = _rope_fwd(proj, tabs["cos"], tabs["sin"], "rope_fwd")
    outs = _attention_fwd(proj_r, p["sink_b"], tabs["bias_a"], tabs["bias_b"], p["bias_c"])
    mixed = _mix_fwd(*outs, p["mix_gain"], "mix_fwd")
    x1 = _mm_nn(mixed, weight("w_out", mixed), cols=False, tn=256, tk=D_MODEL, out_dtype=F32, name="mm_out",
                residual=x0)
    h2 = _rmsnorm_fwd(x1, p["ln_ffn"], "ln_ffn_fwd")
    u0 = _mm_nn(h2, weight("w_up", h2), cols=True, tn=256, tk=D_MODEL, out_dtype=F32, name="mm_up", out_split=2)
    act = _convgate_fwd(u0, p["conv_w"], p["conv_b"], "convgate_fwd")
    x2 = _mm_nn(act, weight("w_down", act), cols=False, tn=512, tk=D_FF // 2, out_dtype=F32, name="mm_down",
                residual=x1)
    return x2, (x0, h1, proj_r, outs, mixed, x1, h2, u0, act)


def _layer_bwd(dx2, dx2_b, saved, p, big, tabs, begin, finish, pending):
    x0, h1, proj_r, outs, mixed, x1, h2, u0, act = saved
    d_act = _mm_nt(dx2_b, big["w_down"], cols=False, to=512, tr=D_MODEL, out_dtype=F32, name="nt_down",
                   after=[pending[1]] if pending else [])
    g_down = _mm_tn(act, dx2_b, tk=D_FF // N_SHARDS, tn=D_MODEL, shards=-N_SHARDS, name="tn_down")
    du0, d_conv_w, d_conv_b = _convgate_bwd(u0, p["conv_w"], p["conv_b"], d_act, "convgate_bwd")
    token = [finish(pending[0], [du0])] if pending else []
    dh2 = _mm_nt(du0, big["w_up"], cols=True, to=1024, tr=D_FF // 4, out_dtype=F32, name="nt_up", after=token)
    g_up = _mm_tn(h2, du0, tk=512, tn=D_FF // 4, shards=N_SHARDS, name="tn_up")
    first, token = begin({"w_down": g_down, "w_up": g_up})
    dx1, dx1_b, d_ln_ffn = _rmsnorm_bwd(x1, p["ln_ffn"], dh2, dx2, "ln_ffn_bwd", after=[token])
    d_mixed = _mm_nt(dx1_b, big["w_out"], cols=False, to=512, tr=D_MODEL, out_dtype=F32, name="nt_out")
    g_out = _mm_tn(mixed, dx1_b, tk=D_MODEL // N_SHARDS, tn=D_MODEL, shards=-N_SHARDS, name="tn_out")
    token = finish(first, [g_out])
    *d_outs, d_mix_gain = _mix_bwd(*outs, p["mix_gain"], d_mixed, "mix_bwd", after=[token])
    d_proj, d_sink, d_rpb = _attention_bwd(proj_r, p["sink_b"], tabs["bias_a"], tabs["bias_b"], p["bias_c"], outs,
                                           d_outs, tabs["cos"], tabs["sin"])
    dh1 = _mm_nt(d_proj, big["w_in"], cols=True, to=1024, tr=IN_COLS // N_SHARDS, out_dtype=F32, name="nt_in")
    g_in = _mm_tn(h1, d_proj, tk=512, tn=IN_COLS // N_SHARDS, shards=N_SHARDS, name="tn_in")
    dx0, dx0_b, d_ln_attn = _rmsnorm_bwd(x0, p["ln_attn"], dh1, dx1, "ln_attn_bwd")
    small = {"ln_attn": d_ln_attn, "sink_b": d_sink, "rpb_c": d_rpb, "mix_gain": d_mix_gain, "ln_ffn": d_ln_ffn,
             "conv_w": d_conv_w, "conv_b": d_conv_b}
    return dx0, dx0_b, small, begin({"w_out": g_out, "w_in": g_in})


HBM_SPEC = pl.BlockSpec(memory_space=pl.ANY)


def _place():
    x, y, c = lax.axis_index("x"), lax.axis_index("y"), lax.axis_index("c")
    chips = ((1 - x, y), (x, 1 - y), (1 - x, 1 - y))
    return x, y, c, chips


def _shard_index(px, py):
    return 2 * px + py


def _remote(src, dst, send_sem, recv_sem, to):
    return pltpu.make_async_remote_copy(src_ref=src, dst_ref=dst, send_sem=send_sem, recv_sem=recv_sem,
                                        device_id=to, device_id_type=MESH)


def _own_slot(w, layer, shard, name):
    _, r, c_dim = w.shape
    rows = r
    for cand in (512, 256, 128):
        if r % cand == 0 and cand * c_dim * 4 <= 2 * MIB:
            rows = cand
            break

    def body(s_ref, w_ref, o_ref):
        o_ref[...] = w_ref[...].astype(BF16)

    return pl.pallas_call(
        body, name=name,
        grid_spec=pltpu.PrefetchScalarGridSpec(
            num_scalar_prefetch=1, grid=(r // rows,),
            in_specs=[pl.BlockSpec((None, rows, c_dim), lambda i, s: (layer, i, 0))],
            out_specs=pl.BlockSpec((None, rows, c_dim), lambda i, s: (s[0], i, 0))),
        out_shape=jax.ShapeDtypeStruct((N_SHARDS, r, c_dim), BF16),
        compiler_params=_params(("arbitrary",), 32),
    )(shard.astype(jnp.int32).reshape(1), w)


HBM_ONLY = pl.BlockSpec(memory_space=pltpu.HBM)
SEM_SPEC = pl.BlockSpec(memory_space=pltpu.SEMAPHORE)
DATAFLOW = pltpu.SideEffectType.DATAFLOW_SIDE_EFFECTING


def _in_hbm(a):
    return pltpu.with_memory_space_constraint(a, pltpu.HBM)


N_DEV = 8


def _all_gather_small(vec, name, after=()):
    n_after = len(after)

    def body(v_ref, *rest):
        o_ref, send, recv, local_sem = rest[n_after:]
        x, y, c, _ = _place()
        me = 4 * x + 2 * y + c
        local = pltpu.make_async_copy(v_ref, o_ref.at[me], local_sem)
        local.start()
        flips = [(fx, fy, fc) for fx in (0, 1) for fy in (0, 1) for fc in (0, 1)][1:]
        peers = [((1 - x) if fx else x, (1 - y) if fy else y, (1 - c) if fc else c) for fx, fy, fc in flips]
        cps = [_remote(v_ref, o_ref.at[me], send.at[k], recv.at[k], peer) for k, peer in enumerate(peers)]
        for cp in cps:
            cp.start()
        for k, (px, py, pc) in enumerate(peers):
            slot = o_ref.at[4 * px + 2 * py + pc]
            _remote(slot, slot, send.at[k], recv.at[k], (px, py, pc)).wait_recv()
        for cp in cps:
            cp.wait_send()
        local.wait()

    return pl.pallas_call(
        body, name=name, in_specs=[HBM_SPEC] * (1 + n_after), out_specs=HBM_SPEC,
        out_shape=jax.ShapeDtypeStruct((N_DEV,) + vec.shape, vec.dtype),
        scratch_shapes=[pltpu.SemaphoreType.DMA((N_DEV - 1,))] * 2 + [pltpu.SemaphoreType.DMA(())],
    )(vec, *after)


def _peers(x, y, c):
    flips = [(fx, fy, fc) for fx in (0, 1) for fy in (0, 1) for fc in (0, 1)][1:]
    return [((1 - x) if fx else x, (1 - y) if fy else y, (1 - c) if fc else c) for fx, fy, fc in flips]


def _small_start(vec, after, name):
    n_after = len(after)

    def body(v_ref, slots_ref, *rest):
        send, recv = rest[n_after], rest[n_after + 1]
        token = rest[-1]
        x, y, c, _ = _place()
        me = 4 * x + 2 * y + c
        for k, peer in enumerate(_peers(x, y, c)):
            _remote(v_ref, slots_ref.at[me], send.at[k], recv.at[k], peer).start()
        token[...] = jnp.zeros_like(token)

    slots = jax.ShapeDtypeStruct((N_DEV,) + vec.shape, vec.dtype)
    res = pl.pallas_call(
        body, name=name,
        out_shape=(pltpu.SemaphoreType.DMA((N_DEV - 1,)), pltpu.SemaphoreType.DMA((N_DEV - 1,)),
                   pltpu.HBM(vec.shape, vec.dtype), pltpu.HBM(slots.shape, slots.dtype),
                   jax.ShapeDtypeStruct((8, LANES), F32)),
        in_specs=[HBM_ONLY, HBM_ONLY] + [ANY_SPEC] * n_after,
        out_specs=(SEM_SPEC, SEM_SPEC, HBM_ONLY, HBM_ONLY, pl.BlockSpec(memory_space=pltpu.VMEM)),
        input_output_aliases={0: 2, 1: 3},
        compiler_params=pltpu.CompilerParams(has_side_effects=DATAFLOW),
    )(_in_hbm(vec), _in_hbm(lax.empty(slots.shape, slots.dtype)), *after)
    return res


def _small_wait(send, recv, vec, slots, after, name):
    def body(v_ref, slots_ref, send_ref, recv_ref, *rest):
        x, y, c, _ = _place()
        for k, (px, py, pc) in enumerate(_peers(x, y, c)):
            cp = _remote(v_ref, slots_ref.at[4 * px + 2 * py + pc], send_ref.at[k], recv_ref.at[k], (px, py, pc))
            cp.wait_send()
            cp.wait_recv()

    return pl.pallas_call(
        body, name=name, out_shape=(pltpu.HBM(vec.shape, vec.dtype), pltpu.HBM(slots.shape, slots.dtype)),
        in_specs=[HBM_ONLY, HBM_ONLY, SEM_SPEC, SEM_SPEC] + [ANY_SPEC] * len(after), out_specs=[HBM_ONLY, HBM_ONLY],
        input_output_aliases={0: 0, 1: 1},
        compiler_params=pltpu.CompilerParams(has_side_effects=DATAFLOW),
    )(vec, slots, send, recv, *after)


def _small_sum(vec, slots, name):
    rows = vec.shape[0]
    blk = min(rows, 256)
    x, y, c = lax.axis_index("x"), lax.axis_index("y"), lax.axis_index("c")
    me = (4 * x + 2 * y + c).astype(jnp.int32).reshape(1)

    def slot_spec(k):
        return pl.BlockSpec((None, blk, LANES), lambda i, w: (jnp.where(w[0] == k, (k + 1) % N_DEV, k), i, 0))

    def body(w_ref, v_ref, *rest):
        o_ref = rest[-1]
        acc = None
        for k in range(N_DEV):
            term = jnp.where(w_ref[0] == k, v_ref[...], rest[k][...])
            acc = term if acc is None else acc + term
        o_ref[...] = acc

    return pl.pallas_call(
        body, name=name,
        grid_spec=pltpu.PrefetchScalarGridSpec(
            num_scalar_prefetch=1, grid=(rows // blk,),
            in_specs=[pl.BlockSpec((blk, LANES), lambda i, w: (i, 0))] + [slot_spec(k) for k in range(N_DEV)],
            out_specs=pl.BlockSpec((blk, LANES), lambda i, w: (i, 0))),
        out_shape=jax.ShapeDtypeStruct(vec.shape, F32), compiler_params=_params(("arbitrary",), 32),
    )(me, vec, *([slots] * N_DEV))


def _half(ref, slot, c):
    half = ref.shape[1] // 2
    return ref.at[slot, pl.ds(pl.multiple_of(c * half, 8), half)]


def _gather_start(bufs, after, name):
    n = len(bufs)
    n_after = len(after)

    def body(*refs):
        ins = refs[:n]
        send, recv = refs[n + n_after], refs[n + n_after + 1]
        token = refs[-1]
        x, y, c, chips = _place()
        me = _shard_index(x, y)
        for t in range(n):
            for j, (px, py) in enumerate(chips):
                mine = _half(ins[t], me, c)
                _remote(mine, mine, send.at[t * 3 + j], recv.at[t * 3 + j], (px, py, c)).start()
        token[...] = jnp.zeros_like(token)

    thru = [pltpu.HBM(b.shape, b.dtype) for b in bufs]
    res = pl.pallas_call(
        body, name=name,
        out_shape=(pltpu.SemaphoreType.DMA((n * 3,)), pltpu.SemaphoreType.DMA((n * 3,)), *thru,
                   jax.ShapeDtypeStruct((8, LANES), F32)),
        in_specs=[HBM_ONLY] * n + [ANY_SPEC] * n_after,
        out_specs=(SEM_SPEC, SEM_SPEC, *([HBM_ONLY] * n), pl.BlockSpec(memory_space=pltpu.VMEM)),
        input_output_aliases={i: 2 + i for i in range(n)},
        compiler_params=pltpu.CompilerParams(has_side_effects=DATAFLOW),
    )(*[_in_hbm(b) for b in bufs], *after)
    return res[0], res[1], list(res[2:2 + n]), res[-1]


def _gather_wait(send, recv, bufs, after, name):
    n = len(bufs)

    def body(*refs):
        ins = refs[:n]
        send_ref, recv_ref = refs[n], refs[n + 1]
        x, y, c, chips = _place()
        me = _shard_index(x, y)
        for t in range(n):
            for j, (px, py) in enumerate(chips):
                cp = _remote(_half(ins[t], me, c), _half(ins[t], _shard_index(px, py), c), send_ref.at[t * 3 + j],
                             recv_ref.at[t * 3 + j], (px, py, c))
                cp.wait_send()
                cp.wait_recv()

    res = pl.pallas_call(
        body, name=name, out_shape=tuple(pltpu.HBM(b.shape, b.dtype) for b in bufs),
        in_specs=[HBM_ONLY] * n + [SEM_SPEC, SEM_SPEC] + [ANY_SPEC] * len(after), out_specs=[HBM_ONLY] * n,
        input_output_aliases={i: i for i in range(n)},
        compiler_params=pltpu.CompilerParams(has_side_effects=DATAFLOW),
    )(*bufs, send, recv, *after)
    return list(res)


def _gather_forward(bufs, name):
    n = len(bufs)

    def body(*refs):
        outs = refs[n:2 * n]
        send, recv = refs[2 * n:]
        x, y, c, chips = _place()
        sibling = (x, y, 1 - c)
        cps = []
        for t in range(n):
            for j, (px, py) in enumerate(chips):
                got = _half(outs[t], _shard_index(px, py), c)
                cp = _remote(got, got, send.at[t * 3 + j], recv.at[t * 3 + j], sibling)
                cp.start()
                cps.append(cp)
        for t in range(n):
            for j, (px, py) in enumerate(chips):
                theirs = _half(outs[t], _shard_index(px, py), 1 - c)
                _remote(theirs, theirs, send.at[t * 3 + j], recv.at[t * 3 + j], sibling).wait_recv()
        for cp in cps:
            cp.wait_send()

    return pl.pallas_call(
        body, name=name, in_specs=[HBM_SPEC] * n, out_specs=[HBM_SPEC] * n,
        out_shape=[jax.ShapeDtypeStruct(b.shape, b.dtype) for b in bufs],
        input_output_aliases={t: t for t in range(n)},
        scratch_shapes=[pltpu.SemaphoreType.DMA((n * 3,))] * 2,
    )(*bufs)


def _gather_forward_start(bufs, carry, name):
    n = len(bufs)

    def body(*refs):
        ins = refs[:n]
        send, recv = refs[n + 1], refs[n + 2]
        x, y, c, chips = _place()
        for t in range(n):
            for j, (px, py) in enumerate(chips):
                got = _half(ins[t], _shard_index(px, py), c)
                _remote(got, got, send.at[t * 3 + j], recv.at[t * 3 + j], (x, y, 1 - c)).start()

    res = pl.pallas_call(
        body, name=name,
        out_shape=(pltpu.SemaphoreType.DMA((n * 3,)), pltpu.SemaphoreType.DMA((n * 3,)),
                   *[pltpu.HBM(b.shape, b.dtype) for b in bufs], pltpu.HBM(carry.shape, carry.dtype)),
        in_specs=[HBM_ONLY] * (n + 1),
        out_specs=(SEM_SPEC, SEM_SPEC, *([HBM_ONLY] * (n + 1))),
        input_output_aliases={i: 2 + i for i in range(n + 1)},
        compiler_params=pltpu.CompilerParams(has_side_effects=DATAFLOW),
    )(*[_in_hbm(b) for b in bufs], _in_hbm(carry))
    return res[0], res[1], list(res[2:2 + n]), res[-1]


def _gather_forward_wait(send, recv, bufs, after, name):
    n = len(bufs)

    def body(*refs):
        ins = refs[:n]
        send_ref, recv_ref = refs[n], refs[n + 1]
        x, y, c, chips = _place()
        for t in range(n):
            for j, (px, py) in enumerate(chips):
                s = _shard_index(px, py)
                cp = _remote(_half(ins[t], s, c), _half(ins[t], s, 1 - c), send_ref.at[t * 3 + j],
                             recv_ref.at[t * 3 + j], (x, y, 1 - c))
                cp.wait_send()
                cp.wait_recv()

    res = pl.pallas_call(
        body, name=name, out_shape=tuple(pltpu.HBM(b.shape, b.dtype) for b in bufs),
        in_specs=[HBM_ONLY] * n + [SEM_SPEC, SEM_SPEC] + [ANY_SPEC] * len(after), out_specs=[HBM_ONLY] * n,
        input_output_aliases={i: i for i in range(n)},
        compiler_params=pltpu.CompilerParams(has_side_effects=DATAFLOW),
    )(*bufs, send, recv, *after)
    return list(res)


def _sibling_rows(ref, c):
    half = ref.shape[1] // 2
    return ref.at[:, pl.ds(pl.multiple_of((1 - c) * half, 8), half)]


def _half_exchange_start(grads, name):
    n = len(grads)

    def body(*refs):
        ins, lands = refs[:n], refs[n:2 * n]
        send, recv = refs[2 * n], refs[2 * n + 1]
        token = refs[-1]
        x, y, c, _ = _place()
        for t in range(n):
            _remote(_sibling_rows(ins[t], c), lands[t], send.at[t], recv.at[t], (x, y, 1 - c)).start()
        token[...] = jnp.zeros_like(token)

    halves = [jax.ShapeDtypeStruct((g.shape[0], g.shape[1] // 2, g.shape[2]), g.dtype) for g in grads]
    res = pl.pallas_call(
        body, name=name,
        out_shape=(pltpu.SemaphoreType.DMA((n,)), pltpu.SemaphoreType.DMA((n,)),
                   *[pltpu.HBM(g.shape, g.dtype) for g in grads], *[pltpu.HBM(h.shape, h.dtype) for h in halves],
                   jax.ShapeDtypeStruct((8, LANES), F32)),
        in_specs=[HBM_ONLY] * (2 * n),
        out_specs=(SEM_SPEC, SEM_SPEC, *([HBM_ONLY] * (2 * n)), pl.BlockSpec(memory_space=pltpu.VMEM)),
        input_output_aliases={i: 2 + i for i in range(2 * n)},
        compiler_params=pltpu.CompilerParams(has_side_effects=DATAFLOW),
    )(*[_in_hbm(g) for g in grads], *[_in_hbm(lax.empty(h.shape, h.dtype)) for h in halves])
    return res[0], res[1], list(res[2:2 + n]), list(res[2 + n:2 + 2 * n]), res[-1]


def _half_exchange_wait(send, recv, grads, lands, after, name):
    n = len(grads)

    def body(*refs):
        ins, got = refs[:n], refs[n:2 * n]
        send_ref, recv_ref = refs[2 * n], refs[2 * n + 1]
        x, y, c, _ = _place()
        for t in range(n):
            cp = _remote(_sibling_rows(ins[t], c), got[t], send_ref.at[t], recv_ref.at[t], (x, y, 1 - c))
            cp.wait_send()
            cp.wait_recv()

    res = pl.pallas_call(
        body, name=name,
        out_shape=(*[pltpu.HBM(g.shape, g.dtype) for g in grads], *[pltpu.HBM(h.shape, h.dtype) for h in lands]),
        in_specs=[HBM_ONLY] * (2 * n) + [SEM_SPEC, SEM_SPEC] + [ANY_SPEC] * len(after),
        out_specs=[HBM_ONLY] * (2 * n),
        input_output_aliases={i: i for i in range(2 * n)},
        compiler_params=pltpu.CompilerParams(has_side_effects=DATAFLOW),
    )(*grads, *lands, send, recv, *after)
    return list(res[:n]), list(res[n:])


def _half_rows(half, c_dim):
    for cand in (512, 256, 128, 64):
        if half % cand == 0 and cand * c_dim * 2 <= 2 * MIB:
            return cand
    raise ValueError((half, c_dim))


def _core_index():
    return lax.axis_index("c").astype(jnp.int32).reshape(1)


def _half_sum(own, other, name):
    s, r, c_dim = own.shape
    rows = _half_rows(r // 2, c_dim)
    per = r // 2 // rows

    def body(c_ref, a_ref, b_ref, o_ref):
        o_ref[...] = (a_ref[...].astype(F32) + b_ref[...].astype(F32)).astype(BF16)

    return pl.pallas_call(
        body, name=name,
        grid_spec=pltpu.PrefetchScalarGridSpec(
            num_scalar_prefetch=1, grid=(s, per),
            in_specs=[pl.BlockSpec((None, rows, c_dim), lambda k, i, c: (k, c[0] * per + i, 0)),
                      pl.BlockSpec((None, rows, c_dim), lambda k, i, c: (k, i, 0))],
            out_specs=pl.BlockSpec((None, rows, c_dim), lambda k, i, c: (k, i, 0))),
        out_shape=pltpu.HBM((s, r // 2, c_dim), BF16), compiler_params=_params(("arbitrary", "arbitrary"), 32),
    )(_core_index(), own, other)


def _reduce_start(pairs, name):
    n = len(pairs)

    def body(*refs):
        ins, lands = refs[:n], refs[n:2 * n]
        send, recv = refs[2 * n], refs[2 * n + 1]
        token = refs[-1]
        x, y, c, chips = _place()
        me = _shard_index(x, y)
        for t in range(n):
            for j, (px, py) in enumerate(chips):
                _remote(ins[t].at[_shard_index(px, py)], lands[t].at[me], send.at[t * 3 + j], recv.at[t * 3 + j],
                        (px, py, c)).start()
        token[...] = jnp.zeros_like(token)

    thru = [pltpu.HBM(b.shape, b.dtype) for b in pairs]
    res = pl.pallas_call(
        body, name=name,
        out_shape=(pltpu.SemaphoreType.DMA((n * 3,)), pltpu.SemaphoreType.DMA((n * 3,)), *thru, *thru,
                   jax.ShapeDtypeStruct((8, LANES), F32)),
        in_specs=[HBM_ONLY] * (2 * n),
        out_specs=(SEM_SPEC, SEM_SPEC, *([HBM_ONLY] * (2 * n)), pl.BlockSpec(memory_space=pltpu.VMEM)),
        input_output_aliases={i: 2 + i for i in range(2 * n)},
        compiler_params=pltpu.CompilerParams(has_side_effects=DATAFLOW),
    )(*[_in_hbm(b) for b in pairs], *[_in_hbm(lax.empty(b.shape, b.dtype)) for b in pairs])
    return res[0], res[1], list(res[2:2 + n]), list(res[2 + n:2 + 2 * n]), res[-1]


def _reduce_wait(send, recv, pairs, lands, after, name):
    n = len(pairs)

    def body(*refs):
        ins, got = refs[:n], refs[n:2 * n]
        send_ref, recv_ref = refs[2 * n], refs[2 * n + 1]
        x, y, c, chips = _place()
        for t in range(n):
            for j, (px, py) in enumerate(chips):
                s = _shard_index(px, py)
                cp = _remote(ins[t].at[s], got[t].at[s], send_ref.at[t * 3 + j], recv_ref.at[t * 3 + j], (px, py, c))
                cp.wait_send()
                cp.wait_recv()

    thru = [pltpu.HBM(b.shape, b.dtype) for b in pairs]
    res = pl.pallas_call(
        body, name=name, out_shape=(*thru, *thru),
        in_specs=[HBM_ONLY] * (2 * n) + [SEM_SPEC, SEM_SPEC] + [ANY_SPEC] * len(after),
        out_specs=[HBM_ONLY] * (2 * n),
        input_output_aliases={i: i for i in range(2 * n)},
        compiler_params=pltpu.CompilerParams(has_side_effects=DATAFLOW),
    )(*pairs, *lands, send, recv, *after)
    return list(res[:n]), list(res[n:])


def _reduce_sum(pair, landed, name):
    s, half, c_dim = pair.shape
    rows = _half_rows(half, c_dim)
    per = half // rows
    shard = _shard_index(lax.axis_index("x"), lax.axis_index("y"))
    where = jnp.stack([shard, lax.axis_index("c")]).astype(jnp.int32)

    def landed_spec(k):
        return pl.BlockSpec((None, rows, c_dim), lambda i, w: (jnp.where(w[0] == k, (k + 1) % s, k), i, 0))

    def body(w_ref, own_ref, *rest):
        o_ref = rest[-1]
        acc = None
        for k in range(s):
            term = jnp.where(w_ref[0] == k, own_ref[...], rest[k][...]).astype(F32)
            acc = term if acc is None else acc + term
        o_ref[...] = acc

    return pl.pallas_call(
        body, name=name,
        grid_spec=pltpu.PrefetchScalarGridSpec(
            num_scalar_prefetch=1, grid=(per,),
            in_specs=[pl.BlockSpec((None, rows, c_dim), lambda i, w: (w[0], i, 0))] + [landed_spec(k) for k in range(s)],
            out_specs=pl.BlockSpec((rows, c_dim), lambda i, w: (w[1] * per + i, 0))),
        out_shape=pltpu.HBM((2 * half, c_dim), F32), compiler_params=_params(("arbitrary",), 40),
    )(where, pair, *([landed] * s))


def _my_rows(ref, c):
    half = ref.shape[0] // 2
    return ref.at[pl.ds(pl.multiple_of(c * half, 8), half)]


def _half_gather_start(bufs, name):
    n = len(bufs)

    def body(*refs):
        ins = refs[:n]
        send, recv = refs[n], refs[n + 1]
        token = refs[-1]
        x, y, c, _ = _place()
        for t in range(n):
            mine = _my_rows(ins[t], c)
            _remote(mine, mine, send.at[t], recv.at[t], (x, y, 1 - c)).start()
        token[...] = jnp.zeros_like(token)

    res = pl.pallas_call(
        body, name=name,
        out_shape=(pltpu.SemaphoreType.DMA((n,)), pltpu.SemaphoreType.DMA((n,)),
                   *[pltpu.HBM(b.shape, b.dtype) for b in bufs], jax.ShapeDtypeStruct((8, LANES), F32)),
        in_specs=[HBM_ONLY] * n,
        out_specs=(SEM_SPEC, SEM_SPEC, *([HBM_ONLY] * n), pl.BlockSpec(memory_space=pltpu.VMEM)),
        input_output_aliases={i: 2 + i for i in range(n)},
        compiler_params=pltpu.CompilerParams(has_side_effects=DATAFLOW),
    )(*[_in_hbm(b) for b in bufs])
    return res[0], res[1], list(res[2:2 + n]), res[-1]


def _half_gather_wait(send, recv, bufs, after, name):
    n = len(bufs)

    def body(*refs):
        ins = refs[:n]
        send_ref, recv_ref = refs[n], refs[n + 1]
        x, y, c, _ = _place()
        for t in range(n):
            cp = _remote(_my_rows(ins[t], c), _my_rows(ins[t], 1 - c), send_ref.at[t], recv_ref.at[t], (x, y, 1 - c))
            cp.wait_send()
            cp.wait_recv()

    res = pl.pallas_call(
        body, name=name, out_shape=tuple(pltpu.HBM(b.shape, b.dtype) for b in bufs),
        in_specs=[HBM_ONLY] * n + [SEM_SPEC, SEM_SPEC] + [ANY_SPEC] * len(after), out_specs=[HBM_ONLY] * n,
        input_output_aliases={i: i for i in range(n)},
        compiler_params=pltpu.CompilerParams(has_side_effects=DATAFLOW),
    )(*bufs, send, recv, *after)
    return list(res)


WEIGHT_NAMES = ("ln_attn", "w_in", "sink_b", "rpb_c", "mix_gain", "w_out", "ln_ffn", "w_up", "conv_w", "conv_b",
                "w_down", "ln_final")
BIG_NAMES = ("w_in", "w_out", "w_up", "w_down")
REPLICATED_NAMES = ("ln_attn", "sink_b", "rpb_c", "mix_gain", "ln_ffn", "conv_b", "ln_final")
PACK_TILE = 8 * LANES


def _pack(arrays, row_multiple):
    pieces = []
    for a in arrays:
        flat = a.reshape(-1)
        pieces.append(jnp.pad(flat, (0, (-flat.shape[0]) % PACK_TILE)))
    flat = jnp.concatenate(pieces)
    flat = jnp.pad(flat, (0, (-flat.shape[0]) % (row_multiple * LANES)))
    return flat.reshape(-1, LANES)


def _unpack(packed, shapes):
    flat = packed.reshape(-1)
    out, off = [], 0
    for shape in shapes:
        size = math.prod(shape)
        out.append(flat[off:off + size].reshape(shape))
        off += size + (-size) % PACK_TILE
    return out


def kernel(x, ln_attn, w_in, sink_b, rpb_c, mix_gain, w_out, ln_ffn, w_up, conv_w, conv_b, w_down, ln_final, loss_target, m_ln_attn, m_w_in, m_sink_b, m_rpb_c, m_mix_gain, m_w_out, m_ln_ffn, m_w_up, m_conv_w, m_conv_b, m_w_down, m_ln_final, v_ln_attn, v_w_in, v_sink_b, v_rpb_c, v_mix_gain, v_w_out, v_ln_ffn, v_w_up, v_conv_w, v_conv_b, v_w_down, v_ln_final):
    w = dict(ln_attn=ln_attn, w_in=w_in, sink_b=sink_b, rpb_c=rpb_c, mix_gain=mix_gain, w_out=w_out, ln_ffn=ln_ffn,
             w_up=w_up, conv_w=conv_w, conv_b=conv_b, w_down=w_down, ln_final=ln_final)
    m = dict(ln_attn=m_ln_attn, w_in=m_w_in, sink_b=m_sink_b, rpb_c=m_rpb_c, mix_gain=m_mix_gain, w_out=m_w_out,
             ln_ffn=m_ln_ffn, w_up=m_w_up, conv_w=m_conv_w, conv_b=m_conv_b, w_down=m_w_down, ln_final=m_ln_final)
    v = dict(ln_attn=v_ln_attn, w_in=v_w_in, sink_b=v_sink_b, rpb_c=v_rpb_c, mix_gain=v_mix_gain, w_out=v_w_out,
             ln_ffn=v_ln_ffn, w_up=v_w_up, conv_w=v_conv_w, conv_b=v_conv_b, w_down=v_w_down, ln_final=v_ln_final)
    shard = _shard_index(lax.axis_index("x"), lax.axis_index("y"))
    up_cols = w_up.shape[2]

    conv_slots = _all_gather_small(_pack([conv_w], 8), "gather_conv_w")
    conv_all = conv_slots[0::2].reshape(N_SHARDS, -1)[:, :conv_w.size].reshape((N_SHARDS,) + conv_w.shape)

    arrivals = []
    group_of = {}
    tokens = []
    rest = ("w_out", "w_up", "w_down")
    for l, names in ((0, ("w_in",)), (0, rest), (1, ("w_in",)), (1, rest)):
        bufs = [_own_slot(w[k], l, shard, "own_" + k) for k in names]
        send, recv, bufs, token = _gather_start(bufs, tokens[-1:] or [conv_slots], "gather_start_%d" % len(arrivals))
        tokens.append(token)
        for k in names:
            group_of[l, k] = len(arrivals)
        arrivals.append({"names": names, "send": send, "recv": recv, "bufs": bufs, "done": None})

    def gathered(l, name, after):
        idx = group_of[l, name]
        group = arrivals[idx]

        def whole(k, buf):
            return buf.reshape(1, -1, buf.shape[2]) if k in ("w_out", "w_down") else buf

        if group["done"] is None:
            follow = list(after) + tokens[-1:]
            if idx == 0:
                follow += [tabs[k] for k in ("cos", "sin", "bias_a", "bias_b")]
                follow += [p[k] for p in layers for k in ("bias_c", "conv_w")]
            bufs = _gather_wait(group["send"], group["recv"], group["bufs"], follow, "gather_wait_%d" % idx)
            first = _gather_forward(bufs[:1], "gather_forward_%d" % idx)[0]
            if len(bufs) > 1:
                send, recv, rest, first = _gather_forward_start(bufs[1:], first, "gather_forward_start_%d" % idx)
                group["rest"] = (send, recv, rest)
            group["done"] = {group["names"][0]: whole(group["names"][0], first)}
        if name not in group["done"]:
            send, recv, rest = group["rest"]
            rest = _gather_forward_wait(send, recv, rest, list(after), "gather_forward_wait_%d" % idx)
            group["done"].update({k: whole(k, buf) for k, buf in zip(group["names"][1:], rest)})
        return group["done"][name]

    cos, sin = _rope_tables(SEQ)
    tabs = {"cos": cos, "sin": sin, "bias_a": _bias_a(), "bias_b": _bias_b()}
    layers = []
    for l in range(DEPTH):
        conv_w_l = conv_all[:, l].reshape(2, N_SHARDS // 2, 3, up_cols).transpose(0, 2, 1, 3).reshape(2, 3, D_FF)
        layers.append({"ln_attn": ln_attn[l][None], "sink_b": sink_b[l], "bias_c": _bias_c(rpb_c[l]),
                       "mix_gain": mix_gain[l][None], "ln_ffn": ln_ffn[l][None], "conv_w": conv_w_l,
                       "conv_b": conv_b[l].reshape(2, 1, D_FF)})

    act = x[0]
    saved = []
    for l in range(DEPTH):
        act, keep = _layer_fwd(act, layers[l], lambda name, after, l=l: gathered(l, name, [after]), tabs)
        saved.append(keep)
    loss_part, dx, dx_b, d_ln_final = _loss_head(act, ln_final[None], loss_target[0], "loss_head")
    loss = lax.psum(loss_part[0, 0], ("x", "y", "c"))

    reductions = []

    opened = [0]

    def begin(l, partial):
        idx = opened[0]
        opened[0] += 1
        names = tuple(partial)
        send_sem, recv_sem, mine, theirs, token = _half_exchange_start([partial[k] for k in names],
                                                                       "half_exchange_start_%d" % idx)
        return {"idx": idx, "layer": l, "names": names, "send": send_sem, "recv": recv_sem, "mine": mine,
                "theirs": theirs}, token

    def finish(handle, after):
        idx, names = handle["idx"], handle["names"]
        mine, theirs = _half_exchange_wait(handle["send"], handle["recv"], handle["mine"], handle["theirs"], after,
                                           "half_exchange_wait_%d" % idx)
        pairs = [_half_sum(a, b, "half_sum_" + k) for k, a, b in zip(names, mine, theirs)]
        send_sem, recv_sem, pairs, lands, token = _reduce_start(pairs, "reduce_start_%d" % idx)
        reductions.append({"layer": handle["layer"], "names": names, "send": send_sem, "recv": recv_sem,
                           "pairs": pairs, "lands": lands})
        return token

    small = [None] * DEPTH
    pending = None
    for l in reversed(range(DEPTH)):
        big = {k: gathered(l, k, []) for k in BIG_NAMES}
        dx, dx_b, small[l], pending = _layer_bwd(dx, dx_b, saved[l], layers[l], big, tabs,
                                                 functools.partial(begin, l), finish, pending)
    after = [finish(pending[0], [pending[1]])]

    stacked = {k: jnp.stack([small[l][k] for l in range(DEPTH)]) for k in small[0]}
    part = {"ln_attn": stacked["ln_attn"][:, 0], "sink_b": stacked["sink_b"], "rpb_c": stacked["rpb_c"],
            "mix_gain": stacked["mix_gain"][:, 0], "ln_ffn": stacked["ln_ffn"][:, 0],
            "conv_b": stacked["conv_b"].reshape(DEPTH, 2 * D_FF), "ln_final": d_ln_final[0],
            "conv_w": stacked["conv_w"].transpose(0, 2, 1, 3).reshape(DEPTH, 3, 2 * D_FF)}
    small_names = REPLICATED_NAMES + ("conv_w",)
    small_send, small_recv, small_vec, small_slots, token = _small_start(
        _pack([part[k] for k in small_names], 256), after, "small_grads_start")
    after = [token]

    grads, delta, new_m, new_v = {}, {}, {}, {}
    updated = dict.fromkeys(BIG_NAMES)

    def arrive(idx, after):
        group = reductions[idx]
        pairs, lands = _reduce_wait(group["send"], group["recv"], group["pairs"], group["lands"], after,
                                    "reduce_wait_%d" % idx)
        halves = [_reduce_sum(pair, landed, "reduce_sum_" + k) for k, pair, landed in zip(group["names"], pairs, lands)]
        send_sem, recv_sem, halves, token = _half_gather_start(halves, "half_gather_start_%d" % idx)
        return {"idx": idx, "send": send_sem, "recv": recv_sem, "bufs": halves, "names": group["names"],
                "layer": group["layer"]}, [token]

    def update(swap, after):
        whole = _half_gather_wait(swap["send"], swap["recv"], swap["bufs"], after,
                                  "half_gather_wait_%d" % swap["idx"])
        for k, g in zip(swap["names"], whole):
            updated[k] = _adamw_layer(w[k], g, m[k], v[k], swap["layer"], updated[k], "adamw_" + k)
        return [updated[k][0] for k in swap["names"]]

    swaps = []
    for idx in range(len(reductions) - 1):
        swap, after = arrive(idx, after)
        swaps.append(swap)
    for swap in swaps[:2]:
        after = update(swap, after)
    swap, after = arrive(len(reductions) - 1, after)
    for swap in swaps[2:] + [swap]:
        after = update(swap, after)
    for k in BIG_NAMES:
        grads[k], delta[k], new_m[k], new_v[k] = updated[k]

    small_vec, small_slots = _small_wait(small_send, small_recv, small_vec, small_slots, after, "small_grads_wait")
    total = _small_sum(small_vec, small_slots, "small_grads_sum")
    for k, g in zip(small_names, _unpack(total, [part[k].shape for k in small_names])):
        grads[k] = g
    grads["conv_w"] = lax.dynamic_slice_in_dim(grads["conv_w"], shard * up_cols, up_cols, axis=2)

    flat = (DEPTH * 3, up_cols)
    res = _adamw(conv_w.reshape(flat), grads["conv_w"].reshape(flat), m["conv_w"].reshape(flat),
                 v["conv_w"].reshape(flat), "adamw_conv_w")
    delta["conv_w"], new_m["conv_w"], new_v["conv_w"] = (r.reshape(conv_w.shape) for r in res)
    shapes = [w[k].shape for k in REPLICATED_NAMES]
    packed = [_pack([d[k] for k in REPLICATED_NAMES], 128) for d in (w, grads, m, v)]
    for d, res in zip((delta, new_m, new_v), _adamw(*packed, "adamw_small")):
        for k, r in zip(REPLICATED_NAMES, _unpack(res, shapes)):
            d[k] = r

    return (loss, dx[None], *[grads[k] for k in WEIGHT_NAMES], *[delta[k] for k in WEIGHT_NAMES],
            *[new_m[k] for k in WEIGHT_NAMES], *[new_v[k] for k in WEIGHT_NAMES])
```

```python
import functools
import math

import jax
import jax.numpy as jnp
from jax import lax
from jax.experimental import pallas as pl
from jax.experimental.pallas import tpu as pltpu

F32 = jnp.float32
BF16 = jnp.bfloat16
MESH = pl.DeviceIdType.MESH

D_MODEL = 2048
SEQ = 2048
DEPTH = 2
HEAD_DIM = 64
N_HEADS_A = 12
N_HEADS_B = 10
N_KV_B = 2
N_HEADS_C = 10
WINDOW_B = 128
GRID_W = 64
NA_ROWS = 8
NA_COLS = 16
WIDTH_A = N_HEADS_A * HEAD_DIM
WIDTH_B = N_HEADS_B * HEAD_DIM
WIDTH_C = N_HEADS_C * HEAD_DIM
IN_COLS = 5120
D_FF = 5632
ROPE_THETA = 10000.0
EPS = 1e-6
NEG_INF = -1e30
N_SHARDS = 4

ADAM_LR = 0.001
ADAM_B1 = 0.9
ADAM_B2 = 0.999
ADAM_EPS = 1e-08
ADAM_WD = 0.01
ADAM_STEP = 10

LANES = 128
QB = 256
NQB = SEQ // QB
ROWS = 256
MIB = 2 ** 20

A_BLK = (0, 6, 12)
B_BLK = (18, 23, 24)
C_BLK = (25, 30, 35)
ROPE_BLKS = tuple(range(0, 12)) + tuple(range(18, 24))
QSCALE_BLKS = tuple(range(0, 6)) + tuple(range(18, 23)) + tuple(range(25, 30))
N_PBLK = IN_COLS // LANES


def _params(sem, vmem_mib):
    return pltpu.CompilerParams(dimension_semantics=sem, vmem_limit_bytes=vmem_mib * MIB)


def _weight_spec(w, cols, t_in, t_out, transposed):
    s, r, c = w.shape
    if cols:
        per = c // t_out
        k_dim, n = r, s * c
        if transposed:
            index = lambda j, rr: (rr // per, j, rr % per)
        else:
            index = lambda j, kk: (j // per, kk, j % per)
    else:
        per = r // t_in
        k_dim, n = s * r, c
        if transposed:
            index = lambda j, rr: (j // per, j % per, rr)
        else:
            index = lambda j, kk: (kk // per, kk % per, j)
    return pl.BlockSpec((None, t_in, t_out), index), k_dim, n


def _mm_nn(a, w, *, cols, tn, tk, out_dtype, name, residual=None, out_split=1):
    m, k_dim = a.shape
    w_spec, k_w, n = _weight_spec(w, cols, tk, tn, False)
    assert k_w == k_dim
    nj, nk = n // tn, k_dim // tk
    in_specs = [pl.BlockSpec((m, tk), lambda j, k: (0, k)), w_spec]
    args = [a, w]
    if residual is not None:
        in_specs.append(pl.BlockSpec((m, tn), lambda j, k: (0, j)))
        args.append(residual)
    if out_split > 1:
        per_o = n // out_split // tn
        out_spec = pl.BlockSpec((None, m, tn), lambda j, k: (j // per_o, 0, j % per_o))
        out_shape = pltpu.HBM((out_split, m, n // out_split), out_dtype)
    else:
        out_spec = pl.BlockSpec((m, tn), lambda j, k: (0, j))
        out_shape = pltpu.HBM((m, n), out_dtype)

    def body(*refs):
        a_ref, w_ref = refs[0], refs[1]
        r_ref = refs[2] if residual is not None else None
        o_ref = refs[3] if residual is not None else refs[2]

        def finish(val):
            if r_ref is not None:
                val = r_ref[...] + val
            o_ref[...] = val.astype(o_ref.dtype)

        part = jnp.dot(a_ref[...], w_ref[...], preferred_element_type=F32)
        if nk == 1:
            finish(part)
        else:
            acc = refs[-1]
            kk = pl.program_id(1)

            @pl.when(kk == 0)
            def _():
                acc[...] = part

            @pl.when(kk > 0)
            def _():
                acc[...] += part

            @pl.when(kk == nk - 1)
            def _():
                finish(acc[...])

    return pl.pallas_call(
        body, name=name, grid=(nj, nk), in_specs=in_specs, out_specs=out_spec, out_shape=out_shape,
        scratch_shapes=[pltpu.VMEM((m, tn), F32)] if nk > 1 else [],
        compiler_params=_params(("arbitrary", "arbitrary"), 56),
    )(*[_in_hbm(a) for a in args])


ANY_SPEC = pl.BlockSpec(memory_space=pl.ANY)


def _mm_nt(dy, w, *, cols, to, tr, out_dtype, name, after=()):
    if dy.ndim == 3:
        m = dy.shape[1]
        n = dy.shape[0] * dy.shape[2]
        per_d = dy.shape[2] // tr
        dy_spec = pl.BlockSpec((None, m, tr), lambda j, r: (r // per_d, 0, r % per_d))
    else:
        m, n = dy.shape
        dy_spec = pl.BlockSpec((m, tr), lambda j, r: (0, r))
    w_spec, k_dim, n_w = _weight_spec(w, cols, to, tr, True)
    assert n_w == n
    nj, nr = k_dim // to, n // tr

    n_after = len(after)

    def body(dy_ref, w_ref, *rest):
        o_ref = rest[n_after]
        part = lax.dot_general(dy_ref[...], w_ref[...], (((1,), (1,)), ((), ())), preferred_element_type=F32)
        if nr == 1:
            o_ref[...] = part.astype(o_ref.dtype)
        else:
            acc = rest[n_after + 1]
            rr = pl.program_id(1)

            @pl.when(rr == 0)
            def _():
                acc[...] = part

            @pl.when(rr > 0)
            def _():
                acc[...] += part

            @pl.when(rr == nr - 1)
            def _():
                o_ref[...] = acc[...].astype(o_ref.dtype)

    return pl.pallas_call(
        body, name=name, grid=(nj, nr), in_specs=[dy_spec, w_spec] + [ANY_SPEC] * n_after,
        out_specs=pl.BlockSpec((m, to), lambda j, r: (0, j)),
        out_shape=pltpu.HBM((m, k_dim), out_dtype),
        scratch_shapes=[pltpu.VMEM((m, to), F32)] if nr > 1 else [],
        compiler_params=_params(("arbitrary", "arbitrary"), 56),
    )(_in_hbm(dy), _in_hbm(w), *after)


def _mm_tn(x, dy, *, tk, tn, shards, name):
    m, k_dim = x.shape
    if dy.ndim == 3:
        n = dy.shape[0] * dy.shape[2]
        per_d = dy.shape[2] // tn
        dy_spec = pl.BlockSpec((None, m, tn), lambda i, j: (j // per_d, 0, j % per_d))
    else:
        n = dy.shape[1]
        dy_spec = pl.BlockSpec((m, tn), lambda i, j: (0, j))
    if shards > 0:
        per = n // shards // tn
        out_shape = pltpu.HBM((shards, k_dim, n // shards), BF16)
        out_spec = pl.BlockSpec((None, tk, tn), lambda i, j: (j // per, i, j % per))
    else:
        s = -shards
        per = k_dim // s // tk
        out_shape = pltpu.HBM((s, k_dim // s, n), BF16)
        out_spec = pl.BlockSpec((None, tk, tn), lambda i, j: (i // per, i % per, j))

    def body(x_ref, dy_ref, o_ref):
        o_ref[...] = lax.dot_general(x_ref[...], dy_ref[...], (((0,), (0,)), ((), ())),
                                     preferred_element_type=F32).astype(BF16)

    return pl.pallas_call(
        body, name=name, grid=(k_dim // tk, n // tn),
        in_specs=[pl.BlockSpec((m, tk), lambda i, j: (0, i)), dy_spec], out_specs=out_spec, out_shape=out_shape,
        compiler_params=_params(("arbitrary", "arbitrary"), 56),
    )(_in_hbm(x), _in_hbm(dy))


def _row_spec(width, rows=ROWS):
    return pl.BlockSpec((rows, width), lambda i: (i, 0))


def _vec_spec(width):
    return pl.BlockSpec((1, width), lambda i: (0, 0))


def _rms_stats(x):
    r = lax.rsqrt(jnp.mean(x * x, axis=-1, keepdims=True) + EPS)
    return r, x * r


def _rmsnorm_fwd(x, gain, name):
    t, d = x.shape

    def body(x_ref, g_ref, o_ref):
        _, n = _rms_stats(x_ref[...])
        o_ref[...] = (n * g_ref[...]).astype(BF16)

    return pl.pallas_call(
        body, name=name, grid=(t // ROWS,), in_specs=[_row_spec(d), _vec_spec(d)], out_specs=_row_spec(d),
        out_shape=pltpu.HBM((t, d), BF16), compiler_params=_params(("arbitrary",), 32),
    )(_in_hbm(x), _in_hbm(gain))


def _rmsnorm_bwd(x, gain, dh, dres, name, after=()):
    t, d = x.shape
    n_after = len(after)

    def body(x_ref, g_ref, dh_ref, dres_ref, *rest):
        dx_ref, dxb_ref, dg_ref = rest[n_after:]
        r, n = _rms_stats(x_ref[...])
        dh_v = dh_ref[...]
        dn = dh_v * g_ref[...]
        dx = dres_ref[...] + r * (dn - n * jnp.mean(dn * n, axis=-1, keepdims=True))
        dx_ref[...] = dx
        dxb_ref[...] = dx.astype(BF16)
        part = jnp.sum(dh_v * n, axis=0, keepdims=True)

        @pl.when(pl.program_id(0) == 0)
        def _():
            dg_ref[...] = part

        @pl.when(pl.program_id(0) > 0)
        def _():
            dg_ref[...] += part

    return pl.pallas_call(
        body, name=name, grid=(t // ROWS,),
        in_specs=[_row_spec(d), _vec_spec(d), _row_spec(d), _row_spec(d)] + [ANY_SPEC] * n_after,
        out_specs=[_row_spec(d), _row_spec(d), _vec_spec(d)],
        out_shape=[pltpu.HBM((t, d), F32), pltpu.HBM((t, d), BF16), jax.ShapeDtypeStruct((1, d), F32)],
        compiler_params=_params(("arbitrary",), 40),
    )(_in_hbm(x), _in_hbm(gain), _in_hbm(dh), _in_hbm(dres), *after)


def _loss_head(x, gain, target, name):
    t, d = x.shape

    def body(x_ref, g_ref, t_ref, loss_ref, dx_ref, dxb_ref, dg_ref):
        r, n = _rms_stats(x_ref[...])
        g = g_ref[...]
        err = n * g - t_ref[...]
        dy = err * (1.0 / d)
        dn = dy * g
        dx = r * (dn - n * jnp.mean(dn * n, axis=-1, keepdims=True))
        dx_ref[...] = dx
        dxb_ref[...] = dx.astype(BF16)
        part = jnp.sum(dy * n, axis=0, keepdims=True)
        lpart = jnp.zeros((8, LANES), F32) + 0.5 * jnp.sum(jnp.mean(err * err, axis=-1, keepdims=True))

        @pl.when(pl.program_id(0) == 0)
        def _():
            dg_ref[...] = part
            loss_ref[...] = lpart

        @pl.when(pl.program_id(0) > 0)
        def _():
            dg_ref[...] += part
            loss_ref[...] += lpart

    return pl.pallas_call(
        body, name=name, grid=(t // ROWS,),
        in_specs=[_row_spec(d), _vec_spec(d), _row_spec(d)],
        out_specs=[pl.BlockSpec((8, LANES), lambda i: (0, 0)), _row_spec(d), _row_spec(d), _vec_spec(d)],
        out_shape=[jax.ShapeDtypeStruct((8, LANES), F32), pltpu.HBM((t, d), F32), pltpu.HBM((t, d), BF16),
                   jax.ShapeDtypeStruct((1, d), F32)],
        compiler_params=_params(("arbitrary",), 40),
    )(x, gain, target)


def _swap_halves(x):
    lane = lax.broadcasted_iota(jnp.int32, x.shape, 1)
    return jnp.where((lane % HEAD_DIM) < HEAD_DIM // 2, pltpu.roll(x, LANES - HEAD_DIM // 2, 1),
                     pltpu.roll(x, HEAD_DIM // 2, 1))


def _rope_tables(t):
    inv_freq = ROPE_THETA ** (-jnp.arange(0, HEAD_DIM, 2, dtype=F32) / HEAD_DIM)
    ang = jnp.arange(t, dtype=F32)[:, None] * inv_freq[None, :]
    cos = jnp.tile(jnp.cos(ang), (1, LANES // (HEAD_DIM // 2)))
    sin = jnp.tile(jnp.sin(ang), (1, LANES // (HEAD_DIM // 2)))
    lane = jnp.arange(LANES)[None, :]
    return cos, jnp.where((lane % HEAD_DIM) < HEAD_DIM // 2, -sin, sin)


def _rope_fwd(proj, cos, sin, name):
    t = proj.shape[0]
    scale = HEAD_DIM ** -0.5

    def body(p_ref, c_ref, s_ref, o_ref):
        cos_v, sin_v = c_ref[...], s_ref[...]
        for b in range(N_PBLK):
            cols = slice(b * LANES, (b + 1) * LANES)
            v = p_ref[:, cols]
            if b in ROPE_BLKS:
                v = v * cos_v + _swap_halves(v) * sin_v
            if b in QSCALE_BLKS:
                v = v * scale
            o_ref[:, cols] = v.astype(BF16)

    return pl.pallas_call(
        body, name=name, grid=(t // ROWS,),
        in_specs=[_row_spec(IN_COLS), _row_spec(LANES), _row_spec(LANES)], out_specs=_row_spec(IN_COLS),
        out_shape=pltpu.HBM((t, IN_COLS), BF16), compiler_params=_params(("arbitrary",), 40),
    )(_in_hbm(proj), _in_hbm(cos), _in_hbm(sin))


def _rope_bwd(grads, cos, sin, name):
    t = grads[0].shape[0]
    scale = HEAD_DIM ** -0.5
    group = N_HEADS_B // N_KV_B

    def body(*refs):
        c_ref, s_ref, o_ref = refs[9], refs[10], refs[11]
        cos_v, sin_v = c_ref[...], s_ref[...]

        def kv_sum(ref):
            parts = []
            for g in range(N_KV_B):
                acc = ref[:, g * group * HEAD_DIM:(g * group + 1) * HEAD_DIM]
                for h in range(g * group + 1, (g + 1) * group):
                    acc = acc + ref[:, h * HEAD_DIM:(h + 1) * HEAD_DIM]
                parts.append(acc)
            return jnp.concatenate(parts, axis=1)

        def emit(b, v):
            if b in ROPE_BLKS:
                v = v * cos_v - _swap_halves(v) * sin_v
            if b in QSCALE_BLKS:
                v = v * scale
            o_ref[:, b * LANES:(b + 1) * LANES] = v.astype(BF16)

        starts = (A_BLK[0], A_BLK[1], A_BLK[2], B_BLK[0], None, None, C_BLK[0], C_BLK[1], C_BLK[2])
        for idx, start in enumerate(starts):
            if start is None:
                continue
            for j in range(refs[idx].shape[1] // LANES):
                emit(start + j, refs[idx][:, j * LANES:(j + 1) * LANES])
        emit(B_BLK[1], kv_sum(refs[4]))
        emit(B_BLK[2], kv_sum(refs[5]))

    return pl.pallas_call(
        body, name=name, grid=(t // ROWS,),
        in_specs=[_row_spec(g.shape[1]) for g in grads] + [_row_spec(LANES), _row_spec(LANES)],
        out_specs=_row_spec(IN_COLS),
        out_shape=pltpu.HBM((t, IN_COLS), BF16), compiler_params=_params(("arbitrary",), 40),
    )(*[_in_hbm(g) for g in grads], _in_hbm(cos), _in_hbm(sin))


GROUP_COLS = ((0, WIDTH_A), (WIDTH_A, WIDTH_A + WIDTH_B), (WIDTH_A + WIDTH_B, D_MODEL))


def _mix_fwd(oa, ob, oc, gain, name):
    t = oa.shape[0]

    def body(a_ref, b_ref, c_ref, g_ref, o_ref):
        for ref, (lo, hi) in zip((a_ref, b_ref, c_ref), GROUP_COLS):
            _, n = _rms_stats(ref[...])
            o_ref[:, lo:hi] = (n * g_ref[:, lo:hi]).astype(BF16)

    return pl.pallas_call(
        body, name=name, grid=(t // ROWS,),
        in_specs=[_row_spec(WIDTH_A), _row_spec(WIDTH_B), _row_spec(WIDTH_C), _vec_spec(D_MODEL)],
        out_specs=_row_spec(D_MODEL),
        out_shape=pltpu.HBM((t, D_MODEL), BF16), compiler_params=_params(("arbitrary",), 32),
    )(_in_hbm(oa), _in_hbm(ob), _in_hbm(oc), _in_hbm(gain))


def _mix_bwd(oa, ob, oc, gain, dmixed, name, after=()):
    t = oa.shape[0]
    n_after = len(after)

    def body(a_ref, b_ref, c_ref, g_ref, dm_ref, *rest):
        da_ref, db_ref, dc_ref, dg_ref = rest[n_after:]
        first = pl.program_id(0) == 0
        for ref, dref, (lo, hi) in zip((a_ref, b_ref, c_ref), (da_ref, db_ref, dc_ref), GROUP_COLS):
            r, n = _rms_stats(ref[...])
            dm = dm_ref[:, lo:hi]
            dn = dm * g_ref[:, lo:hi]
            dref[...] = r * (dn - n * jnp.mean(dn * n, axis=-1, keepdims=True))
            part = jnp.sum(dm * n, axis=0, keepdims=True)

            @pl.when(first)
            def _():
                dg_ref[:, lo:hi] = part

            @pl.when(jnp.logical_not(first))
            def _():
                dg_ref[:, lo:hi] += part

    return pl.pallas_call(
        body, name=name, grid=(t // ROWS,),
        in_specs=[_row_spec(WIDTH_A), _row_spec(WIDTH_B), _row_spec(WIDTH_C), _vec_spec(D_MODEL), _row_spec(D_MODEL)]
        + [ANY_SPEC] * n_after,
        out_specs=[_row_spec(WIDTH_A), _row_spec(WIDTH_B), _row_spec(WIDTH_C), _vec_spec(D_MODEL)],
        out_shape=[pltpu.HBM((t, WIDTH_A), F32), pltpu.HBM((t, WIDTH_B), F32), pltpu.HBM((t, WIDTH_C), F32),
                   jax.ShapeDtypeStruct((1, D_MODEL), F32)],
        compiler_params=_params(("arbitrary",), 40),
    )(_in_hbm(oa), _in_hbm(ob), _in_hbm(oc), _in_hbm(gain), _in_hbm(dmixed), *after)


FF_COLS = 256


SUBLANES = 8
CHUNK_FWD = 256
CHUNK_BWD = 128
HALO = SUBLANES


def _ext_rows(ref, r0, chunk, where):
    t, cols = ref.shape
    zeros = jnp.zeros((HALO, cols), F32)
    if where == "first":
        return jnp.concatenate([zeros, ref[0:chunk + HALO, :]], axis=0)
    if where == "last":
        return jnp.concatenate([ref[t - chunk - HALO:t, :], zeros], axis=0)
    return ref[pl.ds(pl.multiple_of(r0 - HALO, HALO), chunk + 2 * HALO), :]


def _for_chunks(t, chunk, fn):
    fn(0, "first")

    def mid(ci, carry):
        fn(pl.multiple_of(ci * chunk, chunk), "mid")
        return carry

    lax.fori_loop(1, t // chunk - 1, mid, 0)
    fn(t - chunk, "last")


def _roll_rows(x, by):
    return pltpu.roll(x, by % x.shape[0], 0)


def _gate_val(u_ref, r0, chunk, where, w_ref, b_ref):
    ext = [_ext_rows(u_ref.at[h], r0, chunk, where) for h in range(2)]
    before = [_roll_rows(e, 1) for e in ext]
    after = [_roll_rows(e, -1) for e in ext]
    gate, val = ((before[h] * w_ref[h, 0:1, :] + ext[h] * w_ref[h, 1:2, :]) + after[h] * w_ref[h, 2:3, :] + b_ref[h]
                 for h in range(2))
    return gate, val, ext, before, after


def _ff_specs(t):
    u_spec = pl.BlockSpec((2, t, FF_COLS), lambda j: (0, 0, j))
    w_spec = pl.BlockSpec((2, 3, FF_COLS), lambda j: (0, 0, j))
    b_spec = pl.BlockSpec((2, 1, FF_COLS), lambda j: (0, 0, j))
    return u_spec, w_spec, b_spec


def _convgate_fwd(u0, conv_w, conv_b, name):
    t = u0.shape[1]
    u_spec, w_spec, b_spec = _ff_specs(t)

    def body(u_ref, w_ref, b_ref, o_ref):
        def chunk(r0, where):
            gate, val, _, _, _ = _gate_val(u_ref, r0, CHUNK_FWD, where, w_ref, b_ref)
            act = gate * jax.nn.sigmoid(gate) * val
            o_ref[pl.ds(r0, CHUNK_FWD), :] = act[HALO:HALO + CHUNK_FWD].astype(BF16)

        _for_chunks(t, CHUNK_FWD, chunk)

    return pl.pallas_call(
        body, name=name, grid=(D_FF // FF_COLS,), in_specs=[u_spec, w_spec, b_spec],
        out_specs=pl.BlockSpec((t, FF_COLS), lambda j: (0, j)),
        out_shape=pltpu.HBM((t, D_FF), BF16), compiler_params=_params(("arbitrary",), 48),
    )(_in_hbm(u0), conv_w, conv_b)


def _convgate_bwd(u0, conv_w, conv_b, d_act, name):
    t = u0.shape[1]
    u_spec, w_spec, b_spec = _ff_specs(t)

    def body(u_ref, w_ref, b_ref, da_ref, du_ref, dw_ref, db_ref, sums_ref):
        sums_ref[...] = jnp.zeros_like(sums_ref)
        inner = slice(HALO, HALO + CHUNK_BWD)

        def fold(x):
            return jnp.sum(x.reshape(CHUNK_BWD // SUBLANES, SUBLANES, x.shape[1]), axis=0)

        def chunk(r0, where):
            gate, val, ext, before, after = _gate_val(u_ref, r0, CHUNK_BWD, where, w_ref, b_ref)
            sig = jax.nn.sigmoid(gate)
            da = _ext_rows(da_ref, r0, CHUNK_BWD, where)
            d_half = (da * val * (sig * (1.0 + gate * (1.0 - sig))), da * (gate * sig))
            for h in range(2):
                du = d_half[h]
                for k, term in enumerate((du, du * before[h], du * ext[h], du * after[h])):
                    sums_ref[h, k] += fold(term[inner])
                du0 = (_roll_rows(du, -1) * w_ref[h, 0:1, :] + du * w_ref[h, 1:2, :]) + _roll_rows(du, 1) * w_ref[h, 2:3, :]
                du_ref[h, pl.ds(r0, CHUNK_BWD), :] = du0[inner].astype(BF16)

        _for_chunks(t, CHUNK_BWD, chunk)
        for h in range(2):
            db_ref[h] = jnp.sum(sums_ref[h, 0], axis=0, keepdims=True)
            for k in range(3):
                dw_ref[h, k:k + 1, :] = jnp.sum(sums_ref[h, k + 1], axis=0, keepdims=True)

    return pl.pallas_call(
        body, name=name, grid=(D_FF // FF_COLS,),
        in_specs=[u_spec, w_spec, b_spec, pl.BlockSpec((t, FF_COLS), lambda j: (0, j))],
        out_specs=[u_spec, w_spec, b_spec],
        out_shape=[pltpu.HBM((2, t, D_FF), BF16), jax.ShapeDtypeStruct((2, 3, D_FF), F32),
                   jax.ShapeDtypeStruct((2, 1, D_FF), F32)],
        scratch_shapes=[pltpu.VMEM((2, 4, SUBLANES, FF_COLS), F32)],
        compiler_params=_params(("arbitrary",), 56),
    )(_in_hbm(u0), conv_w, conv_b, _in_hbm(d_act))


class _Group:
    def __init__(self, heads, blks, kv_rows, n_win, gqa, bias_per_head):
        self.heads = heads
        self.pairs = heads // 2
        self.q_blk, self.k_blk, self.v_blk = blks
        self.kv_rows = kv_rows
        self.n_win = n_win
        self.full = kv_rows == SEQ
        self.gqa = gqa
        self.bias_per_head = bias_per_head
        self.width = heads * HEAD_DIM
        self.keys = kv_rows * n_win


GROUP_A = _Group(N_HEADS_A, A_BLK, SEQ, 1, False, False)
GROUP_B = _Group(N_HEADS_B, B_BLK, WINDOW_B, 4, True, False)
GROUP_C = _Group(N_HEADS_C, C_BLK, QB, 3, False, True)


def _win_start(grp, i):
    return jnp.clip(i * (QB // grp.kv_rows) - 1, 0, SEQ // grp.kv_rows - grp.n_win)


def _win_variant(i):
    return jnp.minimum(i, 1) + (i == NQB - 1).astype(jnp.int32)


def _attn_in_specs(grp, t):
    q_spec = pl.BlockSpec((QB, LANES), lambda p, i: (i, grp.q_blk + p))

    def col(blk):
        return (lambda p: blk) if grp.gqa else (lambda p: blk + p)

    def kv_specs(blk):
        c = col(blk)
        if grp.full:
            return [pl.BlockSpec((t, LANES), lambda p, i: (0, c(p)))]
        return [pl.BlockSpec((grp.kv_rows, LANES),
                             functools.partial(lambda p, i, w: (_win_start(grp, i) + w, c(p)), w=w))
                for w in range(grp.n_win)]

    nwk = grp.keys
    if grp.bias_per_head:
        bias_spec = pl.BlockSpec((2, None, QB, nwk), lambda p, i: (p, _win_variant(i), 0, 0))
    elif grp.full:
        bias_spec = pl.BlockSpec((1, None, QB, nwk), lambda p, i: (0, i, 0, 0))
    else:
        bias_spec = pl.BlockSpec((1, None, QB, nwk), lambda p, i: (0, _win_variant(i), 0, 0))
    sink_spec = pl.BlockSpec((1, LANES), lambda p, i: (0, p))
    return q_spec, kv_specs(grp.k_blk), kv_specs(grp.v_blk), bias_spec, sink_spec


def _head_kv(grp, whole, e, p):
    lo, hi = whole[:, :HEAD_DIM], whole[:, HEAD_DIM:]
    if grp.gqa:
        return jnp.where(2 * p + e >= N_HEADS_B // N_KV_B, hi, lo)
    return hi if e else lo


def _softmax_parts(q, k, bias, sink):
    s = lax.dot_general(q, k, (((1,), (1,)), ((), ())), preferred_element_type=F32) + bias
    m = jnp.maximum(jnp.max(s, axis=-1, keepdims=True), sink)
    pe = jnp.exp(s - m)
    denom = jnp.sum(pe, axis=-1, keepdims=True) + jnp.exp(sink - m)
    return pe, m, 1.0 / denom


def _attn_fwd(grp, proj, bias, sink, name):
    t = proj.shape[0]
    q_spec, k_specs, v_specs, bias_spec, sink_spec = _attn_in_specs(grp, t)
    nkv = len(k_specs)

    def body(*refs):
        q_ref = refs[0]
        k_refs, v_refs = refs[1:1 + nkv], refs[1 + nkv:1 + 2 * nkv]
        bias_ref, sink_ref, o_ref = refs[1 + 2 * nkv:4 + 2 * nkv]
        p = pl.program_id(0)
        k_all = jnp.concatenate([r[...] for r in k_refs], axis=0)
        v_all = jnp.concatenate([r[...] for r in v_refs], axis=0)
        outs = []
        for e in range(2):
            q = q_ref[:, e * HEAD_DIM:(e + 1) * HEAD_DIM]
            k = _head_kv(grp, k_all, e, p)
            v = _head_kv(grp, v_all, e, p)
            snk = sink_ref[0:1, e * HEAD_DIM:e * HEAD_DIM + 1]
            pe, _, inv = _softmax_parts(q, k, bias_ref[e if grp.bias_per_head else 0], snk)
            outs.append(jnp.dot(pe.astype(BF16), v, preferred_element_type=F32) * inv)
        o_ref[...] = jnp.concatenate(outs, axis=1)

    return pl.pallas_call(
        body, name=name, grid=(grp.pairs, NQB),
        in_specs=[q_spec, *k_specs, *v_specs, bias_spec, sink_spec],
        out_specs=pl.BlockSpec((QB, LANES), lambda p, i: (i, p)),
        out_shape=pltpu.HBM((t, grp.width), F32),
        compiler_params=_params(("arbitrary", "arbitrary"), 48),
    )(*([_in_hbm(proj)] * (1 + 2 * nkv)), _in_hbm(bias), sink)


def _attn_bwd(grp, proj, bias, sink, out, d_out, name):
    t = proj.shape[0]
    q_spec, k_specs, v_specs, bias_spec, sink_spec = _attn_in_specs(grp, t)
    nkv = len(k_specs)
    n_off = 2 * NA_ROWS - 1
    rows_q = QB // GRID_W
    wide = grp.keys > 2 * QB
    o_spec = pl.BlockSpec((QB, LANES), lambda p, i: (i, p))
    acc_spec = pl.BlockSpec((t, LANES), lambda p, i: (0, p))
    out_specs = [o_spec, acc_spec, acc_spec, pl.BlockSpec((None, 8, LANES), lambda p, i: (p, 0, 0))]
    out_shape = [pltpu.HBM((t, grp.width), F32)] * 3 + [jax.ShapeDtypeStruct((grp.pairs, 8, LANES), F32)]
    if grp.bias_per_head:
        out_specs.append(pl.BlockSpec((2, n_off, GRID_W, GRID_W), lambda p, i: (p, 0, 0, 0)))
        out_shape.append(jax.ShapeDtypeStruct((grp.heads, n_off, GRID_W, GRID_W), F32))

    def body(*refs):
        q_ref = refs[0]
        k_refs, v_refs = refs[1:1 + nkv], refs[1 + nkv:1 + 2 * nkv]
        bias_ref, sink_ref, o_ref, do_ref = refs[1 + 2 * nkv:5 + 2 * nkv]
        dq_ref, dk_ref, dv_ref, dsink_ref = refs[5 + 2 * nkv:9 + 2 * nkv]
        dbias_ref = refs[9 + 2 * nkv] if grp.bias_per_head else None
        p, i = pl.program_id(0), pl.program_id(1)

        @pl.when(i == 0)
        def _():
            dk_ref[...] = jnp.zeros_like(dk_ref)
            dv_ref[...] = jnp.zeros_like(dv_ref)
            dsink_ref[...] = jnp.zeros_like(dsink_ref)
            if dbias_ref is not None:
                dbias_ref[...] = jnp.zeros_like(dbias_ref)

        k_all = jnp.concatenate([r[...] for r in k_refs], axis=0)
        v_all = jnp.concatenate([r[...] for r in v_refs], axis=0)
        start = 0 if grp.full else _win_start(grp, i)
        dqs, dks, dvs, dsinks = [], [], [], []
        for e in range(2):
            cols = slice(e * HEAD_DIM, (e + 1) * HEAD_DIM)
            q = q_ref[:, cols]
            k = _head_kv(grp, k_all, e, p)
            v = _head_kv(grp, v_all, e, p)
            snk = sink_ref[0:1, e * HEAD_DIM:e * HEAD_DIM + 1]
            pe, m, inv = _softmax_parts(q, k, bias_ref[e if grp.bias_per_head else 0], snk)
            prob = pe * inv
            do = do_ref[:, cols]
            do_b = do.astype(BF16)
            pe_b = prob.astype(BF16)
            delta = jnp.sum(do * o_ref[:, cols], axis=-1, keepdims=True)
            dp = lax.dot_general(do_b, v, (((1,), (1,)), ((), ())), preferred_element_type=F32)
            ds = prob * (dp - delta)
            ds_b = ds.astype(BF16)
            dqs.append(jnp.dot(ds_b, k, preferred_element_type=F32))
            if wide:
                dks.append(lax.dot_general(q, ds_b, (((0,), (0,)), ((), ())), preferred_element_type=F32))
                dvs.append(lax.dot_general(do_b, pe_b, (((0,), (0,)), ((), ())), preferred_element_type=F32))
            else:
                dks.append(lax.dot_general(ds_b, q, (((0,), (0,)), ((), ())), preferred_element_type=F32))
                dvs.append(lax.dot_general(pe_b, do_b, (((0,), (0,)), ((), ())), preferred_element_type=F32))
            dsinks.append(-jnp.sum(jnp.exp(snk - m) * inv * delta, axis=0, keepdims=True))
            if dbias_ref is not None:
                shift = (i * QB - start * grp.kv_rows) // GRID_W
                for rq in range(rows_q):
                    for rk in range(grp.keys // GRID_W):
                        off = jnp.clip(rk - rq + (NA_ROWS - 1) - shift, 0, n_off - 1)
                        dbias_ref[e, off] += ds[rq * GRID_W:(rq + 1) * GRID_W, rk * GRID_W:(rk + 1) * GRID_W]
        dq_ref[...] = jnp.concatenate(dqs, axis=1)
        rows = pl.ds(0, t) if grp.full else pl.ds(pl.multiple_of(start * grp.kv_rows, grp.kv_rows), grp.keys)
        if wide:
            dk_ref[rows, :] += jnp.concatenate(dks, axis=0).T
            dv_ref[rows, :] += jnp.concatenate(dvs, axis=0).T
        else:
            dk_ref[rows, :] += jnp.concatenate(dks, axis=1)
            dv_ref[rows, :] += jnp.concatenate(dvs, axis=1)
        lane = lax.broadcasted_iota(jnp.int32, (8, LANES), 1)
        dsink_ref[...] += jnp.where(lane < HEAD_DIM, dsinks[0], dsinks[1])

    return pl.pallas_call(
        body, name=name, grid=(grp.pairs, NQB),
        in_specs=[q_spec, *k_specs, *v_specs, bias_spec, sink_spec, o_spec, o_spec],
        out_specs=out_specs, out_shape=out_shape,
        compiler_params=_params(("arbitrary", "arbitrary"), 56),
    )(*([_in_hbm(proj)] * (1 + 2 * nkv)), _in_hbm(bias), sink, _in_hbm(out), _in_hbm(d_out))


DILATED_CONFIGS = ((128, 1), (512, 4), (2048, 16))


def _bias_a():
    d = jnp.arange(SEQ)[None, :] - jnp.arange(SEQ)[:, None]
    mult = jnp.zeros((SEQ, SEQ), F32)
    for window, r in DILATED_CONFIGS:
        reach = (window // (2 * r)) * r
        mult = mult + ((d % r == 0) & (jnp.abs(d) <= reach)).astype(F32)
    return jnp.where(mult > 0, jnp.log(jnp.maximum(mult, 1.0)), NEG_INF).reshape(1, NQB, QB, SEQ)


def _bias_b():
    row = jnp.arange(QB)[None, :, None]
    col = jnp.arange(GROUP_B.keys)[None, None, :]
    var = jnp.arange(3)[:, None, None]
    d = col - (GROUP_B.kv_rows * var + row)
    return jnp.where(jnp.abs(d) <= WINDOW_B, 0.0, NEG_INF).astype(F32)[None]


def _offset_onehot():
    c = jnp.arange(GRID_W)[:, None, None]
    c2 = jnp.arange(GRID_W)[None, :, None]
    b = jnp.arange(LANES)[None, None, :]
    return (c2 - c + NA_COLS - 1 == b).astype(BF16).reshape(GRID_W * GRID_W, LANES)


def _split_dot(x, g):
    hi = x.astype(BF16)
    rest = x - hi.astype(F32)
    mid = rest.astype(BF16)
    lo = (rest - mid.astype(F32)).astype(BF16)
    return (jnp.dot(hi, g, preferred_element_type=F32) + jnp.dot(mid, g, preferred_element_type=F32)
            + jnp.dot(lo, g, preferred_element_type=F32))


def _table_mm(x, g, name):
    def body(x_ref, g_ref, o_ref):
        o_ref[...] = _split_dot(x_ref[...], g_ref[...])

    return pl.pallas_call(
        body, name=name, out_shape=jax.ShapeDtypeStruct((x.shape[0], g.shape[1]), F32),
        in_specs=[pl.BlockSpec(memory_space=pltpu.VMEM)] * 2, out_specs=pl.BlockSpec(memory_space=pltpu.VMEM),
        compiler_params=pltpu.CompilerParams(vmem_limit_bytes=32 * MIB),
    )(x, g)


N_OFF = 2 * NA_ROWS - 1
TABLE_ROWS = 152


def _bias_c(rpb):
    table = jnp.zeros((TABLE_ROWS, LANES), F32).at[:N_HEADS_C * N_OFF, :2 * NA_COLS - 1].set(
        rpb.reshape(N_HEADS_C * N_OFF, 2 * NA_COLS - 1))
    tiles = _table_mm(table, _offset_onehot().T, "rpb_tiles")[:N_HEADS_C * N_OFF]
    tiles = tiles.reshape(N_HEADS_C, N_OFF, GRID_W, GRID_W)
    c = jnp.arange(GRID_W)
    col_start = jnp.clip(c - NA_COLS // 2, 0, GRID_W - NA_COLS)
    col_ok = (c[None, :] >= col_start[:, None]) & (c[None, :] < col_start[:, None] + NA_COLS)
    tiles = jnp.where(col_ok, tiles, NEG_INF)
    rows_q = QB // GRID_W
    rows_k = GROUP_C.keys // GRID_W

    def body(t_ref, o_ref):
        for var in range(3):
            for rq in range(rows_q):
                r_l = rows_q * var + rq
                first = min(max(r_l - NA_ROWS // 2, 0), rows_k - NA_ROWS)
                for rk in range(rows_k):
                    if first <= rk < first + NA_ROWS:
                        tile = t_ref[rk - r_l + NA_ROWS - 1]
                    else:
                        tile = jnp.full((GRID_W, GRID_W), NEG_INF, F32)
                    o_ref[var, rq * GRID_W:(rq + 1) * GRID_W, rk * GRID_W:(rk + 1) * GRID_W] = tile

    return pl.pallas_call(
        body, name="bias_c", grid=(N_HEADS_C,),
        in_specs=[pl.BlockSpec((None, N_OFF, GRID_W, GRID_W), lambda h: (h, 0, 0, 0))],
        out_specs=pl.BlockSpec((None, 3, QB, GROUP_C.keys), lambda h: (h, 0, 0, 0)),
        out_shape=jax.ShapeDtypeStruct((N_HEADS_C, 3, QB, GROUP_C.keys), F32),
        compiler_params=_params(("arbitrary",), 32),
    )(tiles)


def _rpb_grad(d_tiles):
    flat = jnp.zeros((TABLE_ROWS, GRID_W * GRID_W), F32).at[:N_HEADS_C * N_OFF].set(
        d_tiles.reshape(N_HEADS_C * N_OFF, GRID_W * GRID_W))
    out = _table_mm(flat, _offset_onehot(), "rpb_grad")
    return out[:N_HEADS_C * N_OFF, :2 * NA_COLS - 1].reshape(N_HEADS_C, N_OFF, 2 * NA_COLS - 1)


def _sink_lanes(sink):
    return jnp.repeat(sink.astype(F32), HEAD_DIM)[None, :]


def _attention_fwd(proj_r, sink_b, bias_a, bias_b, bias_c):
    no_sink_a = jnp.full((1, WIDTH_A), NEG_INF, F32)
    no_sink_c = jnp.full((1, WIDTH_C), NEG_INF, F32)
    oa = _attn_fwd(GROUP_A, proj_r, bias_a, no_sink_a, "attn_a_fwd")
    ob = _attn_fwd(GROUP_B, proj_r, bias_b, _sink_lanes(sink_b), "attn_b_fwd")
    oc = _attn_fwd(GROUP_C, proj_r, bias_c, no_sink_c, "attn_c_fwd")
    return oa, ob, oc


def _attention_bwd(proj_r, sink_b, bias_a, bias_b, bias_c, outs, d_outs, cos, sin):
    no_sink_a = jnp.full((1, WIDTH_A), NEG_INF, F32)
    no_sink_c = jnp.full((1, WIDTH_C), NEG_INF, F32)
    dqa, dka, dva, _ = _attn_bwd(GROUP_A, proj_r, bias_a, no_sink_a, outs[0], d_outs[0], "attn_a_bwd")
    dqb, dkb, dvb, dsink = _attn_bwd(GROUP_B, proj_r, bias_b, _sink_lanes(sink_b), outs[1], d_outs[1], "attn_b_bwd")
    dqc, dkc, dvc, _, d_tiles = _attn_bwd(GROUP_C, proj_r, bias_c, no_sink_c, outs[2], d_outs[2], "attn_c_bwd")
    d_proj = _rope_bwd((dqa, dka, dva, dqb, dkb, dvb, dqc, dkc, dvc), cos, sin, "rope_bwd")
    d_sink = dsink[:, 0, :].reshape(GROUP_B.pairs, 2, HEAD_DIM)[:, :, 0].reshape(N_HEADS_B)
    return d_proj, d_sink, _rpb_grad(d_tiles)


def _adamw(w, g, m, v, name):
    r, c = w.shape
    rows = r
    for cand in (512, 256, 128, 64, 32, 16, 8):
        if r % cand == 0 and cand * c * 4 <= MIB:
            rows = cand
            break
    spec = pl.BlockSpec((rows, c), lambda i: (i, 0))

    def body(w_ref, g_ref, m_ref, v_ref, d_ref, mo_ref, vo_ref):
        d_ref[...], mo_ref[...], vo_ref[...] = _adamw_step(w_ref[...], g_ref[...], m_ref[...], v_ref[...])

    return pl.pallas_call(
        body, name=name, grid=(r // rows,), in_specs=[spec] * 4, out_specs=[spec] * 3,
        out_shape=[jax.ShapeDtypeStruct((r, c), F32)] * 3, compiler_params=_params(("arbitrary",), 32),
    )(w, g, m, v)


def _adamw_step(w, grad, m, v):
    m_new = ADAM_B1 * m + (1.0 - ADAM_B1) * grad
    v_new = ADAM_B2 * v + (1.0 - ADAM_B2) * jnp.square(grad)
    m_hat = m_new / (1.0 - ADAM_B1 ** ADAM_STEP)
    v_hat = v_new / (1.0 - ADAM_B2 ** ADAM_STEP)
    return -ADAM_LR * (m_hat / (jnp.sqrt(v_hat) + ADAM_EPS) + ADAM_WD * w), m_new, v_new


def _adamw_layer(w, g, m, v, layer, prev, name):
    _, r, c = w.shape
    rows = next(cand for cand in (512, 256, 128, 64, 32, 16, 8) if r % cand == 0 and cand * c * 4 <= 2 * MIB)
    spec = pl.BlockSpec((None, rows, c), lambda i: (layer, i, 0))
    g_spec = pl.BlockSpec((rows, c), lambda i: (i, 0))
    n_prev = 0 if prev is None else 4

    def body(w_ref, g_ref, m_ref, v_ref, *rest):
        go_ref, d_ref, mo_ref, vo_ref = rest[n_prev:]
        grad = g_ref[...]
        go_ref[...] = grad
        d_ref[...], mo_ref[...], vo_ref[...] = _adamw_step(w_ref[...], grad, m_ref[...], v_ref[...])

    return pl.pallas_call(
        body, name=name, grid=(r // rows,), in_specs=[spec, g_spec, spec, spec] + [ANY_SPEC] * n_prev,
        out_specs=[spec] * 4,
        out_shape=[jax.ShapeDtypeStruct(w.shape, F32)] * 4,
        input_output_aliases={4 + i: i for i in range(n_prev)}, compiler_params=_params(("arbitrary",), 48),
    )(w, g, m, v, *(prev or ()))


def _layer_fwd(x0, p, weight, tabs):
    h1 = _rmsnorm_fwd(x0, p["ln_attn"], "ln_attn_fwd")
    proj = _mm_nn(h1, weight("w_in", h1), cols=True, tn=256, tk=D_MODEL, out_dtype=F32, name="mm_in")
    proj_r = _rope_fwd(proj, tabs["cos"], tabs["sin"], "rope_fwd")
    outs = _attention_fwd(proj_r, p["sink_b"], tabs["bias_a"], tabs["bias_b"], p["bias_c"])
    mixed = _mix_fwd(*outs, p["mix_gain"], "mix_fwd")
    x1 = _mm_nn(mixed, weight("w_out", mixed), cols=False, tn=256, tk=D_MODEL, out_dtype=F32, name="mm_out",
                residual=x0)
    h2 = _rmsnorm_fwd(x1, p["ln_ffn"], "ln_ffn_fwd")
    u0 = _mm_nn(h2, weight("w_up", h2), cols=True, tn=256, tk=D_MODEL, out_dtype=F32, name="mm_up", out_split=2)
    act = _convgate_fwd(u0, p["conv_w"], p["conv_b"], "convgate_fwd")
    x2 = _mm_nn(act, weight("w_down", act), cols=False, tn=512, tk=D_FF // 2, out_dtype=F32, name="mm_down",
                residual=x1)
    return x2, (x0, h1, proj_r, outs, mixed, x1, h2, u0, act)


def _layer_bwd(dx2, dx2_b, saved, p, big, tabs, begin, finish, pending):
    x0, h1, proj_r, outs, mixed, x1, h2, u0, act = saved
    d_act = _mm_nt(dx2_b, big["w_down"], cols=False, to=512, tr=D_MODEL, out_dtype=F32, name="nt_down",
                   after=[pending[1]] if pending else [])
    g_down = _mm_tn(act, dx2_b, tk=D_FF // N_SHARDS, tn=D_MODEL, shards=-N_SHARDS, name="tn_down")
    du0, d_conv_w, d_conv_b = _convgate_bwd(u0, p["conv_w"], p["conv_b"], d_act, "convgate_bwd")
    token = [finish(pending[0], [du0])] if pending else []
    dh2 = _mm_nt(du0, big["w_up"], cols=True, to=1024, tr=D_FF // 4, out_dtype=F32, name="nt_up", after=token)
    g_up = _mm_tn(h2, du0, tk=1024, tn=D_FF // 4, shards=N_SHARDS, name="tn_up")
    first, token = begin({"w_down": g_down, "w_up": g_up})
    dx1, dx1_b, d_ln_ffn = _rmsnorm_bwd(x1, p["ln_ffn"], dh2, dx2, "ln_ffn_bwd", after=[token])
    d_mixed = _mm_nt(dx1_b, big["w_out"], cols=False, to=512, tr=D_MODEL, out_dtype=F32, name="nt_out")
    g_out = _mm_tn(mixed, dx1_b, tk=D_MODEL // N_SHARDS, tn=D_MODEL, shards=-N_SHARDS, name="tn_out")
    token = finish(first, [g_out])
    *d_outs, d_mix_gain = _mix_bwd(*outs, p["mix_gain"], d_mixed, "mix_bwd", after=[token])
    d_proj, d_sink, d_rpb = _attention_bwd(proj_r, p["sink_b"], tabs["bias_a"], tabs["bias_b"], p["bias_c"], outs,
                                           d_outs, tabs["cos"], tabs["sin"])
    dh1 = _mm_nt(d_proj, big["w_in"], cols=True, to=1024, tr=IN_COLS // N_SHARDS, out_dtype=F32, name="nt_in")
    g_in = _mm_tn(h1, d_proj, tk=1024, tn=IN_COLS // N_SHARDS, shards=N_SHARDS, name="tn_in")
    dx0, dx0_b, d_ln_attn = _rmsnorm_bwd(x0, p["ln_attn"], dh1, dx1, "ln_attn_bwd")
    small = {"ln_attn": d_ln_attn, "sink_b": d_sink, "rpb_c": d_rpb, "mix_gain": d_mix_gain, "ln_ffn": d_ln_ffn,
             "conv_w": d_conv_w, "conv_b": d_conv_b}
    return dx0, dx0_b, small, begin({"w_out": g_out, "w_in": g_in})


HBM_SPEC = pl.BlockSpec(memory_space=pl.ANY)


def _place():
    x, y, c = lax.axis_index("x"), lax.axis_index("y"), lax.axis_index("c")
    chips = ((1 - x, y), (x, 1 - y), (1 - x, 1 - y))
    return x, y, c, chips


def _shard_index(px, py):
    return 2 * px + py


def _remote(src, dst, send_sem, recv_sem, to):
    return pltpu.make_async_remote_copy(src_ref=src, dst_ref=dst, send_sem=send_sem, recv_sem=recv_sem,
                                        device_id=to, device_id_type=MESH)


def _own_slot(w, layer, shard, name):
    _, r, c_dim = w.shape
    rows = r
    for cand in (512, 256, 128):
        if r % cand == 0 and cand * c_dim * 4 <= 2 * MIB:
            rows = cand
            break

    def body(s_ref, w_ref, o_ref):
        o_ref[...] = w_ref[...].astype(BF16)

    return pl.pallas_call(
        body, name=name,
        grid_spec=pltpu.PrefetchScalarGridSpec(
            num_scalar_prefetch=1, grid=(r // rows,),
            in_specs=[pl.BlockSpec((None, rows, c_dim), lambda i, s: (layer, i, 0))],
            out_specs=pl.BlockSpec((None, rows, c_dim), lambda i, s: (s[0], i, 0))),
        out_shape=jax.ShapeDtypeStruct((N_SHARDS, r, c_dim), BF16),
        compiler_params=_params(("arbitrary",), 32),
    )(shard.astype(jnp.int32).reshape(1), w)


HBM_ONLY = pl.BlockSpec(memory_space=pltpu.HBM)
SEM_SPEC = pl.BlockSpec(memory_space=pltpu.SEMAPHORE)
DATAFLOW = pltpu.SideEffectType.DATAFLOW_SIDE_EFFECTING


def _in_hbm(a):
    return pltpu.with_memory_space_constraint(a, pltpu.HBM)


N_DEV = 8


def _all_gather_small(vec, name, after=()):
    n_after = len(after)

    def body(v_ref, *rest):
        o_ref, send, recv, local_sem = rest[n_after:]
        x, y, c, _ = _place()
        me = 4 * x + 2 * y + c
        local = pltpu.make_async_copy(v_ref, o_ref.at[me], local_sem)
        local.start()
        flips = [(fx, fy, fc) for fx in (0, 1) for fy in (0, 1) for fc in (0, 1)][1:]
        peers = [((1 - x) if fx else x, (1 - y) if fy else y, (1 - c) if fc else c) for fx, fy, fc in flips]
        cps = [_remote(v_ref, o_ref.at[me], send.at[k], recv.at[k], peer) for k, peer in enumerate(peers)]
        for cp in cps:
            cp.start()
        for k, (px, py, pc) in enumerate(peers):
            slot = o_ref.at[4 * px + 2 * py + pc]
            _remote(slot, slot, send.at[k], recv.at[k], (px, py, pc)).wait_recv()
        for cp in cps:
            cp.wait_send()
        local.wait()

    return pl.pallas_call(
        body, name=name, in_specs=[HBM_SPEC] * (1 + n_after), out_specs=HBM_SPEC,
        out_shape=jax.ShapeDtypeStruct((N_DEV,) + vec.shape, vec.dtype),
        scratch_shapes=[pltpu.SemaphoreType.DMA((N_DEV - 1,))] * 2 + [pltpu.SemaphoreType.DMA(())],
    )(vec, *after)


def _peers(x, y, c):
    flips = [(fx, fy, fc) for fx in (0, 1) for fy in (0, 1) for fc in (0, 1)][1:]
    return [((1 - x) if fx else x, (1 - y) if fy else y, (1 - c) if fc else c) for fx, fy, fc in flips]


def _small_start(vec, after, name):
    n_after = len(after)

    def body(v_ref, slots_ref, *rest):
        send, recv = rest[n_after], rest[n_after + 1]
        token = rest[-1]
        x, y, c, _ = _place()
        me = 4 * x + 2 * y + c
        for k, peer in enumerate(_peers(x, y, c)):
            _remote(v_ref, slots_ref.at[me], send.at[k], recv.at[k], peer).start()
        token[...] = jnp.zeros_like(token)

    slots = jax.ShapeDtypeStruct((N_DEV,) + vec.shape, vec.dtype)
    res = pl.pallas_call(
        body, name=name,
        out_shape=(pltpu.SemaphoreType.DMA((N_DEV - 1,)), pltpu.SemaphoreType.DMA((N_DEV - 1,)),
                   pltpu.HBM(vec.shape, vec.dtype), pltpu.HBM(slots.shape, slots.dtype),
                   jax.ShapeDtypeStruct((8, LANES), F32)),
        in_specs=[HBM_ONLY, HBM_ONLY] + [ANY_SPEC] * n_after,
        out_specs=(SEM_SPEC, SEM_SPEC, HBM_ONLY, HBM_ONLY, pl.BlockSpec(memory_space=pltpu.VMEM)),
        input_output_aliases={0: 2, 1: 3},
        compiler_params=pltpu.CompilerParams(has_side_effects=DATAFLOW),
    )(_in_hbm(vec), _in_hbm(lax.empty(slots.shape, slots.dtype)), *after)
    return res


def _small_wait(send, recv, vec, slots, after, name):
    def body(v_ref, slots_ref, send_ref, recv_ref, *rest):
        x, y, c, _ = _place()
        for k, (px, py, pc) in enumerate(_peers(x, y, c)):
            cp = _remote(v_ref, slots_ref.at[4 * px + 2 * py + pc], send_ref.at[k], recv_ref.at[k], (px, py, pc))
            cp.wait_send()
            cp.wait_recv()

    return pl.pallas_call(
        body, name=name, out_shape=(pltpu.HBM(vec.shape, vec.dtype), pltpu.HBM(slots.shape, slots.dtype)),
        in_specs=[HBM_ONLY, HBM_ONLY, SEM_SPEC, SEM_SPEC] + [ANY_SPEC] * len(after), out_specs=[HBM_ONLY, HBM_ONLY],
        input_output_aliases={0: 0, 1: 1},
        compiler_params=pltpu.CompilerParams(has_side_effects=DATAFLOW),
    )(vec, slots, send, recv, *after)


def _small_sum(vec, slots, name):
    rows = vec.shape[0]
    blk = min(rows, 256)
    x, y, c = lax.axis_index("x"), lax.axis_index("y"), lax.axis_index("c")
    me = (4 * x + 2 * y + c).astype(jnp.int32).reshape(1)

    def slot_spec(k):
        return pl.BlockSpec((None, blk, LANES), lambda i, w: (jnp.where(w[0] == k, (k + 1) % N_DEV, k), i, 0))

    def body(w_ref, v_ref, *rest):
        o_ref = rest[-1]
        acc = None
        for k in range(N_DEV):
            term = jnp.where(w_ref[0] == k, v_ref[...], rest[k][...])
            acc = term if acc is None else acc + term
        o_ref[...] = acc

    return pl.pallas_call(
        body, name=name,
        grid_spec=pltpu.PrefetchScalarGridSpec(
            num_scalar_prefetch=1, grid=(rows // blk,),
            in_specs=[pl.BlockSpec((blk, LANES), lambda i, w: (i, 0))] + [slot_spec(k) for k in range(N_DEV)],
            out_specs=pl.BlockSpec((blk, LANES), lambda i, w: (i, 0))),
        out_shape=jax.ShapeDtypeStruct(vec.shape, F32), compiler_params=_params(("arbitrary",), 32),
    )(me, vec, *([slots] * N_DEV))


def _half(ref, slot, c):
    half = ref.shape[1] // 2
    return ref.at[slot, pl.ds(pl.multiple_of(c * half, 8), half)]


def _gather_start(bufs, after, name):
    n = len(bufs)
    n_after = len(after)

    def body(*refs):
        ins = refs[:n]
        send, recv = refs[n + n_after], refs[n + n_after + 1]
        token = refs[-1]
        x, y, c, chips = _place()
        me = _shard_index(x, y)
        for t in range(n):
            for j, (px, py) in enumerate(chips):
                mine = _half(ins[t], me, c)
                _remote(mine, mine, send.at[t * 3 + j], recv.at[t * 3 + j], (px, py, c)).start()
        token[...] = jnp.zeros_like(token)

    thru = [pltpu.HBM(b.shape, b.dtype) for b in bufs]
    res = pl.pallas_call(
        body, name=name,
        out_shape=(pltpu.SemaphoreType.DMA((n * 3,)), pltpu.SemaphoreType.DMA((n * 3,)), *thru,
                   jax.ShapeDtypeStruct((8, LANES), F32)),
        in_specs=[HBM_ONLY] * n + [ANY_SPEC] * n_after,
        out_specs=(SEM_SPEC, SEM_SPEC, *([HBM_ONLY] * n), pl.BlockSpec(memory_space=pltpu.VMEM)),
        input_output_aliases={i: 2 + i for i in range(n)},
        compiler_params=pltpu.CompilerParams(has_side_effects=DATAFLOW),
    )(*[_in_hbm(b) for b in bufs], *after)
    return res[0], res[1], list(res[2:2 + n]), res[-1]


def _gather_wait(send, recv, bufs, after, name):
    n = len(bufs)

    def body(*refs):
        ins = refs[:n]
        send_ref, recv_ref = refs[n], refs[n + 1]
        x, y, c, chips = _place()
        me = _shard_index(x, y)
        for t in range(n):
            for j, (px, py) in enumerate(chips):
                cp = _remote(_half(ins[t], me, c), _half(ins[t], _shard_index(px, py), c), send_ref.at[t * 3 + j],
                             recv_ref.at[t * 3 + j], (px, py, c))
                cp.wait_send()
                cp.wait_recv()

    res = pl.pallas_call(
        body, name=name, out_shape=tuple(pltpu.HBM(b.shape, b.dtype) for b in bufs),
        in_specs=[HBM_ONLY] * n + [SEM_SPEC, SEM_SPEC] + [ANY_SPEC] * len(after), out_specs=[HBM_ONLY] * n,
        input_output_aliases={i: i for i in range(n)},
        compiler_params=pltpu.CompilerParams(has_side_effects=DATAFLOW),
    )(*bufs, send, recv, *after)
    return list(res)


def _gather_forward(bufs, name):
    n = len(bufs)

    def body(*refs):
        outs = refs[n:2 * n]
        send, recv = refs[2 * n:]
        x, y, c, chips = _place()
        sibling = (x, y, 1 - c)
        cps = []
        for t in range(n):
            for j, (px, py) in enumerate(chips):
                got = _half(outs[t], _shard_index(px, py), c)
                cp = _remote(got, got, send.at[t * 3 + j], recv.at[t * 3 + j], sibling)
                cp.start()
                cps.append(cp)
        for t in range(n):
            for j, (px, py) in enumerate(chips):
                theirs = _half(outs[t], _shard_index(px, py), 1 - c)
                _remote(theirs, theirs, send.at[t * 3 + j], recv.at[t * 3 + j], sibling).wait_recv()
        for cp in cps:
            cp.wait_send()

    return pl.pallas_call(
        body, name=name, in_specs=[HBM_SPEC] * n, out_specs=[HBM_SPEC] * n,
        out_shape=[jax.ShapeDtypeStruct(b.shape, b.dtype) for b in bufs],
        input_output_aliases={t: t for t in range(n)},
        scratch_shapes=[pltpu.SemaphoreType.DMA((n * 3,))] * 2,
    )(*bufs)


def _gather_forward_start(bufs, carry, name):
    n = len(bufs)

    def body(*refs):
        ins = refs[:n]
        send, recv = refs[n + 1], refs[n + 2]
        x, y, c, chips = _place()
        for t in range(n):
            for j, (px, py) in enumerate(chips):
                got = _half(ins[t], _shard_index(px, py), c)
                _remote(got, got, send.at[t * 3 + j], recv.at[t * 3 + j], (x, y, 1 - c)).start()

    res = pl.pallas_call(
        body, name=name,
        out_shape=(pltpu.SemaphoreType.DMA((n * 3,)), pltpu.SemaphoreType.DMA((n * 3,)),
                   *[pltpu.HBM(b.shape, b.dtype) for b in bufs], pltpu.HBM(carry.shape, carry.dtype)),
        in_specs=[HBM_ONLY] * (n + 1),
        out_specs=(SEM_SPEC, SEM_SPEC, *([HBM_ONLY] * (n + 1))),
        input_output_aliases={i: 2 + i for i in range(n + 1)},
        compiler_params=pltpu.CompilerParams(has_side_effects=DATAFLOW),
    )(*[_in_hbm(b) for b in bufs], _in_hbm(carry))
    return res[0], res[1], list(res[2:2 + n]), res[-1]


def _gather_forward_wait(send, recv, bufs, after, name):
    n = len(bufs)

    def body(*refs):
        ins = refs[:n]
        send_ref, recv_ref = refs[n], refs[n + 1]
        x, y, c, chips = _place()
        for t in range(n):
            for j, (px, py) in enumerate(chips):
                s = _shard_index(px, py)
                cp = _remote(_half(ins[t], s, c), _half(ins[t], s, 1 - c), send_ref.at[t * 3 + j],
                             recv_ref.at[t * 3 + j], (x, y, 1 - c))
                cp.wait_send()
                cp.wait_recv()

    res = pl.pallas_call(
        body, name=name, out_shape=tuple(pltpu.HBM(b.shape, b.dtype) for b in bufs),
        in_specs=[HBM_ONLY] * n + [SEM_SPEC, SEM_SPEC] + [ANY_SPEC] * len(after), out_specs=[HBM_ONLY] * n,
        input_output_aliases={i: i for i in range(n)},
        compiler_params=pltpu.CompilerParams(has_side_effects=DATAFLOW),
    )(*bufs, send, recv, *after)
    return list(res)


def _sibling_rows(ref, c):
    half = ref.shape[1] // 2
    return ref.at[:, pl.ds(pl.multiple_of((1 - c) * half, 8), half)]


def _half_exchange_start(grads, name):
    n = len(grads)

    def body(*refs):
        ins, lands = refs[:n], refs[n:2 * n]
        send, recv = refs[2 * n], refs[2 * n + 1]
        token = refs[-1]
        x, y, c, _ = _place()
        for t in range(n):
            _remote(_sibling_rows(ins[t], c), lands[t], send.at[t], recv.at[t], (x, y, 1 - c)).start()
        token[...] = jnp.zeros_like(token)

    halves = [jax.ShapeDtypeStruct((g.shape[0], g.shape[1] // 2, g.shape[2]), g.dtype) for g in grads]
    res = pl.pallas_call(
        body, name=name,
        out_shape=(pltpu.SemaphoreType.DMA((n,)), pltpu.SemaphoreType.DMA((n,)),
                   *[pltpu.HBM(g.shape, g.dtype) for g in grads], *[pltpu.HBM(h.shape, h.dtype) for h in halves],
                   jax.ShapeDtypeStruct((8, LANES), F32)),
        in_specs=[HBM_ONLY] * (2 * n),
        out_specs=(SEM_SPEC, SEM_SPEC, *([HBM_ONLY] * (2 * n)), pl.BlockSpec(memory_space=pltpu.VMEM)),
        input_output_aliases={i: 2 + i for i in range(2 * n)},
        compiler_params=pltpu.CompilerParams(has_side_effects=DATAFLOW),
    )(*[_in_hbm(g) for g in grads], *[_in_hbm(lax.empty(h.shape, h.dtype)) for h in halves])
    return res[0], res[1], list(res[2:2 + n]), list(res[2 + n:2 + 2 * n]), res[-1]


def _half_exchange_wait(send, recv, grads, lands, after, name):
    n = len(grads)

    def body(*refs):
        ins, got = refs[:n], refs[n:2 * n]
        send_ref, recv_ref = refs[2 * n], refs[2 * n + 1]
        x, y, c, _ = _place()
        for t in range(n):
            cp = _remote(_sibling_rows(ins[t], c), got[t], send_ref.at[t], recv_ref.at[t], (x, y, 1 - c))
            cp.wait_send()
            cp.wait_recv()

    res = pl.pallas_call(
        body, name=name,
        out_shape=(*[pltpu.HBM(g.shape, g.dtype) for g in grads], *[pltpu.HBM(h.shape, h.dtype) for h in lands]),
        in_specs=[HBM_ONLY] * (2 * n) + [SEM_SPEC, SEM_SPEC] + [ANY_SPEC] * len(after),
        out_specs=[HBM_ONLY] * (2 * n),
        input_output_aliases={i: i for i in range(2 * n)},
        compiler_params=pltpu.CompilerParams(has_side_effects=DATAFLOW),
    )(*grads, *lands, send, recv, *after)
    return list(res[:n]), list(res[n:])


def _half_rows(half, c_dim):
    for cand in (512, 256, 128, 64):
        if half % cand == 0 and cand * c_dim * 2 <= 2 * MIB:
            return cand
    raise ValueError((half, c_dim))


def _core_index():
    return lax.axis_index("c").astype(jnp.int32).reshape(1)


def _half_sum(own, other, name):
    s, r, c_dim = own.shape
    rows = _half_rows(r // 2, c_dim)
    per = r // 2 // rows

    def body(c_ref, a_ref, b_ref, o_ref):
        o_ref[...] = (a_ref[...].astype(F32) + b_ref[...].astype(F32)).astype(BF16)

    return pl.pallas_call(
        body, name=name,
        grid_spec=pltpu.PrefetchScalarGridSpec(
            num_scalar_prefetch=1, grid=(s, per),
            in_specs=[pl.BlockSpec((None, rows, c_dim), lambda k, i, c: (k, c[0] * per + i, 0)),
                      pl.BlockSpec((None, rows, c_dim), lambda k, i, c: (k, i, 0))],
            out_specs=pl.BlockSpec((None, rows, c_dim), lambda k, i, c: (k, i, 0))),
        out_shape=pltpu.HBM((s, r // 2, c_dim), BF16), compiler_params=_params(("arbitrary", "arbitrary"), 32),
    )(_core_index(), own, other)


def _reduce_start(pairs, name):
    n = len(pairs)

    def body(*refs):
        ins, lands = refs[:n], refs[n:2 * n]
        send, recv = refs[2 * n], refs[2 * n + 1]
        token = refs[-1]
        x, y, c, chips = _place()
        me = _shard_index(x, y)
        for t in range(n):
            for j, (px, py) in enumerate(chips):
                _remote(ins[t].at[_shard_index(px, py)], lands[t].at[me], send.at[t * 3 + j], recv.at[t * 3 + j],
                        (px, py, c)).start()
        token[...] = jnp.zeros_like(token)

    thru = [pltpu.HBM(b.shape, b.dtype) for b in pairs]
    res = pl.pallas_call(
        body, name=name,
        out_shape=(pltpu.SemaphoreType.DMA((n * 3,)), pltpu.SemaphoreType.DMA((n * 3,)), *thru, *thru,
                   jax.ShapeDtypeStruct((8, LANES), F32)),
        in_specs=[HBM_ONLY] * (2 * n),
        out_specs=(SEM_SPEC, SEM_SPEC, *([HBM_ONLY] * (2 * n)), pl.BlockSpec(memory_space=pltpu.VMEM)),
        input_output_aliases={i: 2 + i for i in range(2 * n)},
        compiler_params=pltpu.CompilerParams(has_side_effects=DATAFLOW),
    )(*[_in_hbm(b) for b in pairs], *[_in_hbm(lax.empty(b.shape, b.dtype)) for b in pairs])
    return res[0], res[1], list(res[2:2 + n]), list(res[2 + n:2 + 2 * n]), res[-1]


def _reduce_wait(send, recv, pairs, lands, after, name):
    n = len(pairs)

    def body(*refs):
        ins, got = refs[:n], refs[n:2 * n]
        send_ref, recv_ref = refs[2 * n], refs[2 * n + 1]
        x, y, c, chips = _place()
        for t in range(n):
            for j, (px, py) in enumerate(chips):
                s = _shard_index(px, py)
                cp = _remote(ins[t].at[s], got[t].at[s], send_ref.at[t * 3 + j], recv_ref.at[t * 3 + j], (px, py, c))
                cp.wait_send()
                cp.wait_recv()

    thru = [pltpu.HBM(b.shape, b.dtype) for b in pairs]
    res = pl.pallas_call(
        body, name=name, out_shape=(*thru, *thru),
        in_specs=[HBM_ONLY] * (2 * n) + [SEM_SPEC, SEM_SPEC] + [ANY_SPEC] * len(after),
        out_specs=[HBM_ONLY] * (2 * n),
        input_output_aliases={i: i for i in range(2 * n)},
        compiler_params=pltpu.CompilerParams(has_side_effects=DATAFLOW),
    )(*pairs, *lands, send, recv, *after)
    return list(res[:n]), list(res[n:])


def _reduce_sum(pair, landed, name):
    s, half, c_dim = pair.shape
    rows = _half_rows(half, c_dim)
    per = half // rows
    shard = _shard_index(lax.axis_index("x"), lax.axis_index("y"))
    where = jnp.stack([shard, lax.axis_index("c")]).astype(jnp.int32)

    def landed_spec(k):
        return pl.BlockSpec((None, rows, c_dim), lambda i, w: (jnp.where(w[0] == k, (k + 1) % s, k), i, 0))

    def body(w_ref, own_ref, *rest):
        o_ref = rest[-1]
        acc = None
        for k in range(s):
            term = jnp.where(w_ref[0] == k, own_ref[...], rest[k][...]).astype(F32)
            acc = term if acc is None else acc + term
        o_ref[...] = acc

    return pl.pallas_call(
        body, name=name,
        grid_spec=pltpu.PrefetchScalarGridSpec(
            num_scalar_prefetch=1, grid=(per,),
            in_specs=[pl.BlockSpec((None, rows, c_dim), lambda i, w: (w[0], i, 0))] + [landed_spec(k) for k in range(s)],
            out_specs=pl.BlockSpec((rows, c_dim), lambda i, w: (w[1] * per + i, 0))),
        out_shape=pltpu.HBM((2 * half, c_dim), F32), compiler_params=_params(("arbitrary",), 40),
    )(where, pair, *([landed] * s))


def _my_rows(ref, c):
    half = ref.shape[0] // 2
    return ref.at[pl.ds(pl.multiple_of(c * half, 8), half)]


def _half_gather_start(bufs, name):
    n = len(bufs)

    def body(*refs):
        ins = refs[:n]
        send, recv = refs[n], refs[n + 1]
        token = refs[-1]
        x, y, c, _ = _place()
        for t in range(n):
            mine = _my_rows(ins[t], c)
            _remote(mine, mine, send.at[t], recv.at[t], (x, y, 1 - c)).start()
        token[...] = jnp.zeros_like(token)

    res = pl.pallas_call(
        body, name=name,
        out_shape=(pltpu.SemaphoreType.DMA((n,)), pltpu.SemaphoreType.DMA((n,)),
                   *[pltpu.HBM(b.shape, b.dtype) for b in bufs], jax.ShapeDtypeStruct((8, LANES), F32)),
        in_specs=[HBM_ONLY] * n,
        out_specs=(SEM_SPEC, SEM_SPEC, *([HBM_ONLY] * n), pl.BlockSpec(memory_space=pltpu.VMEM)),
        input_output_aliases={i: 2 + i for i in range(n)},
        compiler_params=pltpu.CompilerParams(has_side_effects=DATAFLOW),
    )(*[_in_hbm(b) for b in bufs])
    return res[0], res[1], list(res[2:2 + n]), res[-1]


def _half_gather_wait(send, recv, bufs, after, name):
    n = len(bufs)

    def body(*refs):
        ins = refs[:n]
        send_ref, recv_ref = refs[n], refs[n + 1]
        x, y, c, _ = _place()
        for t in range(n):
            cp = _remote(_my_rows(ins[t], c), _my_rows(ins[t], 1 - c), send_ref.at[t], recv_ref.at[t], (x, y, 1 - c))
            cp.wait_send()
            cp.wait_recv()

    res = pl.pallas_call(
        body, name=name, out_shape=tuple(pltpu.HBM(b.shape, b.dtype) for b in bufs),
        in_specs=[HBM_ONLY] * n + [SEM_SPEC, SEM_SPEC] + [ANY_SPEC] * len(after), out_specs=[HBM_ONLY] * n,
        input_output_aliases={i: i for i in range(n)},
        compiler_params=pltpu.CompilerParams(has_side_effects=DATAFLOW),
    )(*bufs, send, recv, *after)
    return list(res)


WEIGHT_NAMES = ("ln_attn", "w_in", "sink_b", "rpb_c", "mix_gain", "w_out", "ln_ffn", "w_up", "conv_w", "conv_b",
                "w_down", "ln_final")
BIG_NAMES = ("w_in", "w_out", "w_up", "w_down")
REPLICATED_NAMES = ("ln_attn", "sink_b", "rpb_c", "mix_gain", "ln_ffn", "conv_b", "ln_final")
PACK_TILE = 8 * LANES


def _pack(arrays, row_multiple):
    pieces = []
    for a in arrays:
        flat = a.reshape(-1)
        pieces.append(jnp.pad(flat, (0, (-flat.shape[0]) % PACK_TILE)))
    flat = jnp.concatenate(pieces)
    flat = jnp.pad(flat, (0, (-flat.shape[0]) % (row_multiple * LANES)))
    return flat.reshape(-1, LANES)


def _unpack(packed, shapes):
    flat = packed.reshape(-1)
    out, off = [], 0
    for shape in shapes:
        size = math.prod(shape)
        out.append(flat[off:off + size].reshape(shape))
        off += size + (-size) % PACK_TILE
    return out


def kernel(x, ln_attn, w_in, sink_b, rpb_c, mix_gain, w_out, ln_ffn, w_up, conv_w, conv_b, w_down, ln_final, loss_target, m_ln_attn, m_w_in, m_sink_b, m_rpb_c, m_mix_gain, m_w_out, m_ln_ffn, m_w_up, m_conv_w, m_conv_b, m_w_down, m_ln_final, v_ln_attn, v_w_in, v_sink_b, v_rpb_c, v_mix_gain, v_w_out, v_ln_ffn, v_w_up, v_conv_w, v_conv_b, v_w_down, v_ln_final):
    w = dict(ln_attn=ln_attn, w_in=w_in, sink_b=sink_b, rpb_c=rpb_c, mix_gain=mix_gain, w_out=w_out, ln_ffn=ln_ffn,
             w_up=w_up, conv_w=conv_w, conv_b=conv_b, w_down=w_down, ln_final=ln_final)
    m = dict(ln_attn=m_ln_attn, w_in=m_w_in, sink_b=m_sink_b, rpb_c=m_rpb_c, mix_gain=m_mix_gain, w_out=m_w_out,
             ln_ffn=m_ln_ffn, w_up=m_w_up, conv_w=m_conv_w, conv_b=m_conv_b, w_down=m_w_down, ln_final=m_ln_final)
    v = dict(ln_attn=v_ln_attn, w_in=v_w_in, sink_b=v_sink_b, rpb_c=v_rpb_c, mix_gain=v_mix_gain, w_out=v_w_out,
             ln_ffn=v_ln_ffn, w_up=v_w_up, conv_w=v_conv_w, conv_b=v_conv_b, w_down=v_w_down, ln_final=v_ln_final)
    shard = _shard_index(lax.axis_index("x"), lax.axis_index("y"))
    up_cols = w_up.shape[2]

    conv_slots = _all_gather_small(_pack([conv_w], 8), "gather_conv_w")
    conv_all = conv_slots[0::2].reshape(N_SHARDS, -1)[:, :conv_w.size].reshape((N_SHARDS,) + conv_w.shape)

    arrivals = []
    group_of = {}
    tokens = []
    rest = ("w_out", "w_up", "w_down")
    for l, names in ((0, ("w_in",)), (0, rest), (1, ("w_in",)), (1, rest)):
        bufs = [_own_slot(w[k], l, shard, "own_" + k) for k in names]
        send, recv, bufs, token = _gather_start(bufs, tokens[-1:] or [conv_slots], "gather_start_%d" % len(arrivals))
        tokens.append(token)
        for k in names:
            group_of[l, k] = len(arrivals)
        arrivals.append({"names": names, "send": send, "recv": recv, "bufs": bufs, "done": None})

    def gathered(l, name, after):
        idx = group_of[l, name]
        group = arrivals[idx]

        def whole(k, buf):
            return buf.reshape(1, -1, buf.shape[2]) if k in ("w_out", "w_down") else buf

        if group["done"] is None:
            follow = list(after) + tokens[-1:]
            if idx == 0:
                follow += [tabs[k] for k in ("cos", "sin", "bias_a", "bias_b")]
                follow += [p[k] for p in layers for k in ("bias_c", "conv_w")]
            bufs = _gather_wait(group["send"], group["recv"], group["bufs"], follow, "gather_wait_%d" % idx)
            first = _gather_forward(bufs[:1], "gather_forward_%d" % idx)[0]
            if len(bufs) > 1:
                send, recv, rest, first = _gather_forward_start(bufs[1:], first, "gather_forward_start_%d" % idx)
                group["rest"] = (send, recv, rest)
            group["done"] = {group["names"][0]: whole(group["names"][0], first)}
        if name not in group["done"]:
            send, recv, rest = group["rest"]
            rest = _gather_forward_wait(send, recv, rest, list(after), "gather_forward_wait_%d" % idx)
            group["done"].update({k: whole(k, buf) for k, buf in zip(group["names"][1:], rest)})
        return group["done"][name]

    cos, sin = _rope_tables(SEQ)
    tabs = {"cos": cos, "sin": sin, "bias_a": _bias_a(), "bias_b": _bias_b()}
    layers = []
    for l in range(DEPTH):
        conv_w_l = conv_all[:, l].reshape(2, N_SHARDS // 2, 3, up_cols).transpose(0, 2, 1, 3).reshape(2, 3, D_FF)
        layers.append({"ln_attn": ln_attn[l][None], "sink_b": sink_b[l], "bias_c": _bias_c(rpb_c[l]),
                       "mix_gain": mix_gain[l][None], "ln_ffn": ln_ffn[l][None], "conv_w": conv_w_l,
                       "conv_b": conv_b[l].reshape(2, 1, D_FF)})

    act = x[0]
    saved = []
    for l in range(DEPTH):
        act, keep = _layer_fwd(act, layers[l], lambda name, after, l=l: gathered(l, name, [after]), tabs)
        saved.append(keep)
    loss_part, dx, dx_b, d_ln_final = _loss_head(act, ln_final[None], loss_target[0], "loss_head")
    loss = lax.psum(loss_part[0, 0], ("x", "y", "c"))

    reductions = []

    opened = [0]

    def begin(l, partial):
        idx = opened[0]
        opened[0] += 1
        names = tuple(partial)
        send_sem, recv_sem, mine, theirs, token = _half_exchange_start([partial[k] for k in names],
                                                                       "half_exchange_start_%d" % idx)
        return {"idx": idx, "layer": l, "names": names, "send": send_sem, "recv": recv_sem, "mine": mine,
                "theirs": theirs}, token

    def finish(handle, after):
        idx, names = handle["idx"], handle["names"]
        mine, theirs = _half_exchange_wait(handle["send"], handle["recv"], handle["mine"], handle["theirs"], after,
                                           "half_exchange_wait_%d" % idx)
        pairs = [_half_sum(a, b, "half_sum_" + k) for k, a, b in zip(names, mine, theirs)]
        send_sem, recv_sem, pairs, lands, token = _reduce_start(pairs, "reduce_start_%d" % idx)
        reductions.append({"layer": handle["layer"], "names": names, "send": send_sem, "recv": recv_sem,
                           "pairs": pairs, "lands": lands})
        return token

    small = [None] * DEPTH
    pending = None
    for l in reversed(range(DEPTH)):
        big = {k: gathered(l, k, []) for k in BIG_NAMES}
        dx, dx_b, small[l], pending = _layer_bwd(dx, dx_b, saved[l], layers[l], big, tabs,
                                                 functools.partial(begin, l), finish, pending)
    after = [finish(pending[0], [pending[1]])]

    stacked = {k: jnp.stack([small[l][k] for l in range(DEPTH)]) for k in small[0]}
    part = {"ln_attn": stacked["ln_attn"][:, 0], "sink_b": stacked["sink_b"], "rpb_c": stacked["rpb_c"],
            "mix_gain": stacked["mix_gain"][:, 0], "ln_ffn": stacked["ln_ffn"][:, 0],
            "conv_b": stacked["conv_b"].reshape(DEPTH, 2 * D_FF), "ln_final": d_ln_final[0],
            "conv_w": stacked["conv_w"].transpose(0, 2, 1, 3).reshape(DEPTH, 3, 2 * D_FF)}
    small_names = REPLICATED_NAMES + ("conv_w",)
    small_send, small_recv, small_vec, small_slots, token = _small_start(
        _pack([part[k] for k in small_names], 256), after, "small_grads_start")
    after = [token]

    grads, delta, new_m, new_v = {}, {}, {}, {}
    updated = dict.fromkeys(BIG_NAMES)

    def arrive(idx, after):
        group = reductions[idx]
        pairs, lands = _reduce_wait(group["send"], group["recv"], group["pairs"], group["lands"], after,
                                    "reduce_wait_%d" % idx)
        halves = [_reduce_sum(pair, landed, "reduce_sum_" + k) for k, pair, landed in zip(group["names"], pairs, lands)]
        send_sem, recv_sem, halves, token = _half_gather_start(halves, "half_gather_start_%d" % idx)
        return {"idx": idx, "send": send_sem, "recv": recv_sem, "bufs": halves, "names": group["names"],
                "layer": group["layer"]}, [token]

    def update(swap, after):
        whole = _half_gather_wait(swap["send"], swap["recv"], swap["bufs"], after,
                                  "half_gather_wait_%d" % swap["idx"])
        for k, g in zip(swap["names"], whole):
            updated[k] = _adamw_layer(w[k], g, m[k], v[k], swap["layer"], updated[k], "adamw_" + k)
        return [updated[k][0] for k in swap["names"]]

    swaps = []
    for idx in range(len(reductions) - 1):
        swap, after = arrive(idx, after)
        swaps.append(swap)
    for swap in swaps[:2]:
        after = update(swap, after)
    swap, after = arrive(len(reductions) - 1, after)
    for swap in swaps[2:] + [swap]:
        after = update(swap, after)
    for k in BIG_NAMES:
        grads[k], delta[k], new_m[k], new_v[k] = updated[k]

    small_vec, small_slots = _small_wait(small_send, small_recv, small_vec, small_slots, after, "small_grads_wait")
    total = _small_sum(small_vec, small_slots, "small_grads_sum")
    for k, g in zip(small_names, _unpack(total, [part[k].shape for k in small_names])):
        grads[k] = g
    grads["conv_w"] = lax.dynamic_slice_in_dim(grads["conv_w"], shard * up_cols, up_cols, axis=2)

    flat = (DEPTH * 3, up_cols)
    res = _adamw(conv_w.reshape(flat), grads["conv_w"].reshape(flat), m["conv_w"].reshape(flat),
                 v["conv_w"].reshape(flat), "adamw_conv_w")
    delta["conv_w"], new_m["conv_w"], new_v["conv_w"] = (r.reshape(conv_w.shape) for r in res)
    shapes = [w[k].shape for k in REPLICATED_NAMES]
    packed = [_pack([d[k] for k in REPLICATED_NAMES], 128) for d in (w, grads, m, v)]
    for d, res in zip((delta, new_m, new_v), _adamw(*packed, "adamw_small")):
        for k, r in zip(REPLICATED_NAMES, _unpack(res, shapes)):
            d[k] = r

    return (loss, dx[None], *[grads[k] for k in WEIGHT_NAMES], *[delta[k] for k in WEIGHT_NAMES],
            *[new_m[k] for k in WEIGHT_NAMES], *[new_v[k] for k in WEIGHT_NAMES])
```

```python
import functools
import math

import jax
import jax.numpy as jnp
from jax import lax
from jax.experimental import pallas as pl
from jax.experimental.pallas import tpu as pltpu

F32 = jnp.float32
BF16 = jnp.bfloat16
MESH = pl.DeviceIdType.MESH

D_MODEL = 2048
SEQ = 2048
DEPTH = 2
HEAD_DIM = 64
N_HEADS_A = 12
N_HEADS_B = 10
N_KV_B = 2
N_HEADS_C = 10
WINDOW_B = 128
GRID_W = 64
NA_ROWS = 8
NA_COLS = 16
WIDTH_A = N_HEADS_A * HEAD_DIM
WIDTH_B = N_HEADS_B * HEAD_DIM
WIDTH_C = N_HEADS_C * HEAD_DIM
IN_COLS = 5120
D_FF = 5632
ROPE_THETA = 10000.0
EPS = 1e-6
NEG_INF = -1e30
N_SHARDS = 4

ADAM_LR = 0.001
ADAM_B1 = 0.9
ADAM_B2 = 0.999
ADAM_EPS = 1e-08
ADAM_WD = 0.01
ADAM_STEP = 10

LANES = 128
QB = 256
NQB = SEQ // QB
ROWS = 256
MIB = 2 ** 20

A_BLK = (0, 6, 12)
B_BLK = (18, 23, 24)
C_BLK = (25, 30, 35)
ROPE_BLKS = tuple(range(0, 12)) + tuple(range(18, 24))
QSCALE_BLKS = tuple(range(0, 6)) + tuple(range(18, 23)) + tuple(range(25, 30))
N_PBLK = IN_COLS // LANES


def _params(sem, vmem_mib):
    return pltpu.CompilerParams(dimension_semantics=sem, vmem_limit_bytes=vmem_mib * MIB)


def _weight_spec(w, cols, t_in, t_out, transposed):
    s, r, c = w.shape
    if cols:
        per = c // t_out
        k_dim, n = r, s * c
        if transposed:
            index = lambda j, rr: (rr // per, j, rr % per)
        else:
            index = lambda j, kk: (j // per, kk, j % per)
    else:
        per = r // t_in
        k_dim, n = s * r, c
        if transposed:
            index = lambda j, rr: (j // per, j % per, rr)
        else:
            index = lambda j, kk: (kk // per, kk % per, j)
    return pl.BlockSpec((None, t_in, t_out), index), k_dim, n


def _mm_nn(a, w, *, cols, tn, tk, out_dtype, name, residual=None, out_split=1):
    m, k_dim = a.shape
    w_spec, k_w, n = _weight_spec(w, cols, tk, tn, False)
    assert k_w == k_dim
    nj, nk = n // tn, k_dim // tk
    in_specs = [pl.BlockSpec((m, tk), lambda j, k: (0, k)), w_spec]
    args = [a, w]
    if residual is not None:
        in_specs.append(pl.BlockSpec((m, tn), lambda j, k: (0, j)))
        args.append(residual)
    if out_split > 1:
        per_o = n // out_split // tn
        out_spec = pl.BlockSpec((None, m, tn), lambda j, k: (j // per_o, 0, j % per_o))
        out_shape = pltpu.HBM((out_split, m, n // out_split), out_dtype)
    else:
        out_spec = pl.BlockSpec((m, tn), lambda j, k: (0, j))
        out_shape = pltpu.HBM((m, n), out_dtype)

    def body(*refs):
        a_ref, w_ref = refs[0], refs[1]
        r_ref = refs[2] if residual is not None else None
        o_ref = refs[3] if residual is not None else refs[2]

        def finish(val):
            if r_ref is not None:
                val = r_ref[...] + val
            o_ref[...] = val.astype(o_ref.dtype)

        part = jnp.dot(a_ref[...], w_ref[...], preferred_element_type=F32)
        if nk == 1:
            finish(part)
        else:
            acc = refs[-1]
            kk = pl.program_id(1)

            @pl.when(kk == 0)
            def _():
                acc[...] = part

            @pl.when(kk > 0)
            def _():
                acc[...] += part

            @pl.when(kk == nk - 1)
            def _():
                finish(acc[...])

    return pl.pallas_call(
        body, name=name, grid=(nj, nk), in_specs=in_specs, out_specs=out_spec, out_shape=out_shape,
        scratch_shapes=[pltpu.VMEM((m, tn), F32)] if nk > 1 else [],
        compiler_params=_params(("arbitrary", "arbitrary"), 56),
    )(*[_in_hbm(a) for a in args])


ANY_SPEC = pl.BlockSpec(memory_space=pl.ANY)


def _mm_nt(dy, w, *, cols, to, tr, out_dtype, name, after=()):
    if dy.ndim == 3:
        m = dy.shape[1]
        n = dy.shape[0] * dy.shape[2]
        per_d = dy.shape[2] // tr
        dy_spec = pl.BlockSpec((None, m, tr), lambda j, r: (r // per_d, 0, r % per_d))
    else:
        m, n = dy.shape
        dy_spec = pl.BlockSpec((m, tr), lambda j, r: (0, r))
    w_spec, k_dim, n_w = _weight_spec(w, cols, to, tr, True)
    assert n_w == n
    nj, nr = k_dim // to, n // tr

    n_after = len(after)

    def body(dy_ref, w_ref, *rest):
        o_ref = rest[n_after]
        part = lax.dot_general(dy_ref[...], w_ref[...], (((1,), (1,)), ((), ())), preferred_element_type=F32)
        if nr == 1:
            o_ref[...] = part.astype(o_ref.dtype)
        else:
            acc = rest[n_after + 1]
            rr = pl.program_id(1)

            @pl.when(rr == 0)
            def _():
                acc[...] = part

            @pl.when(rr > 0)
            def _():
                acc[...] += part

            @pl.when(rr == nr - 1)
            def _():
                o_ref[...] = acc[...].astype(o_ref.dtype)

    return pl.pallas_call(
        body, name=name, grid=(nj, nr), in_specs=[dy_spec, w_spec] + [ANY_SPEC] * n_after,
        out_specs=pl.BlockSpec((m, to), lambda j, r: (0, j)),
        out_shape=pltpu.HBM((m, k_dim), out_dtype),
        scratch_shapes=[pltpu.VMEM((m, to), F32)] if nr > 1 else [],
        compiler_params=_params(("arbitrary", "arbitrary"), 56),
    )(_in_hbm(dy), _in_hbm(w), *after)


def _mm_tn(x, dy, *, tk, tn, shards, name):
    m, k_dim = x.shape
    if dy.ndim == 3:
        n = dy.shape[0] * dy.shape[2]
        per_d = dy.shape[2] // tn
        dy_spec = pl.BlockSpec((None, m, tn), lambda i, j: (j // per_d, 0, j % per_d))
    else:
        n = dy.shape[1]
        dy_spec = pl.BlockSpec((m, tn), lambda i, j: (0, j))
    if shards > 0:
        per = n // shards // tn
        out_shape = pltpu.HBM((shards, k_dim, n // shards), BF16)
        out_spec = pl.BlockSpec((None, tk, tn), lambda i, j: (j // per, i, j % per))
    else:
        s = -shards
        per = k_dim // s // tk
        out_shape = pltpu.HBM((s, k_dim // s, n), BF16)
        out_spec = pl.BlockSpec((None, tk, tn), lambda i, j: (i // per, i % per, j))

    def body(x_ref, dy_ref, o_ref):
        o_ref[...] = lax.dot_general(x_ref[...], dy_ref[...], (((0,), (0,)), ((), ())),
                                     preferred_element_type=F32).astype(BF16)

    return pl.pallas_call(
        body, name=name, grid=(k_dim // tk, n // tn),
        in_specs=[pl.BlockSpec((m, tk), lambda i, j: (0, i)), dy_spec], out_specs=out_spec, out_shape=out_shape,
        compiler_params=_params(("arbitrary", "arbitrary"), 56),
    )(_in_hbm(x), _in_hbm(dy))


def _row_spec(width, rows=ROWS):
    return pl.BlockSpec((rows, width), lambda i: (i, 0))


def _vec_spec(width):
    return pl.BlockSpec((1, width), lambda i: (0, 0))


def _rms_stats(x):
    r = lax.rsqrt(jnp.mean(x * x, axis=-1, keepdims=True) + EPS)
    return r, x * r


def _rmsnorm_fwd(x, gain, name):
    t, d = x.shape

    def body(x_ref, g_ref, o_ref):
        _, n = _rms_stats(x_ref[...])
        o_ref[...] = (n * g_ref[...]).astype(BF16)

    return pl.pallas_call(
        body, name=name, grid=(t // ROWS,), in_specs=[_row_spec(d), _vec_spec(d)], out_specs=_row_spec(d),
        out_shape=pltpu.HBM((t, d), BF16), compiler_params=_params(("arbitrary",), 32),
    )(_in_hbm(x), _in_hbm(gain))


def _rmsnorm_bwd(x, gain, dh, dres, name, after=()):
    t, d = x.shape
    n_after = len(after)

    def body(x_ref, g_ref, dh_ref, dres_ref, *rest):
        dx_ref, dxb_ref, dg_ref = rest[n_after:]
        r, n = _rms_stats(x_ref[...])
        dh_v = dh_ref[...]
        dn = dh_v * g_ref[...]
        dx = dres_ref[...] + r * (dn - n * jnp.mean(dn * n, axis=-1, keepdims=True))
        dx_ref[...] = dx
        dxb_ref[...] = dx.astype(BF16)
        part = jnp.sum(dh_v * n, axis=0, keepdims=True)

        @pl.when(pl.program_id(0) == 0)
        def _():
            dg_ref[...] = part

        @pl.when(pl.program_id(0) > 0)
        def _():
            dg_ref[...] += part

    return pl.pallas_call(
        body, name=name, grid=(t // ROWS,),
        in_specs=[_row_spec(d), _vec_spec(d), _row_spec(d), _row_spec(d)] + [ANY_SPEC] * n_after,
        out_specs=[_row_spec(d), _row_spec(d), _vec_spec(d)],
        out_shape=[pltpu.HBM((t, d), F32), pltpu.HBM((t, d), BF16), jax.ShapeDtypeStruct((1, d), F32)],
        compiler_params=_params(("arbitrary",), 40),
    )(_in_hbm(x), _in_hbm(gain), _in_hbm(dh), _in_hbm(dres), *after)


def _loss_head(x, gain, target, name):
    t, d = x.shape

    def body(x_ref, g_ref, t_ref, loss_ref, dx_ref, dxb_ref, dg_ref):
        r, n = _rms_stats(x_ref[...])
        g = g_ref[...]
        err = n * g - t_ref[...]
        dy = err * (1.0 / d)
        dn = dy * g
        dx = r * (dn - n * jnp.mean(dn * n, axis=-1, keepdims=True))
        dx_ref[...] = dx
        dxb_ref[...] = dx.astype(BF16)
        part = jnp.sum(dy * n, axis=0, keepdims=True)
        lpart = jnp.zeros((8, LANES), F32) + 0.5 * jnp.sum(jnp.mean(err * err, axis=-1, keepdims=True))

        @pl.when(pl.program_id(0) == 0)
        def _():
            dg_ref[...] = part
            loss_ref[...] = lpart

        @pl.when(pl.program_id(0) > 0)
        def _():
            dg_ref[...] += part
            loss_ref[...] += lpart

    return pl.pallas_call(
        body, name=name, grid=(t // ROWS,),
        in_specs=[_row_spec(d), _vec_spec(d), _row_spec(d)],
        out_specs=[pl.BlockSpec((8, LANES), lambda i: (0, 0)), _row_spec(d), _row_spec(d), _vec_spec(d)],
        out_shape=[jax.ShapeDtypeStruct((8, LANES), F32), pltpu.HBM((t, d), F32), pltpu.HBM((t, d), BF16),
                   jax.ShapeDtypeStruct((1, d), F32)],
        compiler_params=_params(("arbitrary",), 40),
    )(x, gain, target)


def _swap_halves(x):
    lane = lax.broadcasted_iota(jnp.int32, x.shape, 1)
    return jnp.where((lane % HEAD_DIM) < HEAD_DIM // 2, pltpu.roll(x, LANES - HEAD_DIM // 2, 1),
                     pltpu.roll(x, HEAD_DIM // 2, 1))


def _rope_tables(t):
    inv_freq = ROPE_THETA ** (-jnp.arange(0, HEAD_DIM, 2, dtype=F32) / HEAD_DIM)
    ang = jnp.arange(t, dtype=F32)[:, None] * inv_freq[None, :]
    cos = jnp.tile(jnp.cos(ang), (1, LANES // (HEAD_DIM // 2)))
    sin = jnp.tile(jnp.sin(ang), (1, LANES // (HEAD_DIM // 2)))
    lane = jnp.arange(LANES)[None, :]
    return cos, jnp.where((lane % HEAD_DIM) < HEAD_DIM // 2, -sin, sin)


def _rope_fwd(proj, cos, sin, name):
    t = proj.shape[0]
    scale = HEAD_DIM ** -0.5

    def body(p_ref, c_ref, s_ref, o_ref):
        cos_v, sin_v = c_ref[...], s_ref[...]
        for b in range(N_PBLK):
            cols = slice(b * LANES, (b + 1) * LANES)
            v = p_ref[:, cols]
            if b in ROPE_BLKS:
                v = v * cos_v + _swap_halves(v) * sin_v
            if b in QSCALE_BLKS:
                v = v * scale
            o_ref[:, cols] = v.astype(BF16)

    return pl.pallas_call(
        body, name=name, grid=(t // ROWS,),
        in_specs=[_row_spec(IN_COLS), _row_spec(LANES), _row_spec(LANES)], out_specs=_row_spec(IN_COLS),
        out_shape=pltpu.HBM((t, IN_COLS), BF16), compiler_params=_params(("arbitrary",), 40),
    )(_in_hbm(proj), _in_hbm(cos), _in_hbm(sin))


def _rope_bwd(grads, cos, sin, name):
    t = grads[0].shape[0]
    scale = HEAD_DIM ** -0.5
    group = N_HEADS_B // N_KV_B

    def body(*refs):
        c_ref, s_ref, o_ref = refs[9], refs[10], refs[11]
        cos_v, sin_v = c_ref[...], s_ref[...]

        def kv_sum(ref):
            parts = []
            for g in range(N_KV_B):
                acc = ref[:, g * group * HEAD_DIM:(g * group + 1) * HEAD_DIM]
                for h in range(g * group + 1, (g + 1) * group):
                    acc = acc + ref[:, h * HEAD_DIM:(h + 1) * HEAD_DIM]
                parts.append(acc)
            return jnp.concatenate(parts, axis=1)

        def emit(b, v):
            if b in ROPE_BLKS:
                v = v * cos_v - _swap_halves(v) * sin_v
            if b in QSCALE_BLKS:
                v = v * scale
            o_ref[:, b * LANES:(b + 1) * LANES] = v.astype(BF16)

        starts = (A_BLK[0], A_BLK[1], A_BLK[2], B_BLK[0], None, None, C_BLK[0], C_BLK[1], C_BLK[2])
        for idx, start in enumerate(starts):
            if start is None:
                continue
            for j in range(refs[idx].shape[1] // LANES):
                emit(start + j, refs[idx][:, j * LANES:(j + 1) * LANES])
        emit(B_BLK[1], kv_sum(refs[4]))
        emit(B_BLK[2], kv_sum(refs[5]))

    return pl.pallas_call(
        body, name=name, grid=(t // ROWS,),
        in_specs=[_row_spec(g.shape[1]) for g in grads] + [_row_spec(LANES), _row_spec(LANES)],
        out_specs=_row_spec(IN_COLS),
        out_shape=pltpu.HBM((t, IN_COLS), BF16), compiler_params=_params(("arbitrary",), 40),
    )(*[_in_hbm(g) for g in grads], _in_hbm(cos), _in_hbm(sin))


GROUP_COLS = ((0, WIDTH_A), (WIDTH_A, WIDTH_A + WIDTH_B), (WIDTH_A + WIDTH_B, D_MODEL))


def _mix_fwd(oa, ob, oc, gain, name):
    t = oa.shape[0]

    def body(a_ref, b_ref, c_ref, g_ref, o_ref):
        for ref, (lo, hi) in zip((a_ref, b_ref, c_ref), GROUP_COLS):
            _, n = _rms_stats(ref[...])
            o_ref[:, lo:hi] = (n * g_ref[:, lo:hi]).astype(BF16)

    return pl.pallas_call(
        body, name=name, grid=(t // ROWS,),
        in_specs=[_row_spec(WIDTH_A), _row_spec(WIDTH_B), _row_spec(WIDTH_C), _vec_spec(D_MODEL)],
        out_specs=_row_spec(D_MODEL),
        out_shape=pltpu.HBM((t, D_MODEL), BF16), compiler_params=_params(("arbitrary",), 32),
    )(_in_hbm(oa), _in_hbm(ob), _in_hbm(oc), _in_hbm(gain))


def _mix_bwd(oa, ob, oc, gain, dmixed, name, after=()):
    t = oa.shape[0]
    n_after = len(after)

    def body(a_ref, b_ref, c_ref, g_ref, dm_ref, *rest):
        da_ref, db_ref, dc_ref, dg_ref = rest[n_after:]
        first = pl.program_id(0) == 0
        for ref, dref, (lo, hi) in zip((a_ref, b_ref, c_ref), (da_ref, db_ref, dc_ref), GROUP_COLS):
            r, n = _rms_stats(ref[...])
            dm = dm_ref[:, lo:hi]
            dn = dm * g_ref[:, lo:hi]
            dref[...] = r * (dn - n * jnp.mean(dn * n, axis=-1, keepdims=True))
            part = jnp.sum(dm * n, axis=0, keepdims=True)

            @pl.when(first)
            def _():
                dg_ref[:, lo:hi] = part

            @pl.when(jnp.logical_not(first))
            def _():
                dg_ref[:, lo:hi] += part

    return pl.pallas_call(
        body, name=name, grid=(t // ROWS,),
        in_specs=[_row_spec(WIDTH_A), _row_spec(WIDTH_B), _row_spec(WIDTH_C), _vec_spec(D_MODEL), _row_spec(D_MODEL)]
        + [ANY_SPEC] * n_after,
        out_specs=[_row_spec(WIDTH_A), _row_spec(WIDTH_B), _row_spec(WIDTH_C), _vec_spec(D_MODEL)],
        out_shape=[pltpu.HBM((t, WIDTH_A), F32), pltpu.HBM((t, WIDTH_B), F32), pltpu.HBM((t, WIDTH_C), F32),
                   jax.ShapeDtypeStruct((1, D_MODEL), F32)],
        compiler_params=_params(("arbitrary",), 40),
    )(_in_hbm(oa), _in_hbm(ob), _in_hbm(oc), _in_hbm(gain), _in_hbm(dmixed), *after)


FF_COLS = 256


SUBLANES = 8
CHUNK_FWD = 256
CHUNK_BWD = 128
HALO = SUBLANES


def _ext_rows(ref, r0, chunk, where):
    t, cols = ref.shape
    zeros = jnp.zeros((HALO, cols), F32)
    if where == "first":
        return jnp.concatenate([zeros, ref[0:chunk + HALO, :]], axis=0)
    if where == "last":
        return jnp.concatenate([ref[t - chunk - HALO:t, :], zeros], axis=0)
    return ref[pl.ds(pl.multiple_of(r0 - HALO, HALO), chunk + 2 * HALO), :]


def _for_chunks(t, chunk, fn):
    fn(0, "first")

    def mid(ci, carry):
        fn(pl.multiple_of(ci * chunk, chunk), "mid")
        return carry

    lax.fori_loop(1, t // chunk - 1, mid, 0)
    fn(t - chunk, "last")


def _roll_rows(x, by):
    return pltpu.roll(x, by % x.shape[0], 0)


def _gate_val(u_ref, r0, chunk, where, w_ref, b_ref):
    ext = [_ext_rows(u_ref.at[h], r0, chunk, where) for h in range(2)]
    before = [_roll_rows(e, 1) for e in ext]
    after = [_roll_rows(e, -1) for e in ext]
    gate, val = ((before[h] * w_ref[h, 0:1, :] + ext[h] * w_ref[h, 1:2, :]) + after[h] * w_ref[h, 2:3, :] + b_ref[h]
                 for h in range(2))
    return gate, val, ext, before, after


def _ff_specs(t):
    u_spec = pl.BlockSpec((2, t, FF_COLS), lambda j: (0, 0, j))
    w_spec = pl.BlockSpec((2, 3, FF_COLS), lambda j: (0, 0, j))
    b_spec = pl.BlockSpec((2, 1, FF_COLS), lambda j: (0, 0, j))
    return u_spec, w_spec, b_spec


def _convgate_fwd(u0, conv_w, conv_b, name):
    t = u0.shape[1]
    u_spec, w_spec, b_spec = _ff_specs(t)

    def body(u_ref, w_ref, b_ref, o_ref):
        def chunk(r0, where):
            gate, val, _, _, _ = _gate_val(u_ref, r0, CHUNK_FWD, where, w_ref, b_ref)
            act = gate * jax.nn.sigmoid(gate) * val
            o_ref[pl.ds(r0, CHUNK_FWD), :] = act[HALO:HALO + CHUNK_FWD].astype(BF16)

        _for_chunks(t, CHUNK_FWD, chunk)

    return pl.pallas_call(
        body, name=name, grid=(D_FF // FF_COLS,), in_specs=[u_spec, w_spec, b_spec],
        out_specs=pl.BlockSpec((t, FF_COLS), lambda j: (0, j)),
        out_shape=pltpu.HBM((t, D_FF), BF16), compiler_params=_params(("arbitrary",), 48),
    )(_in_hbm(u0), conv_w, conv_b)


def _convgate_bwd(u0, conv_w, conv_b, d_act, name):
    t = u0.shape[1]
    u_spec, w_spec, b_spec = _ff_specs(t)

    def body(u_ref, w_ref, b_ref, da_ref, du_ref, dw_ref, db_ref, sums_ref):
        sums_ref[...] = jnp.zeros_like(sums_ref)
        inner = slice(HALO, HALO + CHUNK_BWD)

        def fold(x):
            return jnp.sum(x.reshape(CHUNK_BWD // SUBLANES, SUBLANES, x.shape[1]), axis=0)

        def chunk(r0, where):
            gate, val, ext, before, after = _gate_val(u_ref, r0, CHUNK_BWD, where, w_ref, b_ref)
            sig = jax.nn.sigmoid(gate)
            da = _ext_rows(da_ref, r0, CHUNK_BWD, where)
            d_half = (da * val * (sig * (1.0 + gate * (1.0 - sig))), da * (gate * sig))
            for h in range(2):
                du = d_half[h]
                for k, term in enumerate((du, du * before[h], du * ext[h], du * after[h])):
                    sums_ref[h, k] += fold(term[inner])
                du0 = (_roll_rows(du, -1) * w_ref[h, 0:1, :] + du * w_ref[h, 1:2, :]) + _roll_rows(du, 1) * w_ref[h, 2:3, :]
                du_ref[h, pl.ds(r0, CHUNK_BWD), :] = du0[inner].astype(BF16)

        _for_chunks(t, CHUNK_BWD, chunk)
        for h in range(2):
            db_ref[h] = jnp.sum(sums_ref[h, 0], axis=0, keepdims=True)
            for k in range(3):
                dw_ref[h, k:k + 1, :] = jnp.sum(sums_ref[h, k + 1], axis=0, keepdims=True)

    return pl.pallas_call(
        body, name=name, grid=(D_FF // FF_COLS,),
        in_specs=[u_spec, w_spec, b_spec, pl.BlockSpec((t, FF_COLS), lambda j: (0, j))],
        out_specs=[u_spec, w_spec, b_spec],
        out_shape=[pltpu.HBM((2, t, D_FF), BF16), jax.ShapeDtypeStruct((2, 3, D_FF), F32),
                   jax.ShapeDtypeStruct((2, 1, D_FF), F32)],
        scratch_shapes=[pltpu.VMEM((2, 4, SUBLANES, FF_COLS), F32)],
        compiler_params=_params(("arbitrary",), 56),
    )(_in_hbm(u0), conv_w, conv_b, _in_hbm(d_act))


class _Group:
    def __init__(self, heads, blks, kv_rows, n_win, gqa, bias_per_head):
        self.heads = heads
        self.pairs = heads // 2
        self.q_blk, self.k_blk, self.v_blk = blks
        self.kv_rows = kv_rows
        self.n_win = n_win
        self.full = kv_rows == SEQ
        self.gqa = gqa
        self.bias_per_head = bias_per_head
        self.width = heads * HEAD_DIM
        self.keys = kv_rows * n_win


GROUP_A = _Group(N_HEADS_A, A_BLK, SEQ, 1, False, False)
GROUP_B = _Group(N_HEADS_B, B_BLK, WINDOW_B, 4, True, False)
GROUP_C = _Group(N_HEADS_C, C_BLK, QB, 3, False, True)


def _win_start(grp, i):
    return jnp.clip(i * (QB // grp.kv_rows) - 1, 0, SEQ // grp.kv_rows - grp.n_win)


def _win_variant(i):
    return jnp.minimum(i, 1) + (i == NQB - 1).astype(jnp.int32)


def _attn_in_specs(grp, t):
    q_spec = pl.BlockSpec((QB, LANES), lambda p, i: (i, grp.q_blk + p))

    def col(blk):
        return (lambda p: blk) if grp.gqa else (lambda p: blk + p)

    def kv_specs(blk):
        c = col(blk)
        if grp.full:
            return [pl.BlockSpec((t, LANES), lambda p, i: (0, c(p)))]
        return [pl.BlockSpec((grp.kv_rows, LANES),
                             functools.partial(lambda p, i, w: (_win_start(grp, i) + w, c(p)), w=w))
                for w in range(grp.n_win)]

    nwk = grp.keys
    if grp.bias_per_head:
        bias_spec = pl.BlockSpec((2, None, QB, nwk), lambda p, i: (p, _win_variant(i), 0, 0))
    elif grp.full:
        bias_spec = pl.BlockSpec((1, None, QB, nwk), lambda p, i: (0, i, 0, 0))
    else:
        bias_spec = pl.BlockSpec((1, None, QB, nwk), lambda p, i: (0, _win_variant(i), 0, 0))
    sink_spec = pl.BlockSpec((1, LANES), lambda p, i: (0, p))
    return q_spec, kv_specs(grp.k_blk), kv_specs(grp.v_blk), bias_spec, sink_spec


def _head_kv(grp, whole, e, p):
    lo, hi = whole[:, :HEAD_DIM], whole[:, HEAD_DIM:]
    if grp.gqa:
        return jnp.where(2 * p + e >= N_HEADS_B // N_KV_B, hi, lo)
    return hi if e else lo


def _softmax_parts(q, k, bias, sink):
    s = lax.dot_general(q, k, (((1,), (1,)), ((), ())), preferred_element_type=F32) + bias
    m = jnp.maximum(jnp.max(s, axis=-1, keepdims=True), sink)
    pe = jnp.exp(s - m)
    denom = jnp.sum(pe, axis=-1, keepdims=True) + jnp.exp(sink - m)
    return pe, m, 1.0 / denom


def _attn_fwd(grp, proj, bias, sink, name):
    t = proj.shape[0]
    q_spec, k_specs, v_specs, bias_spec, sink_spec = _attn_in_specs(grp, t)
    nkv = len(k_specs)

    def body(*refs):
        q_ref = refs[0]
        k_refs, v_refs = refs[1:1 + nkv], refs[1 + nkv:1 + 2 * nkv]
        bias_ref, sink_ref, o_ref = refs[1 + 2 * nkv:4 + 2 * nkv]
        p = pl.program_id(0)
        k_all = jnp.concatenate([r[...] for r in k_refs], axis=0)
        v_all = jnp.concatenate([r[...] for r in v_refs], axis=0)
        outs = []
        for e in range(2):
            q = q_ref[:, e * HEAD_DIM:(e + 1) * HEAD_DIM]
            k = _head_kv(grp, k_all, e, p)
            v = _head_kv(grp, v_all, e, p)
            snk = sink_ref[0:1, e * HEAD_DIM:e * HEAD_DIM + 1]
            pe, _, inv = _softmax_parts(q, k, bias_ref[e if grp.bias_per_head else 0], snk)
            outs.append(jnp.dot(pe.astype(BF16), v, preferred_element_type=F32) * inv)
        o_ref[...] = jnp.concatenate(outs, axis=1)

    return pl.pallas_call(
        body, name=name, grid=(grp.pairs, NQB),
        in_specs=[q_spec, *k_specs, *v_specs, bias_spec, sink_spec],
        out_specs=pl.BlockSpec((QB, LANES), lambda p, i: (i, p)),
        out_shape=pltpu.HBM((t, grp.width), F32),
        compiler_params=_params(("arbitrary", "arbitrary"), 48),
    )(*([_in_hbm(proj)] * (1 + 2 * nkv)), _in_hbm(bias), sink)


def _attn_bwd(grp, proj, bias, sink, out, d_out, name):
    t = proj.shape[0]
    q_spec, k_specs, v_specs, bias_spec, sink_spec = _attn_in_specs(grp, t)
    nkv = len(k_specs)
    n_off = 2 * NA_ROWS - 1
    rows_q = QB // GRID_W
    wide = grp.keys > 2 * QB
    o_spec = pl.BlockSpec((QB, LANES), lambda p, i: (i, p))
    acc_spec = pl.BlockSpec((t, LANES), lambda p, i: (0, p))
    out_specs = [o_spec, acc_spec, acc_spec, pl.BlockSpec((None, 8, LANES), lambda p, i: (p, 0, 0))]
    out_shape = [pltpu.HBM((t, grp.width), F32)] * 3 + [jax.ShapeDtypeStruct((grp.pairs, 8, LANES), F32)]
    if grp.bias_per_head:
        out_specs.append(pl.BlockSpec((2, n_off, GRID_W, GRID_W), lambda p, i: (p, 0, 0, 0)))
        out_shape.append(jax.ShapeDtypeStruct((grp.heads, n_off, GRID_W, GRID_W), F32))

    def body(*refs):
        q_ref = refs[0]
        k_refs, v_refs = refs[1:1 + nkv], refs[1 + nkv:1 + 2 * nkv]
        bias_ref, sink_ref, o_ref, do_ref = refs[1 + 2 * nkv:5 + 2 * nkv]
        dq_ref, dk_ref, dv_ref, dsink_ref = refs[5 + 2 * nkv:9 + 2 * nkv]
        dbias_ref = refs[9 + 2 * nkv] if grp.bias_per_head else None
        p, i = pl.program_id(0), pl.program_id(1)

        @pl.when(i == 0)
        def _():
            dk_ref[...] = jnp.zeros_like(dk_ref)
            dv_ref[...] = jnp.zeros_like(dv_ref)
            dsink_ref[...] = jnp.zeros_like(dsink_ref)
            if dbias_ref is not None:
                dbias_ref[...] = jnp.zeros_like(dbias_ref)

        k_all = jnp.concatenate([r[...] for r in k_refs], axis=0)
        v_all = jnp.concatenate([r[...] for r in v_refs], axis=0)
        start = 0 if grp.full else _win_start(grp, i)
        dqs, dks, dvs, dsinks = [], [], [], []
        for e in range(2):
            cols = slice(e * HEAD_DIM, (e + 1) * HEAD_DIM)
            q = q_ref[:, cols]
            k = _head_kv(grp, k_all, e, p)
            v = _head_kv(grp, v_all, e, p)
            snk = sink_ref[0:1, e * HEAD_DIM:e * HEAD_DIM + 1]
            pe, m, inv = _softmax_parts(q, k, bias_ref[e if grp.bias_per_head else 0], snk)
            prob = pe * inv
            do = do_ref[:, cols]
            do_b = do.astype(BF16)
            pe_b = prob.astype(BF16)
            delta = jnp.sum(do * o_ref[:, cols], axis=-1, keepdims=True)
            dp = lax.dot_general(do_b, v, (((1,), (1,)), ((), ())), preferred_element_type=F32)
            ds = prob * (dp - delta)
            ds_b = ds.astype(BF16)
            dqs.append(jnp.dot(ds_b, k, preferred_element_type=F32))
            if wide:
                dks.append(lax.dot_general(q, ds_b, (((0,), (0,)), ((), ())), preferred_element_type=F32))
                dvs.append(lax.dot_general(do_b, pe_b, (((0,), (0,)), ((), ())), preferred_element_type=F32))
            else:
                dks.append(lax.dot_general(ds_b, q, (((0,), (0,)), ((), ())), preferred_element_type=F32))
                dvs.append(lax.dot_general(pe_b, do_b, (((0,), (0,)), ((), ())), preferred_element_type=F32))
            dsinks.append(-jnp.sum(jnp.exp(snk - m) * inv * delta, axis=0, keepdims=True))
            if dbias_ref is not None:
                shift = (i * QB - start * grp.kv_rows) // GRID_W
                for rq in range(rows_q):
                    for rk in range(grp.keys // GRID_W):
                        off = jnp.clip(rk - rq + (NA_ROWS - 1) - shift, 0, n_off - 1)
                        dbias_ref[e, off] += ds[rq * GRID_W:(rq + 1) * GRID_W, rk * GRID_W:(rk + 1) * GRID_W]
        dq_ref[...] = jnp.concatenate(dqs, axis=1)
        rows = pl.ds(0, t) if grp.full else pl.ds(pl.multiple_of(start * grp.kv_rows, grp.kv_rows), grp.keys)
        if wide:
            dk_ref[rows, :] += jnp.concatenate(dks, axis=0).T
            dv_ref[rows, :] += jnp.concatenate(dvs, axis=0).T
        else:
            dk_ref[rows, :] += jnp.concatenate(dks, axis=1)
            dv_ref[rows, :] += jnp.concatenate(dvs, axis=1)
        lane = lax.broadcasted_iota(jnp.int32, (8, LANES), 1)
        dsink_ref[...] += jnp.where(lane < HEAD_DIM, dsinks[0], dsinks[1])

    return pl.pallas_call(
        body, name=name, grid=(grp.pairs, NQB),
        in_specs=[q_spec, *k_specs, *v_specs, bias_spec, sink_spec, o_spec, o_spec],
        out_specs=out_specs, out_shape=out_shape,
        compiler_params=_params(("arbitrary", "arbitrary"), 56),
    )(*([_in_hbm(proj)] * (1 + 2 * nkv)), _in_hbm(bias), sink, _in_hbm(out), _in_hbm(d_out))


DILATED_CONFIGS = ((128, 1), (512, 4), (2048, 16))


def _bias_a():
    d = jnp.arange(SEQ)[None, :] - jnp.arange(SEQ)[:, None]
    mult = jnp.zeros((SEQ, SEQ), F32)
    for window, r in DILATED_CONFIGS:
        reach = (window // (2 * r)) * r
        mult = mult + ((d % r == 0) & (jnp.abs(d) <= reach)).astype(F32)
    return jnp.where(mult > 0, jnp.log(jnp.maximum(mult, 1.0)), NEG_INF).reshape(1, NQB, QB, SEQ)


def _bias_b():
    row = jnp.arange(QB)[None, :, None]
    col = jnp.arange(GROUP_B.keys)[None, None, :]
    var = jnp.arange(3)[:, None, None]
    d = col - (GROUP_B.kv_rows * var + row)
    return jnp.where(jnp.abs(d) <= WINDOW_B, 0.0, NEG_INF).astype(F32)[None]


def _offset_onehot():
    c = jnp.arange(GRID_W)[:, None, None]
    c2 = jnp.arange(GRID_W)[None, :, None]
    b = jnp.arange(LANES)[None, None, :]
    return (c2 - c + NA_COLS - 1 == b).astype(BF16).reshape(GRID_W * GRID_W, LANES)


def _split_dot(x, g):
    hi = x.astype(BF16)
    rest = x - hi.astype(F32)
    mid = rest.astype(BF16)
    lo = (rest - mid.astype(F32)).astype(BF16)
    return (jnp.dot(hi, g, preferred_element_type=F32) + jnp.dot(mid, g, preferred_element_type=F32)
            + jnp.dot(lo, g, preferred_element_type=F32))


def _table_mm(x, g, name):
    def body(x_ref, g_ref, o_ref):
        o_ref[...] = _split_dot(x_ref[...], g_ref[...])

    return pl.pallas_call(
        body, name=name, out_shape=jax.ShapeDtypeStruct((x.shape[0], g.shape[1]), F32),
        in_specs=[pl.BlockSpec(memory_space=pltpu.VMEM)] * 2, out_specs=pl.BlockSpec(memory_space=pltpu.VMEM),
        compiler_params=pltpu.CompilerParams(vmem_limit_bytes=32 * MIB),
    )(x, g)


N_OFF = 2 * NA_ROWS - 1
TABLE_ROWS = 152


def _bias_c(rpb):
    table = jnp.zeros((TABLE_ROWS, LANES), F32).at[:N_HEADS_C * N_OFF, :2 * NA_COLS - 1].set(
        rpb.reshape(N_HEADS_C * N_OFF, 2 * NA_COLS - 1))
    tiles = _table_mm(table, _offset_onehot().T, "rpb_tiles")[:N_HEADS_C * N_OFF]
    tiles = tiles.reshape(N_HEADS_C, N_OFF, GRID_W, GRID_W)
    c = jnp.arange(GRID_W)
    col_start = jnp.clip(c - NA_COLS // 2, 0, GRID_W - NA_COLS)
    col_ok = (c[None, :] >= col_start[:, None]) & (c[None, :] < col_start[:, None] + NA_COLS)
    tiles = jnp.where(col_ok, tiles, NEG_INF)
    rows_q = QB // GRID_W
    rows_k = GROUP_C.keys // GRID_W

    def body(t_ref, o_ref):
        for var in range(3):
            for rq in range(rows_q):
                r_l = rows_q * var + rq
                first = min(max(r_l - NA_ROWS // 2, 0), rows_k - NA_ROWS)
                for rk in range(rows_k):
                    if first <= rk < first + NA_ROWS:
                        tile = t_ref[rk - r_l + NA_ROWS - 1]
                    else:
                        tile = jnp.full((GRID_W, GRID_W), NEG_INF, F32)
                    o_ref[var, rq * GRID_W:(rq + 1) * GRID_W, rk * GRID_W:(rk + 1) * GRID_W] = tile

    return pl.pallas_call(
        body, name="bias_c", grid=(N_HEADS_C,),
        in_specs=[pl.BlockSpec((None, N_OFF, GRID_W, GRID_W), lambda h: (h, 0, 0, 0))],
        out_specs=pl.BlockSpec((None, 3, QB, GROUP_C.keys), lambda h: (h, 0, 0, 0)),
        out_shape=jax.ShapeDtypeStruct((N_HEADS_C, 3, QB, GROUP_C.keys), F32),
        compiler_params=_params(("arbitrary",), 32),
    )(tiles)


def _rpb_grad(d_tiles):
    flat = jnp.zeros((TABLE_ROWS, GRID_W * GRID_W), F32).at[:N_HEADS_C * N_OFF].set(
        d_tiles.reshape(N_HEADS_C * N_OFF, GRID_W * GRID_W))
    out = _table_mm(flat, _offset_onehot(), "rpb_grad")
    return out[:N_HEADS_C * N_OFF, :2 * NA_COLS - 1].reshape(N_HEADS_C, N_OFF, 2 * NA_COLS - 1)


def _sink_lanes(sink):
    return jnp.repeat(sink.astype(F32), HEAD_DIM)[None, :]


def _attention_fwd(proj_r, sink_b, bias_a, bias_b, bias_c):
    no_sink_a = jnp.full((1, WIDTH_A), NEG_INF, F32)
    no_sink_c = jnp.full((1, WIDTH_C), NEG_INF, F32)
    oa = _attn_fwd(GROUP_A, proj_r, bias_a, no_sink_a, "attn_a_fwd")
    ob = _attn_fwd(GROUP_B, proj_r, bias_b, _sink_lanes(sink_b), "attn_b_fwd")
    oc = _attn_fwd(GROUP_C, proj_r, bias_c, no_sink_c, "attn_c_fwd")
    return oa, ob, oc


def _attention_bwd(proj_r, sink_b, bias_a, bias_b, bias_c, outs, d_outs, cos, sin):
    no_sink_a = jnp.full((1, WIDTH_A), NEG_INF, F32)
    no_sink_c = jnp.full((1, WIDTH_C), NEG_INF, F32)
    dqa, dka, dva, _ = _attn_bwd(GROUP_A, proj_r, bias_a, no_sink_a, outs[0], d_outs[0], "attn_a_bwd")
    dqb, dkb, dvb, dsink = _attn_bwd(GROUP_B, proj_r, bias_b, _sink_lanes(sink_b), outs[1], d_outs[1], "attn_b_bwd")
    dqc, dkc, dvc, _, d_tiles = _attn_bwd(GROUP_C, proj_r, bias_c, no_sink_c, outs[2], d_outs[2], "attn_c_bwd")
    d_proj = _rope_bwd((dqa, dka, dva, dqb, dkb, dvb, dqc, dkc, dvc), cos, sin, "rope_bwd")
    d_sink = dsink[:, 0, :].reshape(GROUP_B.pairs, 2, HEAD_DIM)[:, :, 0].reshape(N_HEADS_B)
    return d_proj, d_sink, _rpb_grad(d_tiles)


def _adamw(w, g, m, v, name):
    r, c = w.shape
    rows = r
    for cand in (512, 256, 128, 64, 32, 16, 8):
        if r % cand == 0 and cand * c * 4 <= MIB:
            rows = cand
            break
    spec = pl.BlockSpec((rows, c), lambda i: (i, 0))

    def body(w_ref, g_ref, m_ref, v_ref, d_ref, mo_ref, vo_ref):
        d_ref[...], mo_ref[...], vo_ref[...] = _adamw_step(w_ref[...], g_ref[...], m_ref[...], v_ref[...])

    return pl.pallas_call(
        body, name=name, grid=(r // rows,), in_specs=[spec] * 4, out_specs=[spec] * 3,
        out_shape=[jax.ShapeDtypeStruct((r, c), F32)] * 3, compiler_params=_params(("arbitrary",), 32),
    )(w, g, m, v)


def _adamw_step(w, grad, m, v):
    m_new = ADAM_B1 * m + (1.0 - ADAM_B1) * grad
    v_new = ADAM_B2 * v + (1.0 - ADAM_B2) * jnp.square(grad)
    m_hat = m_new / (1.0 - ADAM_B1 ** ADAM_STEP)
    v_hat = v_new / (1.0 - ADAM_B2 ** ADAM_STEP)
    return -ADAM_LR * (m_hat / (jnp.sqrt(v_hat) + ADAM_EPS) + ADAM_WD * w), m_new, v_new


def _adamw_layer(w, g, m, v, layer, prev, name):
    _, r, c = w.shape
    rows = next(cand for cand in (512, 256, 128, 64, 32, 16, 8) if r % cand == 0 and cand * c * 4 <= 2 * MIB)
    spec = pl.BlockSpec((None, rows, c), lambda i: (layer, i, 0))
    g_spec = pl.BlockSpec((rows, c), lambda i: (i, 0))
    n_prev = 0 if prev is None else 4

    def body(w_ref, g_ref, m_ref, v_ref, *rest):
        go_ref, d_ref, mo_ref, vo_ref = rest[n_prev:]
        grad = g_ref[...]
        go_ref[...] = grad
        d_ref[...], mo_ref[...], vo_ref[...] = _adamw_step(w_ref[...], grad, m_ref[...], v_ref[...])

    return pl.pallas_call(
        body, name=name, grid=(r // rows,), in_specs=[spec, g_spec, spec, spec] + [ANY_SPEC] * n_prev,
        out_specs=[spec] * 4,
        out_shape=[jax.ShapeDtypeStruct(w.shape, F32)] * 4,
        input_output_aliases={4 + i: i for i in range(n_prev)}, compiler_params=_params(("arbitrary",), 48),
    )(w, g, m, v, *(prev or ()))


def _layer_fwd(x0, p, weight, tabs):
    h1 = _rmsnorm_fwd(x0, p["ln_attn"], "ln_attn_fwd")
    proj = _mm_nn(h1, weight("w_in", h1), cols=True, tn=256, tk=D_MODEL, out_dtype=F32, name="mm_in")
    proj_r = _rope_fwd(proj, tabs["cos"], tabs["sin"], "rope_fwd")
    outs = _attention_fwd(proj_r, p["sink_b"], tabs["bias_a"], tabs["bias_b"], p["bias_c"])
    mixed = _mix_fwd(*outs, p["mix_gain"], "mix_fwd")
    x1 = _mm_nn(mixed, weight("w_out", mixed), cols=False, tn=512, tk=D_MODEL, out_dtype=F32, name="mm_out",
                residual=x0)
    h2 = _rmsnorm_fwd(x1, p["ln_ffn"], "ln_ffn_fwd")
    u0 = _mm_nn(h2, weight("w_up", h2), cols=True, tn=256, tk=D_MODEL, out_dtype=F32, name="mm_up", out_split=2)
    act = _convgate_fwd(u0, p["conv_w"], p["conv_b"], "convgate_fwd")
    x2 = _mm_nn(act, weight("w_down", act), cols=False, tn=512, tk=D_FF // 2, out_dtype=F32, name="mm_down",
                residual=x1)
    return x2, (x0, h1, proj_r, outs, mixed, x1, h2, u0, act)


def _layer_bwd(dx2, dx2_b, saved, p, big, tabs, begin, finish, pending):
    x0, h1, proj_r, outs, mixed, x1, h2, u0, act = saved
    d_act = _mm_nt(dx2_b, big["w_down"], cols=False, to=512, tr=D_MODEL, out_dtype=F32, name="nt_down",
                   after=[pending[1]] if pending else [])
    g_down = _mm_tn(act, dx2_b, tk=D_FF // N_SHARDS, tn=D_MODEL, shards=-N_SHARDS, name="tn_down")
    du0, d_conv_w, d_conv_b = _convgate_bwd(u0, p["conv_w"], p["conv_b"], d_act, "convgate_bwd")
    token = [finish(pending[0], [du0])] if pending else []
    dh2 = _mm_nt(du0, big["w_up"], cols=True, to=1024, tr=D_FF // 4, out_dtype=F32, name="nt_up", after=token)
    g_up = _mm_tn(h2, du0, tk=1024, tn=D_FF // 4, shards=N_SHARDS, name="tn_up")
    first, token = begin({"w_down": g_down, "w_up": g_up})
    dx1, dx1_b, d_ln_ffn = _rmsnorm_bwd(x1, p["ln_ffn"], dh2, dx2, "ln_ffn_bwd", after=[token])
    d_mixed = _mm_nt(dx1_b, big["w_out"], cols=False, to=1024, tr=D_MODEL, out_dtype=F32, name="nt_out")
    g_out = _mm_tn(mixed, dx1_b, tk=D_MODEL // N_SHARDS, tn=D_MODEL, shards=-N_SHARDS, name="tn_out")
    token = finish(first, [g_out])
    *d_outs, d_mix_gain = _mix_bwd(*outs, p["mix_gain"], d_mixed, "mix_bwd", after=[token])
    d_proj, d_sink, d_rpb = _attention_bwd(proj_r, p["sink_b"], tabs["bias_a"], tabs["bias_b"], p["bias_c"], outs,
                                           d_outs, tabs["cos"], tabs["sin"])
    dh1 = _mm_nt(d_proj, big["w_in"], cols=True, to=1024, tr=IN_COLS // N_SHARDS, out_dtype=F32, name="nt_in")
    g_in = _mm_tn(h1, d_proj, tk=1024, tn=IN_COLS // N_SHARDS, shards=N_SHARDS, name="tn_in")
    dx0, dx0_b, d_ln_attn = _rmsnorm_bwd(x0, p["ln_attn"], dh1, dx1, "ln_attn_bwd")
    small = {"ln_attn": d_ln_attn, "sink_b": d_sink, "rpb_c": d_rpb, "mix_gain": d_mix_gain, "ln_ffn": d_ln_ffn,
             "conv_w": d_conv_w, "conv_b": d_conv_b}
    return dx0, dx0_b, small, begin({"w_out": g_out, "w_in": g_in})


HBM_SPEC = pl.BlockSpec(memory_space=pl.ANY)


def _place():
    x, y, c = lax.axis_index("x"), lax.axis_index("y"), lax.axis_index("c")
    chips = ((1 - x, y), (x, 1 - y), (1 - x, 1 - y))
    return x, y, c, chips


def _shard_index(px, py):
    return 2 * px + py


def _remote(src, dst, send_sem, recv_sem, to):
    return pltpu.make_async_remote_copy(src_ref=src, dst_ref=dst, send_sem=send_sem, recv_sem=recv_sem,
                                        device_id=to, device_id_type=MESH)


def _own_slot(w, layer, shard, name):
    _, r, c_dim = w.shape
    rows = r
    for cand in (512, 256, 128):
        if r % cand == 0 and cand * c_dim * 4 <= 2 * MIB:
            rows = cand
            break

    def body(s_ref, w_ref, o_ref):
        o_ref[...] = w_ref[...].astype(BF16)

    return pl.pallas_call(
        body, name=name,
        grid_spec=pltpu.PrefetchScalarGridSpec(
            num_scalar_prefetch=1, grid=(r // rows,),
            in_specs=[pl.BlockSpec((None, rows, c_dim), lambda i, s: (layer, i, 0))],
            out_specs=pl.BlockSpec((None, rows, c_dim), lambda i, s: (s[0], i, 0))),
        out_shape=jax.ShapeDtypeStruct((N_SHARDS, r, c_dim), BF16),
        compiler_params=_params(("arbitrary",), 32),
    )(shard.astype(jnp.int32).reshape(1), w)


HBM_ONLY = pl.BlockSpec(memory_space=pltpu.HBM)
SEM_SPEC = pl.BlockSpec(memory_space=pltpu.SEMAPHORE)
DATAFLOW = pltpu.SideEffectType.DATAFLOW_SIDE_EFFECTING


def _in_hbm(a):
    return pltpu.with_memory_space_constraint(a, pltpu.HBM)


N_DEV = 8


def _peers(x, y, c):
    flips = [(fx, fy, fc) for fx in (0, 1) for fy in (0, 1) for fc in (0, 1)][1:]
    return [((1 - x) if fx else x, (1 - y) if fy else y, (1 - c) if fc else c) for fx, fy, fc in flips]


def _small_start(vec, after, name):
    n_after = len(after)

    def body(v_ref, slots_ref, *rest):
        send, recv = rest[n_after], rest[n_after + 1]
        token = rest[-1]
        x, y, c, _ = _place()
        me = 4 * x + 2 * y + c
        for k, peer in enumerate(_peers(x, y, c)):
            _remote(v_ref, slots_ref.at[me], send.at[k], recv.at[k], peer).start()
        token[...] = jnp.zeros_like(token)

    slots = jax.ShapeDtypeStruct((N_DEV,) + vec.shape, vec.dtype)
    res = pl.pallas_call(
        body, name=name,
        out_shape=(pltpu.SemaphoreType.DMA((N_DEV - 1,)), pltpu.SemaphoreType.DMA((N_DEV - 1,)),
                   pltpu.HBM(vec.shape, vec.dtype), pltpu.HBM(slots.shape, slots.dtype),
                   jax.ShapeDtypeStruct((8, LANES), F32)),
        in_specs=[HBM_ONLY, HBM_ONLY] + [ANY_SPEC] * n_after,
        out_specs=(SEM_SPEC, SEM_SPEC, HBM_ONLY, HBM_ONLY, pl.BlockSpec(memory_space=pltpu.VMEM)),
        input_output_aliases={0: 2, 1: 3},
        compiler_params=pltpu.CompilerParams(has_side_effects=DATAFLOW),
    )(_in_hbm(vec), _in_hbm(lax.empty(slots.shape, slots.dtype)), *after)
    return res


def _small_wait(send, recv, vec, slots, after, name):
    def body(v_ref, slots_ref, send_ref, recv_ref, *rest):
        x, y, c, _ = _place()
        for k, (px, py, pc) in enumerate(_peers(x, y, c)):
            cp = _remote(v_ref, slots_ref.at[4 * px + 2 * py + pc], send_ref.at[k], recv_ref.at[k], (px, py, pc))
            cp.wait_send()
            cp.wait_recv()

    return pl.pallas_call(
        body, name=name, out_shape=(pltpu.HBM(vec.shape, vec.dtype), pltpu.HBM(slots.shape, slots.dtype)),
        in_specs=[HBM_ONLY, HBM_ONLY, SEM_SPEC, SEM_SPEC] + [ANY_SPEC] * len(after), out_specs=[HBM_ONLY, HBM_ONLY],
        input_output_aliases={0: 0, 1: 1},
        compiler_params=pltpu.CompilerParams(has_side_effects=DATAFLOW),
    )(vec, slots, send, recv, *after)


def _small_sum(vec, slots, name):
    rows = vec.shape[0]
    blk = min(rows, 256)
    x, y, c = lax.axis_index("x"), lax.axis_index("y"), lax.axis_index("c")
    me = (4 * x + 2 * y + c).astype(jnp.int32).reshape(1)

    def slot_spec(k):
        return pl.BlockSpec((None, blk, LANES), lambda i, w: (jnp.where(w[0] == k, (k + 1) % N_DEV, k), i, 0))

    def body(w_ref, v_ref, *rest):
        o_ref = rest[-1]
        acc = None
        for k in range(N_DEV):
            term = jnp.where(w_ref[0] == k, v_ref[...], rest[k][...])
            acc = term if acc is None else acc + term
        o_ref[...] = acc

    return pl.pallas_call(
        body, name=name,
        grid_spec=pltpu.PrefetchScalarGridSpec(
            num_scalar_prefetch=1, grid=(rows // blk,),
            in_specs=[pl.BlockSpec((blk, LANES), lambda i, w: (i, 0))] + [slot_spec(k) for k in range(N_DEV)],
            out_specs=pl.BlockSpec((blk, LANES), lambda i, w: (i, 0))),
        out_shape=jax.ShapeDtypeStruct(vec.shape, F32), compiler_params=_params(("arbitrary",), 32),
    )(me, vec, *([slots] * N_DEV))


def _half(ref, slot, c):
    half = ref.shape[1] // 2
    return ref.at[slot, pl.ds(pl.multiple_of(c * half, 8), half)]


def _gather_start(bufs, after, name):
    n = len(bufs)
    n_after = len(after)

    def body(*refs):
        ins = refs[:n]
        send, recv = refs[n + n_after], refs[n + n_after + 1]
        token = refs[-1]
        x, y, c, chips = _place()
        me = _shard_index(x, y)
        for t in range(n):
            for j, (px, py) in enumerate(chips):
                mine = _half(ins[t], me, c)
                _remote(mine, mine, send.at[t * 3 + j], recv.at[t * 3 + j], (px, py, c)).start()
        token[...] = jnp.zeros_like(token)

    thru = [pltpu.HBM(b.shape, b.dtype) for b in bufs]
    res = pl.pallas_call(
        body, name=name,
        out_shape=(pltpu.SemaphoreType.DMA((n * 3,)), pltpu.SemaphoreType.DMA((n * 3,)), *thru,
                   jax.ShapeDtypeStruct((8, LANES), F32)),
        in_specs=[HBM_ONLY] * n + [ANY_SPEC] * n_after,
        out_specs=(SEM_SPEC, SEM_SPEC, *([HBM_ONLY] * n), pl.BlockSpec(memory_space=pltpu.VMEM)),
        input_output_aliases={i: 2 + i for i in range(n)},
        compiler_params=pltpu.CompilerParams(has_side_effects=DATAFLOW),
    )(*[_in_hbm(b) for b in bufs], *after)
    return res[0], res[1], list(res[2:2 + n]), res[-1]


def _gather_wait(send, recv, bufs, after, name):
    n = len(bufs)

    def body(*refs):
        ins = refs[:n]
        send_ref, recv_ref = refs[n], refs[n + 1]
        x, y, c, chips = _place()
        me = _shard_index(x, y)
        for t in range(n):
            for j, (px, py) in enumerate(chips):
                cp = _remote(_half(ins[t], me, c), _half(ins[t], _shard_index(px, py), c), send_ref.at[t * 3 + j],
                             recv_ref.at[t * 3 + j], (px, py, c))
                cp.wait_send()
                cp.wait_recv()

    res = pl.pallas_call(
        body, name=name, out_shape=tuple(pltpu.HBM(b.shape, b.dtype) for b in bufs),
        in_specs=[HBM_ONLY] * n + [SEM_SPEC, SEM_SPEC] + [ANY_SPEC] * len(after), out_specs=[HBM_ONLY] * n,
        input_output_aliases={i: i for i in range(n)},
        compiler_params=pltpu.CompilerParams(has_side_effects=DATAFLOW),
    )(*bufs, send, recv, *after)
    return list(res)


def _gather_forward(bufs, name):
    n = len(bufs)

    def body(*refs):
        outs = refs[n:2 * n]
        send, recv = refs[2 * n:]
        x, y, c, chips = _place()
        sibling = (x, y, 1 - c)
        cps = []
        for t in range(n):
            for j, (px, py) in enumerate(chips):
                got = _half(outs[t], _shard_index(px, py), c)
                cp = _remote(got, got, send.at[t * 3 + j], recv.at[t * 3 + j], sibling)
                cp.start()
                cps.append(cp)
        for t in range(n):
            for j, (px, py) in enumerate(chips):
                theirs = _half(outs[t], _shard_index(px, py), 1 - c)
                _remote(theirs, theirs, send.at[t * 3 + j], recv.at[t * 3 + j], sibling).wait_recv()
        for cp in cps:
            cp.wait_send()

    return pl.pallas_call(
        body, name=name, in_specs=[HBM_SPEC] * n, out_specs=[HBM_SPEC] * n,
        out_shape=[jax.ShapeDtypeStruct(b.shape, b.dtype) for b in bufs],
        input_output_aliases={t: t for t in range(n)},
        scratch_shapes=[pltpu.SemaphoreType.DMA((n * 3,))] * 2,
    )(*bufs)


def _gather_forward_start(bufs, carry, name):
    n = len(bufs)

    def body(*refs):
        ins = refs[:n]
        send, recv = refs[n + 1], refs[n + 2]
        x, y, c, chips = _place()
        for t in range(n):
            for j, (px, py) in enumerate(chips):
                got = _half(ins[t], _shard_index(px, py), c)
                _remote(got, got, send.at[t * 3 + j], recv.at[t * 3 + j], (x, y, 1 - c)).start()

    res = pl.pallas_call(
        body, name=name,
        out_shape=(pltpu.SemaphoreType.DMA((n * 3,)), pltpu.SemaphoreType.DMA((n * 3,)),
                   *[pltpu.HBM(b.shape, b.dtype) for b in bufs], pltpu.HBM(carry.shape, carry.dtype)),
        in_specs=[HBM_ONLY] * (n + 1),
        out_specs=(SEM_SPEC, SEM_SPEC, *([HBM_ONLY] * (n + 1))),
        input_output_aliases={i: 2 + i for i in range(n + 1)},
        compiler_params=pltpu.CompilerParams(has_side_effects=DATAFLOW),
    )(*[_in_hbm(b) for b in bufs], _in_hbm(carry))
    return res[0], res[1], list(res[2:2 + n]), res[-1]


def _gather_forward_wait(send, recv, bufs, after, name):
    n = len(bufs)

    def body(*refs):
        ins = refs[:n]
        send_ref, recv_ref = refs[n], refs[n + 1]
        x, y, c, chips = _place()
        for t in range(n):
            for j, (px, py) in enumerate(chips):
                s = _shard_index(px, py)
                cp = _remote(_half(ins[t], s, c), _half(ins[t], s, 1 - c), send_ref.at[t * 3 + j],
                             recv_ref.at[t * 3 + j], (x, y, 1 - c))
                cp.wait_send()
                cp.wait_recv()

    res = pl.pallas_call(
        body, name=name, out_shape=tuple(pltpu.HBM(b.shape, b.dtype) for b in bufs),
        in_specs=[HBM_ONLY] * n + [SEM_SPEC, SEM_SPEC] + [ANY_SPEC] * len(after), out_specs=[HBM_ONLY] * n,
        input_output_aliases={i: i for i in range(n)},
        compiler_params=pltpu.CompilerParams(has_side_effects=DATAFLOW),
    )(*bufs, send, recv, *after)
    return list(res)


def _sibling_rows(ref, c):
    half = ref.shape[1] // 2
    return ref.at[:, pl.ds(pl.multiple_of((1 - c) * half, 8), half)]


def _half_exchange_start(grads, name):
    n = len(grads)

    def body(*refs):
        ins, lands = refs[:n], refs[n:2 * n]
        send, recv = refs[2 * n], refs[2 * n + 1]
        token = refs[-1]
        x, y, c, _ = _place()
        for t in range(n):
            _remote(_sibling_rows(ins[t], c), lands[t], send.at[t], recv.at[t], (x, y, 1 - c)).start()
        token[...] = jnp.zeros_like(token)

    halves = [jax.ShapeDtypeStruct((g.shape[0], g.shape[1] // 2, g.shape[2]), g.dtype) for g in grads]
    res = pl.pallas_call(
        body, name=name,
        out_shape=(pltpu.SemaphoreType.DMA((n,)), pltpu.SemaphoreType.DMA((n,)),
                   *[pltpu.HBM(g.shape, g.dtype) for g in grads], *[pltpu.HBM(h.shape, h.dtype) for h in halves],
                   jax.ShapeDtypeStruct((8, LANES), F32)),
        in_specs=[HBM_ONLY] * (2 * n),
        out_specs=(SEM_SPEC, SEM_SPEC, *([HBM_ONLY] * (2 * n)), pl.BlockSpec(memory_space=pltpu.VMEM)),
        input_output_aliases={i: 2 + i for i in range(2 * n)},
        compiler_params=pltpu.CompilerParams(has_side_effects=DATAFLOW),
    )(*[_in_hbm(g) for g in grads], *[_in_hbm(lax.empty(h.shape, h.dtype)) for h in halves])
    return res[0], res[1], list(res[2:2 + n]), list(res[2 + n:2 + 2 * n]), res[-1]


def _half_exchange_wait(send, recv, grads, lands, after, name):
    n = len(grads)

    def body(*refs):
        ins, got = refs[:n], refs[n:2 * n]
        send_ref, recv_ref = refs[2 * n], refs[2 * n + 1]
        x, y, c, _ = _place()
        for t in range(n):
            cp = _remote(_sibling_rows(ins[t], c), got[t], send_ref.at[t], recv_ref.at[t], (x, y, 1 - c))
            cp.wait_send()
            cp.wait_recv()

    res = pl.pallas_call(
        body, name=name,
        out_shape=(*[pltpu.HBM(g.shape, g.dtype) for g in grads], *[pltpu.HBM(h.shape, h.dtype) for h in lands]),
        in_specs=[HBM_ONLY] * (2 * n) + [SEM_SPEC, SEM_SPEC] + [ANY_SPEC] * len(after),
        out_specs=[HBM_ONLY] * (2 * n),
        input_output_aliases={i: i for i in range(2 * n)},
        compiler_params=pltpu.CompilerParams(has_side_effects=DATAFLOW),
    )(*grads, *lands, send, recv, *after)
    return list(res[:n]), list(res[n:])


def _half_rows(half, c_dim):
    for cand in (512, 256, 128, 64):
        if half % cand == 0 and cand * c_dim * 2 <= 2 * MIB:
            return cand
    raise ValueError((half, c_dim))


def _core_index():
    return lax.axis_index("c").astype(jnp.int32).reshape(1)


def _half_sum(own, other, name):
    s, r, c_dim = own.shape
    rows = _half_rows(r // 2, c_dim)
    per = r // 2 // rows

    def body(c_ref, a_ref, b_ref, o_ref):
        o_ref[...] = (a_ref[...].astype(F32) + b_ref[...].astype(F32)).astype(BF16)

    return pl.pallas_call(
        body, name=name,
        grid_spec=pltpu.PrefetchScalarGridSpec(
            num_scalar_prefetch=1, grid=(s, per),
            in_specs=[pl.BlockSpec((None, rows, c_dim), lambda k, i, c: (k, c[0] * per + i, 0)),
                      pl.BlockSpec((None, rows, c_dim), lambda k, i, c: (k, i, 0))],
            out_specs=pl.BlockSpec((None, rows, c_dim), lambda k, i, c: (k, i, 0))),
        out_shape=pltpu.HBM((s, r // 2, c_dim), BF16), compiler_params=_params(("arbitrary", "arbitrary"), 32),
    )(_core_index(), own, other)


def _reduce_start(pairs, name):
    n = len(pairs)

    def body(*refs):
        ins, lands = refs[:n], refs[n:2 * n]
        send, recv = refs[2 * n], refs[2 * n + 1]
        token = refs[-1]
        x, y, c, chips = _place()
        me = _shard_index(x, y)
        for t in range(n):
            for j, (px, py) in enumerate(chips):
                _remote(ins[t].at[_shard_index(px, py)], lands[t].at[me], send.at[t * 3 + j], recv.at[t * 3 + j],
                        (px, py, c)).start()
        token[...] = jnp.zeros_like(token)

    thru = [pltpu.HBM(b.shape, b.dtype) for b in pairs]
    res = pl.pallas_call(
        body, name=name,
        out_shape=(pltpu.SemaphoreType.DMA((n * 3,)), pltpu.SemaphoreType.DMA((n * 3,)), *thru, *thru,
                   jax.ShapeDtypeStruct((8, LANES), F32)),
        in_specs=[HBM_ONLY] * (2 * n),
        out_specs=(SEM_SPEC, SEM_SPEC, *([HBM_ONLY] * (2 * n)), pl.BlockSpec(memory_space=pltpu.VMEM)),
        input_output_aliases={i: 2 + i for i in range(2 * n)},
        compiler_params=pltpu.CompilerParams(has_side_effects=DATAFLOW),
    )(*[_in_hbm(b) for b in pairs], *[_in_hbm(lax.empty(b.shape, b.dtype)) for b in pairs])
    return res[0], res[1], list(res[2:2 + n]), list(res[2 + n:2 + 2 * n]), res[-1]


def _reduce_wait(send, recv, pairs, lands, after, name):
    n = len(pairs)

    def body(*refs):
        ins, got = refs[:n], refs[n:2 * n]
        send_ref, recv_ref = refs[2 * n], refs[2 * n + 1]
        x, y, c, chips = _place()
        for t in range(n):
            for j, (px, py) in enumerate(chips):
                s = _shard_index(px, py)
                cp = _remote(ins[t].at[s], got[t].at[s], send_ref.at[t * 3 + j], recv_ref.at[t * 3 + j], (px, py, c))
                cp.wait_send()
                cp.wait_recv()

    thru = [pltpu.HBM(b.shape, b.dtype) for b in pairs]
    res = pl.pallas_call(
        body, name=name, out_shape=(*thru, *thru),
        in_specs=[HBM_ONLY] * (2 * n) + [SEM_SPEC, SEM_SPEC] + [ANY_SPEC] * len(after),
        out_specs=[HBM_ONLY] * (2 * n),
        input_output_aliases={i: i for i in range(2 * n)},
        compiler_params=pltpu.CompilerParams(has_side_effects=DATAFLOW),
    )(*pairs, *lands, send, recv, *after)
    return list(res[:n]), list(res[n:])


def _reduce_sum(pair, landed, name):
    s, half, c_dim = pair.shape
    rows = _half_rows(half, c_dim)
    per = half // rows
    shard = _shard_index(lax.axis_index("x"), lax.axis_index("y"))
    where = jnp.stack([shard, lax.axis_index("c")]).astype(jnp.int32)

    def landed_spec(k):
        return pl.BlockSpec((None, rows, c_dim), lambda i, w: (jnp.where(w[0] == k, (k + 1) % s, k), i, 0))

    def body(w_ref, own_ref, *rest):
        o_ref = rest[-1]
        acc = None
        for k in range(s):
            term = jnp.where(w_ref[0] == k, own_ref[...], rest[k][...]).astype(F32)
            acc = term if acc is None else acc + term
        o_ref[...] = acc

    return pl.pallas_call(
        body, name=name,
        grid_spec=pltpu.PrefetchScalarGridSpec(
            num_scalar_prefetch=1, grid=(per,),
            in_specs=[pl.BlockSpec((None, rows, c_dim), lambda i, w: (w[0], i, 0))] + [landed_spec(k) for k in range(s)],
            out_specs=pl.BlockSpec((rows, c_dim), lambda i, w: (w[1] * per + i, 0))),
        out_shape=pltpu.HBM((2 * half, c_dim), F32), compiler_params=_params(("arbitrary",), 40),
    )(where, pair, *([landed] * s))


def _my_rows(ref, c):
    half = ref.shape[0] // 2
    return ref.at[pl.ds(pl.multiple_of(c * half, 8), half)]


def _half_gather_start(bufs, name):
    n = len(bufs)

    def body(*refs):
        ins = refs[:n]
        send, recv = refs[n], refs[n + 1]
        token = refs[-1]
        x, y, c, _ = _place()
        for t in range(n):
            mine = _my_rows(ins[t], c)
            _remote(mine, mine, send.at[t], recv.at[t], (x, y, 1 - c)).start()
        token[...] = jnp.zeros_like(token)

    res = pl.pallas_call(
        body, name=name,
        out_shape=(pltpu.SemaphoreType.DMA((n,)), pltpu.SemaphoreType.DMA((n,)),
                   *[pltpu.HBM(b.shape, b.dtype) for b in bufs], jax.ShapeDtypeStruct((8, LANES), F32)),
        in_specs=[HBM_ONLY] * n,
        out_specs=(SEM_SPEC, SEM_SPEC, *([HBM_ONLY] * n), pl.BlockSpec(memory_space=pltpu.VMEM)),
        input_output_aliases={i: 2 + i for i in range(n)},
        compiler_params=pltpu.CompilerParams(has_side_effects=DATAFLOW),
    )(*[_in_hbm(b) for b in bufs])
    return res[0], res[1], list(res[2:2 + n]), res[-1]


def _half_gather_wait(send, recv, bufs, after, name):
    n = len(bufs)

    def body(*refs):
        ins = refs[:n]
        send_ref, recv_ref = refs[n], refs[n + 1]
        x, y, c, _ = _place()
        for t in range(n):
            cp = _remote(_my_rows(ins[t], c), _my_rows(ins[t], 1 - c), send_ref.at[t], recv_ref.at[t], (x, y, 1 - c))
            cp.wait_send()
            cp.wait_recv()

    res = pl.pallas_call(
        body, name=name, out_shape=tuple(pltpu.HBM(b.shape, b.dtype) for b in bufs),
        in_specs=[HBM_ONLY] * n + [SEM_SPEC, SEM_SPEC] + [ANY_SPEC] * len(after), out_specs=[HBM_ONLY] * n,
        input_output_aliases={i: i for i in range(n)},
        compiler_params=pltpu.CompilerParams(has_side_effects=DATAFLOW),
    )(*bufs, send, recv, *after)
    return list(res)


WEIGHT_NAMES = ("ln_attn", "w_in", "sink_b", "rpb_c", "mix_gain", "w_out", "ln_ffn", "w_up", "conv_w", "conv_b",
                "w_down", "ln_final")
BIG_NAMES = ("w_in", "w_out", "w_up", "w_down")
REPLICATED_NAMES = ("ln_attn", "sink_b", "rpb_c", "mix_gain", "ln_ffn", "conv_b", "ln_final")
PACK_TILE = 8 * LANES


def _pack(arrays, row_multiple):
    pieces = []
    for a in arrays:
        flat = a.reshape(-1)
        pieces.append(jnp.pad(flat, (0, (-flat.shape[0]) % PACK_TILE)))
    flat = jnp.concatenate(pieces)
    flat = jnp.pad(flat, (0, (-flat.shape[0]) % (row_multiple * LANES)))
    return flat.reshape(-1, LANES)


def _unpack(packed, shapes):
    flat = packed.reshape(-1)
    out, off = [], 0
    for shape in shapes:
        size = math.prod(shape)
        out.append(flat[off:off + size].reshape(shape))
        off += size + (-size) % PACK_TILE
    return out


def kernel(x, ln_attn, w_in, sink_b, rpb_c, mix_gain, w_out, ln_ffn, w_up, conv_w, conv_b, w_down, ln_final, loss_target, m_ln_attn, m_w_in, m_sink_b, m_rpb_c, m_mix_gain, m_w_out, m_ln_ffn, m_w_up, m_conv_w, m_conv_b, m_w_down, m_ln_final, v_ln_attn, v_w_in, v_sink_b, v_rpb_c, v_mix_gain, v_w_out, v_ln_ffn, v_w_up, v_conv_w, v_conv_b, v_w_down, v_ln_final):
    w = dict(ln_attn=ln_attn, w_in=w_in, sink_b=sink_b, rpb_c=rpb_c, mix_gain=mix_gain, w_out=w_out, ln_ffn=ln_ffn,
             w_up=w_up, conv_w=conv_w, conv_b=conv_b, w_down=w_down, ln_final=ln_final)
    m = dict(ln_attn=m_ln_attn, w_in=m_w_in, sink_b=m_sink_b, rpb_c=m_rpb_c, mix_gain=m_mix_gain, w_out=m_w_out,
             ln_ffn=m_ln_ffn, w_up=m_w_up, conv_w=m_conv_w, conv_b=m_conv_b, w_down=m_w_down, ln_final=m_ln_final)
    v = dict(ln_attn=v_ln_attn, w_in=v_w_in, sink_b=v_sink_b, rpb_c=v_rpb_c, mix_gain=v_mix_gain, w_out=v_w_out,
             ln_ffn=v_ln_ffn, w_up=v_w_up, conv_w=v_conv_w, conv_b=v_conv_b, w_down=v_w_down, ln_final=v_ln_final)
    shard = _shard_index(lax.axis_index("x"), lax.axis_index("y"))
    up_cols = w_up.shape[2]

    conv_send, conv_recv, conv_vec, conv_slots, conv_token = _small_start(_pack([conv_w], 8), [], "conv_w_start")

    arrivals = []
    group_of = {}
    tokens = []
    rest = ("w_out", "w_up", "w_down")
    for l, names in ((0, ("w_in",)), (0, rest), (1, ("w_in",)), (1, rest)):
        bufs = [_own_slot(w[k], l, shard, "own_" + k) for k in names]
        send, recv, bufs, token = _gather_start(bufs, tokens[-1:] or [conv_token], "gather_start_%d" % len(arrivals))
        tokens.append(token)
        for k in names:
            group_of[l, k] = len(arrivals)
        arrivals.append({"names": names, "send": send, "recv": recv, "bufs": bufs, "done": None})

    def gathered(l, name, after):
        idx = group_of[l, name]
        group = arrivals[idx]

        def whole(k, buf):
            return buf.reshape(1, -1, buf.shape[2]) if k in ("w_out", "w_down") else buf

        if group["done"] is None:
            follow = list(after) + tokens[-1:]
            if idx == 0:
                follow += [tabs[k] for k in ("cos", "sin", "bias_a", "bias_b")]
                follow += [p[k] for p in layers for k in ("bias_c", "conv_w")]
            bufs = _gather_wait(group["send"], group["recv"], group["bufs"], follow, "gather_wait_%d" % idx)
            first = _gather_forward(bufs[:1], "gather_forward_%d" % idx)[0]
            if len(bufs) > 1:
                send, recv, rest, first = _gather_forward_start(bufs[1:], first, "gather_forward_start_%d" % idx)
                group["rest"] = (send, recv, rest)
            group["done"] = {group["names"][0]: whole(group["names"][0], first)}
        if name not in group["done"]:
            send, recv, rest = group["rest"]
            rest = _gather_forward_wait(send, recv, rest, list(after), "gather_forward_wait_%d" % idx)
            group["done"].update({k: whole(k, buf) for k, buf in zip(group["names"][1:], rest)})
        return group["done"][name]

    conv_vec, conv_slots = _small_wait(conv_send, conv_recv, conv_vec, conv_slots, tokens[-1:], "conv_w_wait")
    device = 4 * lax.axis_index("x") + 2 * lax.axis_index("y") + lax.axis_index("c")
    conv_slots = lax.dynamic_update_index_in_dim(conv_slots, conv_vec, device, 0)
    conv_all = conv_slots[0::2].reshape(N_SHARDS, -1)[:, :conv_w.size].reshape((N_SHARDS,) + conv_w.shape)

    cos, sin = _rope_tables(SEQ)
    tabs = {"cos": cos, "sin": sin, "bias_a": _bias_a(), "bias_b": _bias_b()}
    layers = []
    for l in range(DEPTH):
        conv_w_l = conv_all[:, l].reshape(2, N_SHARDS // 2, 3, up_cols).transpose(0, 2, 1, 3).reshape(2, 3, D_FF)
        layers.append({"ln_attn": ln_attn[l][None], "sink_b": sink_b[l], "bias_c": _bias_c(rpb_c[l]),
                       "mix_gain": mix_gain[l][None], "ln_ffn": ln_ffn[l][None], "conv_w": conv_w_l,
                       "conv_b": conv_b[l].reshape(2, 1, D_FF)})

    act = x[0]
    saved = []
    for l in range(DEPTH):
        act, keep = _layer_fwd(act, layers[l], lambda name, after, l=l: gathered(l, name, [after]), tabs)
        saved.append(keep)
    loss_part, dx, dx_b, d_ln_final = _loss_head(act, ln_final[None], loss_target[0], "loss_head")
    loss = lax.psum(loss_part[0, 0], ("x", "y", "c"))

    reductions = []

    opened = [0]

    def begin(l, partial):
        idx = opened[0]
        opened[0] += 1
        names = tuple(partial)
        send_sem, recv_sem, mine, theirs, token = _half_exchange_start([partial[k] for k in names],
                                                                       "half_exchange_start_%d" % idx)
        return {"idx": idx, "layer": l, "names": names, "send": send_sem, "recv": recv_sem, "mine": mine,
                "theirs": theirs}, token

    def finish(handle, after):
        idx, names = handle["idx"], handle["names"]
        mine, theirs = _half_exchange_wait(handle["send"], handle["recv"], handle["mine"], handle["theirs"], after,
                                           "half_exchange_wait_%d" % idx)
        pairs = [_half_sum(a, b, "half_sum_" + k) for k, a, b in zip(names, mine, theirs)]
        send_sem, recv_sem, pairs, lands, token = _reduce_start(pairs, "reduce_start_%d" % idx)
        reductions.append({"layer": handle["layer"], "names": names, "send": send_sem, "recv": recv_sem,
                           "pairs": pairs, "lands": lands})
        return token

    small = [None] * DEPTH
    pending = None
    for l in reversed(range(DEPTH)):
        big = {k: gathered(l, k, []) for k in BIG_NAMES}
        dx, dx_b, small[l], pending = _layer_bwd(dx, dx_b, saved[l], layers[l], big, tabs,
                                                 functools.partial(begin, l), finish, pending)
    after = [finish(pending[0], [pending[1]])]

    stacked = {k: jnp.stack([small[l][k] for l in range(DEPTH)]) for k in small[0]}
    part = {"ln_attn": stacked["ln_attn"][:, 0], "sink_b": stacked["sink_b"], "rpb_c": stacked["rpb_c"],
            "mix_gain": stacked["mix_gain"][:, 0], "ln_ffn": stacked["ln_ffn"][:, 0],
            "conv_b": stacked["conv_b"].reshape(DEPTH, 2 * D_FF), "ln_final": d_ln_final[0],
            "conv_w": stacked["conv_w"].transpose(0, 2, 1, 3).reshape(DEPTH, 3, 2 * D_FF)}
    small_names = REPLICATED_NAMES + ("conv_w",)
    small_send, small_recv, small_vec, small_slots, token = _small_start(
        _pack([part[k] for k in small_names], 256), after, "small_grads_start")
    after = [token]

    grads, delta, new_m, new_v = {}, {}, {}, {}
    updated = dict.fromkeys(BIG_NAMES)

    def arrive(idx, after):
        group = reductions[idx]
        pairs, lands = _reduce_wait(group["send"], group["recv"], group["pairs"], group["lands"], after,
                                    "reduce_wait_%d" % idx)
        halves = [_reduce_sum(pair, landed, "reduce_sum_" + k) for k, pair, landed in zip(group["names"], pairs, lands)]
        send_sem, recv_sem, halves, token = _half_gather_start(halves, "half_gather_start_%d" % idx)
        return {"idx": idx, "send": send_sem, "recv": recv_sem, "bufs": halves, "names": group["names"],
                "layer": group["layer"]}, [token]

    def update(swap, after):
        whole = _half_gather_wait(swap["send"], swap["recv"], swap["bufs"], after,
                                  "half_gather_wait_%d" % swap["idx"])
        for k, g in zip(swap["names"], whole):
            updated[k] = _adamw_layer(w[k], g, m[k], v[k], swap["layer"], updated[k], "adamw_" + k)
        return [updated[k][0] for k in swap["names"]]

    swaps = []
    for idx in range(len(reductions) - 1):
        swap, after = arrive(idx, after)
        swaps.append(swap)
    for swap in swaps[:2]:
        after = update(swap, after)
    swap, after = arrive(len(reductions) - 1, after)
    for swap in swaps[2:] + [swap]:
        after = update(swap, after)
    for k in BIG_NAMES:
        grads[k], delta[k], new_m[k], new_v[k] = updated[k]

    small_vec, small_slots = _small_wait(small_send, small_recv, small_vec, small_slots, after, "small_grads_wait")
    total = _small_sum(small_vec, small_slots, "small_grads_sum")
    for k, g in zip(small_names, _unpack(total, [part[k].shape for k in small_names])):
        grads[k] = g
    grads["conv_w"] = lax.dynamic_slice_in_dim(grads["conv_w"], shard * up_cols, up_cols, axis=2)

    flat = (DEPTH * 3, up_cols)
    res = _adamw(conv_w.reshape(flat), grads["conv_w"].reshape(flat), m["conv_w"].reshape(flat),
                 v["conv_w"].reshape(flat), "adamw_conv_w")
    delta["conv_w"], new_m["conv_w"], new_v["conv_w"] = (r.reshape(conv_w.shape) for r in res)
    shapes = [w[k].shape for k in REPLICATED_NAMES]
    packed = [_pack([d[k] for k in REPLICATED_NAMES], 128) for d in (w, grads, m, v)]
    for d, res in zip((delta, new_m, new_v), _adamw(*packed, "adamw_small")):
        for k, r in zip(REPLICATED_NAMES, _unpack(res, shapes)):
            d[k] = r

    return (loss, dx[None], *[grads[k] for k in WEIGHT_NAMES], *[delta[k] for k in WEIGHT_NAMES],
            *[new_m[k] for k in WEIGHT_NAMES], *[new_v[k] for k in WEIGHT_NAMES])
```

```python
import functools
import math

import jax
import jax.numpy as jnp
from jax import lax
from jax.experimental import pallas as pl
from jax.experimental.pallas import tpu as pltpu

F32 = jnp.float32
BF16 = jnp.bfloat16
MESH = pl.DeviceIdType.MESH

D_MODEL = 2048
SEQ = 2048
DEPTH = 2
HEAD_DIM = 64
N_HEADS_A = 12
N_HEADS_B = 10
N_KV_B = 2
N_HEADS_C = 10
WINDOW_B = 128
GRID_W = 64
NA_ROWS = 8
NA_COLS = 16
WIDTH_A = N_HEADS_A * HEAD_DIM
WIDTH_B = N_HEADS_B * HEAD_DIM
WIDTH_C = N_HEADS_C * HEAD_DIM
IN_COLS = 5120
D_FF = 5632
ROPE_THETA = 10000.0
EPS = 1e-6
NEG_INF = -1e30
N_SHARDS = 4

ADAM_LR = 0.001
ADAM_B1 = 0.9
ADAM_B2 = 0.999
ADAM_EPS = 1e-08
ADAM_WD = 0.01
ADAM_STEP = 10

LANES = 128
QB = 256
NQB = SEQ // QB
ROWS = 256
MIB = 2 ** 20

A_BLK = (0, 6, 12)
B_BLK = (18, 23, 24)
C_BLK = (25, 30, 35)
ROPE_BLKS = tuple(range(0, 12)) + tuple(range(18, 24))
QSCALE_BLKS = tuple(range(0, 6)) + tuple(range(18, 23)) + tuple(range(25, 30))
N_PBLK = IN_COLS // LANES


def _params(sem, vmem_mib):
    return pltpu.CompilerParams(dimension_semantics=sem, vmem_limit_bytes=vmem_mib * MIB)


def _weight_spec(w, cols, t_in, t_out, transposed):
    s, r, c = w.shape
    if cols:
        per = c // t_out
        k_dim, n = r, s * c
        if transposed:
            index = lambda j, rr: (rr // per, j, rr % per)
        else:
            index = lambda j, kk: (j // per, kk, j % per)
    else:
        per = r // t_in
        k_dim, n = s * r, c
        if transposed:
            index = lambda j, rr: (j // per, j % per, rr)
        else:
            index = lambda j, kk: (kk // per, kk % per, j)
    return pl.BlockSpec((None, t_in, t_out), index), k_dim, n


def _mm_nn(a, w, *, cols, tn, tk, out_dtype, name, residual=None, out_split=1):
    m, k_dim = a.shape
    w_spec, k_w, n = _weight_spec(w, cols, tk, tn, False)
    assert k_w == k_dim
    nj, nk = n // tn, k_dim // tk
    in_specs = [pl.BlockSpec((m, tk), lambda j, k: (0, k)), w_spec]
    args = [a, w]
    if residual is not None:
        in_specs.append(pl.BlockSpec((m, tn), lambda j, k: (0, j)))
        args.append(residual)
    if out_split > 1:
        per_o = n // out_split // tn
        out_spec = pl.BlockSpec((None, m, tn), lambda j, k: (j // per_o, 0, j % per_o))
        out_shape = pltpu.HBM((out_split, m, n // out_split), out_dtype)
    else:
        out_spec = pl.BlockSpec((m, tn), lambda j, k: (0, j))
        out_shape = pltpu.HBM((m, n), out_dtype)

    def body(*refs):
        a_ref, w_ref = refs[0], refs[1]
        r_ref = refs[2] if residual is not None else None
        o_ref = refs[3] if residual is not None else refs[2]

        def finish(val):
            if r_ref is not None:
                val = r_ref[...] + val
            o_ref[...] = val.astype(o_ref.dtype)

        part = jnp.dot(a_ref[...], w_ref[...], preferred_element_type=F32)
        if nk == 1:
            finish(part)
        else:
            acc = refs[-1]
            kk = pl.program_id(1)

            @pl.when(kk == 0)
            def _():
                acc[...] = part

            @pl.when(kk > 0)
            def _():
                acc[...] += part

            @pl.when(kk == nk - 1)
            def _():
                finish(acc[...])

    return pl.pallas_call(
        body, name=name, grid=(nj, nk), in_specs=in_specs, out_specs=out_spec, out_shape=out_shape,
        scratch_shapes=[pltpu.VMEM((m, tn), F32)] if nk > 1 else [],
        compiler_params=_params(("arbitrary", "arbitrary"), 56),
    )(*[_in_hbm(a) for a in args])


ANY_SPEC = pl.BlockSpec(memory_space=pl.ANY)


def _mm_nt(dy, w, *, cols, to, tr, out_dtype, name, after=()):
    if dy.ndim == 3:
        m = dy.shape[1]
        n = dy.shape[0] * dy.shape[2]
        per_d = dy.shape[2] // tr
        dy_spec = pl.BlockSpec((None, m, tr), lambda j, r: (r // per_d, 0, r % per_d))
    else:
        m, n = dy.shape
        dy_spec = pl.BlockSpec((m, tr), lambda j, r: (0, r))
    w_spec, k_dim, n_w = _weight_spec(w, cols, to, tr, True)
    assert n_w == n
    nj, nr = k_dim // to, n // tr

    n_after = len(after)

    def body(dy_ref, w_ref, *rest):
        o_ref = rest[n_after]
        part = lax.dot_general(dy_ref[...], w_ref[...], (((1,), (1,)), ((), ())), preferred_element_type=F32)
        if nr == 1:
            o_ref[...] = part.astype(o_ref.dtype)
        else:
            acc = rest[n_after + 1]
            rr = pl.program_id(1)

            @pl.when(rr == 0)
            def _():
                acc[...] = part

            @pl.when(rr > 0)
            def _():
                acc[...] += part

            @pl.when(rr == nr - 1)
            def _():
                o_ref[...] = acc[...].astype(o_ref.dtype)

    return pl.pallas_call(
        body, name=name, grid=(nj, nr), in_specs=[dy_spec, w_spec] + [ANY_SPEC] * n_after,
        out_specs=pl.BlockSpec((m, to), lambda j, r: (0, j)),
        out_shape=pltpu.HBM((m, k_dim), out_dtype),
        scratch_shapes=[pltpu.VMEM((m, to), F32)] if nr > 1 else [],
        compiler_params=_params(("arbitrary", "arbitrary"), 56),
    )(_in_hbm(dy), _in_hbm(w), *after)


def _mm_tn(x, dy, *, tk, tn, shards, name):
    m, k_dim = x.shape
    if dy.ndim == 3:
        n = dy.shape[0] * dy.shape[2]
        per_d = dy.shape[2] // tn
        dy_spec = pl.BlockSpec((None, m, tn), lambda i, j: (j // per_d, 0, j % per_d))
    else:
        n = dy.shape[1]
        dy_spec = pl.BlockSpec((m, tn), lambda i, j: (0, j))
    if shards > 0:
        per = n // shards // tn
        out_shape = pltpu.HBM((shards, k_dim, n // shards), BF16)
        out_spec = pl.BlockSpec((None, tk, tn), lambda i, j: (j // per, i, j % per))
    else:
        s = -shards
        per = k_dim // s // tk
        out_shape = pltpu.HBM((s, k_dim // s, n), BF16)
        out_spec = pl.BlockSpec((None, tk, tn), lambda i, j: (i // per, i % per, j))

    def body(x_ref, dy_ref, o_ref):
        o_ref[...] = lax.dot_general(x_ref[...], dy_ref[...], (((0,), (0,)), ((), ())),
                                     preferred_element_type=F32).astype(BF16)

    return pl.pallas_call(
        body, name=name, grid=(k_dim // tk, n // tn),
        in_specs=[pl.BlockSpec((m, tk), lambda i, j: (0, i)), dy_spec], out_specs=out_spec, out_shape=out_shape,
        compiler_params=_params(("arbitrary", "arbitrary"), 56),
    )(_in_hbm(x), _in_hbm(dy))


def _row_spec(width, rows=ROWS):
    return pl.BlockSpec((rows, width), lambda i: (i, 0))


def _vec_spec(width):
    return pl.BlockSpec((1, width), lambda i: (0, 0))


def _rms_stats(x):
    r = lax.rsqrt(jnp.mean(x * x, axis=-1, keepdims=True) + EPS)
    return r, x * r


def _rmsnorm_fwd(x, gain, name):
    t, d = x.shape

    def body(x_ref, g_ref, o_ref):
        _, n = _rms_stats(x_ref[...])
        o_ref[...] = (n * g_ref[...]).astype(BF16)

    return pl.pallas_call(
        body, name=name, grid=(t // ROWS,), in_specs=[_row_spec(d), _vec_spec(d)], out_specs=_row_spec(d),
        out_shape=pltpu.HBM((t, d), BF16), compiler_params=_params(("arbitrary",), 32),
    )(_in_hbm(x), _in_hbm(gain))


def _rmsnorm_bwd(x, gain, dh, dres, name, after=()):
    t, d = x.shape
    n_after = len(after)

    def body(x_ref, g_ref, dh_ref, dres_ref, *rest):
        dx_ref, dxb_ref, dg_ref = rest[n_after:]
        r, n = _rms_stats(x_ref[...])
        dh_v = dh_ref[...]
        dn = dh_v * g_ref[...]
        dx = dres_ref[...] + r * (dn - n * jnp.mean(dn * n, axis=-1, keepdims=True))
        dx_ref[...] = dx
        dxb_ref[...] = dx.astype(BF16)
        part = jnp.sum(dh_v * n, axis=0, keepdims=True)

        @pl.when(pl.program_id(0) == 0)
        def _():
            dg_ref[...] = part

        @pl.when(pl.program_id(0) > 0)
        def _():
            dg_ref[...] += part

    return pl.pallas_call(
        body, name=name, grid=(t // ROWS,),
        in_specs=[_row_spec(d), _vec_spec(d), _row_spec(d), _row_spec(d)] + [ANY_SPEC] * n_after,
        out_specs=[_row_spec(d), _row_spec(d), _vec_spec(d)],
        out_shape=[pltpu.HBM((t, d), F32), pltpu.HBM((t, d), BF16), jax.ShapeDtypeStruct((1, d), F32)],
        compiler_params=_params(("arbitrary",), 40),
    )(_in_hbm(x), _in_hbm(gain), _in_hbm(dh), _in_hbm(dres), *after)


def _loss_head(x, gain, target, name):
    t, d = x.shape

    def body(x_ref, g_ref, t_ref, loss_ref, dx_ref, dxb_ref, dg_ref):
        r, n = _rms_stats(x_ref[...])
        g = g_ref[...]
        err = n * g - t_ref[...]
        dy = err * (1.0 / d)
        dn = dy * g
        dx = r * (dn - n * jnp.mean(dn * n, axis=-1, keepdims=True))
        dx_ref[...] = dx
        dxb_ref[...] = dx.astype(BF16)
        part = jnp.sum(dy * n, axis=0, keepdims=True)
        lpart = jnp.zeros((8, LANES), F32) + 0.5 * jnp.sum(jnp.mean(err * err, axis=-1, keepdims=True))

        @pl.when(pl.program_id(0) == 0)
        def _():
            dg_ref[...] = part
            loss_ref[...] = lpart

        @pl.when(pl.program_id(0) > 0)
        def _():
            dg_ref[...] += part
            loss_ref[...] += lpart

    return pl.pallas_call(
        body, name=name, grid=(t // ROWS,),
        in_specs=[_row_spec(d), _vec_spec(d), _row_spec(d)],
        out_specs=[pl.BlockSpec((8, LANES), lambda i: (0, 0)), _row_spec(d), _row_spec(d), _vec_spec(d)],
        out_shape=[jax.ShapeDtypeStruct((8, LANES), F32), pltpu.HBM((t, d), F32), pltpu.HBM((t, d), BF16),
                   jax.ShapeDtypeStruct((1, d), F32)],
        compiler_params=_params(("arbitrary",), 40),
    )(x, gain, target)


def _swap_halves(x):
    lane = lax.broadcasted_iota(jnp.int32, x.shape, 1)
    return jnp.where((lane % HEAD_DIM) < HEAD_DIM // 2, pltpu.roll(x, LANES - HEAD_DIM // 2, 1),
                     pltpu.roll(x, HEAD_DIM // 2, 1))


def _rope_tables(t):
    inv_freq = ROPE_THETA ** (-jnp.arange(0, HEAD_DIM, 2, dtype=F32) / HEAD_DIM)
    ang = jnp.arange(t, dtype=F32)[:, None] * inv_freq[None, :]
    cos = jnp.tile(jnp.cos(ang), (1, LANES // (HEAD_DIM // 2)))
    sin = jnp.tile(jnp.sin(ang), (1, LANES // (HEAD_DIM // 2)))
    lane = jnp.arange(LANES)[None, :]
    return cos, jnp.where((lane % HEAD_DIM) < HEAD_DIM // 2, -sin, sin)


def _rope_fwd(proj, cos, sin, name):
    t = proj.shape[0]
    scale = HEAD_DIM ** -0.5

    def body(p_ref, c_ref, s_ref, o_ref):
        cos_v, sin_v = c_ref[...], s_ref[...]
        for b in range(N_PBLK):
            cols = slice(b * LANES, (b + 1) * LANES)
            v = p_ref[:, cols]
            if b in ROPE_BLKS:
                v = v * cos_v + _swap_halves(v) * sin_v
            if b in QSCALE_BLKS:
                v = v * scale
            o_ref[:, cols] = v.astype(BF16)

    return pl.pallas_call(
        body, name=name, grid=(t // ROWS,),
        in_specs=[_row_spec(IN_COLS), _row_spec(LANES), _row_spec(LANES)], out_specs=_row_spec(IN_COLS),
        out_shape=pltpu.HBM((t, IN_COLS), BF16), compiler_params=_params(("arbitrary",), 40),
    )(_in_hbm(proj), _in_hbm(cos), _in_hbm(sin))


def _rope_bwd(grads, cos, sin, name):
    t = grads[0].shape[0]
    scale = HEAD_DIM ** -0.5
    group = N_HEADS_B // N_KV_B

    def body(*refs):
        c_ref, s_ref, o_ref = refs[9], refs[10], refs[11]
        cos_v, sin_v = c_ref[...], s_ref[...]

        def kv_sum(ref):
            parts = []
            for g in range(N_KV_B):
                acc = ref[:, g * group * HEAD_DIM:(g * group + 1) * HEAD_DIM]
                for h in range(g * group + 1, (g + 1) * group):
                    acc = acc + ref[:, h * HEAD_DIM:(h + 1) * HEAD_DIM]
                parts.append(acc)
            return jnp.concatenate(parts, axis=1)

        def emit(b, v):
            if b in ROPE_BLKS:
                v = v * cos_v - _swap_halves(v) * sin_v
            if b in QSCALE_BLKS:
                v = v * scale
            o_ref[:, b * LANES:(b + 1) * LANES] = v.astype(BF16)

        starts = (A_BLK[0], A_BLK[1], A_BLK[2], B_BLK[0], None, None, C_BLK[0], C_BLK[1], C_BLK[2])
        for idx, start in enumerate(starts):
            if start is None:
                continue
            for j in range(refs[idx].shape[1] // LANES):
                emit(start + j, refs[idx][:, j * LANES:(j + 1) * LANES])
        emit(B_BLK[1], kv_sum(refs[4]))
        emit(B_BLK[2], kv_sum(refs[5]))

    return pl.pallas_call(
        body, name=name, grid=(t // ROWS,),
        in_specs=[_row_spec(g.shape[1]) for g in grads] + [_row_spec(LANES), _row_spec(LANES)],
        out_specs=_row_spec(IN_COLS),
        out_shape=pltpu.HBM((t, IN_COLS), BF16), compiler_params=_params(("arbitrary",), 40),
    )(*[_in_hbm(g) for g in grads], _in_hbm(cos), _in_hbm(sin))


GROUP_COLS = ((0, WIDTH_A), (WIDTH_A, WIDTH_A + WIDTH_B), (WIDTH_A + WIDTH_B, D_MODEL))


def _mix_fwd(oa, ob, oc, gain, name):
    t = oa.shape[0]

    def body(a_ref, b_ref, c_ref, g_ref, o_ref):
        for ref, (lo, hi) in zip((a_ref, b_ref, c_ref), GROUP_COLS):
            _, n = _rms_stats(ref[...])
            o_ref[:, lo:hi] = (n * g_ref[:, lo:hi]).astype(BF16)

    return pl.pallas_call(
        body, name=name, grid=(t // ROWS,),
        in_specs=[_row_spec(WIDTH_A), _row_spec(WIDTH_B), _row_spec(WIDTH_C), _vec_spec(D_MODEL)],
        out_specs=_row_spec(D_MODEL),
        out_shape=pltpu.HBM((t, D_MODEL), BF16), compiler_params=_params(("arbitrary",), 32),
    )(_in_hbm(oa), _in_hbm(ob), _in_hbm(oc), _in_hbm(gain))


def _mix_bwd(oa, ob, oc, gain, dmixed, name, after=()):
    t = oa.shape[0]
    n_after = len(after)

    def body(a_ref, b_ref, c_ref, g_ref, dm_ref, *rest):
        da_ref, db_ref, dc_ref, dg_ref = rest[n_after:]
        first = pl.program_id(0) == 0
        for ref, dref, (lo, hi) in zip((a_ref, b_ref, c_ref), (da_ref, db_ref, dc_ref), GROUP_COLS):
            r, n = _rms_stats(ref[...])
            dm = dm_ref[:, lo:hi]
            dn = dm * g_ref[:, lo:hi]
            dref[...] = r * (dn - n * jnp.mean(dn * n, axis=-1, keepdims=True))
            part = jnp.sum(dm * n, axis=0, keepdims=True)

            @pl.when(first)
            def _():
                dg_ref[:, lo:hi] = part

            @pl.when(jnp.logical_not(first))
            def _():
                dg_ref[:, lo:hi] += part

    return pl.pallas_call(
        body, name=name, grid=(t // ROWS,),
        in_specs=[_row_spec(WIDTH_A), _row_spec(WIDTH_B), _row_spec(WIDTH_C), _vec_spec(D_MODEL), _row_spec(D_MODEL)]
        + [ANY_SPEC] * n_after,
        out_specs=[_row_spec(WIDTH_A), _row_spec(WIDTH_B), _row_spec(WIDTH_C), _vec_spec(D_MODEL)],
        out_shape=[pltpu.HBM((t, WIDTH_A), F32), pltpu.HBM((t, WIDTH_B), F32), pltpu.HBM((t, WIDTH_C), F32),
                   jax.ShapeDtypeStruct((1, D_MODEL), F32)],
        compiler_params=_params(("arbitrary",), 40),
    )(_in_hbm(oa), _in_hbm(ob), _in_hbm(oc), _in_hbm(gain), _in_hbm(dmixed), *after)


FF_COLS = 256


SUBLANES = 8
CHUNK_FWD = 256
CHUNK_BWD = 128
HALO = SUBLANES


def _ext_rows(ref, r0, chunk, where):
    t, cols = ref.shape
    zeros = jnp.zeros((HALO, cols), F32)
    if where == "first":
        return jnp.concatenate([zeros, ref[0:chunk + HALO, :]], axis=0)
    if where == "last":
        return jnp.concatenate([ref[t - chunk - HALO:t, :], zeros], axis=0)
    return ref[pl.ds(pl.multiple_of(r0 - HALO, HALO), chunk + 2 * HALO), :]


def _for_chunks(t, chunk, fn):
    fn(0, "first")

    def mid(ci, carry):
        fn(pl.multiple_of(ci * chunk, chunk), "mid")
        return carry

    lax.fori_loop(1, t // chunk - 1, mid, 0)
    fn(t - chunk, "last")


def _roll_rows(x, by):
    return pltpu.roll(x, by % x.shape[0], 0)


def _gate_val(u_ref, r0, chunk, where, w_ref, b_ref):
    ext = [_ext_rows(u_ref.at[h], r0, chunk, where) for h in range(2)]
    before = [_roll_rows(e, 1) for e in ext]
    after = [_roll_rows(e, -1) for e in ext]
    gate, val = ((before[h] * w_ref[h, 0:1, :] + ext[h] * w_ref[h, 1:2, :]) + after[h] * w_ref[h, 2:3, :] + b_ref[h]
                 for h in range(2))
    return gate, val, ext, before, after


def _ff_specs(t):
    u_spec = pl.BlockSpec((2, t, FF_COLS), lambda j: (0, 0, j))
    w_spec = pl.BlockSpec((2, 3, FF_COLS), lambda j: (0, 0, j))
    b_spec = pl.BlockSpec((2, 1, FF_COLS), lambda j: (0, 0, j))
    return u_spec, w_spec, b_spec


def _convgate_fwd(u0, conv_w, conv_b, name):
    t = u0.shape[1]
    u_spec, w_spec, b_spec = _ff_specs(t)

    def body(u_ref, w_ref, b_ref, o_ref):
        def chunk(r0, where):
            gate, val, _, _, _ = _gate_val(u_ref, r0, CHUNK_FWD, where, w_ref, b_ref)
            act = gate * jax.nn.sigmoid(gate) * val
            o_ref[pl.ds(r0, CHUNK_FWD), :] = act[HALO:HALO + CHUNK_FWD].astype(BF16)

        _for_chunks(t, CHUNK_FWD, chunk)

    return pl.pallas_call(
        body, name=name, grid=(D_FF // FF_COLS,), in_specs=[u_spec, w_spec, b_spec],
        out_specs=pl.BlockSpec((t, FF_COLS), lambda j: (0, j)),
        out_shape=pltpu.HBM((t, D_FF), BF16), compiler_params=_params(("arbitrary",), 48),
    )(_in_hbm(u0), conv_w, conv_b)


def _convgate_bwd(u0, conv_w, conv_b, d_act, name):
    t = u0.shape[1]
    u_spec, w_spec, b_spec = _ff_specs(t)

    def body(u_ref, w_ref, b_ref, da_ref, du_ref, dw_ref, db_ref, sums_ref):
        sums_ref[...] = jnp.zeros_like(sums_ref)
        inner = slice(HALO, HALO + CHUNK_BWD)

        def fold(x):
            return jnp.sum(x.reshape(CHUNK_BWD // SUBLANES, SUBLANES, x.shape[1]), axis=0)

        def chunk(r0, where):
            gate, val, ext, before, after = _gate_val(u_ref, r0, CHUNK_BWD, where, w_ref, b_ref)
            sig = jax.nn.sigmoid(gate)
            da = _ext_rows(da_ref, r0, CHUNK_BWD, where)
            d_half = (da * val * (sig * (1.0 + gate * (1.0 - sig))), da * (gate * sig))
            for h in range(2):
                du = d_half[h]
                for k, term in enumerate((du, du * before[h], du * ext[h], du * after[h])):
                    sums_ref[h, k] += fold(term[inner])
                du0 = (_roll_rows(du, -1) * w_ref[h, 0:1, :] + du * w_ref[h, 1:2, :]) + _roll_rows(du, 1) * w_ref[h, 2:3, :]
                du_ref[h, pl.ds(r0, CHUNK_BWD), :] = du0[inner].astype(BF16)

        _for_chunks(t, CHUNK_BWD, chunk)
        for h in range(2):
            db_ref[h] = jnp.sum(sums_ref[h, 0], axis=0, keepdims=True)
            for k in range(3):
                dw_ref[h, k:k + 1, :] = jnp.sum(sums_ref[h, k + 1], axis=0, keepdims=True)

    return pl.pallas_call(
        body, name=name, grid=(D_FF // FF_COLS,),
        in_specs=[u_spec, w_spec, b_spec, pl.BlockSpec((t, FF_COLS), lambda j: (0, j))],
        out_specs=[u_spec, w_spec, b_spec],
        out_shape=[pltpu.HBM((2, t, D_FF), BF16), jax.ShapeDtypeStruct((2, 3, D_FF), F32),
                   jax.ShapeDtypeStruct((2, 1, D_FF), F32)],
        scratch_shapes=[pltpu.VMEM((2, 4, SUBLANES, FF_COLS), F32)],
        compiler_params=_params(("arbitrary",), 56),
    )(_in_hbm(u0), conv_w, conv_b, _in_hbm(d_act))


class _Group:
    def __init__(self, heads, blks, kv_rows, n_win, gqa, bias_per_head):
        self.heads = heads
        self.pairs = heads // 2
        self.q_blk, self.k_blk, self.v_blk = blks
        self.kv_rows = kv_rows
        self.n_win = n_win
        self.full = kv_rows == SEQ
        self.gqa = gqa
        self.bias_per_head = bias_per_head
        self.width = heads * HEAD_DIM
        self.keys = kv_rows * n_win


GROUP_A = _Group(N_HEADS_A, A_BLK, SEQ, 1, False, False)
GROUP_B = _Group(N_HEADS_B, B_BLK, WINDOW_B, 4, True, False)
GROUP_C = _Group(N_HEADS_C, C_BLK, QB, 3, False, True)


def _win_start(grp, i):
    return jnp.clip(i * (QB // grp.kv_rows) - 1, 0, SEQ // grp.kv_rows - grp.n_win)


def _win_variant(i):
    return jnp.minimum(i, 1) + (i == NQB - 1).astype(jnp.int32)


def _attn_in_specs(grp, t):
    q_spec = pl.BlockSpec((QB, LANES), lambda p, i: (i, grp.q_blk + p))

    def col(blk):
        return (lambda p: blk) if grp.gqa else (lambda p: blk + p)

    def kv_specs(blk):
        c = col(blk)
        if grp.full:
            return [pl.BlockSpec((t, LANES), lambda p, i: (0, c(p)))]
        return [pl.BlockSpec((grp.kv_rows, LANES),
                             functools.partial(lambda p, i, w: (_win_start(grp, i) + w, c(p)), w=w))
                for w in range(grp.n_win)]

    nwk = grp.keys
    if grp.bias_per_head:
        bias_spec = pl.BlockSpec((2, None, QB, nwk), lambda p, i: (p, _win_variant(i), 0, 0))
    elif grp.full:
        bias_spec = pl.BlockSpec((1, None, QB, nwk), lambda p, i: (0, i, 0, 0))
    else:
        bias_spec = pl.BlockSpec((1, None, QB, nwk), lambda p, i: (0, _win_variant(i), 0, 0))
    sink_spec = pl.BlockSpec((1, LANES), lambda p, i: (0, p))
    return q_spec, kv_specs(grp.k_blk), kv_specs(grp.v_blk), bias_spec, sink_spec


def _head_kv(grp, whole, e, p):
    lo, hi = whole[:, :HEAD_DIM], whole[:, HEAD_DIM:]
    if grp.gqa:
        return jnp.where(2 * p + e >= N_HEADS_B // N_KV_B, hi, lo)
    return hi if e else lo


def _softmax_parts(q, k, bias, sink):
    s = lax.dot_general(q, k, (((1,), (1,)), ((), ())), preferred_element_type=F32) + bias
    m = jnp.maximum(jnp.max(s, axis=-1, keepdims=True), sink)
    pe = jnp.exp(s - m)
    denom = jnp.sum(pe, axis=-1, keepdims=True) + jnp.exp(sink - m)
    return pe, m, 1.0 / denom


def _attn_fwd(grp, proj, bias, sink, name):
    t = proj.shape[0]
    q_spec, k_specs, v_specs, bias_spec, sink_spec = _attn_in_specs(grp, t)
    nkv = len(k_specs)

    def body(*refs):
        q_ref = refs[0]
        k_refs, v_refs = refs[1:1 + nkv], refs[1 + nkv:1 + 2 * nkv]
        bias_ref, sink_ref, o_ref = refs[1 + 2 * nkv:4 + 2 * nkv]
        p = pl.program_id(0)
        k_all = jnp.concatenate([r[...] for r in k_refs], axis=0)
        v_all = jnp.concatenate([r[...] for r in v_refs], axis=0)
        outs = []
        for e in range(2):
            q = q_ref[:, e * HEAD_DIM:(e + 1) * HEAD_DIM]
            k = _head_kv(grp, k_all, e, p)
            v = _head_kv(grp, v_all, e, p)
            snk = sink_ref[0:1, e * HEAD_DIM:e * HEAD_DIM + 1]
            pe, _, inv = _softmax_parts(q, k, bias_ref[e if grp.bias_per_head else 0], snk)
            outs.append(jnp.dot(pe.astype(BF16), v, preferred_element_type=F32) * inv)
        o_ref[...] = jnp.concatenate(outs, axis=1)

    return pl.pallas_call(
        body, name=name, grid=(grp.pairs, NQB),
        in_specs=[q_spec, *k_specs, *v_specs, bias_spec, sink_spec],
        out_specs=pl.BlockSpec((QB, LANES), lambda p, i: (i, p)),
        out_shape=pltpu.HBM((t, grp.width), F32),
        compiler_params=_params(("arbitrary", "arbitrary"), 48),
    )(*([_in_hbm(proj)] * (1 + 2 * nkv)), _in_hbm(bias), sink)


def _attn_bwd(grp, proj, bias, sink, out, d_out, name):
    t = proj.shape[0]
    q_spec, k_specs, v_specs, bias_spec, sink_spec = _attn_in_specs(grp, t)
    nkv = len(k_specs)
    n_off = 2 * NA_ROWS - 1
    rows_q = QB // GRID_W
    wide = grp.keys > 2 * QB
    o_spec = pl.BlockSpec((QB, LANES), lambda p, i: (i, p))
    acc_spec = pl.BlockSpec((t, LANES), lambda p, i: (0, p))
    out_specs = [o_spec, acc_spec, acc_spec, pl.BlockSpec((None, 8, LANES), lambda p, i: (p, 0, 0))]
    out_shape = [pltpu.HBM((t, grp.width), F32)] * 3 + [jax.ShapeDtypeStruct((grp.pairs, 8, LANES), F32)]
    if grp.bias_per_head:
        out_specs.append(pl.BlockSpec((2, n_off, GRID_W, GRID_W), lambda p, i: (p, 0, 0, 0)))
        out_shape.append(jax.ShapeDtypeStruct((grp.heads, n_off, GRID_W, GRID_W), F32))

    def body(*refs):
        q_ref = refs[0]
        k_refs, v_refs = refs[1:1 + nkv], refs[1 + nkv:1 + 2 * nkv]
        bias_ref, sink_ref, o_ref, do_ref = refs[1 + 2 * nkv:5 + 2 * nkv]
        dq_ref, dk_ref, dv_ref, dsink_ref = refs[5 + 2 * nkv:9 + 2 * nkv]
        dbias_ref = refs[9 + 2 * nkv] if grp.bias_per_head else None
        p, i = pl.program_id(0), pl.program_id(1)

        @pl.when(i == 0)
        def _():
            dk_ref[...] = jnp.zeros_like(dk_ref)
            dv_ref[...] = jnp.zeros_like(dv_ref)
            dsink_ref[...] = jnp.zeros_like(dsink_ref)
            if dbias_ref is not None:
                dbias_ref[...] = jnp.zeros_like(dbias_ref)

        k_all = jnp.concatenate([r[...] for r in k_refs], axis=0)
        v_all = jnp.concatenate([r[...] for r in v_refs], axis=0)
        start = 0 if grp.full else _win_start(grp, i)
        dqs, dks, dvs, dsinks = [], [], [], []
        for e in range(2):
            cols = slice(e * HEAD_DIM, (e + 1) * HEAD_DIM)
            q = q_ref[:, cols]
            k = _head_kv(grp, k_all, e, p)
            v = _head_kv(grp, v_all, e, p)
            snk = sink_ref[0:1, e * HEAD_DIM:e * HEAD_DIM + 1]
            pe, m, inv = _softmax_parts(q, k, bias_ref[e if grp.bias_per_head else 0], snk)
            prob = pe * inv
            do = do_ref[:, cols]
            do_b = do.astype(BF16)
            pe_b = prob.astype(BF16)
            delta = jnp.sum(do * o_ref[:, cols], axis=-1, keepdims=True)
            dp = lax.dot_general(do_b, v, (((1,), (1,)), ((), ())), preferred_element_type=F32)
            ds = prob * (dp - delta)
            ds_b = ds.astype(BF16)
            dqs.append(jnp.dot(ds_b, k, preferred_element_type=F32))
            if wide:
                dks.append(lax.dot_general(q, ds_b, (((0,), (0,)), ((), ())), preferred_element_type=F32))
                dvs.append(lax.dot_general(do_b, pe_b, (((0,), (0,)), ((), ())), preferred_element_type=F32))
            else:
                dks.append(lax.dot_general(ds_b, q, (((0,), (0,)), ((), ())), preferred_element_type=F32))
                dvs.append(lax.dot_general(pe_b, do_b, (((0,), (0,)), ((), ())), preferred_element_type=F32))
            dsinks.append(-jnp.sum(jnp.exp(snk - m) * inv * delta, axis=0, keepdims=True))
            if dbias_ref is not None:
                shift = (i * QB - start * grp.kv_rows) // GRID_W
                for rq in range(rows_q):
                    for rk in range(grp.keys // GRID_W):
                        off = jnp.clip(rk - rq + (NA_ROWS - 1) - shift, 0, n_off - 1)
                        dbias_ref[e, off] += ds[rq * GRID_W:(rq + 1) * GRID_W, rk * GRID_W:(rk + 1) * GRID_W]
        dq_ref[...] = jnp.concatenate(dqs, axis=1)
        rows = pl.ds(0, t) if grp.full else pl.ds(pl.multiple_of(start * grp.kv_rows, grp.kv_rows), grp.keys)
        if wide:
            dk_ref[rows, :] += jnp.concatenate(dks, axis=0).T
            dv_ref[rows, :] += jnp.concatenate(dvs, axis=0).T
        else:
            dk_ref[rows, :] += jnp.concatenate(dks, axis=1)
            dv_ref[rows, :] += jnp.concatenate(dvs, axis=1)
        lane = lax.broadcasted_iota(jnp.int32, (8, LANES), 1)
        dsink_ref[...] += jnp.where(lane < HEAD_DIM, dsinks[0], dsinks[1])

    return pl.pallas_call(
        body, name=name, grid=(grp.pairs, NQB),
        in_specs=[q_spec, *k_specs, *v_specs, bias_spec, sink_spec, o_spec, o_spec],
        out_specs=out_specs, out_shape=out_shape,
        compiler_params=_params(("arbitrary", "arbitrary"), 56),
    )(*([_in_hbm(proj)] * (1 + 2 * nkv)), _in_hbm(bias), sink, _in_hbm(out), _in_hbm(d_out))


DILATED_CONFIGS = ((128, 1), (512, 4), (2048, 16))


def _bias_a():
    d = jnp.arange(SEQ)[None, :] - jnp.arange(SEQ)[:, None]
    mult = jnp.zeros((SEQ, SEQ), F32)
    for window, r in DILATED_CONFIGS:
        reach = (window // (2 * r)) * r
        mult = mult + ((d % r == 0) & (jnp.abs(d) <= reach)).astype(F32)
    return jnp.where(mult > 0, jnp.log(jnp.maximum(mult, 1.0)), NEG_INF).reshape(1, NQB, QB, SEQ)


def _bias_b():
    row = jnp.arange(QB)[None, :, None]
    col = jnp.arange(GROUP_B.keys)[None, None, :]
    var = jnp.arange(3)[:, None, None]
    d = col - (GROUP_B.kv_rows * var + row)
    return jnp.where(jnp.abs(d) <= WINDOW_B, 0.0, NEG_INF).astype(F32)[None]


def _offset_onehot():
    c = jnp.arange(GRID_W)[:, None, None]
    c2 = jnp.arange(GRID_W)[None, :, None]
    b = jnp.arange(LANES)[None, None, :]
    return (c2 - c + NA_COLS - 1 == b).astype(BF16).reshape(GRID_W * GRID_W, LANES)


def _split_dot(x, g):
    hi = x.astype(BF16)
    rest = x - hi.astype(F32)
    mid = rest.astype(BF16)
    lo = (rest - mid.astype(F32)).astype(BF16)
    return (jnp.dot(hi, g, preferred_element_type=F32) + jnp.dot(mid, g, preferred_element_type=F32)
            + jnp.dot(lo, g, preferred_element_type=F32))


def _table_mm(x, g, name):
    def body(x_ref, g_ref, o_ref):
        o_ref[...] = _split_dot(x_ref[...], g_ref[...])

    return pl.pallas_call(
        body, name=name, out_shape=jax.ShapeDtypeStruct((x.shape[0], g.shape[1]), F32),
        in_specs=[pl.BlockSpec(memory_space=pltpu.VMEM)] * 2, out_specs=pl.BlockSpec(memory_space=pltpu.VMEM),
        compiler_params=pltpu.CompilerParams(vmem_limit_bytes=32 * MIB),
    )(x, g)


N_OFF = 2 * NA_ROWS - 1
TABLE_ROWS = 152


def _bias_c(rpb):
    table = jnp.zeros((TABLE_ROWS, LANES), F32).at[:N_HEADS_C * N_OFF, :2 * NA_COLS - 1].set(
        rpb.reshape(N_HEADS_C * N_OFF, 2 * NA_COLS - 1))
    tiles = _table_mm(table, _offset_onehot().T, "rpb_tiles")[:N_HEADS_C * N_OFF]
    tiles = tiles.reshape(N_HEADS_C, N_OFF, GRID_W, GRID_W)
    c = jnp.arange(GRID_W)
    col_start = jnp.clip(c - NA_COLS // 2, 0, GRID_W - NA_COLS)
    col_ok = (c[None, :] >= col_start[:, None]) & (c[None, :] < col_start[:, None] + NA_COLS)
    tiles = jnp.where(col_ok, tiles, NEG_INF)
    rows_q = QB // GRID_W
    rows_k = GROUP_C.keys // GRID_W

    def body(t_ref, o_ref):
        for var in range(3):
            for rq in range(rows_q):
                r_l = rows_q * var + rq
                first = min(max(r_l - NA_ROWS // 2, 0), rows_k - NA_ROWS)
                for rk in range(rows_k):
                    if first <= rk < first + NA_ROWS:
                        tile = t_ref[rk - r_l + NA_ROWS - 1]
                    else:
                        tile = jnp.full((GRID_W, GRID_W), NEG_INF, F32)
                    o_ref[var, rq * GRID_W:(rq + 1) * GRID_W, rk * GRID_W:(rk + 1) * GRID_W] = tile

    return pl.pallas_call(
        body, name="bias_c", grid=(N_HEADS_C,),
        in_specs=[pl.BlockSpec((None, N_OFF, GRID_W, GRID_W), lambda h: (h, 0, 0, 0))],
        out_specs=pl.BlockSpec((None, 3, QB, GROUP_C.keys), lambda h: (h, 0, 0, 0)),
        out_shape=jax.ShapeDtypeStruct((N_HEADS_C, 3, QB, GROUP_C.keys), F32),
        compiler_params=_params(("arbitrary",), 32),
    )(tiles)


def _rpb_grad(d_tiles):
    flat = jnp.zeros((TABLE_ROWS, GRID_W * GRID_W), F32).at[:N_HEADS_C * N_OFF].set(
        d_tiles.reshape(N_HEADS_C * N_OFF, GRID_W * GRID_W))
    out = _table_mm(flat, _offset_onehot(), "rpb_grad")
    return out[:N_HEADS_C * N_OFF, :2 * NA_COLS - 1].reshape(N_HEADS_C, N_OFF, 2 * NA_COLS - 1)


def _sink_lanes(sink):
    return jnp.repeat(sink.astype(F32), HEAD_DIM)[None, :]


def _attention_fwd(proj_r, sink_b, bias_a, bias_b, bias_c):
    no_sink_a = jnp.full((1, WIDTH_A), NEG_INF, F32)
    no_sink_c = jnp.full((1, WIDTH_C), NEG_INF, F32)
    oa = _attn_fwd(GROUP_A, proj_r, bias_a, no_sink_a, "attn_a_fwd")
    ob = _attn_fwd(GROUP_B, proj_r, bias_b, _sink_lanes(sink_b), "attn_b_fwd")
    oc = _attn_fwd(GROUP_C, proj_r, bias_c, no_sink_c, "attn_c_fwd")
    return oa, ob, oc


def _attention_bwd(proj_r, sink_b, bias_a, bias_b, bias_c, outs, d_outs, cos, sin):
    no_sink_a = jnp.full((1, WIDTH_A), NEG_INF, F32)
    no_sink_c = jnp.full((1, WIDTH_C), NEG_INF, F32)
    dqa, dka, dva, _ = _attn_bwd(GROUP_A, proj_r, bias_a, no_sink_a, outs[0], d_outs[0], "attn_a_bwd")
    dqb, dkb, dvb, dsink = _attn_bwd(GROUP_B, proj_r, bias_b, _sink_lanes(sink_b), outs[1], d_outs[1], "attn_b_bwd")
    dqc, dkc, dvc, _, d_tiles = _attn_bwd(GROUP_C, proj_r, bias_c, no_sink_c, outs[2], d_outs[2], "attn_c_bwd")
    d_proj = _rope_bwd((dqa, dka, dva, dqb, dkb, dvb, dqc, dkc, dvc), cos, sin, "rope_bwd")
    d_sink = dsink[:, 0, :].reshape(GROUP_B.pairs, 2, HEAD_DIM)[:, :, 0].reshape(N_HEADS_B)
    return d_proj, d_sink, _rpb_grad(d_tiles)


def _adamw(w, g, m, v, name):
    r, c = w.shape
    rows = r
    for cand in (512, 256, 128, 64, 32, 16, 8):
        if r % cand == 0 and cand * c * 4 <= MIB:
            rows = cand
            break
    spec = pl.BlockSpec((rows, c), lambda i: (i, 0))

    def body(w_ref, g_ref, m_ref, v_ref, d_ref, mo_ref, vo_ref):
        d_ref[...], mo_ref[...], vo_ref[...] = _adamw_step(w_ref[...], g_ref[...], m_ref[...], v_ref[...])

    return pl.pallas_call(
        body, name=name, grid=(r // rows,), in_specs=[spec] * 4, out_specs=[spec] * 3,
        out_shape=[jax.ShapeDtypeStruct((r, c), F32)] * 3, compiler_params=_params(("arbitrary",), 32),
    )(w, g, m, v)


def _adamw_step(w, grad, m, v):
    m_new = ADAM_B1 * m + (1.0 - ADAM_B1) * grad
    v_new = ADAM_B2 * v + (1.0 - ADAM_B2) * jnp.square(grad)
    m_hat = m_new / (1.0 - ADAM_B1 ** ADAM_STEP)
    v_hat = v_new / (1.0 - ADAM_B2 ** ADAM_STEP)
    return -ADAM_LR * (m_hat / (jnp.sqrt(v_hat) + ADAM_EPS) + ADAM_WD * w), m_new, v_new


def _adamw_layer(w, g, m, v, layer, prev, name):
    _, r, c = w.shape
    rows = next(cand for cand in (512, 256, 128, 64, 32, 16, 8) if r % cand == 0 and cand * c * 4 <= 2 * MIB)
    spec = pl.BlockSpec((None, rows, c), lambda i: (layer, i, 0))
    g_spec = pl.BlockSpec((rows, c), lambda i: (i, 0))
    n_prev = 0 if prev is None else 4

    def body(w_ref, g_ref, m_ref, v_ref, *rest):
        go_ref, d_ref, mo_ref, vo_ref = rest[n_prev:]
        grad = g_ref[...]
        go_ref[...] = grad
        d_ref[...], mo_ref[...], vo_ref[...] = _adamw_step(w_ref[...], grad, m_ref[...], v_ref[...])

    return pl.pallas_call(
        body, name=name, grid=(r // rows,), in_specs=[spec, g_spec, spec, spec] + [ANY_SPEC] * n_prev,
        out_specs=[spec] * 4,
        out_shape=[jax.ShapeDtypeStruct(w.shape, F32)] * 4,
        input_output_aliases={4 + i: i for i in range(n_prev)}, compiler_params=_params(("arbitrary",), 48),
    )(w, g, m, v, *(prev or ()))


def _layer_fwd(x0, p, weight, tabs):
    h1 = _rmsnorm_fwd(x0, p["ln_attn"], "ln_attn_fwd")
    proj = _mm_nn(h1, weight("w_in", h1), cols=True, tn=256, tk=D_MODEL, out_dtype=F32, name="mm_in")
    proj_r = _rope_fwd(proj, tabs["cos"], tabs["sin"], "rope_fwd")
    outs = _attention_fwd(proj_r, p["sink_b"], tabs["bias_a"], tabs["bias_b"], p["bias_c"])
    mixed = _mix_fwd(*outs, p["mix_gain"], "mix_fwd")
    x1 = _mm_nn(mixed, weight("w_out", mixed), cols=False, tn=256, tk=D_MODEL, out_dtype=F32, name="mm_out",
                residual=x0)
    h2 = _rmsnorm_fwd(x1, p["ln_ffn"], "ln_ffn_fwd")
    u0 = _mm_nn(h2, weight("w_up", h2), cols=True, tn=256, tk=D_MODEL, out_dtype=F32, name="mm_up", out_split=2)
    act = _convgate_fwd(u0, p["conv_w"], p["conv_b"], "convgate_fwd")
    x2 = _mm_nn(act, weight("w_down", act), cols=False, tn=512, tk=D_FF // 2, out_dtype=F32, name="mm_down",
                residual=x1)
    return x2, (x0, h1, proj_r, outs, mixed, x1, h2, u0, act)


def _layer_bwd(dx2, dx2_b, saved, p, big, tabs, begin, finish, pending):
    x0, h1, proj_r, outs, mixed, x1, h2, u0, act = saved
    d_act = _mm_nt(dx2_b, big["w_down"], cols=False, to=512, tr=D_MODEL, out_dtype=F32, name="nt_down",
                   after=[pending[1]] if pending else [])
    g_down = _mm_tn(act, dx2_b, tk=D_FF // N_SHARDS, tn=D_MODEL, shards=-N_SHARDS, name="tn_down")
    du0, d_conv_w, d_conv_b = _convgate_bwd(u0, p["conv_w"], p["conv_b"], d_act, "convgate_bwd")
    token = [finish(pending[0], [du0])] if pending else []
    dh2 = _mm_nt(du0, big["w_up"], cols=True, to=1024, tr=D_FF // 4, out_dtype=F32, name="nt_up", after=token)
    g_up = _mm_tn(h2, du0, tk=1024, tn=D_FF // 4, shards=N_SHARDS, name="tn_up")
    first, token = begin({"w_down": g_down, "w_up": g_up})
    dx1, dx1_b, d_ln_ffn = _rmsnorm_bwd(x1, p["ln_ffn"], dh2, dx2, "ln_ffn_bwd", after=[token])
    d_mixed = _mm_nt(dx1_b, big["w_out"], cols=False, to=512, tr=D_MODEL, out_dtype=F32, name="nt_out")
    g_out = _mm_tn(mixed, dx1_b, tk=D_MODEL // N_SHARDS, tn=D_MODEL, shards=-N_SHARDS, name="tn_out")
    token = finish(first, [g_out])
    *d_outs, d_mix_gain = _mix_bwd(*outs, p["mix_gain"], d_mixed, "mix_bwd", after=[token])
    d_proj, d_sink, d_rpb = _attention_bwd(proj_r, p["sink_b"], tabs["bias_a"], tabs["bias_b"], p["bias_c"], outs,
                                           d_outs, tabs["cos"], tabs["sin"])
    dh1 = _mm_nt(d_proj, big["w_in"], cols=True, to=1024, tr=IN_COLS // N_SHARDS, out_dtype=F32, name="nt_in")
    g_in = _mm_tn(h1, d_proj, tk=1024, tn=IN_COLS // N_SHARDS, shards=N_SHARDS, name="tn_in")
    dx0, dx0_b, d_ln_attn = _rmsnorm_bwd(x0, p["ln_attn"], dh1, dx1, "ln_attn_bwd")
    small = {"ln_attn": d_ln_attn, "sink_b": d_sink, "rpb_c": d_rpb, "mix_gain": d_mix_gain, "ln_ffn": d_ln_ffn,
             "conv_w": d_conv_w, "conv_b": d_conv_b}
    return dx0, dx0_b, small, begin({"w_out": g_out, "w_in": g_in})


HBM_SPEC = pl.BlockSpec(memory_space=pl.ANY)


def _place():
    x, y, c = lax.axis_index("x"), lax.axis_index("y"), lax.axis_index("c")
    chips = ((1 - x, y), (x, 1 - y), (1 - x, 1 - y))
    return x, y, c, chips


def _shard_index(px, py):
    return 2 * px + py


def _remote(src, dst, send_sem, recv_sem, to):
    return pltpu.make_async_remote_copy(src_ref=src, dst_ref=dst, send_sem=send_sem, recv_sem=recv_sem,
                                        device_id=to, device_id_type=MESH)


def _own_slot(w, layer, shard, name):
    _, r, c_dim = w.shape
    rows = r
    for cand in (512, 256, 128):
        if r % cand == 0 and cand * c_dim * 4 <= 2 * MIB:
            rows = cand
            break

    def body(s_ref, w_ref, o_ref):
        o_ref[...] = w_ref[...].astype(BF16)

    return pl.pallas_call(
        body, name=name,
        grid_spec=pltpu.PrefetchScalarGridSpec(
            num_scalar_prefetch=1, grid=(r // rows,),
            in_specs=[pl.BlockSpec((None, rows, c_dim), lambda i, s: (layer, i, 0))],
            out_specs=pl.BlockSpec((None, rows, c_dim), lambda i, s: (s[0], i, 0))),
        out_shape=jax.ShapeDtypeStruct((N_SHARDS, r, c_dim), BF16),
        compiler_params=_params(("arbitrary",), 32),
    )(shard.astype(jnp.int32).reshape(1), w)


HBM_ONLY = pl.BlockSpec(memory_space=pltpu.HBM)
SEM_SPEC = pl.BlockSpec(memory_space=pltpu.SEMAPHORE)
DATAFLOW = pltpu.SideEffectType.DATAFLOW_SIDE_EFFECTING


def _in_hbm(a):
    return pltpu.with_memory_space_constraint(a, pltpu.HBM)


N_DEV = 8


def _peers(x, y, c):
    flips = [(fx, fy, fc) for fx in (0, 1) for fy in (0, 1) for fc in (0, 1)][1:]
    return [((1 - x) if fx else x, (1 - y) if fy else y, (1 - c) if fc else c) for fx, fy, fc in flips]


def _small_start(vec, after, name):
    n_after = len(after)

    def body(v_ref, slots_ref, *rest):
        send, recv = rest[n_after], rest[n_after + 1]
        token = rest[-1]
        x, y, c, _ = _place()
        me = 4 * x + 2 * y + c
        for k, peer in enumerate(_peers(x, y, c)):
            _remote(v_ref, slots_ref.at[me], send.at[k], recv.at[k], peer).start()
        token[...] = jnp.zeros_like(token)

    slots = jax.ShapeDtypeStruct((N_DEV,) + vec.shape, vec.dtype)
    res = pl.pallas_call(
        body, name=name,
        out_shape=(pltpu.SemaphoreType.DMA((N_DEV - 1,)), pltpu.SemaphoreType.DMA((N_DEV - 1,)),
                   pltpu.HBM(vec.shape, vec.dtype), pltpu.HBM(slots.shape, slots.dtype),
                   jax.ShapeDtypeStruct((8, LANES), F32)),
        in_specs=[HBM_ONLY, HBM_ONLY] + [ANY_SPEC] * n_after,
        out_specs=(SEM_SPEC, SEM_SPEC, HBM_ONLY, HBM_ONLY, pl.BlockSpec(memory_space=pltpu.VMEM)),
        input_output_aliases={0: 2, 1: 3},
        compiler_params=pltpu.CompilerParams(has_side_effects=DATAFLOW),
    )(_in_hbm(vec), _in_hbm(lax.empty(slots.shape, slots.dtype)), *after)
    return res


def _small_wait(send, recv, vec, slots, after, name):
    def body(v_ref, slots_ref, send_ref, recv_ref, *rest):
        x, y, c, _ = _place()
        for k, (px, py, pc) in enumerate(_peers(x, y, c)):
            cp = _remote(v_ref, slots_ref.at[4 * px + 2 * py + pc], send_ref.at[k], recv_ref.at[k], (px, py, pc))
            cp.wait_send()
            cp.wait_recv()

    return pl.pallas_call(
        body, name=name, out_shape=(pltpu.HBM(vec.shape, vec.dtype), pltpu.HBM(slots.shape, slots.dtype)),
        in_specs=[HBM_ONLY, HBM_ONLY, SEM_SPEC, SEM_SPEC] + [ANY_SPEC] * len(after), out_specs=[HBM_ONLY, HBM_ONLY],
        input_output_aliases={0: 0, 1: 1},
        compiler_params=pltpu.CompilerParams(has_side_effects=DATAFLOW),
    )(vec, slots, send, recv, *after)


def _small_sum(vec, slots, name):
    rows = vec.shape[0]
    blk = min(rows, 256)
    x, y, c = lax.axis_index("x"), lax.axis_index("y"), lax.axis_index("c")
    me = (4 * x + 2 * y + c).astype(jnp.int32).reshape(1)

    def slot_spec(k):
        return pl.BlockSpec((None, blk, LANES), lambda i, w: (jnp.where(w[0] == k, (k + 1) % N_DEV, k), i, 0))

    def body(w_ref, v_ref, *rest):
        o_ref = rest[-1]
        acc = None
        for k in range(N_DEV):
            term = jnp.where(w_ref[0] == k, v_ref[...], rest[k][...])
            acc = term if acc is None else acc + term
        o_ref[...] = acc

    return pl.pallas_call(
        body, name=name,
        grid_spec=pltpu.PrefetchScalarGridSpec(
            num_scalar_prefetch=1, grid=(rows // blk,),
            in_specs=[pl.BlockSpec((blk, LANES), lambda i, w: (i, 0))] + [slot_spec(k) for k in range(N_DEV)],
            out_specs=pl.BlockSpec((blk, LANES), lambda i, w: (i, 0))),
        out_shape=jax.ShapeDtypeStruct(vec.shape, F32), compiler_params=_params(("arbitrary",), 32),
    )(me, vec, *([slots] * N_DEV))


def _half(ref, slot, c):
    half = ref.shape[1] // 2
    return ref.at[slot, pl.ds(pl.multiple_of(c * half, 8), half)]


def _gather_start(bufs, after, name):
    n = len(bufs)
    n_after = len(after)

    def body(*refs):
        ins = refs[:n]
        send, recv = refs[n + n_after], refs[n + n_after + 1]
        token = refs[-1]
        x, y, c, chips = _place()
        me = _shard_index(x, y)
        for t in range(n):
            for j, (px, py) in enumerate(chips):
                mine = _half(ins[t], me, c)
                _remote(mine, mine, send.at[t * 3 + j], recv.at[t * 3 + j], (px, py, c)).start()
        token[...] = jnp.zeros_like(token)

    thru = [pltpu.HBM(b.shape, b.dtype) for b in bufs]
    res = pl.pallas_call(
        body, name=name,
        out_shape=(pltpu.SemaphoreType.DMA((n * 3,)), pltpu.SemaphoreType.DMA((n * 3,)), *thru,
                   jax.ShapeDtypeStruct((8, LANES), F32)),
        in_specs=[HBM_ONLY] * n + [ANY_SPEC] * n_after,
        out_specs=(SEM_SPEC, SEM_SPEC, *([HBM_ONLY] * n), pl.BlockSpec(memory_space=pltpu.VMEM)),
        input_output_aliases={i: 2 + i for i in range(n)},
        compiler_params=pltpu.CompilerParams(has_side_effects=DATAFLOW),
    )(*[_in_hbm(b) for b in bufs], *after)
    return res[0], res[1], list(res[2:2 + n]), res[-1]


def _gather_wait(send, recv, bufs, after, name):
    n = len(bufs)

    def body(*refs):
        ins = refs[:n]
        send_ref, recv_ref = refs[n], refs[n + 1]
        x, y, c, chips = _place()
        me = _shard_index(x, y)
        for t in range(n):
            for j, (px, py) in enumerate(chips):
                cp = _remote(_half(ins[t], me, c), _half(ins[t], _shard_index(px, py), c), send_ref.at[t * 3 + j],
                             recv_ref.at[t * 3 + j], (px, py, c))
                cp.wait_send()
                cp.wait_recv()

    res = pl.pallas_call(
        body, name=name, out_shape=tuple(pltpu.HBM(b.shape, b.dtype) for b in bufs),
        in_specs=[HBM_ONLY] * n + [SEM_SPEC, SEM_SPEC] + [ANY_SPEC] * len(after), out_specs=[HBM_ONLY] * n,
        input_output_aliases={i: i for i in range(n)},
        compiler_params=pltpu.CompilerParams(has_side_effects=DATAFLOW),
    )(*bufs, send, recv, *after)
    return list(res)


def _gather_forward(bufs, name):
    n = len(bufs)

    def body(*refs):
        outs = refs[n:2 * n]
        send, recv = refs[2 * n:]
        x, y, c, chips = _place()
        sibling = (x, y, 1 - c)
        cps = []
        for t in range(n):
            for j, (px, py) in enumerate(chips):
                got = _half(outs[t], _shard_index(px, py), c)
                cp = _remote(got, got, send.at[t * 3 + j], recv.at[t * 3 + j], sibling)
                cp.start()
                cps.append(cp)
        for t in range(n):
            for j, (px, py) in enumerate(chips):
                theirs = _half(outs[t], _shard_index(px, py), 1 - c)
                _remote(theirs, theirs, send.at[t * 3 + j], recv.at[t * 3 + j], sibling).wait_recv()
        for cp in cps:
            cp.wait_send()

    return pl.pallas_call(
        body, name=name, in_specs=[HBM_SPEC] * n, out_specs=[HBM_SPEC] * n,
        out_shape=[jax.ShapeDtypeStruct(b.shape, b.dtype) for b in bufs],
        input_output_aliases={t: t for t in range(n)},
        scratch_shapes=[pltpu.SemaphoreType.DMA((n * 3,))] * 2,
    )(*bufs)


def _gather_forward_start(bufs, carry, name):
    n = len(bufs)

    def body(*refs):
        ins = refs[:n]
        send, recv = refs[n + 1], refs[n + 2]
        x, y, c, chips = _place()
        for t in range(n):
            for j, (px, py) in enumerate(chips):
                got = _half(ins[t], _shard_index(px, py), c)
                _remote(got, got, send.at[t * 3 + j], recv.at[t * 3 + j], (x, y, 1 - c)).start()

    res = pl.pallas_call(
        body, name=name,
        out_shape=(pltpu.SemaphoreType.DMA((n * 3,)), pltpu.SemaphoreType.DMA((n * 3,)),
                   *[pltpu.HBM(b.shape, b.dtype) for b in bufs], pltpu.HBM(carry.shape, carry.dtype)),
        in_specs=[HBM_ONLY] * (n + 1),
        out_specs=(SEM_SPEC, SEM_SPEC, *([HBM_ONLY] * (n + 1))),
        input_output_aliases={i: 2 + i for i in range(n + 1)},
        compiler_params=pltpu.CompilerParams(has_side_effects=DATAFLOW),
    )(*[_in_hbm(b) for b in bufs], _in_hbm(carry))
    return res[0], res[1], list(res[2:2 + n]), res[-1]


def _gather_forward_wait(send, recv, bufs, after, name):
    n = len(bufs)

    def body(*refs):
        ins = refs[:n]
        send_ref, recv_ref = refs[n], refs[n + 1]
        x, y, c, chips = _place()
        for t in range(n):
            for j, (px, py) in enumerate(chips):
                s = _shard_index(px, py)
                cp = _remote(_half(ins[t], s, c), _half(ins[t], s, 1 - c), send_ref.at[t * 3 + j],
                             recv_ref.at[t * 3 + j], (x, y, 1 - c))
                cp.wait_send()
                cp.wait_recv()

    res = pl.pallas_call(
        body, name=name, out_shape=tuple(pltpu.HBM(b.shape, b.dtype) for b in bufs),
        in_specs=[HBM_ONLY] * n + [SEM_SPEC, SEM_SPEC] + [ANY_SPEC] * len(after), out_specs=[HBM_ONLY] * n,
        input_output_aliases={i: i for i in range(n)},
        compiler_params=pltpu.CompilerParams(has_side_effects=DATAFLOW),
    )(*bufs, send, recv, *after)
    return list(res)


def _sibling_rows(ref, c):
    half = ref.shape[1] // 2
    return ref.at[:, pl.ds(pl.multiple_of((1 - c) * half, 8), half)]


def _half_exchange_start(grads, name):
    n = len(grads)

    def body(*refs):
        ins, lands = refs[:n], refs[n:2 * n]
        send, recv = refs[2 * n], refs[2 * n + 1]
        token = refs[-1]
        x, y, c, _ = _place()
        for t in range(n):
            _remote(_sibling_rows(ins[t], c), lands[t], send.at[t], recv.at[t], (x, y, 1 - c)).start()
        token[...] = jnp.zeros_like(token)

    halves = [jax.ShapeDtypeStruct((g.shape[0], g.shape[1] // 2, g.shape[2]), g.dtype) for g in grads]
    res = pl.pallas_call(
        body, name=name,
        out_shape=(pltpu.SemaphoreType.DMA((n,)), pltpu.SemaphoreType.DMA((n,)),
                   *[pltpu.HBM(g.shape, g.dtype) for g in grads], *[pltpu.HBM(h.shape, h.dtype) for h in halves],
                   jax.ShapeDtypeStruct((8, LANES), F32)),
        in_specs=[HBM_ONLY] * (2 * n),
        out_specs=(SEM_SPEC, SEM_SPEC, *([HBM_ONLY] * (2 * n)), pl.BlockSpec(memory_space=pltpu.VMEM)),
        input_output_aliases={i: 2 + i for i in range(2 * n)},
        compiler_params=pltpu.CompilerParams(has_side_effects=DATAFLOW),
    )(*[_in_hbm(g) for g in grads], *[_in_hbm(lax.empty(h.shape, h.dtype)) for h in halves])
    return res[0], res[1], list(res[2:2 + n]), list(res[2 + n:2 + 2 * n]), res[-1]


def _half_exchange_wait(send, recv, grads, lands, after, name):
    n = len(grads)

    def body(*refs):
        ins, got = refs[:n], refs[n:2 * n]
        send_ref, recv_ref = refs[2 * n], refs[2 * n + 1]
        x, y, c, _ = _place()
        for t in range(n):
            cp = _remote(_sibling_rows(ins[t], c), got[t], send_ref.at[t], recv_ref.at[t], (x, y, 1 - c))
            cp.wait_send()
            cp.wait_recv()

    res = pl.pallas_call(
        body, name=name,
        out_shape=(*[pltpu.HBM(g.shape, g.dtype) for g in grads], *[pltpu.HBM(h.shape, h.dtype) for h in lands]),
        in_specs=[HBM_ONLY] * (2 * n) + [SEM_SPEC, SEM_SPEC] + [ANY_SPEC] * len(after),
        out_specs=[HBM_ONLY] * (2 * n),
        input_output_aliases={i: i for i in range(2 * n)},
        compiler_params=pltpu.CompilerParams(has_side_effects=DATAFLOW),
    )(*grads, *lands, send, recv, *after)
    return list(res[:n]), list(res[n:])


def _half_rows(half, c_dim):
    for cand in (512, 256, 128, 64):
        if half % cand == 0 and cand * c_dim * 2 <= 2 * MIB:
            return cand
    raise ValueError((half, c_dim))


def _core_index():
    return lax.axis_index("c").astype(jnp.int32).reshape(1)


def _half_sum(own, other, name):
    s, r, c_dim = own.shape
    rows = _half_rows(r // 2, c_dim)
    per = r // 2 // rows

    def body(c_ref, a_ref, b_ref, o_ref):
        o_ref[...] = (a_ref[...].astype(F32) + b_ref[...].astype(F32)).astype(BF16)

    return pl.pallas_call(
        body, name=name,
        grid_spec=pltpu.PrefetchScalarGridSpec(
            num_scalar_prefetch=1, grid=(s, per),
            in_specs=[pl.BlockSpec((None, rows, c_dim), lambda k, i, c: (k, c[0] * per + i, 0)),
                      pl.BlockSpec((None, rows, c_dim), lambda k, i, c: (k, i, 0))],
            out_specs=pl.BlockSpec((None, rows, c_dim), lambda k, i, c: (k, i, 0))),
        out_shape=pltpu.HBM((s, r // 2, c_dim), BF16), compiler_params=_params(("arbitrary", "arbitrary"), 32),
    )(_core_index(), own, other)


def _reduce_start(pairs, name):
    n = len(pairs)

    def body(*refs):
        ins, lands = refs[:n], refs[n:2 * n]
        send, recv = refs[2 * n], refs[2 * n + 1]
        token = refs[-1]
        x, y, c, chips = _place()
        me = _shard_index(x, y)
        for t in range(n):
            for j, (px, py) in enumerate(chips):
                _remote(ins[t].at[_shard_index(px, py)], lands[t].at[me], send.at[t * 3 + j], recv.at[t * 3 + j],
                        (px, py, c)).start()
        token[...] = jnp.zeros_like(token)

    thru = [pltpu.HBM(b.shape, b.dtype) for b in pairs]
    res = pl.pallas_call(
        body, name=name,
        out_shape=(pltpu.SemaphoreType.DMA((n * 3,)), pltpu.SemaphoreType.DMA((n * 3,)), *thru, *thru,
                   jax.ShapeDtypeStruct((8, LANES), F32)),
        in_specs=[HBM_ONLY] * (2 * n),
        out_specs=(SEM_SPEC, SEM_SPEC, *([HBM_ONLY] * (2 * n)), pl.BlockSpec(memory_space=pltpu.VMEM)),
        input_output_aliases={i: 2 + i for i in range(2 * n)},
        compiler_params=pltpu.CompilerParams(has_side_effects=DATAFLOW),
    )(*[_in_hbm(b) for b in pairs], *[_in_hbm(lax.empty(b.shape, b.dtype)) for b in pairs])
    return res[0], res[1], list(res[2:2 + n]), list(res[2 + n:2 + 2 * n]), res[-1]


def _reduce_wait(send, recv, pairs, lands, after, name):
    n = len(pairs)

    def body(*refs):
        ins, got = refs[:n], refs[n:2 * n]
        send_ref, recv_ref = refs[2 * n], refs[2 * n + 1]
        x, y, c, chips = _place()
        for t in range(n):
            for j, (px, py) in enumerate(chips):
                s = _shard_index(px, py)
                cp = _remote(ins[t].at[s], got[t].at[s], send_ref.at[t * 3 + j], recv_ref.at[t * 3 + j], (px, py, c))
                cp.wait_send()
                cp.wait_recv()

    thru = [pltpu.HBM(b.shape, b.dtype) for b in pairs]
    res = pl.pallas_call(
        body, name=name, out_shape=(*thru, *thru),
        in_specs=[HBM_ONLY] * (2 * n) + [SEM_SPEC, SEM_SPEC] + [ANY_SPEC] * len(after),
        out_specs=[HBM_ONLY] * (2 * n),
        input_output_aliases={i: i for i in range(2 * n)},
        compiler_params=pltpu.CompilerParams(has_side_effects=DATAFLOW),
    )(*pairs, *lands, send, recv, *after)
    return list(res[:n]), list(res[n:])


def _reduce_sum(pair, landed, name):
    s, half, c_dim = pair.shape
    rows = _half_rows(half, c_dim)
    per = half // rows
    shard = _shard_index(lax.axis_index("x"), lax.axis_index("y"))
    where = jnp.stack([shard, lax.axis_index("c")]).astype(jnp.int32)

    def landed_spec(k):
        return pl.BlockSpec((None, rows, c_dim), lambda i, w: (jnp.where(w[0] == k, (k + 1) % s, k), i, 0))

    def body(w_ref, own_ref, *rest):
        o_ref = rest[-1]
        acc = None
        for k in range(s):
            term = jnp.where(w_ref[0] == k, own_ref[...], rest[k][...]).astype(F32)
            acc = term if acc is None else acc + term
        o_ref[...] = acc

    return pl.pallas_call(
        body, name=name,
        grid_spec=pltpu.PrefetchScalarGridSpec(
            num_scalar_prefetch=1, grid=(per,),
            in_specs=[pl.BlockSpec((None, rows, c_dim), lambda i, w: (w[0], i, 0))] + [landed_spec(k) for k in range(s)],
            out_specs=pl.BlockSpec((rows, c_dim), lambda i, w: (w[1] * per + i, 0))),
        out_shape=pltpu.HBM((2 * half, c_dim), F32), compiler_params=_params(("arbitrary",), 40),
    )(where, pair, *([landed] * s))


def _my_rows(ref, c):
    half = ref.shape[0] // 2
    return ref.at[pl.ds(pl.multiple_of(c * half, 8), half)]


def _half_gather_start(bufs, name):
    n = len(bufs)

    def body(*refs):
        ins = refs[:n]
        send, recv = refs[n], refs[n + 1]
        token = refs[-1]
        x, y, c, _ = _place()
        for t in range(n):
            mine = _my_rows(ins[t], c)
            _remote(mine, mine, send.at[t], recv.at[t], (x, y, 1 - c)).start()
        token[...] = jnp.zeros_like(token)

    res = pl.pallas_call(
        body, name=name,
        out_shape=(pltpu.SemaphoreType.DMA((n,)), pltpu.SemaphoreType.DMA((n,)),
                   *[pltpu.HBM(b.shape, b.dtype) for b in bufs], jax.ShapeDtypeStruct((8, LANES), F32)),
        in_specs=[HBM_ONLY] * n,
        out_specs=(SEM_SPEC, SEM_SPEC, *([HBM_ONLY] * n), pl.BlockSpec(memory_space=pltpu.VMEM)),
        input_output_aliases={i: 2 + i for i in range(n)},
        compiler_params=pltpu.CompilerParams(has_side_effects=DATAFLOW),
    )(*[_in_hbm(b) for b in bufs])
    return res[0], res[1], list(res[2:2 + n]), res[-1]


def _half_gather_wait(send, recv, bufs, after, name):
    n = len(bufs)

    def body(*refs):
        ins = refs[:n]
        send_ref, recv_ref = refs[n], refs[n + 1]
        x, y, c, _ = _place()
        for t in range(n):
            cp = _remote(_my_rows(ins[t], c), _my_rows(ins[t], 1 - c), send_ref.at[t], recv_ref.at[t], (x, y, 1 - c))
            cp.wait_send()
            cp.wait_recv()

    res = pl.pallas_call(
        body, name=name, out_shape=tuple(pltpu.HBM(b.shape, b.dtype) for b in bufs),
        in_specs=[HBM_ONLY] * n + [SEM_SPEC, SEM_SPEC] + [ANY_SPEC] * len(after), out_specs=[HBM_ONLY] * n,
        input_output_aliases={i: i for i in range(n)},
        compiler_params=pltpu.CompilerParams(has_side_effects=DATAFLOW),
    )(*bufs, send, recv, *after)
    return list(res)


WEIGHT_NAMES = ("ln_attn", "w_in", "sink_b", "rpb_c", "mix_gain", "w_out", "ln_ffn", "w_up", "conv_w", "conv_b",
                "w_down", "ln_final")
BIG_NAMES = ("w_in", "w_out", "w_up", "w_down")
REPLICATED_NAMES = ("ln_attn", "sink_b", "rpb_c", "mix_gain", "ln_ffn", "conv_b", "ln_final")
PACK_TILE = 8 * LANES


def _pack(arrays, row_multiple):
    pieces = []
    for a in arrays:
        flat = a.reshape(-1)
        pieces.append(jnp.pad(flat, (0, (-flat.shape[0]) % PACK_TILE)))
    flat = jnp.concatenate(pieces)
    flat = jnp.pad(flat, (0, (-flat.shape[0]) % (row_multiple * LANES)))
    return flat.reshape(-1, LANES)


def _unpack(packed, shapes):
    flat = packed.reshape(-1)
    out, off = [], 0
    for shape in shapes:
        size = math.prod(shape)
        out.append(flat[off:off + size].reshape(shape))
        off += size + (-size) % PACK_TILE
    return out


def kernel(x, ln_attn, w_in, sink_b, rpb_c, mix_gain, w_out, ln_ffn, w_up, conv_w, conv_b, w_down, ln_final, loss_target, m_ln_attn, m_w_in, m_sink_b, m_rpb_c, m_mix_gain, m_w_out, m_ln_ffn, m_w_up, m_conv_w, m_conv_b, m_w_down, m_ln_final, v_ln_attn, v_w_in, v_sink_b, v_rpb_c, v_mix_gain, v_w_out, v_ln_ffn, v_w_up, v_conv_w, v_conv_b, v_w_down, v_ln_final):
    w = dict(ln_attn=ln_attn, w_in=w_in, sink_b=sink_b, rpb_c=rpb_c, mix_gain=mix_gain, w_out=w_out, ln_ffn=ln_ffn,
             w_up=w_up, conv_w=conv_w, conv_b=conv_b, w_down=w_down, ln_final=ln_final)
    m = dict(ln_attn=m_ln_attn, w_in=m_w_in, sink_b=m_sink_b, rpb_c=m_rpb_c, mix_gain=m_mix_gain, w_out=m_w_out,
             ln_ffn=m_ln_ffn, w_up=m_w_up, conv_w=m_conv_w, conv_b=m_conv_b, w_down=m_w_down, ln_final=m_ln_final)
    v = dict(ln_attn=v_ln_attn, w_in=v_w_in, sink_b=v_sink_b, rpb_c=v_rpb_c, mix_gain=v_mix_gain, w_out=v_w_out,
             ln_ffn=v_ln_ffn, w_up=v_w_up, conv_w=v_conv_w, conv_b=v_conv_b, w_down=v_w_down, ln_final=v_ln_final)
    shard = _shard_index(lax.axis_index("x"), lax.axis_index("y"))
    up_cols = w_up.shape[2]

    conv_send, conv_recv, conv_vec, conv_slots, conv_token = _small_start(_pack([conv_w], 8), [], "conv_w_start")

    arrivals = []
    group_of = {}
    tokens = []
    rest = ("w_out", "w_up", "w_down")
    for l, names in ((0, ("w_in",)), (0, rest), (1, ("w_in",)), (1, rest)):
        bufs = [_own_slot(w[k], l, shard, "own_" + k) for k in names]
        send, recv, bufs, token = _gather_start(bufs, tokens[-1:] or [conv_token], "gather_start_%d" % len(arrivals))
        tokens.append(token)
        for k in names:
            group_of[l, k] = len(arrivals)
        arrivals.append({"names": names, "send": send, "recv": recv, "bufs": bufs, "done": None})

    def gathered(l, name, after):
        idx = group_of[l, name]
        group = arrivals[idx]

        def whole(k, buf):
            return buf.reshape(1, -1, buf.shape[2]) if k in ("w_out", "w_down") else buf

        if group["done"] is None:
            follow = list(after) + tokens[-1:]
            if idx == 0:
                follow += [tabs[k] for k in ("cos", "sin", "bias_a", "bias_b")]
                follow += [p[k] for p in layers for k in ("bias_c", "conv_w")]
            bufs = _gather_wait(group["send"], group["recv"], group["bufs"], follow, "gather_wait_%d" % idx)
            first = _gather_forward(bufs[:1], "gather_forward_%d" % idx)[0]
            if len(bufs) > 1:
                send, recv, rest, first = _gather_forward_start(bufs[1:], first, "gather_forward_start_%d" % idx)
                group["rest"] = (send, recv, rest)
            group["done"] = {group["names"][0]: whole(group["names"][0], first)}
        if name not in group["done"]:
            send, recv, rest = group["rest"]
            rest = _gather_forward_wait(send, recv, rest, list(after), "gather_forward_wait_%d" % idx)
            group["done"].update({k: whole(k, buf) for k, buf in zip(group["names"][1:], rest)})
        return group["done"][name]

    conv_vec, conv_slots = _small_wait(conv_send, conv_recv, conv_vec, conv_slots, tokens[-1:], "conv_w_wait")
    device = 4 * lax.axis_index("x") + 2 * lax.axis_index("y") + lax.axis_index("c")
    conv_slots = lax.dynamic_update_index_in_dim(conv_slots, conv_vec, device, 0)
    conv_all = conv_slots[0::2].reshape(N_SHARDS, -1)[:, :conv_w.size].reshape((N_SHARDS,) + conv_w.shape)

    cos, sin = _rope_tables(SEQ)
    tabs = {"cos": cos, "sin": sin, "bias_a": _bias_a(), "bias_b": _bias_b()}
    layers = []
    for l in range(DEPTH):
        conv_w_l = conv_all[:, l].reshape(2, N_SHARDS // 2, 3, up_cols).transpose(0, 2, 1, 3).reshape(2, 3, D_FF)
        layers.append({"ln_attn": ln_attn[l][None], "sink_b": sink_b[l], "bias_c": _bias_c(rpb_c[l]),
                       "mix_gain": mix_gain[l][None], "ln_ffn": ln_ffn[l][None], "conv_w": conv_w_l,
                       "conv_b": conv_b[l].reshape(2, 1, D_FF)})

    act = x[0]
    saved = []
    for l in range(DEPTH):
        act, keep = _layer_fwd(act, layers[l], lambda name, after, l=l: gathered(l, name, [after]), tabs)
        saved.append(keep)
    loss_part, dx, dx_b, d_ln_final = _loss_head(act, ln_final[None], loss_target[0], "loss_head")
    loss = lax.psum(loss_part[0, 0], ("x", "y", "c"))

    reductions = []

    opened = [0]

    def begin(l, partial):
        idx = opened[0]
        opened[0] += 1
        names = tuple(partial)
        send_sem, recv_sem, mine, theirs, token = _half_exchange_start([partial[k] for k in names],
                                                                       "half_exchange_start_%d" % idx)
        return {"idx": idx, "layer": l, "names": names, "send": send_sem, "recv": recv_sem, "mine": mine,
                "theirs": theirs}, token

    def finish(handle, after):
        idx, names = handle["idx"], handle["names"]
        mine, theirs = _half_exchange_wait(handle["send"], handle["recv"], handle["mine"], handle["theirs"], after,
                                           "half_exchange_wait_%d" % idx)
        pairs = [_half_sum(a, b, "half_sum_" + k) for k, a, b in zip(names, mine, theirs)]
        send_sem, recv_sem, pairs, lands, token = _reduce_start(pairs, "reduce_start_%d" % idx)
        reductions.append({"layer": handle["layer"], "names": names, "send": send_sem, "recv": recv_sem,
                           "pairs": pairs, "lands": lands})
        return token

    small = [None] * DEPTH
    pending = None
    for l in reversed(range(DEPTH)):
        big = {k: gathered(l, k, []) for k in BIG_NAMES}
        dx, dx_b, small[l], pending = _layer_bwd(dx, dx_b, saved[l], layers[l], big, tabs,
                                                 functools.partial(begin, l), finish, pending)
    after = [finish(pending[0], [pending[1]])]

    stacked = {k: jnp.stack([small[l][k] for l in range(DEPTH)]) for k in small[0]}
    part = {"ln_attn": stacked["ln_attn"][:, 0], "sink_b": stacked["sink_b"], "rpb_c": stacked["rpb_c"],
            "mix_gain": stacked["mix_gain"][:, 0], "ln_ffn": stacked["ln_ffn"][:, 0],
            "conv_b": stacked["conv_b"].reshape(DEPTH, 2 * D_FF), "ln_final": d_ln_final[0],
            "conv_w": stacked["conv_w"].transpose(0, 2, 1, 3).reshape(DEPTH, 3, 2 * D_FF)}
    small_names = REPLICATED_NAMES + ("conv_w",)
    small_send, small_recv, small_vec, small_slots, token = _small_start(
        _pack([part[k] for k in small_names], 256), after, "small_grads_start")
    after = [token]

    grads, delta, new_m, new_v = {}, {}, {}, {}
    updated = dict.fromkeys(BIG_NAMES)

    def arrive(idx, after):
        group = reductions[idx]
        pairs, lands = _reduce_wait(group["send"], group["recv"], group["pairs"], group["lands"], after,
                                    "reduce_wait_%d" % idx)
        halves = [_reduce_sum(pair, landed, "reduce_sum_" + k) for k, pair, landed in zip(group["names"], pairs, lands)]
        send_sem, recv_sem, halves, token = _half_gather_start(halves, "half_gather_start_%d" % idx)
        return {"idx": idx, "send": send_sem, "recv": recv_sem, "bufs": halves, "names": group["names"],
                "layer": group["layer"]}, [token]

    def update(swap, after):
        whole = _half_gather_wait(swap["send"], swap["recv"], swap["bufs"], after,
                                  "half_gather_wait_%d" % swap["idx"])
        for k, g in zip(swap["names"], whole):
            updated[k] = _adamw_layer(w[k], g, m[k], v[k], swap["layer"], updated[k], "adamw_" + k)
        return [updated[k][0] for k in swap["names"]]

    swaps = []
    for idx in range(len(reductions) - 1):
        swap, after = arrive(idx, after)
        swaps.append(swap)
    for swap in swaps[:2]:
        after = update(swap, after)
    swap, after = arrive(len(reductions) - 1, after)
    for swap in swaps[2:] + [swap]:
        after = update(swap, after)
    for k in BIG_NAMES:
        grads[k], delta[k], new_m[k], new_v[k] = updated[k]

    small_vec, small_slots = _small_wait(small_send, small_recv, small_vec, small_slots, after, "small_grads_wait")
    total = _small_sum(small_vec, small_slots, "small_grads_sum")
    for k, g in zip(small_names, _unpack(total, [part[k].shape for k in small_names])):
        grads[k] = g
    grads["conv_w"] = lax.dynamic_slice_in_dim(grads["conv_w"], shard * up_cols, up_cols, axis=2)

    flat = (DEPTH * 3, up_cols)
    res = _adamw(conv_w.reshape(flat), grads["conv_w"].reshape(flat), m["conv_w"].reshape(flat),
                 v["conv_w"].reshape(flat), "adamw_conv_w")
    delta["conv_w"], new_m["conv_w"], new_v["conv_w"] = (r.reshape(conv_w.shape) for r in res)
    shapes = [w[k].shape for k in REPLICATED_NAMES]
    packed = [_pack([d[k] for k in REPLICATED_NAMES], 128) for d in (w, grads, m, v)]
    for d, res in zip((delta, new_m, new_v), _adamw(*packed, "adamw_small")):
        for k, r in zip(REPLICATED_NAMES, _unpack(res, shapes)):
            d[k] = r

    return (loss, dx[None], *[grads[k] for k in WEIGHT_NAMES], *[delta[k] for k in WEIGHT_NAMES],
            *[new_m[k] for k in WEIGHT_NAMES], *[new_v[k] for k in WEIGHT_NAMES])
```

```python
import functools
import math

import jax
import jax.numpy as jnp
from jax import lax
from jax.experimental import pallas as pl
from jax.experimental.pallas import tpu as pltpu

F32 = jnp.float32
BF16 = jnp.bfloat16
MESH = pl.DeviceIdType.MESH

D_MODEL = 2048
SEQ = 2048
DEPTH = 2
HEAD_DIM = 64
N_HEADS_A = 12
N_HEADS_B = 10
N_KV_B = 2
N_HEADS_C = 10
WINDOW_B = 128
GRID_W = 64
NA_ROWS = 8
NA_COLS = 16
WIDTH_A = N_HEADS_A * HEAD_DIM
WIDTH_B = N_HEADS_B * HEAD_DIM
WIDTH_C = N_HEADS_C * HEAD_DIM
IN_COLS = 5120
D_FF = 5632
ROPE_THETA = 10000.0
EPS = 1e-6
NEG_INF = -1e30
N_SHARDS = 4

ADAM_LR = 0.001
ADAM_B1 = 0.9
ADAM_B2 = 0.999
ADAM_EPS = 1e-08
ADAM_WD = 0.01
ADAM_STEP = 10

LANES = 128
QB = 256
NQB = SEQ // QB
ROWS = 256
MIB = 2 ** 20

A_BLK = (0, 6, 12)
B_BLK = (18, 23, 24)
C_BLK = (25, 30, 35)
ROPE_BLKS = tuple(range(0, 12)) + tuple(range(18, 24))
QSCALE_BLKS = tuple(range(0, 6)) + tuple(range(18, 23)) + tuple(range(25, 30))
N_PBLK = IN_COLS // LANES


def _params(sem, vmem_mib):
    return pltpu.CompilerParams(dimension_semantics=sem, vmem_limit_bytes=vmem_mib * MIB)


def _weight_spec(w, cols, t_in, t_out, transposed):
    s, r, c = w.shape
    if cols:
        per = c // t_out
        k_dim, n = r, s * c
        if transposed:
            index = lambda j, rr: (rr // per, j, rr % per)
        else:
            index = lambda j, kk: (j // per, kk, j % per)
    else:
        per = r // t_in
        k_dim, n = s * r, c
        if transposed:
            index = lambda j, rr: (j // per, j % per, rr)
        else:
            index = lambda j, kk: (kk // per, kk % per, j)
    return pl.BlockSpec((None, t_in, t_out), index), k_dim, n


def _mm_nn(a, w, *, cols, tn, tk, out_dtype, name, residual=None, out_split=1):
    m, k_dim = a.shape
    w_spec, k_w, n = _weight_spec(w, cols, tk, tn, False)
    assert k_w == k_dim
    nj, nk = n // tn, k_dim // tk
    in_specs = [pl.BlockSpec((m, tk), lambda j, k: (0, k)), w_spec]
    args = [a, w]
    if residual is not None:
        in_specs.append(pl.BlockSpec((m, tn), lambda j, k: (0, j)))
        args.append(residual)
    if out_split > 1:
        per_o = n // out_split // tn
        out_spec = pl.BlockSpec((None, m, tn), lambda j, k: (j // per_o, 0, j % per_o))
        out_shape = pltpu.HBM((out_split, m, n // out_split), out_dtype)
    else:
        out_spec = pl.BlockSpec((m, tn), lambda j, k: (0, j))
        out_shape = pltpu.HBM((m, n), out_dtype)

    def body(*refs):
        a_ref, w_ref = refs[0], refs[1]
        r_ref = refs[2] if residual is not None else None
        o_ref = refs[3] if residual is not None else refs[2]

        def finish(val):
            if r_ref is not None:
                val = r_ref[...] + val
            o_ref[...] = val.astype(o_ref.dtype)

        part = jnp.dot(a_ref[...], w_ref[...], preferred_element_type=F32)
        if nk == 1:
            finish(part)
        else:
            acc = refs[-1]
            kk = pl.program_id(1)

            @pl.when(kk == 0)
            def _():
                acc[...] = part

            @pl.when(kk > 0)
            def _():
                acc[...] += part

            @pl.when(kk == nk - 1)
            def _():
                finish(acc[...])

    return pl.pallas_call(
        body, name=name, grid=(nj, nk), in_specs=in_specs, out_specs=out_spec, out_shape=out_shape,
        scratch_shapes=[pltpu.VMEM((m, tn), F32)] if nk > 1 else [],
        compiler_params=_params(("arbitrary", "arbitrary"), 56),
    )(*[_in_hbm(a) for a in args])


ANY_SPEC = pl.BlockSpec(memory_space=pl.ANY)


def _mm_nt(dy, w, *, cols, to, tr, out_dtype, name, after=()):
    if dy.ndim == 3:
        m = dy.shape[1]
        n = dy.shape[0] * dy.shape[2]
        per_d = dy.shape[2] // tr
        dy_spec = pl.BlockSpec((None, m, tr), lambda j, r: (r // per_d, 0, r % per_d))
    else:
        m, n = dy.shape
        dy_spec = pl.BlockSpec((m, tr), lambda j, r: (0, r))
    w_spec, k_dim, n_w = _weight_spec(w, cols, to, tr, True)
    assert n_w == n
    nj, nr = k_dim // to, n // tr

    n_after = len(after)

    def body(dy_ref, w_ref, *rest):
        o_ref = rest[n_after]
        part = lax.dot_general(dy_ref[...], w_ref[...], (((1,), (1,)), ((), ())), preferred_element_type=F32)
        if nr == 1:
            o_ref[...] = part.astype(o_ref.dtype)
        else:
            acc = rest[n_after + 1]
            rr = pl.program_id(1)

            @pl.when(rr == 0)
            def _():
                acc[...] = part

            @pl.when(rr > 0)
            def _():
                acc[...] += part

            @pl.when(rr == nr - 1)
            def _():
                o_ref[...] = acc[...].astype(o_ref.dtype)

    return pl.pallas_call(
        body, name=name, grid=(nj, nr), in_specs=[dy_spec, w_spec] + [ANY_SPEC] * n_after,
        out_specs=pl.BlockSpec((m, to), lambda j, r: (0, j)),
        out_shape=pltpu.HBM((m, k_dim), out_dtype),
        scratch_shapes=[pltpu.VMEM((m, to), F32)] if nr > 1 else [],
        compiler_params=_params(("arbitrary", "arbitrary"), 56),
    )(_in_hbm(dy), _in_hbm(w), *after)


def _mm_tn(x, dy, *, tk, tn, shards, name):
    m, k_dim = x.shape
    if dy.ndim == 3:
        n = dy.shape[0] * dy.shape[2]
        per_d = dy.shape[2] // tn
        dy_spec = pl.BlockSpec((None, m, tn), lambda i, j: (j // per_d, 0, j % per_d))
    else:
        n = dy.shape[1]
        dy_spec = pl.BlockSpec((m, tn), lambda i, j: (0, j))
    if shards > 0:
        per = n // shards // tn
        out_shape = pltpu.HBM((shards, k_dim, n // shards), BF16)
        out_spec = pl.BlockSpec((None, tk, tn), lambda i, j: (j // per, i, j % per))
    else:
        s = -shards
        per = k_dim // s // tk
        out_shape = pltpu.HBM((s, k_dim // s, n), BF16)
        out_spec = pl.BlockSpec((None, tk, tn), lambda i, j: (i // per, i % per, j))

    def body(x_ref, dy_ref, o_ref):
        o_ref[...] = lax.dot_general(x_ref[...], dy_ref[...], (((0,), (0,)), ((), ())),
                                     preferred_element_type=F32).astype(BF16)

    return pl.pallas_call(
        body, name=name, grid=(k_dim // tk, n // tn),
        in_specs=[pl.BlockSpec((m, tk), lambda i, j: (0, i)), dy_spec], out_specs=out_spec, out_shape=out_shape,
        compiler_params=_params(("arbitrary", "arbitrary"), 56),
    )(_in_hbm(x), _in_hbm(dy))


def _row_spec(width, rows=ROWS):
    return pl.BlockSpec((rows, width), lambda i: (i, 0))


def _vec_spec(width):
    return pl.BlockSpec((1, width), lambda i: (0, 0))


def _rms_stats(x):
    r = lax.rsqrt(jnp.mean(x * x, axis=-1, keepdims=True) + EPS)
    return r, x * r


def _rmsnorm_fwd(x, gain, name):
    t, d = x.shape

    def body(x_ref, g_ref, o_ref):
        _, n = _rms_stats(x_ref[...])
        o_ref[...] = (n * g_ref[...]).astype(BF16)

    return pl.pallas_call(
        body, name=name, grid=(t // ROWS,), in_specs=[_row_spec(d), _vec_spec(d)], out_specs=_row_spec(d),
        out_shape=pltpu.HBM((t, d), BF16), compiler_params=_params(("arbitrary",), 32),
    )(x, _in_hbm(gain))


def _rmsnorm_bwd(x, gain, dh, dres, name, after=()):
    t, d = x.shape
    n_after = len(after)

    def body(x_ref, g_ref, dh_ref, dres_ref, *rest):
        dx_ref, dxb_ref, dg_ref = rest[n_after:]
        r, n = _rms_stats(x_ref[...])
        dh_v = dh_ref[...]
        dn = dh_v * g_ref[...]
        dx = dres_ref[...] + r * (dn - n * jnp.mean(dn * n, axis=-1, keepdims=True))
        dx_ref[...] = dx
        dxb_ref[...] = dx.astype(BF16)
        part = jnp.sum(dh_v * n, axis=0, keepdims=True)

        @pl.when(pl.program_id(0) == 0)
        def _():
            dg_ref[...] = part

        @pl.when(pl.program_id(0) > 0)
        def _():
            dg_ref[...] += part

    return pl.pallas_call(
        body, name=name, grid=(t // ROWS,),
        in_specs=[_row_spec(d), _vec_spec(d), _row_spec(d), _row_spec(d)] + [ANY_SPEC] * n_after,
        out_specs=[_row_spec(d), _row_spec(d), _vec_spec(d)],
        out_shape=[pltpu.HBM((t, d), F32), pltpu.HBM((t, d), BF16), jax.ShapeDtypeStruct((1, d), F32)],
        compiler_params=_params(("arbitrary",), 40),
    )(x, _in_hbm(gain), dh, dres, *after)


def _loss_head(x, gain, target, name):
    t, d = x.shape

    def body(x_ref, g_ref, t_ref, loss_ref, dx_ref, dxb_ref, dg_ref):
        r, n = _rms_stats(x_ref[...])
        g = g_ref[...]
        err = n * g - t_ref[...]
        dy = err * (1.0 / d)
        dn = dy * g
        dx = r * (dn - n * jnp.mean(dn * n, axis=-1, keepdims=True))
        dx_ref[...] = dx
        dxb_ref[...] = dx.astype(BF16)
        part = jnp.sum(dy * n, axis=0, keepdims=True)
        lpart = jnp.zeros((8, LANES), F32) + 0.5 * jnp.sum(jnp.mean(err * err, axis=-1, keepdims=True))

        @pl.when(pl.program_id(0) == 0)
        def _():
            dg_ref[...] = part
            loss_ref[...] = lpart

        @pl.when(pl.program_id(0) > 0)
        def _():
            dg_ref[...] += part
            loss_ref[...] += lpart

    return pl.pallas_call(
        body, name=name, grid=(t // ROWS,),
        in_specs=[_row_spec(d), _vec_spec(d), _row_spec(d)],
        out_specs=[pl.BlockSpec((8, LANES), lambda i: (0, 0)), _row_spec(d), _row_spec(d), _vec_spec(d)],
        out_shape=[jax.ShapeDtypeStruct((8, LANES), F32), pltpu.HBM((t, d), F32), pltpu.HBM((t, d), BF16),
                   jax.ShapeDtypeStruct((1, d), F32)],
        compiler_params=_params(("arbitrary",), 40),
    )(x, gain, target)


def _swap_halves(x):
    lane = lax.broadcasted_iota(jnp.int32, x.shape, 1)
    return jnp.where((lane % HEAD_DIM) < HEAD_DIM // 2, pltpu.roll(x, LANES - HEAD_DIM // 2, 1),
                     pltpu.roll(x, HEAD_DIM // 2, 1))


def _rope_tables(t):
    inv_freq = ROPE_THETA ** (-jnp.arange(0, HEAD_DIM, 2, dtype=F32) / HEAD_DIM)
    ang = jnp.arange(t, dtype=F32)[:, None] * inv_freq[None, :]
    cos = jnp.tile(jnp.cos(ang), (1, LANES // (HEAD_DIM // 2)))
    sin = jnp.tile(jnp.sin(ang), (1, LANES // (HEAD_DIM // 2)))
    lane = jnp.arange(LANES)[None, :]
    return cos, jnp.where((lane % HEAD_DIM) < HEAD_DIM // 2, -sin, sin)


def _rope_fwd(proj, cos, sin, name):
    t = proj.shape[0]
    scale = HEAD_DIM ** -0.5

    def body(p_ref, c_ref, s_ref, o_ref):
        cos_v, sin_v = c_ref[...], s_ref[...]
        for b in range(N_PBLK):
            cols = slice(b * LANES, (b + 1) * LANES)
            v = p_ref[:, cols]
            if b in ROPE_BLKS:
                v = v * cos_v + _swap_halves(v) * sin_v
            if b in QSCALE_BLKS:
                v = v * scale
            o_ref[:, cols] = v.astype(BF16)

    return pl.pallas_call(
        body, name=name, grid=(t // ROWS,),
        in_specs=[_row_spec(IN_COLS), _row_spec(LANES), _row_spec(LANES)], out_specs=_row_spec(IN_COLS),
        out_shape=pltpu.HBM((t, IN_COLS), BF16), compiler_params=_params(("arbitrary",), 40),
    )(_in_hbm(proj), _in_hbm(cos), _in_hbm(sin))


def _rope_bwd(grads, cos, sin, name):
    t = grads[0].shape[0]
    scale = HEAD_DIM ** -0.5
    group = N_HEADS_B // N_KV_B

    def body(*refs):
        c_ref, s_ref, o_ref = refs[9], refs[10], refs[11]
        cos_v, sin_v = c_ref[...], s_ref[...]

        def kv_sum(ref):
            parts = []
            for g in range(N_KV_B):
                acc = ref[:, g * group * HEAD_DIM:(g * group + 1) * HEAD_DIM]
                for h in range(g * group + 1, (g + 1) * group):
                    acc = acc + ref[:, h * HEAD_DIM:(h + 1) * HEAD_DIM]
                parts.append(acc)
            return jnp.concatenate(parts, axis=1)

        def emit(b, v):
            if b in ROPE_BLKS:
                v = v * cos_v - _swap_halves(v) * sin_v
            if b in QSCALE_BLKS:
                v = v * scale
            o_ref[:, b * LANES:(b + 1) * LANES] = v.astype(BF16)

        starts = (A_BLK[0], A_BLK[1], A_BLK[2], B_BLK[0], None, None, C_BLK[0], C_BLK[1], C_BLK[2])
        for idx, start in enumerate(starts):
            if start is None:
                continue
            for j in range(refs[idx].shape[1] // LANES):
                emit(start + j, refs[idx][:, j * LANES:(j + 1) * LANES])
        emit(B_BLK[1], kv_sum(refs[4]))
        emit(B_BLK[2], kv_sum(refs[5]))

    return pl.pallas_call(
        body, name=name, grid=(t // ROWS,),
        in_specs=[_row_spec(g.shape[1]) for g in grads] + [_row_spec(LANES), _row_spec(LANES)],
        out_specs=_row_spec(IN_COLS),
        out_shape=pltpu.HBM((t, IN_COLS), BF16), compiler_params=_params(("arbitrary",), 40),
    )(*[_in_hbm(g) for g in grads], _in_hbm(cos), _in_hbm(sin))


GROUP_COLS = ((0, WIDTH_A), (WIDTH_A, WIDTH_A + WIDTH_B), (WIDTH_A + WIDTH_B, D_MODEL))


def _mix_fwd(oa, ob, oc, gain, name):
    t = oa.shape[0]

    def body(a_ref, b_ref, c_ref, g_ref, o_ref):
        for ref, (lo, hi) in zip((a_ref, b_ref, c_ref), GROUP_COLS):
            _, n = _rms_stats(ref[...])
            o_ref[:, lo:hi] = (n * g_ref[:, lo:hi]).astype(BF16)

    return pl.pallas_call(
        body, name=name, grid=(t // ROWS,),
        in_specs=[_row_spec(WIDTH_A), _row_spec(WIDTH_B), _row_spec(WIDTH_C), _vec_spec(D_MODEL)],
        out_specs=_row_spec(D_MODEL),
        out_shape=pltpu.HBM((t, D_MODEL), BF16), compiler_params=_params(("arbitrary",), 32),
    )(oa, ob, oc, _in_hbm(gain))


def _mix_bwd(oa, ob, oc, gain, dmixed, name, after=()):
    t = oa.shape[0]
    n_after = len(after)

    def body(a_ref, b_ref, c_ref, g_ref, dm_ref, *rest):
        da_ref, db_ref, dc_ref, dg_ref = rest[n_after:]
        first = pl.program_id(0) == 0
        for ref, dref, (lo, hi) in zip((a_ref, b_ref, c_ref), (da_ref, db_ref, dc_ref), GROUP_COLS):
            r, n = _rms_stats(ref[...])
            dm = dm_ref[:, lo:hi]
            dn = dm * g_ref[:, lo:hi]
            dref[...] = r * (dn - n * jnp.mean(dn * n, axis=-1, keepdims=True))
            part = jnp.sum(dm * n, axis=0, keepdims=True)

            @pl.when(first)
            def _():
                dg_ref[:, lo:hi] = part

            @pl.when(jnp.logical_not(first))
            def _():
                dg_ref[:, lo:hi] += part

    return pl.pallas_call(
        body, name=name, grid=(t // ROWS,),
        in_specs=[_row_spec(WIDTH_A), _row_spec(WIDTH_B), _row_spec(WIDTH_C), _vec_spec(D_MODEL), _row_spec(D_MODEL)]
        + [ANY_SPEC] * n_after,
        out_specs=[_row_spec(WIDTH_A), _row_spec(WIDTH_B), _row_spec(WIDTH_C), _vec_spec(D_MODEL)],
        out_shape=[pltpu.HBM((t, WIDTH_A), F32), pltpu.HBM((t, WIDTH_B), F32), pltpu.HBM((t, WIDTH_C), F32),
                   jax.ShapeDtypeStruct((1, D_MODEL), F32)],
        compiler_params=_params(("arbitrary",), 40),
    )(oa, ob, oc, _in_hbm(gain), dmixed, *after)


FF_COLS = 256


SUBLANES = 8
CHUNK_FWD = 256
CHUNK_BWD = 128
HALO = SUBLANES


def _ext_rows(ref, r0, chunk, where):
    t, cols = ref.shape
    zeros = jnp.zeros((HALO, cols), F32)
    if where == "first":
        return jnp.concatenate([zeros, ref[0:chunk + HALO, :]], axis=0)
    if where == "last":
        return jnp.concatenate([ref[t - chunk - HALO:t, :], zeros], axis=0)
    return ref[pl.ds(pl.multiple_of(r0 - HALO, HALO), chunk + 2 * HALO), :]


def _for_chunks(t, chunk, fn):
    fn(0, "first")

    def mid(ci, carry):
        fn(pl.multiple_of(ci * chunk, chunk), "mid")
        return carry

    lax.fori_loop(1, t // chunk - 1, mid, 0)
    fn(t - chunk, "last")


def _roll_rows(x, by):
    return pltpu.roll(x, by % x.shape[0], 0)


def _gate_val(u_ref, r0, chunk, where, w_ref, b_ref):
    ext = [_ext_rows(u_ref.at[h], r0, chunk, where) for h in range(2)]
    before = [_roll_rows(e, 1) for e in ext]
    after = [_roll_rows(e, -1) for e in ext]
    gate, val = ((before[h] * w_ref[h, 0:1, :] + ext[h] * w_ref[h, 1:2, :]) + after[h] * w_ref[h, 2:3, :] + b_ref[h]
                 for h in range(2))
    return gate, val, ext, before, after


def _ff_specs(t):
    u_spec = pl.BlockSpec((2, t, FF_COLS), lambda j: (0, 0, j))
    w_spec = pl.BlockSpec((2, 3, FF_COLS), lambda j: (0, 0, j))
    b_spec = pl.BlockSpec((2, 1, FF_COLS), lambda j: (0, 0, j))
    return u_spec, w_spec, b_spec


def _convgate_fwd(u0, conv_w, conv_b, name):
    t = u0.shape[1]
    u_spec, w_spec, b_spec = _ff_specs(t)

    def body(u_ref, w_ref, b_ref, o_ref):
        def chunk(r0, where):
            gate, val, _, _, _ = _gate_val(u_ref, r0, CHUNK_FWD, where, w_ref, b_ref)
            act = gate * jax.nn.sigmoid(gate) * val
            o_ref[pl.ds(r0, CHUNK_FWD), :] = act[HALO:HALO + CHUNK_FWD].astype(BF16)

        _for_chunks(t, CHUNK_FWD, chunk)

    return pl.pallas_call(
        body, name=name, grid=(D_FF // FF_COLS,), in_specs=[u_spec, w_spec, b_spec],
        out_specs=pl.BlockSpec((t, FF_COLS), lambda j: (0, j)),
        out_shape=pltpu.HBM((t, D_FF), BF16), compiler_params=_params(("arbitrary",), 48),
    )(_in_hbm(u0), conv_w, conv_b)


def _convgate_bwd(u0, conv_w, conv_b, d_act, name):
    t = u0.shape[1]
    u_spec, w_spec, b_spec = _ff_specs(t)

    def body(u_ref, w_ref, b_ref, da_ref, du_ref, dw_ref, db_ref, sums_ref):
        sums_ref[...] = jnp.zeros_like(sums_ref)
        inner = slice(HALO, HALO + CHUNK_BWD)

        def fold(x):
            return jnp.sum(x.reshape(CHUNK_BWD // SUBLANES, SUBLANES, x.shape[1]), axis=0)

        def chunk(r0, where):
            gate, val, ext, before, after = _gate_val(u_ref, r0, CHUNK_BWD, where, w_ref, b_ref)
            sig = jax.nn.sigmoid(gate)
            da = _ext_rows(da_ref, r0, CHUNK_BWD, where)
            d_half = (da * val * (sig * (1.0 + gate * (1.0 - sig))), da * (gate * sig))
            for h in range(2):
                du = d_half[h]
                for k, term in enumerate((du, du * before[h], du * ext[h], du * after[h])):
                    sums_ref[h, k] += fold(term[inner])
                du0 = (_roll_rows(du, -1) * w_ref[h, 0:1, :] + du * w_ref[h, 1:2, :]) + _roll_rows(du, 1) * w_ref[h, 2:3, :]
                du_ref[h, pl.ds(r0, CHUNK_BWD), :] = du0[inner].astype(BF16)

        _for_chunks(t, CHUNK_BWD, chunk)
        for h in range(2):
            db_ref[h] = jnp.sum(sums_ref[h, 0], axis=0, keepdims=True)
            for k in range(3):
                dw_ref[h, k:k + 1, :] = jnp.sum(sums_ref[h, k + 1], axis=0, keepdims=True)

    return pl.pallas_call(
        body, name=name, grid=(D_FF // FF_COLS,),
        in_specs=[u_spec, w_spec, b_spec, pl.BlockSpec((t, FF_COLS), lambda j: (0, j))],
        out_specs=[u_spec, w_spec, b_spec],
        out_shape=[pltpu.HBM((2, t, D_FF), BF16), jax.ShapeDtypeStruct((2, 3, D_FF), F32),
                   jax.ShapeDtypeStruct((2, 1, D_FF), F32)],
        scratch_shapes=[pltpu.VMEM((2, 4, SUBLANES, FF_COLS), F32)],
        compiler_params=_params(("arbitrary",), 56),
    )(_in_hbm(u0), conv_w, conv_b, _in_hbm(d_act))


class _Group:
    def __init__(self, heads, blks, kv_rows, n_win, gqa, bias_per_head):
        self.heads = heads
        self.pairs = heads // 2
        self.q_blk, self.k_blk, self.v_blk = blks
        self.kv_rows = kv_rows
        self.n_win = n_win
        self.full = kv_rows == SEQ
        self.gqa = gqa
        self.bias_per_head = bias_per_head
        self.width = heads * HEAD_DIM
        self.keys = kv_rows * n_win


GROUP_A = _Group(N_HEADS_A, A_BLK, SEQ, 1, False, False)
GROUP_B = _Group(N_HEADS_B, B_BLK, WINDOW_B, 4, True, False)
GROUP_C = _Group(N_HEADS_C, C_BLK, QB, 3, False, True)


def _win_start(grp, i):
    return jnp.clip(i * (QB // grp.kv_rows) - 1, 0, SEQ // grp.kv_rows - grp.n_win)


def _win_variant(i):
    return jnp.minimum(i, 1) + (i == NQB - 1).astype(jnp.int32)


def _attn_in_specs(grp, t):
    q_spec = pl.BlockSpec((QB, LANES), lambda p, i: (i, grp.q_blk + p))

    def col(blk):
        return (lambda p: blk) if grp.gqa else (lambda p: blk + p)

    def kv_specs(blk):
        c = col(blk)
        if grp.full:
            return [pl.BlockSpec((t, LANES), lambda p, i: (0, c(p)))]
        return [pl.BlockSpec((grp.kv_rows, LANES),
                             functools.partial(lambda p, i, w: (_win_start(grp, i) + w, c(p)), w=w))
                for w in range(grp.n_win)]

    nwk = grp.keys
    if grp.bias_per_head:
        bias_spec = pl.BlockSpec((2, None, QB, nwk), lambda p, i: (p, _win_variant(i), 0, 0))
    elif grp.full:
        bias_spec = pl.BlockSpec((1, None, QB, nwk), lambda p, i: (0, i, 0, 0))
    else:
        bias_spec = pl.BlockSpec((1, None, QB, nwk), lambda p, i: (0, _win_variant(i), 0, 0))
    sink_spec = pl.BlockSpec((1, LANES), lambda p, i: (0, p))
    return q_spec, kv_specs(grp.k_blk), kv_specs(grp.v_blk), bias_spec, sink_spec


def _head_kv(grp, whole, e, p):
    lo, hi = whole[:, :HEAD_DIM], whole[:, HEAD_DIM:]
    if grp.gqa:
        return jnp.where(2 * p + e >= N_HEADS_B // N_KV_B, hi, lo)
    return hi if e else lo


def _softmax_parts(q, k, bias, sink):
    s = lax.dot_general(q, k, (((1,), (1,)), ((), ())), preferred_element_type=F32) + bias
    m = jnp.maximum(jnp.max(s, axis=-1, keepdims=True), sink)
    pe = jnp.exp(s - m)
    denom = jnp.sum(pe, axis=-1, keepdims=True) + jnp.exp(sink - m)
    return pe, m, 1.0 / denom


def _attn_fwd(grp, proj, bias, sink, name):
    t = proj.shape[0]
    q_spec, k_specs, v_specs, bias_spec, sink_spec = _attn_in_specs(grp, t)
    nkv = len(k_specs)

    def body(*refs):
        q_ref = refs[0]
        k_refs, v_refs = refs[1:1 + nkv], refs[1 + nkv:1 + 2 * nkv]
        bias_ref, sink_ref, o_ref = refs[1 + 2 * nkv:4 + 2 * nkv]
        p = pl.program_id(0)
        k_all = jnp.concatenate([r[...] for r in k_refs], axis=0)
        v_all = jnp.concatenate([r[...] for r in v_refs], axis=0)
        outs = []
        for e in range(2):
            q = q_ref[:, e * HEAD_DIM:(e + 1) * HEAD_DIM]
            k = _head_kv(grp, k_all, e, p)
            v = _head_kv(grp, v_all, e, p)
            snk = sink_ref[0:1, e * HEAD_DIM:e * HEAD_DIM + 1]
            pe, _, inv = _softmax_parts(q, k, bias_ref[e if grp.bias_per_head else 0], snk)
            outs.append(jnp.dot(pe.astype(BF16), v, preferred_element_type=F32) * inv)
        o_ref[...] = jnp.concatenate(outs, axis=1)

    return pl.pallas_call(
        body, name=name, grid=(grp.pairs, NQB),
        in_specs=[q_spec, *k_specs, *v_specs, bias_spec, sink_spec],
        out_specs=pl.BlockSpec((QB, LANES), lambda p, i: (i, p)),
        out_shape=pltpu.HBM((t, grp.width), F32),
        compiler_params=_params(("arbitrary", "arbitrary"), 48),
    )(*([_in_hbm(proj)] * (1 + 2 * nkv)), _in_hbm(bias), sink)


def _attn_bwd(grp, proj, bias, sink, out, d_out, name):
    t = proj.shape[0]
    q_spec, k_specs, v_specs, bias_spec, sink_spec = _attn_in_specs(grp, t)
    nkv = len(k_specs)
    n_off = 2 * NA_ROWS - 1
    rows_q = QB // GRID_W
    wide = grp.keys > 2 * QB
    o_spec = pl.BlockSpec((QB, LANES), lambda p, i: (i, p))
    acc_spec = pl.BlockSpec((t, LANES), lambda p, i: (0, p))
    out_specs = [o_spec, acc_spec, acc_spec, pl.BlockSpec((None, 8, LANES), lambda p, i: (p, 0, 0))]
    out_shape = [pltpu.HBM((t, grp.width), F32)] * 3 + [jax.ShapeDtypeStruct((grp.pairs, 8, LANES), F32)]
    if grp.bias_per_head:
        out_specs.append(pl.BlockSpec((2, n_off, GRID_W, GRID_W), lambda p, i: (p, 0, 0, 0)))
        out_shape.append(jax.ShapeDtypeStruct((grp.heads, n_off, GRID_W, GRID_W), F32))

    def body(*refs):
        q_ref = refs[0]
        k_refs, v_refs = refs[1:1 + nkv], refs[1 + nkv:1 + 2 * nkv]
        bias_ref, sink_ref, o_ref, do_ref = refs[1 + 2 * nkv:5 + 2 * nkv]
        dq_ref, dk_ref, dv_ref, dsink_ref = refs[5 + 2 * nkv:9 + 2 * nkv]
        dbias_ref = refs[9 + 2 * nkv] if grp.bias_per_head else None
        p, i = pl.program_id(0), pl.program_id(1)

        @pl.when(i == 0)
        def _():
            dk_ref[...] = jnp.zeros_like(dk_ref)
            dv_ref[...] = jnp.zeros_like(dv_ref)
            dsink_ref[...] = jnp.zeros_like(dsink_ref)
            if dbias_ref is not None:
                dbias_ref[...] = jnp.zeros_like(dbias_ref)

        k_all = jnp.concatenate([r[...] for r in k_refs], axis=0)
        v_all = jnp.concatenate([r[...] for r in v_refs], axis=0)
        start = 0 if grp.full else _win_start(grp, i)
        dqs, dks, dvs, dsinks = [], [], [], []
        for e in range(2):
            cols = slice(e * HEAD_DIM, (e + 1) * HEAD_DIM)
            q = q_ref[:, cols]
            k = _head_kv(grp, k_all, e, p)
            v = _head_kv(grp, v_all, e, p)
            snk = sink_ref[0:1, e * HEAD_DIM:e * HEAD_DIM + 1]
            pe, m, inv = _softmax_parts(q, k, bias_ref[e if grp.bias_per_head else 0], snk)
            prob = pe * inv
            do = do_ref[:, cols]
            do_b = do.astype(BF16)
            pe_b = prob.astype(BF16)
            delta = jnp.sum(do * o_ref[:, cols], axis=-1, keepdims=True)
            dp = lax.dot_general(do_b, v, (((1,), (1,)), ((), ())), preferred_element_type=F32)
            ds = prob * (dp - delta)
            ds_b = ds.astype(BF16)
            dqs.append(jnp.dot(ds_b, k, preferred_element_type=F32))
            if wide:
                dks.append(lax.dot_general(q, ds_b, (((0,), (0,)), ((), ())), preferred_element_type=F32))
                dvs.append(lax.dot_general(do_b, pe_b, (((0,), (0,)), ((), ())), preferred_element_type=F32))
            else:
                dks.append(lax.dot_general(ds_b, q, (((0,), (0,)), ((), ())), preferred_element_type=F32))
                dvs.append(lax.dot_general(pe_b, do_b, (((0,), (0,)), ((), ())), preferred_element_type=F32))
            dsinks.append(-jnp.sum(jnp.exp(snk - m) * inv * delta, axis=0, keepdims=True))
            if dbias_ref is not None:
                shift = (i * QB - start * grp.kv_rows) // GRID_W
                for rq in range(rows_q):
                    for rk in range(grp.keys // GRID_W):
                        off = jnp.clip(rk - rq + (NA_ROWS - 1) - shift, 0, n_off - 1)
                        dbias_ref[e, off] += ds[rq * GRID_W:(rq + 1) * GRID_W, rk * GRID_W:(rk + 1) * GRID_W]
        dq_ref[...] = jnp.concatenate(dqs, axis=1)
        rows = pl.ds(0, t) if grp.full else pl.ds(pl.multiple_of(start * grp.kv_rows, grp.kv_rows), grp.keys)
        if wide:
            dk_ref[rows, :] += jnp.concatenate(dks, axis=0).T
            dv_ref[rows, :] += jnp.concatenate(dvs, axis=0).T
        else:
            dk_ref[rows, :] += jnp.concatenate(dks, axis=1)
            dv_ref[rows, :] += jnp.concatenate(dvs, axis=1)
        lane = lax.broadcasted_iota(jnp.int32, (8, LANES), 1)
        dsink_ref[...] += jnp.where(lane < HEAD_DIM, dsinks[0], dsinks[1])

    return pl.pallas_call(
        body, name=name, grid=(grp.pairs, NQB),
        in_specs=[q_spec, *k_specs, *v_specs, bias_spec, sink_spec, o_spec, o_spec],
        out_specs=out_specs, out_shape=out_shape,
        compiler_params=_params(("arbitrary", "arbitrary"), 56),
    )(*([_in_hbm(proj)] * (1 + 2 * nkv)), _in_hbm(bias), sink, _in_hbm(out), _in_hbm(d_out))


DILATED_CONFIGS = ((128, 1), (512, 4), (2048, 16))


def _bias_a():
    d = jnp.arange(SEQ)[None, :] - jnp.arange(SEQ)[:, None]
    mult = jnp.zeros((SEQ, SEQ), F32)
    for window, r in DILATED_CONFIGS:
        reach = (window // (2 * r)) * r
        mult = mult + ((d % r == 0) & (jnp.abs(d) <= reach)).astype(F32)
    return jnp.where(mult > 0, jnp.log(jnp.maximum(mult, 1.0)), NEG_INF).reshape(1, NQB, QB, SEQ)


def _bias_b():
    row = jnp.arange(QB)[None, :, None]
    col = jnp.arange(GROUP_B.keys)[None, None, :]
    var = jnp.arange(3)[:, None, None]
    d = col - (GROUP_B.kv_rows * var + row)
    return jnp.where(jnp.abs(d) <= WINDOW_B, 0.0, NEG_INF).astype(F32)[None]


def _offset_onehot():
    c = jnp.arange(GRID_W)[:, None, None]
    c2 = jnp.arange(GRID_W)[None, :, None]
    b = jnp.arange(LANES)[None, None, :]
    return (c2 - c + NA_COLS - 1 == b).astype(BF16).reshape(GRID_W * GRID_W, LANES)


def _split_dot(x, g):
    hi = x.astype(BF16)
    rest = x - hi.astype(F32)
    mid = rest.astype(BF16)
    lo = (rest - mid.astype(F32)).astype(BF16)
    return (jnp.dot(hi, g, preferred_element_type=F32) + jnp.dot(mid, g, preferred_element_type=F32)
            + jnp.dot(lo, g, preferred_element_type=F32))


def _table_mm(x, g, name):
    def body(x_ref, g_ref, o_ref):
        o_ref[...] = _split_dot(x_ref[...], g_ref[...])

    return pl.pallas_call(
        body, name=name, out_shape=jax.ShapeDtypeStruct((x.shape[0], g.shape[1]), F32),
        in_specs=[pl.BlockSpec(memory_space=pltpu.VMEM)] * 2, out_specs=pl.BlockSpec(memory_space=pltpu.VMEM),
        compiler_params=pltpu.CompilerParams(vmem_limit_bytes=32 * MIB),
    )(x, g)


N_OFF = 2 * NA_ROWS - 1
TABLE_ROWS = 152


def _bias_c(rpb):
    table = jnp.zeros((TABLE_ROWS, LANES), F32).at[:N_HEADS_C * N_OFF, :2 * NA_COLS - 1].set(
        rpb.reshape(N_HEADS_C * N_OFF, 2 * NA_COLS - 1))
    tiles = _table_mm(table, _offset_onehot().T, "rpb_tiles")[:N_HEADS_C * N_OFF]
    tiles = tiles.reshape(N_HEADS_C, N_OFF, GRID_W, GRID_W)
    c = jnp.arange(GRID_W)
    col_start = jnp.clip(c - NA_COLS // 2, 0, GRID_W - NA_COLS)
    col_ok = (c[None, :] >= col_start[:, None]) & (c[None, :] < col_start[:, None] + NA_COLS)
    tiles = jnp.where(col_ok, tiles, NEG_INF)
    rows_q = QB // GRID_W
    rows_k = GROUP_C.keys // GRID_W

    def body(t_ref, o_ref):
        for var in range(3):
            for rq in range(rows_q):
                r_l = rows_q * var + rq
                first = min(max(r_l - NA_ROWS // 2, 0), rows_k - NA_ROWS)
                for rk in range(rows_k):
                    if first <= rk < first + NA_ROWS:
                        tile = t_ref[rk - r_l + NA_ROWS - 1]
                    else:
                        tile = jnp.full((GRID_W, GRID_W), NEG_INF, F32)
                    o_ref[var, rq * GRID_W:(rq + 1) * GRID_W, rk * GRID_W:(rk + 1) * GRID_W] = tile

    return pl.pallas_call(
        body, name="bias_c", grid=(N_HEADS_C,),
        in_specs=[pl.BlockSpec((None, N_OFF, GRID_W, GRID_W), lambda h: (h, 0, 0, 0))],
        out_specs=pl.BlockSpec((None, 3, QB, GROUP_C.keys), lambda h: (h, 0, 0, 0)),
        out_shape=jax.ShapeDtypeStruct((N_HEADS_C, 3, QB, GROUP_C.keys), F32),
        compiler_params=_params(("arbitrary",), 32),
    )(tiles)


def _rpb_grad(d_tiles):
    flat = jnp.zeros((TABLE_ROWS, GRID_W * GRID_W), F32).at[:N_HEADS_C * N_OFF].set(
        d_tiles.reshape(N_HEADS_C * N_OFF, GRID_W * GRID_W))
    out = _table_mm(flat, _offset_onehot(), "rpb_grad")
    return out[:N_HEADS_C * N_OFF, :2 * NA_COLS - 1].reshape(N_HEADS_C, N_OFF, 2 * NA_COLS - 1)


def _sink_lanes(sink):
    return jnp.repeat(sink.astype(F32), HEAD_DIM)[None, :]


def _attention_fwd(proj_r, sink_b, bias_a, bias_b, bias_c):
    no_sink_a = jnp.full((1, WIDTH_A), NEG_INF, F32)
    no_sink_c = jnp.full((1, WIDTH_C), NEG_INF, F32)
    oa = _attn_fwd(GROUP_A, proj_r, bias_a, no_sink_a, "attn_a_fwd")
    ob = _attn_fwd(GROUP_B, proj_r, bias_b, _sink_lanes(sink_b), "attn_b_fwd")
    oc = _attn_fwd(GROUP_C, proj_r, bias_c, no_sink_c, "attn_c_fwd")
    return oa, ob, oc


def _attention_bwd(proj_r, sink_b, bias_a, bias_b, bias_c, outs, d_outs, cos, sin):
    no_sink_a = jnp.full((1, WIDTH_A), NEG_INF, F32)
    no_sink_c = jnp.full((1, WIDTH_C), NEG_INF, F32)
    dqa, dka, dva, _ = _attn_bwd(GROUP_A, proj_r, bias_a, no_sink_a, outs[0], d_outs[0], "attn_a_bwd")
    dqb, dkb, dvb, dsink = _attn_bwd(GROUP_B, proj_r, bias_b, _sink_lanes(sink_b), outs[1], d_outs[1], "attn_b_bwd")
    dqc, dkc, dvc, _, d_tiles = _attn_bwd(GROUP_C, proj_r, bias_c, no_sink_c, outs[2], d_outs[2], "attn_c_bwd")
    d_proj = _rope_bwd((dqa, dka, dva, dqb, dkb, dvb, dqc, dkc, dvc), cos, sin, "rope_bwd")
    d_sink = dsink[:, 0, :].reshape(GROUP_B.pairs, 2, HEAD_DIM)[:, :, 0].reshape(N_HEADS_B)
    return d_proj, d_sink, _rpb_grad(d_tiles)


def _adamw(w, g, m, v, name):
    r, c = w.shape
    rows = r
    for cand in (512, 256, 128, 64, 32, 16, 8):
        if r % cand == 0 and cand * c * 4 <= MIB:
            rows = cand
            break
    spec = pl.BlockSpec((rows, c), lambda i: (i, 0))

    def body(w_ref, g_ref, m_ref, v_ref, d_ref, mo_ref, vo_ref):
        d_ref[...], mo_ref[...], vo_ref[...] = _adamw_step(w_ref[...], g_ref[...], m_ref[...], v_ref[...])

    return pl.pallas_call(
        body, name=name, grid=(r // rows,), in_specs=[spec] * 4, out_specs=[spec] * 3,
        out_shape=[jax.ShapeDtypeStruct((r, c), F32)] * 3, compiler_params=_params(("arbitrary",), 32),
    )(w, g, m, v)


def _adamw_step(w, grad, m, v):
    m_new = ADAM_B1 * m + (1.0 - ADAM_B1) * grad
    v_new = ADAM_B2 * v + (1.0 - ADAM_B2) * jnp.square(grad)
    m_hat = m_new / (1.0 - ADAM_B1 ** ADAM_STEP)
    v_hat = v_new / (1.0 - ADAM_B2 ** ADAM_STEP)
    return -ADAM_LR * (m_hat / (jnp.sqrt(v_hat) + ADAM_EPS) + ADAM_WD * w), m_new, v_new


def _adamw_layer(w, g, m, v, layer, prev, name):
    _, r, c = w.shape
    rows = next(cand for cand in (512, 256, 128, 64, 32, 16, 8) if r % cand == 0 and cand * c * 4 <= 2 * MIB)
    spec = pl.BlockSpec((None, rows, c), lambda i: (layer, i, 0))
    g_spec = pl.BlockSpec((rows, c), lambda i: (i, 0))
    n_prev = 0 if prev is None else 4

    def body(w_ref, g_ref, m_ref, v_ref, *rest):
        go_ref, d_ref, mo_ref, vo_ref = rest[n_prev:]
        grad = g_ref[...]
        go_ref[...] = grad
        d_ref[...], mo_ref[...], vo_ref[...] = _adamw_step(w_ref[...], grad, m_ref[...], v_ref[...])

    return pl.pallas_call(
        body, name=name, grid=(r // rows,), in_specs=[spec, g_spec, spec, spec] + [ANY_SPEC] * n_prev,
        out_specs=[spec] * 4,
        out_shape=[jax.ShapeDtypeStruct(w.shape, F32)] * 4,
        input_output_aliases={4 + i: i for i in range(n_prev)}, compiler_params=_params(("arbitrary",), 48),
    )(w, g, m, v, *(prev or ()))


def _layer_fwd(x0, p, weight, tabs):
    h1 = _rmsnorm_fwd(x0, p["ln_attn"], "ln_attn_fwd")
    proj = _mm_nn(h1, weight("w_in", h1), cols=True, tn=256, tk=D_MODEL, out_dtype=F32, name="mm_in")
    proj_r = _rope_fwd(proj, tabs["cos"], tabs["sin"], "rope_fwd")
    outs = _attention_fwd(proj_r, p["sink_b"], tabs["bias_a"], tabs["bias_b"], p["bias_c"])
    mixed = _mix_fwd(*outs, p["mix_gain"], "mix_fwd")
    x1 = _mm_nn(mixed, weight("w_out", mixed), cols=False, tn=256, tk=D_MODEL, out_dtype=F32, name="mm_out",
                residual=x0)
    h2 = _rmsnorm_fwd(x1, p["ln_ffn"], "ln_ffn_fwd")
    u0 = _mm_nn(h2, weight("w_up", h2), cols=True, tn=256, tk=D_MODEL, out_dtype=F32, name="mm_up", out_split=2)
    act = _convgate_fwd(u0, p["conv_w"], p["conv_b"], "convgate_fwd")
    x2 = _mm_nn(act, weight("w_down", act), cols=False, tn=512, tk=D_FF // 2, out_dtype=F32, name="mm_down",
                residual=x1)
    return x2, (x0, h1, proj_r, outs, mixed, x1, h2, u0, act)


def _layer_bwd(dx2, dx2_b, saved, p, big, tabs, begin, finish, pending):
    x0, h1, proj_r, outs, mixed, x1, h2, u0, act = saved
    d_act = _mm_nt(dx2_b, big["w_down"], cols=False, to=512, tr=D_MODEL, out_dtype=F32, name="nt_down",
                   after=[pending[1]] if pending else [])
    g_down = _mm_tn(act, dx2_b, tk=D_FF // N_SHARDS, tn=D_MODEL, shards=-N_SHARDS, name="tn_down")
    du0, d_conv_w, d_conv_b = _convgate_bwd(u0, p["conv_w"], p["conv_b"], d_act, "convgate_bwd")
    token = [finish(pending[0], [du0])] if pending else []
    dh2 = _mm_nt(du0, big["w_up"], cols=True, to=1024, tr=D_FF // 4, out_dtype=F32, name="nt_up", after=token)
    g_up = _mm_tn(h2, du0, tk=1024, tn=D_FF // 4, shards=N_SHARDS, name="tn_up")
    first, token = begin({"w_down": g_down, "w_up": g_up})
    dx1, dx1_b, d_ln_ffn = _rmsnorm_bwd(x1, p["ln_ffn"], dh2, dx2, "ln_ffn_bwd", after=[token])
    d_mixed = _mm_nt(dx1_b, big["w_out"], cols=False, to=512, tr=D_MODEL, out_dtype=F32, name="nt_out")
    g_out = _mm_tn(mixed, dx1_b, tk=D_MODEL // N_SHARDS, tn=D_MODEL, shards=-N_SHARDS, name="tn_out")
    token = finish(first, [g_out])
    *d_outs, d_mix_gain = _mix_bwd(*outs, p["mix_gain"], d_mixed, "mix_bwd", after=[token])
    d_proj, d_sink, d_rpb = _attention_bwd(proj_r, p["sink_b"], tabs["bias_a"], tabs["bias_b"], p["bias_c"], outs,
                                           d_outs, tabs["cos"], tabs["sin"])
    dh1 = _mm_nt(d_proj, big["w_in"], cols=True, to=1024, tr=IN_COLS // N_SHARDS, out_dtype=F32, name="nt_in")
    g_in = _mm_tn(h1, d_proj, tk=1024, tn=IN_COLS // N_SHARDS, shards=N_SHARDS, name="tn_in")
    dx0, dx0_b, d_ln_attn = _rmsnorm_bwd(x0, p["ln_attn"], dh1, dx1, "ln_attn_bwd")
    small = {"ln_attn": d_ln_attn, "sink_b": d_sink, "rpb_c": d_rpb, "mix_gain": d_mix_gain, "ln_ffn": d_ln_ffn,
             "conv_w": d_conv_w, "conv_b": d_conv_b}
    return dx0, dx0_b, small, begin({"w_out": g_out, "w_in": g_in})


HBM_SPEC = pl.BlockSpec(memory_space=pl.ANY)


def _place():
    x, y, c = lax.axis_index("x"), lax.axis_index("y"), lax.axis_index("c")
    chips = ((1 - x, y), (x, 1 - y), (1 - x, 1 - y))
    return x, y, c, chips


def _shard_index(px, py):
    return 2 * px + py


def _remote(src, dst, send_sem, recv_sem, to):
    return pltpu.make_async_remote_copy(src_ref=src, dst_ref=dst, send_sem=send_sem, recv_sem=recv_sem,
                                        device_id=to, device_id_type=MESH)


def _own_slot(w, layer, shard, name):
    _, r, c_dim = w.shape
    rows = r
    for cand in (512, 256, 128):
        if r % cand == 0 and cand * c_dim * 4 <= 2 * MIB:
            rows = cand
            break

    def body(s_ref, w_ref, o_ref):
        o_ref[...] = w_ref[...].astype(BF16)

    return pl.pallas_call(
        body, name=name,
        grid_spec=pltpu.PrefetchScalarGridSpec(
            num_scalar_prefetch=1, grid=(r // rows,),
            in_specs=[pl.BlockSpec((None, rows, c_dim), lambda i, s: (layer, i, 0))],
            out_specs=pl.BlockSpec((None, rows, c_dim), lambda i, s: (s[0], i, 0))),
        out_shape=jax.ShapeDtypeStruct((N_SHARDS, r, c_dim), BF16),
        compiler_params=_params(("arbitrary",), 32),
    )(shard.astype(jnp.int32).reshape(1), w)


HBM_ONLY = pl.BlockSpec(memory_space=pltpu.HBM)
SEM_SPEC = pl.BlockSpec(memory_space=pltpu.SEMAPHORE)
DATAFLOW = pltpu.SideEffectType.DATAFLOW_SIDE_EFFECTING


def _in_hbm(a):
    return pltpu.with_memory_space_constraint(a, pltpu.HBM)


N_DEV = 8


def _peers(x, y, c):
    flips = [(fx, fy, fc) for fx in (0, 1) for fy in (0, 1) for fc in (0, 1)][1:]
    return [((1 - x) if fx else x, (1 - y) if fy else y, (1 - c) if fc else c) for fx, fy, fc in flips]


def _small_start(vec, after, name):
    n_after = len(after)

    def body(v_ref, slots_ref, *rest):
        send, recv = rest[n_after], rest[n_after + 1]
        token = rest[-1]
        x, y, c, _ = _place()
        me = 4 * x + 2 * y + c
        for k, peer in enumerate(_peers(x, y, c)):
            _remote(v_ref, slots_ref.at[me], send.at[k], recv.at[k], peer).start()
        token[...] = jnp.zeros_like(token)

    slots = jax.ShapeDtypeStruct((N_DEV,) + vec.shape, vec.dtype)
    res = pl.pallas_call(
        body, name=name,
        out_shape=(pltpu.SemaphoreType.DMA((N_DEV - 1,)), pltpu.SemaphoreType.DMA((N_DEV - 1,)),
                   pltpu.HBM(vec.shape, vec.dtype), pltpu.HBM(slots.shape, slots.dtype),
                   jax.ShapeDtypeStruct((8, LANES), F32)),
        in_specs=[HBM_ONLY, HBM_ONLY] + [ANY_SPEC] * n_after,
        out_specs=(SEM_SPEC, SEM_SPEC, HBM_ONLY, HBM_ONLY, pl.BlockSpec(memory_space=pltpu.VMEM)),
        input_output_aliases={0: 2, 1: 3},
        compiler_params=pltpu.CompilerParams(has_side_effects=DATAFLOW),
    )(_in_hbm(vec), _in_hbm(lax.empty(slots.shape, slots.dtype)), *after)
    return res


def _small_wait(send, recv, vec, slots, after, name):
    def body(v_ref, slots_ref, send_ref, recv_ref, *rest):
        x, y, c, _ = _place()
        for k, (px, py, pc) in enumerate(_peers(x, y, c)):
            cp = _remote(v_ref, slots_ref.at[4 * px + 2 * py + pc], send_ref.at[k], recv_ref.at[k], (px, py, pc))
            cp.wait_send()
            cp.wait_recv()

    return pl.pallas_call(
        body, name=name, out_shape=(pltpu.HBM(vec.shape, vec.dtype), pltpu.HBM(slots.shape, slots.dtype)),
        in_specs=[HBM_ONLY, HBM_ONLY, SEM_SPEC, SEM_SPEC] + [ANY_SPEC] * len(after), out_specs=[HBM_ONLY, HBM_ONLY],
        input_output_aliases={0: 0, 1: 1},
        compiler_params=pltpu.CompilerParams(has_side_effects=DATAFLOW),
    )(vec, slots, send, recv, *after)


def _small_sum(vec, slots, name):
    rows = vec.shape[0]
    blk = min(rows, 256)
    x, y, c = lax.axis_index("x"), lax.axis_index("y"), lax.axis_index("c")
    me = (4 * x + 2 * y + c).astype(jnp.int32).reshape(1)

    def slot_spec(k):
        return pl.BlockSpec((None, blk, LANES), lambda i, w: (jnp.where(w[0] == k, (k + 1) % N_DEV, k), i, 0))

    def body(w_ref, v_ref, *rest):
        o_ref = rest[-1]
        acc = None
        for k in range(N_DEV):
            term = jnp.where(w_ref[0] == k, v_ref[...], rest[k][...])
            acc = term if acc is None else acc + term
        o_ref[...] = acc

    return pl.pallas_call(
        body, name=name,
        grid_spec=pltpu.PrefetchScalarGridSpec(
            num_scalar_prefetch=1, grid=(rows // blk,),
            in_specs=[pl.BlockSpec((blk, LANES), lambda i, w: (i, 0))] + [slot_spec(k) for k in range(N_DEV)],
            out_specs=pl.BlockSpec((blk, LANES), lambda i, w: (i, 0))),
        out_shape=jax.ShapeDtypeStruct(vec.shape, F32), compiler_params=_params(("arbitrary",), 32),
    )(me, vec, *([slots] * N_DEV))


def _half(ref, slot, c):
    half = ref.shape[1] // 2
    return ref.at[slot, pl.ds(pl.multiple_of(c * half, 8), half)]


def _gather_start(bufs, after, name):
    n = len(bufs)
    n_after = len(after)

    def body(*refs):
        ins = refs[:n]
        send, recv = refs[n + n_after], refs[n + n_after + 1]
        token = refs[-1]
        x, y, c, chips = _place()
        me = _shard_index(x, y)
        for t in range(n):
            for j, (px, py) in enumerate(chips):
                mine = _half(ins[t], me, c)
                _remote(mine, mine, send.at[t * 3 + j], recv.at[t * 3 + j], (px, py, c)).start()
        token[...] = jnp.zeros_like(token)

    thru = [pltpu.HBM(b.shape, b.dtype) for b in bufs]
    res = pl.pallas_call(
        body, name=name,
        out_shape=(pltpu.SemaphoreType.DMA((n * 3,)), pltpu.SemaphoreType.DMA((n * 3,)), *thru,
                   jax.ShapeDtypeStruct((8, LANES), F32)),
        in_specs=[HBM_ONLY] * n + [ANY_SPEC] * n_after,
        out_specs=(SEM_SPEC, SEM_SPEC, *([HBM_ONLY] * n), pl.BlockSpec(memory_space=pltpu.VMEM)),
        input_output_aliases={i: 2 + i for i in range(n)},
        compiler_params=pltpu.CompilerParams(has_side_effects=DATAFLOW),
    )(*[_in_hbm(b) for b in bufs], *after)
    return res[0], res[1], list(res[2:2 + n]), res[-1]


def _gather_wait(send, recv, bufs, after, name):
    n = len(bufs)

    def body(*refs):
        ins = refs[:n]
        send_ref, recv_ref = refs[n], refs[n + 1]
        x, y, c, chips = _place()
        me = _shard_index(x, y)
        for t in range(n):
            for j, (px, py) in enumerate(chips):
                cp = _remote(_half(ins[t], me, c), _half(ins[t], _shard_index(px, py), c), send_ref.at[t * 3 + j],
                             recv_ref.at[t * 3 + j], (px, py, c))
                cp.wait_send()
                cp.wait_recv()

    res = pl.pallas_call(
        body, name=name, out_shape=tuple(pltpu.HBM(b.shape, b.dtype) for b in bufs),
        in_specs=[HBM_ONLY] * n + [SEM_SPEC, SEM_SPEC] + [ANY_SPEC] * len(after), out_specs=[HBM_ONLY] * n,
        input_output_aliases={i: i for i in range(n)},
        compiler_params=pltpu.CompilerParams(has_side_effects=DATAFLOW),
    )(*bufs, send, recv, *after)
    return list(res)


def _gather_forward(bufs, name):
    n = len(bufs)

    def body(*refs):
        outs = refs[n:2 * n]
        send, recv = refs[2 * n:]
        x, y, c, chips = _place()
        sibling = (x, y, 1 - c)
        cps = []
        for t in range(n):
            for j, (px, py) in enumerate(chips):
                got = _half(outs[t], _shard_index(px, py), c)
                cp = _remote(got, got, send.at[t * 3 + j], recv.at[t * 3 + j], sibling)
                cp.start()
                cps.append(cp)
        for t in range(n):
            for j, (px, py) in enumerate(chips):
                theirs = _half(outs[t], _shard_index(px, py), 1 - c)
                _remote(theirs, theirs, send.at[t * 3 + j], recv.at[t * 3 + j], sibling).wait_recv()
        for cp in cps:
            cp.wait_send()

    return pl.pallas_call(
        body, name=name, in_specs=[HBM_SPEC] * n, out_specs=[HBM_SPEC] * n,
        out_shape=[jax.ShapeDtypeStruct(b.shape, b.dtype) for b in bufs],
        input_output_aliases={t: t for t in range(n)},
        scratch_shapes=[pltpu.SemaphoreType.DMA((n * 3,))] * 2,
    )(*bufs)


def _gather_forward_start(bufs, carry, name):
    n = len(bufs)

    def body(*refs):
        ins = refs[:n]
        send, recv = refs[n + 1], refs[n + 2]
        x, y, c, chips = _place()
        for t in range(n):
            for j, (px, py) in enumerate(chips):
                got = _half(ins[t], _shard_index(px, py), c)
                _remote(got, got, send.at[t * 3 + j], recv.at[t * 3 + j], (x, y, 1 - c)).start()

    res = pl.pallas_call(
        body, name=name,
        out_shape=(pltpu.SemaphoreType.DMA((n * 3,)), pltpu.SemaphoreType.DMA((n * 3,)),
                   *[pltpu.HBM(b.shape, b.dtype) for b in bufs], pltpu.HBM(carry.shape, carry.dtype)),
        in_specs=[HBM_ONLY] * (n + 1),
        out_specs=(SEM_SPEC, SEM_SPEC, *([HBM_ONLY] * (n + 1))),
        input_output_aliases={i: 2 + i for i in range(n + 1)},
        compiler_params=pltpu.CompilerParams(has_side_effects=DATAFLOW),
    )(*[_in_hbm(b) for b in bufs], _in_hbm(carry))
    return res[0], res[1], list(res[2:2 + n]), res[-1]


def _gather_forward_wait(send, recv, bufs, after, name):
    n = len(bufs)

    def body(*refs):
        ins = refs[:n]
        send_ref, recv_ref = refs[n], refs[n + 1]
        x, y, c, chips = _place()
        for t in range(n):
            for j, (px, py) in enumerate(chips):
                s = _shard_index(px, py)
                cp = _remote(_half(ins[t], s, c), _half(ins[t], s, 1 - c), send_ref.at[t * 3 + j],
                             recv_ref.at[t * 3 + j], (x, y, 1 - c))
                cp.wait_send()
                cp.wait_recv()

    res = pl.pallas_call(
        body, name=name, out_shape=tuple(pltpu.HBM(b.shape, b.dtype) for b in bufs),
        in_specs=[HBM_ONLY] * n + [SEM_SPEC, SEM_SPEC] + [ANY_SPEC] * len(after), out_specs=[HBM_ONLY] * n,
        input_output_aliases={i: i for i in range(n)},
        compiler_params=pltpu.CompilerParams(has_side_effects=DATAFLOW),
    )(*bufs, send, recv, *after)
    return list(res)


def _sibling_rows(ref, c):
    half = ref.shape[1] // 2
    return ref.at[:, pl.ds(pl.multiple_of((1 - c) * half, 8), half)]


def _half_exchange_start(grads, name):
    n = len(grads)

    def body(*refs):
        ins, lands = refs[:n], refs[n:2 * n]
        send, recv = refs[2 * n], refs[2 * n + 1]
        token = refs[-1]
        x, y, c, _ = _place()
        for t in range(n):
            _remote(_sibling_rows(ins[t], c), lands[t], send.at[t], recv.at[t], (x, y, 1 - c)).start()
        token[...] = jnp.zeros_like(token)

    halves = [jax.ShapeDtypeStruct((g.shape[0], g.shape[1] // 2, g.shape[2]), g.dtype) for g in grads]
    res = pl.pallas_call(
        body, name=name,
        out_shape=(pltpu.SemaphoreType.DMA((n,)), pltpu.SemaphoreType.DMA((n,)),
                   *[pltpu.HBM(g.shape, g.dtype) for g in grads], *[pltpu.HBM(h.shape, h.dtype) for h in halves],
                   jax.ShapeDtypeStruct((8, LANES), F32)),
        in_specs=[HBM_ONLY] * (2 * n),
        out_specs=(SEM_SPEC, SEM_SPEC, *([HBM_ONLY] * (2 * n)), pl.BlockSpec(memory_space=pltpu.VMEM)),
        input_output_aliases={i: 2 + i for i in range(2 * n)},
        compiler_params=pltpu.CompilerParams(has_side_effects=DATAFLOW),
    )(*[_in_hbm(g) for g in grads], *[_in_hbm(lax.empty(h.shape, h.dtype)) for h in halves])
    return res[0], res[1], list(res[2:2 + n]), list(res[2 + n:2 + 2 * n]), res[-1]


def _half_exchange_wait(send, recv, grads, lands, after, name):
    n = len(grads)

    def body(*refs):
        ins, got = refs[:n], refs[n:2 * n]
        send_ref, recv_ref = refs[2 * n], refs[2 * n + 1]
        x, y, c, _ = _place()
        for t in range(n):
            cp = _remote(_sibling_rows(ins[t], c), got[t], send_ref.at[t], recv_ref.at[t], (x, y, 1 - c))
            cp.wait_send()
            cp.wait_recv()

    res = pl.pallas_call(
        body, name=name,
        out_shape=(*[pltpu.HBM(g.shape, g.dtype) for g in grads], *[pltpu.HBM(h.shape, h.dtype) for h in lands]),
        in_specs=[HBM_ONLY] * (2 * n) + [SEM_SPEC, SEM_SPEC] + [ANY_SPEC] * len(after),
        out_specs=[HBM_ONLY] * (2 * n),
        input_output_aliases={i: i for i in range(2 * n)},
        compiler_params=pltpu.CompilerParams(has_side_effects=DATAFLOW),
    )(*grads, *lands, send, recv, *after)
    return list(res[:n]), list(res[n:])


def _half_rows(half, c_dim):
    for cand in (512, 256, 128, 64):
        if half % cand == 0 and cand * c_dim * 2 <= 2 * MIB:
            return cand
    raise ValueError((half, c_dim))


def _core_index():
    return lax.axis_index("c").astype(jnp.int32).reshape(1)


def _half_sum(own, other, name):
    s, r, c_dim = own.shape
    rows = _half_rows(r // 2, c_dim)
    per = r // 2 // rows

    def body(c_ref, a_ref, b_ref, o_ref):
        o_ref[...] = (a_ref[...].astype(F32) + b_ref[...].astype(F32)).astype(BF16)

    return pl.pallas_call(
        body, name=name,
        grid_spec=pltpu.PrefetchScalarGridSpec(
            num_scalar_prefetch=1, grid=(s, per),
            in_specs=[pl.BlockSpec((None, rows, c_dim), lambda k, i, c: (k, c[0] * per + i, 0)),
                      pl.BlockSpec((None, rows, c_dim), lambda k, i, c: (k, i, 0))],
            out_specs=pl.BlockSpec((None, rows, c_dim), lambda k, i, c: (k, i, 0))),
        out_shape=pltpu.HBM((s, r // 2, c_dim), BF16), compiler_params=_params(("arbitrary", "arbitrary"), 32),
    )(_core_index(), own, other)


def _reduce_start(pairs, name):
    n = len(pairs)

    def body(*refs):
        ins, lands = refs[:n], refs[n:2 * n]
        send, recv = refs[2 * n], refs[2 * n + 1]
        token = refs[-1]
        x, y, c, chips = _place()
        me = _shard_index(x, y)
        for t in range(n):
            for j, (px, py) in enumerate(chips):
                _remote(ins[t].at[_shard_index(px, py)], lands[t].at[me], send.at[t * 3 + j], recv.at[t * 3 + j],
                        (px, py, c)).start()
        token[...] = jnp.zeros_like(token)

    thru = [pltpu.HBM(b.shape, b.dtype) for b in pairs]
    res = pl.pallas_call(
        body, name=name,
        out_shape=(pltpu.SemaphoreType.DMA((n * 3,)), pltpu.SemaphoreType.DMA((n * 3,)), *thru, *thru,
                   jax.ShapeDtypeStruct((8, LANES), F32)),
        in_specs=[HBM_ONLY] * (2 * n),
        out_specs=(SEM_SPEC, SEM_SPEC, *([HBM_ONLY] * (2 * n)), pl.BlockSpec(memory_space=pltpu.VMEM)),
        input_output_aliases={i: 2 + i for i in range(2 * n)},
        compiler_params=pltpu.CompilerParams(has_side_effects=DATAFLOW),
    )(*[_in_hbm(b) for b in pairs], *[_in_hbm(lax.empty(b.shape, b.dtype)) for b in pairs])
    return res[0], res[1], list(res[2:2 + n]), list(res[2 + n:2 + 2 * n]), res[-1]


def _reduce_wait(send, recv, pairs, lands, after, name):
    n = len(pairs)

    def body(*refs):
        ins, got = refs[:n], refs[n:2 * n]
        send_ref, recv_ref = refs[2 * n], refs[2 * n + 1]
        x, y, c, chips = _place()
        for t in range(n):
            for j, (px, py) in enumerate(chips):
                s = _shard_index(px, py)
                cp = _remote(ins[t].at[s], got[t].at[s], send_ref.at[t * 3 + j], recv_ref.at[t * 3 + j], (px, py, c))
                cp.wait_send()
                cp.wait_recv()

    thru = [pltpu.HBM(b.shape, b.dtype) for b in pairs]
    res = pl.pallas_call(
        body, name=name, out_shape=(*thru, *thru),
        in_specs=[HBM_ONLY] * (2 * n) + [SEM_SPEC, SEM_SPEC] + [ANY_SPEC] * len(after),
        out_specs=[HBM_ONLY] * (2 * n),
        input_output_aliases={i: i for i in range(2 * n)},
        compiler_params=pltpu.CompilerParams(has_side_effects=DATAFLOW),
    )(*pairs, *lands, send, recv, *after)
    return list(res[:n]), list(res[n:])


def _reduce_sum(pair, landed, name):
    s, half, c_dim = pair.shape
    rows = _half_rows(half, c_dim)
    per = half // rows
    shard = _shard_index(lax.axis_index("x"), lax.axis_index("y"))
    where = jnp.stack([shard, lax.axis_index("c")]).astype(jnp.int32)

    def landed_spec(k):
        return pl.BlockSpec((None, rows, c_dim), lambda i, w: (jnp.where(w[0] == k, (k + 1) % s, k), i, 0))

    def body(w_ref, own_ref, *rest):
        o_ref = rest[-1]
        acc = None
        for k in range(s):
            term = jnp.where(w_ref[0] == k, own_ref[...], rest[k][...]).astype(F32)
            acc = term if acc is None else acc + term
        o_ref[...] = acc

    return pl.pallas_call(
        body, name=name,
        grid_spec=pltpu.PrefetchScalarGridSpec(
            num_scalar_prefetch=1, grid=(per,),
            in_specs=[pl.BlockSpec((None, rows, c_dim), lambda i, w: (w[0], i, 0))] + [landed_spec(k) for k in range(s)],
            out_specs=pl.BlockSpec((rows, c_dim), lambda i, w: (w[1] * per + i, 0))),
        out_shape=pltpu.HBM((2 * half, c_dim), F32), compiler_params=_params(("arbitrary",), 40),
    )(where, pair, *([landed] * s))


def _my_rows(ref, c):
    half = ref.shape[0] // 2
    return ref.at[pl.ds(pl.multiple_of(c * half, 8), half)]


def _half_gather_start(bufs, name):
    n = len(bufs)

    def body(*refs):
        ins = refs[:n]
        send, recv = refs[n], refs[n + 1]
        token = refs[-1]
        x, y, c, _ = _place()
        for t in range(n):
            mine = _my_rows(ins[t], c)
            _remote(mine, mine, send.at[t], recv.at[t], (x, y, 1 - c)).start()
        token[...] = jnp.zeros_like(token)

    res = pl.pallas_call(
        body, name=name,
        out_shape=(pltpu.SemaphoreType.DMA((n,)), pltpu.SemaphoreType.DMA((n,)),
                   *[pltpu.HBM(b.shape, b.dtype) for b in bufs], jax.ShapeDtypeStruct((8, LANES), F32)),
        in_specs=[HBM_ONLY] * n,
        out_specs=(SEM_SPEC, SEM_SPEC, *([HBM_ONLY] * n), pl.BlockSpec(memory_space=pltpu.VMEM)),
        input_output_aliases={i: 2 + i for i in range(n)},
        compiler_params=pltpu.CompilerParams(has_side_effects=DATAFLOW),
    )(*[_in_hbm(b) for b in bufs])
    return res[0], res[1], list(res[2:2 + n]), res[-1]


def _half_gather_wait(send, recv, bufs, after, name):
    n = len(bufs)

    def body(*refs):
        ins = refs[:n]
        send_ref, recv_ref = refs[n], refs[n + 1]
        x, y, c, _ = _place()
        for t in range(n):
            cp = _remote(_my_rows(ins[t], c), _my_rows(ins[t], 1 - c), send_ref.at[t], recv_ref.at[t], (x, y, 1 - c))
            cp.wait_send()
            cp.wait_recv()

    res = pl.pallas_call(
        body, name=name, out_shape=tuple(pltpu.HBM(b.shape, b.dtype) for b in bufs),
        in_specs=[HBM_ONLY] * n + [SEM_SPEC, SEM_SPEC] + [ANY_SPEC] * len(after), out_specs=[HBM_ONLY] * n,
        input_output_aliases={i: i for i in range(n)},
        compiler_params=pltpu.CompilerParams(has_side_effects=DATAFLOW),
    )(*bufs, send, recv, *after)
    return list(res)


WEIGHT_NAMES = ("ln_attn", "w_in", "sink_b", "rpb_c", "mix_gain", "w_out", "ln_ffn", "w_up", "conv_w", "conv_b",
                "w_down", "ln_final")
BIG_NAMES = ("w_in", "w_out", "w_up", "w_down")
REPLICATED_NAMES = ("ln_attn", "sink_b", "rpb_c", "mix_gain", "ln_ffn", "conv_b", "ln_final")
PACK_TILE = 8 * LANES


def _pack(arrays, row_multiple):
    pieces = []
    for a in arrays:
        flat = a.reshape(-1)
        pieces.append(jnp.pad(flat, (0, (-flat.shape[0]) % PACK_TILE)))
    flat = jnp.concatenate(pieces)
    flat = jnp.pad(flat, (0, (-flat.shape[0]) % (row_multiple * LANES)))
    return flat.reshape(-1, LANES)


def _unpack(packed, shapes):
    flat = packed.reshape(-1)
    out, off = [], 0
    for shape in shapes:
        size = math.prod(shape)
        out.append(flat[off:off + size].reshape(shape))
        off += size + (-size) % PACK_TILE
    return out


def kernel(x, ln_attn, w_in, sink_b, rpb_c, mix_gain, w_out, ln_ffn, w_up, conv_w, conv_b, w_down, ln_final, loss_target, m_ln_attn, m_w_in, m_sink_b, m_rpb_c, m_mix_gain, m_w_out, m_ln_ffn, m_w_up, m_conv_w, m_conv_b, m_w_down, m_ln_final, v_ln_attn, v_w_in, v_sink_b, v_rpb_c, v_mix_gain, v_w_out, v_ln_ffn, v_w_up, v_conv_w, v_conv_b, v_w_down, v_ln_final):
    w = dict(ln_attn=ln_attn, w_in=w_in, sink_b=sink_b, rpb_c=rpb_c, mix_gain=mix_gain, w_out=w_out, ln_ffn=ln_ffn,
             w_up=w_up, conv_w=conv_w, conv_b=conv_b, w_down=w_down, ln_final=ln_final)
    m = dict(ln_attn=m_ln_attn, w_in=m_w_in, sink_b=m_sink_b, rpb_c=m_rpb_c, mix_gain=m_mix_gain, w_out=m_w_out,
             ln_ffn=m_ln_ffn, w_up=m_w_up, conv_w=m_conv_w, conv_b=m_conv_b, w_down=m_w_down, ln_final=m_ln_final)
    v = dict(ln_attn=v_ln_attn, w_in=v_w_in, sink_b=v_sink_b, rpb_c=v_rpb_c, mix_gain=v_mix_gain, w_out=v_w_out,
             ln_ffn=v_ln_ffn, w_up=v_w_up, conv_w=v_conv_w, conv_b=v_conv_b, w_down=v_w_down, ln_final=v_ln_final)
    shard = _shard_index(lax.axis_index("x"), lax.axis_index("y"))
    up_cols = w_up.shape[2]

    conv_send, conv_recv, conv_vec, conv_slots, conv_token = _small_start(_pack([conv_w], 8), [], "conv_w_start")

    arrivals = []
    group_of = {}
    tokens = []
    rest = ("w_out", "w_up", "w_down")
    for l, names in ((0, ("w_in",)), (0, rest), (1, ("w_in",)), (1, rest)):
        bufs = [_own_slot(w[k], l, shard, "own_" + k) for k in names]
        send, recv, bufs, token = _gather_start(bufs, tokens[-1:] or [conv_token], "gather_start_%d" % len(arrivals))
        tokens.append(token)
        for k in names:
            group_of[l, k] = len(arrivals)
        arrivals.append({"names": names, "send": send, "recv": recv, "bufs": bufs, "done": None})

    def gathered(l, name, after):
        idx = group_of[l, name]
        group = arrivals[idx]

        def whole(k, buf):
            return buf.reshape(1, -1, buf.shape[2]) if k in ("w_out", "w_down") else buf

        if group["done"] is None:
            follow = list(after) + tokens[-1:]
            if idx == 0:
                follow += [tabs[k] for k in ("cos", "sin", "bias_a", "bias_b")]
                follow += [p[k] for p in layers for k in ("bias_c", "conv_w")]
            bufs = _gather_wait(group["send"], group["recv"], group["bufs"], follow, "gather_wait_%d" % idx)
            first = _gather_forward(bufs[:1], "gather_forward_%d" % idx)[0]
            if len(bufs) > 1:
                send, recv, rest, first = _gather_forward_start(bufs[1:], first, "gather_forward_start_%d" % idx)
                group["rest"] = (send, recv, rest)
            group["done"] = {group["names"][0]: whole(group["names"][0], first)}
        if name not in group["done"]:
            send, recv, rest = group["rest"]
            rest = _gather_forward_wait(send, recv, rest, list(after), "gather_forward_wait_%d" % idx)
            group["done"].update({k: whole(k, buf) for k, buf in zip(group["names"][1:], rest)})
        return group["done"][name]

    conv_vec, conv_slots = _small_wait(conv_send, conv_recv, conv_vec, conv_slots, tokens[-1:], "conv_w_wait")
    device = 4 * lax.axis_index("x") + 2 * lax.axis_index("y") + lax.axis_index("c")
    conv_slots = lax.dynamic_update_index_in_dim(conv_slots, conv_vec, device, 0)
    conv_all = conv_slots[0::2].reshape(N_SHARDS, -1)[:, :conv_w.size].reshape((N_SHARDS,) + conv_w.shape)

    cos, sin = _rope_tables(SEQ)
    tabs = {"cos": cos, "sin": sin, "bias_a": _bias_a(), "bias_b": _bias_b()}
    layers = []
    for l in range(DEPTH):
        conv_w_l = conv_all[:, l].reshape(2, N_SHARDS // 2, 3, up_cols).transpose(0, 2, 1, 3).reshape(2, 3, D_FF)
        layers.append({"ln_attn": ln_attn[l][None], "sink_b": sink_b[l], "bias_c": _bias_c(rpb_c[l]),
                       "mix_gain": mix_gain[l][None], "ln_ffn": ln_ffn[l][None], "conv_w": conv_w_l,
                       "conv_b": conv_b[l].reshape(2, 1, D_FF)})

    act = x[0]
    saved = []
    for l in range(DEPTH):
        act, keep = _layer_fwd(act, layers[l], lambda name, after, l=l: gathered(l, name, [after]), tabs)
        saved.append(keep)
    loss_part, dx, dx_b, d_ln_final = _loss_head(act, ln_final[None], loss_target[0], "loss_head")
    loss = lax.psum(loss_part[0, 0], ("x", "y", "c"))

    reductions = []

    opened = [0]

    def begin(l, partial):
        idx = opened[0]
        opened[0] += 1
        names = tuple(partial)
        send_sem, recv_sem, mine, theirs, token = _half_exchange_start([partial[k] for k in names],
                                                                       "half_exchange_start_%d" % idx)
        return {"idx": idx, "layer": l, "names": names, "send": send_sem, "recv": recv_sem, "mine": mine,
                "theirs": theirs}, token

    def finish(handle, after):
        idx, names = handle["idx"], handle["names"]
        mine, theirs = _half_exchange_wait(handle["send"], handle["recv"], handle["mine"], handle["theirs"], after,
                                           "half_exchange_wait_%d" % idx)
        pairs = [_half_sum(a, b, "half_sum_" + k) for k, a, b in zip(names, mine, theirs)]
        send_sem, recv_sem, pairs, lands, token = _reduce_start(pairs, "reduce_start_%d" % idx)
        reductions.append({"layer": handle["layer"], "names": names, "send": send_sem, "recv": recv_sem,
                           "pairs": pairs, "lands": lands})
        return token

    small = [None] * DEPTH
    pending = None
    for l in reversed(range(DEPTH)):
        big = {k: gathered(l, k, []) for k in BIG_NAMES}
        dx, dx_b, small[l], pending = _layer_bwd(dx, dx_b, saved[l], layers[l], big, tabs,
                                                 functools.partial(begin, l), finish, pending)
    after = [finish(pending[0], [pending[1]])]

    stacked = {k: jnp.stack([small[l][k] for l in range(DEPTH)]) for k in small[0]}
    part = {"ln_attn": stacked["ln_attn"][:, 0], "sink_b": stacked["sink_b"], "rpb_c": stacked["rpb_c"],
            "mix_gain": stacked["mix_gain"][:, 0], "ln_ffn": stacked["ln_ffn"][:, 0],
            "conv_b": stacked["conv_b"].reshape(DEPTH, 2 * D_FF), "ln_final": d_ln_final[0],
            "conv_w": stacked["conv_w"].transpose(0, 2, 1, 3).reshape(DEPTH, 3, 2 * D_FF)}
    small_names = REPLICATED_NAMES + ("conv_w",)
    small_send, small_recv, small_vec, small_slots, token = _small_start(
        _pack([part[k] for k in small_names], 256), after, "small_grads_start")
    after = [token]

    grads, delta, new_m, new_v = {}, {}, {}, {}
    updated = dict.fromkeys(BIG_NAMES)

    def arrive(idx, after):
        group = reductions[idx]
        pairs, lands = _reduce_wait(group["send"], group["recv"], group["pairs"], group["lands"], after,
                                    "reduce_wait_%d" % idx)
        halves = [_reduce_sum(pair, landed, "reduce_sum_" + k) for k, pair, landed in zip(group["names"], pairs, lands)]
        send_sem, recv_sem, halves, token = _half_gather_start(halves, "half_gather_start_%d" % idx)
        return {"idx": idx, "send": send_sem, "recv": recv_sem, "bufs": halves, "names": group["names"],
                "layer": group["layer"]}, [token]

    def update(swap, after):
        whole = _half_gather_wait(swap["send"], swap["recv"], swap["bufs"], after,
                                  "half_gather_wait_%d" % swap["idx"])
        for k, g in zip(swap["names"], whole):
            updated[k] = _adamw_layer(w[k], g, m[k], v[k], swap["layer"], updated[k], "adamw_" + k)
        return [updated[k][0] for k in swap["names"]]

    swaps = []
    for idx in range(len(reductions) - 1):
        swap, after = arrive(idx, after)
        swaps.append(swap)
    for swap in swaps[:2]:
        after = update(swap, after)
    swap, after = arrive(len(reductions) - 1, after)
    for swap in swaps[2:] + [swap]:
        after = update(swap, after)
    for k in BIG_NAMES:
        grads[k], delta[k], new_m[k], new_v[k] = updated[k]

    small_vec, small_slots = _small_wait(small_send, small_recv, small_vec, small_slots, after, "small_grads_wait")
    total = _small_sum(small_vec, small_slots, "small_grads_sum")
    for k, g in zip(small_names, _unpack(total, [part[k].shape for k in small_names])):
        grads[k] = g
    grads["conv_w"] = lax.dynamic_slice_in_dim(grads["conv_w"], shard * up_cols, up_cols, axis=2)

    flat = (DEPTH * 3, up_cols)
    res = _adamw(conv_w.reshape(flat), grads["conv_w"].reshape(flat), m["conv_w"].reshape(flat),
                 v["conv_w"].reshape(flat), "adamw_conv_w")
    delta["conv_w"], new_m["conv_w"], new_v["conv_w"] = (r.reshape(conv_w.shape) for r in res)
    shapes = [w[k].shape for k in REPLICATED_NAMES]
    packed = [_pack([d[k] for k in REPLICATED_NAMES], 128) for d in (w, grads, m, v)]
    for d, res in zip((delta, new_m, new_v), _adamw(*packed, "adamw_small")):
        for k, r in zip(REPLICATED_NAMES, _unpack(res, shapes)):
            d[k] = r

    return (loss, dx[None], *[grads[k] for k in WEIGHT_NAMES], *[delta[k] for k in WEIGHT_NAMES],
            *[new_m[k] for k in WEIGHT_NAMES], *[new_v[k] for k in WEIGHT_NAMES])
```

```python
import functools
import math

import jax
import jax.numpy as jnp
from jax import lax
from jax.experimental import pallas as pl
from jax.experimental.pallas import tpu as pltpu

F32 = jnp.float32
BF16 = jnp.bfloat16
MESH = pl.DeviceIdType.MESH

D_MODEL = 2048
SEQ = 2048
DEPTH = 2
HEAD_DIM = 64
N_HEADS_A = 12
N_HEADS_B = 10
N_KV_B = 2
N_HEADS_C = 10
WINDOW_B = 128
GRID_W = 64
NA_ROWS = 8
NA_COLS = 16
WIDTH_A = N_HEADS_A * HEAD_DIM
WIDTH_B = N_HEADS_B * HEAD_DIM
WIDTH_C = N_HEADS_C * HEAD_DIM
IN_COLS = 5120
D_FF = 5632
ROPE_THETA = 10000.0
EPS = 1e-6
NEG_INF = -1e30
N_SHARDS = 4

ADAM_LR = 0.001
ADAM_B1 = 0.9
ADAM_B2 = 0.999
ADAM_EPS = 1e-08
ADAM_WD = 0.01
ADAM_STEP = 10

LANES = 128
QB = 256
NQB = SEQ // QB
ROWS = 256
MIB = 2 ** 20

A_BLK = (0, 6, 12)
B_BLK = (18, 23, 24)
C_BLK = (25, 30, 35)
ROPE_BLKS = tuple(range(0, 12)) + tuple(range(18, 24))
QSCALE_BLKS = tuple(range(0, 6)) + tuple(range(18, 23)) + tuple(range(25, 30))
N_PBLK = IN_COLS // LANES


def _params(sem, vmem_mib):
    return pltpu.CompilerParams(dimension_semantics=sem, vmem_limit_bytes=vmem_mib * MIB)


def _weight_spec(w, cols, t_in, t_out, transposed):
    s, r, c = w.shape
    if cols:
        per = c // t_out
        k_dim, n = r, s * c
        if transposed:
            index = lambda j, rr: (rr // per, j, rr % per)
        else:
            index = lambda j, kk: (j // per, kk, j % per)
    else:
        per = r // t_in
        k_dim, n = s * r, c
        if transposed:
            index = lambda j, rr: (j // per, j % per, rr)
        else:
            index = lambda j, kk: (kk // per, kk % per, j)
    return pl.BlockSpec((None, t_in, t_out), index), k_dim, n


def _mm_nn(a, w, *, cols, tn, tk, out_dtype, name, residual=None, out_split=1):
    m, k_dim = a.shape
    w_spec, k_w, n = _weight_spec(w, cols, tk, tn, False)
    assert k_w == k_dim
    nj, nk = n // tn, k_dim // tk
    in_specs = [pl.BlockSpec((m, tk), lambda j, k: (0, k)), w_spec]
    args = [a, w]
    if residual is not None:
        in_specs.append(pl.BlockSpec((m, tn), lambda j, k: (0, j)))
        args.append(residual)
    if out_split > 1:
        per_o = n // out_split // tn
        out_spec = pl.BlockSpec((None, m, tn), lambda j, k: (j // per_o, 0, j % per_o))
        out_shape = pltpu.HBM((out_split, m, n // out_split), out_dtype)
    else:
        out_spec = pl.BlockSpec((m, tn), lambda j, k: (0, j))
        out_shape = pltpu.HBM((m, n), out_dtype)

    def body(*refs):
        a_ref, w_ref = refs[0], refs[1]
        r_ref = refs[2] if residual is not None else None
        o_ref = refs[3] if residual is not None else refs[2]

        def finish(val):
            if r_ref is not None:
                val = r_ref[...] + val
            o_ref[...] = val.astype(o_ref.dtype)

        part = jnp.dot(a_ref[...], w_ref[...], preferred_element_type=F32)
        if nk == 1:
            finish(part)
        else:
            acc = refs[-1]
            kk = pl.program_id(1)

            @pl.when(kk == 0)
            def _():
                acc[...] = part

            @pl.when(kk > 0)
            def _():
                acc[...] += part

            @pl.when(kk == nk - 1)
            def _():
                finish(acc[...])

    return pl.pallas_call(
        body, name=name, grid=(nj, nk), in_specs=in_specs, out_specs=out_spec, out_shape=out_shape,
        scratch_shapes=[pltpu.VMEM((m, tn), F32)] if nk > 1 else [],
        compiler_params=_params(("arbitrary", "arbitrary"), 56),
    )(*[_in_hbm(a) for a in args])


ANY_SPEC = pl.BlockSpec(memory_space=pl.ANY)


def _mm_nt(dy, w, *, cols, to, tr, out_dtype, name, after=()):
    if dy.ndim == 3:
        m = dy.shape[1]
        n = dy.shape[0] * dy.shape[2]
        per_d = dy.shape[2] // tr
        dy_spec = pl.BlockSpec((None, m, tr), lambda j, r: (r // per_d, 0, r % per_d))
    else:
        m, n = dy.shape
        dy_spec = pl.BlockSpec((m, tr), lambda j, r: (0, r))
    w_spec, k_dim, n_w = _weight_spec(w, cols, to, tr, True)
    assert n_w == n
    nj, nr = k_dim // to, n // tr

    n_after = len(after)

    def body(dy_ref, w_ref, *rest):
        o_ref = rest[n_after]
        part = lax.dot_general(dy_ref[...], w_ref[...], (((1,), (1,)), ((), ())), preferred_element_type=F32)
        if nr == 1:
            o_ref[...] = part.astype(o_ref.dtype)
        else:
            acc = rest[n_after + 1]
            rr = pl.program_id(1)

            @pl.when(rr == 0)
            def _():
                acc[...] = part

            @pl.when(rr > 0)
            def _():
                acc[...] += part

            @pl.when(rr == nr - 1)
            def _():
                o_ref[...] = acc[...].astype(o_ref.dtype)

    return pl.pallas_call(
        body, name=name, grid=(nj, nr), in_specs=[dy_spec, w_spec] + [ANY_SPEC] * n_after,
        out_specs=pl.BlockSpec((m, to), lambda j, r: (0, j)),
        out_shape=pltpu.HBM((m, k_dim), out_dtype),
        scratch_shapes=[pltpu.VMEM((m, to), F32)] if nr > 1 else [],
        compiler_params=_params(("arbitrary", "arbitrary"), 56),
    )(_in_hbm(dy), _in_hbm(w), *after)


def _mm_tn(x, dy, *, tk, tn, shards, name):
    m, k_dim = x.shape
    if dy.ndim == 3:
        n = dy.shape[0] * dy.shape[2]
        per_d = dy.shape[2] // tn
        dy_spec = pl.BlockSpec((None, m, tn), lambda i, j: (j // per_d, 0, j % per_d))
    else:
        n = dy.shape[1]
        dy_spec = pl.BlockSpec((m, tn), lambda i, j: (0, j))
    if shards > 0:
        per = n // shards // tn
        out_shape = pltpu.HBM((shards, k_dim, n // shards), BF16)
        out_spec = pl.BlockSpec((None, tk, tn), lambda i, j: (j // per, i, j % per))
    else:
        s = -shards
        per = k_dim // s // tk
        out_shape = pltpu.HBM((s, k_dim // s, n), BF16)
        out_spec = pl.BlockSpec((None, tk, tn), lambda i, j: (i // per, i % per, j))

    def body(x_ref, dy_ref, o_ref):
        o_ref[...] = lax.dot_general(x_ref[...], dy_ref[...], (((0,), (0,)), ((), ())),
                                     preferred_element_type=F32).astype(BF16)

    return pl.pallas_call(
        body, name=name, grid=(k_dim // tk, n // tn),
        in_specs=[pl.BlockSpec((m, tk), lambda i, j: (0, i)), dy_spec], out_specs=out_spec, out_shape=out_shape,
        compiler_params=_params(("arbitrary", "arbitrary"), 56),
    )(_in_hbm(x), _in_hbm(dy))


def _row_spec(width, rows=ROWS):
    return pl.BlockSpec((rows, width), lambda i: (i, 0))


def _vec_spec(width):
    return pl.BlockSpec((1, width), lambda i: (0, 0))


def _rms_stats(x):
    r = lax.rsqrt(jnp.mean(x * x, axis=-1, keepdims=True) + EPS)
    return r, x * r


def _rmsnorm_fwd(x, gain, name):
    t, d = x.shape

    def body(x_ref, g_ref, o_ref):
        _, n = _rms_stats(x_ref[...])
        o_ref[...] = (n * g_ref[...]).astype(BF16)

    return pl.pallas_call(
        body, name=name, grid=(t // ROWS,), in_specs=[_row_spec(d), _vec_spec(d)], out_specs=_row_spec(d),
        out_shape=pltpu.HBM((t, d), BF16), compiler_params=_params(("arbitrary",), 32),
    )(_in_hbm(x), _in_hbm(gain))


def _rmsnorm_bwd(x, gain, dh, dres, name, after=()):
    t, d = x.shape
    n_after = len(after)

    def body(x_ref, g_ref, dh_ref, dres_ref, *rest):
        dx_ref, dxb_ref, dg_ref = rest[n_after:]
        r, n = _rms_stats(x_ref[...])
        dh_v = dh_ref[...]
        dn = dh_v * g_ref[...]
        dx = dres_ref[...] + r * (dn - n * jnp.mean(dn * n, axis=-1, keepdims=True))
        dx_ref[...] = dx
        dxb_ref[...] = dx.astype(BF16)
        part = jnp.sum(dh_v * n, axis=0, keepdims=True)

        @pl.when(pl.program_id(0) == 0)
        def _():
            dg_ref[...] = part

        @pl.when(pl.program_id(0) > 0)
        def _():
            dg_ref[...] += part

    return pl.pallas_call(
        body, name=name, grid=(t // ROWS,),
        in_specs=[_row_spec(d), _vec_spec(d), _row_spec(d), _row_spec(d)] + [ANY_SPEC] * n_after,
        out_specs=[_row_spec(d), _row_spec(d), _vec_spec(d)],
        out_shape=[pltpu.HBM((t, d), F32), pltpu.HBM((t, d), BF16), jax.ShapeDtypeStruct((1, d), F32)],
        compiler_params=_params(("arbitrary",), 40),
    )(_in_hbm(x), _in_hbm(gain), _in_hbm(dh), _in_hbm(dres), *after)


def _loss_head(x, gain, target, name):
    t, d = x.shape

    def body(x_ref, g_ref, t_ref, loss_ref, dx_ref, dxb_ref, dg_ref):
        r, n = _rms_stats(x_ref[...])
        g = g_ref[...]
        err = n * g - t_ref[...]
        dy = err * (1.0 / d)
        dn = dy * g
        dx = r * (dn - n * jnp.mean(dn * n, axis=-1, keepdims=True))
        dx_ref[...] = dx
        dxb_ref[...] = dx.astype(BF16)
        part = jnp.sum(dy * n, axis=0, keepdims=True)
        lpart = jnp.zeros((8, LANES), F32) + 0.5 * jnp.sum(jnp.mean(err * err, axis=-1, keepdims=True))

        @pl.when(pl.program_id(0) == 0)
        def _():
            dg_ref[...] = part
            loss_ref[...] = lpart

        @pl.when(pl.program_id(0) > 0)
        def _():
            dg_ref[...] += part
            loss_ref[...] += lpart

    return pl.pallas_call(
        body, name=name, grid=(t // ROWS,),
        in_specs=[_row_spec(d), _vec_spec(d), _row_spec(d)],
        out_specs=[pl.BlockSpec((8, LANES), lambda i: (0, 0)), _row_spec(d), _row_spec(d), _vec_spec(d)],
        out_shape=[jax.ShapeDtypeStruct((8, LANES), F32), pltpu.HBM((t, d), F32), pltpu.HBM((t, d), BF16),
                   jax.ShapeDtypeStruct((1, d), F32)],
        compiler_params=_params(("arbitrary",), 40),
    )(x, gain, target)


def _swap_halves(x):
    lane = lax.broadcasted_iota(jnp.int32, x.shape, 1)
    return jnp.where((lane % HEAD_DIM) < HEAD_DIM // 2, pltpu.roll(x, LANES - HEAD_DIM // 2, 1),
                     pltpu.roll(x, HEAD_DIM // 2, 1))


def _rope_tables(t):
    inv_freq = ROPE_THETA ** (-jnp.arange(0, HEAD_DIM, 2, dtype=F32) / HEAD_DIM)
    ang = jnp.arange(t, dtype=F32)[:, None] * inv_freq[None, :]
    cos = jnp.tile(jnp.cos(ang), (1, LANES // (HEAD_DIM // 2)))
    sin = jnp.tile(jnp.sin(ang), (1, LANES // (HEAD_DIM // 2)))
    lane = jnp.arange(LANES)[None, :]
    return cos, jnp.where((lane % HEAD_DIM) < HEAD_DIM // 2, -sin, sin)


def _rope_fwd(proj, cos, sin, name):
    t = proj.shape[0]
    scale = HEAD_DIM ** -0.5

    def body(p_ref, c_ref, s_ref, o_ref):
        cos_v, sin_v = c_ref[...], s_ref[...]
        for b in range(N_PBLK):
            cols = slice(b * LANES, (b + 1) * LANES)
            v = p_ref[:, cols]
            if b in ROPE_BLKS:
                v = v * cos_v + _swap_halves(v) * sin_v
            if b in QSCALE_BLKS:
                v = v * scale
            o_ref[:, cols] = v.astype(BF16)

    return pl.pallas_call(
        body, name=name, grid=(t // ROWS,),
        in_specs=[_row_spec(IN_COLS), _row_spec(LANES), _row_spec(LANES)], out_specs=_row_spec(IN_COLS),
        out_shape=pltpu.HBM((t, IN_COLS), BF16), compiler_params=_params(("arbitrary",), 40),
    )(_in_hbm(proj), _in_hbm(cos), _in_hbm(sin))


def _rope_bwd(grads, cos, sin, name):
    t = grads[0].shape[0]
    scale = HEAD_DIM ** -0.5
    group = N_HEADS_B // N_KV_B

    def body(*refs):
        c_ref, s_ref, o_ref = refs[9], refs[10], refs[11]
        cos_v, sin_v = c_ref[...], s_ref[...]

        def kv_sum(ref):
            parts = []
            for g in range(N_KV_B):
                acc = ref[:, g * group * HEAD_DIM:(g * group + 1) * HEAD_DIM]
                for h in range(g * group + 1, (g + 1) * group):
                    acc = acc + ref[:, h * HEAD_DIM:(h + 1) * HEAD_DIM]
                parts.append(acc)
            return jnp.concatenate(parts, axis=1)

        def emit(b, v):
            if b in ROPE_BLKS:
                v = v * cos_v - _swap_halves(v) * sin_v
            if b in QSCALE_BLKS:
                v = v * scale
            o_ref[:, b * LANES:(b + 1) * LANES] = v.astype(BF16)

        starts = (A_BLK[0], A_BLK[1], A_BLK[2], B_BLK[0], None, None, C_BLK[0], C_BLK[1], C_BLK[2])
        for idx, start in enumerate(starts):
            if start is None:
                continue
            for j in range(refs[idx].shape[1] // LANES):
                emit(start + j, refs[idx][:, j * LANES:(j + 1) * LANES])
        emit(B_BLK[1], kv_sum(refs[4]))
        emit(B_BLK[2], kv_sum(refs[5]))

    return pl.pallas_call(
        body, name=name, grid=(t // ROWS,),
        in_specs=[_row_spec(g.shape[1]) for g in grads] + [_row_spec(LANES), _row_spec(LANES)],
        out_specs=_row_spec(IN_COLS),
        out_shape=pltpu.HBM((t, IN_COLS), BF16), compiler_params=_params(("arbitrary",), 40),
    )(*[_in_hbm(g) for g in grads], _in_hbm(cos), _in_hbm(sin))


GROUP_COLS = ((0, WIDTH_A), (WIDTH_A, WIDTH_A + WIDTH_B), (WIDTH_A + WIDTH_B, D_MODEL))


def _mix_fwd(oa, ob, oc, gain, name):
    t = oa.shape[0]

    def body(a_ref, b_ref, c_ref, g_ref, o_ref):
        for ref, (lo, hi) in zip((a_ref, b_ref, c_ref), GROUP_COLS):
            _, n = _rms_stats(ref[...])
            o_ref[:, lo:hi] = (n * g_ref[:, lo:hi]).astype(BF16)

    return pl.pallas_call(
        body, name=name, grid=(t // ROWS,),
        in_specs=[_row_spec(WIDTH_A), _row_spec(WIDTH_B), _row_spec(WIDTH_C), _vec_spec(D_MODEL)],
        out_specs=_row_spec(D_MODEL),
        out_shape=pltpu.HBM((t, D_MODEL), BF16), compiler_params=_params(("arbitrary",), 32),
    )(_in_hbm(oa), _in_hbm(ob), _in_hbm(oc), _in_hbm(gain))


def _mix_bwd(oa, ob, oc, gain, dmixed, name, after=()):
    t = oa.shape[0]
    n_after = len(after)

    def body(a_ref, b_ref, c_ref, g_ref, dm_ref, *rest):
        da_ref, db_ref, dc_ref, dg_ref = rest[n_after:]
        first = pl.program_id(0) == 0
        for ref, dref, (lo, hi) in zip((a_ref, b_ref, c_ref), (da_ref, db_ref, dc_ref), GROUP_COLS):
            r, n = _rms_stats(ref[...])
            dm = dm_ref[:, lo:hi]
            dn = dm * g_ref[:, lo:hi]
            dref[...] = r * (dn - n * jnp.mean(dn * n, axis=-1, keepdims=True))
            part = jnp.sum(dm * n, axis=0, keepdims=True)

            @pl.when(first)
            def _():
                dg_ref[:, lo:hi] = part

            @pl.when(jnp.logical_not(first))
            def _():
                dg_ref[:, lo:hi] += part

    return pl.pallas_call(
        body, name=name, grid=(t // ROWS,),
        in_specs=[_row_spec(WIDTH_A), _row_spec(WIDTH_B), _row_spec(WIDTH_C), _vec_spec(D_MODEL), _row_spec(D_MODEL)]
        + [ANY_SPEC] * n_after,
        out_specs=[_row_spec(WIDTH_A), _row_spec(WIDTH_B), _row_spec(WIDTH_C), _vec_spec(D_MODEL)],
        out_shape=[pltpu.HBM((t, WIDTH_A), F32), pltpu.HBM((t, WIDTH_B), F32), pltpu.HBM((t, WIDTH_C), F32),
                   jax.ShapeDtypeStruct((1, D_MODEL), F32)],
        compiler_params=_params(("arbitrary",), 40),
    )(_in_hbm(oa), _in_hbm(ob), _in_hbm(oc), _in_hbm(gain), _in_hbm(dmixed), *after)


FF_COLS = 256


SUBLANES = 8
CHUNK_FWD = 256
CHUNK_BWD = 128
HALO = SUBLANES


def _ext_rows(ref, r0, chunk, where):
    t, cols = ref.shape
    zeros = jnp.zeros((HALO, cols), F32)
    if where == "first":
        return jnp.concatenate([zeros, ref[0:chunk + HALO, :]], axis=0)
    if where == "last":
        return jnp.concatenate([ref[t - chunk - HALO:t, :], zeros], axis=0)
    return ref[pl.ds(pl.multiple_of(r0 - HALO, HALO), chunk + 2 * HALO), :]


def _for_chunks(t, chunk, fn):
    fn(0, "first")

    def mid(ci, carry):
        fn(pl.multiple_of(ci * chunk, chunk), "mid")
        return carry

    lax.fori_loop(1, t // chunk - 1, mid, 0)
    fn(t - chunk, "last")


def _roll_rows(x, by):
    return pltpu.roll(x, by % x.shape[0], 0)


def _gate_val(u_ref, r0, chunk, where, w_ref, b_ref):
    ext = [_ext_rows(u_ref.at[h], r0, chunk, where) for h in range(2)]
    before = [_roll_rows(e, 1) for e in ext]
    after = [_roll_rows(e, -1) for e in ext]
    gate, val = ((before[h] * w_ref[h, 0:1, :] + ext[h] * w_ref[h, 1:2, :]) + after[h] * w_ref[h, 2:3, :] + b_ref[h]
                 for h in range(2))
    return gate, val, ext, before, after


def _ff_specs(t):
    u_spec = pl.BlockSpec((2, t, FF_COLS), lambda j: (0, 0, j))
    w_spec = pl.BlockSpec((2, 3, FF_COLS), lambda j: (0, 0, j))
    b_spec = pl.BlockSpec((2, 1, FF_COLS), lambda j: (0, 0, j))
    return u_spec, w_spec, b_spec


def _convgate_fwd(u0, conv_w, conv_b, name):
    t = u0.shape[1]
    u_spec, w_spec, b_spec = _ff_specs(t)

    def body(u_ref, w_ref, b_ref, o_ref):
        def chunk(r0, where):
            gate, val, _, _, _ = _gate_val(u_ref, r0, CHUNK_FWD, where, w_ref, b_ref)
            act = gate * jax.nn.sigmoid(gate) * val
            o_ref[pl.ds(r0, CHUNK_FWD), :] = act[HALO:HALO + CHUNK_FWD].astype(BF16)

        _for_chunks(t, CHUNK_FWD, chunk)

    return pl.pallas_call(
        body, name=name, grid=(D_FF // FF_COLS,), in_specs=[u_spec, w_spec, b_spec],
        out_specs=pl.BlockSpec((t, FF_COLS), lambda j: (0, j)),
        out_shape=pltpu.HBM((t, D_FF), BF16), compiler_params=_params(("arbitrary",), 48),
    )(_in_hbm(u0), conv_w, conv_b)


def _convgate_bwd(u0, conv_w, conv_b, d_act, name):
    t = u0.shape[1]
    u_spec, w_spec, b_spec = _ff_specs(t)

    def body(u_ref, w_ref, b_ref, da_ref, du_ref, dw_ref, db_ref, sums_ref):
        sums_ref[...] = jnp.zeros_like(sums_ref)
        inner = slice(HALO, HALO + CHUNK_BWD)

        def fold(x):
            return jnp.sum(x.reshape(CHUNK_BWD // SUBLANES, SUBLANES, x.shape[1]), axis=0)

        def chunk(r0, where):
            gate, val, ext, before, after = _gate_val(u_ref, r0, CHUNK_BWD, where, w_ref, b_ref)
            sig = jax.nn.sigmoid(gate)
            da = _ext_rows(da_ref, r0, CHUNK_BWD, where)
            d_half = (da * val * (sig * (1.0 + gate * (1.0 - sig))), da * (gate * sig))
            for h in range(2):
                du = d_half[h]
                for k, term in enumerate((du, du * before[h], du * ext[h], du * after[h])):
                    sums_ref[h, k] += fold(term[inner])
                du0 = (_roll_rows(du, -1) * w_ref[h, 0:1, :] + du * w_ref[h, 1:2, :]) + _roll_rows(du, 1) * w_ref[h, 2:3, :]
                du_ref[h, pl.ds(r0, CHUNK_BWD), :] = du0[inner].astype(BF16)

        _for_chunks(t, CHUNK_BWD, chunk)
        for h in range(2):
            db_ref[h] = jnp.sum(sums_ref[h, 0], axis=0, keepdims=True)
            for k in range(3):
                dw_ref[h, k:k + 1, :] = jnp.sum(sums_ref[h, k + 1], axis=0, keepdims=True)

    return pl.pallas_call(
        body, name=name, grid=(D_FF // FF_COLS,),
        in_specs=[u_spec, w_spec, b_spec, pl.BlockSpec((t, FF_COLS), lambda j: (0, j))],
        out_specs=[u_spec, w_spec, b_spec],
        out_shape=[pltpu.HBM((2, t, D_FF), BF16), jax.ShapeDtypeStruct((2, 3, D_FF), F32),
                   jax.ShapeDtypeStruct((2, 1, D_FF), F32)],
        scratch_shapes=[pltpu.VMEM((2, 4, SUBLANES, FF_COLS), F32)],
        compiler_params=_params(("arbitrary",), 56),
    )(_in_hbm(u0), conv_w, conv_b, _in_hbm(d_act))


class _Group:
    def __init__(self, heads, blks, kv_rows, n_win, gqa, bias_per_head):
        self.heads = heads
        self.pairs = heads // 2
        self.q_blk, self.k_blk, self.v_blk = blks
        self.kv_rows = kv_rows
        self.n_win = n_win
        self.full = kv_rows == SEQ
        self.gqa = gqa
        self.bias_per_head = bias_per_head
        self.width = heads * HEAD_DIM
        self.keys = kv_rows * n_win


GROUP_A = _Group(N_HEADS_A, A_BLK, SEQ, 1, False, False)
GROUP_B = _Group(N_HEADS_B, B_BLK, WINDOW_B, 4, True, False)
GROUP_C = _Group(N_HEADS_C, C_BLK, QB, 3, False, True)


def _win_start(grp, i):
    return jnp.clip(i * (QB // grp.kv_rows) - 1, 0, SEQ // grp.kv_rows - grp.n_win)


def _win_variant(i):
    return jnp.minimum(i, 1) + (i == NQB - 1).astype(jnp.int32)


def _attn_in_specs(grp, t):
    q_spec = pl.BlockSpec((QB, LANES), lambda p, i: (i, grp.q_blk + p))

    def col(blk):
        return (lambda p: blk) if grp.gqa else (lambda p: blk + p)

    def kv_specs(blk):
        c = col(blk)
        if grp.full:
            return [pl.BlockSpec((t, LANES), lambda p, i: (0, c(p)))]
        return [pl.BlockSpec((grp.kv_rows, LANES),
                             functools.partial(lambda p, i, w: (_win_start(grp, i) + w, c(p)), w=w))
                for w in range(grp.n_win)]

    nwk = grp.keys
    if grp.bias_per_head:
        bias_spec = pl.BlockSpec((2, None, QB, nwk), lambda p, i: (p, _win_variant(i), 0, 0))
    elif grp.full:
        bias_spec = pl.BlockSpec((1, None, QB, nwk), lambda p, i: (0, i, 0, 0))
    else:
        bias_spec = pl.BlockSpec((1, None, QB, nwk), lambda p, i: (0, _win_variant(i), 0, 0))
    sink_spec = pl.BlockSpec((1, LANES), lambda p, i: (0, p))
    return q_spec, kv_specs(grp.k_blk), kv_specs(grp.v_blk), bias_spec, sink_spec


def _head_kv(grp, whole, e, p):
    lo, hi = whole[:, :HEAD_DIM], whole[:, HEAD_DIM:]
    if grp.gqa:
        return jnp.where(2 * p + e >= N_HEADS_B // N_KV_B, hi, lo)
    return hi if e else lo


def _softmax_parts(q, k, bias, sink):
    s = lax.dot_general(q, k, (((1,), (1,)), ((), ())), preferred_element_type=F32) + bias
    m = jnp.maximum(jnp.max(s, axis=-1, keepdims=True), sink)
    pe = jnp.exp(s - m)
    denom = jnp.sum(pe, axis=-1, keepdims=True) + jnp.exp(sink - m)
    return pe, m, 1.0 / denom


def _attn_fwd(grp, proj, bias, sink, name):
    t = proj.shape[0]
    q_spec, k_specs, v_specs, bias_spec, sink_spec = _attn_in_specs(grp, t)
    nkv = len(k_specs)

    def body(*refs):
        q_ref = refs[0]
        k_refs, v_refs = refs[1:1 + nkv], refs[1 + nkv:1 + 2 * nkv]
        bias_ref, sink_ref, o_ref = refs[1 + 2 * nkv:4 + 2 * nkv]
        p = pl.program_id(0)
        k_all = jnp.concatenate([r[...] for r in k_refs], axis=0)
        v_all = jnp.concatenate([r[...] for r in v_refs], axis=0)
        outs = []
        for e in range(2):
            q = q_ref[:, e * HEAD_DIM:(e + 1) * HEAD_DIM]
            k = _head_kv(grp, k_all, e, p)
            v = _head_kv(grp, v_all, e, p)
            snk = sink_ref[0:1, e * HEAD_DIM:e * HEAD_DIM + 1]
            pe, _, inv = _softmax_parts(q, k, bias_ref[e if grp.bias_per_head else 0], snk)
            outs.append(jnp.dot(pe.astype(BF16), v, preferred_element_type=F32) * inv)
        o_ref[...] = jnp.concatenate(outs, axis=1)

    return pl.pallas_call(
        body, name=name, grid=(grp.pairs, NQB),
        in_specs=[q_spec, *k_specs, *v_specs, bias_spec, sink_spec],
        out_specs=pl.BlockSpec((QB, LANES), lambda p, i: (i, p)),
        out_shape=pltpu.HBM((t, grp.width), F32),
        compiler_params=_params(("arbitrary", "arbitrary"), 48),
    )(*([_in_hbm(proj)] * (1 + 2 * nkv)), _in_hbm(bias), sink)


def _attn_bwd(grp, proj, bias, sink, out, d_out, name):
    t = proj.shape[0]
    q_spec, k_specs, v_specs, bias_spec, sink_spec = _attn_in_specs(grp, t)
    nkv = len(k_specs)
    n_off = 2 * NA_ROWS - 1
    rows_q = QB // GRID_W
    wide = grp.keys > 2 * QB
    o_spec = pl.BlockSpec((QB, LANES), lambda p, i: (i, p))
    acc_spec = pl.BlockSpec((t, LANES), lambda p, i: (0, p))
    out_specs = [o_spec, acc_spec, acc_spec, pl.BlockSpec((None, 8, LANES), lambda p, i: (p, 0, 0))]
    out_shape = [pltpu.HBM((t, grp.width), F32)] * 3 + [jax.ShapeDtypeStruct((grp.pairs, 8, LANES), F32)]
    if grp.bias_per_head:
        out_specs.append(pl.BlockSpec((2, n_off, GRID_W, GRID_W), lambda p, i: (p, 0, 0, 0)))
        out_shape.append(jax.ShapeDtypeStruct((grp.heads, n_off, GRID_W, GRID_W), F32))

    def body(*refs):
        q_ref = refs[0]
        k_refs, v_refs = refs[1:1 + nkv], refs[1 + nkv:1 + 2 * nkv]
        bias_ref, sink_ref, o_ref, do_ref = refs[1 + 2 * nkv:5 + 2 * nkv]
        dq_ref, dk_ref, dv_ref, dsink_ref = refs[5 + 2 * nkv:9 + 2 * nkv]
        dbias_ref = refs[9 + 2 * nkv] if grp.bias_per_head else None
        p, i = pl.program_id(0), pl.program_id(1)

        @pl.when(i == 0)
        def _():
            dk_ref[...] = jnp.zeros_like(dk_ref)
            dv_ref[...] = jnp.zeros_like(dv_ref)
            dsink_ref[...] = jnp.zeros_like(dsink_ref)
            if dbias_ref is not None:
                dbias_ref[...] = jnp.zeros_like(dbias_ref)

        k_all = jnp.concatenate([r[...] for r in k_refs], axis=0)
        v_all = jnp.concatenate([r[...] for r in v_refs], axis=0)
        start = 0 if grp.full else _win_start(grp, i)
        dqs, dks, dvs, dsinks = [], [], [], []
        for e in range(2):
            cols = slice(e * HEAD_DIM, (e + 1) * HEAD_DIM)
            q = q_ref[:, cols]
            k = _head_kv(grp, k_all, e, p)
            v = _head_kv(grp, v_all, e, p)
            snk = sink_ref[0:1, e * HEAD_DIM:e * HEAD_DIM + 1]
            pe, m, inv = _softmax_parts(q, k, bias_ref[e if grp.bias_per_head else 0], snk)
            prob = pe * inv
            do = do_ref[:, cols]
            do_b = do.astype(BF16)
            pe_b = prob.astype(BF16)
            delta = jnp.sum(do * o_ref[:, cols], axis=-1, keepdims=True)
            dp = lax.dot_general(do_b, v, (((1,), (1,)), ((), ())), preferred_element_type=F32)
            ds = prob * (dp - delta)
            ds_b = ds.astype(BF16)
            dqs.append(jnp.dot(ds_b, k, preferred_element_type=F32))
            if wide:
                dks.append(lax.dot_general(q, ds_b, (((0,), (0,)), ((), ())), preferred_element_type=F32))
                dvs.append(lax.dot_general(do_b, pe_b, (((0,), (0,)), ((), ())), preferred_element_type=F32))
            else:
                dks.append(lax.dot_general(ds_b, q, (((0,), (0,)), ((), ())), preferred_element_type=F32))
                dvs.append(lax.dot_general(pe_b, do_b, (((0,), (0,)), ((), ())), preferred_element_type=F32))
            dsinks.append(-jnp.sum(jnp.exp(snk - m) * inv * delta, axis=0, keepdims=True))
            if dbias_ref is not None:
                shift = (i * QB - start * grp.kv_rows) // GRID_W
                for rq in range(rows_q):
                    for rk in range(grp.keys // GRID_W):
                        off = jnp.clip(rk - rq + (NA_ROWS - 1) - shift, 0, n_off - 1)
                        dbias_ref[e, off] += ds[rq * GRID_W:(rq + 1) * GRID_W, rk * GRID_W:(rk + 1) * GRID_W]
        dq_ref[...] = jnp.concatenate(dqs, axis=1)
        rows = pl.ds(0, t) if grp.full else pl.ds(pl.multiple_of(start * grp.kv_rows, grp.kv_rows), grp.keys)
        if wide:
            dk_ref[rows, :] += jnp.concatenate(dks, axis=0).T
            dv_ref[rows, :] += jnp.concatenate(dvs, axis=0).T
        else:
            dk_ref[rows, :] += jnp.concatenate(dks, axis=1)
            dv_ref[rows, :] += jnp.concatenate(dvs, axis=1)
        lane = lax.broadcasted_iota(jnp.int32, (8, LANES), 1)
        dsink_ref[...] += jnp.where(lane < HEAD_DIM, dsinks[0], dsinks[1])

    return pl.pallas_call(
        body, name=name, grid=(grp.pairs, NQB),
        in_specs=[q_spec, *k_specs, *v_specs, bias_spec, sink_spec, o_spec, o_spec],
        out_specs=out_specs, out_shape=out_shape,
        compiler_params=_params(("arbitrary", "arbitrary"), 56),
    )(*([_in_hbm(proj)] * (1 + 2 * nkv)), _in_hbm(bias), sink, _in_hbm(out), _in_hbm(d_out))


DILATED_CONFIGS = ((128, 1), (512, 4), (2048, 16))


def _bias_a():
    d = jnp.arange(SEQ)[None, :] - jnp.arange(SEQ)[:, None]
    mult = jnp.zeros((SEQ, SEQ), F32)
    for window, r in DILATED_CONFIGS:
        reach = (window // (2 * r)) * r
        mult = mult + ((d % r == 0) & (jnp.abs(d) <= reach)).astype(F32)
    return jnp.where(mult > 0, jnp.log(jnp.maximum(mult, 1.0)), NEG_INF).reshape(1, NQB, QB, SEQ)


def _bias_b():
    row = jnp.arange(QB)[None, :, None]
    col = jnp.arange(GROUP_B.keys)[None, None, :]
    var = jnp.arange(3)[:, None, None]
    d = col - (GROUP_B.kv_rows * var + row)
    return jnp.where(jnp.abs(d) <= WINDOW_B, 0.0, NEG_INF).astype(F32)[None]


def _offset_onehot():
    c = jnp.arange(GRID_W)[:, None, None]
    c2 = jnp.arange(GRID_W)[None, :, None]
    b = jnp.arange(LANES)[None, None, :]
    return (c2 - c + NA_COLS - 1 == b).astype(BF16).reshape(GRID_W * GRID_W, LANES)


def _split_dot(x, g):
    hi = x.astype(BF16)
    rest = x - hi.astype(F32)
    mid = rest.astype(BF16)
    lo = (rest - mid.astype(F32)).astype(BF16)
    return (jnp.dot(hi, g, preferred_element_type=F32) + jnp.dot(mid, g, preferred_element_type=F32)
            + jnp.dot(lo, g, preferred_element_type=F32))


def _table_mm(x, g, name):
    def body(x_ref, g_ref, o_ref):
        o_ref[...] = _split_dot(x_ref[...], g_ref[...])

    return pl.pallas_call(
        body, name=name, out_shape=jax.ShapeDtypeStruct((x.shape[0], g.shape[1]), F32),
        in_specs=[pl.BlockSpec(memory_space=pltpu.VMEM)] * 2, out_specs=pl.BlockSpec(memory_space=pltpu.VMEM),
        compiler_params=pltpu.CompilerParams(vmem_limit_bytes=32 * MIB),
    )(x, g)


N_OFF = 2 * NA_ROWS - 1
TABLE_ROWS = 152


def _bias_c(rpb):
    table = jnp.zeros((TABLE_ROWS, LANES), F32).at[:N_HEADS_C * N_OFF, :2 * NA_COLS - 1].set(
        rpb.reshape(N_HEADS_C * N_OFF, 2 * NA_COLS - 1))
    tiles = _table_mm(table, _offset_onehot().T, "rpb_tiles")[:N_HEADS_C * N_OFF]
    tiles = tiles.reshape(N_HEADS_C, N_OFF, GRID_W, GRID_W)
    c = jnp.arange(GRID_W)
    col_start = jnp.clip(c - NA_COLS // 2, 0, GRID_W - NA_COLS)
    col_ok = (c[None, :] >= col_start[:, None]) & (c[None, :] < col_start[:, None] + NA_COLS)
    tiles = jnp.where(col_ok, tiles, NEG_INF)
    rows_q = QB // GRID_W
    rows_k = GROUP_C.keys // GRID_W

    def body(t_ref, o_ref):
        for var in range(3):
            for rq in range(rows_q):
                r_l = rows_q * var + rq
                first = min(max(r_l - NA_ROWS // 2, 0), rows_k - NA_ROWS)
                for rk in range(rows_k):
                    if first <= rk < first + NA_ROWS:
                        tile = t_ref[rk - r_l + NA_ROWS - 1]
                    else:
                        tile = jnp.full((GRID_W, GRID_W), NEG_INF, F32)
                    o_ref[var, rq * GRID_W:(rq + 1) * GRID_W, rk * GRID_W:(rk + 1) * GRID_W] = tile

    return pl.pallas_call(
        body, name="bias_c", grid=(N_HEADS_C,),
        in_specs=[pl.BlockSpec((None, N_OFF, GRID_W, GRID_W), lambda h: (h, 0, 0, 0))],
        out_specs=pl.BlockSpec((None, 3, QB, GROUP_C.keys), lambda h: (h, 0, 0, 0)),
        out_shape=jax.ShapeDtypeStruct((N_HEADS_C, 3, QB, GROUP_C.keys), F32),
        compiler_params=_params(("arbitrary",), 32),
    )(tiles)


def _rpb_grad(d_tiles):
    flat = jnp.zeros((TABLE_ROWS, GRID_W * GRID_W), F32).at[:N_HEADS_C * N_OFF].set(
        d_tiles.reshape(N_HEADS_C * N_OFF, GRID_W * GRID_W))
    out = _table_mm(flat, _offset_onehot(), "rpb_grad")
    return out[:N_HEADS_C * N_OFF, :2 * NA_COLS - 1].reshape(N_HEADS_C, N_OFF, 2 * NA_COLS - 1)


def _sink_lanes(sink):
    return jnp.repeat(sink.astype(F32), HEAD_DIM)[None, :]


def _attention_fwd(proj_r, sink_b, bias_a, bias_b, bias_c):
    no_sink_a = jnp.full((1, WIDTH_A), NEG_INF, F32)
    no_sink_c = jnp.full((1, WIDTH_C), NEG_INF, F32)
    oa = _attn_fwd(GROUP_A, proj_r, bias_a, no_sink_a, "attn_a_fwd")
    ob = _attn_fwd(GROUP_B, proj_r, bias_b, _sink_lanes(sink_b), "attn_b_fwd")
    oc = _attn_fwd(GROUP_C, proj_r, bias_c, no_sink_c, "attn_c_fwd")
    return oa, ob, oc


def _attention_bwd(proj_r, sink_b, bias_a, bias_b, bias_c, outs, d_outs, cos, sin):
    no_sink_a = jnp.full((1, WIDTH_A), NEG_INF, F32)
    no_sink_c = jnp.full((1, WIDTH_C), NEG_INF, F32)
    dqa, dka, dva, _ = _attn_bwd(GROUP_A, proj_r, bias_a, no_sink_a, outs[0], d_outs[0], "attn_a_bwd")
    dqb, dkb, dvb, dsink = _attn_bwd(GROUP_B, proj_r, bias_b, _sink_lanes(sink_b), outs[1], d_outs[1], "attn_b_bwd")
    dqc, dkc, dvc, _, d_tiles = _attn_bwd(GROUP_C, proj_r, bias_c, no_sink_c, outs[2], d_outs[2], "attn_c_bwd")
    d_proj = _rope_bwd((dqa, dka, dva, dqb, dkb, dvb, dqc, dkc, dvc), cos, sin, "rope_bwd")
    d_sink = dsink[:, 0, :].reshape(GROUP_B.pairs, 2, HEAD_DIM)[:, :, 0].reshape(N_HEADS_B)
    return d_proj, d_sink, _rpb_grad(d_tiles)


def _adamw(w, g, m, v, name):
    r, c = w.shape
    rows = r
    for cand in (512, 256, 128, 64, 32, 16, 8):
        if r % cand == 0 and cand * c * 4 <= MIB:
            rows = cand
            break
    spec = pl.BlockSpec((rows, c), lambda i: (i, 0))

    def body(w_ref, g_ref, m_ref, v_ref, d_ref, mo_ref, vo_ref):
        d_ref[...], mo_ref[...], vo_ref[...] = _adamw_step(w_ref[...], g_ref[...], m_ref[...], v_ref[...])

    return pl.pallas_call(
        body, name=name, grid=(r // rows,), in_specs=[spec] * 4, out_specs=[spec] * 3,
        out_shape=[jax.ShapeDtypeStruct((r, c), F32)] * 3, compiler_params=_params(("arbitrary",), 32),
    )(w, g, m, v)


def _adamw_step(w, grad, m, v):
    m_new = ADAM_B1 * m + (1.0 - ADAM_B1) * grad
    v_new = ADAM_B2 * v + (1.0 - ADAM_B2) * jnp.square(grad)
    m_hat = m_new / (1.0 - ADAM_B1 ** ADAM_STEP)
    v_hat = v_new / (1.0 - ADAM_B2 ** ADAM_STEP)
    return -ADAM_LR * (m_hat / (jnp.sqrt(v_hat) + ADAM_EPS) + ADAM_WD * w), m_new, v_new


def _adamw_layer(w, g, m, v, layer, prev, name):
    _, r, c = w.shape
    rows = next(cand for cand in (512, 256, 128, 64, 32, 16, 8) if r % cand == 0 and cand * c * 4 <= 2 * MIB)
    spec = pl.BlockSpec((None, rows, c), lambda i: (layer, i, 0))
    g_spec = pl.BlockSpec((rows, c), lambda i: (i, 0))
    n_prev = 0 if prev is None else 4

    def body(w_ref, g_ref, m_ref, v_ref, *rest):
        go_ref, d_ref, mo_ref, vo_ref = rest[n_prev:]
        grad = g_ref[...]
        go_ref[...] = grad
        d_ref[...], mo_ref[...], vo_ref[...] = _adamw_step(w_ref[...], grad, m_ref[...], v_ref[...])

    return pl.pallas_call(
        body, name=name, grid=(r // rows,), in_specs=[spec, g_spec, spec, spec] + [ANY_SPEC] * n_prev,
        out_specs=[spec] * 4,
        out_shape=[jax.ShapeDtypeStruct(w.shape, F32)] * 4,
        input_output_aliases={4 + i: i for i in range(n_prev)}, compiler_params=_params(("arbitrary",), 48),
    )(w, g, m, v, *(prev or ()))


def _layer_fwd(x0, p, weight, tabs):
    h1 = _rmsnorm_fwd(x0, p["ln_attn"], "ln_attn_fwd")
    proj = _mm_nn(h1, weight("w_in", h1), cols=True, tn=256, tk=D_MODEL, out_dtype=F32, name="mm_in")
    proj_r = _rope_fwd(proj, tabs["cos"], tabs["sin"], "rope_fwd")
    outs = _attention_fwd(proj_r, p["sink_b"], tabs["bias_a"], tabs["bias_b"], p["bias_c"])
    mixed = _mix_fwd(*outs, p["mix_gain"], "mix_fwd")
    x1 = _mm_nn(mixed, weight("w_out", mixed), cols=False, tn=256, tk=D_MODEL, out_dtype=F32, name="mm_out",
                residual=x0)
    h2 = _rmsnorm_fwd(x1, p["ln_ffn"], "ln_ffn_fwd")
    u0 = _mm_nn(h2, weight("w_up", h2), cols=True, tn=256, tk=D_MODEL, out_dtype=F32, name="mm_up", out_split=2)
    act = _convgate_fwd(u0, p["conv_w"], p["conv_b"], "convgate_fwd")
    x2 = _mm_nn(act, weight("w_down", act), cols=False, tn=512, tk=D_FF // 2, out_dtype=F32, name="mm_down",
                residual=x1)
    return x2, (x0, h1, proj_r, outs, mixed, x1, h2, u0, act)


def _layer_bwd(dx2, dx2_b, saved, p, big, tabs, begin, finish, pending):
    x0, h1, proj_r, outs, mixed, x1, h2, u0, act = saved
    d_act = _mm_nt(dx2_b, big["w_down"], cols=False, to=512, tr=D_MODEL, out_dtype=F32, name="nt_down",
                   after=[pending[1]] if pending else [])
    g_down = _mm_tn(act, dx2_b, tk=D_FF // N_SHARDS, tn=D_MODEL, shards=-N_SHARDS, name="tn_down")
    du0, d_conv_w, d_conv_b = _convgate_bwd(u0, p["conv_w"], p["conv_b"], d_act, "convgate_bwd")
    token = [finish(pending[0], [du0])] if pending else []
    dh2 = _mm_nt(du0, big["w_up"], cols=True, to=1024, tr=D_FF // 4, out_dtype=F32, name="nt_up", after=token)
    g_up = _mm_tn(h2, du0, tk=1024, tn=D_FF // 4, shards=N_SHARDS, name="tn_up")
    first, token = begin({"w_down": g_down, "w_up": g_up})
    dx1, dx1_b, d_ln_ffn = _rmsnorm_bwd(x1, p["ln_ffn"], dh2, dx2, "ln_ffn_bwd", after=[token])
    d_mixed = _mm_nt(dx1_b, big["w_out"], cols=False, to=512, tr=D_MODEL, out_dtype=F32, name="nt_out")
    g_out = _mm_tn(mixed, dx1_b, tk=D_MODEL // N_SHARDS, tn=D_MODEL, shards=-N_SHARDS, name="tn_out")
    token = finish(first, [g_out])
    *d_outs, d_mix_gain = _mix_bwd(*outs, p["mix_gain"], d_mixed, "mix_bwd", after=[token])
    d_proj, d_sink, d_rpb = _attention_bwd(proj_r, p["sink_b"], tabs["bias_a"], tabs["bias_b"], p["bias_c"], outs,
                                           d_outs, tabs["cos"], tabs["sin"])
    dh1 = _mm_nt(d_proj, big["w_in"], cols=True, to=1024, tr=IN_COLS // N_SHARDS, out_dtype=F32, name="nt_in")
    g_in = _mm_tn(h1, d_proj, tk=1024, tn=IN_COLS // N_SHARDS, shards=N_SHARDS, name="tn_in")
    dx0, dx0_b, d_ln_attn = _rmsnorm_bwd(x0, p["ln_attn"], dh1, dx1, "ln_attn_bwd")
    small = {"ln_attn": d_ln_attn, "sink_b": d_sink, "rpb_c": d_rpb, "mix_gain": d_mix_gain, "ln_ffn": d_ln_ffn,
             "conv_w": d_conv_w, "conv_b": d_conv_b}
    return dx0, dx0_b, small, begin({"w_out": g_out, "w_in": g_in})


HBM_SPEC = pl.BlockSpec(memory_space=pl.ANY)


def _place():
    x, y, c = lax.axis_index("x"), lax.axis_index("y"), lax.axis_index("c")
    chips = ((1 - x, y), (x, 1 - y), (1 - x, 1 - y))
    return x, y, c, chips


def _shard_index(px, py):
    return 2 * px + py


def _remote(src, dst, send_sem, recv_sem, to):
    return pltpu.make_async_remote_copy(src_ref=src, dst_ref=dst, send_sem=send_sem, recv_sem=recv_sem,
                                        device_id=to, device_id_type=MESH)


def _own_slot(w, layer, shard, name):
    _, r, c_dim = w.shape
    rows = r
    for cand in (512, 256, 128):
        if r % cand == 0 and cand * c_dim * 4 <= 2 * MIB:
            rows = cand
            break

    def body(s_ref, w_ref, o_ref):
        o_ref[...] = w_ref[...].astype(BF16)

    return pl.pallas_call(
        body, name=name,
        grid_spec=pltpu.PrefetchScalarGridSpec(
            num_scalar_prefetch=1, grid=(r // rows,),
            in_specs=[pl.BlockSpec((None, rows, c_dim), lambda i, s: (layer, i, 0))],
            out_specs=pl.BlockSpec((None, rows, c_dim), lambda i, s: (s[0], i, 0))),
        out_shape=jax.ShapeDtypeStruct((N_SHARDS, r, c_dim), BF16),
        compiler_params=_params(("arbitrary",), 32),
    )(shard.astype(jnp.int32).reshape(1), w)


HBM_ONLY = pl.BlockSpec(memory_space=pltpu.HBM)
SEM_SPEC = pl.BlockSpec(memory_space=pltpu.SEMAPHORE)
DATAFLOW = pltpu.SideEffectType.DATAFLOW_SIDE_EFFECTING


def _in_hbm(a):
    return pltpu.with_memory_space_constraint(a, pltpu.HBM)


N_DEV = 8


def _peers(x, y, c):
    flips = [(fx, fy, fc) for fx in (0, 1) for fy in (0, 1) for fc in (0, 1)][1:]
    return [((1 - x) if fx else x, (1 - y) if fy else y, (1 - c) if fc else c) for fx, fy, fc in flips]


def _small_start(vec, after, name):
    n_after = len(after)

    def body(v_ref, slots_ref, *rest):
        send, recv = rest[n_after], rest[n_after + 1]
        token = rest[-1]
        x, y, c, _ = _place()
        me = 4 * x + 2 * y + c
        for k, peer in enumerate(_peers(x, y, c)):
            _remote(v_ref, slots_ref.at[me], send.at[k], recv.at[k], peer).start()
        token[...] = jnp.zeros_like(token)

    slots = jax.ShapeDtypeStruct((N_DEV,) + vec.shape, vec.dtype)
    res = pl.pallas_call(
        body, name=name,
        out_shape=(pltpu.SemaphoreType.DMA((N_DEV - 1,)), pltpu.SemaphoreType.DMA((N_DEV - 1,)),
                   pltpu.HBM(vec.shape, vec.dtype), pltpu.HBM(slots.shape, slots.dtype),
                   jax.ShapeDtypeStruct((8, LANES), F32)),
        in_specs=[HBM_ONLY, HBM_ONLY] + [ANY_SPEC] * n_after,
        out_specs=(SEM_SPEC, SEM_SPEC, HBM_ONLY, HBM_ONLY, pl.BlockSpec(memory_space=pltpu.VMEM)),
        input_output_aliases={0: 2, 1: 3},
        compiler_params=pltpu.CompilerParams(has_side_effects=DATAFLOW),
    )(_in_hbm(vec), _in_hbm(lax.empty(slots.shape, slots.dtype)), *after)
    return res


def _small_wait(send, recv, vec, slots, after, name):
    def body(v_ref, slots_ref, send_ref, recv_ref, *rest):
        x, y, c, _ = _place()
        for k, (px, py, pc) in enumerate(_peers(x, y, c)):
            cp = _remote(v_ref, slots_ref.at[4 * px + 2 * py + pc], send_ref.at[k], recv_ref.at[k], (px, py, pc))
            cp.wait_send()
            cp.wait_recv()

    return pl.pallas_call(
        body, name=name, out_shape=(pltpu.HBM(vec.shape, vec.dtype), pltpu.HBM(slots.shape, slots.dtype)),
        in_specs=[HBM_ONLY, HBM_ONLY, SEM_SPEC, SEM_SPEC] + [ANY_SPEC] * len(after), out_specs=[HBM_ONLY, HBM_ONLY],
        input_output_aliases={0: 0, 1: 1},
        compiler_params=pltpu.CompilerParams(has_side_effects=DATAFLOW),
    )(vec, slots, send, recv, *after)


def _small_sum(vec, slots, name):
    rows = vec.shape[0]
    blk = min(rows, 256)
    x, y, c = lax.axis_index("x"), lax.axis_index("y"), lax.axis_index("c")
    me = (4 * x + 2 * y + c).astype(jnp.int32).reshape(1)

    def slot_spec(k):
        return pl.BlockSpec((None, blk, LANES), lambda i, w: (jnp.where(w[0] == k, (k + 1) % N_DEV, k), i, 0))

    def body(w_ref, v_ref, *rest):
        o_ref = rest[-1]
        acc = None
        for k in range(N_DEV):
            term = jnp.where(w_ref[0] == k, v_ref[...], rest[k][...])
            acc = term if acc is None else acc + term
        o_ref[...] = acc

    return pl.pallas_call(
        body, name=name,
        grid_spec=pltpu.PrefetchScalarGridSpec(
            num_scalar_prefetch=1, grid=(rows // blk,),
            in_specs=[pl.BlockSpec((blk, LANES), lambda i, w: (i, 0))] + [slot_spec(k) for k in range(N_DEV)],
            out_specs=pl.BlockSpec((blk, LANES), lambda i, w: (i, 0))),
        out_shape=jax.ShapeDtypeStruct(vec.shape, F32), compiler_params=_params(("arbitrary",), 32),
    )(me, vec, *([slots] * N_DEV))


def _half(ref, slot, c):
    half = ref.shape[1] // 2
    return ref.at[slot, pl.ds(pl.multiple_of(c * half, 8), half)]


def _gather_start(bufs, after, name):
    n = len(bufs)
    n_after = len(after)

    def body(*refs):
        ins = refs[:n]
        send, recv = refs[n + n_after], refs[n + n_after + 1]
        token = refs[-1]
        x, y, c, chips = _place()
        me = _shard_index(x, y)
        for t in range(n):
            for j, (px, py) in enumerate(chips):
                mine = _half(ins[t], me, c)
                _remote(mine, mine, send.at[t * 3 + j], recv.at[t * 3 + j], (px, py, c)).start()
        token[...] = jnp.zeros_like(token)

    thru = [pltpu.HBM(b.shape, b.dtype) for b in bufs]
    res = pl.pallas_call(
        body, name=name,
        out_shape=(pltpu.SemaphoreType.DMA((n * 3,)), pltpu.SemaphoreType.DMA((n * 3,)), *thru,
                   jax.ShapeDtypeStruct((8, LANES), F32)),
        in_specs=[HBM_ONLY] * n + [ANY_SPEC] * n_after,
        out_specs=(SEM_SPEC, SEM_SPEC, *([HBM_ONLY] * n), pl.BlockSpec(memory_space=pltpu.VMEM)),
        input_output_aliases={i: 2 + i for i in range(n)},
        compiler_params=pltpu.CompilerParams(has_side_effects=DATAFLOW),
    )(*[_in_hbm(b) for b in bufs], *after)
    return res[0], res[1], list(res[2:2 + n]), res[-1]


def _gather_wait(send, recv, bufs, after, name):
    n = len(bufs)

    def body(*refs):
        ins = refs[:n]
        send_ref, recv_ref = refs[n], refs[n + 1]
        x, y, c, chips = _place()
        me = _shard_index(x, y)
        for t in range(n):
            for j, (px, py) in enumerate(chips):
                cp = _remote(_half(ins[t], me, c), _half(ins[t], _shard_index(px, py), c), send_ref.at[t * 3 + j],
                             recv_ref.at[t * 3 + j], (px, py, c))
                cp.wait_send()
                cp.wait_recv()

    res = pl.pallas_call(
        body, name=name, out_shape=tuple(pltpu.HBM(b.shape, b.dtype) for b in bufs),
        in_specs=[HBM_ONLY] * n + [SEM_SPEC, SEM_SPEC] + [ANY_SPEC] * len(after), out_specs=[HBM_ONLY] * n,
        input_output_aliases={i: i for i in range(n)},
        compiler_params=pltpu.CompilerParams(has_side_effects=DATAFLOW),
    )(*bufs, send, recv, *after)
    return list(res)


def _gather_forward(bufs, name):
    n = len(bufs)

    def body(*refs):
        outs = refs[n:2 * n]
        send, recv = refs[2 * n:]
        x, y, c, chips = _place()
        sibling = (x, y, 1 - c)
        cps = []
        for t in range(n):
            for j, (px, py) in enumerate(chips):
                got = _half(outs[t], _shard_index(px, py), c)
                cp = _remote(got, got, send.at[t * 3 + j], recv.at[t * 3 + j], sibling)
                cp.start()
                cps.append(cp)
        for t in range(n):
            for j, (px, py) in enumerate(chips):
                theirs = _half(outs[t], _shard_index(px, py), 1 - c)
                _remote(theirs, theirs, send.at[t * 3 + j], recv.at[t * 3 + j], sibling).wait_recv()
        for cp in cps:
            cp.wait_send()

    return pl.pallas_call(
        body, name=name, in_specs=[HBM_SPEC] * n, out_specs=[HBM_SPEC] * n,
        out_shape=[jax.ShapeDtypeStruct(b.shape, b.dtype) for b in bufs],
        input_output_aliases={t: t for t in range(n)},
        scratch_shapes=[pltpu.SemaphoreType.DMA((n * 3,))] * 2,
    )(*bufs)


def _gather_forward_start(bufs, carry, name):
    n = len(bufs)

    def body(*refs):
        ins = refs[:n]
        send, recv = refs[n + 1], refs[n + 2]
        x, y, c, chips = _place()
        for t in range(n):
            for j, (px, py) in enumerate(chips):
                got = _half(ins[t], _shard_index(px, py), c)
                _remote(got, got, send.at[t * 3 + j], recv.at[t * 3 + j], (x, y, 1 - c)).start()

    res = pl.pallas_call(
        body, name=name,
        out_shape=(pltpu.SemaphoreType.DMA((n * 3,)), pltpu.SemaphoreType.DMA((n * 3,)),
                   *[pltpu.HBM(b.shape, b.dtype) for b in bufs], pltpu.HBM(carry.shape, carry.dtype)),
        in_specs=[HBM_ONLY] * (n + 1),
        out_specs=(SEM_SPEC, SEM_SPEC, *([HBM_ONLY] * (n + 1))),
        input_output_aliases={i: 2 + i for i in range(n + 1)},
        compiler_params=pltpu.CompilerParams(has_side_effects=DATAFLOW),
    )(*[_in_hbm(b) for b in bufs], _in_hbm(carry))
    return res[0], res[1], list(res[2:2 + n]), res[-1]


def _gather_forward_wait(send, recv, bufs, after, name):
    n = len(bufs)

    def body(*refs):
        ins = refs[:n]
        send_ref, recv_ref = refs[n], refs[n + 1]
        x, y, c, chips = _place()
        for t in range(n):
            for j, (px, py) in enumerate(chips):
                s = _shard_index(px, py)
                cp = _remote(_half(ins[t], s, c), _half(ins[t], s, 1 - c), send_ref.at[t * 3 + j],
                             recv_ref.at[t * 3 + j], (x, y, 1 - c))
                cp.wait_send()
                cp.wait_recv()

    res = pl.pallas_call(
        body, name=name, out_shape=tuple(pltpu.HBM(b.shape, b.dtype) for b in bufs),
        in_specs=[HBM_ONLY] * n + [SEM_SPEC, SEM_SPEC] + [ANY_SPEC] * len(after), out_specs=[HBM_ONLY] * n,
        input_output_aliases={i: i for i in range(n)},
        compiler_params=pltpu.CompilerParams(has_side_effects=DATAFLOW),
    )(*bufs, send, recv, *after)
    return list(res)


def _sibling_rows(ref, c):
    half = ref.shape[1] // 2
    return ref.at[:, pl.ds(pl.multiple_of((1 - c) * half, 8), half)]


def _half_exchange_start(grads, name):
    n = len(grads)

    def body(*refs):
        ins, lands = refs[:n], refs[n:2 * n]
        send, recv = refs[2 * n], refs[2 * n + 1]
        token = refs[-1]
        x, y, c, _ = _place()
        for t in range(n):
            _remote(_sibling_rows(ins[t], c), lands[t], send.at[t], recv.at[t], (x, y, 1 - c)).start()
        token[...] = jnp.zeros_like(token)

    halves = [jax.ShapeDtypeStruct((g.shape[0], g.shape[1] // 2, g.shape[2]), g.dtype) for g in grads]
    res = pl.pallas_call(
        body, name=name,
        out_shape=(pltpu.SemaphoreType.DMA((n,)), pltpu.SemaphoreType.DMA((n,)),
                   *[pltpu.HBM(g.shape, g.dtype) for g in grads], *[pltpu.HBM(h.shape, h.dtype) for h in halves],
                   jax.ShapeDtypeStruct((8, LANES), F32)),
        in_specs=[HBM_ONLY] * (2 * n),
        out_specs=(SEM_SPEC, SEM_SPEC, *([HBM_ONLY] * (2 * n)), pl.BlockSpec(memory_space=pltpu.VMEM)),
        input_output_aliases={i: 2 + i for i in range(2 * n)},
        compiler_params=pltpu.CompilerParams(has_side_effects=DATAFLOW),
    )(*[_in_hbm(g) for g in grads], *[_in_hbm(lax.empty(h.shape, h.dtype)) for h in halves])
    return res[0], res[1], list(res[2:2 + n]), list(res[2 + n:2 + 2 * n]), res[-1]


def _half_exchange_wait(send, recv, grads, lands, after, name):
    n = len(grads)

    def body(*refs):
        ins, got = refs[:n], refs[n:2 * n]
        send_ref, recv_ref = refs[2 * n], refs[2 * n + 1]
        x, y, c, _ = _place()
        for t in range(n):
            cp = _remote(_sibling_rows(ins[t], c), got[t], send_ref.at[t], recv_ref.at[t], (x, y, 1 - c))
            cp.wait_send()
            cp.wait_recv()

    res = pl.pallas_call(
        body, name=name,
        out_shape=(*[pltpu.HBM(g.shape, g.dtype) for g in grads], *[pltpu.HBM(h.shape, h.dtype) for h in lands]),
        in_specs=[HBM_ONLY] * (2 * n) + [SEM_SPEC, SEM_SPEC] + [ANY_SPEC] * len(after),
        out_specs=[HBM_ONLY] * (2 * n),
        input_output_aliases={i: i for i in range(2 * n)},
        compiler_params=pltpu.CompilerParams(has_side_effects=DATAFLOW),
    )(*grads, *lands, send, recv, *after)
    return list(res[:n]), list(res[n:])


def _half_rows(half, c_dim):
    if half * c_dim * 2 <= 3 * MIB:
        return half
    for cand in (512, 256, 128, 64):
        if half % cand == 0 and cand * c_dim * 2 <= 2 * MIB:
            return cand
    raise ValueError((half, c_dim))


def _core_index():
    return lax.axis_index("c").astype(jnp.int32).reshape(1)


def _half_sum(own, other, name):
    s, r, c_dim = own.shape
    rows = _half_rows(r // 2, c_dim)
    per = r // 2 // rows

    def body(c_ref, a_ref, b_ref, o_ref):
        o_ref[...] = (a_ref[...].astype(F32) + b_ref[...].astype(F32)).astype(BF16)

    return pl.pallas_call(
        body, name=name,
        grid_spec=pltpu.PrefetchScalarGridSpec(
            num_scalar_prefetch=1, grid=(s, per),
            in_specs=[pl.BlockSpec((None, rows, c_dim), lambda k, i, c: (k, c[0] * per + i, 0)),
                      pl.BlockSpec((None, rows, c_dim), lambda k, i, c: (k, i, 0))],
            out_specs=pl.BlockSpec((None, rows, c_dim), lambda k, i, c: (k, i, 0))),
        out_shape=pltpu.HBM((s, r // 2, c_dim), BF16), compiler_params=_params(("arbitrary", "arbitrary"), 32),
    )(_core_index(), own, other)


def _reduce_start(pairs, name):
    n = len(pairs)

    def body(*refs):
        ins, lands = refs[:n], refs[n:2 * n]
        send, recv = refs[2 * n], refs[2 * n + 1]
        token = refs[-1]
        x, y, c, chips = _place()
        me = _shard_index(x, y)
        for t in range(n):
            for j, (px, py) in enumerate(chips):
                _remote(ins[t].at[_shard_index(px, py)], lands[t].at[me], send.at[t * 3 + j], recv.at[t * 3 + j],
                        (px, py, c)).start()
        token[...] = jnp.zeros_like(token)

    thru = [pltpu.HBM(b.shape, b.dtype) for b in pairs]
    res = pl.pallas_call(
        body, name=name,
        out_shape=(pltpu.SemaphoreType.DMA((n * 3,)), pltpu.SemaphoreType.DMA((n * 3,)), *thru, *thru,
                   jax.ShapeDtypeStruct((8, LANES), F32)),
        in_specs=[HBM_ONLY] * (2 * n),
        out_specs=(SEM_SPEC, SEM_SPEC, *([HBM_ONLY] * (2 * n)), pl.BlockSpec(memory_space=pltpu.VMEM)),
        input_output_aliases={i: 2 + i for i in range(2 * n)},
        compiler_params=pltpu.CompilerParams(has_side_effects=DATAFLOW),
    )(*[_in_hbm(b) for b in pairs], *[_in_hbm(lax.empty(b.shape, b.dtype)) for b in pairs])
    return res[0], res[1], list(res[2:2 + n]), list(res[2 + n:2 + 2 * n]), res[-1]


def _reduce_wait(send, recv, pairs, lands, after, name):
    n = len(pairs)

    def body(*refs):
        ins, got = refs[:n], refs[n:2 * n]
        send_ref, recv_ref = refs[2 * n], refs[2 * n + 1]
        x, y, c, chips = _place()
        for t in range(n):
            for j, (px, py) in enumerate(chips):
                s = _shard_index(px, py)
                cp = _remote(ins[t].at[s], got[t].at[s], send_ref.at[t * 3 + j], recv_ref.at[t * 3 + j], (px, py, c))
                cp.wait_send()
                cp.wait_recv()

    thru = [pltpu.HBM(b.shape, b.dtype) for b in pairs]
    res = pl.pallas_call(
        body, name=name, out_shape=(*thru, *thru),
        in_specs=[HBM_ONLY] * (2 * n) + [SEM_SPEC, SEM_SPEC] + [ANY_SPEC] * len(after),
        out_specs=[HBM_ONLY] * (2 * n),
        input_output_aliases={i: i for i in range(2 * n)},
        compiler_params=pltpu.CompilerParams(has_side_effects=DATAFLOW),
    )(*pairs, *lands, send, recv, *after)
    return list(res[:n]), list(res[n:])


def _reduce_sum(pair, landed, name):
    s, half, c_dim = pair.shape
    rows = _half_rows(half, c_dim)
    per = half // rows
    shard = _shard_index(lax.axis_index("x"), lax.axis_index("y"))
    where = jnp.stack([shard, lax.axis_index("c")]).astype(jnp.int32)

    def landed_spec(k):
        return pl.BlockSpec((None, rows, c_dim), lambda i, w: (jnp.where(w[0] == k, (k + 1) % s, k), i, 0))

    def body(w_ref, own_ref, *rest):
        o_ref = rest[-1]
        acc = None
        for k in range(s):
            term = jnp.where(w_ref[0] == k, own_ref[...], rest[k][...]).astype(F32)
            acc = term if acc is None else acc + term
        o_ref[...] = acc

    return pl.pallas_call(
        body, name=name,
        grid_spec=pltpu.PrefetchScalarGridSpec(
            num_scalar_prefetch=1, grid=(per,),
            in_specs=[pl.BlockSpec((None, rows, c_dim), lambda i, w: (w[0], i, 0))] + [landed_spec(k) for k in range(s)],
            out_specs=pl.BlockSpec((rows, c_dim), lambda i, w: (w[1] * per + i, 0))),
        out_shape=pltpu.HBM((2 * half, c_dim), F32), compiler_params=_params(("arbitrary",), 48),
    )(where, pair, *([landed] * s))


def _my_rows(ref, c):
    half = ref.shape[0] // 2
    return ref.at[pl.ds(pl.multiple_of(c * half, 8), half)]


def _half_gather_start(bufs, name):
    n = len(bufs)

    def body(*refs):
        ins = refs[:n]
        send, recv = refs[n], refs[n + 1]
        token = refs[-1]
        x, y, c, _ = _place()
        for t in range(n):
            mine = _my_rows(ins[t], c)
            _remote(mine, mine, send.at[t], recv.at[t], (x, y, 1 - c)).start()
        token[...] = jnp.zeros_like(token)

    res = pl.pallas_call(
        body, name=name,
        out_shape=(pltpu.SemaphoreType.DMA((n,)), pltpu.SemaphoreType.DMA((n,)),
                   *[pltpu.HBM(b.shape, b.dtype) for b in bufs], jax.ShapeDtypeStruct((8, LANES), F32)),
        in_specs=[HBM_ONLY] * n,
        out_specs=(SEM_SPEC, SEM_SPEC, *([HBM_ONLY] * n), pl.BlockSpec(memory_space=pltpu.VMEM)),
        input_output_aliases={i: 2 + i for i in range(n)},
        compiler_params=pltpu.CompilerParams(has_side_effects=DATAFLOW),
    )(*[_in_hbm(b) for b in bufs])
    return res[0], res[1], list(res[2:2 + n]), res[-1]


def _half_gather_wait(send, recv, bufs, after, name):
    n = len(bufs)

    def body(*refs):
        ins = refs[:n]
        send_ref, recv_ref = refs[n], refs[n + 1]
        x, y, c, _ = _place()
        for t in range(n):
            cp = _remote(_my_rows(ins[t], c), _my_rows(ins[t], 1 - c), send_ref.at[t], recv_ref.at[t], (x, y, 1 - c))
            cp.wait_send()
            cp.wait_recv()

    res = pl.pallas_call(
        body, name=name, out_shape=tuple(pltpu.HBM(b.shape, b.dtype) for b in bufs),
        in_specs=[HBM_ONLY] * n + [SEM_SPEC, SEM_SPEC] + [ANY_SPEC] * len(after), out_specs=[HBM_ONLY] * n,
        input_output_aliases={i: i for i in range(n)},
        compiler_params=pltpu.CompilerParams(has_side_effects=DATAFLOW),
    )(*bufs, send, recv, *after)
    return list(res)


WEIGHT_NAMES = ("ln_attn", "w_in", "sink_b", "rpb_c", "mix_gain", "w_out", "ln_ffn", "w_up", "conv_w", "conv_b",
                "w_down", "ln_final")
BIG_NAMES = ("w_in", "w_out", "w_up", "w_down")
REPLICATED_NAMES = ("ln_attn", "sink_b", "rpb_c", "mix_gain", "ln_ffn", "conv_b", "ln_final")
PACK_TILE = 8 * LANES


def _pack(arrays, row_multiple):
    pieces = []
    for a in arrays:
        flat = a.reshape(-1)
        pieces.append(jnp.pad(flat, (0, (-flat.shape[0]) % PACK_TILE)))
    flat = jnp.concatenate(pieces)
    flat = jnp.pad(flat, (0, (-flat.shape[0]) % (row_multiple * LANES)))
    return flat.reshape(-1, LANES)


def _unpack(packed, shapes):
    flat = packed.reshape(-1)
    out, off = [], 0
    for shape in shapes:
        size = math.prod(shape)
        out.append(flat[off:off + size].reshape(shape))
        off += size + (-size) % PACK_TILE
    return out


def kernel(x, ln_attn, w_in, sink_b, rpb_c, mix_gain, w_out, ln_ffn, w_up, conv_w, conv_b, w_down, ln_final, loss_target, m_ln_attn, m_w_in, m_sink_b, m_rpb_c, m_mix_gain, m_w_out, m_ln_ffn, m_w_up, m_conv_w, m_conv_b, m_w_down, m_ln_final, v_ln_attn, v_w_in, v_sink_b, v_rpb_c, v_mix_gain, v_w_out, v_ln_ffn, v_w_up, v_conv_w, v_conv_b, v_w_down, v_ln_final):
    w = dict(ln_attn=ln_attn, w_in=w_in, sink_b=sink_b, rpb_c=rpb_c, mix_gain=mix_gain, w_out=w_out, ln_ffn=ln_ffn,
             w_up=w_up, conv_w=conv_w, conv_b=conv_b, w_down=w_down, ln_final=ln_final)
    m = dict(ln_attn=m_ln_attn, w_in=m_w_in, sink_b=m_sink_b, rpb_c=m_rpb_c, mix_gain=m_mix_gain, w_out=m_w_out,
             ln_ffn=m_ln_ffn, w_up=m_w_up, conv_w=m_conv_w, conv_b=m_conv_b, w_down=m_w_down, ln_final=m_ln_final)
    v = dict(ln_attn=v_ln_attn, w_in=v_w_in, sink_b=v_sink_b, rpb_c=v_rpb_c, mix_gain=v_mix_gain, w_out=v_w_out,
             ln_ffn=v_ln_ffn, w_up=v_w_up, conv_w=v_conv_w, conv_b=v_conv_b, w_down=v_w_down, ln_final=v_ln_final)
    shard = _shard_index(lax.axis_index("x"), lax.axis_index("y"))
    up_cols = w_up.shape[2]

    conv_send, conv_recv, conv_vec, conv_slots, conv_token = _small_start(_pack([conv_w], 8), [], "conv_w_start")

    arrivals = []
    group_of = {}
    tokens = []
    rest = ("w_out", "w_up", "w_down")
    for l, names in ((0, ("w_in",)), (0, rest), (1, ("w_in",)), (1, rest)):
        bufs = [_own_slot(w[k], l, shard, "own_" + k) for k in names]
        send, recv, bufs, token = _gather_start(bufs, tokens[-1:] or [conv_token], "gather_start_%d" % len(arrivals))
        tokens.append(token)
        for k in names:
            group_of[l, k] = len(arrivals)
        arrivals.append({"names": names, "send": send, "recv": recv, "bufs": bufs, "done": None})

    def gathered(l, name, after):
        idx = group_of[l, name]
        group = arrivals[idx]

        def whole(k, buf):
            return buf.reshape(1, -1, buf.shape[2]) if k in ("w_out", "w_down") else buf

        if group["done"] is None:
            follow = list(after) + tokens[-1:]
            if idx == 0:
                follow += [tabs[k] for k in ("cos", "sin", "bias_a", "bias_b")]
                follow += [p[k] for p in layers for k in ("bias_c", "conv_w")]
            bufs = _gather_wait(group["send"], group["recv"], group["bufs"], follow, "gather_wait_%d" % idx)
            first = _gather_forward(bufs[:1], "gather_forward_%d" % idx)[0]
            if len(bufs) > 1:
                send, recv, rest, first = _gather_forward_start(bufs[1:], first, "gather_forward_start_%d" % idx)
                group["rest"] = (send, recv, rest)
            group["done"] = {group["names"][0]: whole(group["names"][0], first)}
        if name not in group["done"]:
            send, recv, rest = group["rest"]
            rest = _gather_forward_wait(send, recv, rest, list(after), "gather_forward_wait_%d" % idx)
            group["done"].update({k: whole(k, buf) for k, buf in zip(group["names"][1:], rest)})
        return group["done"][name]

    conv_vec, conv_slots = _small_wait(conv_send, conv_recv, conv_vec, conv_slots, tokens[-1:], "conv_w_wait")
    device = 4 * lax.axis_index("x") + 2 * lax.axis_index("y") + lax.axis_index("c")
    conv_slots = lax.dynamic_update_index_in_dim(conv_slots, conv_vec, device, 0)
    conv_all = conv_slots[0::2].reshape(N_SHARDS, -1)[:, :conv_w.size].reshape((N_SHARDS,) + conv_w.shape)

    cos, sin = _rope_tables(SEQ)
    tabs = {"cos": cos, "sin": sin, "bias_a": _bias_a(), "bias_b": _bias_b()}
    layers = []
    for l in range(DEPTH):
        conv_w_l = conv_all[:, l].reshape(2, N_SHARDS // 2, 3, up_cols).transpose(0, 2, 1, 3).reshape(2, 3, D_FF)
        layers.append({"ln_attn": ln_attn[l][None], "sink_b": sink_b[l], "bias_c": _bias_c(rpb_c[l]),
                       "mix_gain": mix_gain[l][None], "ln_ffn": ln_ffn[l][None], "conv_w": conv_w_l,
                       "conv_b": conv_b[l].reshape(2, 1, D_FF)})

    act = x[0]
    saved = []
    for l in range(DEPTH):
        act, keep = _layer_fwd(act, layers[l], lambda name, after, l=l: gathered(l, name, [after]), tabs)
        saved.append(keep)
    loss_part, dx, dx_b, d_ln_final = _loss_head(act, ln_final[None], loss_target[0], "loss_head")
    loss = lax.psum(loss_part[0, 0], ("x", "y", "c"))

    reductions = []

    opened = [0]

    def begin(l, partial):
        idx = opened[0]
        opened[0] += 1
        names = tuple(partial)
        send_sem, recv_sem, mine, theirs, token = _half_exchange_start([partial[k] for k in names],
                                                                       "half_exchange_start_%d" % idx)
        return {"idx": idx, "layer": l, "names": names, "send": send_sem, "recv": recv_sem, "mine": mine,
                "theirs": theirs}, token

    def finish(handle, after):
        idx, names = handle["idx"], handle["names"]
        mine, theirs = _half_exchange_wait(handle["send"], handle["recv"], handle["mine"], handle["theirs"], after,
                                           "half_exchange_wait_%d" % idx)
        pairs = [_half_sum(a, b, "half_sum_" + k) for k, a, b in zip(names, mine, theirs)]
        send_sem, recv_sem, pairs, lands, token = _reduce_start(pairs, "reduce_start_%d" % idx)
        reductions.append({"layer": handle["layer"], "names": names, "send": send_sem, "recv": recv_sem,
                           "pairs": pairs, "lands": lands})
        return token

    small = [None] * DEPTH
    pending = None
    for l in reversed(range(DEPTH)):
        big = {k: gathered(l, k, []) for k in BIG_NAMES}
        dx, dx_b, small[l], pending = _layer_bwd(dx, dx_b, saved[l], layers[l], big, tabs,
                                                 functools.partial(begin, l), finish, pending)
    after = [finish(pending[0], [pending[1]])]

    stacked = {k: jnp.stack([small[l][k] for l in range(DEPTH)]) for k in small[0]}
    part = {"ln_attn": stacked["ln_attn"][:, 0], "sink_b": stacked["sink_b"], "rpb_c": stacked["rpb_c"],
            "mix_gain": stacked["mix_gain"][:, 0], "ln_ffn": stacked["ln_ffn"][:, 0],
            "conv_b": stacked["conv_b"].reshape(DEPTH, 2 * D_FF), "ln_final": d_ln_final[0],
            "conv_w": stacked["conv_w"].transpose(0, 2, 1, 3).reshape(DEPTH, 3, 2 * D_FF)}
    small_names = REPLICATED_NAMES + ("conv_w",)
    small_send, small_recv, small_vec, small_slots, token = _small_start(
        _pack([part[k] for k in small_names], 256), after, "small_grads_start")
    after = [token]

    grads, delta, new_m, new_v = {}, {}, {}, {}
    updated = dict.fromkeys(BIG_NAMES)

    def arrive(idx, after):
        group = reductions[idx]
        pairs, lands = _reduce_wait(group["send"], group["recv"], group["pairs"], group["lands"], after,
                                    "reduce_wait_%d" % idx)
        halves = [_reduce_sum(pair, landed, "reduce_sum_" + k) for k, pair, landed in zip(group["names"], pairs, lands)]
        send_sem, recv_sem, halves, token = _half_gather_start(halves, "half_gather_start_%d" % idx)
        return {"idx": idx, "send": send_sem, "recv": recv_sem, "bufs": halves, "names": group["names"],
                "layer": group["layer"]}, [token]

    def update(swap, after):
        whole = _half_gather_wait(swap["send"], swap["recv"], swap["bufs"], after,
                                  "half_gather_wait_%d" % swap["idx"])
        for k, g in zip(swap["names"], whole):
            updated[k] = _adamw_layer(w[k], g, m[k], v[k], swap["layer"], updated[k], "adamw_" + k)
        return [updated[k][0] for k in swap["names"]]

    swaps = []
    for idx in range(len(reductions) - 1):
        swap, after = arrive(idx, after)
        swaps.append(swap)
    for swap in swaps[:2]:
        after = update(swap, after)
    swap, after = arrive(len(reductions) - 1, after)
    for swap in swaps[2:] + [swap]:
        after = update(swap, after)
    for k in BIG_NAMES:
        grads[k], delta[k], new_m[k], new_v[k] = updated[k]

    small_vec, small_slots = _small_wait(small_send, small_recv, small_vec, small_slots, after, "small_grads_wait")
    total = _small_sum(small_vec, small_slots, "small_grads_sum")
    for k, g in zip(small_names, _unpack(total, [part[k].shape for k in small_names])):
        grads[k] = g
    grads["conv_w"] = lax.dynamic_slice_in_dim(grads["conv_w"], shard * up_cols, up_cols, axis=2)

    flat = (DEPTH * 3, up_cols)
    res = _adamw(conv_w.reshape(flat), grads["conv_w"].reshape(flat), m["conv_w"].reshape(flat),
                 v["conv_w"].reshape(flat), "adamw_conv_w")
    delta["conv_w"], new_m["conv_w"], new_v["conv_w"] = (r.reshape(conv_w.shape) for r in res)
    shapes = [w[k].shape for k in REPLICATED_NAMES]
    packed = [_pack([d[k] for k in REPLICATED_NAMES], 128) for d in (w, grads, m, v)]
    for d, res in zip((delta, new_m, new_v), _adamw(*packed, "adamw_small")):
        for k, r in zip(REPLICATED_NAMES, _unpack(res, shapes)):
            d[k] = r

    return (loss, dx[None], *[grads[k] for k in WEIGHT_NAMES], *[delta[k] for k in WEIGHT_NAMES],
            *[new_m[k] for k in WEIGHT_NAMES], *[new_v[k] for k in WEIGHT_NAMES])
```

```python
import functools
import math

import jax
import jax.numpy as jnp
from jax import lax
from jax.experimental import pallas as pl
from jax.experimental.pallas import tpu as pltpu

F32 = jnp.float32
BF16 = jnp.bfloat16
MESH = pl.DeviceIdType.MESH

D_MODEL = 2048
SEQ = 2048
DEPTH = 2
HEAD_DIM = 64
N_HEADS_A = 12
N_HEADS_B = 10
N_KV_B = 2
N_HEADS_C = 10
WINDOW_B = 128
GRID_W = 64
NA_ROWS = 8
NA_COLS = 16
WIDTH_A = N_HEADS_A * HEAD_DIM
WIDTH_B = N_HEADS_B * HEAD_DIM
WIDTH_C = N_HEADS_C * HEAD_DIM
IN_COLS = 5120
D_FF = 5632
ROPE_THETA = 10000.0
EPS = 1e-6
NEG_INF = -1e30
N_SHARDS = 4

ADAM_LR = 0.001
ADAM_B1 = 0.9
ADAM_B2 = 0.999
ADAM_EPS = 1e-08
ADAM_WD = 0.01
ADAM_STEP = 10

LANES = 128
QB = 256
NQB = SEQ // QB
ROWS = 256
MIB = 2 ** 20

A_BLK = (0, 6, 12)
B_BLK = (18, 23, 24)
C_BLK = (25, 30, 35)
ROPE_BLKS = tuple(range(0, 12)) + tuple(range(18, 24))
QSCALE_BLKS = tuple(range(0, 6)) + tuple(range(18, 23)) + tuple(range(25, 30))
N_PBLK = IN_COLS // LANES


def _params(sem, vmem_mib):
    return pltpu.CompilerParams(dimension_semantics=sem, vmem_limit_bytes=vmem_mib * MIB)


def _weight_spec(w, cols, t_in, t_out, transposed):
    s, r, c = w.shape
    if cols:
        per = c // t_out
        k_dim, n = r, s * c
        if transposed:
            index = lambda j, rr: (rr // per, j, rr % per)
        else:
            index = lambda j, kk: (j // per, kk, j % per)
    else:
        per = r // t_in
        k_dim, n = s * r, c
        if transposed:
            index = lambda j, rr: (j // per, j % per, rr)
        else:
            index = lambda j, kk: (kk // per, kk % per, j)
    return pl.BlockSpec((None, t_in, t_out), index), k_dim, n


def _mm_nn(a, w, *, cols, tn, tk, out_dtype, name, residual=None, out_split=1):
    m, k_dim = a.shape
    w_spec, k_w, n = _weight_spec(w, cols, tk, tn, False)
    assert k_w == k_dim
    nj, nk = n // tn, k_dim // tk
    in_specs = [pl.BlockSpec((m, tk), lambda j, k: (0, k)), w_spec]
    args = [a, w]
    if residual is not None:
        in_specs.append(pl.BlockSpec((m, tn), lambda j, k: (0, j)))
        args.append(residual)
    if out_split > 1:
        per_o = n // out_split // tn
        out_spec = pl.BlockSpec((None, m, tn), lambda j, k: (j // per_o, 0, j % per_o))
        out_shape = pltpu.HBM((out_split, m, n // out_split), out_dtype)
    else:
        out_spec = pl.BlockSpec((m, tn), lambda j, k: (0, j))
        out_shape = pltpu.HBM((m, n), out_dtype)

    def body(*refs):
        a_ref, w_ref = refs[0], refs[1]
        r_ref = refs[2] if residual is not None else None
        o_ref = refs[3] if residual is not None else refs[2]

        def finish(val):
            if r_ref is not None:
                val = r_ref[...] + val
            o_ref[...] = val.astype(o_ref.dtype)

        part = jnp.dot(a_ref[...], w_ref[...], preferred_element_type=F32)
        if nk == 1:
            finish(part)
        else:
            acc = refs[-1]
            kk = pl.program_id(1)

            @pl.when(kk == 0)
            def _():
                acc[...] = part

            @pl.when(kk > 0)
            def _():
                acc[...] += part

            @pl.when(kk == nk - 1)
            def _():
                finish(acc[...])

    return pl.pallas_call(
        body, name=name, grid=(nj, nk), in_specs=in_specs, out_specs=out_spec, out_shape=out_shape,
        scratch_shapes=[pltpu.VMEM((m, tn), F32)] if nk > 1 else [],
        compiler_params=_params(("arbitrary", "arbitrary"), 56),
    )(*[_in_hbm(a) for a in args])


ANY_SPEC = pl.BlockSpec(memory_space=pl.ANY)


def _mm_nt(dy, w, *, cols, to, tr, out_dtype, name, after=()):
    if dy.ndim == 3:
        m = dy.shape[1]
        n = dy.shape[0] * dy.shape[2]
        per_d = dy.shape[2] // tr
        dy_spec = pl.BlockSpec((None, m, tr), lambda j, r: (r // per_d, 0, r % per_d))
    else:
        m, n = dy.shape
        dy_spec = pl.BlockSpec((m, tr), lambda j, r: (0, r))
    w_spec, k_dim, n_w = _weight_spec(w, cols, to, tr, True)
    assert n_w == n
    nj, nr = k_dim // to, n // tr

    n_after = len(after)

    def body(dy_ref, w_ref, *rest):
        o_ref = rest[n_after]
        part = lax.dot_general(dy_ref[...], w_ref[...], (((1,), (1,)), ((), ())), preferred_element_type=F32)
        if nr == 1:
            o_ref[...] = part.astype(o_ref.dtype)
        else:
            acc = rest[n_after + 1]
            rr = pl.program_id(1)

            @pl.when(rr == 0)
            def _():
                acc[...] = part

            @pl.when(rr > 0)
            def _():
                acc[...] += part

            @pl.when(rr == nr - 1)
            def _():
                o_ref[...] = acc[...].astype(o_ref.dtype)

    return pl.pallas_call(
        body, name=name, grid=(nj, nr), in_specs=[dy_spec, w_spec] + [ANY_SPEC] * n_after,
        out_specs=pl.BlockSpec((m, to), lambda j, r: (0, j)),
        out_shape=pltpu.HBM((m, k_dim), out_dtype),
        scratch_shapes=[pltpu.VMEM((m, to), F32)] if nr > 1 else [],
        compiler_params=_params(("arbitrary", "arbitrary"), 56),
    )(_in_hbm(dy), _in_hbm(w), *after)


def _mm_tn(x, dy, *, tk, tn, shards, name):
    m, k_dim = x.shape
    if dy.ndim == 3:
        n = dy.shape[0] * dy.shape[2]
        per_d = dy.shape[2] // tn
        dy_spec = pl.BlockSpec((None, m, tn), lambda i, j: (j // per_d, 0, j % per_d))
    else:
        n = dy.shape[1]
        dy_spec = pl.BlockSpec((m, tn), lambda i, j: (0, j))
    if shards > 0:
        per = n // shards // tn
        out_shape = pltpu.HBM((shards, k_dim, n // shards), BF16)
        out_spec = pl.BlockSpec((None, tk, tn), lambda i, j: (j // per, i, j % per))
    else:
        s = -shards
        per = k_dim // s // tk
        out_shape = pltpu.HBM((s, k_dim // s, n), BF16)
        out_spec = pl.BlockSpec((None, tk, tn), lambda i, j: (i // per, i % per, j))

    def body(x_ref, dy_ref, o_ref):
        o_ref[...] = lax.dot_general(x_ref[...], dy_ref[...], (((0,), (0,)), ((), ())),
                                     preferred_element_type=F32).astype(BF16)

    return pl.pallas_call(
        body, name=name, grid=(k_dim // tk, n // tn),
        in_specs=[pl.BlockSpec((m, tk), lambda i, j: (0, i)), dy_spec], out_specs=out_spec, out_shape=out_shape,
        compiler_params=_params(("arbitrary", "arbitrary"), 56),
    )(_in_hbm(x), _in_hbm(dy))


def _row_spec(width, rows=ROWS):
    return pl.BlockSpec((rows, width), lambda i: (i, 0))


def _vec_spec(width):
    return pl.BlockSpec((1, width), lambda i: (0, 0))


def _rms_stats(x):
    r = lax.rsqrt(jnp.mean(x * x, axis=-1, keepdims=True) + EPS)
    return r, x * r


def _rmsnorm_fwd(x, gain, name):
    t, d = x.shape

    def body(x_ref, g_ref, o_ref):
        _, n = _rms_stats(x_ref[...])
        o_ref[...] = (n * g_ref[...]).astype(BF16)

    return pl.pallas_call(
        body, name=name, grid=(t // ROWS,), in_specs=[_row_spec(d), _vec_spec(d)], out_specs=_row_spec(d),
        out_shape=pltpu.HBM((t, d), BF16), compiler_params=_params(("arbitrary",), 32),
    )(_in_hbm(x), _in_hbm(gain))


def _rmsnorm_bwd(x, gain, dh, dres, name, after=()):
    t, d = x.shape
    n_after = len(after)

    def body(x_ref, g_ref, dh_ref, dres_ref, *rest):
        dx_ref, dxb_ref, dg_ref = rest[n_after:]
        r, n = _rms_stats(x_ref[...])
        dh_v = dh_ref[...]
        dn = dh_v * g_ref[...]
        dx = dres_ref[...] + r * (dn - n * jnp.mean(dn * n, axis=-1, keepdims=True))
        dx_ref[...] = dx
        dxb_ref[...] = dx.astype(BF16)
        part = jnp.sum(dh_v * n, axis=0, keepdims=True)

        @pl.when(pl.program_id(0) == 0)
        def _():
            dg_ref[...] = part

        @pl.when(pl.program_id(0) > 0)
        def _():
            dg_ref[...] += part

    return pl.pallas_call(
        body, name=name, grid=(t // ROWS,),
        in_specs=[_row_spec(d), _vec_spec(d), _row_spec(d), _row_spec(d)] + [ANY_SPEC] * n_after,
        out_specs=[_row_spec(d), _row_spec(d), _vec_spec(d)],
        out_shape=[pltpu.HBM((t, d), F32), pltpu.HBM((t, d), BF16), jax.ShapeDtypeStruct((1, d), F32)],
        compiler_params=_params(("arbitrary",), 40),
    )(_in_hbm(x), _in_hbm(gain), _in_hbm(dh), _in_hbm(dres), *after)


def _loss_head(x, gain, target, name):
    t, d = x.shape

    def body(x_ref, g_ref, t_ref, loss_ref, dx_ref, dxb_ref, dg_ref):
        r, n = _rms_stats(x_ref[...])
        g = g_ref[...]
        err = n * g - t_ref[...]
        dy = err * (1.0 / d)
        dn = dy * g
        dx = r * (dn - n * jnp.mean(dn * n, axis=-1, keepdims=True))
        dx_ref[...] = dx
        dxb_ref[...] = dx.astype(BF16)
        part = jnp.sum(dy * n, axis=0, keepdims=True)
        lpart = jnp.zeros((8, LANES), F32) + 0.5 * jnp.sum(jnp.mean(err * err, axis=-1, keepdims=True))

        @pl.when(pl.program_id(0) == 0)
        def _():
            dg_ref[...] = part
            loss_ref[...] = lpart

        @pl.when(pl.program_id(0) > 0)
        def _():
            dg_ref[...] += part
            loss_ref[...] += lpart

    return pl.pallas_call(
        body, name=name, grid=(t // ROWS,),
        in_specs=[_row_spec(d), _vec_spec(d), _row_spec(d)],
        out_specs=[pl.BlockSpec((8, LANES), lambda i: (0, 0)), _row_spec(d), _row_spec(d), _vec_spec(d)],
        out_shape=[jax.ShapeDtypeStruct((8, LANES), F32), pltpu.HBM((t, d), F32), pltpu.HBM((t, d), BF16),
                   jax.ShapeDtypeStruct((1, d), F32)],
        compiler_params=_params(("arbitrary",), 40),
    )(x, gain, target)


def _swap_halves(x):
    lane = lax.broadcasted_iota(jnp.int32, x.shape, 1)
    return jnp.where((lane % HEAD_DIM) < HEAD_DIM // 2, pltpu.roll(x, LANES - HEAD_DIM // 2, 1),
                     pltpu.roll(x, HEAD_DIM // 2, 1))


def _rope_tables(t):
    inv_freq = ROPE_THETA ** (-jnp.arange(0, HEAD_DIM, 2, dtype=F32) / HEAD_DIM)
    ang = jnp.arange(t, dtype=F32)[:, None] * inv_freq[None, :]
    cos = jnp.tile(jnp.cos(ang), (1, LANES // (HEAD_DIM // 2)))
    sin = jnp.tile(jnp.sin(ang), (1, LANES // (HEAD_DIM // 2)))
    lane = jnp.arange(LANES)[None, :]
    return cos, jnp.where((lane % HEAD_DIM) < HEAD_DIM // 2, -sin, sin)


def _rope_fwd(proj, cos, sin, name):
    t = proj.shape[0]
    scale = HEAD_DIM ** -0.5

    def body(p_ref, c_ref, s_ref, o_ref):
        cos_v, sin_v = c_ref[...], s_ref[...]
        for b in range(N_PBLK):
            cols = slice(b * LANES, (b + 1) * LANES)
            v = p_ref[:, cols]
            if b in ROPE_BLKS:
                v = v * cos_v + _swap_halves(v) * sin_v
            if b in QSCALE_BLKS:
                v = v * scale
            o_ref[:, cols] = v.astype(BF16)

    return pl.pallas_call(
        body, name=name, grid=(t // ROWS,),
        in_specs=[_row_spec(IN_COLS), _row_spec(LANES), _row_spec(LANES)], out_specs=_row_spec(IN_COLS),
        out_shape=pltpu.HBM((t, IN_COLS), BF16), compiler_params=_params(("arbitrary",), 40),
    )(_in_hbm(proj), _in_hbm(cos), _in_hbm(sin))


def _rope_bwd(grads, cos, sin, name):
    t = grads[0].shape[0]
    scale = HEAD_DIM ** -0.5
    group = N_HEADS_B // N_KV_B

    def body(*refs):
        c_ref, s_ref, o_ref = refs[9], refs[10], refs[11]
        cos_v, sin_v = c_ref[...], s_ref[...]

        def kv_sum(ref):
            parts = []
            for g in range(N_KV_B):
                acc = ref[:, g * group * HEAD_DIM:(g * group + 1) * HEAD_DIM]
                for h in range(g * group + 1, (g + 1) * group):
                    acc = acc + ref[:, h * HEAD_DIM:(h + 1) * HEAD_DIM]
                parts.append(acc)
            return jnp.concatenate(parts, axis=1)

        def emit(b, v):
            if b in ROPE_BLKS:
                v = v * cos_v - _swap_halves(v) * sin_v
            if b in QSCALE_BLKS:
                v = v * scale
            o_ref[:, b * LANES:(b + 1) * LANES] = v.astype(BF16)

        starts = (A_BLK[0], A_BLK[1], A_BLK[2], B_BLK[0], None, None, C_BLK[0], C_BLK[1], C_BLK[2])
        for idx, start in enumerate(starts):
            if start is None:
                continue
            for j in range(refs[idx].shape[1] // LANES):
                emit(start + j, refs[idx][:, j * LANES:(j + 1) * LANES])
        emit(B_BLK[1], kv_sum(refs[4]))
        emit(B_BLK[2], kv_sum(refs[5]))

    return pl.pallas_call(
        body, name=name, grid=(t // ROWS,),
        in_specs=[_row_spec(g.shape[1]) for g in grads] + [_row_spec(LANES), _row_spec(LANES)],
        out_specs=_row_spec(IN_COLS),
        out_shape=pltpu.HBM((t, IN_COLS), BF16), compiler_params=_params(("arbitrary",), 40),
    )(*[_in_hbm(g) for g in grads], _in_hbm(cos), _in_hbm(sin))


GROUP_COLS = ((0, WIDTH_A), (WIDTH_A, WIDTH_A + WIDTH_B), (WIDTH_A + WIDTH_B, D_MODEL))


def _mix_fwd(oa, ob, oc, gain, name):
    t = oa.shape[0]

    def body(a_ref, b_ref, c_ref, g_ref, o_ref):
        for ref, (lo, hi) in zip((a_ref, b_ref, c_ref), GROUP_COLS):
            _, n = _rms_stats(ref[...])
            o_ref[:, lo:hi] = (n * g_ref[:, lo:hi]).astype(BF16)

    return pl.pallas_call(
        body, name=name, grid=(t // ROWS,),
        in_specs=[_row_spec(WIDTH_A), _row_spec(WIDTH_B), _row_spec(WIDTH_C), _vec_spec(D_MODEL)],
        out_specs=_row_spec(D_MODEL),
        out_shape=pltpu.HBM((t, D_MODEL), BF16), compiler_params=_params(("arbitrary",), 32),
    )(_in_hbm(oa), _in_hbm(ob), _in_hbm(oc), _in_hbm(gain))


def _mix_bwd(oa, ob, oc, gain, dmixed, name, after=()):
    t = oa.shape[0]
    n_after = len(after)

    def body(a_ref, b_ref, c_ref, g_ref, dm_ref, *rest):
        da_ref, db_ref, dc_ref, dg_ref = rest[n_after:]
        first = pl.program_id(0) == 0
        for ref, dref, (lo, hi) in zip((a_ref, b_ref, c_ref), (da_ref, db_ref, dc_ref), GROUP_COLS):
            r, n = _rms_stats(ref[...])
            dm = dm_ref[:, lo:hi]
            dn = dm * g_ref[:, lo:hi]
            dref[...] = r * (dn - n * jnp.mean(dn * n, axis=-1, keepdims=True))
            part = jnp.sum(dm * n, axis=0, keepdims=True)

            @pl.when(first)
            def _():
                dg_ref[:, lo:hi] = part

            @pl.when(jnp.logical_not(first))
            def _():
                dg_ref[:, lo:hi] += part

    return pl.pallas_call(
        body, name=name, grid=(t // ROWS,),
        in_specs=[_row_spec(WIDTH_A), _row_spec(WIDTH_B), _row_spec(WIDTH_C), _vec_spec(D_MODEL), _row_spec(D_MODEL)]
        + [ANY_SPEC] * n_after,
        out_specs=[_row_spec(WIDTH_A), _row_spec(WIDTH_B), _row_spec(WIDTH_C), _vec_spec(D_MODEL)],
        out_shape=[pltpu.HBM((t, WIDTH_A), F32), pltpu.HBM((t, WIDTH_B), F32), pltpu.HBM((t, WIDTH_C), F32),
                   jax.ShapeDtypeStruct((1, D_MODEL), F32)],
        compiler_params=_params(("arbitrary",), 40),
    )(_in_hbm(oa), _in_hbm(ob), _in_hbm(oc), _in_hbm(gain), _in_hbm(dmixed), *after)


FF_COLS = 256


SUBLANES = 8
CHUNK_FWD = 256
CHUNK_BWD = 128
HALO = SUBLANES


def _ext_rows(ref, r0, chunk, where):
    t, cols = ref.shape
    zeros = jnp.zeros((HALO, cols), F32)
    if where == "first":
        return jnp.concatenate([zeros, ref[0:chunk + HALO, :]], axis=0)
    if where == "last":
        return jnp.concatenate([ref[t - chunk - HALO:t, :], zeros], axis=0)
    return ref[pl.ds(pl.multiple_of(r0 - HALO, HALO), chunk + 2 * HALO), :]


def _for_chunks(t, chunk, fn):
    fn(0, "first")

    def mid(ci, carry):
        fn(pl.multiple_of(ci * chunk, chunk), "mid")
        return carry

    lax.fori_loop(1, t // chunk - 1, mid, 0)
    fn(t - chunk, "last")


def _roll_rows(x, by):
    return pltpu.roll(x, by % x.shape[0], 0)


def _gate_val(u_ref, r0, chunk, where, w_ref, b_ref):
    ext = [_ext_rows(u_ref.at[h], r0, chunk, where) for h in range(2)]
    before = [_roll_rows(e, 1) for e in ext]
    after = [_roll_rows(e, -1) for e in ext]
    gate, val = ((before[h] * w_ref[h, 0:1, :] + ext[h] * w_ref[h, 1:2, :]) + after[h] * w_ref[h, 2:3, :] + b_ref[h]
                 for h in range(2))
    return gate, val, ext, before, after


def _ff_specs(t):
    u_spec = pl.BlockSpec((2, t, FF_COLS), lambda j: (0, 0, j))
    w_spec = pl.BlockSpec((2, 3, FF_COLS), lambda j: (0, 0, j))
    b_spec = pl.BlockSpec((2, 1, FF_COLS), lambda j: (0, 0, j))
    return u_spec, w_spec, b_spec


def _convgate_fwd(u0, conv_w, conv_b, name):
    t = u0.shape[1]
    u_spec, w_spec, b_spec = _ff_specs(t)

    def body(u_ref, w_ref, b_ref, o_ref):
        def chunk(r0, where):
            gate, val, _, _, _ = _gate_val(u_ref, r0, CHUNK_FWD, where, w_ref, b_ref)
            act = gate * jax.nn.sigmoid(gate) * val
            o_ref[pl.ds(r0, CHUNK_FWD), :] = act[HALO:HALO + CHUNK_FWD].astype(BF16)

        _for_chunks(t, CHUNK_FWD, chunk)

    return pl.pallas_call(
        body, name=name, grid=(D_FF // FF_COLS,), in_specs=[u_spec, w_spec, b_spec],
        out_specs=pl.BlockSpec((t, FF_COLS), lambda j: (0, j)),
        out_shape=pltpu.HBM((t, D_FF), BF16), compiler_params=_params(("arbitrary",), 48),
    )(_in_hbm(u0), conv_w, conv_b)


def _convgate_bwd(u0, conv_w, conv_b, d_act, name):
    t = u0.shape[1]
    u_spec, w_spec, b_spec = _ff_specs(t)

    def body(u_ref, w_ref, b_ref, da_ref, du_ref, dw_ref, db_ref, sums_ref):
        sums_ref[...] = jnp.zeros_like(sums_ref)
        inner = slice(HALO, HALO + CHUNK_BWD)

        def fold(x):
            return jnp.sum(x.reshape(CHUNK_BWD // SUBLANES, SUBLANES, x.shape[1]), axis=0)

        def chunk(r0, where):
            gate, val, ext, before, after = _gate_val(u_ref, r0, CHUNK_BWD, where, w_ref, b_ref)
            sig = jax.nn.sigmoid(gate)
            da = _ext_rows(da_ref, r0, CHUNK_BWD, where)
            d_half = (da * val * (sig * (1.0 + gate * (1.0 - sig))), da * (gate * sig))
            for h in range(2):
                du = d_half[h]
                for k, term in enumerate((du, du * before[h], du * ext[h], du * after[h])):
                    sums_ref[h, k] += fold(term[inner])
                du0 = (_roll_rows(du, -1) * w_ref[h, 0:1, :] + du * w_ref[h, 1:2, :]) + _roll_rows(du, 1) * w_ref[h, 2:3, :]
                du_ref[h, pl.ds(r0, CHUNK_BWD), :] = du0[inner].astype(BF16)

        _for_chunks(t, CHUNK_BWD, chunk)
        for h in range(2):
            db_ref[h] = jnp.sum(sums_ref[h, 0], axis=0, keepdims=True)
            for k in range(3):
                dw_ref[h, k:k + 1, :] = jnp.sum(sums_ref[h, k + 1], axis=0, keepdims=True)

    return pl.pallas_call(
        body, name=name, grid=(D_FF // FF_COLS,),
        in_specs=[u_spec, w_spec, b_spec, pl.BlockSpec((t, FF_COLS), lambda j: (0, j))],
        out_specs=[u_spec, w_spec, b_spec],
        out_shape=[pltpu.HBM((2, t, D_FF), BF16), jax.ShapeDtypeStruct((2, 3, D_FF), F32),
                   jax.ShapeDtypeStruct((2, 1, D_FF), F32)],
        scratch_shapes=[pltpu.VMEM((2, 4, SUBLANES, FF_COLS), F32)],
        compiler_params=_params(("arbitrary",), 56),
    )(_in_hbm(u0), conv_w, conv_b, _in_hbm(d_act))


class _Group:
    def __init__(self, heads, blks, kv_rows, n_win, gqa, bias_per_head):
        self.heads = heads
        self.pairs = heads // 2
        self.q_blk, self.k_blk, self.v_blk = blks
        self.kv_rows = kv_rows
        self.n_win = n_win
        self.full = kv_rows == SEQ
        self.gqa = gqa
        self.bias_per_head = bias_per_head
        self.width = heads * HEAD_DIM
        self.keys = kv_rows * n_win


GROUP_A = _Group(N_HEADS_A, A_BLK, SEQ, 1, False, False)
GROUP_B = _Group(N_HEADS_B, B_BLK, WINDOW_B, 4, True, False)
GROUP_C = _Group(N_HEADS_C, C_BLK, QB, 3, False, True)


def _win_start(grp, i):
    return jnp.clip(i * (QB // grp.kv_rows) - 1, 0, SEQ // grp.kv_rows - grp.n_win)


def _win_variant(i):
    return jnp.minimum(i, 1) + (i == NQB - 1).astype(jnp.int32)


def _attn_in_specs(grp, t):
    q_spec = pl.BlockSpec((QB, LANES), lambda p, i: (i, grp.q_blk + p))

    def col(blk):
        return (lambda p: blk) if grp.gqa else (lambda p: blk + p)

    def kv_specs(blk):
        c = col(blk)
        if grp.full:
            return [pl.BlockSpec((t, LANES), lambda p, i: (0, c(p)))]
        return [pl.BlockSpec((grp.kv_rows, LANES),
                             functools.partial(lambda p, i, w: (_win_start(grp, i) + w, c(p)), w=w))
                for w in range(grp.n_win)]

    nwk = grp.keys
    if grp.bias_per_head:
        bias_spec = pl.BlockSpec((2, None, QB, nwk), lambda p, i: (p, _win_variant(i), 0, 0))
    elif grp.full:
        bias_spec = pl.BlockSpec((1, None, QB, nwk), lambda p, i: (0, i, 0, 0))
    else:
        bias_spec = pl.BlockSpec((1, None, QB, nwk), lambda p, i: (0, _win_variant(i), 0, 0))
    sink_spec = pl.BlockSpec((1, LANES), lambda p, i: (0, p))
    return q_spec, kv_specs(grp.k_blk), kv_specs(grp.v_blk), bias_spec, sink_spec


def _head_kv(grp, whole, e, p):
    lo, hi = whole[:, :HEAD_DIM], whole[:, HEAD_DIM:]
    if grp.gqa:
        return jnp.where(2 * p + e >= N_HEADS_B // N_KV_B, hi, lo)
    return hi if e else lo


def _softmax_parts(q, k, bias, sink):
    s = lax.dot_general(q, k, (((1,), (1,)), ((), ())), preferred_element_type=F32) + bias
    m = jnp.maximum(jnp.max(s, axis=-1, keepdims=True), sink)
    pe = jnp.exp(s - m)
    denom = jnp.sum(pe, axis=-1, keepdims=True) + jnp.exp(sink - m)
    return pe, m, 1.0 / denom


def _attn_fwd(grp, proj, bias, sink, name):
    t = proj.shape[0]
    q_spec, k_specs, v_specs, bias_spec, sink_spec = _attn_in_specs(grp, t)
    nkv = len(k_specs)

    def body(*refs):
        q_ref = refs[0]
        k_refs, v_refs = refs[1:1 + nkv], refs[1 + nkv:1 + 2 * nkv]
        bias_ref, sink_ref, o_ref = refs[1 + 2 * nkv:4 + 2 * nkv]
        p = pl.program_id(0)
        k_all = jnp.concatenate([r[...] for r in k_refs], axis=0)
        v_all = jnp.concatenate([r[...] for r in v_refs], axis=0)
        outs = []
        for e in range(2):
            q = q_ref[:, e * HEAD_DIM:(e + 1) * HEAD_DIM]
            k = _head_kv(grp, k_all, e, p)
            v = _head_kv(grp, v_all, e, p)
            snk = sink_ref[0:1, e * HEAD_DIM:e * HEAD_DIM + 1]
            pe, _, inv = _softmax_parts(q, k, bias_ref[e if grp.bias_per_head else 0], snk)
            outs.append(jnp.dot(pe.astype(BF16), v, preferred_element_type=F32) * inv)
        o_ref[...] = jnp.concatenate(outs, axis=1)

    return pl.pallas_call(
        body, name=name, grid=(grp.pairs, NQB),
        in_specs=[q_spec, *k_specs, *v_specs, bias_spec, sink_spec],
        out_specs=pl.BlockSpec((QB, LANES), lambda p, i: (i, p)),
        out_shape=pltpu.HBM((t, grp.width), F32),
        compiler_params=_params(("arbitrary", "arbitrary"), 48),
    )(*([_in_hbm(proj)] * (1 + 2 * nkv)), _in_hbm(bias), sink)


def _attn_bwd(grp, proj, bias, sink, out, d_out, name):
    t = proj.shape[0]
    q_spec, k_specs, v_specs, bias_spec, sink_spec = _attn_in_specs(grp, t)
    nkv = len(k_specs)
    n_off = 2 * NA_ROWS - 1
    rows_q = QB // GRID_W
    wide = grp.keys > 2 * QB
    o_spec = pl.BlockSpec((QB, LANES), lambda p, i: (i, p))
    acc_spec = pl.BlockSpec((t, LANES), lambda p, i: (0, p))
    out_specs = [o_spec, acc_spec, acc_spec, pl.BlockSpec((None, 8, LANES), lambda p, i: (p, 0, 0))]
    out_shape = [pltpu.HBM((t, grp.width), F32)] * 3 + [jax.ShapeDtypeStruct((grp.pairs, 8, LANES), F32)]
    if grp.bias_per_head:
        out_specs.append(pl.BlockSpec((2, n_off, GRID_W, GRID_W), lambda p, i: (p, 0, 0, 0)))
        out_shape.append(jax.ShapeDtypeStruct((grp.heads, n_off, GRID_W, GRID_W), F32))

    def body(*refs):
        q_ref = refs[0]
        k_refs, v_refs = refs[1:1 + nkv], refs[1 + nkv:1 + 2 * nkv]
        bias_ref, sink_ref, o_ref, do_ref = refs[1 + 2 * nkv:5 + 2 * nkv]
        dq_ref, dk_ref, dv_ref, dsink_ref = refs[5 + 2 * nkv:9 + 2 * nkv]
        dbias_ref = refs[9 + 2 * nkv] if grp.bias_per_head else None
        p, i = pl.program_id(0), pl.program_id(1)

        @pl.when(i == 0)
        def _():
            dk_ref[...] = jnp.zeros_like(dk_ref)
            dv_ref[...] = jnp.zeros_like(dv_ref)
            dsink_ref[...] = jnp.zeros_like(dsink_ref)
            if dbias_ref is not None:
                dbias_ref[...] = jnp.zeros_like(dbias_ref)

        k_all = jnp.concatenate([r[...] for r in k_refs], axis=0)
        v_all = jnp.concatenate([r[...] for r in v_refs], axis=0)
        start = 0 if grp.full else _win_start(grp, i)
        dqs, dks, dvs, dsinks = [], [], [], []
        for e in range(2):
            cols = slice(e * HEAD_DIM, (e + 1) * HEAD_DIM)
            q = q_ref[:, cols]
            k = _head_kv(grp, k_all, e, p)
            v = _head_kv(grp, v_all, e, p)
            snk = sink_ref[0:1, e * HEAD_DIM:e * HEAD_DIM + 1]
            pe, m, inv = _softmax_parts(q, k, bias_ref[e if grp.bias_per_head else 0], snk)
            prob = pe * inv
            do = do_ref[:, cols]
            do_b = do.astype(BF16)
            pe_b = prob.astype(BF16)
            delta = jnp.sum(do * o_ref[:, cols], axis=-1, keepdims=True)
            dp = lax.dot_general(do_b, v, (((1,), (1,)), ((), ())), preferred_element_type=F32)
            ds = prob * (dp - delta)
            ds_b = ds.astype(BF16)
            dqs.append(jnp.dot(ds_b, k, preferred_element_type=F32))
            if wide:
                dks.append(lax.dot_general(q, ds_b, (((0,), (0,)), ((), ())), preferred_element_type=F32))
                dvs.append(lax.dot_general(do_b, pe_b, (((0,), (0,)), ((), ())), preferred_element_type=F32))
            else:
                dks.append(lax.dot_general(ds_b, q, (((0,), (0,)), ((), ())), preferred_element_type=F32))
                dvs.append(lax.dot_general(pe_b, do_b, (((0,), (0,)), ((), ())), preferred_element_type=F32))
            dsinks.append(-jnp.sum(jnp.exp(snk - m) * inv * delta, axis=0, keepdims=True))
            if dbias_ref is not None:
                shift = (i * QB - start * grp.kv_rows) // GRID_W
                for rq in range(rows_q):
                    for rk in range(grp.keys // GRID_W):
                        off = jnp.clip(rk - rq + (NA_ROWS - 1) - shift, 0, n_off - 1)
                        dbias_ref[e, off] += ds[rq * GRID_W:(rq + 1) * GRID_W, rk * GRID_W:(rk + 1) * GRID_W]
        dq_ref[...] = jnp.concatenate(dqs, axis=1)
        rows = pl.ds(0, t) if grp.full else pl.ds(pl.multiple_of(start * grp.kv_rows, grp.kv_rows), grp.keys)
        if wide:
            dk_ref[rows, :] += jnp.concatenate(dks, axis=0).T
            dv_ref[rows, :] += jnp.concatenate(dvs, axis=0).T
        else:
            dk_ref[rows, :] += jnp.concatenate(dks, axis=1)
            dv_ref[rows, :] += jnp.concatenate(dvs, axis=1)
        lane = lax.broadcasted_iota(jnp.int32, (8, LANES), 1)
        dsink_ref[...] += jnp.where(lane < HEAD_DIM, dsinks[0], dsinks[1])

    return pl.pallas_call(
        body, name=name, grid=(grp.pairs, NQB),
        in_specs=[q_spec, *k_specs, *v_specs, bias_spec, sink_spec, o_spec, o_spec],
        out_specs=out_specs, out_shape=out_shape,
        compiler_params=_params(("arbitrary", "arbitrary"), 56),
    )(*([_in_hbm(proj)] * (1 + 2 * nkv)), _in_hbm(bias), sink, _in_hbm(out), _in_hbm(d_out))


DILATED_CONFIGS = ((128, 1), (512, 4), (2048, 16))


def _bias_a():
    d = jnp.arange(SEQ)[None, :] - jnp.arange(SEQ)[:, None]
    mult = jnp.zeros((SEQ, SEQ), F32)
    for window, r in DILATED_CONFIGS:
        reach = (window // (2 * r)) * r
        mult = mult + ((d % r == 0) & (jnp.abs(d) <= reach)).astype(F32)
    return jnp.where(mult > 0, jnp.log(jnp.maximum(mult, 1.0)), NEG_INF).reshape(1, NQB, QB, SEQ)


def _bias_b():
    row = jnp.arange(QB)[None, :, None]
    col = jnp.arange(GROUP_B.keys)[None, None, :]
    var = jnp.arange(3)[:, None, None]
    d = col - (GROUP_B.kv_rows * var + row)
    return jnp.where(jnp.abs(d) <= WINDOW_B, 0.0, NEG_INF).astype(F32)[None]


def _offset_onehot():
    c = jnp.arange(GRID_W)[:, None, None]
    c2 = jnp.arange(GRID_W)[None, :, None]
    b = jnp.arange(LANES)[None, None, :]
    return (c2 - c + NA_COLS - 1 == b).astype(BF16).reshape(GRID_W * GRID_W, LANES)


def _split_dot(x, g):
    hi = x.astype(BF16)
    rest = x - hi.astype(F32)
    mid = rest.astype(BF16)
    lo = (rest - mid.astype(F32)).astype(BF16)
    return (jnp.dot(hi, g, preferred_element_type=F32) + jnp.dot(mid, g, preferred_element_type=F32)
            + jnp.dot(lo, g, preferred_element_type=F32))


def _table_mm(x, g, name):
    def body(x_ref, g_ref, o_ref):
        o_ref[...] = _split_dot(x_ref[...], g_ref[...])

    return pl.pallas_call(
        body, name=name, out_shape=jax.ShapeDtypeStruct((x.shape[0], g.shape[1]), F32),
        in_specs=[pl.BlockSpec(memory_space=pltpu.VMEM)] * 2, out_specs=pl.BlockSpec(memory_space=pltpu.VMEM),
        compiler_params=pltpu.CompilerParams(vmem_limit_bytes=32 * MIB),
    )(x, g)


N_OFF = 2 * NA_ROWS - 1
TABLE_ROWS = 152


def _bias_c(rpb):
    table = jnp.zeros((TABLE_ROWS, LANES), F32).at[:N_HEADS_C * N_OFF, :2 * NA_COLS - 1].set(
        rpb.reshape(N_HEADS_C * N_OFF, 2 * NA_COLS - 1))
    tiles = _table_mm(table, _offset_onehot().T, "rpb_tiles")[:N_HEADS_C * N_OFF]
    tiles = tiles.reshape(N_HEADS_C, N_OFF, GRID_W, GRID_W)
    c = jnp.arange(GRID_W)
    col_start = jnp.clip(c - NA_COLS // 2, 0, GRID_W - NA_COLS)
    col_ok = (c[None, :] >= col_start[:, None]) & (c[None, :] < col_start[:, None] + NA_COLS)
    tiles = jnp.where(col_ok, tiles, NEG_INF)
    rows_q = QB // GRID_W
    rows_k = GROUP_C.keys // GRID_W

    def body(t_ref, o_ref):
        for var in range(3):
            for rq in range(rows_q):
                r_l = rows_q * var + rq
                first = min(max(r_l - NA_ROWS // 2, 0), rows_k - NA_ROWS)
                for rk in range(rows_k):
                    if first <= rk < first + NA_ROWS:
                        tile = t_ref[rk - r_l + NA_ROWS - 1]
                    else:
                        tile = jnp.full((GRID_W, GRID_W), NEG_INF, F32)
                    o_ref[var, rq * GRID_W:(rq + 1) * GRID_W, rk * GRID_W:(rk + 1) * GRID_W] = tile

    return pl.pallas_call(
        body, name="bias_c", grid=(N_HEADS_C,),
        in_specs=[pl.BlockSpec((None, N_OFF, GRID_W, GRID_W), lambda h: (h, 0, 0, 0))],
        out_specs=pl.BlockSpec((None, 3, QB, GROUP_C.keys), lambda h: (h, 0, 0, 0)),
        out_shape=jax.ShapeDtypeStruct((N_HEADS_C, 3, QB, GROUP_C.keys), F32),
        compiler_params=_params(("arbitrary",), 32),
    )(tiles)


def _rpb_grad(d_tiles):
    flat = jnp.zeros((TABLE_ROWS, GRID_W * GRID_W), F32).at[:N_HEADS_C * N_OFF].set(
        d_tiles.reshape(N_HEADS_C * N_OFF, GRID_W * GRID_W))
    out = _table_mm(flat, _offset_onehot(), "rpb_grad")
    return out[:N_HEADS_C * N_OFF, :2 * NA_COLS - 1].reshape(N_HEADS_C, N_OFF, 2 * NA_COLS - 1)


def _sink_lanes(sink):
    return jnp.repeat(sink.astype(F32), HEAD_DIM)[None, :]


def _attention_fwd(proj_r, sink_b, bias_a, bias_b, bias_c):
    no_sink_a = jnp.full((1, WIDTH_A), NEG_INF, F32)
    no_sink_c = jnp.full((1, WIDTH_C), NEG_INF, F32)
    oa = _attn_fwd(GROUP_A, proj_r, bias_a, no_sink_a, "attn_a_fwd")
    ob = _attn_fwd(GROUP_B, proj_r, bias_b, _sink_lanes(sink_b), "attn_b_fwd")
    oc = _attn_fwd(GROUP_C, proj_r, bias_c, no_sink_c, "attn_c_fwd")
    return oa, ob, oc


def _attention_bwd(proj_r, sink_b, bias_a, bias_b, bias_c, outs, d_outs, cos, sin):
    no_sink_a = jnp.full((1, WIDTH_A), NEG_INF, F32)
    no_sink_c = jnp.full((1, WIDTH_C), NEG_INF, F32)
    dqa, dka, dva, _ = _attn_bwd(GROUP_A, proj_r, bias_a, no_sink_a, outs[0], d_outs[0], "attn_a_bwd")
    dqb, dkb, dvb, dsink = _attn_bwd(GROUP_B, proj_r, bias_b, _sink_lanes(sink_b), outs[1], d_outs[1], "attn_b_bwd")
    dqc, dkc, dvc, _, d_tiles = _attn_bwd(GROUP_C, proj_r, bias_c, no_sink_c, outs[2], d_outs[2], "attn_c_bwd")
    d_proj = _rope_bwd((dqa, dka, dva, dqb, dkb, dvb, dqc, dkc, dvc), cos, sin, "rope_bwd")
    d_sink = dsink[:, 0, :].reshape(GROUP_B.pairs, 2, HEAD_DIM)[:, :, 0].reshape(N_HEADS_B)
    return d_proj, d_sink, _rpb_grad(d_tiles)


def _adamw(w, g, m, v, name):
    r, c = w.shape
    rows = r
    for cand in (512, 256, 128, 64, 32, 16, 8):
        if r % cand == 0 and cand * c * 4 <= MIB:
            rows = cand
            break
    spec = pl.BlockSpec((rows, c), lambda i: (i, 0))

    def body(w_ref, g_ref, m_ref, v_ref, d_ref, mo_ref, vo_ref):
        d_ref[...], mo_ref[...], vo_ref[...] = _adamw_step(w_ref[...], g_ref[...], m_ref[...], v_ref[...])

    return pl.pallas_call(
        body, name=name, grid=(r // rows,), in_specs=[spec] * 4, out_specs=[spec] * 3,
        out_shape=[jax.ShapeDtypeStruct((r, c), F32)] * 3, compiler_params=_params(("arbitrary",), 32),
    )(w, g, m, v)


def _adamw_step(w, grad, m, v):
    m_new = ADAM_B1 * m + (1.0 - ADAM_B1) * grad
    v_new = ADAM_B2 * v + (1.0 - ADAM_B2) * jnp.square(grad)
    m_hat = m_new / (1.0 - ADAM_B1 ** ADAM_STEP)
    v_hat = v_new / (1.0 - ADAM_B2 ** ADAM_STEP)
    return -ADAM_LR * (m_hat / (jnp.sqrt(v_hat) + ADAM_EPS) + ADAM_WD * w), m_new, v_new


def _adamw_layer(w, g, m, v, layer, prev, name):
    _, r, c = w.shape
    rows = next(cand for cand in (512, 256, 128, 64, 32, 16, 8) if r % cand == 0 and cand * c * 4 <= 2 * MIB)
    spec = pl.BlockSpec((None, rows, c), lambda i: (layer, i, 0))
    g_spec = pl.BlockSpec((rows, c), lambda i: (i, 0))
    n_prev = 0 if prev is None else 4

    def body(w_ref, g_ref, m_ref, v_ref, *rest):
        go_ref, d_ref, mo_ref, vo_ref = rest[n_prev:]
        grad = g_ref[...]
        go_ref[...] = grad
        d_ref[...], mo_ref[...], vo_ref[...] = _adamw_step(w_ref[...], grad, m_ref[...], v_ref[...])

    return pl.pallas_call(
        body, name=name, grid=(r // rows,), in_specs=[spec, g_spec, spec, spec] + [ANY_SPEC] * n_prev,
        out_specs=[spec] * 4,
        out_shape=[jax.ShapeDtypeStruct(w.shape, F32)] * 4,
        input_output_aliases={4 + i: i for i in range(n_prev)}, compiler_params=_params(("arbitrary",), 48),
    )(w, g, m, v, *(prev or ()))


def _layer_fwd(x0, p, weight, tabs):
    h1 = _rmsnorm_fwd(x0, p["ln_attn"], "ln_attn_fwd")
    proj = _mm_nn(h1, weight("w_in", h1), cols=True, tn=256, tk=D_MODEL, out_dtype=F32, name="mm_in")
    proj_r = _rope_fwd(proj, tabs["cos"], tabs["sin"], "rope_fwd")
    outs = _attention_fwd(proj_r, p["sink_b"], tabs["bias_a"], tabs["bias_b"], p["bias_c"])
    mixed = _mix_fwd(*outs, p["mix_gain"], "mix_fwd")
    x1 = _mm_nn(mixed, weight("w_out", mixed), cols=False, tn=256, tk=D_MODEL, out_dtype=F32, name="mm_out",
                residual=x0)
    h2 = _rmsnorm_fwd(x1, p["ln_ffn"], "ln_ffn_fwd")
    u0 = _mm_nn(h2, weight("w_up", h2), cols=True, tn=256, tk=D_MODEL, out_dtype=F32, name="mm_up", out_split=2)
    act = _convgate_fwd(u0, p["conv_w"], p["conv_b"], "convgate_fwd")
    x2 = _mm_nn(act, weight("w_down", act), cols=False, tn=512, tk=D_FF // 2, out_dtype=F32, name="mm_down",
                residual=x1)
    return x2, (x0, h1, proj_r, outs, mixed, x1, h2, u0, act)


def _layer_bwd(dx2, dx2_b, saved, p, big, tabs, begin, finish, pending):
    x0, h1, proj_r, outs, mixed, x1, h2, u0, act = saved
    d_act = _mm_nt(dx2_b, big["w_down"], cols=False, to=512, tr=D_MODEL, out_dtype=F32, name="nt_down",
                   after=[pending[1]] if pending else [])
    g_down = _mm_tn(act, dx2_b, tk=D_FF // N_SHARDS, tn=D_MODEL, shards=-N_SHARDS, name="tn_down")
    du0, d_conv_w, d_conv_b = _convgate_bwd(u0, p["conv_w"], p["conv_b"], d_act, "convgate_bwd")
    token = [finish(pending[0], [du0])] if pending else []
    dh2 = _mm_nt(du0, big["w_up"], cols=True, to=1024, tr=D_FF // 4, out_dtype=F32, name="nt_up", after=token)
    g_up = _mm_tn(h2, du0, tk=1024, tn=D_FF // 4, shards=N_SHARDS, name="tn_up")
    first, token = begin({"w_down": g_down, "w_up": g_up})
    dx1, dx1_b, d_ln_ffn = _rmsnorm_bwd(x1, p["ln_ffn"], dh2, dx2, "ln_ffn_bwd", after=[token])
    d_mixed = _mm_nt(dx1_b, big["w_out"], cols=False, to=512, tr=D_MODEL, out_dtype=F32, name="nt_out")
    g_out = _mm_tn(mixed, dx1_b, tk=D_MODEL // N_SHARDS, tn=D_MODEL, shards=-N_SHARDS, name="tn_out")
    token = finish(first, [g_out])
    *d_outs, d_mix_gain = _mix_bwd(*outs, p["mix_gain"], d_mixed, "mix_bwd", after=[token])
    d_proj, d_sink, d_rpb = _attention_bwd(proj_r, p["sink_b"], tabs["bias_a"], tabs["bias_b"], p["bias_c"], outs,
                                           d_outs, tabs["cos"], tabs["sin"])
    dh1 = _mm_nt(d_proj, big["w_in"], cols=True, to=1024, tr=IN_COLS // N_SHARDS, out_dtype=F32, name="nt_in")
    g_in = _mm_tn(h1, d_proj, tk=1024, tn=IN_COLS // N_SHARDS, shards=N_SHARDS, name="tn_in")
    dx0, dx0_b, d_ln_attn = _rmsnorm_bwd(x0, p["ln_attn"], dh1, dx1, "ln_attn_bwd")
    small = {"ln_attn": d_ln_attn, "sink_b": d_sink, "rpb_c": d_rpb, "mix_gain": d_mix_gain, "ln_ffn": d_ln_ffn,
             "conv_w": d_conv_w, "conv_b": d_conv_b}
    return dx0, dx0_b, small, begin({"w_out": g_out, "w_in": g_in})


HBM_SPEC = pl.BlockSpec(memory_space=pl.ANY)


def _place():
    x, y, c = lax.axis_index("x"), lax.axis_index("y"), lax.axis_index("c")
    chips = ((1 - x, y), (x, 1 - y), (1 - x, 1 - y))
    return x, y, c, chips


def _shard_index(px, py):
    return 2 * px + py


def _remote(src, dst, send_sem, recv_sem, to):
    return pltpu.make_async_remote_copy(src_ref=src, dst_ref=dst, send_sem=send_sem, recv_sem=recv_sem,
                                        device_id=to, device_id_type=MESH)


def _own_slot(w, layer, shard, name):
    _, r, c_dim = w.shape
    rows = r
    for cand in (512, 256, 128):
        if r % cand == 0 and cand * c_dim * 4 <= 2 * MIB:
            rows = cand
            break

    def body(s_ref, w_ref, o_ref):
        o_ref[...] = w_ref[...].astype(BF16)

    return pl.pallas_call(
        body, name=name,
        grid_spec=pltpu.PrefetchScalarGridSpec(
            num_scalar_prefetch=1, grid=(r // rows,),
            in_specs=[pl.BlockSpec((None, rows, c_dim), lambda i, s: (layer, i, 0))],
            out_specs=pl.BlockSpec((None, rows, c_dim), lambda i, s: (s[0], i, 0))),
        out_shape=jax.ShapeDtypeStruct((N_SHARDS, r, c_dim), BF16),
        compiler_params=_params(("arbitrary",), 32),
    )(shard.astype(jnp.int32).reshape(1), w)


HBM_ONLY = pl.BlockSpec(memory_space=pltpu.HBM)
SEM_SPEC = pl.BlockSpec(memory_space=pltpu.SEMAPHORE)
DATAFLOW = pltpu.SideEffectType.DATAFLOW_SIDE_EFFECTING


def _in_hbm(a):
    return pltpu.with_memory_space_constraint(a, pltpu.HBM)


N_DEV = 8


def _peers(x, y, c):
    flips = [(fx, fy, fc) for fx in (0, 1) for fy in (0, 1) for fc in (0, 1)][1:]
    return [((1 - x) if fx else x, (1 - y) if fy else y, (1 - c) if fc else c) for fx, fy, fc in flips]


def _small_start(vec, after, name):
    n_after = len(after)

    def body(v_ref, slots_ref, *rest):
        send, recv = rest[n_after], rest[n_after + 1]
        token = rest[-1]
        x, y, c, _ = _place()
        me = 4 * x + 2 * y + c
        for k, peer in enumerate(_peers(x, y, c)):
            _remote(v_ref, slots_ref.at[me], send.at[k], recv.at[k], peer).start()
        token[...] = jnp.zeros_like(token)

    slots = jax.ShapeDtypeStruct((N_DEV,) + vec.shape, vec.dtype)
    res = pl.pallas_call(
        body, name=name,
        out_shape=(pltpu.SemaphoreType.DMA((N_DEV - 1,)), pltpu.SemaphoreType.DMA((N_DEV - 1,)),
                   pltpu.HBM(vec.shape, vec.dtype), pltpu.HBM(slots.shape, slots.dtype),
                   jax.ShapeDtypeStruct((8, LANES), F32)),
        in_specs=[HBM_ONLY, HBM_ONLY] + [ANY_SPEC] * n_after,
        out_specs=(SEM_SPEC, SEM_SPEC, HBM_ONLY, HBM_ONLY, pl.BlockSpec(memory_space=pltpu.VMEM)),
        input_output_aliases={0: 2, 1: 3},
        compiler_params=pltpu.CompilerParams(has_side_effects=DATAFLOW),
    )(_in_hbm(vec), _in_hbm(lax.empty(slots.shape, slots.dtype)), *after)
    return res


def _small_wait(send, recv, vec, slots, after, name):
    def body(v_ref, slots_ref, send_ref, recv_ref, *rest):
        x, y, c, _ = _place()
        for k, (px, py, pc) in enumerate(_peers(x, y, c)):
            cp = _remote(v_ref, slots_ref.at[4 * px + 2 * py + pc], send_ref.at[k], recv_ref.at[k], (px, py, pc))
            cp.wait_send()
            cp.wait_recv()

    return pl.pallas_call(
        body, name=name, out_shape=(pltpu.HBM(vec.shape, vec.dtype), pltpu.HBM(slots.shape, slots.dtype)),
        in_specs=[HBM_ONLY, HBM_ONLY, SEM_SPEC, SEM_SPEC] + [ANY_SPEC] * len(after), out_specs=[HBM_ONLY, HBM_ONLY],
        input_output_aliases={0: 0, 1: 1},
        compiler_params=pltpu.CompilerParams(has_side_effects=DATAFLOW),
    )(vec, slots, send, recv, *after)


def _small_sum(vec, slots, name):
    rows = vec.shape[0]
    blk = min(rows, 256)
    x, y, c = lax.axis_index("x"), lax.axis_index("y"), lax.axis_index("c")
    me = (4 * x + 2 * y + c).astype(jnp.int32).reshape(1)

    def slot_spec(k):
        return pl.BlockSpec((None, blk, LANES), lambda i, w: (jnp.where(w[0] == k, (k + 1) % N_DEV, k), i, 0))

    def body(w_ref, v_ref, *rest):
        o_ref = rest[-1]
        acc = None
        for k in range(N_DEV):
            term = jnp.where(w_ref[0] == k, v_ref[...], rest[k][...])
            acc = term if acc is None else acc + term
        o_ref[...] = acc

    return pl.pallas_call(
        body, name=name,
        grid_spec=pltpu.PrefetchScalarGridSpec(
            num_scalar_prefetch=1, grid=(rows // blk,),
            in_specs=[pl.BlockSpec((blk, LANES), lambda i, w: (i, 0))] + [slot_spec(k) for k in range(N_DEV)],
            out_specs=pl.BlockSpec((blk, LANES), lambda i, w: (i, 0))),
        out_shape=jax.ShapeDtypeStruct(vec.shape, F32), compiler_params=_params(("arbitrary",), 32),
    )(me, vec, *([slots] * N_DEV))


def _half(ref, slot, c):
    half = ref.shape[1] // 2
    return ref.at[slot, pl.ds(pl.multiple_of(c * half, 8), half)]


def _gather_start(bufs, after, name):
    n = len(bufs)
    n_after = len(after)

    def body(*refs):
        ins = refs[:n]
        send, recv = refs[n + n_after], refs[n + n_after + 1]
        token = refs[-1]
        x, y, c, chips = _place()
        me = _shard_index(x, y)
        for t in range(n):
            for j, (px, py) in enumerate(chips):
                mine = _half(ins[t], me, c)
                _remote(mine, mine, send.at[t * 3 + j], recv.at[t * 3 + j], (px, py, c)).start()
        token[...] = jnp.zeros_like(token)

    thru = [pltpu.HBM(b.shape, b.dtype) for b in bufs]
    res = pl.pallas_call(
        body, name=name,
        out_shape=(pltpu.SemaphoreType.DMA((n * 3,)), pltpu.SemaphoreType.DMA((n * 3,)), *thru,
                   jax.ShapeDtypeStruct((8, LANES), F32)),
        in_specs=[HBM_ONLY] * n + [ANY_SPEC] * n_after,
        out_specs=(SEM_SPEC, SEM_SPEC, *([HBM_ONLY] * n), pl.BlockSpec(memory_space=pltpu.VMEM)),
        input_output_aliases={i: 2 + i for i in range(n)},
        compiler_params=pltpu.CompilerParams(has_side_effects=DATAFLOW),
    )(*[_in_hbm(b) for b in bufs], *after)
    return res[0], res[1], list(res[2:2 + n]), res[-1]


def _gather_wait(send, recv, bufs, after, name):
    n = len(bufs)

    def body(*refs):
        ins = refs[:n]
        send_ref, recv_ref = refs[n], refs[n + 1]
        x, y, c, chips = _place()
        me = _shard_index(x, y)
        for t in range(n):
            for j, (px, py) in enumerate(chips):
                cp = _remote(_half(ins[t], me, c), _half(ins[t], _shard_index(px, py), c), send_ref.at[t * 3 + j],
                             recv_ref.at[t * 3 + j], (px, py, c))
                cp.wait_send()
                cp.wait_recv()

    res = pl.pallas_call(
        body, name=name, out_shape=tuple(pltpu.HBM(b.shape, b.dtype) for b in bufs),
        in_specs=[HBM_ONLY] * n + [SEM_SPEC, SEM_SPEC] + [ANY_SPEC] * len(after), out_specs=[HBM_ONLY] * n,
        input_output_aliases={i: i for i in range(n)},
        compiler_params=pltpu.CompilerParams(has_side_effects=DATAFLOW),
    )(*bufs, send, recv, *after)
    return list(res)


def _gather_forward(bufs, name):
    n = len(bufs)

    def body(*refs):
        outs = refs[n:2 * n]
        send, recv = refs[2 * n:]
        x, y, c, chips = _place()
        sibling = (x, y, 1 - c)
        cps = []
        for t in range(n):
            for j, (px, py) in enumerate(chips):
                got = _half(outs[t], _shard_index(px, py), c)
                cp = _remote(got, got, send.at[t * 3 + j], recv.at[t * 3 + j], sibling)
                cp.start()
                cps.append(cp)
        for t in range(n):
            for j, (px, py) in enumerate(chips):
                theirs = _half(outs[t], _shard_index(px, py), 1 - c)
                _remote(theirs, theirs, send.at[t * 3 + j], recv.at[t * 3 + j], sibling).wait_recv()
        for cp in cps:
            cp.wait_send()

    return pl.pallas_call(
        body, name=name, in_specs=[HBM_SPEC] * n, out_specs=[HBM_SPEC] * n,
        out_shape=[jax.ShapeDtypeStruct(b.shape, b.dtype) for b in bufs],
        input_output_aliases={t: t for t in range(n)},
        scratch_shapes=[pltpu.SemaphoreType.DMA((n * 3,))] * 2,
    )(*bufs)


def _gather_forward_start(bufs, carry, name):
    n = len(bufs)

    def body(*refs):
        ins = refs[:n]
        send, recv = refs[n + 1], refs[n + 2]
        x, y, c, chips = _place()
        for t in range(n):
            for j, (px, py) in enumerate(chips):
                got = _half(ins[t], _shard_index(px, py), c)
                _remote(got, got, send.at[t * 3 + j], recv.at[t * 3 + j], (x, y, 1 - c)).start()

    res = pl.pallas_call(
        body, name=name,
        out_shape=(pltpu.SemaphoreType.DMA((n * 3,)), pltpu.SemaphoreType.DMA((n * 3,)),
                   *[pltpu.HBM(b.shape, b.dtype) for b in bufs], pltpu.HBM(carry.shape, carry.dtype)),
        in_specs=[HBM_ONLY] * (n + 1),
        out_specs=(SEM_SPEC, SEM_SPEC, *([HBM_ONLY] * (n + 1))),
        input_output_aliases={i: 2 + i for i in range(n + 1)},
        compiler_params=pltpu.CompilerParams(has_side_effects=DATAFLOW),
    )(*[_in_hbm(b) for b in bufs], _in_hbm(carry))
    return res[0], res[1], list(res[2:2 + n]), res[-1]


def _gather_forward_wait(send, recv, bufs, after, name):
    n = len(bufs)

    def body(*refs):
        ins = refs[:n]
        send_ref, recv_ref = refs[n], refs[n + 1]
        x, y, c, chips = _place()
        for t in range(n):
            for j, (px, py) in enumerate(chips):
                s = _shard_index(px, py)
                cp = _remote(_half(ins[t], s, c), _half(ins[t], s, 1 - c), send_ref.at[t * 3 + j],
                             recv_ref.at[t * 3 + j], (x, y, 1 - c))
                cp.wait_send()
                cp.wait_recv()

    res = pl.pallas_call(
        body, name=name, out_shape=tuple(pltpu.HBM(b.shape, b.dtype) for b in bufs),
        in_specs=[HBM_ONLY] * n + [SEM_SPEC, SEM_SPEC] + [ANY_SPEC] * len(after), out_specs=[HBM_ONLY] * n,
        input_output_aliases={i: i for i in range(n)},
        compiler_params=pltpu.CompilerParams(has_side_effects=DATAFLOW),
    )(*bufs, send, recv, *after)
    return list(res)


def _sibling_rows(ref, c):
    half = ref.shape[1] // 2
    return ref.at[:, pl.ds(pl.multiple_of((1 - c) * half, 8), half)]


def _half_exchange_start(grads, name):
    n = len(grads)

    def body(*refs):
        ins, lands = refs[:n], refs[n:2 * n]
        send, recv = refs[2 * n], refs[2 * n + 1]
        token = refs[-1]
        x, y, c, _ = _place()
        for t in range(n):
            _remote(_sibling_rows(ins[t], c), lands[t], send.at[t], recv.at[t], (x, y, 1 - c)).start()
        token[...] = jnp.zeros_like(token)

    halves = [jax.ShapeDtypeStruct((g.shape[0], g.shape[1] // 2, g.shape[2]), g.dtype) for g in grads]
    res = pl.pallas_call(
        body, name=name,
        out_shape=(pltpu.SemaphoreType.DMA((n,)), pltpu.SemaphoreType.DMA((n,)),
                   *[pltpu.HBM(g.shape, g.dtype) for g in grads], *[pltpu.HBM(h.shape, h.dtype) for h in halves],
                   jax.ShapeDtypeStruct((8, LANES), F32)),
        in_specs=[HBM_ONLY] * (2 * n),
        out_specs=(SEM_SPEC, SEM_SPEC, *([HBM_ONLY] * (2 * n)), pl.BlockSpec(memory_space=pltpu.VMEM)),
        input_output_aliases={i: 2 + i for i in range(2 * n)},
        compiler_params=pltpu.CompilerParams(has_side_effects=DATAFLOW),
    )(*[_in_hbm(g) for g in grads], *[_in_hbm(lax.empty(h.shape, h.dtype)) for h in halves])
    return res[0], res[1], list(res[2:2 + n]), list(res[2 + n:2 + 2 * n]), res[-1]


def _half_exchange_wait(send, recv, grads, lands, after, name):
    n = len(grads)

    def body(*refs):
        ins, got = refs[:n], refs[n:2 * n]
        send_ref, recv_ref = refs[2 * n], refs[2 * n + 1]
        x, y, c, _ = _place()
        for t in range(n):
            cp = _remote(_sibling_rows(ins[t], c), got[t], send_ref.at[t], recv_ref.at[t], (x, y, 1 - c))
            cp.wait_send()
            cp.wait_recv()

    res = pl.pallas_call(
        body, name=name,
        out_shape=(*[pltpu.HBM(g.shape, g.dtype) for g in grads], *[pltpu.HBM(h.shape, h.dtype) for h in lands]),
        in_specs=[HBM_ONLY] * (2 * n) + [SEM_SPEC, SEM_SPEC] + [ANY_SPEC] * len(after),
        out_specs=[HBM_ONLY] * (2 * n),
        input_output_aliases={i: i for i in range(2 * n)},
        compiler_params=pltpu.CompilerParams(has_side_effects=DATAFLOW),
    )(*grads, *lands, send, recv, *after)
    return list(res[:n]), list(res[n:])


def _half_rows(half, c_dim):
    for cand in (half, 512, 256, 128, 64):
        if half % cand == 0 and cand * c_dim * 2 <= 3 * MIB:
            return cand
    raise ValueError((half, c_dim))


def _core_index():
    return lax.axis_index("c").astype(jnp.int32).reshape(1)


def _half_sum(own, other, name):
    s, r, c_dim = own.shape
    rows = _half_rows(r // 2, c_dim)
    per = r // 2 // rows

    def body(c_ref, a_ref, b_ref, o_ref):
        o_ref[...] = (a_ref[...].astype(F32) + b_ref[...].astype(F32)).astype(BF16)

    return pl.pallas_call(
        body, name=name,
        grid_spec=pltpu.PrefetchScalarGridSpec(
            num_scalar_prefetch=1, grid=(s, per),
            in_specs=[pl.BlockSpec((None, rows, c_dim), lambda k, i, c: (k, c[0] * per + i, 0)),
                      pl.BlockSpec((None, rows, c_dim), lambda k, i, c: (k, i, 0))],
            out_specs=pl.BlockSpec((None, rows, c_dim), lambda k, i, c: (k, i, 0))),
        out_shape=pltpu.HBM((s, r // 2, c_dim), BF16), compiler_params=_params(("arbitrary", "arbitrary"), 32),
    )(_core_index(), own, other)


def _reduce_start(pairs, name):
    n = len(pairs)

    def body(*refs):
        ins, lands = refs[:n], refs[n:2 * n]
        send, recv = refs[2 * n], refs[2 * n + 1]
        token = refs[-1]
        x, y, c, chips = _place()
        me = _shard_index(x, y)
        for t in range(n):
            for j, (px, py) in enumerate(chips):
                _remote(ins[t].at[_shard_index(px, py)], lands[t].at[me], send.at[t * 3 + j], recv.at[t * 3 + j],
                        (px, py, c)).start()
        token[...] = jnp.zeros_like(token)

    thru = [pltpu.HBM(b.shape, b.dtype) for b in pairs]
    res = pl.pallas_call(
        body, name=name,
        out_shape=(pltpu.SemaphoreType.DMA((n * 3,)), pltpu.SemaphoreType.DMA((n * 3,)), *thru, *thru,
                   jax.ShapeDtypeStruct((8, LANES), F32)),
        in_specs=[HBM_ONLY] * (2 * n),
        out_specs=(SEM_SPEC, SEM_SPEC, *([HBM_ONLY] * (2 * n)), pl.BlockSpec(memory_space=pltpu.VMEM)),
        input_output_aliases={i: 2 + i for i in range(2 * n)},
        compiler_params=pltpu.CompilerParams(has_side_effects=DATAFLOW),
    )(*[_in_hbm(b) for b in pairs], *[_in_hbm(lax.empty(b.shape, b.dtype)) for b in pairs])
    return res[0], res[1], list(res[2:2 + n]), list(res[2 + n:2 + 2 * n]), res[-1]


def _reduce_wait(send, recv, pairs, lands, after, name):
    n = len(pairs)

    def body(*refs):
        ins, got = refs[:n], refs[n:2 * n]
        send_ref, recv_ref = refs[2 * n], refs[2 * n + 1]
        x, y, c, chips = _place()
        for t in range(n):
            for j, (px, py) in enumerate(chips):
                s = _shard_index(px, py)
                cp = _remote(ins[t].at[s], got[t].at[s], send_ref.at[t * 3 + j], recv_ref.at[t * 3 + j], (px, py, c))
                cp.wait_send()
                cp.wait_recv()

    thru = [pltpu.HBM(b.shape, b.dtype) for b in pairs]
    res = pl.pallas_call(
        body, name=name, out_shape=(*thru, *thru),
        in_specs=[HBM_ONLY] * (2 * n) + [SEM_SPEC, SEM_SPEC] + [ANY_SPEC] * len(after),
        out_specs=[HBM_ONLY] * (2 * n),
        input_output_aliases={i: i for i in range(2 * n)},
        compiler_params=pltpu.CompilerParams(has_side_effects=DATAFLOW),
    )(*pairs, *lands, send, recv, *after)
    return list(res[:n]), list(res[n:])


def _reduce_sum(pair, landed, name):
    s, half, c_dim = pair.shape
    rows = _half_rows(half, c_dim)
    per = half // rows
    shard = _shard_index(lax.axis_index("x"), lax.axis_index("y"))
    where = jnp.stack([shard, lax.axis_index("c")]).astype(jnp.int32)

    def landed_spec(k):
        return pl.BlockSpec((None, rows, c_dim), lambda i, w: (jnp.where(w[0] == k, (k + 1) % s, k), i, 0))

    def body(w_ref, own_ref, *rest):
        o_ref = rest[-1]
        acc = None
        for k in range(s):
            term = jnp.where(w_ref[0] == k, own_ref[...], rest[k][...]).astype(F32)
            acc = term if acc is None else acc + term
        o_ref[...] = acc

    return pl.pallas_call(
        body, name=name,
        grid_spec=pltpu.PrefetchScalarGridSpec(
            num_scalar_prefetch=1, grid=(per,),
            in_specs=[pl.BlockSpec((None, rows, c_dim), lambda i, w: (w[0], i, 0))] + [landed_spec(k) for k in range(s)],
            out_specs=pl.BlockSpec((rows, c_dim), lambda i, w: (w[1] * per + i, 0))),
        out_shape=pltpu.HBM((2 * half, c_dim), F32), compiler_params=_params(("arbitrary",), 48),
    )(where, pair, *([landed] * s))


def _my_rows(ref, c):
    half = ref.shape[0] // 2
    return ref.at[pl.ds(pl.multiple_of(c * half, 8), half)]


def _half_gather_start(bufs, name):
    n = len(bufs)

    def body(*refs):
        ins = refs[:n]
        send, recv = refs[n], refs[n + 1]
        token = refs[-1]
        x, y, c, _ = _place()
        for t in range(n):
            mine = _my_rows(ins[t], c)
            _remote(mine, mine, send.at[t], recv.at[t], (x, y, 1 - c)).start()
        token[...] = jnp.zeros_like(token)

    res = pl.pallas_call(
        body, name=name,
        out_shape=(pltpu.SemaphoreType.DMA((n,)), pltpu.SemaphoreType.DMA((n,)),
                   *[pltpu.HBM(b.shape, b.dtype) for b in bufs], jax.ShapeDtypeStruct((8, LANES), F32)),
        in_specs=[HBM_ONLY] * n,
        out_specs=(SEM_SPEC, SEM_SPEC, *([HBM_ONLY] * n), pl.BlockSpec(memory_space=pltpu.VMEM)),
        input_output_aliases={i: 2 + i for i in range(n)},
        compiler_params=pltpu.CompilerParams(has_side_effects=DATAFLOW),
    )(*[_in_hbm(b) for b in bufs])
    return res[0], res[1], list(res[2:2 + n]), res[-1]


def _half_gather_wait(send, recv, bufs, after, name):
    n = len(bufs)

    def body(*refs):
        ins = refs[:n]
        send_ref, recv_ref = refs[n], refs[n + 1]
        x, y, c, _ = _place()
        for t in range(n):
            cp = _remote(_my_rows(ins[t], c), _my_rows(ins[t], 1 - c), send_ref.at[t], recv_ref.at[t], (x, y, 1 - c))
            cp.wait_send()
            cp.wait_recv()

    res = pl.pallas_call(
        body, name=name, out_shape=tuple(pltpu.HBM(b.shape, b.dtype) for b in bufs),
        in_specs=[HBM_ONLY] * n + [SEM_SPEC, SEM_SPEC] + [ANY_SPEC] * len(after), out_specs=[HBM_ONLY] * n,
        input_output_aliases={i: i for i in range(n)},
        compiler_params=pltpu.CompilerParams(has_side_effects=DATAFLOW),
    )(*bufs, send, recv, *after)
    return list(res)


WEIGHT_NAMES = ("ln_attn", "w_in", "sink_b", "rpb_c", "mix_gain", "w_out", "ln_ffn", "w_up", "conv_w", "conv_b",
                "w_down", "ln_final")
BIG_NAMES = ("w_in", "w_out", "w_up", "w_down")
REPLICATED_NAMES = ("ln_attn", "sink_b", "rpb_c", "mix_gain", "ln_ffn", "conv_b", "ln_final")
PACK_TILE = 8 * LANES


def _pack(arrays, row_multiple):
    pieces = []
    for a in arrays:
        flat = a.reshape(-1)
        pieces.append(jnp.pad(flat, (0, (-flat.shape[0]) % PACK_TILE)))
    flat = jnp.concatenate(pieces)
    flat = jnp.pad(flat, (0, (-flat.shape[0]) % (row_multiple * LANES)))
    return flat.reshape(-1, LANES)


def _unpack(packed, shapes):
    flat = packed.reshape(-1)
    out, off = [], 0
    for shape in shapes:
        size = math.prod(shape)
        out.append(flat[off:off + size].reshape(shape))
        off += size + (-size) % PACK_TILE
    return out


def kernel(x, ln_attn, w_in, sink_b, rpb_c, mix_gain, w_out, ln_ffn, w_up, conv_w, conv_b, w_down, ln_final, loss_target, m_ln_attn, m_w_in, m_sink_b, m_rpb_c, m_mix_gain, m_w_out, m_ln_ffn, m_w_up, m_conv_w, m_conv_b, m_w_down, m_ln_final, v_ln_attn, v_w_in, v_sink_b, v_rpb_c, v_mix_gain, v_w_out, v_ln_ffn, v_w_up, v_conv_w, v_conv_b, v_w_down, v_ln_final):
    w = dict(ln_attn=ln_attn, w_in=w_in, sink_b=sink_b, rpb_c=rpb_c, mix_gain=mix_gain, w_out=w_out, ln_ffn=ln_ffn,
             w_up=w_up, conv_w=conv_w, conv_b=conv_b, w_down=w_down, ln_final=ln_final)
    m = dict(ln_attn=m_ln_attn, w_in=m_w_in, sink_b=m_sink_b, rpb_c=m_rpb_c, mix_gain=m_mix_gain, w_out=m_w_out,
             ln_ffn=m_ln_ffn, w_up=m_w_up, conv_w=m_conv_w, conv_b=m_conv_b, w_down=m_w_down, ln_final=m_ln_final)
    v = dict(ln_attn=v_ln_attn, w_in=v_w_in, sink_b=v_sink_b, rpb_c=v_rpb_c, mix_gain=v_mix_gain, w_out=v_w_out,
             ln_ffn=v_ln_ffn, w_up=v_w_up, conv_w=v_conv_w, conv_b=v_conv_b, w_down=v_w_down, ln_final=v_ln_final)
    shard = _shard_index(lax.axis_index("x"), lax.axis_index("y"))
    up_cols = w_up.shape[2]

    conv_send, conv_recv, conv_vec, conv_slots, conv_token = _small_start(_pack([conv_w], 8), [], "conv_w_start")

    arrivals = []
    group_of = {}
    tokens = []
    rest = ("w_out", "w_up", "w_down")
    for l, names in ((0, ("w_in",)), (0, rest), (1, ("w_in",)), (1, rest)):
        bufs = [_own_slot(w[k], l, shard, "own_" + k) for k in names]
        send, recv, bufs, token = _gather_start(bufs, tokens[-1:] or [conv_token], "gather_start_%d" % len(arrivals))
        tokens.append(token)
        for k in names:
            group_of[l, k] = len(arrivals)
        arrivals.append({"names": names, "send": send, "recv": recv, "bufs": bufs, "done": None})

    def gathered(l, name, after):
        idx = group_of[l, name]
        group = arrivals[idx]

        def whole(k, buf):
            return buf.reshape(1, -1, buf.shape[2]) if k in ("w_out", "w_down") else buf

        if group["done"] is None:
            follow = list(after) + tokens[-1:]
            if idx == 0:
                follow += [tabs[k] for k in ("cos", "sin", "bias_a", "bias_b")]
                follow += [p[k] for p in layers for k in ("bias_c", "conv_w")]
            bufs = _gather_wait(group["send"], group["recv"], group["bufs"], follow, "gather_wait_%d" % idx)
            first = _gather_forward(bufs[:1], "gather_forward_%d" % idx)[0]
            if len(bufs) > 1:
                send, recv, rest, first = _gather_forward_start(bufs[1:], first, "gather_forward_start_%d" % idx)
                group["rest"] = (send, recv, rest)
            group["done"] = {group["names"][0]: whole(group["names"][0], first)}
        if name not in group["done"]:
            send, recv, rest = group["rest"]
            rest = _gather_forward_wait(send, recv, rest, list(after), "gather_forward_wait_%d" % idx)
            group["done"].update({k: whole(k, buf) for k, buf in zip(group["names"][1:], rest)})
        return group["done"][name]

    conv_vec, conv_slots = _small_wait(conv_send, conv_recv, conv_vec, conv_slots, tokens[-1:], "conv_w_wait")
    device = 4 * lax.axis_index("x") + 2 * lax.axis_index("y") + lax.axis_index("c")
    conv_slots = lax.dynamic_update_index_in_dim(conv_slots, conv_vec, device, 0)
    conv_all = conv_slots[0::2].reshape(N_SHARDS, -1)[:, :conv_w.size].reshape((N_SHARDS,) + conv_w.shape)

    cos, sin = _rope_tables(SEQ)
    tabs = {"cos": cos, "sin": sin, "bias_a": _bias_a(), "bias_b": _bias_b()}
    layers = []
    for l in range(DEPTH):
        conv_w_l = conv_all[:, l].reshape(2, N_SHARDS // 2, 3, up_cols).transpose(0, 2, 1, 3).reshape(2, 3, D_FF)
        layers.append({"ln_attn": ln_attn[l][None], "sink_b": sink_b[l], "bias_c": _bias_c(rpb_c[l]),
                       "mix_gain": mix_gain[l][None], "ln_ffn": ln_ffn[l][None], "conv_w": conv_w_l,
                       "conv_b": conv_b[l].reshape(2, 1, D_FF)})

    act = x[0]
    saved = []
    for l in range(DEPTH):
        act, keep = _layer_fwd(act, layers[l], lambda name, after, l=l: gathered(l, name, [after]), tabs)
        saved.append(keep)
    loss_part, dx, dx_b, d_ln_final = _loss_head(act, ln_final[None], loss_target[0], "loss_head")
    loss = lax.psum(loss_part[0, 0], ("x", "y", "c"))

    reductions = []

    opened = [0]

    def begin(l, partial):
        idx = opened[0]
        opened[0] += 1
        names = tuple(partial)
        send_sem, recv_sem, mine, theirs, token = _half_exchange_start([partial[k] for k in names],
                                                                       "half_exchange_start_%d" % idx)
        return {"idx": idx, "layer": l, "names": names, "send": send_sem, "recv": recv_sem, "mine": mine,
                "theirs": theirs}, token

    def finish(handle, after):
        idx, names = handle["idx"], handle["names"]
        mine, theirs = _half_exchange_wait(handle["send"], handle["recv"], handle["mine"], handle["theirs"], after,
                                           "half_exchange_wait_%d" % idx)
        pairs = [_half_sum(a, b, "half_sum_" + k) for k, a, b in zip(names, mine, theirs)]
        send_sem, recv_sem, pairs, lands, token = _reduce_start(pairs, "reduce_start_%d" % idx)
        reductions.append({"layer": handle["layer"], "names": names, "send": send_sem, "recv": recv_sem,
                           "pairs": pairs, "lands": lands})
        return token

    small = [None] * DEPTH
    pending = None
    for l in reversed(range(DEPTH)):
        big = {k: gathered(l, k, []) for k in BIG_NAMES}
        dx, dx_b, small[l], pending = _layer_bwd(dx, dx_b, saved[l], layers[l], big, tabs,
                                                 functools.partial(begin, l), finish, pending)
    after = [finish(pending[0], [pending[1]])]

    stacked = {k: jnp.stack([small[l][k] for l in range(DEPTH)]) for k in small[0]}
    part = {"ln_attn": stacked["ln_attn"][:, 0], "sink_b": stacked["sink_b"], "rpb_c": stacked["rpb_c"],
            "mix_gain": stacked["mix_gain"][:, 0], "ln_ffn": stacked["ln_ffn"][:, 0],
            "conv_b": stacked["conv_b"].reshape(DEPTH, 2 * D_FF), "ln_final": d_ln_final[0],
            "conv_w": stacked["conv_w"].transpose(0, 2, 1, 3).reshape(DEPTH, 3, 2 * D_FF)}
    small_names = REPLICATED_NAMES + ("conv_w",)
    small_send, small_recv, small_vec, small_slots, token = _small_start(
        _pack([part[k] for k in small_names], 256), after, "small_grads_start")
    after = [token]

    grads, delta, new_m, new_v = {}, {}, {}, {}
    updated = dict.fromkeys(BIG_NAMES)

    def arrive(idx, after):
        group = reductions[idx]
        pairs, lands = _reduce_wait(group["send"], group["recv"], group["pairs"], group["lands"], after,
                                    "reduce_wait_%d" % idx)
        halves = [_reduce_sum(pair, landed, "reduce_sum_" + k) for k, pair, landed in zip(group["names"], pairs, lands)]
        send_sem, recv_sem, halves, token = _half_gather_start(halves, "half_gather_start_%d" % idx)
        return {"idx": idx, "send": send_sem, "recv": recv_sem, "bufs": halves, "names": group["names"],
                "layer": group["layer"]}, [token]

    def update(swap, after):
        whole = _half_gather_wait(swap["send"], swap["recv"], swap["bufs"], after,
                                  "half_gather_wait_%d" % swap["idx"])
        for k, g in zip(swap["names"], whole):
            updated[k] = _adamw_layer(w[k], g, m[k], v[k], swap["layer"], updated[k], "adamw_" + k)
        return [updated[k][0] for k in swap["names"]]

    swaps = []
    for idx in range(len(reductions) - 1):
        swap, after = arrive(idx, after)
        swaps.append(swap)
    for swap in swaps[:2]:
        after = update(swap, after)
    swap, after = arrive(len(reductions) - 1, after)
    for swap in swaps[2:] + [swap]:
        after = update(swap, after)
    for k in BIG_NAMES:
        grads[k], delta[k], new_m[k], new_v[k] = updated[k]

    small_vec, small_slots = _small_wait(small_send, small_recv, small_vec, small_slots, after, "small_grads_wait")
    total = _small_sum(small_vec, small_slots, "small_grads_sum")
    for k, g in zip(small_names, _unpack(total, [part[k].shape for k in small_names])):
        grads[k] = g
    grads["conv_w"] = lax.dynamic_slice_in_dim(grads["conv_w"], shard * up_cols, up_cols, axis=2)

    flat = (DEPTH * 3, up_cols)
    res = _adamw(conv_w.reshape(flat), grads["conv_w"].reshape(flat), m["conv_w"].reshape(flat),
                 v["conv_w"].reshape(flat), "adamw_conv_w")
    delta["conv_w"], new_m["conv_w"], new_v["conv_w"] = (r.reshape(conv_w.shape) for r in res)
    shapes = [w[k].shape for k in REPLICATED_NAMES]
    packed = [_pack([d[k] for k in REPLICATED_NAMES], 128) for d in (w, grads, m, v)]
    for d, res in zip((delta, new_m, new_v), _adamw(*packed, "adamw_small")):
        for k, r in zip(REPLICATED_NAMES, _unpack(res, shapes)):
            d[k] = r

    return (loss, dx[None], *[grads[k] for k in WEIGHT_NAMES], *[delta[k] for k in WEIGHT_NAMES],
            *[new_m[k] for k in WEIGHT_NAMES], *[new_v[k] for k in WEIGHT_NAMES])
```
